```python
import math
import numpy as np
import jax
import jax.numpy as jnp
from jax import lax

D_MODEL = 1024
BATCH = 8
SEQ = 4096
DEPTH = 4

N_GROUPS = 4
GROUP_W = D_MODEL // N_GROUPS
MIX_W = N_GROUPS * GROUP_W
HEADS = 4
HEAD_DIM = GROUP_W // HEADS
Q_BLOCK = 128
EPS = 1e-6

NSA_KV_DIM = HEAD_DIM
CMP_LEN = 32
CMP_STRIDE = 16
CMP_HIDDEN = 256
SLC_BLOCK = 64
SLC_TOPK = 16
WINDOW = 512
N_NSA_BRANCH = 3

DIFF_QK_DIM = HEAD_DIM // 2
DIFF_V_DIM = HEAD_DIM

RET_CHUNK = 128

SSM_HEAD_DIM = HEAD_DIM
SSM_HEADS = GROUP_W // SSM_HEAD_DIM
SSM_GROUPS = 2
SSM_STATE = 128
CONV_W = 4
CONV_CH = GROUP_W + 2 * SSM_GROUPS * SSM_STATE
SSM_CHUNK = 128

N_ALIBI_HEADS = 2 * HEADS

IN_LAYOUT = (
    ("nsa_q", GROUP_W), ("nsa_k_cmp", NSA_KV_DIM), ("nsa_v_cmp", NSA_KV_DIM),
    ("nsa_k_slc", NSA_KV_DIM), ("nsa_v_slc", NSA_KV_DIM), ("nsa_k_win", NSA_KV_DIM),
    ("nsa_v_win", NSA_KV_DIM), ("nsa_gate", HEADS * N_NSA_BRANCH), ("nsa_z", GROUP_W),
    ("diff_q", GROUP_W), ("diff_k", GROUP_W), ("diff_v", GROUP_W), ("diff_z", GROUP_W),
    ("ret_q", GROUP_W), ("ret_k", GROUP_W), ("ret_v", GROUP_W), ("ret_z", GROUP_W),
    ("ssm_z", GROUP_W), ("ssm_xbc", CONV_CH), ("ssm_dt", SSM_HEADS),
)
IN_W = sum(w for _, w in IN_LAYOUT)

kernel_name = "hymba_nsa_diff_retnet_ssd_trunk"


def rms_norm(x, w):
    xf = x.astype(jnp.float32)
    y = xf * lax.rsqrt(jnp.mean(xf * xf, axis=-1, keepdims=True) + EPS)
    return (y * w.astype(jnp.float32)).astype(x.dtype)


def masked_softmax(s, mask):
    s = jnp.where(mask, s, -jnp.inf)
    m = jnp.max(s, axis=-1, keepdims=True)
    p = jnp.exp(s - jnp.where(jnp.isfinite(m), m, 0.0))
    return p / jnp.maximum(jnp.sum(p, axis=-1, keepdims=True), 1e-30)


def alibi_slopes(n):
    return jnp.asarray(np.array([2.0 ** (-8.0 * (i + 1) / n) for i in range(n)], dtype=np.float32))


def split_columns(h):
    out = {}
    off = 0
    for name, w in IN_LAYOUT:
        out[name] = h[..., off:off + w]
        off += w
    return out


def nsa_mixer(q, k_cmp, v_cmp, k_slc, v_slc, k_win, v_win, gate_logits,
              pe_k, pe_v, w_ck1, w_ck2, w_cv1, w_cv2, slopes):
    f32 = jnp.float32
    B, S, _ = q.shape
    H, dh = HEADS, HEAD_DIM
    scale = dh ** -0.5
    q = q.reshape(B, S, H, dh)
    n_qb = S // Q_BLOCK
    tpos = jnp.arange(S)

    nc = (S - CMP_LEN) // CMP_STRIDE + 1
    blk = np.arange(nc)[:, None] * CMP_STRIDE + np.arange(CMP_LEN)[None, :]
    c_start, c_end = blk[:, 0], blk[:, -1]

    def compress(kv, pe, w1, w2):
        blocks = (kv[:, blk] + pe).reshape(B, nc, CMP_LEN * NSA_KV_DIM)
        return jax.nn.silu(blocks @ w1) @ w2

    kc = compress(k_cmp, pe_k, w_ck1, w_ck2)
    vc = compress(v_cmp, pe_v, w_cv1, w_cv2)
    c_center = jnp.asarray(((c_start + c_end) / 2.0).astype(np.float32))
    dist_c = tpos.astype(f32)[:, None] - c_center[None, :]
    s_c = jnp.einsum('bqhd,bnd->bhqn', q, kc).astype(f32) * scale - slopes[:, None, None] * dist_c
    p_c = masked_softmax(s_c, jnp.asarray(c_end)[None, :] <= tpos[:, None])
    o_cmp = jnp.einsum('bhqn,bnd->bqhd', p_c.astype(vc.dtype), vc)

    ns = S // SLC_BLOCK
    top = min(SLC_TOPK, ns)
    s_start = np.arange(ns) * SLC_BLOCK
    s_end = s_start + SLC_BLOCK - 1
    overlap = ((c_start[:, None] <= s_end[None, :]) & (c_end[:, None] >= s_start[None, :])).astype(np.float32)
    imp = jnp.einsum('bhqn,nj->bqj', p_c, jnp.asarray(overlap))
    jb = jnp.arange(ns)
    cur = tpos // SLC_BLOCK
    forced = (jb[None, :] == 0) | (jb[None, :] == cur[:, None]) | (jb[None, :] == cur[:, None] - 1)
    valid = jb[None, :] * SLC_BLOCK <= tpos[:, None]
    score = jnp.where(forced, jnp.inf, jnp.where(valid, imp, -jnp.inf))
    _, sel = lax.top_k(score, top)

    def select_block(args):
        qb, selb, tb = args
        tok = (selb[..., None] * SLC_BLOCK + jnp.arange(SLC_BLOCK)).reshape(B, Q_BLOCK, top * SLC_BLOCK)
        kg = jax.vmap(lambda kk, ii: kk[ii])(k_slc, tok)
        vg = jax.vmap(lambda vv, ii: vv[ii])(v_slc, tok)
        dist = tb[None, :, None] - tok
        s = (jnp.einsum('bqhd,bqkd->bhqk', qb, kg).astype(f32) * scale
             - slopes[None, :, None, None] * dist[:, None].astype(f32))
        p = masked_softmax(s, (dist >= 0)[:, None])
        return jnp.einsum('bhqk,bqkd->bqhd', p.astype(vg.dtype), vg)

    q_blocks = q.reshape(B, n_qb, Q_BLOCK, H, dh).transpose(1, 0, 2, 3, 4)
    sel_blocks = sel.reshape(B, n_qb, Q_BLOCK, top).transpose(1, 0, 2, 3)
    t_blocks = tpos.reshape(n_qb, Q_BLOCK)
    o_slc = lax.map(select_block, (q_blocks, sel_blocks, t_blocks))
    o_slc = o_slc.transpose(1, 0, 2, 3, 4).reshape(B, S, H, dh)

    n_pre = WINDOW // Q_BLOCK
    kw_len = (n_pre + 1) * Q_BLOCK

    def band(kv):
        kp = jnp.pad(kv, ((0, 0), (WINDOW, 0), (0, 0))).reshape(B, n_qb + n_pre, Q_BLOCK, NSA_KV_DIM)
        return jnp.concatenate([kp[:, i:i + n_qb] for i in range(n_pre + 1)], axis=2)

    kw, vw = band(k_win), band(v_win)
    tq = tpos.reshape(n_qb, Q_BLOCK)
    tk = jnp.arange(n_qb)[:, None] * Q_BLOCK - WINDOW + jnp.arange(kw_len)[None, :]
    dist = tq[:, :, None] - tk[:, None, :]
    mask = (dist >= 0) & (dist < WINDOW) & (tk[:, None, :] >= 0)
    s_w = (jnp.einsum('bnqhd,bnkd->bnhqk', q.reshape(B, n_qb, Q_BLOCK, H, dh), kw).astype(f32) * scale
           - slopes[None, None, :, None, None] * dist[None, :, None].astype(f32))
    p_w = masked_softmax(s_w, mask[None, :, None])
    o_win = jnp.einsum('bnhqk,bnkd->bnqhd', p_w.astype(vw.dtype), vw).reshape(B, S, H, dh)

    g = jax.nn.sigmoid(gate_logits.astype(f32)).reshape(B, S, H, N_NSA_BRANCH)
    o = g[..., 0:1] * o_cmp + g[..., 1:2] * o_slc + g[..., 2:3] * o_win
    return o.reshape(B, S, H * dh)


def diff_mixer(q, k, v, lam_q1, lam_k1, lam_q2, lam_k2, subln_w, slopes, layer_idx):
    f32 = jnp.float32
    B, S, _ = q.shape
    H, dq, dv = HEADS, DIFF_QK_DIM, DIFF_V_DIM
    q = q.reshape(B, S, H, 2, dq)
    k = k.reshape(B, S, H, 2, dq)
    v = v.reshape(B, S, H, dv)
    scale = dq ** -0.5
    lam_init = 0.8 - 0.6 * math.exp(-0.3 * layer_idx)
    lam = (jnp.exp(jnp.sum(lam_q1.astype(f32) * lam_k1.astype(f32)))
           - jnp.exp(jnp.sum(lam_q2.astype(f32) * lam_k2.astype(f32))) + lam_init)
    n_qb = S // Q_BLOCK
    tk = jnp.arange(S)

    def attend(args):
        qb, tb = args
        s = jnp.einsum('bqhid,bkhid->bhiqk', qb, k).astype(f32) * scale
        dist = tb[:, None] - tk[None, :]
        s = s - slopes[None, :, None, None, None] * dist.astype(f32)
        p = masked_softmax(s, dist >= 0)
        a = p[:, :, 0] - lam * p[:, :, 1]
        return jnp.einsum('bhqk,bkhd->bqhd', a.astype(v.dtype), v)

    qb = q.reshape(B, n_qb, Q_BLOCK, H, 2, dq).transpose(1, 0, 2, 3, 4, 5)
    o = lax.map(attend, (qb, tk.reshape(n_qb, Q_BLOCK)))
    o = o.transpose(1, 0, 2, 3, 4).reshape(B, S, H, dv)
    o = rms_norm(o, subln_w).astype(f32) * (1.0 - lam_init)
    return o.reshape(B, S, H * dv)


def retention_mixer(q, k, v, gn_w):
    f32 = jnp.float32
    B, S, _ = q.shape
    H, dh, C = HEADS, HEAD_DIM, RET_CHUNK
    n = S // C
    q = q.astype(f32).reshape(B, n, C, H, dh) * dh ** -0.5
    k = k.astype(f32).reshape(B, n, C, H, dh)
    v = v.astype(f32).reshape(B, n, C, H, dh)
    log_g = jnp.log(1.0 - 2.0 ** (-5.0 - jnp.arange(H, dtype=f32)))
    pos = jnp.arange(C, dtype=f32)
    rel = pos[:, None] - pos[None, :]
    decay = jnp.where(rel >= 0, jnp.exp(log_g[:, None, None] * jnp.maximum(rel, 0.0)), 0.0)
    inner = jnp.einsum('bnqhd,bnkhd->bnhqk', q, k) * decay
    inner = jnp.einsum('bnhqk,bnkhe->bnqhe', inner, v)
    xi = jnp.exp(log_g[:, None] * (pos + 1.0))
    zeta = jnp.exp(log_g[:, None] * (C - 1.0 - pos))
    chunk_decay = jnp.exp(log_g * C)
    kv = jnp.einsum('bnkhd,hk,bnkhe->bnhde', k, zeta, v)

    def step(state, kv_c):
        return state * chunk_decay[None, :, None, None] + kv_c, state

    _, prev = lax.scan(step, jnp.zeros((B, H, dh, dh), f32), jnp.moveaxis(kv, 1, 0))
    prev = jnp.moveaxis(prev, 0, 1)
    cross = jnp.einsum('bnqhd,hq,bnhde->bnqhe', q, xi, prev)
    o = (inner + cross).reshape(B, S, H, dh)
    mu = jnp.mean(o, axis=-1, keepdims=True)
    var = jnp.mean((o - mu) ** 2, axis=-1, keepdims=True)
    o = ((o - mu) * lax.rsqrt(var + EPS)).reshape(B, S, H * dh)
    return o * gn_w.astype(f32)


def ssd_mixer(xbc, dt_raw, conv_w, conv_b, dt_bias, A_log, D_skip):
    f32 = jnp.float32
    B, S, _ = xbc.shape
    H, P, G, N, L = SSM_HEADS, SSM_HEAD_DIM, SSM_GROUPS, SSM_STATE, SSM_CHUNK
    HG = H // G
    n = S // L
    xbc = lax.conv_general_dilated(xbc, conv_w[:, None, :].astype(xbc.dtype), (1,), [(CONV_W - 1, 0)],
                                   dimension_numbers=('NWC', 'WIO', 'NWC'),
                                   feature_group_count=CONV_CH) + conv_b
    xbc = jax.nn.silu(xbc).astype(f32)
    x = xbc[..., :GROUP_W].reshape(B, n, L, G, HG, P)
    Bm = xbc[..., GROUP_W:GROUP_W + G * N].reshape(B, n, L, G, N)
    Cm = xbc[..., GROUP_W + G * N:].reshape(B, n, L, G, N)
    dt = jax.nn.softplus(dt_raw.astype(f32) + dt_bias.astype(f32))
    A = -jnp.exp(A_log.astype(f32))
    dA = (dt * A).reshape(B, n, L, G, HG)
    cs = jnp.cumsum(dA, axis=2).transpose(0, 1, 3, 4, 2)
    xdt = x * dt.reshape(B, n, L, G, HG)[..., None]
    causal = jnp.asarray(np.tril(np.ones((L, L), dtype=bool)))
    seg = jnp.exp(jnp.where(causal, cs[..., :, None] - cs[..., None, :], -jnp.inf))
    cb = jnp.einsum('bclgn,bcsgn->bcgls', Cm, Bm)
    y_diag = jnp.einsum('bcgls,bcghls,bcsghp->bclghp', cb, seg, xdt)
    decay_states = jnp.exp(cs[..., -1:] - cs)
    states = jnp.einsum('bclgn,bcghl,bclghp->bcghpn', Bm, decay_states, xdt)
    chunk_decay = jnp.exp(cs[..., -1])

    def step(h, inp):
        st, dec = inp
        return h * dec[..., None, None] + st, h

    _, prev = lax.scan(step, jnp.zeros((B, G, HG, P, N), f32),
                       (jnp.moveaxis(states, 1, 0), jnp.moveaxis(chunk_decay, 1, 0)))
    prev = jnp.moveaxis(prev, 0, 1)
    y_off = jnp.einsum('bclgn,bcghpn,bcghl->bclghp', Cm, prev, jnp.exp(cs))
    y = y_diag + y_off + D_skip.astype(f32).reshape(G, HG)[:, :, None] * x
    return y.reshape(B, S, H * P)


def setup_inputs(seed: int = 0) -> dict:
    key = jax.random.key(seed)
    ks = jax.random.split(key, 24)
    f32 = jnp.float32

    def nrm(k, shape, scale):
        return jax.random.normal(k, shape, f32) * scale

    Ld = DEPTH
    dt = jnp.exp(jax.random.uniform(ks[18], (Ld, SSM_HEADS), f32, math.log(1e-3), math.log(1e-1)))
    return {
        "x": nrm(ks[0], (BATCH, SEQ, D_MODEL), 1.0),
        "norm_w": 1.0 + nrm(ks[1], (Ld, D_MODEL), 0.02),
        "w_in": nrm(ks[2], (Ld, D_MODEL, IN_W), D_MODEL ** -0.5),
        "w_out": nrm(ks[3], (Ld, MIX_W, D_MODEL), 0.5 * MIX_W ** -0.5),
        "nsa_pe_k": nrm(ks[4], (Ld, CMP_LEN, NSA_KV_DIM), 0.1),
        "nsa_pe_v": nrm(ks[5], (Ld, CMP_LEN, NSA_KV_DIM), 0.1),
        "nsa_w_ck1": nrm(ks[6], (Ld, CMP_LEN * NSA_KV_DIM, CMP_HIDDEN), (CMP_LEN * NSA_KV_DIM) ** -0.5),
        "nsa_w_ck2": nrm(ks[7], (Ld, CMP_HIDDEN, NSA_KV_DIM), CMP_HIDDEN ** -0.5),
        "nsa_w_cv1": nrm(ks[8], (Ld, CMP_LEN * NSA_KV_DIM, CMP_HIDDEN), (CMP_LEN * NSA_KV_DIM) ** -0.5),
        "nsa_w_cv2": nrm(ks[9], (Ld, CMP_HIDDEN, NSA_KV_DIM), CMP_HIDDEN ** -0.5),
        "diff_lam_q1": nrm(ks[10], (Ld, DIFF_QK_DIM), 0.1),
        "diff_lam_k1": nrm(ks[11], (Ld, DIFF_QK_DIM), 0.1),
        "diff_lam_q2": nrm(ks[12], (Ld, DIFF_QK_DIM), 0.1),
        "diff_lam_k2": nrm(ks[13], (Ld, DIFF_QK_DIM), 0.1),
        "diff_subln_w": 1.0 + nrm(ks[14], (Ld, DIFF_V_DIM), 0.02),
        "ret_gn_w": 1.0 + nrm(ks[15], (Ld, GROUP_W), 0.02),
        "ssm_conv_w": nrm(ks[16], (Ld, CONV_W, CONV_CH), CONV_W ** -0.5),
        "ssm_conv_b": nrm(ks[17], (Ld, CONV_CH), 0.01),
        "ssm_dt_bias": dt + jnp.log(-jnp.expm1(-dt)),
        "ssm_A_log": jnp.log(jax.random.uniform(ks[19], (Ld, SSM_HEADS), f32, 1.0, 16.0)),
        "ssm_D": 1.0 + nrm(ks[20], (Ld, SSM_HEADS), 0.1),
        "ssm_norm_w": 1.0 + nrm(ks[21], (Ld, GROUP_W), 0.02),
        "final_norm_w": 1.0 + nrm(ks[22], (D_MODEL,), 0.02),
    }


def reference(x, norm_w, w_in, w_out, nsa_pe_k, nsa_pe_v, nsa_w_ck1, nsa_w_ck2, nsa_w_cv1, nsa_w_cv2,
              diff_lam_q1, diff_lam_k1, diff_lam_q2, diff_lam_k2, diff_subln_w, ret_gn_w,
              ssm_conv_w, ssm_conv_b, ssm_dt_bias, ssm_A_log, ssm_D, ssm_norm_w, final_norm_w):
    slopes = alibi_slopes(N_ALIBI_HEADS)
    nsa_slopes = slopes[0::2]
    diff_slopes = slopes[1::2]
    for i in range(DEPTH):
        h = rms_norm(x, norm_w[i])
        c = split_columns(h @ w_in[i])
        y_nsa = nsa_mixer(c["nsa_q"], c["nsa_k_cmp"], c["nsa_v_cmp"], c["nsa_k_slc"], c["nsa_v_slc"],
                          c["nsa_k_win"], c["nsa_v_win"], c["nsa_gate"], nsa_pe_k[i], nsa_pe_v[i],
                          nsa_w_ck1[i], nsa_w_ck2[i], nsa_w_cv1[i], nsa_w_cv2[i], nsa_slopes)
        y_nsa = y_nsa * jax.nn.silu(c["nsa_z"])
        y_diff = diff_mixer(c["diff_q"], c["diff_k"], c["diff_v"], diff_lam_q1[i], diff_lam_k1[i],
                            diff_lam_q2[i], diff_lam_k2[i], diff_subln_w[i], diff_slopes, i)
        y_diff = y_diff * jax.nn.silu(c["diff_z"])
        y_ret = retention_mixer(c["ret_q"], c["ret_k"], c["ret_v"], ret_gn_w[i]) * jax.nn.silu(c["ret_z"])
        y_ssm = ssd_mixer(c["ssm_xbc"], c["ssm_dt"], ssm_conv_w[i], ssm_conv_b[i], ssm_dt_bias[i],
                          ssm_A_log[i], ssm_D[i])
        y_ssm = rms_norm(y_ssm * jax.nn.silu(c["ssm_z"]), ssm_norm_w[i])
        mixed = jnp.concatenate([y_nsa.astype(x.dtype), y_diff.astype(x.dtype),
                                 y_ret.astype(x.dtype), y_ssm.astype(x.dtype)], axis=-1)
        x = x + mixed @ w_out[i]
    return rms_norm(x, final_norm_w)
```

```python
import functools
import math

import numpy as np
import jax
import jax.numpy as jnp
from jax import lax
from jax.experimental import pallas as pl
from jax.experimental.pallas import tpu as pltpu

F32 = jnp.float32
BF16 = jnp.bfloat16
NEG_INF = float("-inf")

D_MODEL = 1024
DEPTH = 4
GROUP_W = 256
HEADS = 4
HEAD_DIM = 64
EPS = 1e-6
CMP_LEN = 32
CMP_STRIDE = 16
CMP_HIDDEN = 256
SLC_BLOCK = 64
SLC_SHIFT = 6
SLC_TOPK = 16
WINDOW = 512
DIFF_QK_DIM = 32
RET_CHUNK = 128
SSM_STATE = 128
SSM_CHUNK = 128
CONV_W = 4
CONV_CH = 768
N_ALIBI_HEADS = 8

IN_LAYOUT = (
    ("nsa_q", 256), ("nsa_k_cmp", 64), ("nsa_v_cmp", 64), ("nsa_k_slc", 64), ("nsa_v_slc", 64),
    ("nsa_k_win", 64), ("nsa_v_win", 64), ("nsa_gate", 12), ("nsa_z", 256),
    ("diff_q", 256), ("diff_k", 256), ("diff_v", 256), ("diff_z", 256),
    ("ret_q", 256), ("ret_k", 256), ("ret_v", 256), ("ret_z", 256),
    ("ssm_z", 256), ("ssm_xbc", 768), ("ssm_dt", 4),
)
IN_OFF = {}
_o = 0
for _n, _w in IN_LAYOUT:
    IN_OFF[_n] = (_o, _w)
    _o += _w
IN_W = _o

GATE_COL = 0
DT_COL = 12
IN_OUTPUTS = (
    ("nsa_q", BF16, ("nsa_q",), 256),
    ("nsa_kvsw", BF16, ("nsa_k_slc", "nsa_v_slc", "nsa_k_win", "nsa_v_win"), 256),
    ("nsa_cmp", F32, ("nsa_k_cmp", "nsa_v_cmp"), 128),
    ("misc", F32, ("nsa_gate", "ssm_dt"), 128),
    ("z_all", F32, ("nsa_z", "diff_z", "ret_z", "ssm_z"), 1024),
    ("diff_qkv", BF16, ("diff_q", "diff_k", "diff_v"), 768),
    ("ret_qkv", BF16, ("ret_q", "ret_k", "ret_v"), 768),
    ("xbc", F32, ("ssm_xbc",), 768),
)
IN_SEGS = []
_o = 0
for _n, _dt, _src, _w in IN_OUTPUTS:
    IN_SEGS.append((_o, _o + _w))
    _o += _w
IN_WP = _o

VMEM_LIMIT = 56 * 1024 * 1024


def _alibi_slopes():
    return [2.0 ** (-8.0 * (i + 1) / N_ALIBI_HEADS) for i in range(N_ALIBI_HEADS)]


NSA_SLOPES = _alibi_slopes()[0::2]
DIFF_SLOPES = _alibi_slopes()[1::2]


def _silu(x):
    return x * jax.nn.sigmoid(x)


def _dot(a, b):
    return jnp.dot(a, b, preferred_element_type=F32)


def _dot_nt(a, b):
    return lax.dot_general(a, b, (((1,), (1,)), ((), ())), preferred_element_type=F32)


def _dot_tn(a, b):
    return lax.dot_general(a, b, (((0,), (0,)), ((), ())), preferred_element_type=F32)


def _params(sem):
    return pltpu.CompilerParams(dimension_semantics=sem, vmem_limit_bytes=VMEM_LIMIT)


def _relayout_w_in(w_in):
    cols = []
    for _n, _dt, src, width in IN_OUTPUTS:
        used = 0
        for s in src:
            off, w = IN_OFF[s]
            cols.append(w_in[:, :, off:off + w])
            used += w
        if used < width:
            cols.append(jnp.zeros(w_in.shape[:2] + (width - used,), w_in.dtype))
    return jnp.concatenate(cols, axis=-1).astype(BF16)


def _in_proj_kernel(x_ref, nw_ref, w_ref, *out_refs):
    x = x_ref[...]
    ms = jnp.mean(x * x, axis=-1, keepdims=True)
    h = (x * lax.rsqrt(ms + EPS) * nw_ref[...]).astype(BF16)
    for ref, (a, b) in zip(out_refs, IN_SEGS):
        ref[...] = _dot(h, w_ref[:, a:b]).astype(ref.dtype)


def _in_proj(x2d, norm_w, w_r, tm):
    m = x2d.shape[0]
    out_shape = [jax.ShapeDtypeStruct((m, b - a), dt) for (_n, dt, _s, _w), (a, b) in zip(IN_OUTPUTS, IN_SEGS)]
    out_specs = [pl.BlockSpec((tm, b - a), lambda i: (i, 0)) for (a, b) in IN_SEGS]
    return pl.pallas_call(
        _in_proj_kernel,
        grid=(m // tm,),
        in_specs=[pl.BlockSpec((tm, D_MODEL), lambda i: (i, 0)),
                  pl.BlockSpec((1, D_MODEL), lambda i: (0, 0)),
                  pl.BlockSpec((D_MODEL, IN_WP), lambda i: (0, 0))],
        out_specs=out_specs,
        out_shape=out_shape,
        compiler_params=_params(("parallel",)),
        name="in_proj",
    )(x2d, norm_w.reshape(1, D_MODEL), w_r)


def _out_proj_kernel(y0_ref, y1_ref, y2_ref, y3_ref, w_ref, x_ref, fw_ref, o_ref, *, final):
    acc = x_ref[...]
    for g, y_ref in enumerate((y0_ref, y1_ref, y2_ref, y3_ref)):
        acc = acc + _dot(y_ref[...], w_ref[g * GROUP_W:(g + 1) * GROUP_W, :])
    if final:
        ms = jnp.mean(acc * acc, axis=-1, keepdims=True)
        acc = acc * lax.rsqrt(ms + EPS) * fw_ref[...]
    o_ref[...] = acc


def _out_proj(ys, w_out_b, x2d, final_w, tm, final):
    m = x2d.shape[0]
    yspec = pl.BlockSpec((tm, GROUP_W), lambda i: (i, 0))
    return pl.pallas_call(
        functools.partial(_out_proj_kernel, final=final),
        grid=(m // tm,),
        in_specs=[yspec, yspec, yspec, yspec,
                  pl.BlockSpec((D_MODEL, D_MODEL), lambda i: (0, 0)),
                  pl.BlockSpec((tm, D_MODEL), lambda i: (i, 0)),
                  pl.BlockSpec((1, D_MODEL), lambda i: (0, 0))],
        out_specs=pl.BlockSpec((tm, D_MODEL), lambda i: (i, 0)),
        out_shape=jax.ShapeDtypeStruct((m, D_MODEL), F32),
        compiler_params=_params(("parallel",)),
        name="out_proj",
    )(*ys, w_out_b, x2d, final_w.reshape(1, D_MODEL))


def _flash_update(s, v, m_ref, l_ref, acc_ref, idx):
    m_old = m_ref[idx]
    m_new = jnp.maximum(m_old, jnp.max(s, axis=-1, keepdims=True))
    alpha = jnp.exp(m_old - m_new)
    p = jnp.exp(s - m_new)
    l_ref[idx] = alpha * l_ref[idx] + jnp.sum(p, axis=-1, keepdims=True)
    acc_ref[idx] = alpha * acc_ref[idx] + _dot(p.astype(BF16), v)
    m_ref[idx] = m_new


def _diff_kernel(lam_ref, sw_ref, q_ref, k_ref, v_ref, z_ref, o_ref, m_ref, l_ref, acc_ref, *, tq, lam_init):
    qi = pl.program_id(1)
    scale = DIFF_QK_DIM ** -0.5
    q = q_ref[...]
    m_ref[...] = jnp.full(m_ref.shape, NEG_INF, F32)
    l_ref[...] = jnp.zeros(l_ref.shape, F32)
    acc_ref[...] = jnp.zeros(acc_ref.shape, F32)
    rel = (lax.broadcasted_iota(jnp.int32, (tq, tq), 0) - lax.broadcasted_iota(jnp.int32, (tq, tq), 1)).astype(F32)

    def tile(ki, diag):
        start = pl.multiple_of(ki * tq, tq)
        k = k_ref[pl.ds(start, tq), :]
        v = v_ref[pl.ds(start, tq), :]
        dist = rel + ((qi - ki) * tq).astype(F32)
        for h in range(HEADS):
            bias = DIFF_SLOPES[h] * dist
            vh = v[:, h * HEAD_DIM:(h + 1) * HEAD_DIM]
            for i in range(2):
                j = 2 * h + i
                sl = slice(j * DIFF_QK_DIM, (j + 1) * DIFF_QK_DIM)
                s = _dot_nt(q[:, sl], k[:, sl]) * scale - bias
                if diag:
                    s = jnp.where(rel >= 0, s, NEG_INF)
                _flash_update(s, vh, m_ref, l_ref, acc_ref, j)

    def body(ki, carry):
        tile(ki, False)
        return carry

    lax.fori_loop(0, qi, body, 0)
    tile(qi, True)

    lp = lam_ref[...]
    lam = (jnp.exp(jnp.sum(lp[0:1] * lp[1:2], axis=-1, keepdims=True))
           - jnp.exp(jnp.sum(lp[2:3] * lp[3:4], axis=-1, keepdims=True)) + lam_init)
    sw = sw_ref[...]
    outs = []
    for h in range(HEADS):
        o1 = acc_ref[2 * h] / jnp.maximum(l_ref[2 * h], 1e-30)
        o2 = acc_ref[2 * h + 1] / jnp.maximum(l_ref[2 * h + 1], 1e-30)
        o = o1 - lam * o2
        ms = jnp.mean(o * o, axis=-1, keepdims=True)
        outs.append(o * lax.rsqrt(ms + EPS) * sw * (1.0 - lam_init))
    y = jnp.concatenate(outs, axis=-1) * _silu(z_ref[...])
    o_ref[...] = y.astype(o_ref.dtype)


def _diff_attention(qkv, z_all, lam_p, subln_w, batch, seq, layer_idx, tq):
    nq = seq // tq
    lam_init = 0.8 - 0.6 * math.exp(-0.3 * layer_idx)
    return pl.pallas_call(
        functools.partial(_diff_kernel, tq=tq, lam_init=lam_init),
        grid=(batch, nq),
        in_specs=[pl.BlockSpec((4, DIFF_QK_DIM), lambda b, i: (0, 0)),
                  pl.BlockSpec((1, HEAD_DIM), lambda b, i: (0, 0)),
                  pl.BlockSpec((tq, GROUP_W), lambda b, i: (b * nq + i, 0)),
                  pl.BlockSpec((seq, GROUP_W), lambda b, i: (b, 1)),
                  pl.BlockSpec((seq, GROUP_W), lambda b, i: (b, 2)),
                  pl.BlockSpec((tq, GROUP_W), lambda b, i: (b * nq + i, 1))],
        out_specs=pl.BlockSpec((tq, GROUP_W), lambda b, i: (b * nq + i, 0)),
        out_shape=jax.ShapeDtypeStruct((batch * seq, GROUP_W), BF16),
        scratch_shapes=[pltpu.VMEM((8, tq, 1), F32), pltpu.VMEM((8, tq, 1), F32),
                        pltpu.VMEM((8, tq, HEAD_DIM), F32)],
        compiler_params=_params(("parallel", "parallel")),
        name="diff_attention",
    )(lam_p, subln_w.reshape(1, HEAD_DIM), qkv, qkv, qkv, z_all)


def _nsa_compress_kernel(gk_ref, gv_ref, pek_ref, pev_ref, wk1_ref, wk2_ref, wv1_ref, wv2t_ref, kc_ref, vct_ref):
    half = CMP_STRIDE * HEAD_DIM

    def hidden(g_ref, pe_ref, w1_ref):
        g = g_ref[0]
        top = (g + pe_ref[0:1, :]).astype(BF16)
        bot = (g + pe_ref[1:2, :]).astype(BF16)
        a = _dot(top, w1_ref[0:half, :])
        b = _dot(bot, w1_ref[half:2 * half, :])
        nrow = b.shape[0]
        b = pltpu.roll(b, nrow - 1, 0)
        return _silu(a + b).astype(BF16)

    hk = hidden(gk_ref, pek_ref, wk1_ref)
    kc_ref[0] = _dot(hk, wk2_ref[...]).astype(kc_ref.dtype)
    hv = hidden(gv_ref, pev_ref, wv1_ref)
    vct_ref[0] = _dot_nt(wv2t_ref[...], hv).astype(vct_ref.dtype)


def _nsa_compress(cmp2d, pe_k, pe_v, w_ck1, w_ck2, w_cv1, w_cv2, batch, seq):
    ng = seq // CMP_STRIDE
    half = CMP_STRIDE * HEAD_DIM
    gk = cmp2d[:, :HEAD_DIM].reshape(batch, ng, half)
    gv = cmp2d[:, HEAD_DIM:].reshape(batch, ng, half)
    gspec = pl.BlockSpec((1, ng, half), lambda b: (b, 0, 0))

    def full(shape):
        return pl.BlockSpec(shape, lambda b: (0,) * len(shape))

    return pl.pallas_call(
        _nsa_compress_kernel,
        grid=(batch,),
        in_specs=[gspec, gspec, full((2, half)), full((2, half)),
                  full((2 * half, CMP_HIDDEN)), full((CMP_HIDDEN, HEAD_DIM)),
                  full((2 * half, CMP_HIDDEN)), full((HEAD_DIM, CMP_HIDDEN))],
        out_specs=[pl.BlockSpec((1, ng, HEAD_DIM), lambda b: (b, 0, 0)),
                   pl.BlockSpec((1, HEAD_DIM, ng), lambda b: (b, 0, 0))],
        out_shape=[jax.ShapeDtypeStruct((batch, ng, HEAD_DIM), BF16),
                   jax.ShapeDtypeStruct((batch, HEAD_DIM, ng), BF16)],
        compiler_params=_params(("parallel",)),
        name="nsa_compress",
    )(gk, gv, pe_k.reshape(2, half), pe_v.reshape(2, half),
      w_ck1.astype(BF16), w_ck2.astype(BF16), w_cv1.astype(BF16), w_cv2.T.astype(BF16))


def _nsa_kernel(ovl_ref, q_ref, kv_ref, kc_ref, vct_ref, misc_ref, z_ref, o_ref,
                m_ref, l_ref, acc_ref, *, tq, seq):
    qi = pl.program_id(1)
    scale = HEAD_DIM ** -0.5
    ng = seq // CMP_STRIDE
    ns = seq // SLC_BLOCK
    nsp = ((ns + 127) // 128) * 128
    top = min(SLC_TOPK, ns)
    q = q_ref[...]
    m_ref[...] = jnp.full(m_ref.shape, NEG_INF, F32)
    l_ref[...] = jnp.zeros(l_ref.shape, F32)
    acc_ref[...] = jnp.zeros(acc_ref.shape, F32)

    t_lane = qi * tq + lax.broadcasted_iota(jnp.int32, (1, tq), 1)
    t_lane_f = t_lane.astype(F32)
    n_sub = lax.broadcasted_iota(jnp.int32, (ng, 1), 0)
    c_center = (n_sub * CMP_STRIDE).astype(F32) + (CMP_LEN - 1) / 2.0
    c_valid = (n_sub * CMP_STRIDE + (CMP_LEN - 1)) <= t_lane
    dist_c = t_lane_f - c_center
    kc = kc_ref[0]
    vct = vct_ref[0]
    ovl = ovl_ref[...]
    imp = jnp.zeros((ns, tq), F32)
    o_cmp_t = []
    for h in range(HEADS):
        qh = q[:, h * HEAD_DIM:(h + 1) * HEAD_DIM]
        s = _dot_nt(kc, qh) * scale - NSA_SLOPES[h] * dist_c
        s = jnp.where(c_valid, s, NEG_INF)
        mx = jnp.max(s, axis=0, keepdims=True)
        mx = jnp.where(mx > NEG_INF, mx, 0.0)
        p = jnp.exp(s - mx)
        p = p / jnp.maximum(jnp.sum(p, axis=0, keepdims=True), 1e-30)
        pb = p.astype(BF16)
        o_cmp_t.append(_dot(vct, pb))
        imp = imp + _dot(ovl, pb)
    o_cmp = jnp.concatenate(o_cmp_t, axis=0).T

    j_sub = lax.broadcasted_iota(jnp.int32, (ns, 1), 0)
    j_sub_f = j_sub.astype(F32)
    cur = jnp.right_shift(t_lane, SLC_SHIFT)
    forced = (j_sub == 0) | (j_sub == cur) | (j_sub == cur - 1)
    valid = (j_sub * SLC_BLOCK) <= t_lane
    score = jnp.where(forced, 1e30, jnp.where(valid, imp, -1.0))
    sel = jnp.zeros((ns, tq), F32)
    for _ in range(top):
        mx = jnp.max(score, axis=0, keepdims=True)
        idx = jnp.min(jnp.where(score == mx, j_sub_f, float(ns)), axis=0, keepdims=True)
        pick = j_sub_f == idx
        sel = jnp.where(pick, 1.0, sel)
        score = jnp.where(pick, -2.0, score)
    if nsp > ns:
        sel = jnp.concatenate([sel, jnp.zeros((nsp - ns, tq), F32)], axis=0)
    sel_q = sel.T.astype(BF16)

    rel = lax.broadcasted_iota(jnp.int32, (tq, tq), 0) - lax.broadcasted_iota(jnp.int32, (tq, tq), 1)
    rel_f = rel.astype(F32)
    blk_of_col = jnp.right_shift(lax.broadcasted_iota(jnp.int32, (nsp, tq), 1), SLC_SHIFT)
    blk_row = lax.broadcasted_iota(jnp.int32, (nsp, tq), 0)

    def slc_tile(ki, diag):
        start = pl.multiple_of(ki * tq, tq)
        kv = kv_ref[pl.ds(start, tq), :]
        k = kv[:, 0:HEAD_DIM]
        v = kv[:, HEAD_DIM:2 * HEAD_DIM]
        expand = jnp.where(blk_row == blk_of_col + ki * (tq // SLC_BLOCK), 1.0, 0.0).astype(BF16)
        keep = _dot(sel_q, expand) > 0.5
        if diag:
            keep = keep & (rel >= 0)
        dist = rel_f + ((qi - ki) * tq).astype(F32)
        for h in range(HEADS):
            qh = q[:, h * HEAD_DIM:(h + 1) * HEAD_DIM]
            s = _dot_nt(qh, k) * scale - NSA_SLOPES[h] * dist
            s = jnp.where(keep, s, NEG_INF)
            _flash_update(s, v, m_ref, l_ref, acc_ref, h)

    def win_tile(ki):
        start = pl.multiple_of(ki * tq, tq)
        kv = kv_ref[pl.ds(start, tq), :]
        k = kv[:, 2 * HEAD_DIM:3 * HEAD_DIM]
        v = kv[:, 3 * HEAD_DIM:4 * HEAD_DIM]
        dist_i = rel + (qi - ki) * tq
        keep = (dist_i >= 0) & (dist_i < WINDOW)
        dist = dist_i.astype(F32)
        for h in range(HEADS):
            qh = q[:, h * HEAD_DIM:(h + 1) * HEAD_DIM]
            s = _dot_nt(qh, k) * scale - NSA_SLOPES[h] * dist
            s = jnp.where(keep, s, NEG_INF)
            _flash_update(s, v, m_ref, l_ref, acc_ref, HEADS + h)

    def body(ki, carry):
        slc_tile(ki, False)
        return carry

    lax.fori_loop(0, qi, body, 0)
    slc_tile(qi, True)

    win_tile(qi)
    for off in range(1, (WINDOW + tq - 1) // tq + 1):
        @pl.when(qi >= off)
        def _():
            win_tile(qi - off)

    g = jax.nn.sigmoid(misc_ref[...])
    outs = []
    for h in range(HEADS):
        o_slc = acc_ref[h] / jnp.maximum(l_ref[h], 1e-30)
        o_win = acc_ref[HEADS + h] / jnp.maximum(l_ref[HEADS + h], 1e-30)
        c0 = GATE_COL + 3 * h
        outs.append(g[:, c0:c0 + 1] * o_cmp[:, h * HEAD_DIM:(h + 1) * HEAD_DIM]
                    + g[:, c0 + 1:c0 + 2] * o_slc + g[:, c0 + 2:c0 + 3] * o_win)
    y = jnp.concatenate(outs, axis=-1) * _silu(z_ref[...])
    o_ref[...] = y.astype(o_ref.dtype)


def _overlap_t(seq):
    nc = (seq - CMP_LEN) // CMP_STRIDE + 1
    ng = seq // CMP_STRIDE
    ns = seq // SLC_BLOCK
    c_start = np.arange(ng) * CMP_STRIDE
    c_end = c_start + CMP_LEN - 1
    s_start = np.arange(ns) * SLC_BLOCK
    s_end = s_start + SLC_BLOCK - 1
    ov = (c_start[None, :] <= s_end[:, None]) & (c_end[None, :] >= s_start[:, None]) & (np.arange(ng)[None, :] < nc)
    return jnp.asarray(ov.astype(np.float32), dtype=BF16)


def _nsa_attention(q, kvsw, kc, vct, misc, z_all, batch, seq, tq):
    nq = seq // tq
    ng = seq // CMP_STRIDE
    ns = seq // SLC_BLOCK
    return pl.pallas_call(
        functools.partial(_nsa_kernel, tq=tq, seq=seq),
        grid=(batch, nq),
        in_specs=[pl.BlockSpec((ns, ng), lambda b, i: (0, 0)),
                  pl.BlockSpec((tq, GROUP_W), lambda b, i: (b * nq + i, 0)),
                  pl.BlockSpec((seq, GROUP_W), lambda b, i: (b, 0)),
                  pl.BlockSpec((1, ng, HEAD_DIM), lambda b, i: (b, 0, 0)),
                  pl.BlockSpec((1, HEAD_DIM, ng), lambda b, i: (b, 0, 0)),
                  pl.BlockSpec((tq, 128), lambda b, i: (b * nq + i, 0)),
                  pl.BlockSpec((tq, GROUP_W), lambda b, i: (b * nq + i, 0))],
        out_specs=pl.BlockSpec((tq, GROUP_W), lambda b, i: (b * nq + i, 0)),
        out_shape=jax.ShapeDtypeStruct((batch * seq, GROUP_W), BF16),
        scratch_shapes=[pltpu.VMEM((8, tq, 1), F32), pltpu.VMEM((8, tq, 1), F32),
                        pltpu.VMEM((8, tq, HEAD_DIM), F32)],
        compiler_params=_params(("parallel", "parallel")),
        name="nsa_attention",
    )(_overlap_t(seq), q, kvsw, kc, vct, misc, z_all)


def _ret_tables():
    c = RET_CHUNK
    h = np.arange(HEADS, dtype=np.float32)
    log_g = jnp.log(1.0 - 2.0 ** (-5.0 - jnp.asarray(h)))
    pos = jnp.arange(c, dtype=F32)
    rel = pos[:, None] - pos[None, :]
    decay = jnp.where(rel >= 0, jnp.exp(log_g[:, None, None] * jnp.maximum(rel, 0.0)), 0.0)
    xi = jnp.exp(log_g[:, None] * (pos + 1.0))
    zeta = jnp.exp(log_g[:, None] * (c - 1.0 - pos))
    chunk_decay = jnp.exp(log_g * c)
    xi_tab = jnp.repeat(xi.T, HEAD_DIM, axis=1)
    zeta_tab = jnp.repeat(zeta.T, HEAD_DIM, axis=1)
    cd_tab = jnp.repeat(chunk_decay, HEAD_DIM)[None, :]
    return decay, xi_tab, zeta_tab, cd_tab


def _ret_kernel(decay_ref, xi_ref, zeta_ref, cd_ref, gn_ref, q_ref, k_ref, v_ref, z_ref, o_ref, st_ref):
    n = pl.program_id(1)

    @pl.when(n == 0)
    def _():
        st_ref[...] = jnp.zeros(st_ref.shape, F32)

    q = (q_ref[...].astype(F32) * (HEAD_DIM ** -0.5)).astype(BF16)
    k = k_ref[...]
    v = v_ref[...]
    kz_t = (k.astype(F32) * zeta_ref[...]).T.astype(BF16)
    xi = xi_ref[...]
    cd = cd_ref[...]
    outs = []
    for h in range(HEADS):
        sl = slice(h * HEAD_DIM, (h + 1) * HEAD_DIM)
        qh, kh, vh = q[:, sl], k[:, sl], v[:, sl]
        prev = st_ref[h]
        inner = (_dot_nt(qh, kh) * decay_ref[h]).astype(BF16)
        o = _dot(inner, vh) + _dot(qh, prev.astype(BF16)) * xi[:, sl]
        st_ref[h] = prev * cd[:, sl] + _dot(kz_t[sl, :], vh)
        mu = jnp.mean(o, axis=-1, keepdims=True)
        d = o - mu
        var = jnp.mean(d * d, axis=-1, keepdims=True)
        outs.append(d * lax.rsqrt(var + EPS))
    y = jnp.concatenate(outs, axis=-1) * gn_ref[...] * _silu(z_ref[...])
    o_ref[...] = y.astype(o_ref.dtype)


def _retention(qkv, z_all, gn_w, batch, seq):
    c = RET_CHUNK
    nch = seq // c
    decay, xi_tab, zeta_tab, cd_tab = _ret_tables()

    def full(shape):
        return pl.BlockSpec(shape, lambda b, n: (0,) * len(shape))

    def blk(col):
        return pl.BlockSpec((c, GROUP_W), lambda b, n: (b * nch + n, col))

    return pl.pallas_call(
        _ret_kernel,
        grid=(batch, nch),
        in_specs=[full((HEADS, c, c)), full((c, GROUP_W)), full((c, GROUP_W)), full((1, GROUP_W)),
                  full((1, GROUP_W)), blk(0), blk(1), blk(2), blk(2)],
        out_specs=blk(0),
        out_shape=jax.ShapeDtypeStruct((batch * seq, GROUP_W), BF16),
        scratch_shapes=[pltpu.VMEM((HEADS, HEAD_DIM, HEAD_DIM), F32)],
        compiler_params=_params(("parallel", "arbitrary")),
        name="retention",
    )(decay, xi_tab, zeta_tab, cd_tab, gn_w.reshape(1, GROUP_W), qkv, qkv, qkv, z_all)


def _ssd_kernel(cw_ref, cb_ref, dtb_ref, a_ref, dsk_ref, nw_ref, xbc_ref, misc_ref, z_ref, o_ref,
                ext_ref, st_ref):
    n = pl.program_id(1)
    L = SSM_CHUNK
    hi = lax.Precision.HIGHEST

    @pl.when(n == 0)
    def _():
        st_ref[...] = jnp.zeros(st_ref.shape, F32)
        ext_ref[0:8, :] = jnp.zeros((8, CONV_CH), F32)

    raw = xbc_ref[...]
    ext_ref[8:8 + L, :] = raw
    conv = cb_ref[...] + raw * cw_ref[CONV_W - 1:CONV_W, :]
    for w in range(CONV_W - 1):
        shift = CONV_W - 1 - w
        conv = conv + ext_ref[8 - shift:8 - shift + L, :] * cw_ref[w:w + 1, :]
    ext_ref[0:8, :] = raw[L - 8:L, :]
    xc = _silu(conv)
    x = xc[:, 0:GROUP_W]
    bm = xc[:, GROUP_W:GROUP_W + 2 * SSM_STATE].astype(BF16)
    cm = xc[:, GROUP_W + 2 * SSM_STATE:].astype(BF16)

    dt_full = jax.nn.softplus(misc_ref[...] + dtb_ref[...])
    da = dt_full * a_ref[...]
    row = lax.broadcasted_iota(jnp.int32, (L, L), 0)
    col = lax.broadcasted_iota(jnp.int32, (L, L), 1)
    tril = jnp.where(row >= col, 1.0, 0.0).astype(F32)
    cs_col = jnp.dot(tril, da, precision=hi, preferred_element_type=F32)
    cs_row = lax.dot_general(da, tril, (((0,), (1,)), ((), ())), precision=hi,
                             preferred_element_type=F32)
    causal = row >= col
    dsk = dsk_ref[...]

    outs = []
    for h in range(HEADS):
        g = h // 2
        c0 = DT_COL + h
        sl = slice(h * HEAD_DIM, (h + 1) * HEAD_DIM)
        gs = slice(g * SSM_STATE, (g + 1) * SSM_STATE)
        cs_c = cs_col[:, c0:c0 + 1]
        cs_r = cs_row[c0:c0 + 1, :]
        cs_last = cs_col[L - 1:L, c0:c0 + 1]
        xh = x[:, sl]
        xdt = xh * dt_full[:, c0:c0 + 1]
        seg = jnp.exp(jnp.where(causal, cs_c - cs_r, NEG_INF))
        cb = _dot_nt(cm[:, gs], bm[:, gs])
        y = _dot((cb * seg).astype(BF16), xdt.astype(BF16))
        prev = st_ref[h]
        y = y + _dot(cm[:, gs], prev.astype(BF16)) * jnp.exp(cs_c)
        y = y + dsk[:, sl] * xh
        dec = jnp.exp(cs_last - cs_c)
        st_ref[h] = prev * jnp.exp(cs_last) + _dot_tn(bm[:, gs], (xdt * dec).astype(BF16))
        outs.append(y)
    y = jnp.concatenate(outs, axis=-1) * _silu(z_ref[...])
    ms = jnp.mean(y * y, axis=-1, keepdims=True)
    o_ref[...] = (y * lax.rsqrt(ms + EPS) * nw_ref[...]).astype(o_ref.dtype)


def _ssd(xbc, misc, z_all, conv_w, conv_b, dt_bias, a_log, d_skip, norm_w, batch, seq):
    L = SSM_CHUNK
    nch = seq // L
    dtb = jnp.zeros((1, 128), F32).at[0, DT_COL:DT_COL + HEADS].set(dt_bias)
    a_full = jnp.zeros((1, 128), F32).at[0, DT_COL:DT_COL + HEADS].set(-jnp.exp(a_log))
    dsk = jnp.repeat(d_skip, HEAD_DIM)[None, :]

    def full(shape):
        return pl.BlockSpec(shape, lambda b, n: (0,) * len(shape))

    return pl.pallas_call(
        _ssd_kernel,
        grid=(batch, nch),
        in_specs=[full((CONV_W, CONV_CH)), full((1, CONV_CH)), full((1, 128)), full((1, 128)),
                  full((1, GROUP_W)), full((1, GROUP_W)),
                  pl.BlockSpec((L, CONV_CH), lambda b, n: (b * nch + n, 0)),
                  pl.BlockSpec((L, 128), lambda b, n: (b * nch + n, 0)),
                  pl.BlockSpec((L, GROUP_W), lambda b, n: (b * nch + n, 3))],
        out_specs=pl.BlockSpec((L, GROUP_W), lambda b, n: (b * nch + n, 0)),
        out_shape=jax.ShapeDtypeStruct((batch * seq, GROUP_W), BF16),
        scratch_shapes=[pltpu.VMEM((8 + L, CONV_CH), F32),
                        pltpu.VMEM((HEADS, SSM_STATE, HEAD_DIM), F32)],
        compiler_params=_params(("parallel", "arbitrary")),
        name="ssd",
    )(conv_w, conv_b.reshape(1, CONV_CH), dtb, a_full, dsk, norm_w.reshape(1, GROUP_W), xbc, misc, z_all)


def _pick_tile(n, pref):
    t = pref
    while n % t:
        t //= 2
    return t


def kernel(x, norm_w, w_in, w_out, nsa_pe_k, nsa_pe_v, nsa_w_ck1, nsa_w_ck2, nsa_w_cv1, nsa_w_cv2,
           diff_lam_q1, diff_lam_k1, diff_lam_q2, diff_lam_k2, diff_subln_w, ret_gn_w,
           ssm_conv_w, ssm_conv_b, ssm_dt_bias, ssm_A_log, ssm_D, ssm_norm_w, final_norm_w):
    batch, seq, _ = x.shape
    depth = w_in.shape[0]
    m = batch * seq
    tm = _pick_tile(m, 512)
    tq = _pick_tile(seq, 256)
    w_r = _relayout_w_in(w_in)
    w_out_b = w_out.astype(BF16)
    x2d = x.reshape(m, D_MODEL)
    for i in range(depth):
        nsa_q, nsa_kvsw, nsa_cmp, misc, z_all, diff_qkv, ret_qkv, xbc = _in_proj(x2d, norm_w[i], w_r[i], tm)
        kc, vct = _nsa_compress(nsa_cmp, nsa_pe_k[i], nsa_pe_v[i], nsa_w_ck1[i], nsa_w_ck2[i],
                                nsa_w_cv1[i], nsa_w_cv2[i], batch, seq)
        y_nsa = _nsa_attention(nsa_q, nsa_kvsw, kc, vct, misc, z_all, batch, seq, tq)
        lam_p = jnp.stack([diff_lam_q1[i], diff_lam_k1[i], diff_lam_q2[i], diff_lam_k2[i]])
        y_diff = _diff_attention(diff_qkv, z_all, lam_p, diff_subln_w[i], batch, seq, i, tq)
        y_ret = _retention(ret_qkv, z_all, ret_gn_w[i], batch, seq)
        y_ssm = _ssd(xbc, misc, z_all, ssm_conv_w[i], ssm_conv_b[i], ssm_dt_bias[i], ssm_A_log[i],
                     ssm_D[i], ssm_norm_w[i], batch, seq)
        x2d = _out_proj((y_nsa, y_diff, y_ret, y_ssm), w_out_b[i], x2d, final_norm_w, tm, i == depth - 1)
    return x2d.reshape(batch, seq, D_MODEL)
```

```python
import functools
import math

import numpy as np
import jax
import jax.numpy as jnp
from jax import lax
from jax.experimental import pallas as pl
from jax.experimental.pallas import tpu as pltpu

F32 = jnp.float32
BF16 = jnp.bfloat16
NEG_INF = float("-inf")
LOG2E = 1.4426950408889634

D_MODEL = 1024
DEPTH = 4
GROUP_W = 256
HEADS = 4
HEAD_DIM = 64
EPS = 1e-6
CMP_LEN = 32
CMP_STRIDE = 16
CMP_HIDDEN = 256
SLC_BLOCK = 64
SLC_SHIFT = 6
SLC_TOPK = 16
WINDOW = 512
DIFF_QK_DIM = 32
RET_CHUNK = 128
SSM_STATE = 128
SSM_CHUNK = 128
CONV_W = 4
CONV_CH = 768
N_ALIBI_HEADS = 8
LANES = 128
BF16_ROWS = 16
ACC_ROWS = HEAD_DIM + BF16_ROWS
QK_LOOKAHEAD = 4

IN_LAYOUT = (
    ("nsa_q", 256), ("nsa_k_cmp", 64), ("nsa_v_cmp", 64), ("nsa_k_slc", 64), ("nsa_v_slc", 64),
    ("nsa_k_win", 64), ("nsa_v_win", 64), ("nsa_gate", 12), ("nsa_z", 256),
    ("diff_q", 256), ("diff_k", 256), ("diff_v", 256), ("diff_z", 256),
    ("ret_q", 256), ("ret_k", 256), ("ret_v", 256), ("ret_z", 256),
    ("ssm_z", 256), ("ssm_xbc", 768), ("ssm_dt", 4),
)
IN_OFF = {}
_o = 0
for _n, _w in IN_LAYOUT:
    IN_OFF[_n] = (_o, _w)
    _o += _w
IN_W = _o

GATE_COL = 0
DT_COL = 12
IN_OUTPUTS = (
    ("nsa_q", BF16, ("nsa_q",), 256),
    ("nsa_k2", BF16, ("nsa_k_slc", "nsa_k_win"), 128),
    ("nsa_cmp", F32, ("nsa_k_cmp", "nsa_v_cmp"), 128),
    ("misc", F32, ("nsa_gate", "ssm_dt"), 128),
    ("z_all", F32, ("nsa_z", "diff_z", "ret_z", "ssm_z"), 1024),
    ("diff_qk", BF16, ("diff_q", "diff_k"), 512),
    ("ret_qkv", BF16, ("ret_q", "ret_k", "ret_v"), 768),
    ("xbc", F32, ("ssm_xbc",), 768),
)
IN_T_SRC = ("diff_v", "nsa_v_slc", "nsa_v_win")
IN_T_ROWS = 384
DIFF_VT_BLK = 0
NSA_VT_BLK = 2
IN_SEGS = []
_o = 0
for _n, _dt, _src, _w in IN_OUTPUTS:
    IN_SEGS.append((_o, _o + _w))
    _o += _w
IN_WP = _o

VMEM_LIMIT = 56 * 1024 * 1024


def _alibi_slopes():
    return [2.0 ** (-8.0 * (i + 1) / N_ALIBI_HEADS) for i in range(N_ALIBI_HEADS)]


NSA_SLOPES = _alibi_slopes()[0::2]
DIFF_SLOPES = _alibi_slopes()[1::2]


def _silu(x):
    return x * jax.nn.sigmoid(x)


def _dot(a, b):
    return jnp.dot(a, b, preferred_element_type=F32)


def _dot_nt(a, b):
    return lax.dot_general(a, b, (((1,), (1,)), ((), ())), preferred_element_type=F32)


def _dot_tn(a, b):
    return lax.dot_general(a, b, (((0,), (0,)), ((), ())), preferred_element_type=F32)


def _params(sem):
    return pltpu.CompilerParams(dimension_semantics=sem, vmem_limit_bytes=VMEM_LIMIT)


def _relayout_w_in(w_in):
    cols = []
    for _n, _dt, src, width in IN_OUTPUTS:
        used = 0
        for s in src:
            off, w = IN_OFF[s]
            cols.append(w_in[:, :, off:off + w])
            used += w
        if used < width:
            cols.append(jnp.zeros(w_in.shape[:2] + (width - used,), w_in.dtype))
    return jnp.concatenate(cols, axis=-1).astype(BF16)


def _relayout_w_in_t(w_in):
    cols = [w_in[:, :, IN_OFF[s][0]:IN_OFF[s][0] + IN_OFF[s][1]] for s in IN_T_SRC]
    return jnp.swapaxes(jnp.concatenate(cols, axis=-1), 1, 2).astype(BF16)


def _in_proj_kernel(x_ref, nw_ref, w_ref, wt_ref, *out_refs):
    x = x_ref[...]
    ms = jnp.mean(x * x, axis=-1, keepdims=True)
    h = (x * lax.rsqrt(ms + EPS) * nw_ref[...]).astype(BF16)
    for ref, (a, b) in zip(out_refs[:-1], IN_SEGS):
        ref[...] = _dot(h, w_ref[:, a:b]).astype(ref.dtype)
    out_refs[-1][...] = _dot_nt(wt_ref[...], h).astype(BF16)


def _in_proj(x2d, norm_w, w_r, w_t, tm):
    m = x2d.shape[0]
    out_shape = [jax.ShapeDtypeStruct((m, b - a), dt) for (_n, dt, _s, _w), (a, b) in zip(IN_OUTPUTS, IN_SEGS)]
    out_specs = [pl.BlockSpec((tm, b - a), lambda i: (i, 0)) for (a, b) in IN_SEGS]
    out_shape.append(jax.ShapeDtypeStruct((IN_T_ROWS, m), BF16))
    out_specs.append(pl.BlockSpec((IN_T_ROWS, tm), lambda i: (0, i)))
    return pl.pallas_call(
        _in_proj_kernel,
        grid=(m // tm,),
        in_specs=[pl.BlockSpec((tm, D_MODEL), lambda i: (i, 0)),
                  pl.BlockSpec((1, D_MODEL), lambda i: (0, 0)),
                  pl.BlockSpec((D_MODEL, IN_WP), lambda i: (0, 0)),
                  pl.BlockSpec((IN_T_ROWS, D_MODEL), lambda i: (0, 0))],
        out_specs=out_specs,
        out_shape=out_shape,
        compiler_params=_params(("parallel",)),
        name="in_proj",
    )(x2d, norm_w.reshape(1, D_MODEL), w_r, w_t)


def _out_proj_kernel(y0_ref, y1_ref, y2_ref, y3_ref, w_ref, x_ref, fw_ref, o_ref, *, final):
    acc = x_ref[...]
    for g, y_ref in enumerate((y0_ref, y1_ref, y2_ref, y3_ref)):
        acc = acc + _dot(y_ref[...], w_ref[g * GROUP_W:(g + 1) * GROUP_W, :])
    if final:
        ms = jnp.mean(acc * acc, axis=-1, keepdims=True)
        acc = acc * lax.rsqrt(ms + EPS) * fw_ref[...]
    o_ref[...] = acc


def _out_proj(ys, w_out_b, x2d, final_w, tm, final):
    m = x2d.shape[0]
    yspec = pl.BlockSpec((tm, GROUP_W), lambda i: (i, 0))
    return pl.pallas_call(
        functools.partial(_out_proj_kernel, final=final),
        grid=(m // tm,),
        in_specs=[yspec, yspec, yspec, yspec,
                  pl.BlockSpec((D_MODEL, D_MODEL), lambda i: (0, 0)),
                  pl.BlockSpec((tm, D_MODEL), lambda i: (i, 0)),
                  pl.BlockSpec((1, D_MODEL), lambda i: (0, 0))],
        out_specs=pl.BlockSpec((tm, D_MODEL), lambda i: (i, 0)),
        out_shape=jax.ShapeDtypeStruct((m, D_MODEL), F32),
        compiler_params=_params(("parallel",)),
        name="out_proj",
    )(*ys, w_out_b, x2d, final_w.reshape(1, D_MODEL))


def _flash_update_t(t2, shift, vt_ext, m_ref, acc_ref, idx):
    m_old = m_ref[idx]
    m_new = jnp.maximum(m_old, jnp.max(t2, axis=0, keepdims=True) - shift)
    alpha = jnp.exp2(m_old - m_new)
    p = jnp.exp2(t2 - (m_new + shift))
    acc_ref[idx] = alpha * acc_ref[idx] + _dot(vt_ext, p.astype(BF16))
    m_ref[idx] = m_new


def _normalized(acc):
    return acc[0:HEAD_DIM] / jnp.maximum(acc[HEAD_DIM:HEAD_DIM + 1], 1e-30)


def _rel_t(tq):
    return lax.broadcasted_iota(jnp.int32, (tq, tq), 1) - lax.broadcasted_iota(jnp.int32, (tq, tq), 0)


def _diff_kernel(lam_ref, sw_ref, q_ref, k_ref, vt_ref, z_ref, o_ref, m_ref, acc_ref, wq_ref, b2_ref,
                 *, tq, lam_init):
    qi = pl.program_id(1)
    c1 = (DIFF_QK_DIM ** -0.5) * LOG2E
    m_ref[...] = jnp.full(m_ref.shape, NEG_INF, F32)
    acc_ref[...] = jnp.zeros(acc_ref.shape, F32)
    rel = _rel_t(tq)
    rel_f = rel.astype(F32)
    for h in range(HEADS):
        b2_ref[h] = (DIFF_SLOPES[h] * LOG2E) * rel_f

    qt = q_ref[...].astype(F32).T.astype(BF16)
    row = lax.broadcasted_iota(jnp.int32, (LANES, tq), 0)
    for j in range(2 * HEADS):
        g, r0 = divmod(j * DIFF_QK_DIM, LANES)
        qg = qt[g * LANES:(g + 1) * LANES]
        wq_ref[j] = jnp.where((row >= r0) & (row < r0 + DIFF_QK_DIM), qg, jnp.zeros_like(qg))

    ones = jnp.ones((BF16_ROWS, tq), BF16)

    def tile(ki, diag):
        start = pl.multiple_of(ki * tq, tq)
        k = k_ref[pl.ds(start, tq), :]
        vt = vt_ref[:, pl.ds(start, tq)]
        off = ((qi - ki) * tq).astype(F32)

        def scores(j):
            g = (j * DIFF_QK_DIM) // LANES
            return _dot(k[:, g * LANES:(g + 1) * LANES], wq_ref[j])

        pending = [scores(j) for j in range(QK_LOOKAHEAD)]
        for j in range(2 * HEADS):
            h = j // 2
            s = pending.pop(0)
            if j + QK_LOOKAHEAD < 2 * HEADS:
                pending.append(scores(j + QK_LOOKAHEAD))
            vt_ext = jnp.concatenate([vt[h * HEAD_DIM:(h + 1) * HEAD_DIM], ones], axis=0)
            t2 = s * c1 - b2_ref[h]
            if diag:
                t2 = jnp.where(rel >= 0, t2, NEG_INF)
            _flash_update_t(t2, (DIFF_SLOPES[h] * LOG2E) * off, vt_ext, m_ref, acc_ref, j)

    def body(ki, carry):
        tile(ki, False)
        return carry

    lax.fori_loop(0, qi, body, 0)
    tile(qi, True)

    lp = lam_ref[...]
    lam = (jnp.exp(jnp.sum(lp[0:1] * lp[1:2], axis=-1, keepdims=True))
           - jnp.exp(jnp.sum(lp[2:3] * lp[3:4], axis=-1, keepdims=True)) + lam_init)
    sw = sw_ref[...]
    o_t = jnp.concatenate([_normalized(acc_ref[2 * h]) - lam * _normalized(acc_ref[2 * h + 1])
                           for h in range(HEADS)], axis=0)
    o = o_t.T
    outs = []
    for h in range(HEADS):
        oh = o[:, h * HEAD_DIM:(h + 1) * HEAD_DIM]
        ms = jnp.mean(oh * oh, axis=-1, keepdims=True)
        outs.append(oh * lax.rsqrt(ms + EPS) * sw * (1.0 - lam_init))
    y = jnp.concatenate(outs, axis=-1) * _silu(z_ref[...])
    o_ref[...] = y.astype(o_ref.dtype)


def _diff_attention(qk, vt_all, z_all, lam_p, subln_w, batch, seq, layer_idx, tq):
    nq = seq // tq
    lam_init = 0.8 - 0.6 * math.exp(-0.3 * layer_idx)
    return pl.pallas_call(
        functools.partial(_diff_kernel, tq=tq, lam_init=lam_init),
        grid=(batch, nq),
        in_specs=[pl.BlockSpec((4, DIFF_QK_DIM), lambda b, i: (0, 0)),
                  pl.BlockSpec((1, HEAD_DIM), lambda b, i: (0, 0)),
                  pl.BlockSpec((tq, GROUP_W), lambda b, i: (b * nq + i, 0)),
                  pl.BlockSpec((seq, GROUP_W), lambda b, i: (b, 1)),
                  pl.BlockSpec((GROUP_W, seq), lambda b, i: (DIFF_VT_BLK, b)),
                  pl.BlockSpec((tq, GROUP_W), lambda b, i: (b * nq + i, 1))],
        out_specs=pl.BlockSpec((tq, GROUP_W), lambda b, i: (b * nq + i, 0)),
        out_shape=jax.ShapeDtypeStruct((batch * seq, GROUP_W), BF16),
        scratch_shapes=[pltpu.VMEM((2 * HEADS, 1, tq), F32),
                        pltpu.VMEM((2 * HEADS, ACC_ROWS, tq), F32),
                        pltpu.VMEM((2 * HEADS, LANES, tq), BF16),
                        pltpu.VMEM((HEADS, tq, tq), F32)],
        compiler_params=_params(("parallel", "parallel")),
        name="diff_attention",
    )(lam_p, subln_w.reshape(1, HEAD_DIM), qk, qk, vt_all, z_all)


def _nsa_compress_kernel(gk_ref, gv_ref, pek_ref, pev_ref, wk1_ref, wk2_ref, wv1_ref, wv2t_ref, kc_ref, vct_ref):
    half = CMP_STRIDE * HEAD_DIM

    def hidden(g_ref, pe_ref, w1_ref):
        g = g_ref[0]
        top = (g + pe_ref[0:1, :]).astype(BF16)
        bot = (g + pe_ref[1:2, :]).astype(BF16)
        a = _dot(top, w1_ref[0:half, :])
        b = _dot(bot, w1_ref[half:2 * half, :])
        nrow = b.shape[0]
        b = pltpu.roll(b, nrow - 1, 0)
        return _silu(a + b).astype(BF16)

    hk = hidden(gk_ref, pek_ref, wk1_ref)
    kc_ref[0] = _dot(hk, wk2_ref[...]).astype(kc_ref.dtype)
    hv = hidden(gv_ref, pev_ref, wv1_ref)
    vct_ref[0] = _dot_nt(wv2t_ref[...], hv).astype(vct_ref.dtype)


def _nsa_compress(cmp2d, pe_k, pe_v, w_ck1, w_ck2, w_cv1, w_cv2, batch, seq):
    ng = seq // CMP_STRIDE
    half = CMP_STRIDE * HEAD_DIM
    gk = cmp2d[:, :HEAD_DIM].reshape(batch, ng, half)
    gv = cmp2d[:, HEAD_DIM:].reshape(batch, ng, half)
    gspec = pl.BlockSpec((1, ng, half), lambda b: (b, 0, 0))

    def full(shape):
        return pl.BlockSpec(shape, lambda b: (0,) * len(shape))

    return pl.pallas_call(
        _nsa_compress_kernel,
        grid=(batch,),
        in_specs=[gspec, gspec, full((2, half)), full((2, half)),
                  full((2 * half, CMP_HIDDEN)), full((CMP_HIDDEN, HEAD_DIM)),
                  full((2 * half, CMP_HIDDEN)), full((HEAD_DIM, CMP_HIDDEN))],
        out_specs=[pl.BlockSpec((1, ng, HEAD_DIM), lambda b: (b, 0, 0)),
                   pl.BlockSpec((1, HEAD_DIM, ng), lambda b: (b, 0, 0))],
        out_shape=[jax.ShapeDtypeStruct((batch, ng, HEAD_DIM), BF16),
                   jax.ShapeDtypeStruct((batch, HEAD_DIM, ng), BF16)],
        compiler_params=_params(("parallel",)),
        name="nsa_compress",
    )(gk, gv, pe_k.reshape(2, half), pe_v.reshape(2, half),
      w_ck1.astype(BF16), w_ck2.astype(BF16), w_cv1.astype(BF16), w_cv2.T.astype(BF16))


def _nsa_kernel(ovl_ref, q_ref, k_ref, vt_ref, kc_ref, vct_ref, misc_ref, z_ref, o_ref,
                m_ref, acc_ref, wq_ref, b2_ref, *, tq, seq):
    qi = pl.program_id(1)
    scale = HEAD_DIM ** -0.5
    c1 = scale * LOG2E
    ng = seq // CMP_STRIDE
    ns = seq // SLC_BLOCK
    top = min(SLC_TOPK, ns)
    m_ref[...] = jnp.full(m_ref.shape, NEG_INF, F32)
    acc_ref[...] = jnp.zeros(acc_ref.shape, F32)
    rel = _rel_t(tq)
    rel_f = rel.astype(F32)
    for h in range(HEADS):
        b2_ref[h] = (NSA_SLOPES[h] * LOG2E) * rel_f

    qt = q_ref[...].astype(F32).T.astype(BF16)
    zeros_q = jnp.zeros((HEAD_DIM, tq), BF16)
    for h in range(HEADS):
        qh = qt[h * HEAD_DIM:(h + 1) * HEAD_DIM]
        wq_ref[h] = jnp.concatenate([qh, zeros_q], axis=0)
        wq_ref[HEADS + h] = jnp.concatenate([zeros_q, qh], axis=0)

    t_lane = qi * tq + lax.broadcasted_iota(jnp.int32, (1, tq), 1)
    t_lane_f = t_lane.astype(F32)
    n_sub = lax.broadcasted_iota(jnp.int32, (ng, 1), 0)
    c_center = (n_sub * CMP_STRIDE).astype(F32) + (CMP_LEN - 1) / 2.0
    c_valid = (n_sub * CMP_STRIDE + (CMP_LEN - 1)) <= t_lane
    dist_c = t_lane_f - c_center
    kc = kc_ref[0]
    vct = vct_ref[0]
    ovl = ovl_ref[...]
    imp = jnp.zeros((ns, tq), F32)
    o_cmp = []
    for h in range(HEADS):
        s = _dot(kc, qt[h * HEAD_DIM:(h + 1) * HEAD_DIM]) * scale - NSA_SLOPES[h] * dist_c
        s = jnp.where(c_valid, s, NEG_INF)
        mx = jnp.max(s, axis=0, keepdims=True)
        mx = jnp.where(mx > NEG_INF, mx, 0.0)
        p = jnp.exp(s - mx)
        p = p / jnp.maximum(jnp.sum(p, axis=0, keepdims=True), 1e-30)
        pb = p.astype(BF16)
        o_cmp.append(_dot(vct, pb))
        imp = imp + _dot(ovl, pb)

    j_sub = lax.broadcasted_iota(jnp.int32, (ns, 1), 0)
    j_sub_f = j_sub.astype(F32)
    cur = jnp.right_shift(t_lane, SLC_SHIFT)
    forced = (j_sub == 0) | (j_sub == cur) | (j_sub == cur - 1)
    valid = (j_sub * SLC_BLOCK) <= t_lane
    score = jnp.where(forced, 1e30, jnp.where(valid, imp, -1.0))
    sel = jnp.zeros((ns, tq), F32)
    for _ in range(top):
        mx = jnp.max(score, axis=0, keepdims=True)
        idx = jnp.min(jnp.where(score == mx, j_sub_f, float(ns)), axis=0, keepdims=True)
        pick = j_sub_f == idx
        sel = jnp.where(pick, 1.0, sel)
        score = jnp.where(pick, -2.0, score)
    sel_b = sel.astype(BF16)

    blk_of_row = jnp.right_shift(lax.broadcasted_iota(jnp.int32, (tq, ns), 0), SLC_SHIFT)
    blk_col = lax.broadcasted_iota(jnp.int32, (tq, ns), 1)
    ones = jnp.ones((BF16_ROWS, tq), BF16)

    def slc_tile(ki, diag):
        start = pl.multiple_of(ki * tq, tq)
        k = k_ref[pl.ds(start, tq), :]
        vt = vt_ref[0:HEAD_DIM, pl.ds(start, tq)]
        vt_ext = jnp.concatenate([vt, ones], axis=0)
        expand = jnp.where(blk_col == blk_of_row + ki * (tq // SLC_BLOCK), 1.0, 0.0).astype(BF16)
        keep = _dot(expand, sel_b) > 0.5
        if diag:
            keep = keep & (rel >= 0)
        off = ((qi - ki) * tq).astype(F32)
        pending = [_dot(k, wq_ref[h]) for h in range(HEADS)]
        for h in range(HEADS):
            t2 = pending[h] * c1 - b2_ref[h]
            t2 = jnp.where(keep, t2, NEG_INF)
            _flash_update_t(t2, (NSA_SLOPES[h] * LOG2E) * off, vt_ext, m_ref, acc_ref, h)

    def win_tile(ki):
        start = pl.multiple_of(ki * tq, tq)
        k = k_ref[pl.ds(start, tq), :]
        vt = vt_ref[HEAD_DIM:2 * HEAD_DIM, pl.ds(start, tq)]
        vt_ext = jnp.concatenate([vt, ones], axis=0)
        dist = rel + (qi - ki) * tq
        keep = (dist >= 0) & (dist < WINDOW)
        off = ((qi - ki) * tq).astype(F32)
        pending = [_dot(k, wq_ref[HEADS + h]) for h in range(HEADS)]
        for h in range(HEADS):
            t2 = pending[h] * c1 - b2_ref[h]
            t2 = jnp.where(keep, t2, NEG_INF)
            _flash_update_t(t2, (NSA_SLOPES[h] * LOG2E) * off, vt_ext, m_ref, acc_ref, HEADS + h)

    def body(ki, carry):
        slc_tile(ki, False)
        return carry

    lax.fori_loop(0, qi, body, 0)
    slc_tile(qi, True)

    win_tile(qi)
    for off_t in range(1, (WINDOW + tq - 1) // tq + 1):
        @pl.when(qi >= off_t)
        def _():
            win_tile(qi - off_t)

    g_t = jax.nn.sigmoid(misc_ref[...]).T
    outs = []
    for h in range(HEADS):
        r0 = GATE_COL + 3 * h
        outs.append(g_t[r0:r0 + 1] * o_cmp[h] + g_t[r0 + 1:r0 + 2] * _normalized(acc_ref[h])
                    + g_t[r0 + 2:r0 + 3] * _normalized(acc_ref[HEADS + h]))
    y = jnp.concatenate(outs, axis=0).T * _silu(z_ref[...])
    o_ref[...] = y.astype(o_ref.dtype)


def _overlap_t(seq):
    nc = (seq - CMP_LEN) // CMP_STRIDE + 1
    ng = seq // CMP_STRIDE
    ns = seq // SLC_BLOCK
    c_start = np.arange(ng) * CMP_STRIDE
    c_end = c_start + CMP_LEN - 1
    s_start = np.arange(ns) * SLC_BLOCK
    s_end = s_start + SLC_BLOCK - 1
    ov = (c_start[None, :] <= s_end[:, None]) & (c_end[None, :] >= s_start[:, None]) & (np.arange(ng)[None, :] < nc)
    return jnp.asarray(ov.astype(np.float32), dtype=BF16)


def _nsa_attention(q, k2, vt_all, kc, vct, misc, z_all, batch, seq, tq):
    nq = seq // tq
    ng = seq // CMP_STRIDE
    ns = seq // SLC_BLOCK
    return pl.pallas_call(
        functools.partial(_nsa_kernel, tq=tq, seq=seq),
        grid=(batch, nq),
        in_specs=[pl.BlockSpec((ns, ng), lambda b, i: (0, 0)),
                  pl.BlockSpec((tq, GROUP_W), lambda b, i: (b * nq + i, 0)),
                  pl.BlockSpec((seq, LANES), lambda b, i: (b, 0)),
                  pl.BlockSpec((2 * HEAD_DIM, seq), lambda b, i: (NSA_VT_BLK, b)),
                  pl.BlockSpec((1, ng, HEAD_DIM), lambda b, i: (b, 0, 0)),
                  pl.BlockSpec((1, HEAD_DIM, ng), lambda b, i: (b, 0, 0)),
                  pl.BlockSpec((tq, LANES), lambda b, i: (b * nq + i, 0)),
                  pl.BlockSpec((tq, GROUP_W), lambda b, i: (b * nq + i, 0))],
        out_specs=pl.BlockSpec((tq, GROUP_W), lambda b, i: (b * nq + i, 0)),
        out_shape=jax.ShapeDtypeStruct((batch * seq, GROUP_W), BF16),
        scratch_shapes=[pltpu.VMEM((2 * HEADS, 1, tq), F32),
                        pltpu.VMEM((2 * HEADS, ACC_ROWS, tq), F32),
                        pltpu.VMEM((2 * HEADS, LANES, tq), BF16),
                        pltpu.VMEM((HEADS, tq, tq), F32)],
        compiler_params=_params(("parallel", "parallel")),
        name="nsa_attention",
    )(_overlap_t(seq), q, k2, vt_all, kc, vct, misc, z_all)


def _ret_tables():
    c = RET_CHUNK
    h = np.arange(HEADS, dtype=np.float32)
    log_g = jnp.log(1.0 - 2.0 ** (-5.0 - jnp.asarray(h)))
    pos = jnp.arange(c, dtype=F32)
    rel = pos[:, None] - pos[None, :]
    decay = jnp.where(rel >= 0, jnp.exp(log_g[:, None, None] * jnp.maximum(rel, 0.0)), 0.0)
    xi = jnp.exp(log_g[:, None] * (pos + 1.0))
    zeta = jnp.exp(log_g[:, None] * (c - 1.0 - pos))
    chunk_decay = jnp.exp(log_g * c)
    xi_tab = jnp.repeat(xi.T, HEAD_DIM, axis=1)
    zeta_tab = jnp.repeat(zeta.T, HEAD_DIM, axis=1)
    cd_tab = jnp.repeat(chunk_decay, HEAD_DIM)[None, :]
    return decay, xi_tab, zeta_tab, cd_tab


def _ret_kernel(decay_ref, xi_ref, zeta_ref, cd_ref, gn_ref, q_ref, k_ref, v_ref, z_ref, o_ref, st_ref):
    n = pl.program_id(1)

    @pl.when(n == 0)
    def _():
        st_ref[...] = jnp.zeros(st_ref.shape, F32)

    q = (q_ref[...].astype(F32) * (HEAD_DIM ** -0.5)).astype(BF16)
    k = k_ref[...]
    v = v_ref[...]
    kz_t = (k.astype(F32) * zeta_ref[...]).T.astype(BF16)
    xi = xi_ref[...]
    cd = cd_ref[...]
    outs = []
    for h in range(HEADS):
        sl = slice(h * HEAD_DIM, (h + 1) * HEAD_DIM)
        qh, kh, vh = q[:, sl], k[:, sl], v[:, sl]
        prev = st_ref[h]
        inner = (_dot_nt(qh, kh) * decay_ref[h]).astype(BF16)
        o = _dot(inner, vh) + _dot(qh, prev.astype(BF16)) * xi[:, sl]
        st_ref[h] = prev * cd[:, sl] + _dot(kz_t[sl, :], vh)
        mu = jnp.mean(o, axis=-1, keepdims=True)
        d = o - mu
        var = jnp.mean(d * d, axis=-1, keepdims=True)
        outs.append(d * lax.rsqrt(var + EPS))
    y = jnp.concatenate(outs, axis=-1) * gn_ref[...] * _silu(z_ref[...])
    o_ref[...] = y.astype(o_ref.dtype)


def _retention(qkv, z_all, gn_w, batch, seq):
    c = RET_CHUNK
    nch = seq // c
    decay, xi_tab, zeta_tab, cd_tab = _ret_tables()

    def full(shape):
        return pl.BlockSpec(shape, lambda b, n: (0,) * len(shape))

    def blk(col):
        return pl.BlockSpec((c, GROUP_W), lambda b, n: (b * nch + n, col))

    return pl.pallas_call(
        _ret_kernel,
        grid=(batch, nch),
        in_specs=[full((HEADS, c, c)), full((c, GROUP_W)), full((c, GROUP_W)), full((1, GROUP_W)),
                  full((1, GROUP_W)), blk(0), blk(1), blk(2), blk(2)],
        out_specs=blk(0),
        out_shape=jax.ShapeDtypeStruct((batch * seq, GROUP_W), BF16),
        scratch_shapes=[pltpu.VMEM((HEADS, HEAD_DIM, HEAD_DIM), F32)],
        compiler_params=_params(("parallel", "arbitrary")),
        name="retention",
    )(decay, xi_tab, zeta_tab, cd_tab, gn_w.reshape(1, GROUP_W), qkv, qkv, qkv, z_all)


def _ssd_kernel(cw_ref, cb_ref, dtb_ref, a_ref, dsk_ref, nw_ref, xbc_ref, misc_ref, z_ref, o_ref,
                ext_ref, st_ref):
    n = pl.program_id(1)
    L = SSM_CHUNK
    hi = lax.Precision.HIGHEST

    @pl.when(n == 0)
    def _():
        st_ref[...] = jnp.zeros(st_ref.shape, F32)
        ext_ref[0:8, :] = jnp.zeros((8, CONV_CH), F32)

    raw = xbc_ref[...]
    ext_ref[8:8 + L, :] = raw
    conv = cb_ref[...] + raw * cw_ref[CONV_W - 1:CONV_W, :]
    for w in range(CONV_W - 1):
        shift = CONV_W - 1 - w
        conv = conv + ext_ref[8 - shift:8 - shift + L, :] * cw_ref[w:w + 1, :]
    ext_ref[0:8, :] = raw[L - 8:L, :]
    xc = _silu(conv)
    x = xc[:, 0:GROUP_W]
    bm = xc[:, GROUP_W:GROUP_W + 2 * SSM_STATE].astype(BF16)
    cm = xc[:, GROUP_W + 2 * SSM_STATE:].astype(BF16)

    dt_full = jax.nn.softplus(misc_ref[...] + dtb_ref[...])
    da = dt_full * a_ref[...]
    row = lax.broadcasted_iota(jnp.int32, (L, L), 0)
    col = lax.broadcasted_iota(jnp.int32, (L, L), 1)
    tril = jnp.where(row >= col, 1.0, 0.0).astype(F32)
    cs_col = jnp.dot(tril, da, precision=hi, preferred_element_type=F32)
    cs_row = lax.dot_general(da, tril, (((0,), (1,)), ((), ())), precision=hi,
                             preferred_element_type=F32)
    causal = row >= col
    dsk = dsk_ref[...]

    outs = []
    for h in range(HEADS):
        g = h // 2
        c0 = DT_COL + h
        sl = slice(h * HEAD_DIM, (h + 1) * HEAD_DIM)
        gs = slice(g * SSM_STATE, (g + 1) * SSM_STATE)
        cs_c = cs_col[:, c0:c0 + 1]
        cs_r = cs_row[c0:c0 + 1, :]
        cs_last = cs_col[L - 1:L, c0:c0 + 1]
        xh = x[:, sl]
        xdt = xh * dt_full[:, c0:c0 + 1]
        seg = jnp.exp(jnp.where(causal, cs_c - cs_r, NEG_INF))
        cb = _dot_nt(cm[:, gs], bm[:, gs])
        y = _dot((cb * seg).astype(BF16), xdt.astype(BF16))
        prev = st_ref[h]
        y = y + _dot(cm[:, gs], prev.astype(BF16)) * jnp.exp(cs_c)
        y = y + dsk[:, sl] * xh
        dec = jnp.exp(cs_last - cs_c)
        st_ref[h] = prev * jnp.exp(cs_last) + _dot_tn(bm[:, gs], (xdt * dec).astype(BF16))
        outs.append(y)
    y = jnp.concatenate(outs, axis=-1) * _silu(z_ref[...])
    ms = jnp.mean(y * y, axis=-1, keepdims=True)
    o_ref[...] = (y * lax.rsqrt(ms + EPS) * nw_ref[...]).astype(o_ref.dtype)


def _ssd(xbc, misc, z_all, conv_w, conv_b, dt_bias, a_log, d_skip, norm_w, batch, seq):
    L = SSM_CHUNK
    nch = seq // L
    dtb = jnp.zeros((1, 128), F32).at[0, DT_COL:DT_COL + HEADS].set(dt_bias)
    a_full = jnp.zeros((1, 128), F32).at[0, DT_COL:DT_COL + HEADS].set(-jnp.exp(a_log))
    dsk = jnp.repeat(d_skip, HEAD_DIM)[None, :]

    def full(shape):
        return pl.BlockSpec(shape, lambda b, n: (0,) * len(shape))

    return pl.pallas_call(
        _ssd_kernel,
        grid=(batch, nch),
        in_specs=[full((CONV_W, CONV_CH)), full((1, CONV_CH)), full((1, 128)), full((1, 128)),
                  full((1, GROUP_W)), full((1, GROUP_W)),
                  pl.BlockSpec((L, CONV_CH), lambda b, n: (b * nch + n, 0)),
                  pl.BlockSpec((L, 128), lambda b, n: (b * nch + n, 0)),
                  pl.BlockSpec((L, GROUP_W), lambda b, n: (b * nch + n, 3))],
        out_specs=pl.BlockSpec((L, GROUP_W), lambda b, n: (b * nch + n, 0)),
        out_shape=jax.ShapeDtypeStruct((batch * seq, GROUP_W), BF16),
        scratch_shapes=[pltpu.VMEM((8 + L, CONV_CH), F32),
                        pltpu.VMEM((HEADS, SSM_STATE, HEAD_DIM), F32)],
        compiler_params=_params(("parallel", "arbitrary")),
        name="ssd",
    )(conv_w, conv_b.reshape(1, CONV_CH), dtb, a_full, dsk, norm_w.reshape(1, GROUP_W), xbc, misc, z_all)


def _pick_tile(n, pref):
    t = pref
    while n % t:
        t //= 2
    return t


def kernel(x, norm_w, w_in, w_out, nsa_pe_k, nsa_pe_v, nsa_w_ck1, nsa_w_ck2, nsa_w_cv1, nsa_w_cv2,
           diff_lam_q1, diff_lam_k1, diff_lam_q2, diff_lam_k2, diff_subln_w, ret_gn_w,
           ssm_conv_w, ssm_conv_b, ssm_dt_bias, ssm_A_log, ssm_D, ssm_norm_w, final_norm_w):
    batch, seq, _ = x.shape
    depth = w_in.shape[0]
    m = batch * seq
    tm = _pick_tile(m, 512)
    tq = _pick_tile(seq, 256)
    w_r = _relayout_w_in(w_in)
    w_t = _relayout_w_in_t(w_in)
    w_out_b = w_out.astype(BF16)
    x2d = x.reshape(m, D_MODEL)
    for i in range(depth):
        nsa_q, nsa_k2, nsa_cmp, misc, z_all, diff_qk, ret_qkv, xbc, vt_all = _in_proj(
            x2d, norm_w[i], w_r[i], w_t[i], tm)
        kc, vct = _nsa_compress(nsa_cmp, nsa_pe_k[i], nsa_pe_v[i], nsa_w_ck1[i], nsa_w_ck2[i],
                                nsa_w_cv1[i], nsa_w_cv2[i], batch, seq)
        y_nsa = _nsa_attention(nsa_q, nsa_k2, vt_all, kc, vct, misc, z_all, batch, seq, tq)
        lam_p = jnp.stack([diff_lam_q1[i], diff_lam_k1[i], diff_lam_q2[i], diff_lam_k2[i]])
        y_diff = _diff_attention(diff_qk, vt_all, z_all, lam_p, diff_subln_w[i], batch, seq, i, tq)
        y_ret = _retention(ret_qkv, z_all, ret_gn_w[i], batch, seq)
        y_ssm = _ssd(xbc, misc, z_all, ssm_conv_w[i], ssm_conv_b[i], ssm_dt_bias[i], ssm_A_log[i],
                     ssm_D[i], ssm_norm_w[i], batch, seq)
        x2d = _out_proj((y_nsa, y_diff, y_ret, y_ssm), w_out_b[i], x2d, final_norm_w, tm, i == depth - 1)
    return x2d.reshape(batch, seq, D_MODEL)
```

```python
import functools
import math

import numpy as np
import jax
import jax.numpy as jnp
from jax import lax
from jax.experimental import pallas as pl
from jax.experimental.pallas import tpu as pltpu

F32 = jnp.float32
BF16 = jnp.bfloat16
NEG_INF = float("-inf")
LOG2E = 1.4426950408889634

D_MODEL = 1024
DEPTH = 4
GROUP_W = 256
HEADS = 4
HEAD_DIM = 64
EPS = 1e-6
CMP_LEN = 32
CMP_STRIDE = 16
CMP_HIDDEN = 256
SLC_BLOCK = 64
SLC_SHIFT = 6
SLC_TOPK = 16
WINDOW = 512
DIFF_QK_DIM = 32
RET_CHUNK = 128
SSM_STATE = 128
SSM_CHUNK = 128
CONV_W = 4
CONV_CH = 768
N_ALIBI_HEADS = 8
LANES = 128
BF16_ROWS = 16
ACC_ROWS = HEAD_DIM + BF16_ROWS
QK_LOOKAHEAD = 4
ALIBI_ROWS = 3
MASK_BIG = 2.0 ** 100
BLK_LANE0 = 64

IN_LAYOUT = (
    ("nsa_q", 256), ("nsa_k_cmp", 64), ("nsa_v_cmp", 64), ("nsa_k_slc", 64), ("nsa_v_slc", 64),
    ("nsa_k_win", 64), ("nsa_v_win", 64), ("nsa_gate", 12), ("nsa_z", 256),
    ("diff_q", 256), ("diff_k", 256), ("diff_v", 256), ("diff_z", 256),
    ("ret_q", 256), ("ret_k", 256), ("ret_v", 256), ("ret_z", 256),
    ("ssm_z", 256), ("ssm_xbc", 768), ("ssm_dt", 4),
)
IN_OFF = {}
_o = 0
for _n, _w in IN_LAYOUT:
    IN_OFF[_n] = (_o, _w)
    _o += _w
IN_W = _o

GATE_COL = 0
DT_COL = 12
IN_OUTPUTS = (
    ("nsa_q", BF16, ("nsa_q",), 256),
    ("nsa_k2", BF16, ("nsa_k_slc", "nsa_k_win"), 128),
    ("nsa_cmp", F32, ("nsa_k_cmp", "nsa_v_cmp"), 128),
    ("misc", F32, ("nsa_gate", "ssm_dt"), 128),
    ("z_all", F32, ("nsa_z", "diff_z", "ret_z", "ssm_z"), 1024),
    ("diff_qk", BF16, ("diff_q", "diff_k"), 512),
    ("ret_qkv", BF16, ("ret_q", "ret_k", "ret_v"), 768),
    ("xbc", F32, ("ssm_xbc",), 768),
)
IN_T_SRC = ("diff_v", "nsa_v_slc", "nsa_v_win")
IN_T_ROWS = 384
DIFF_VT_BLK = 0
NSA_VT_BLK = 2
IN_SEGS = []
_o = 0
for _n, _dt, _src, _w in IN_OUTPUTS:
    IN_SEGS.append((_o, _o + _w))
    _o += _w
IN_WP = _o

VMEM_LIMIT = 56 * 1024 * 1024


def _alibi_slopes():
    return [2.0 ** (-8.0 * (i + 1) / N_ALIBI_HEADS) for i in range(N_ALIBI_HEADS)]


NSA_SLOPES = _alibi_slopes()[0::2]
DIFF_SLOPES = _alibi_slopes()[1::2]


def _silu(x):
    return x * jax.nn.sigmoid(x)


def _dot(a, b):
    return jnp.dot(a, b, preferred_element_type=F32)


def _dot_nt(a, b):
    return lax.dot_general(a, b, (((1,), (1,)), ((), ())), preferred_element_type=F32)


def _dot_tn(a, b):
    return lax.dot_general(a, b, (((0,), (0,)), ((), ())), preferred_element_type=F32)


def _params(sem):
    return pltpu.CompilerParams(dimension_semantics=sem, vmem_limit_bytes=VMEM_LIMIT)


def _relayout_w_in(w_in):
    cols = []
    for _n, _dt, src, width in IN_OUTPUTS:
        used = 0
        for s in src:
            off, w = IN_OFF[s]
            cols.append(w_in[:, :, off:off + w])
            used += w
        if used < width:
            cols.append(jnp.zeros(w_in.shape[:2] + (width - used,), w_in.dtype))
    return jnp.concatenate(cols, axis=-1).astype(BF16)


def _relayout_w_in_t(w_in):
    cols = [w_in[:, :, IN_OFF[s][0]:IN_OFF[s][0] + IN_OFF[s][1]] for s in IN_T_SRC]
    return jnp.swapaxes(jnp.concatenate(cols, axis=-1), 1, 2).astype(BF16)


def _in_proj_kernel(x_ref, nw_ref, w_ref, wt_ref, *out_refs):
    x = x_ref[...]
    ms = jnp.mean(x * x, axis=-1, keepdims=True)
    h = (x * lax.rsqrt(ms + EPS) * nw_ref[...]).astype(BF16)
    for ref, (a, b) in zip(out_refs[:-1], IN_SEGS):
        ref[...] = _dot(h, w_ref[:, a:b]).astype(ref.dtype)
    out_refs[-1][...] = _dot_nt(wt_ref[...], h).astype(BF16)


def _in_proj(x2d, norm_w, w_r, w_t, tm):
    m = x2d.shape[0]
    out_shape = [jax.ShapeDtypeStruct((m, b - a), dt) for (_n, dt, _s, _w), (a, b) in zip(IN_OUTPUTS, IN_SEGS)]
    out_specs = [pl.BlockSpec((tm, b - a), lambda i: (i, 0)) for (a, b) in IN_SEGS]
    out_shape.append(jax.ShapeDtypeStruct((IN_T_ROWS, m), BF16))
    out_specs.append(pl.BlockSpec((IN_T_ROWS, tm), lambda i: (0, i)))
    return pl.pallas_call(
        _in_proj_kernel,
        grid=(m // tm,),
        in_specs=[pl.BlockSpec((tm, D_MODEL), lambda i: (i, 0)),
                  pl.BlockSpec((1, D_MODEL), lambda i: (0, 0)),
                  pl.BlockSpec((D_MODEL, IN_WP), lambda i: (0, 0)),
                  pl.BlockSpec((IN_T_ROWS, D_MODEL), lambda i: (0, 0))],
        out_specs=out_specs,
        out_shape=out_shape,
        compiler_params=_params(("parallel",)),
        name="in_proj",
    )(x2d, norm_w.reshape(1, D_MODEL), w_r, w_t)


def _out_proj_kernel(y0_ref, y1_ref, y2_ref, y3_ref, w_ref, x_ref, fw_ref, o_ref, *, final):
    acc = x_ref[...]
    for g, y_ref in enumerate((y0_ref, y1_ref, y2_ref, y3_ref)):
        acc = acc + _dot(y_ref[...], w_ref[g * GROUP_W:(g + 1) * GROUP_W, :])
    if final:
        ms = jnp.mean(acc * acc, axis=-1, keepdims=True)
        acc = acc * lax.rsqrt(ms + EPS) * fw_ref[...]
    o_ref[...] = acc


def _out_proj(ys, w_out_b, x2d, final_w, tm, final):
    m = x2d.shape[0]
    yspec = pl.BlockSpec((tm, GROUP_W), lambda i: (i, 0))
    return pl.pallas_call(
        functools.partial(_out_proj_kernel, final=final),
        grid=(m // tm,),
        in_specs=[yspec, yspec, yspec, yspec,
                  pl.BlockSpec((D_MODEL, D_MODEL), lambda i: (0, 0)),
                  pl.BlockSpec((tm, D_MODEL), lambda i: (i, 0)),
                  pl.BlockSpec((1, D_MODEL), lambda i: (0, 0))],
        out_specs=pl.BlockSpec((tm, D_MODEL), lambda i: (i, 0)),
        out_shape=jax.ShapeDtypeStruct((m, D_MODEL), F32),
        compiler_params=_params(("parallel",)),
        name="out_proj",
    )(*ys, w_out_b, x2d, final_w.reshape(1, D_MODEL))


def _flash_update_t(s, c1, shift, vt_ext, m_ref, acc_ref, idx):
    m_old = m_ref[idx]
    m_new = jnp.maximum(m_old, c1 * jnp.max(s(), axis=0, keepdims=True) + shift)
    alpha = jnp.exp2(m_old - m_new)
    p = jnp.exp2(c1 * s() - (m_new - shift))
    acc_ref[idx] = alpha * acc_ref[idx] + _dot(vt_ext, p.astype(BF16))
    m_ref[idx] = m_new


def _bf16_pieces(x, n=3):
    out = []
    for _ in range(n):
        p = float(np.asarray(x, np.float32).astype(BF16).astype(np.float32))
        out.append(p)
        x = x - p
    return out


def _key_pos_features(tk):
    lane = lax.broadcasted_iota(jnp.int32, (tk, LANES), 1)
    row = lax.broadcasted_iota(jnp.int32, (tk, LANES), 0)
    return jnp.where(lane < ALIBI_ROWS, row, 0).astype(F32).astype(BF16)


def _alibi_rows(beta, tq):
    row = lax.broadcasted_iota(jnp.int32, (LANES, tq), 0)
    out = jnp.zeros((LANES, tq), F32)
    for r, piece in enumerate(_bf16_pieces(beta, ALIBI_ROWS)):
        out = jnp.where(row == r, piece, out)
    return out.astype(BF16)


def _normalized(acc):
    return acc[0:HEAD_DIM] / jnp.maximum(acc[HEAD_DIM:HEAD_DIM + 1], 1e-30)


def _rel_t(tk, tq):
    return lax.broadcasted_iota(jnp.int32, (tk, tq), 1) - lax.broadcasted_iota(jnp.int32, (tk, tq), 0)


def _diff_kernel(lam_ref, sw_ref, q_ref, k_ref, vt_ref, z_ref, o_ref, m_ref, acc_ref, wq_ref, s_ref,
                 *, tq, tk, lam_init):
    qi = pl.program_id(1)
    kpq = tq // tk
    scale = DIFF_QK_DIM ** -0.5
    c1 = scale * LOG2E
    m_ref[...] = jnp.full(m_ref.shape, NEG_INF, F32)
    acc_ref[...] = jnp.zeros(acc_ref.shape, F32)
    rel = _rel_t(tk, tq)

    qt = q_ref[...].astype(F32).T.astype(BF16)
    row = lax.broadcasted_iota(jnp.int32, (LANES, tq), 0)
    for j in range(2 * HEADS):
        g, r0 = divmod(j * DIFF_QK_DIM, LANES)
        qg = qt[g * LANES:(g + 1) * LANES]
        wq_ref[j, 0:LANES, :] = jnp.where((row >= r0) & (row < r0 + DIFF_QK_DIM), qg, jnp.zeros_like(qg))
        wq_ref[j, LANES:2 * LANES, :] = _alibi_rows(DIFF_SLOPES[j // 2] / scale, tq)

    ones = jnp.ones((BF16_ROWS, tk), BF16)
    kpos = _key_pos_features(tk)

    def tile(ki, mask_off):
        start = pl.multiple_of(ki * tk, tk)
        k = k_ref[pl.ds(start, tk), :]
        vt = vt_ref[:, pl.ds(start, tk)]
        key0 = (ki * tk).astype(F32)

        def scores(j):
            g = (j * DIFF_QK_DIM) // LANES
            lhs = jnp.concatenate([k[:, g * LANES:(g + 1) * LANES], kpos], axis=1)
            s_ref[j] = _dot(lhs, wq_ref[j])

        def staged(j):
            if mask_off is None:
                return lambda: s_ref[j]
            return lambda: jnp.where(rel >= mask_off, s_ref[j], NEG_INF)

        for j in range(QK_LOOKAHEAD):
            scores(j)
        for j in range(2 * HEADS):
            h = j // 2
            if j + QK_LOOKAHEAD < 2 * HEADS:
                scores(j + QK_LOOKAHEAD)
            vt_ext = jnp.concatenate([vt[h * HEAD_DIM:(h + 1) * HEAD_DIM], ones], axis=0)
            _flash_update_t(staged(j), c1, (DIFF_SLOPES[h] * LOG2E) * key0, vt_ext, m_ref, acc_ref, j)

    def body(ki, carry):
        tile(ki, None)
        return carry

    lax.fori_loop(0, qi * kpq, body, 0)
    for d in range(kpq):
        tile(qi * kpq + d, d * tk)

    lp = lam_ref[...]
    lam = (jnp.exp(jnp.sum(lp[0:1] * lp[1:2], axis=-1, keepdims=True))
           - jnp.exp(jnp.sum(lp[2:3] * lp[3:4], axis=-1, keepdims=True)) + lam_init)
    sw = sw_ref[...]
    o_t = jnp.concatenate([_normalized(acc_ref[2 * h]) - lam * _normalized(acc_ref[2 * h + 1])
                           for h in range(HEADS)], axis=0)
    o = o_t.T
    outs = []
    for h in range(HEADS):
        oh = o[:, h * HEAD_DIM:(h + 1) * HEAD_DIM]
        ms = jnp.mean(oh * oh, axis=-1, keepdims=True)
        outs.append(oh * lax.rsqrt(ms + EPS) * sw * (1.0 - lam_init))
    y = jnp.concatenate(outs, axis=-1) * _silu(z_ref[...])
    o_ref[...] = y.astype(o_ref.dtype)


def _diff_attention(qk, vt_all, z_all, lam_p, subln_w, batch, seq, layer_idx, tq, tk):
    nq = seq // tq
    lam_init = 0.8 - 0.6 * math.exp(-0.3 * layer_idx)
    return pl.pallas_call(
        functools.partial(_diff_kernel, tq=tq, tk=tk, lam_init=lam_init),
        grid=(batch, nq),
        in_specs=[pl.BlockSpec((4, DIFF_QK_DIM), lambda b, i: (0, 0)),
                  pl.BlockSpec((1, HEAD_DIM), lambda b, i: (0, 0)),
                  pl.BlockSpec((tq, GROUP_W), lambda b, i: (b * nq + i, 0)),
                  pl.BlockSpec((seq, GROUP_W), lambda b, i: (b, 1)),
                  pl.BlockSpec((GROUP_W, seq), lambda b, i: (DIFF_VT_BLK, b)),
                  pl.BlockSpec((tq, GROUP_W), lambda b, i: (b * nq + i, 1))],
        out_specs=pl.BlockSpec((tq, GROUP_W), lambda b, i: (b * nq + i, 0)),
        out_shape=jax.ShapeDtypeStruct((batch * seq, GROUP_W), BF16),
        scratch_shapes=[pltpu.VMEM((2 * HEADS, 1, tq), F32),
                        pltpu.VMEM((2 * HEADS, ACC_ROWS, tq), F32),
                        pltpu.VMEM((2 * HEADS, 2 * LANES, tq), BF16),
                        pltpu.VMEM((2 * HEADS, tk, tq), F32)],
        compiler_params=_params(("parallel", "parallel")),
        name="diff_attention",
    )(lam_p, subln_w.reshape(1, HEAD_DIM), qk, qk, vt_all, z_all)


def _nsa_compress_kernel(gk_ref, gv_ref, pek_ref, pev_ref, wk1_ref, wk2_ref, wv1_ref, wv2t_ref, kc_ref, vct_ref):
    half = CMP_STRIDE * HEAD_DIM

    def hidden(g_ref, pe_ref, w1_ref):
        g = g_ref[0]
        top = (g + pe_ref[0:1, :]).astype(BF16)
        bot = (g + pe_ref[1:2, :]).astype(BF16)
        a = _dot(top, w1_ref[0:half, :])
        b = _dot(bot, w1_ref[half:2 * half, :])
        nrow = b.shape[0]
        b = pltpu.roll(b, nrow - 1, 0)
        return _silu(a + b).astype(BF16)

    hk = hidden(gk_ref, pek_ref, wk1_ref)
    kc_ref[0] = _dot(hk, wk2_ref[...]).astype(kc_ref.dtype)
    hv = hidden(gv_ref, pev_ref, wv1_ref)
    vct_ref[0] = _dot_nt(wv2t_ref[...], hv).astype(vct_ref.dtype)


def _nsa_compress(cmp2d, pe_k, pe_v, w_ck1, w_ck2, w_cv1, w_cv2, batch, seq):
    ng = seq // CMP_STRIDE
    half = CMP_STRIDE * HEAD_DIM
    gk = cmp2d[:, :HEAD_DIM].reshape(batch, ng, half)
    gv = cmp2d[:, HEAD_DIM:].reshape(batch, ng, half)
    gspec = pl.BlockSpec((1, ng, half), lambda b: (b, 0, 0))

    def full(shape):
        return pl.BlockSpec(shape, lambda b: (0,) * len(shape))

    return pl.pallas_call(
        _nsa_compress_kernel,
        grid=(batch,),
        in_specs=[gspec, gspec, full((2, half)), full((2, half)),
                  full((2 * half, CMP_HIDDEN)), full((CMP_HIDDEN, HEAD_DIM)),
                  full((2 * half, CMP_HIDDEN)), full((HEAD_DIM, CMP_HIDDEN))],
        out_specs=[pl.BlockSpec((1, ng, HEAD_DIM), lambda b: (b, 0, 0)),
                   pl.BlockSpec((1, HEAD_DIM, ng), lambda b: (b, 0, 0))],
        out_shape=[jax.ShapeDtypeStruct((batch, ng, HEAD_DIM), BF16),
                   jax.ShapeDtypeStruct((batch, HEAD_DIM, ng), BF16)],
        compiler_params=_params(("parallel",)),
        name="nsa_compress",
    )(gk, gv, pe_k.reshape(2, half), pe_v.reshape(2, half),
      w_ck1.astype(BF16), w_ck2.astype(BF16), w_cv1.astype(BF16), w_cv2.T.astype(BF16))


def _nsa_kernel(ovl_ref, q_ref, k_ref, vt_ref, kc_ref, vct_ref, misc_ref, z_ref, o_ref,
                m_ref, acc_ref, wq_ref, s_ref, *, tq, tk, seq):
    qi = pl.program_id(1)
    kpq = tq // tk
    scale = HEAD_DIM ** -0.5
    c1 = scale * LOG2E
    ng = seq // CMP_STRIDE
    ns = seq // SLC_BLOCK
    top = min(SLC_TOPK, ns)
    m_ref[...] = jnp.full(m_ref.shape, NEG_INF, F32)
    acc_ref[...] = jnp.zeros(acc_ref.shape, F32)
    rel = _rel_t(tk, tq)

    qt = q_ref[...].astype(F32).T.astype(BF16)
    zeros_q = jnp.zeros((HEAD_DIM, tq), BF16)
    for h in range(HEADS):
        qh = qt[h * HEAD_DIM:(h + 1) * HEAD_DIM]
        alibi = _alibi_rows(NSA_SLOPES[h] / scale, tq)
        wq_ref[h, 0:LANES, :] = jnp.concatenate([qh, zeros_q], axis=0)
        wq_ref[h, LANES:2 * LANES, :] = alibi
        wq_ref[HEADS + h, 0:LANES, :] = jnp.concatenate([zeros_q, qh], axis=0)
        wq_ref[HEADS + h, LANES:2 * LANES, :] = alibi

    t_lane = qi * tq + lax.broadcasted_iota(jnp.int32, (1, tq), 1)
    t_lane_f = t_lane.astype(F32)
    n_sub = lax.broadcasted_iota(jnp.int32, (ng, 1), 0)
    c_center = (n_sub * CMP_STRIDE).astype(F32) + (CMP_LEN - 1) / 2.0
    c_valid = (n_sub * CMP_STRIDE + (CMP_LEN - 1)) <= t_lane
    dist_c = t_lane_f - c_center
    kc = kc_ref[0]
    vct = vct_ref[0]
    ovl = ovl_ref[...]
    imp = jnp.zeros((ns, tq), F32)
    o_cmp = []
    for h in range(HEADS):
        s = _dot(kc, qt[h * HEAD_DIM:(h + 1) * HEAD_DIM]) * scale - NSA_SLOPES[h] * dist_c
        s = jnp.where(c_valid, s, NEG_INF)
        mx = jnp.max(s, axis=0, keepdims=True)
        mx = jnp.where(mx > NEG_INF, mx, 0.0)
        p = jnp.exp(s - mx)
        p = p / jnp.maximum(jnp.sum(p, axis=0, keepdims=True), 1e-30)
        pb = p.astype(BF16)
        o_cmp.append(_dot(vct, pb))
        imp = imp + _dot(ovl, pb)

    j_sub = lax.broadcasted_iota(jnp.int32, (ns, 1), 0)
    j_sub_f = j_sub.astype(F32)
    cur = jnp.right_shift(t_lane, SLC_SHIFT)
    forced = (j_sub == 0) | (j_sub == cur) | (j_sub == cur - 1)
    valid = (j_sub * SLC_BLOCK) <= t_lane
    score = jnp.where(forced, 1e30, jnp.where(valid, imp, -1.0))
    sel = jnp.zeros((ns, tq), F32)
    for _ in range(top):
        mx = jnp.max(score, axis=0, keepdims=True)
        idx = jnp.min(jnp.where(score == mx, j_sub_f, float(ns)), axis=0, keepdims=True)
        pick = j_sub_f == idx
        sel = jnp.where(pick, 1.0, sel)
        score = jnp.where(pick, -2.0, score)
    unsel = jnp.where(sel > 0.5, 0.0, -MASK_BIG).astype(BF16)
    for h in range(HEADS):
        wq_ref[h, LANES + BLK_LANE0:LANES + BLK_LANE0 + ns, :] = unsel

    aux_lane = lax.broadcasted_iota(jnp.int32, (tk, LANES), 1)
    aux_blk = jnp.right_shift(lax.broadcasted_iota(jnp.int32, (tk, LANES), 0), SLC_SHIFT) + BLK_LANE0
    kpos = _key_pos_features(tk)
    ones = jnp.ones((BF16_ROWS, tk), BF16)

    def slc_tile(ki, mask_off):
        start = pl.multiple_of(ki * tk, tk)
        k = k_ref[pl.ds(start, tk), :]
        vt = vt_ref[0:HEAD_DIM, pl.ds(start, tk)]
        vt_ext = jnp.concatenate([vt, ones], axis=0)
        onehot = aux_lane == aux_blk + ki * (tk // SLC_BLOCK)
        aux = jnp.where(onehot, jnp.ones_like(kpos), kpos)
        lhs = jnp.concatenate([k, aux], axis=1)
        key0 = (ki * tk).astype(F32)
        for h in range(HEADS):
            s_ref[h] = _dot(lhs, wq_ref[h])

        def staged(h):
            if mask_off is None:
                return lambda: s_ref[h]
            return lambda: jnp.where(rel >= mask_off, s_ref[h], NEG_INF)

        for h in range(HEADS):
            _flash_update_t(staged(h), c1, (NSA_SLOPES[h] * LOG2E) * key0, vt_ext, m_ref, acc_ref, h)

    def win_tile(ki):
        start = pl.multiple_of(ki * tk, tk)
        k = k_ref[pl.ds(start, tk), :]
        vt = vt_ref[HEAD_DIM:2 * HEAD_DIM, pl.ds(start, tk)]
        vt_ext = jnp.concatenate([vt, ones], axis=0)
        dist = rel + (qi * tq - ki * tk)
        keep = (dist >= 0) & (dist < WINDOW)
        lhs = jnp.concatenate([k, kpos], axis=1)
        key0 = (ki * tk).astype(F32)
        for h in range(HEADS):
            s_ref[HEADS + h] = _dot(lhs, wq_ref[HEADS + h])
        for h in range(HEADS):
            _flash_update_t(lambda h=h: jnp.where(keep, s_ref[HEADS + h], NEG_INF), c1,
                            (NSA_SLOPES[h] * LOG2E) * key0, vt_ext, m_ref, acc_ref, HEADS + h)

    def body(ki, carry):
        slc_tile(ki, None)
        return carry

    lax.fori_loop(0, qi * kpq, body, 0)
    for d in range(kpq):
        slc_tile(qi * kpq + d, d * tk)

    for d in range(kpq):
        win_tile(qi * kpq + d)
    for back in range(1, (WINDOW + tk - 1) // tk + 1):
        @pl.when(qi * kpq >= back)
        def _():
            win_tile(qi * kpq - back)

    g_t = jax.nn.sigmoid(misc_ref[...]).T
    outs = []
    for h in range(HEADS):
        r0 = GATE_COL + 3 * h
        outs.append(g_t[r0:r0 + 1] * o_cmp[h] + g_t[r0 + 1:r0 + 2] * _normalized(acc_ref[h])
                    + g_t[r0 + 2:r0 + 3] * _normalized(acc_ref[HEADS + h]))
    y = jnp.concatenate(outs, axis=0).T * _silu(z_ref[...])
    o_ref[...] = y.astype(o_ref.dtype)


def _overlap_t(seq):
    nc = (seq - CMP_LEN) // CMP_STRIDE + 1
    ng = seq // CMP_STRIDE
    ns = seq // SLC_BLOCK
    c_start = np.arange(ng) * CMP_STRIDE
    c_end = c_start + CMP_LEN - 1
    s_start = np.arange(ns) * SLC_BLOCK
    s_end = s_start + SLC_BLOCK - 1
    ov = (c_start[None, :] <= s_end[:, None]) & (c_end[None, :] >= s_start[:, None]) & (np.arange(ng)[None, :] < nc)
    return jnp.asarray(ov.astype(np.float32), dtype=BF16)


def _nsa_attention(q, k2, vt_all, kc, vct, misc, z_all, batch, seq, tq, tk):
    nq = seq // tq
    ng = seq // CMP_STRIDE
    ns = seq // SLC_BLOCK
    assert tq - tk < WINDOW and tk % SLC_BLOCK == 0 and ns <= LANES - BLK_LANE0
    return pl.pallas_call(
        functools.partial(_nsa_kernel, tq=tq, tk=tk, seq=seq),
        grid=(batch, nq),
        in_specs=[pl.BlockSpec((ns, ng), lambda b, i: (0, 0)),
                  pl.BlockSpec((tq, GROUP_W), lambda b, i: (b * nq + i, 0)),
                  pl.BlockSpec((seq, LANES), lambda b, i: (b, 0)),
                  pl.BlockSpec((2 * HEAD_DIM, seq), lambda b, i: (NSA_VT_BLK, b)),
                  pl.BlockSpec((1, ng, HEAD_DIM), lambda b, i: (b, 0, 0)),
                  pl.BlockSpec((1, HEAD_DIM, ng), lambda b, i: (b, 0, 0)),
                  pl.BlockSpec((tq, LANES), lambda b, i: (b * nq + i, 0)),
                  pl.BlockSpec((tq, GROUP_W), lambda b, i: (b * nq + i, 0))],
        out_specs=pl.BlockSpec((tq, GROUP_W), lambda b, i: (b * nq + i, 0)),
        out_shape=jax.ShapeDtypeStruct((batch * seq, GROUP_W), BF16),
        scratch_shapes=[pltpu.VMEM((2 * HEADS, 1, tq), F32),
                        pltpu.VMEM((2 * HEADS, ACC_ROWS, tq), F32),
                        pltpu.VMEM((2 * HEADS, 2 * LANES, tq), BF16),
                        pltpu.VMEM((2 * HEADS, tk, tq), F32)],
        compiler_params=_params(("parallel", "parallel")),
        name="nsa_attention",
    )(_overlap_t(seq), q, k2, vt_all, kc, vct, misc, z_all)


def _ret_tables():
    c = RET_CHUNK
    h = np.arange(HEADS, dtype=np.float32)
    log_g = jnp.log(1.0 - 2.0 ** (-5.0 - jnp.asarray(h)))
    pos = jnp.arange(c, dtype=F32)
    rel = pos[:, None] - pos[None, :]
    decay = jnp.where(rel >= 0, jnp.exp(log_g[:, None, None] * jnp.maximum(rel, 0.0)), 0.0)
    xi = jnp.exp(log_g[:, None] * (pos + 1.0))
    zeta = jnp.exp(log_g[:, None] * (c - 1.0 - pos))
    chunk_decay = jnp.exp(log_g * c)
    xi_tab = jnp.repeat(xi.T, HEAD_DIM, axis=1)
    zeta_tab = jnp.repeat(zeta.T, HEAD_DIM, axis=1)
    cd_tab = jnp.repeat(chunk_decay, HEAD_DIM)[None, :]
    return decay, xi_tab, zeta_tab, cd_tab


def _ret_kernel(decay_ref, xi_ref, zeta_ref, cd_ref, gn_ref, q_ref, k_ref, v_ref, z_ref, o_ref, st_ref):
    n = pl.program_id(1)

    @pl.when(n == 0)
    def _():
        st_ref[...] = jnp.zeros(st_ref.shape, F32)

    q = (q_ref[...].astype(F32) * (HEAD_DIM ** -0.5)).astype(BF16)
    k = k_ref[...]
    v = v_ref[...]
    kz_t = (k.astype(F32) * zeta_ref[...]).T.astype(BF16)
    xi = xi_ref[...]
    cd = cd_ref[...]
    outs = []
    for h in range(HEADS):
        sl = slice(h * HEAD_DIM, (h + 1) * HEAD_DIM)
        qh, kh, vh = q[:, sl], k[:, sl], v[:, sl]
        prev = st_ref[h]
        inner = (_dot_nt(qh, kh) * decay_ref[h]).astype(BF16)
        o = _dot(inner, vh) + _dot(qh, prev.astype(BF16)) * xi[:, sl]
        st_ref[h] = prev * cd[:, sl] + _dot(kz_t[sl, :], vh)
        mu = jnp.mean(o, axis=-1, keepdims=True)
        d = o - mu
        var = jnp.mean(d * d, axis=-1, keepdims=True)
        outs.append(d * lax.rsqrt(var + EPS))
    y = jnp.concatenate(outs, axis=-1) * gn_ref[...] * _silu(z_ref[...])
    o_ref[...] = y.astype(o_ref.dtype)


def _retention(qkv, z_all, gn_w, batch, seq):
    c = RET_CHUNK
    nch = seq // c
    decay, xi_tab, zeta_tab, cd_tab = _ret_tables()

    def full(shape):
        return pl.BlockSpec(shape, lambda b, n: (0,) * len(shape))

    def blk(col):
        return pl.BlockSpec((c, GROUP_W), lambda b, n: (b * nch + n, col))

    return pl.pallas_call(
        _ret_kernel,
        grid=(batch, nch),
        in_specs=[full((HEADS, c, c)), full((c, GROUP_W)), full((c, GROUP_W)), full((1, GROUP_W)),
                  full((1, GROUP_W)), blk(0), blk(1), blk(2), blk(2)],
        out_specs=blk(0),
        out_shape=jax.ShapeDtypeStruct((batch * seq, GROUP_W), BF16),
        scratch_shapes=[pltpu.VMEM((HEADS, HEAD_DIM, HEAD_DIM), F32)],
        compiler_params=_params(("parallel", "arbitrary")),
        name="retention",
    )(decay, xi_tab, zeta_tab, cd_tab, gn_w.reshape(1, GROUP_W), qkv, qkv, qkv, z_all)


def _ssd_kernel(cw_ref, cb_ref, dtb_ref, a_ref, dsk_ref, nw_ref, xbc_ref, misc_ref, z_ref, o_ref,
                ext_ref, st_ref):
    n = pl.program_id(1)
    L = SSM_CHUNK
    hi = lax.Precision.HIGHEST

    @pl.when(n == 0)
    def _():
        st_ref[...] = jnp.zeros(st_ref.shape, F32)
        ext_ref[0:8, :] = jnp.zeros((8, CONV_CH), F32)

    raw = xbc_ref[...]
    ext_ref[8:8 + L, :] = raw
    conv = cb_ref[...] + raw * cw_ref[CONV_W - 1:CONV_W, :]
    for w in range(CONV_W - 1):
        shift = CONV_W - 1 - w
        conv = conv + ext_ref[8 - shift:8 - shift + L, :] * cw_ref[w:w + 1, :]
    ext_ref[0:8, :] = raw[L - 8:L, :]
    xc = _silu(conv)
    x = xc[:, 0:GROUP_W]
    bm = xc[:, GROUP_W:GROUP_W + 2 * SSM_STATE].astype(BF16)
    cm = xc[:, GROUP_W + 2 * SSM_STATE:].astype(BF16)

    dt_full = jax.nn.softplus(misc_ref[...] + dtb_ref[...])
    da = dt_full * a_ref[...]
    row = lax.broadcasted_iota(jnp.int32, (L, L), 0)
    col = lax.broadcasted_iota(jnp.int32, (L, L), 1)
    tril = jnp.where(row >= col, 1.0, 0.0).astype(F32)
    cs_col = jnp.dot(tril, da, precision=hi, preferred_element_type=F32)
    cs_row = lax.dot_general(da, tril, (((0,), (1,)), ((), ())), precision=hi,
                             preferred_element_type=F32)
    causal = row >= col
    dsk = dsk_ref[...]

    outs = []
    for h in range(HEADS):
        g = h // 2
        c0 = DT_COL + h
        sl = slice(h * HEAD_DIM, (h + 1) * HEAD_DIM)
        gs = slice(g * SSM_STATE, (g + 1) * SSM_STATE)
        cs_c = cs_col[:, c0:c0 + 1]
        cs_r = cs_row[c0:c0 + 1, :]
        cs_last = cs_col[L - 1:L, c0:c0 + 1]
        xh = x[:, sl]
        xdt = xh * dt_full[:, c0:c0 + 1]
        seg = jnp.exp(jnp.where(causal, cs_c - cs_r, NEG_INF))
        cb = _dot_nt(cm[:, gs], bm[:, gs])
        y = _dot((cb * seg).astype(BF16), xdt.astype(BF16))
        prev = st_ref[h]
        y = y + _dot(cm[:, gs], prev.astype(BF16)) * jnp.exp(cs_c)
        y = y + dsk[:, sl] * xh
        dec = jnp.exp(cs_last - cs_c)
        st_ref[h] = prev * jnp.exp(cs_last) + _dot_tn(bm[:, gs], (xdt * dec).astype(BF16))
        outs.append(y)
    y = jnp.concatenate(outs, axis=-1) * _silu(z_ref[...])
    ms = jnp.mean(y * y, axis=-1, keepdims=True)
    o_ref[...] = (y * lax.rsqrt(ms + EPS) * nw_ref[...]).astype(o_ref.dtype)


def _ssd(xbc, misc, z_all, conv_w, conv_b, dt_bias, a_log, d_skip, norm_w, batch, seq):
    L = SSM_CHUNK
    nch = seq // L
    dtb = jnp.zeros((1, 128), F32).at[0, DT_COL:DT_COL + HEADS].set(dt_bias)
    a_full = jnp.zeros((1, 128), F32).at[0, DT_COL:DT_COL + HEADS].set(-jnp.exp(a_log))
    dsk = jnp.repeat(d_skip, HEAD_DIM)[None, :]

    def full(shape):
        return pl.BlockSpec(shape, lambda b, n: (0,) * len(shape))

    return pl.pallas_call(
        _ssd_kernel,
        grid=(batch, nch),
        in_specs=[full((CONV_W, CONV_CH)), full((1, CONV_CH)), full((1, 128)), full((1, 128)),
                  full((1, GROUP_W)), full((1, GROUP_W)),
                  pl.BlockSpec((L, CONV_CH), lambda b, n: (b * nch + n, 0)),
                  pl.BlockSpec((L, 128), lambda b, n: (b * nch + n, 0)),
                  pl.BlockSpec((L, GROUP_W), lambda b, n: (b * nch + n, 3))],
        out_specs=pl.BlockSpec((L, GROUP_W), lambda b, n: (b * nch + n, 0)),
        out_shape=jax.ShapeDtypeStruct((batch * seq, GROUP_W), BF16),
        scratch_shapes=[pltpu.VMEM((8 + L, CONV_CH), F32),
                        pltpu.VMEM((HEADS, SSM_STATE, HEAD_DIM), F32)],
        compiler_params=_params(("parallel", "arbitrary")),
        name="ssd",
    )(conv_w, conv_b.reshape(1, CONV_CH), dtb, a_full, dsk, norm_w.reshape(1, GROUP_W), xbc, misc, z_all)


def _pick_tile(n, pref):
    t = pref
    while n % t:
        t //= 2
    return t


def kernel(x, norm_w, w_in, w_out, nsa_pe_k, nsa_pe_v, nsa_w_ck1, nsa_w_ck2, nsa_w_cv1, nsa_w_cv2,
           diff_lam_q1, diff_lam_k1, diff_lam_q2, diff_lam_k2, diff_subln_w, ret_gn_w,
           ssm_conv_w, ssm_conv_b, ssm_dt_bias, ssm_A_log, ssm_D, ssm_norm_w, final_norm_w):
    batch, seq, _ = x.shape
    depth = w_in.shape[0]
    m = batch * seq
    tm = _pick_tile(m, 512)
    tq = _pick_tile(seq, 512)
    tk = _pick_tile(seq, 256)
    w_r = _relayout_w_in(w_in)
    w_t = _relayout_w_in_t(w_in)
    w_out_b = w_out.astype(BF16)
    x2d = x.reshape(m, D_MODEL)
    for i in range(depth):
        nsa_q, nsa_k2, nsa_cmp, misc, z_all, diff_qk, ret_qkv, xbc, vt_all = _in_proj(
            x2d, norm_w[i], w_r[i], w_t[i], tm)
        kc, vct = _nsa_compress(nsa_cmp, nsa_pe_k[i], nsa_pe_v[i], nsa_w_ck1[i], nsa_w_ck2[i],
                                nsa_w_cv1[i], nsa_w_cv2[i], batch, seq)
        y_nsa = _nsa_attention(nsa_q, nsa_k2, vt_all, kc, vct, misc, z_all, batch, seq, tq, tk)
        lam_p = jnp.stack([diff_lam_q1[i], diff_lam_k1[i], diff_lam_q2[i], diff_lam_k2[i]])
        y_diff = _diff_attention(diff_qk, vt_all, z_all, lam_p, diff_subln_w[i], batch, seq, i, tq, tk)
        y_ret = _retention(ret_qkv, z_all, ret_gn_w[i], batch, seq)
        y_ssm = _ssd(xbc, misc, z_all, ssm_conv_w[i], ssm_conv_b[i], ssm_dt_bias[i], ssm_A_log[i],
                     ssm_D[i], ssm_norm_w[i], batch, seq)
        x2d = _out_proj((y_nsa, y_diff, y_ret, y_ssm), w_out_b[i], x2d, final_norm_w, tm, i == depth - 1)
    return x2d.reshape(batch, seq, D_MODEL)
```

```python
import functools
import math

import numpy as np
import jax
import jax.numpy as jnp
from jax import lax
from jax.experimental import pallas as pl
from jax.experimental.pallas import tpu as pltpu

F32 = jnp.float32
BF16 = jnp.bfloat16
NEG_INF = float("-inf")
LOG2E = 1.4426950408889634

D_MODEL = 1024
DEPTH = 4
GROUP_W = 256
HEADS = 4
HEAD_DIM = 64
EPS = 1e-6
CMP_LEN = 32
CMP_STRIDE = 16
CMP_HIDDEN = 256
SLC_BLOCK = 64
SLC_SHIFT = 6
SLC_TOPK = 16
WINDOW = 512
DIFF_QK_DIM = 32
RET_CHUNK = 128
SSM_STATE = 128
SSM_CHUNK = 128
CONV_W = 4
CONV_CH = 768
N_ALIBI_HEADS = 8
LANES = 128
BF16_ROWS = 16
ACC_ROWS = HEAD_DIM + BF16_ROWS
QK_LOOKAHEAD = 4
ALIBI_ROWS = 3
MASK_BIG = 2.0 ** 100
BLK_LANE0 = 64
REC_BATCH = 2

IN_LAYOUT = (
    ("nsa_q", 256), ("nsa_k_cmp", 64), ("nsa_v_cmp", 64), ("nsa_k_slc", 64), ("nsa_v_slc", 64),
    ("nsa_k_win", 64), ("nsa_v_win", 64), ("nsa_gate", 12), ("nsa_z", 256),
    ("diff_q", 256), ("diff_k", 256), ("diff_v", 256), ("diff_z", 256),
    ("ret_q", 256), ("ret_k", 256), ("ret_v", 256), ("ret_z", 256),
    ("ssm_z", 256), ("ssm_xbc", 768), ("ssm_dt", 4),
)
IN_OFF = {}
_o = 0
for _n, _w in IN_LAYOUT:
    IN_OFF[_n] = (_o, _w)
    _o += _w
IN_W = _o

GATE_COL = 0
DT_COL = 12
IN_OUTPUTS = (
    ("nsa_q", BF16, ("nsa_q",), 256),
    ("nsa_k2", BF16, ("nsa_k_slc", "nsa_k_win"), 128),
    ("nsa_cmp", F32, ("nsa_k_cmp", "nsa_v_cmp"), 128),
    ("misc", F32, ("nsa_gate", "ssm_dt"), 128),
    ("z_all", F32, ("nsa_z", "diff_z", "ret_z", "ssm_z"), 1024),
    ("diff_qk", BF16, ("diff_q", "diff_k"), 512),
    ("ret_qkv", BF16, ("ret_q", "ret_k", "ret_v"), 768),
    ("xbc", F32, ("ssm_xbc",), 768),
)
IN_T_SRC = ("diff_v", "nsa_v_slc", "nsa_v_win")
IN_T_ROWS = 384
DIFF_VT_BLK = 0
NSA_VT_BLK = 2
IN_SEGS = []
_o = 0
for _n, _dt, _src, _w in IN_OUTPUTS:
    IN_SEGS.append((_o, _o + _w))
    _o += _w
IN_WP = _o

VMEM_LIMIT = 56 * 1024 * 1024


def _alibi_slopes():
    return [2.0 ** (-8.0 * (i + 1) / N_ALIBI_HEADS) for i in range(N_ALIBI_HEADS)]


NSA_SLOPES = _alibi_slopes()[0::2]
DIFF_SLOPES = _alibi_slopes()[1::2]


def _silu(x):
    return x * jax.nn.sigmoid(x)


def _dot(a, b):
    return jnp.dot(a, b, preferred_element_type=F32)


def _dot_nt(a, b):
    return lax.dot_general(a, b, (((1,), (1,)), ((), ())), preferred_element_type=F32)


def _dot_tn(a, b):
    return lax.dot_general(a, b, (((0,), (0,)), ((), ())), preferred_element_type=F32)


def _params(sem):
    return pltpu.CompilerParams(dimension_semantics=sem, vmem_limit_bytes=VMEM_LIMIT)


def _relayout_w_in(w_in):
    cols = []
    for _n, _dt, src, width in IN_OUTPUTS:
        used = 0
        for s in src:
            off, w = IN_OFF[s]
            cols.append(w_in[:, :, off:off + w])
            used += w
        if used < width:
            cols.append(jnp.zeros(w_in.shape[:2] + (width - used,), w_in.dtype))
    return jnp.concatenate(cols, axis=-1).astype(BF16)


def _relayout_w_in_t(w_in):
    cols = [w_in[:, :, IN_OFF[s][0]:IN_OFF[s][0] + IN_OFF[s][1]] for s in IN_T_SRC]
    return jnp.swapaxes(jnp.concatenate(cols, axis=-1), 1, 2).astype(BF16)


def _norm_project(x, nw_ref, w_ref, wt_ref, out_refs):
    ms = jnp.mean(x * x, axis=-1, keepdims=True)
    h = (x * lax.rsqrt(ms + EPS) * nw_ref[...]).astype(BF16)
    for ref, (a, b) in zip(out_refs[:-1], IN_SEGS):
        ref[...] = _dot(h, w_ref[:, a:b]).astype(ref.dtype)
    out_refs[-1][...] = _dot_nt(wt_ref[...], h).astype(BF16)


def _in_proj_kernel(x_ref, nw_ref, w_ref, wt_ref, *out_refs):
    _norm_project(x_ref[...], nw_ref, w_ref, wt_ref, out_refs)


def _in_proj_outputs(m, tm):
    out_shape = [jax.ShapeDtypeStruct((m, b - a), dt) for (_n, dt, _s, _w), (a, b) in zip(IN_OUTPUTS, IN_SEGS)]
    out_specs = [pl.BlockSpec((tm, b - a), lambda i: (i, 0)) for (a, b) in IN_SEGS]
    out_shape.append(jax.ShapeDtypeStruct((IN_T_ROWS, m), BF16))
    out_specs.append(pl.BlockSpec((IN_T_ROWS, tm), lambda i: (0, i)))
    return out_shape, out_specs


def _in_proj(x2d, norm_w, w_r, w_t, tm):
    m = x2d.shape[0]
    out_shape, out_specs = _in_proj_outputs(m, tm)
    return pl.pallas_call(
        _in_proj_kernel,
        grid=(m // tm,),
        in_specs=[pl.BlockSpec((tm, D_MODEL), lambda i: (i, 0)),
                  pl.BlockSpec((1, D_MODEL), lambda i: (0, 0)),
                  pl.BlockSpec((D_MODEL, IN_WP), lambda i: (0, 0)),
                  pl.BlockSpec((IN_T_ROWS, D_MODEL), lambda i: (0, 0))],
        out_specs=out_specs,
        out_shape=out_shape,
        compiler_params=_params(("parallel",)),
        name="in_proj",
    )(x2d, norm_w.reshape(1, D_MODEL), w_r, w_t)


def _out_proj_kernel(y0_ref, y1_ref, y2_ref, y3_ref, w_ref, x_ref, fw_ref, o_ref):
    acc = x_ref[...]
    for g, y_ref in enumerate((y0_ref, y1_ref, y2_ref, y3_ref)):
        acc = acc + _dot(y_ref[...], w_ref[g * GROUP_W:(g + 1) * GROUP_W, :])
    ms = jnp.mean(acc * acc, axis=-1, keepdims=True)
    o_ref[...] = acc * lax.rsqrt(ms + EPS) * fw_ref[...]


def _out_proj(ys, w_out_b, x2d, final_w, tm):
    m = x2d.shape[0]
    yspec = pl.BlockSpec((tm, GROUP_W), lambda i: (i, 0))
    return pl.pallas_call(
        _out_proj_kernel,
        grid=(m // tm,),
        in_specs=[yspec, yspec, yspec, yspec,
                  pl.BlockSpec((D_MODEL, D_MODEL), lambda i: (0, 0)),
                  pl.BlockSpec((tm, D_MODEL), lambda i: (i, 0)),
                  pl.BlockSpec((1, D_MODEL), lambda i: (0, 0))],
        out_specs=pl.BlockSpec((tm, D_MODEL), lambda i: (i, 0)),
        out_shape=jax.ShapeDtypeStruct((m, D_MODEL), F32),
        compiler_params=_params(("parallel",)),
        name="out_proj",
    )(*ys, w_out_b, x2d, final_w.reshape(1, D_MODEL))


def _out_in_proj_kernel(y0_ref, y1_ref, y2_ref, y3_ref, wo_ref, x_ref, nw_ref, w_ref, wt_ref, xo_ref, *out_refs):
    acc = x_ref[...]
    for g, y_ref in enumerate((y0_ref, y1_ref, y2_ref, y3_ref)):
        acc = acc + _dot(y_ref[...], wo_ref[g * GROUP_W:(g + 1) * GROUP_W, :])
    xo_ref[...] = acc
    _norm_project(acc, nw_ref, w_ref, wt_ref, out_refs)


def _out_in_proj(ys, w_out_b, x2d, norm_w, w_r, w_t, tm):
    m = x2d.shape[0]
    yspec = pl.BlockSpec((tm, GROUP_W), lambda i: (i, 0))
    xspec = pl.BlockSpec((tm, D_MODEL), lambda i: (i, 0))
    out_shape, out_specs = _in_proj_outputs(m, tm)
    return pl.pallas_call(
        _out_in_proj_kernel,
        grid=(m // tm,),
        in_specs=[yspec, yspec, yspec, yspec,
                  pl.BlockSpec((D_MODEL, D_MODEL), lambda i: (0, 0)),
                  xspec,
                  pl.BlockSpec((1, D_MODEL), lambda i: (0, 0)),
                  pl.BlockSpec((D_MODEL, IN_WP), lambda i: (0, 0)),
                  pl.BlockSpec((IN_T_ROWS, D_MODEL), lambda i: (0, 0))],
        out_specs=[xspec] + out_specs,
        out_shape=[jax.ShapeDtypeStruct((m, D_MODEL), F32)] + out_shape,
        compiler_params=_params(("parallel",)),
        name="out_in_proj",
    )(*ys, w_out_b, x2d, norm_w.reshape(1, D_MODEL), w_r, w_t)


def _flash_update_t(s, c1, shift, vt_ext, m_ref, acc_ref, idx):
    m_old = m_ref[idx]
    m_new = jnp.maximum(m_old, c1 * jnp.max(s(), axis=0, keepdims=True) + shift)
    alpha = jnp.exp2(m_old - m_new)
    p = jnp.exp2(c1 * s() - (m_new - shift))
    acc_ref[idx] = alpha * acc_ref[idx] + _dot(vt_ext, p.astype(BF16))
    m_ref[idx] = m_new


def _bf16_pieces(x, n=3):
    out = []
    for _ in range(n):
        p = float(np.asarray(x, np.float32).astype(BF16).astype(np.float32))
        out.append(p)
        x = x - p
    return out


def _key_pos_features(tk):
    lane = lax.broadcasted_iota(jnp.int32, (tk, LANES), 1)
    row = lax.broadcasted_iota(jnp.int32, (tk, LANES), 0)
    return jnp.where(lane < ALIBI_ROWS, row, 0).astype(F32).astype(BF16)


def _alibi_rows(beta, tq, rows=LANES):
    row = lax.broadcasted_iota(jnp.int32, (rows, tq), 0)
    out = jnp.zeros((rows, tq), F32)
    for r, piece in enumerate(_bf16_pieces(beta, ALIBI_ROWS)):
        out = jnp.where(row == r, piece, out)
    return out.astype(BF16)


def _normalized(acc):
    return acc[0:HEAD_DIM] / jnp.maximum(acc[HEAD_DIM:HEAD_DIM + 1], 1e-30)


def _rel_t(tk, tq):
    return lax.broadcasted_iota(jnp.int32, (tk, tq), 1) - lax.broadcasted_iota(jnp.int32, (tk, tq), 0)


def _diff_kernel(lam_ref, sw_ref, q_ref, k_ref, vt_ref, z_ref, o_ref, m_ref, acc_ref, wq_ref, s_ref,
                 *, tq, tk, lam_init):
    qi = pl.program_id(1)
    kpq = tq // tk
    scale = DIFF_QK_DIM ** -0.5
    c1 = scale * LOG2E
    m_ref[...] = jnp.full(m_ref.shape, NEG_INF, F32)
    acc_ref[...] = jnp.zeros(acc_ref.shape, F32)
    rel = _rel_t(tk, tq)

    qt = q_ref[...].astype(F32).T.astype(BF16)
    row = lax.broadcasted_iota(jnp.int32, (LANES, tq), 0)
    for j in range(2 * HEADS):
        g, r0 = divmod(j * DIFF_QK_DIM, LANES)
        qg = qt[g * LANES:(g + 1) * LANES]
        wq_ref[j, 0:LANES, :] = jnp.where((row >= r0) & (row < r0 + DIFF_QK_DIM), qg, jnp.zeros_like(qg))
        wq_ref[j, LANES:2 * LANES, :] = _alibi_rows(DIFF_SLOPES[j // 2] / scale, tq)

    ones = jnp.ones((BF16_ROWS, tk), BF16)
    kpos = _key_pos_features(tk)

    n_maps = 2 * HEADS

    def tiles(kis, mask_offs):
        loaded = []
        for ki in kis:
            start = pl.multiple_of(ki * tk, tk)
            loaded.append((k_ref[pl.ds(start, tk), :],
                           vt_ref[:, pl.ds(start, tk)],
                           (ki * tk).astype(F32)))
        items = [(t, j) for t in range(len(kis)) for j in range(n_maps)]

        def scores(i):
            t, j = items[i]
            g = (j * DIFF_QK_DIM) // LANES
            lhs = jnp.concatenate([loaded[t][0][:, g * LANES:(g + 1) * LANES], kpos], axis=1)
            s_ref[i % n_maps] = _dot(lhs, wq_ref[j])

        def staged(i):
            mask_off = mask_offs[items[i][0]]
            if mask_off is None:
                return lambda: s_ref[i % n_maps]
            return lambda: jnp.where(rel >= mask_off, s_ref[i % n_maps], NEG_INF)

        for i in range(QK_LOOKAHEAD):
            scores(i)
        for i, (t, j) in enumerate(items):
            h = j // 2
            if i + QK_LOOKAHEAD < len(items):
                scores(i + QK_LOOKAHEAD)
            _k, vt, key0 = loaded[t]
            vt_ext = jnp.concatenate([vt[h * HEAD_DIM:(h + 1) * HEAD_DIM], ones], axis=0)
            _flash_update_t(staged(i), c1, (DIFF_SLOPES[h] * LOG2E) * key0, vt_ext, m_ref, acc_ref, j)

    def body(i, carry):
        tiles([i * kpq + d for d in range(kpq)], [None] * kpq)
        return carry

    lax.fori_loop(0, qi, body, 0)
    tiles([qi * kpq + d for d in range(kpq)], [d * tk for d in range(kpq)])

    lp = lam_ref[...]
    lam = (jnp.exp(jnp.sum(lp[0:1] * lp[1:2], axis=-1, keepdims=True))
           - jnp.exp(jnp.sum(lp[2:3] * lp[3:4], axis=-1, keepdims=True)) + lam_init)
    sw = sw_ref[...]
    o_t = jnp.concatenate([_normalized(acc_ref[2 * h]) - lam * _normalized(acc_ref[2 * h + 1])
                           for h in range(HEADS)], axis=0)
    o = o_t.T
    outs = []
    for h in range(HEADS):
        oh = o[:, h * HEAD_DIM:(h + 1) * HEAD_DIM]
        ms = jnp.mean(oh * oh, axis=-1, keepdims=True)
        outs.append(oh * lax.rsqrt(ms + EPS) * sw * (1.0 - lam_init))
    y = jnp.concatenate(outs, axis=-1) * _silu(z_ref[...])
    o_ref[...] = y.astype(o_ref.dtype)


def _diff_attention(qk, vt_all, z_all, lam_p, subln_w, batch, seq, layer_idx, tq, tk):
    nq = seq // tq
    lam_init = 0.8 - 0.6 * math.exp(-0.3 * layer_idx)
    return pl.pallas_call(
        functools.partial(_diff_kernel, tq=tq, tk=tk, lam_init=lam_init),
        grid=(batch, nq),
        in_specs=[pl.BlockSpec((4, DIFF_QK_DIM), lambda b, i: (0, 0)),
                  pl.BlockSpec((1, HEAD_DIM), lambda b, i: (0, 0)),
                  pl.BlockSpec((tq, GROUP_W), lambda b, i: (b * nq + i, 0)),
                  pl.BlockSpec((seq, GROUP_W), lambda b, i: (b, 1)),
                  pl.BlockSpec((GROUP_W, seq), lambda b, i: (DIFF_VT_BLK, b)),
                  pl.BlockSpec((tq, GROUP_W), lambda b, i: (b * nq + i, 1))],
        out_specs=pl.BlockSpec((tq, GROUP_W), lambda b, i: (b * nq + i, 0)),
        out_shape=jax.ShapeDtypeStruct((batch * seq, GROUP_W), BF16),
        scratch_shapes=[pltpu.VMEM((2 * HEADS, 1, tq), F32),
                        pltpu.VMEM((2 * HEADS, ACC_ROWS, tq), F32),
                        pltpu.VMEM((2 * HEADS, 2 * LANES, tq), BF16),
                        pltpu.VMEM((2 * HEADS, tk, tq), F32)],
        compiler_params=_params(("parallel", "parallel")),
        name="diff_attention",
    )(lam_p, subln_w.reshape(1, HEAD_DIM), qk, qk, vt_all, z_all)


def _nsa_compress_kernel(gk_ref, gv_ref, pek_ref, pev_ref, wk1_ref, wk2_ref, wv1_ref, wv2t_ref, kc_ref, vct_ref):
    half = CMP_STRIDE * HEAD_DIM

    def hidden(g_ref, pe_ref, w1_ref):
        g = g_ref[0]
        top = (g + pe_ref[0:1, :]).astype(BF16)
        bot = (g + pe_ref[1:2, :]).astype(BF16)
        a = _dot(top, w1_ref[0:half, :])
        b = _dot(bot, w1_ref[half:2 * half, :])
        nrow = b.shape[0]
        b = pltpu.roll(b, nrow - 1, 0)
        return _silu(a + b).astype(BF16)

    hk = hidden(gk_ref, pek_ref, wk1_ref)
    kc = _dot(hk, wk2_ref[...])
    lane = lax.broadcasted_iota(jnp.int32, kc.shape, 1)
    blk = lax.broadcasted_iota(jnp.int32, kc.shape, 0).astype(F32)
    kc = jnp.where((lane >= HEAD_DIM) & (lane < HEAD_DIM + ALIBI_ROWS), blk, kc)
    kc_ref[0] = kc.astype(kc_ref.dtype)
    hv = hidden(gv_ref, pev_ref, wv1_ref)
    vct_ref[0] = _dot_nt(wv2t_ref[...], hv).astype(vct_ref.dtype)


def _nsa_compress(cmp2d, pe_k, pe_v, w_ck1, w_ck2, w_cv1, w_cv2, batch, seq):
    ng = seq // CMP_STRIDE
    half = CMP_STRIDE * HEAD_DIM
    gk = cmp2d[:, :HEAD_DIM].reshape(batch, ng, half)
    gv = cmp2d[:, HEAD_DIM:].reshape(batch, ng, half)
    gspec = pl.BlockSpec((1, ng, half), lambda b: (b, 0, 0))

    def full(shape):
        return pl.BlockSpec(shape, lambda b: (0,) * len(shape))

    return pl.pallas_call(
        _nsa_compress_kernel,
        grid=(batch,),
        in_specs=[gspec, gspec, full((2, half)), full((2, half)),
                  full((2 * half, CMP_HIDDEN)), full((CMP_HIDDEN, LANES)),
                  full((2 * half, CMP_HIDDEN)), full((HEAD_DIM, CMP_HIDDEN))],
        out_specs=[pl.BlockSpec((1, ng, LANES), lambda b: (b, 0, 0)),
                   pl.BlockSpec((1, HEAD_DIM, ng), lambda b: (b, 0, 0))],
        out_shape=[jax.ShapeDtypeStruct((batch, ng, LANES), BF16),
                   jax.ShapeDtypeStruct((batch, HEAD_DIM, ng), BF16)],
        compiler_params=_params(("parallel",)),
        name="nsa_compress",
    )(gk, gv, pe_k.reshape(2, half), pe_v.reshape(2, half),
      w_ck1.astype(BF16), jnp.pad(w_ck2, ((0, 0), (0, LANES - HEAD_DIM))).astype(BF16),
      w_cv1.astype(BF16), w_cv2.T.astype(BF16))


def _nsa_kernel(ovl_ref, q_ref, k_ref, vt_ref, kc_ref, vct_ref, misc_ref, z_ref, o_ref,
                m_ref, acc_ref, wq_ref, s_ref, *, tq, tk, seq):
    qi = pl.program_id(1)
    kpq = tq // tk
    scale = HEAD_DIM ** -0.5
    c1 = scale * LOG2E
    ng = seq // CMP_STRIDE
    ns = seq // SLC_BLOCK
    top = min(SLC_TOPK, ns)
    m_ref[...] = jnp.full(m_ref.shape, NEG_INF, F32)
    acc_ref[...] = jnp.zeros(acc_ref.shape, F32)
    rel = _rel_t(tk, tq)

    qt = q_ref[...].astype(F32).T.astype(BF16)
    zeros_q = jnp.zeros((HEAD_DIM, tq), BF16)
    for h in range(HEADS):
        qh = qt[h * HEAD_DIM:(h + 1) * HEAD_DIM]
        alibi = _alibi_rows(NSA_SLOPES[h] / scale, tq)
        wq_ref[h, 0:LANES, :] = jnp.concatenate([qh, zeros_q], axis=0)
        wq_ref[h, LANES:2 * LANES, :] = alibi
        wq_ref[HEADS + h, 0:LANES, :] = jnp.concatenate([zeros_q, qh], axis=0)
        wq_ref[HEADS + h, LANES:2 * LANES, :] = alibi

    t_lane = qi * tq + lax.broadcasted_iota(jnp.int32, (1, tq), 1)
    n_sub = lax.broadcasted_iota(jnp.int32, (ng, 1), 0)
    c_valid = (n_sub * CMP_STRIDE + (CMP_LEN - 1)) <= t_lane
    kc = kc_ref[0]
    vct = vct_ref[0]
    ovl = ovl_ref[...]
    imp = jnp.zeros((ns, tq), F32)
    o_cmp = []
    for h in range(HEADS):
        rhs = jnp.concatenate([qt[h * HEAD_DIM:(h + 1) * HEAD_DIM],
                               _alibi_rows(NSA_SLOPES[h] * CMP_STRIDE / scale, tq, HEAD_DIM)], axis=0)
        s = jnp.where(c_valid, _dot(kc, rhs), NEG_INF)
        mx = jnp.max(s, axis=0, keepdims=True)
        mx = jnp.where(mx > NEG_INF, mx, 0.0)
        p = jnp.exp2(c1 * s - c1 * mx)
        p = p * (1.0 / jnp.maximum(jnp.sum(p, axis=0, keepdims=True), 1e-30))
        pb = p.astype(BF16)
        o_cmp.append(_dot(vct, pb))
        imp = imp + _dot(ovl, pb)

    j_sub = lax.broadcasted_iota(jnp.int32, (ns, 1), 0)
    j_sub_f = j_sub.astype(F32)
    cur = jnp.right_shift(t_lane, SLC_SHIFT)
    forced = (j_sub == 0) | (j_sub == cur) | (j_sub == cur - 1)
    valid = (j_sub * SLC_BLOCK) <= t_lane
    score = jnp.where(forced, 1e30, jnp.where(valid, imp, -1.0))
    sel = jnp.zeros((ns, tq), F32)
    for _ in range(top):
        mx = jnp.max(score, axis=0, keepdims=True)
        idx = jnp.min(jnp.where(score == mx, j_sub_f, float(ns)), axis=0, keepdims=True)
        pick = j_sub_f == idx
        sel = jnp.where(pick, 1.0, sel)
        score = jnp.where(pick, -2.0, score)
    unsel = jnp.where(sel > 0.5, 0.0, -MASK_BIG).astype(BF16)
    for h in range(HEADS):
        wq_ref[h, LANES + BLK_LANE0:LANES + BLK_LANE0 + ns, :] = unsel

    aux_lane = lax.broadcasted_iota(jnp.int32, (tk, LANES), 1)
    aux_blk = jnp.right_shift(lax.broadcasted_iota(jnp.int32, (tk, LANES), 0), SLC_SHIFT) + BLK_LANE0
    kpos = _key_pos_features(tk)
    ones = jnp.ones((BF16_ROWS, tk), BF16)

    n_slots = s_ref.shape[0]

    def tiles(window, kis, mask_offs=None):
        base = HEADS if window else 0
        loaded = []
        for ki in kis:
            start = pl.multiple_of(ki * tk, tk)
            k = k_ref[pl.ds(start, tk), :]
            if window:
                vt = vt_ref[HEAD_DIM:2 * HEAD_DIM, pl.ds(start, tk)]
                aux = kpos
                dist = rel + (qi * tq - ki * tk)
                keep = (dist >= 0) & (dist < WINDOW)
            else:
                vt = vt_ref[0:HEAD_DIM, pl.ds(start, tk)]
                onehot = aux_lane == aux_blk + ki * (tk // SLC_BLOCK)
                aux = jnp.where(onehot, jnp.ones_like(kpos), kpos)
                keep = None
            loaded.append((jnp.concatenate([k, aux], axis=1),
                           jnp.concatenate([vt, ones], axis=0), (ki * tk).astype(F32), keep))
        items = [(t, h) for t in range(len(kis)) for h in range(HEADS)]

        def scores(i):
            t, h = items[i]
            s_ref[i % n_slots] = _dot(loaded[t][0], wq_ref[base + h])

        def staged(i):
            t = items[i][0]
            if window:
                return lambda: jnp.where(loaded[t][3], s_ref[i % n_slots], NEG_INF)
            if mask_offs[t] is None:
                return lambda: s_ref[i % n_slots]
            return lambda: jnp.where(rel >= mask_offs[t], s_ref[i % n_slots], NEG_INF)

        for i in range(QK_LOOKAHEAD):
            scores(i)
        for i, (t, h) in enumerate(items):
            if i + QK_LOOKAHEAD < len(items):
                scores(i + QK_LOOKAHEAD)
            _lhs, vt_ext, key0, _keep = loaded[t]
            _flash_update_t(staged(i), c1, (NSA_SLOPES[h] * LOG2E) * key0, vt_ext, m_ref, acc_ref, base + h)

    def body(i, carry):
        tiles(False, [i * kpq + d for d in range(kpq)], [None] * kpq)
        return carry

    lax.fori_loop(0, qi, body, 0)
    tiles(False, [qi * kpq + d for d in range(kpq)], [d * tk for d in range(kpq)])

    tiles(True, [qi * kpq + d for d in range(kpq)])
    n_back = (WINDOW + tk - 1) // tk
    for g in range((n_back + kpq - 1) // kpq):
        backs = list(range(g * kpq + 1, min((g + 1) * kpq, n_back) + 1))

        @pl.when(qi > g)
        def _():
            tiles(True, [qi * kpq - back for back in backs])

    g_t = jax.nn.sigmoid(misc_ref[...]).T
    outs = []
    for h in range(HEADS):
        r0 = GATE_COL + 3 * h
        outs.append(g_t[r0:r0 + 1] * o_cmp[h] + g_t[r0 + 1:r0 + 2] * _normalized(acc_ref[h])
                    + g_t[r0 + 2:r0 + 3] * _normalized(acc_ref[HEADS + h]))
    y = jnp.concatenate(outs, axis=0).T * _silu(z_ref[...])
    o_ref[...] = y.astype(o_ref.dtype)


def _overlap_t(seq):
    nc = (seq - CMP_LEN) // CMP_STRIDE + 1
    ng = seq // CMP_STRIDE
    ns = seq // SLC_BLOCK
    c_start = np.arange(ng) * CMP_STRIDE
    c_end = c_start + CMP_LEN - 1
    s_start = np.arange(ns) * SLC_BLOCK
    s_end = s_start + SLC_BLOCK - 1
    ov = (c_start[None, :] <= s_end[:, None]) & (c_end[None, :] >= s_start[:, None]) & (np.arange(ng)[None, :] < nc)
    return jnp.asarray(ov.astype(np.float32), dtype=BF16)


def _nsa_attention(q, k2, vt_all, kc, vct, misc, z_all, batch, seq, tq, tk):
    nq = seq // tq
    ng = seq // CMP_STRIDE
    ns = seq // SLC_BLOCK
    assert tq - tk < WINDOW and tk % SLC_BLOCK == 0 and ns <= LANES - BLK_LANE0
    return pl.pallas_call(
        functools.partial(_nsa_kernel, tq=tq, tk=tk, seq=seq),
        grid=(batch, nq),
        in_specs=[pl.BlockSpec((ns, ng), lambda b, i: (0, 0)),
                  pl.BlockSpec((tq, GROUP_W), lambda b, i: (b * nq + i, 0)),
                  pl.BlockSpec((seq, LANES), lambda b, i: (b, 0)),
                  pl.BlockSpec((2 * HEAD_DIM, seq), lambda b, i: (NSA_VT_BLK, b)),
                  pl.BlockSpec((1, ng, LANES), lambda b, i: (b, 0, 0)),
                  pl.BlockSpec((1, HEAD_DIM, ng), lambda b, i: (b, 0, 0)),
                  pl.BlockSpec((tq, LANES), lambda b, i: (b * nq + i, 0)),
                  pl.BlockSpec((tq, GROUP_W), lambda b, i: (b * nq + i, 0))],
        out_specs=pl.BlockSpec((tq, GROUP_W), lambda b, i: (b * nq + i, 0)),
        out_shape=jax.ShapeDtypeStruct((batch * seq, GROUP_W), BF16),
        scratch_shapes=[pltpu.VMEM((2 * HEADS, 1, tq), F32),
                        pltpu.VMEM((2 * HEADS, ACC_ROWS, tq), F32),
                        pltpu.VMEM((2 * HEADS, 2 * LANES, tq), BF16),
                        pltpu.VMEM((2 * HEADS, tk, tq), F32)],
        compiler_params=_params(("parallel", "parallel")),
        name="nsa_attention",
    )(_overlap_t(seq), q, k2, vt_all, kc, vct, misc, z_all)


def _ret_tables():
    c = RET_CHUNK
    h = np.arange(HEADS, dtype=np.float32)
    log_g = jnp.log(1.0 - 2.0 ** (-5.0 - jnp.asarray(h)))
    pos = jnp.arange(c, dtype=F32)
    rel = pos[:, None] - pos[None, :]
    decay = jnp.where(rel >= 0, jnp.exp(log_g[:, None, None] * jnp.maximum(rel, 0.0)), 0.0)
    xi = jnp.exp(log_g[:, None] * (pos + 1.0))
    zeta = jnp.exp(log_g[:, None] * (c - 1.0 - pos))
    chunk_decay = jnp.exp(log_g * c)
    xi_tab = jnp.repeat(xi.T, HEAD_DIM, axis=1)
    zeta_tab = jnp.repeat(zeta.T, HEAD_DIM, axis=1)
    cd_tab = jnp.repeat(chunk_decay, HEAD_DIM)[None, :]
    return decay, xi_tab, zeta_tab, cd_tab


def _ret_kernel(decay_ref, xi_ref, zeta_ref, cd_ref, gn_ref, q_ref, k_ref, v_ref, z_ref, o_ref, st_ref):
    n = pl.program_id(1)

    @pl.when(n == 0)
    def _():
        st_ref[...] = jnp.zeros(st_ref.shape, F32)

    xi = xi_ref[...]
    cd = cd_ref[...]
    for bb in range(q_ref.shape[0]):
        q = (q_ref[bb].astype(F32) * (HEAD_DIM ** -0.5)).astype(BF16)
        k = k_ref[bb]
        v = v_ref[bb]
        kz_t = (k.astype(F32) * zeta_ref[...]).T.astype(BF16)
        outs = []
        for h in range(HEADS):
            sl = slice(h * HEAD_DIM, (h + 1) * HEAD_DIM)
            qh, kh, vh = q[:, sl], k[:, sl], v[:, sl]
            prev = st_ref[bb * HEADS + h]
            inner = (_dot_nt(qh, kh) * decay_ref[h]).astype(BF16)
            o = _dot(inner, vh) + _dot(qh, prev.astype(BF16)) * xi[:, sl]
            st_ref[bb * HEADS + h] = prev * cd[:, sl] + _dot(kz_t[sl, :], vh)
            mu = jnp.mean(o, axis=-1, keepdims=True)
            d = o - mu
            var = jnp.mean(d * d, axis=-1, keepdims=True)
            outs.append(d * lax.rsqrt(var + EPS))
        y = jnp.concatenate(outs, axis=-1) * gn_ref[...] * _silu(z_ref[bb])
        o_ref[bb] = y.astype(o_ref.dtype)


def _retention(qkv, z_all, gn_w, batch, seq, bpb):
    c = RET_CHUNK
    nch = seq // c
    decay, xi_tab, zeta_tab, cd_tab = _ret_tables()

    def full(shape):
        return pl.BlockSpec(shape, lambda b, n: (0,) * len(shape))

    def blk(col):
        return pl.BlockSpec((bpb, c, GROUP_W), lambda b, n: (b, n, col))

    out = pl.pallas_call(
        _ret_kernel,
        grid=(batch // bpb, nch),
        in_specs=[full((HEADS, c, c)), full((c, GROUP_W)), full((c, GROUP_W)), full((1, GROUP_W)),
                  full((1, GROUP_W)), blk(0), blk(1), blk(2), blk(2)],
        out_specs=blk(0),
        out_shape=jax.ShapeDtypeStruct((batch, seq, GROUP_W), BF16),
        scratch_shapes=[pltpu.VMEM((bpb * HEADS, HEAD_DIM, HEAD_DIM), F32)],
        compiler_params=_params(("parallel", "arbitrary")),
        name="retention",
    )(decay, xi_tab, zeta_tab, cd_tab, gn_w.reshape(1, GROUP_W),
      *(qkv.reshape(batch, seq, -1),) * 3, z_all.reshape(batch, seq, -1))
    return out.reshape(batch * seq, GROUP_W)


def _ssd_kernel(cw_ref, cb_ref, dtb_ref, a_ref, dsk_ref, nw_ref, xbc_ref, misc_ref, z_ref, o_ref,
                ext_ref, st_ref):
    n = pl.program_id(1)
    L = SSM_CHUNK
    hi = lax.Precision.HIGHEST

    @pl.when(n == 0)
    def _():
        st_ref[...] = jnp.zeros(st_ref.shape, F32)
        ext_ref[:, 0:8, :] = jnp.zeros((ext_ref.shape[0], 8, CONV_CH), F32)

    row = lax.broadcasted_iota(jnp.int32, (L, L), 0)
    col = lax.broadcasted_iota(jnp.int32, (L, L), 1)
    causal = row >= col
    tril = jnp.where(causal, 1.0, 0.0).astype(F32)
    dsk = dsk_ref[...]

    for bb in range(xbc_ref.shape[0]):
        raw = xbc_ref[bb]
        ext_ref[bb, 8:8 + L, :] = raw
        conv = cb_ref[...] + raw * cw_ref[CONV_W - 1:CONV_W, :]
        for w in range(CONV_W - 1):
            shift = CONV_W - 1 - w
            conv = conv + ext_ref[bb, 8 - shift:8 - shift + L, :] * cw_ref[w:w + 1, :]
        ext_ref[bb, 0:8, :] = raw[L - 8:L, :]
        xc = _silu(conv)
        x = xc[:, 0:GROUP_W]
        bm = xc[:, GROUP_W:GROUP_W + 2 * SSM_STATE].astype(BF16)
        cm = xc[:, GROUP_W + 2 * SSM_STATE:].astype(BF16)

        dt_full = jax.nn.softplus(misc_ref[bb] + dtb_ref[...])
        da = dt_full * a_ref[...]
        cs_col = jnp.dot(tril, da, precision=hi, preferred_element_type=F32)
        cs_row = lax.dot_general(da, tril, (((0,), (1,)), ((), ())), precision=hi,
                                 preferred_element_type=F32)

        outs = []
        for h in range(HEADS):
            g = h // 2
            c0 = DT_COL + h
            sl = slice(h * HEAD_DIM, (h + 1) * HEAD_DIM)
            gs = slice(g * SSM_STATE, (g + 1) * SSM_STATE)
            cs_c = cs_col[:, c0:c0 + 1]
            cs_r = cs_row[c0:c0 + 1, :]
            cs_last = cs_col[L - 1:L, c0:c0 + 1]
            xh = x[:, sl]
            xdt = xh * dt_full[:, c0:c0 + 1]
            seg = jnp.exp(jnp.where(causal, cs_c - cs_r, NEG_INF))
            cb = _dot_nt(cm[:, gs], bm[:, gs])
            y = _dot((cb * seg).astype(BF16), xdt.astype(BF16))
            prev = st_ref[bb * HEADS + h]
            y = y + _dot(cm[:, gs], prev.astype(BF16)) * jnp.exp(cs_c)
            y = y + dsk[:, sl] * xh
            dec = jnp.exp(cs_last - cs_c)
            st_ref[bb * HEADS + h] = prev * jnp.exp(cs_last) + _dot_tn(bm[:, gs], (xdt * dec).astype(BF16))
            outs.append(y)
        y = jnp.concatenate(outs, axis=-1) * _silu(z_ref[bb])
        ms = jnp.mean(y * y, axis=-1, keepdims=True)
        o_ref[bb] = (y * lax.rsqrt(ms + EPS) * nw_ref[...]).astype(o_ref.dtype)


def _ssd(xbc, misc, z_all, conv_w, conv_b, dt_bias, a_log, d_skip, norm_w, batch, seq, bpb):
    L = SSM_CHUNK
    nch = seq // L
    dtb = jnp.zeros((1, 128), F32).at[0, DT_COL:DT_COL + HEADS].set(dt_bias)
    a_full = jnp.zeros((1, 128), F32).at[0, DT_COL:DT_COL + HEADS].set(-jnp.exp(a_log))
    dsk = jnp.repeat(d_skip, HEAD_DIM)[None, :]

    def full(shape):
        return pl.BlockSpec(shape, lambda b, n: (0,) * len(shape))

    out = pl.pallas_call(
        _ssd_kernel,
        grid=(batch // bpb, nch),
        in_specs=[full((CONV_W, CONV_CH)), full((1, CONV_CH)), full((1, 128)), full((1, 128)),
                  full((1, GROUP_W)), full((1, GROUP_W)),
                  pl.BlockSpec((bpb, L, CONV_CH), lambda b, n: (b, n, 0)),
                  pl.BlockSpec((bpb, L, 128), lambda b, n: (b, n, 0)),
                  pl.BlockSpec((bpb, L, GROUP_W), lambda b, n: (b, n, 3))],
        out_specs=pl.BlockSpec((bpb, L, GROUP_W), lambda b, n: (b, n, 0)),
        out_shape=jax.ShapeDtypeStruct((batch, seq, GROUP_W), BF16),
        scratch_shapes=[pltpu.VMEM((bpb, 8 + L, CONV_CH), F32),
                        pltpu.VMEM((bpb * HEADS, SSM_STATE, HEAD_DIM), F32)],
        compiler_params=_params(("parallel", "arbitrary")),
        name="ssd",
    )(conv_w, conv_b.reshape(1, CONV_CH), dtb, a_full, dsk, norm_w.reshape(1, GROUP_W),
      xbc.reshape(batch, seq, -1), misc.reshape(batch, seq, -1), z_all.reshape(batch, seq, -1))
    return out.reshape(batch * seq, GROUP_W)


def _pick_tile(n, pref):
    t = pref
    while n % t:
        t //= 2
    return t


def kernel(x, norm_w, w_in, w_out, nsa_pe_k, nsa_pe_v, nsa_w_ck1, nsa_w_ck2, nsa_w_cv1, nsa_w_cv2,
           diff_lam_q1, diff_lam_k1, diff_lam_q2, diff_lam_k2, diff_subln_w, ret_gn_w,
           ssm_conv_w, ssm_conv_b, ssm_dt_bias, ssm_A_log, ssm_D, ssm_norm_w, final_norm_w):
    batch, seq, _ = x.shape
    depth = w_in.shape[0]
    m = batch * seq
    tm = _pick_tile(m, 512)
    tq = _pick_tile(seq, 512)
    tk = _pick_tile(seq, 256)
    bpb = _pick_tile(batch, REC_BATCH)
    w_r = _relayout_w_in(w_in)
    w_t = _relayout_w_in_t(w_in)
    w_out_b = w_out.astype(BF16)
    x2d = x.reshape(m, D_MODEL)
    projected = _in_proj(x2d, norm_w[0], w_r[0], w_t[0], tm)
    for i in range(depth):
        nsa_q, nsa_k2, nsa_cmp, misc, z_all, diff_qk, ret_qkv, xbc, vt_all = projected
        kc, vct = _nsa_compress(nsa_cmp, nsa_pe_k[i], nsa_pe_v[i], nsa_w_ck1[i], nsa_w_ck2[i],
                                nsa_w_cv1[i], nsa_w_cv2[i], batch, seq)
        y_nsa = _nsa_attention(nsa_q, nsa_k2, vt_all, kc, vct, misc, z_all, batch, seq, tq, tk)
        lam_p = jnp.stack([diff_lam_q1[i], diff_lam_k1[i], diff_lam_q2[i], diff_lam_k2[i]])
        y_diff = _diff_attention(diff_qk, vt_all, z_all, lam_p, diff_subln_w[i], batch, seq, i, tq, tk)
        y_ret = _retention(ret_qkv, z_all, ret_gn_w[i], batch, seq, bpb)
        y_ssm = _ssd(xbc, misc, z_all, ssm_conv_w[i], ssm_conv_b[i], ssm_dt_bias[i], ssm_A_log[i],
                     ssm_D[i], ssm_norm_w[i], batch, seq, bpb)
        ys = (y_nsa, y_diff, y_ret, y_ssm)
        if i + 1 < depth:
            x2d, *projected = _out_in_proj(ys, w_out_b[i], x2d, norm_w[i + 1], w_r[i + 1], w_t[i + 1], tm)
        else:
            x2d = _out_proj(ys, w_out_b[i], x2d, final_norm_w, tm)
    return x2d.reshape(batch, seq, D_MODEL)
```

```python
import functools
import math

import numpy as np
import jax
import jax.numpy as jnp
from jax import lax
from jax.experimental import pallas as pl
from jax.experimental.pallas import tpu as pltpu

F32 = jnp.float32
BF16 = jnp.bfloat16
NEG_INF = float("-inf")
LOG2E = 1.4426950408889634

D_MODEL = 1024
DEPTH = 4
GROUP_W = 256
HEADS = 4
HEAD_DIM = 64
EPS = 1e-6
CMP_LEN = 32
CMP_STRIDE = 16
CMP_HIDDEN = 256
SLC_BLOCK = 64
SLC_SHIFT = 6
SLC_TOPK = 16
WINDOW = 512
DIFF_QK_DIM = 32
RET_CHUNK = 128
SSM_STATE = 128
SSM_CHUNK = 128
CONV_W = 4
CONV_CH = 768
N_ALIBI_HEADS = 8
LANES = 128
BF16_ROWS = 16
ACC_ROWS = HEAD_DIM + BF16_ROWS
QK_LOOKAHEAD = 6
ALIBI_ROWS = 3
MASK_BIG = 2.0 ** 100
BLK_LANE0 = 64
REC_BATCH = 4

IN_LAYOUT = (
    ("nsa_q", 256), ("nsa_k_cmp", 64), ("nsa_v_cmp", 64), ("nsa_k_slc", 64), ("nsa_v_slc", 64),
    ("nsa_k_win", 64), ("nsa_v_win", 64), ("nsa_gate", 12), ("nsa_z", 256),
    ("diff_q", 256), ("diff_k", 256), ("diff_v", 256), ("diff_z", 256),
    ("ret_q", 256), ("ret_k", 256), ("ret_v", 256), ("ret_z", 256),
    ("ssm_z", 256), ("ssm_xbc", 768), ("ssm_dt", 4),
)
IN_OFF = {}
_o = 0
for _n, _w in IN_LAYOUT:
    IN_OFF[_n] = (_o, _w)
    _o += _w
IN_W = _o

GATE_COL = 0
DT_COL = 12
IN_OUTPUTS = (
    ("nsa_q", BF16, ("nsa_q",), 256),
    ("nsa_k2", BF16, ("nsa_k_slc", "nsa_k_win"), 128),
    ("nsa_cmp", F32, ("nsa_k_cmp", "nsa_v_cmp"), 128),
    ("misc", F32, ("nsa_gate", "ssm_dt"), 128),
    ("z_all", F32, ("nsa_z", "diff_z", "ret_z", "ssm_z"), 1024),
    ("diff_qk", BF16, ("diff_q", "diff_k"), 512),
    ("ret_qkv", BF16, ("ret_q", "ret_k", "ret_v"), 768),
    ("xbc", F32, ("ssm_xbc",), 768),
)
IN_T_SRC = ("diff_v", "nsa_v_slc", "nsa_v_win")
IN_T_ROWS = 384
DIFF_VT_BLK = 0
NSA_VT_BLK = 2
IN_SEGS = []
_o = 0
for _n, _dt, _src, _w in IN_OUTPUTS:
    IN_SEGS.append((_o, _o + _w))
    _o += _w
IN_WP = _o

VMEM_LIMIT = 56 * 1024 * 1024


def _alibi_slopes():
    return [2.0 ** (-8.0 * (i + 1) / N_ALIBI_HEADS) for i in range(N_ALIBI_HEADS)]


NSA_SLOPES = _alibi_slopes()[0::2]
DIFF_SLOPES = _alibi_slopes()[1::2]


def _silu(x):
    return x * jax.nn.sigmoid(x)


def _dot(a, b):
    return jnp.dot(a, b, preferred_element_type=F32)


def _dot_nt(a, b):
    return lax.dot_general(a, b, (((1,), (1,)), ((), ())), preferred_element_type=F32)


def _dot_tn(a, b):
    return lax.dot_general(a, b, (((0,), (0,)), ((), ())), preferred_element_type=F32)


def _params(sem):
    return pltpu.CompilerParams(dimension_semantics=sem, vmem_limit_bytes=VMEM_LIMIT)


def _relayout_w_in(w_in):
    cols = []
    for _n, _dt, src, width in IN_OUTPUTS:
        used = 0
        for s in src:
            off, w = IN_OFF[s]
            cols.append(w_in[:, :, off:off + w])
            used += w
        if used < width:
            cols.append(jnp.zeros(w_in.shape[:2] + (width - used,), w_in.dtype))
    return jnp.concatenate(cols, axis=-1).astype(BF16)


def _relayout_w_in_t(w_in):
    cols = [w_in[:, :, IN_OFF[s][0]:IN_OFF[s][0] + IN_OFF[s][1]] for s in IN_T_SRC]
    return jnp.swapaxes(jnp.concatenate(cols, axis=-1), 1, 2).astype(BF16)


def _norm_project(x, nw_ref, w_ref, wt_ref, out_refs):
    ms = jnp.mean(x * x, axis=-1, keepdims=True)
    h = (x * lax.rsqrt(ms + EPS) * nw_ref[...]).astype(BF16)
    for ref, (a, b) in zip(out_refs[:-1], IN_SEGS):
        ref[...] = _dot(h, w_ref[:, a:b]).astype(ref.dtype)
    out_refs[-1][...] = _dot_nt(wt_ref[...], h).astype(BF16)


def _in_proj_kernel(x_ref, nw_ref, w_ref, wt_ref, *out_refs):
    _norm_project(x_ref[...], nw_ref, w_ref, wt_ref, out_refs)


def _in_proj_outputs(m, tm):
    out_shape = [jax.ShapeDtypeStruct((m, b - a), dt) for (_n, dt, _s, _w), (a, b) in zip(IN_OUTPUTS, IN_SEGS)]
    out_specs = [pl.BlockSpec((tm, b - a), lambda i: (i, 0)) for (a, b) in IN_SEGS]
    out_shape.append(jax.ShapeDtypeStruct((IN_T_ROWS, m), BF16))
    out_specs.append(pl.BlockSpec((IN_T_ROWS, tm), lambda i: (0, i)))
    return out_shape, out_specs


def _in_proj(x2d, norm_w, w_r, w_t, tm):
    m = x2d.shape[0]
    out_shape, out_specs = _in_proj_outputs(m, tm)
    return pl.pallas_call(
        _in_proj_kernel,
        grid=(m // tm,),
        in_specs=[pl.BlockSpec((tm, D_MODEL), lambda i: (i, 0)),
                  pl.BlockSpec((1, D_MODEL), lambda i: (0, 0)),
                  pl.BlockSpec((D_MODEL, IN_WP), lambda i: (0, 0)),
                  pl.BlockSpec((IN_T_ROWS, D_MODEL), lambda i: (0, 0))],
        out_specs=out_specs,
        out_shape=out_shape,
        compiler_params=_params(("parallel",)),
        name="in_proj",
    )(x2d, norm_w.reshape(1, D_MODEL), w_r, w_t)


def _out_proj_kernel(y0_ref, y1_ref, y2_ref, y3_ref, w_ref, x_ref, fw_ref, o_ref):
    acc = x_ref[...]
    for g, y_ref in enumerate((y0_ref, y1_ref, y2_ref, y3_ref)):
        acc = acc + _dot(y_ref[...], w_ref[g * GROUP_W:(g + 1) * GROUP_W, :])
    ms = jnp.mean(acc * acc, axis=-1, keepdims=True)
    o_ref[...] = acc * lax.rsqrt(ms + EPS) * fw_ref[...]


def _out_proj(ys, w_out_b, x2d, final_w, tm):
    m = x2d.shape[0]
    yspec = pl.BlockSpec((tm, GROUP_W), lambda i: (i, 0))
    return pl.pallas_call(
        _out_proj_kernel,
        grid=(m // tm,),
        in_specs=[yspec, yspec, yspec, yspec,
                  pl.BlockSpec((D_MODEL, D_MODEL), lambda i: (0, 0)),
                  pl.BlockSpec((tm, D_MODEL), lambda i: (i, 0)),
                  pl.BlockSpec((1, D_MODEL), lambda i: (0, 0))],
        out_specs=pl.BlockSpec((tm, D_MODEL), lambda i: (i, 0)),
        out_shape=jax.ShapeDtypeStruct((m, D_MODEL), F32),
        compiler_params=_params(("parallel",)),
        name="out_proj",
    )(*ys, w_out_b, x2d, final_w.reshape(1, D_MODEL))


def _out_in_proj_kernel(y0_ref, y1_ref, y2_ref, y3_ref, wo_ref, x_ref, nw_ref, w_ref, wt_ref, xo_ref, *out_refs):
    acc = x_ref[...]
    for g, y_ref in enumerate((y0_ref, y1_ref, y2_ref, y3_ref)):
        acc = acc + _dot(y_ref[...], wo_ref[g * GROUP_W:(g + 1) * GROUP_W, :])
    xo_ref[...] = acc
    _norm_project(acc, nw_ref, w_ref, wt_ref, out_refs)


def _out_in_proj(ys, w_out_b, x2d, norm_w, w_r, w_t, tm):
    m = x2d.shape[0]
    yspec = pl.BlockSpec((tm, GROUP_W), lambda i: (i, 0))
    xspec = pl.BlockSpec((tm, D_MODEL), lambda i: (i, 0))
    out_shape, out_specs = _in_proj_outputs(m, tm)
    return pl.pallas_call(
        _out_in_proj_kernel,
        grid=(m // tm,),
        in_specs=[yspec, yspec, yspec, yspec,
                  pl.BlockSpec((D_MODEL, D_MODEL), lambda i: (0, 0)),
                  xspec,
                  pl.BlockSpec((1, D_MODEL), lambda i: (0, 0)),
                  pl.BlockSpec((D_MODEL, IN_WP), lambda i: (0, 0)),
                  pl.BlockSpec((IN_T_ROWS, D_MODEL), lambda i: (0, 0))],
        out_specs=[xspec] + out_specs,
        out_shape=[jax.ShapeDtypeStruct((m, D_MODEL), F32)] + out_shape,
        compiler_params=_params(("parallel",)),
        name="out_in_proj",
    )(*ys, w_out_b, x2d, norm_w.reshape(1, D_MODEL), w_r, w_t)


def _flash_update_t(s, c1, shift, vt_ext, m_ref, acc_ref, idx, cols):
    m_old = m_ref[idx, :, cols]
    m_new = jnp.maximum(m_old, c1 * jnp.max(s(), axis=0, keepdims=True) + shift)
    alpha = jnp.exp2(m_old - m_new)
    p = jnp.exp2(c1 * s() - (m_new - shift))
    acc_ref[idx, :, cols] = alpha * acc_ref[idx, :, cols] + _dot(vt_ext, p.astype(BF16))
    m_ref[idx, :, cols] = m_new


def _bf16_pieces(x, n=3):
    out = []
    for _ in range(n):
        p = float(np.asarray(x, np.float32).astype(BF16).astype(np.float32))
        out.append(p)
        x = x - p
    return out


def _key_pos_features(tk):
    lane = lax.broadcasted_iota(jnp.int32, (tk, LANES), 1)
    row = lax.broadcasted_iota(jnp.int32, (tk, LANES), 0)
    return jnp.where(lane < ALIBI_ROWS, row, 0).astype(F32).astype(BF16)


def _alibi_rows(beta, tq, rows=LANES):
    row = lax.broadcasted_iota(jnp.int32, (rows, tq), 0)
    out = jnp.zeros((rows, tq), F32)
    for r, piece in enumerate(_bf16_pieces(beta, ALIBI_ROWS)):
        out = jnp.where(row == r, piece, out)
    return out.astype(BF16)


def _normalized(acc):
    return acc[0:HEAD_DIM] / jnp.maximum(acc[HEAD_DIM:HEAD_DIM + 1], 1e-30)


def _rel_t(tk, tq):
    return lax.broadcasted_iota(jnp.int32, (tk, tq), 1) - lax.broadcasted_iota(jnp.int32, (tk, tq), 0)


def _diff_kernel(lam_ref, sw_ref, q_ref, k_ref, vt_ref, z_ref, o_ref, m_ref, acc_ref, wq_ref, s_ref,
                 *, tq, tk, lam_init):
    qi = pl.program_id(1)
    kpq = tq // tk
    scale = DIFF_QK_DIM ** -0.5
    c1 = scale * LOG2E
    m_ref[...] = jnp.full(m_ref.shape, NEG_INF, F32)
    acc_ref[...] = jnp.zeros(acc_ref.shape, F32)
    rel = _rel_t(tk, tq)

    qt = q_ref[...].astype(F32).T.astype(BF16)
    row = lax.broadcasted_iota(jnp.int32, (LANES, tq), 0)
    for j in range(2 * HEADS):
        g, r0 = divmod(j * DIFF_QK_DIM, LANES)
        qg = qt[g * LANES:(g + 1) * LANES]
        wq_ref[j, 0:LANES, :] = jnp.where((row >= r0) & (row < r0 + DIFF_QK_DIM), qg, jnp.zeros_like(qg))
        wq_ref[j, LANES:2 * LANES, :] = _alibi_rows(DIFF_SLOPES[j // 2] / scale, tq)

    ones = jnp.ones((BF16_ROWS, tk), BF16)
    kpos = _key_pos_features(tk)

    n_maps = 2 * HEADS

    def tiles(kis, mask_offs):
        loaded = []
        for ki in kis:
            start = pl.multiple_of(ki * tk, tk)
            loaded.append((k_ref[pl.ds(start, tk), :],
                           vt_ref[:, pl.ds(start, tk)],
                           (ki * tk).astype(F32)))
        items = [(t, j) for t in range(len(kis)) for j in range(n_maps)]
        cols = [slice(0, tq)] * len(kis)

        def scores(i):
            t, j = items[i]
            g = (j * DIFF_QK_DIM) // LANES
            lhs = jnp.concatenate([loaded[t][0][:, g * LANES:(g + 1) * LANES], kpos], axis=1)
            s_ref[i % n_maps, :, cols[t]] = _dot(lhs, wq_ref[j, :, cols[t]])

        def staged(i):
            t = items[i][0]
            if mask_offs[t] is None:
                return lambda: s_ref[i % n_maps, :, cols[t]]
            return lambda: jnp.where(rel[:, cols[t]] >= mask_offs[t], s_ref[i % n_maps, :, cols[t]], NEG_INF)

        for i in range(QK_LOOKAHEAD):
            scores(i)
        for i, (t, j) in enumerate(items):
            h = j // 2
            if i + QK_LOOKAHEAD < len(items):
                scores(i + QK_LOOKAHEAD)
            _k, vt, key0 = loaded[t]
            vt_ext = jnp.concatenate([vt[h * HEAD_DIM:(h + 1) * HEAD_DIM], ones], axis=0)
            _flash_update_t(staged(i), c1, (DIFF_SLOPES[h] * LOG2E) * key0, vt_ext, m_ref, acc_ref, j, cols[t])

    def body(i, carry):
        tiles([i * kpq + d for d in range(kpq)], [None] * kpq)
        return carry

    lax.fori_loop(0, qi, body, 0)
    tiles([qi * kpq + d for d in range(kpq)], [d * tk for d in range(kpq)])

    lp = lam_ref[...]
    lam = (jnp.exp(jnp.sum(lp[0:1] * lp[1:2], axis=-1, keepdims=True))
           - jnp.exp(jnp.sum(lp[2:3] * lp[3:4], axis=-1, keepdims=True)) + lam_init)
    sw = sw_ref[...]
    o_t = jnp.concatenate([_normalized(acc_ref[2 * h]) - lam * _normalized(acc_ref[2 * h + 1])
                           for h in range(HEADS)], axis=0)
    o = o_t.T
    outs = []
    for h in range(HEADS):
        oh = o[:, h * HEAD_DIM:(h + 1) * HEAD_DIM]
        ms = jnp.mean(oh * oh, axis=-1, keepdims=True)
        outs.append(oh * lax.rsqrt(ms + EPS) * sw * (1.0 - lam_init))
    y = jnp.concatenate(outs, axis=-1) * _silu(z_ref[...])
    o_ref[...] = y.astype(o_ref.dtype)


def _diff_attention(qk, vt_all, z_all, lam_p, subln_w, batch, seq, layer_idx, tq, tk):
    nq = seq // tq
    lam_init = 0.8 - 0.6 * math.exp(-0.3 * layer_idx)
    return pl.pallas_call(
        functools.partial(_diff_kernel, tq=tq, tk=tk, lam_init=lam_init),
        grid=(batch, nq),
        in_specs=[pl.BlockSpec((4, DIFF_QK_DIM), lambda b, i: (0, 0)),
                  pl.BlockSpec((1, HEAD_DIM), lambda b, i: (0, 0)),
                  pl.BlockSpec((tq, GROUP_W), lambda b, i: (b * nq + i, 0)),
                  pl.BlockSpec((seq, GROUP_W), lambda b, i: (b, 1)),
                  pl.BlockSpec((GROUP_W, seq), lambda b, i: (DIFF_VT_BLK, b)),
                  pl.BlockSpec((tq, GROUP_W), lambda b, i: (b * nq + i, 1))],
        out_specs=pl.BlockSpec((tq, GROUP_W), lambda b, i: (b * nq + i, 0)),
        out_shape=jax.ShapeDtypeStruct((batch * seq, GROUP_W), BF16),
        scratch_shapes=[pltpu.VMEM((2 * HEADS, 1, tq), F32),
                        pltpu.VMEM((2 * HEADS, ACC_ROWS, tq), F32),
                        pltpu.VMEM((2 * HEADS, 2 * LANES, tq), BF16),
                        pltpu.VMEM((2 * HEADS, tk, tq), F32)],
        compiler_params=_params(("parallel", "parallel")),
        name="diff_attention",
    )(lam_p, subln_w.reshape(1, HEAD_DIM), qk, qk, vt_all, z_all)


def _nsa_compress_kernel(gk_ref, gv_ref, pek_ref, pev_ref, wk1_ref, wk2_ref, wv1_ref, wv2t_ref, kc_ref, vct_ref):
    half = CMP_STRIDE * HEAD_DIM

    def hidden(g_ref, pe_ref, w1_ref):
        g = g_ref[0]
        top = (g + pe_ref[0:1, :]).astype(BF16)
        bot = (g + pe_ref[1:2, :]).astype(BF16)
        a = _dot(top, w1_ref[0:half, :])
        b = _dot(bot, w1_ref[half:2 * half, :])
        nrow = b.shape[0]
        b = pltpu.roll(b, nrow - 1, 0)
        return _silu(a + b).astype(BF16)

    hk = hidden(gk_ref, pek_ref, wk1_ref)
    kc = _dot(hk, wk2_ref[...])
    lane = lax.broadcasted_iota(jnp.int32, kc.shape, 1)
    blk = lax.broadcasted_iota(jnp.int32, kc.shape, 0).astype(F32)
    kc = jnp.where((lane >= HEAD_DIM) & (lane < HEAD_DIM + ALIBI_ROWS), blk, kc)
    kc_ref[0] = kc.astype(kc_ref.dtype)
    hv = hidden(gv_ref, pev_ref, wv1_ref)
    vct_ref[0] = _dot_nt(wv2t_ref[...], hv).astype(vct_ref.dtype)


def _nsa_compress(cmp2d, pe_k, pe_v, w_ck1, w_ck2, w_cv1, w_cv2, batch, seq):
    ng = seq // CMP_STRIDE
    half = CMP_STRIDE * HEAD_DIM
    gk = cmp2d[:, :HEAD_DIM].reshape(batch, ng, half)
    gv = cmp2d[:, HEAD_DIM:].reshape(batch, ng, half)
    gspec = pl.BlockSpec((1, ng, half), lambda b: (b, 0, 0))

    def full(shape):
        return pl.BlockSpec(shape, lambda b: (0,) * len(shape))

    return pl.pallas_call(
        _nsa_compress_kernel,
        grid=(batch,),
        in_specs=[gspec, gspec, full((2, half)), full((2, half)),
                  full((2 * half, CMP_HIDDEN)), full((CMP_HIDDEN, LANES)),
                  full((2 * half, CMP_HIDDEN)), full((HEAD_DIM, CMP_HIDDEN))],
        out_specs=[pl.BlockSpec((1, ng, LANES), lambda b: (b, 0, 0)),
                   pl.BlockSpec((1, HEAD_DIM, ng), lambda b: (b, 0, 0))],
        out_shape=[jax.ShapeDtypeStruct((batch, ng, LANES), BF16),
                   jax.ShapeDtypeStruct((batch, HEAD_DIM, ng), BF16)],
        compiler_params=_params(("parallel",)),
        name="nsa_compress",
    )(gk, gv, pe_k.reshape(2, half), pe_v.reshape(2, half),
      w_ck1.astype(BF16), jnp.pad(w_ck2, ((0, 0), (0, LANES - HEAD_DIM))).astype(BF16),
      w_cv1.astype(BF16), w_cv2.T.astype(BF16))


def _nsa_kernel(ovl_ref, q_ref, k_ref, vt_ref, kc_ref, vct_ref, misc_ref, z_ref, o_ref,
                m_ref, acc_ref, wq_ref, s_ref, *, tq, tk, seq):
    qi = pl.program_id(1)
    kpq = tq // tk
    scale = HEAD_DIM ** -0.5
    c1 = scale * LOG2E
    ng = seq // CMP_STRIDE
    ns = seq // SLC_BLOCK
    top = min(SLC_TOPK, ns)
    m_ref[...] = jnp.full(m_ref.shape, NEG_INF, F32)
    acc_ref[...] = jnp.zeros(acc_ref.shape, F32)
    rel = _rel_t(tk, tq)

    qt = q_ref[...].astype(F32).T.astype(BF16)
    zeros_q = jnp.zeros((HEAD_DIM, tq), BF16)
    for h in range(HEADS):
        qh = qt[h * HEAD_DIM:(h + 1) * HEAD_DIM]
        alibi = _alibi_rows(NSA_SLOPES[h] / scale, tq)
        wq_ref[h, 0:LANES, :] = jnp.concatenate([qh, zeros_q], axis=0)
        wq_ref[h, LANES:2 * LANES, :] = alibi
        wq_ref[HEADS + h, 0:LANES, :] = jnp.concatenate([zeros_q, qh], axis=0)
        wq_ref[HEADS + h, LANES:2 * LANES, :] = alibi

    t_lane = qi * tq + lax.broadcasted_iota(jnp.int32, (1, tq), 1)
    n_sub = lax.broadcasted_iota(jnp.int32, (ng, 1), 0)
    c_valid = (n_sub * CMP_STRIDE + (CMP_LEN - 1)) <= t_lane
    kc = kc_ref[0]
    vct = vct_ref[0]
    ovl = ovl_ref[...]
    imp = jnp.zeros((ns, tq), F32)
    o_cmp = []
    for h in range(HEADS):
        rhs = jnp.concatenate([qt[h * HEAD_DIM:(h + 1) * HEAD_DIM],
                               _alibi_rows(NSA_SLOPES[h] * CMP_STRIDE / scale, tq, HEAD_DIM)], axis=0)
        s = jnp.where(c_valid, _dot(kc, rhs), NEG_INF)
        mx = jnp.max(s, axis=0, keepdims=True)
        mx = jnp.where(mx > NEG_INF, mx, 0.0)
        p = jnp.exp2(c1 * s - c1 * mx)
        p = p * (1.0 / jnp.maximum(jnp.sum(p, axis=0, keepdims=True), 1e-30))
        pb = p.astype(BF16)
        o_cmp.append(_dot(vct, pb))
        imp = imp + _dot(ovl, pb)

    def select_blocks():
        j_sub = lax.broadcasted_iota(jnp.int32, (ns, 1), 0)
        j_sub_f = j_sub.astype(F32)
        cur = jnp.right_shift(t_lane, SLC_SHIFT)
        forced = (j_sub == 0) | (j_sub == cur) | (j_sub == cur - 1)
        valid = (j_sub * SLC_BLOCK) <= t_lane
        score = jnp.where(forced, 1e30, jnp.where(valid, imp, -1.0))
        sel = jnp.zeros((ns, tq), F32)
        for _ in range(top):
            mx = jnp.max(score, axis=0, keepdims=True)
            idx = jnp.min(jnp.where(score == mx, j_sub_f, float(ns)), axis=0, keepdims=True)
            pick = j_sub_f == idx
            sel = jnp.where(pick, 1.0, sel)
            score = jnp.where(pick, -2.0, score)
        unsel = jnp.where(sel > 0.5, 0.0, -MASK_BIG).astype(BF16)
        for h in range(HEADS):
            wq_ref[h, LANES + BLK_LANE0:LANES + BLK_LANE0 + ns, :] = unsel

    aux_lane = lax.broadcasted_iota(jnp.int32, (tk, LANES), 1)
    aux_blk = jnp.right_shift(lax.broadcasted_iota(jnp.int32, (tk, LANES), 0), SLC_SHIFT) + BLK_LANE0
    kpos = _key_pos_features(tk)
    ones = jnp.ones((BF16_ROWS, tk), BF16)

    n_slots = s_ref.shape[0]

    def tiles(window, kis, cols, mask_offs=None):
        base = HEADS if window else 0
        loaded = []
        for t, ki in enumerate(kis):
            start = pl.multiple_of(ki * tk, tk)
            k = k_ref[pl.ds(start, tk), :]
            if window:
                vt = vt_ref[HEAD_DIM:2 * HEAD_DIM, pl.ds(start, tk)]
                aux = kpos
                dist = rel[:, cols[t]] + (qi * tq - ki * tk)
                keep = (dist >= 0) & (dist < WINDOW)
            else:
                vt = vt_ref[0:HEAD_DIM, pl.ds(start, tk)]
                onehot = aux_lane == aux_blk + ki * (tk // SLC_BLOCK)
                aux = jnp.where(onehot, jnp.ones_like(kpos), kpos)
                keep = None
            loaded.append((jnp.concatenate([k, aux], axis=1),
                           jnp.concatenate([vt, ones], axis=0), (ki * tk).astype(F32), keep))
        items = [(t, h) for t in range(len(kis)) for h in range(HEADS)]

        def scores(i):
            t, h = items[i]
            s_ref[i % n_slots, :, cols[t]] = _dot(loaded[t][0], wq_ref[base + h, :, cols[t]])

        def staged(i):
            t = items[i][0]
            if window:
                return lambda: jnp.where(loaded[t][3], s_ref[i % n_slots, :, cols[t]], NEG_INF)
            if mask_offs[t] is None:
                return lambda: s_ref[i % n_slots, :, cols[t]]
            return lambda: jnp.where(rel[:, cols[t]] >= mask_offs[t], s_ref[i % n_slots, :, cols[t]], NEG_INF)

        for i in range(QK_LOOKAHEAD):
            scores(i)
        for i, (t, h) in enumerate(items):
            if i + QK_LOOKAHEAD < len(items):
                scores(i + QK_LOOKAHEAD)
            _lhs, vt_ext, key0, _keep = loaded[t]
            _flash_update_t(staged(i), c1, (NSA_SLOPES[h] * LOG2E) * key0, vt_ext, m_ref, acc_ref, base + h,
                            cols[t])

    all_cols = slice(0, tq)
    diag_cols = [slice(d * tk, tq) for d in range(kpq)]

    def body(i, carry):
        tiles(False, [i * kpq + d for d in range(kpq)], [all_cols] * kpq, [None] * kpq)
        return carry

    tiles(True, [qi * kpq + d for d in range(kpq)], diag_cols)
    select_blocks()

    lax.fori_loop(0, qi, body, 0)
    tiles(False, [qi * kpq + d for d in range(kpq)], diag_cols, [d * tk for d in range(kpq)])

    n_back = (WINDOW + tk - 1) // tk
    for g in range((n_back + kpq - 1) // kpq):
        backs = list(range(g * kpq + 1, min((g + 1) * kpq, n_back) + 1))
        back_cols = [slice(0, min(tq, -(-(WINDOW - (back - 1) * tk - 1) // LANES) * LANES)) for back in backs]

        @pl.when(qi > g)
        def _():
            tiles(True, [qi * kpq - back for back in backs], back_cols)

    g_t = jax.nn.sigmoid(misc_ref[...]).T
    outs = []
    for h in range(HEADS):
        r0 = GATE_COL + 3 * h
        outs.append(g_t[r0:r0 + 1] * o_cmp[h] + g_t[r0 + 1:r0 + 2] * _normalized(acc_ref[h])
                    + g_t[r0 + 2:r0 + 3] * _normalized(acc_ref[HEADS + h]))
    y = jnp.concatenate(outs, axis=0).T * _silu(z_ref[...])
    o_ref[...] = y.astype(o_ref.dtype)


def _overlap_t(seq):
    nc = (seq - CMP_LEN) // CMP_STRIDE + 1
    ng = seq // CMP_STRIDE
    ns = seq // SLC_BLOCK
    c_start = np.arange(ng) * CMP_STRIDE
    c_end = c_start + CMP_LEN - 1
    s_start = np.arange(ns) * SLC_BLOCK
    s_end = s_start + SLC_BLOCK - 1
    ov = (c_start[None, :] <= s_end[:, None]) & (c_end[None, :] >= s_start[:, None]) & (np.arange(ng)[None, :] < nc)
    return jnp.asarray(ov.astype(np.float32), dtype=BF16)


def _nsa_attention(q, k2, vt_all, kc, vct, misc, z_all, batch, seq, tq, tk):
    nq = seq // tq
    ng = seq // CMP_STRIDE
    ns = seq // SLC_BLOCK
    assert tq - tk < WINDOW and tk % SLC_BLOCK == 0 and ns <= LANES - BLK_LANE0
    return pl.pallas_call(
        functools.partial(_nsa_kernel, tq=tq, tk=tk, seq=seq),
        grid=(batch, nq),
        in_specs=[pl.BlockSpec((ns, ng), lambda b, i: (0, 0)),
                  pl.BlockSpec((tq, GROUP_W), lambda b, i: (b * nq + i, 0)),
                  pl.BlockSpec((seq, LANES), lambda b, i: (b, 0)),
                  pl.BlockSpec((2 * HEAD_DIM, seq), lambda b, i: (NSA_VT_BLK, b)),
                  pl.BlockSpec((1, ng, LANES), lambda b, i: (b, 0, 0)),
                  pl.BlockSpec((1, HEAD_DIM, ng), lambda b, i: (b, 0, 0)),
                  pl.BlockSpec((tq, LANES), lambda b, i: (b * nq + i, 0)),
                  pl.BlockSpec((tq, GROUP_W), lambda b, i: (b * nq + i, 0))],
        out_specs=pl.BlockSpec((tq, GROUP_W), lambda b, i: (b * nq + i, 0)),
        out_shape=jax.ShapeDtypeStruct((batch * seq, GROUP_W), BF16),
        scratch_shapes=[pltpu.VMEM((2 * HEADS, 1, tq), F32),
                        pltpu.VMEM((2 * HEADS, ACC_ROWS, tq), F32),
                        pltpu.VMEM((2 * HEADS, 2 * LANES, tq), BF16),
                        pltpu.VMEM((2 * HEADS, tk, tq), F32)],
        compiler_params=_params(("parallel", "parallel")),
        name="nsa_attention",
    )(_overlap_t(seq), q, k2, vt_all, kc, vct, misc, z_all)


def _ret_tables():
    c = RET_CHUNK
    h = np.arange(HEADS, dtype=np.float32)
    log_g = jnp.log(1.0 - 2.0 ** (-5.0 - jnp.asarray(h)))
    pos = jnp.arange(c, dtype=F32)
    rel = pos[:, None] - pos[None, :]
    decay = jnp.where(rel >= 0, jnp.exp(log_g[:, None, None] * jnp.maximum(rel, 0.0)), 0.0)
    xi = jnp.exp(log_g[:, None] * (pos + 1.0))
    zeta = jnp.exp(log_g[:, None] * (c - 1.0 - pos))
    chunk_decay = jnp.exp(log_g * c)
    xi_tab = jnp.repeat(xi.T, HEAD_DIM, axis=1)
    zeta_tab = jnp.repeat(zeta.T, HEAD_DIM, axis=1)
    cd_tab = jnp.repeat(chunk_decay, HEAD_DIM)[None, :]
    return decay, xi_tab, zeta_tab, cd_tab


def _ret_kernel(decay_ref, xi_ref, zeta_ref, cd_ref, gn_ref, q_ref, k_ref, v_ref, z_ref, o_ref, st_ref):
    n = pl.program_id(1)

    @pl.when(n == 0)
    def _():
        st_ref[...] = jnp.zeros(st_ref.shape, F32)

    xi = xi_ref[...]
    cd = cd_ref[...]
    for bb in range(q_ref.shape[0]):
        q = (q_ref[bb].astype(F32) * (HEAD_DIM ** -0.5)).astype(BF16)
        k = k_ref[bb]
        v = v_ref[bb]
        kz_t = (k.astype(F32) * zeta_ref[...]).T.astype(BF16)
        outs = []
        for h in range(HEADS):
            sl = slice(h * HEAD_DIM, (h + 1) * HEAD_DIM)
            qh, kh, vh = q[:, sl], k[:, sl], v[:, sl]
            prev = st_ref[bb * HEADS + h]
            inner = (_dot_nt(qh, kh) * decay_ref[h]).astype(BF16)
            o = _dot(inner, vh) + _dot(qh, prev.astype(BF16)) * xi[:, sl]
            st_ref[bb * HEADS + h] = prev * cd[:, sl] + _dot(kz_t[sl, :], vh)
            mu = jnp.mean(o, axis=-1, keepdims=True)
            d = o - mu
            var = jnp.mean(d * d, axis=-1, keepdims=True)
            outs.append(d * lax.rsqrt(var + EPS))
        y = jnp.concatenate(outs, axis=-1) * gn_ref[...] * _silu(z_ref[bb])
        o_ref[bb] = y.astype(o_ref.dtype)


def _retention(qkv, z_all, gn_w, batch, seq, bpb):
    c = RET_CHUNK
    nch = seq // c
    decay, xi_tab, zeta_tab, cd_tab = _ret_tables()

    def full(shape):
        return pl.BlockSpec(shape, lambda b, n: (0,) * len(shape))

    def blk(col):
        return pl.BlockSpec((bpb, c, GROUP_W), lambda b, n: (b, n, col))

    out = pl.pallas_call(
        _ret_kernel,
        grid=(batch // bpb, nch),
        in_specs=[full((HEADS, c, c)), full((c, GROUP_W)), full((c, GROUP_W)), full((1, GROUP_W)),
                  full((1, GROUP_W)), blk(0), blk(1), blk(2), blk(2)],
        out_specs=blk(0),
        out_shape=jax.ShapeDtypeStruct((batch, seq, GROUP_W), BF16),
        scratch_shapes=[pltpu.VMEM((bpb * HEADS, HEAD_DIM, HEAD_DIM), F32)],
        compiler_params=_params(("parallel", "arbitrary")),
        name="retention",
    )(decay, xi_tab, zeta_tab, cd_tab, gn_w.reshape(1, GROUP_W),
      *(qkv.reshape(batch, seq, -1),) * 3, z_all.reshape(batch, seq, -1))
    return out.reshape(batch * seq, GROUP_W)


def _ssd_kernel(cw_ref, cb_ref, dtb_ref, a_ref, dsk_ref, nw_ref, xbc_ref, misc_ref, z_ref, o_ref,
                ext_ref, st_ref):
    n = pl.program_id(1)
    L = SSM_CHUNK
    hi = lax.Precision.HIGHEST

    @pl.when(n == 0)
    def _():
        st_ref[...] = jnp.zeros(st_ref.shape, F32)
        ext_ref[:, 0:8, :] = jnp.zeros((ext_ref.shape[0], 8, CONV_CH), F32)

    row = lax.broadcasted_iota(jnp.int32, (L, L), 0)
    col = lax.broadcasted_iota(jnp.int32, (L, L), 1)
    causal = row >= col
    tril = jnp.where(causal, 1.0, 0.0).astype(F32)
    dsk = dsk_ref[...]

    for bb in range(xbc_ref.shape[0]):
        raw = xbc_ref[bb]
        ext_ref[bb, 8:8 + L, :] = raw
        conv = cb_ref[...] + raw * cw_ref[CONV_W - 1:CONV_W, :]
        for w in range(CONV_W - 1):
            shift = CONV_W - 1 - w
            conv = conv + ext_ref[bb, 8 - shift:8 - shift + L, :] * cw_ref[w:w + 1, :]
        ext_ref[bb, 0:8, :] = raw[L - 8:L, :]
        xc = _silu(conv)
        x = xc[:, 0:GROUP_W]
        bm = xc[:, GROUP_W:GROUP_W + 2 * SSM_STATE].astype(BF16)
        cm = xc[:, GROUP_W + 2 * SSM_STATE:].astype(BF16)

        dt_full = jax.nn.softplus(misc_ref[bb] + dtb_ref[...])
        da = dt_full * a_ref[...]
        cs_col = jnp.dot(tril, da, precision=hi, preferred_element_type=F32)
        cs_row = lax.dot_general(da, tril, (((0,), (1,)), ((), ())), precision=hi,
                                 preferred_element_type=F32)

        outs = []
        for h in range(HEADS):
            g = h // 2
            c0 = DT_COL + h
            sl = slice(h * HEAD_DIM, (h + 1) * HEAD_DIM)
            gs = slice(g * SSM_STATE, (g + 1) * SSM_STATE)
            cs_c = cs_col[:, c0:c0 + 1]
            cs_r = cs_row[c0:c0 + 1, :]
            cs_last = cs_col[L - 1:L, c0:c0 + 1]
            xh = x[:, sl]
            xdt = xh * dt_full[:, c0:c0 + 1]
            seg = jnp.exp(jnp.where(causal, cs_c - cs_r, NEG_INF))
            cb = _dot_nt(cm[:, gs], bm[:, gs])
            y = _dot((cb * seg).astype(BF16), xdt.astype(BF16))
            prev = st_ref[bb * HEADS + h]
            y = y + _dot(cm[:, gs], prev.astype(BF16)) * jnp.exp(cs_c)
            y = y + dsk[:, sl] * xh
            dec = jnp.exp(cs_last - cs_c)
            st_ref[bb * HEADS + h] = prev * jnp.exp(cs_last) + _dot_tn(bm[:, gs], (xdt * dec).astype(BF16))
            outs.append(y)
        y = jnp.concatenate(outs, axis=-1) * _silu(z_ref[bb])
        ms = jnp.mean(y * y, axis=-1, keepdims=True)
        o_ref[bb] = (y * lax.rsqrt(ms + EPS) * nw_ref[...]).astype(o_ref.dtype)


def _ssd(xbc, misc, z_all, conv_w, conv_b, dt_bias, a_log, d_skip, norm_w, batch, seq, bpb):
    L = SSM_CHUNK
    nch = seq // L
    dtb = jnp.zeros((1, 128), F32).at[0, DT_COL:DT_COL + HEADS].set(dt_bias)
    a_full = jnp.zeros((1, 128), F32).at[0, DT_COL:DT_COL + HEADS].set(-jnp.exp(a_log))
    dsk = jnp.repeat(d_skip, HEAD_DIM)[None, :]

    def full(shape):
        return pl.BlockSpec(shape, lambda b, n: (0,) * len(shape))

    out = pl.pallas_call(
        _ssd_kernel,
        grid=(batch // bpb, nch),
        in_specs=[full((CONV_W, CONV_CH)), full((1, CONV_CH)), full((1, 128)), full((1, 128)),
                  full((1, GROUP_W)), full((1, GROUP_W)),
                  pl.BlockSpec((bpb, L, CONV_CH), lambda b, n: (b, n, 0)),
                  pl.BlockSpec((bpb, L, 128), lambda b, n: (b, n, 0)),
                  pl.BlockSpec((bpb, L, GROUP_W), lambda b, n: (b, n, 3))],
        out_specs=pl.BlockSpec((bpb, L, GROUP_W), lambda b, n: (b, n, 0)),
        out_shape=jax.ShapeDtypeStruct((batch, seq, GROUP_W), BF16),
        scratch_shapes=[pltpu.VMEM((bpb, 8 + L, CONV_CH), F32),
                        pltpu.VMEM((bpb * HEADS, SSM_STATE, HEAD_DIM), F32)],
        compiler_params=_params(("parallel", "arbitrary")),
        name="ssd",
    )(conv_w, conv_b.reshape(1, CONV_CH), dtb, a_full, dsk, norm_w.reshape(1, GROUP_W),
      xbc.reshape(batch, seq, -1), misc.reshape(batch, seq, -1), z_all.reshape(batch, seq, -1))
    return out.reshape(batch * seq, GROUP_W)


def _pick_tile(n, pref):
    t = pref
    while n % t:
        t //= 2
    return t


def kernel(x, norm_w, w_in, w_out, nsa_pe_k, nsa_pe_v, nsa_w_ck1, nsa_w_ck2, nsa_w_cv1, nsa_w_cv2,
           diff_lam_q1, diff_lam_k1, diff_lam_q2, diff_lam_k2, diff_subln_w, ret_gn_w,
           ssm_conv_w, ssm_conv_b, ssm_dt_bias, ssm_A_log, ssm_D, ssm_norm_w, final_norm_w):
    batch, seq, _ = x.shape
    depth = w_in.shape[0]
    m = batch * seq
    tm = _pick_tile(m, 512)
    tq = _pick_tile(seq, 512)
    tk = _pick_tile(seq, 256)
    bpb = _pick_tile(batch, REC_BATCH)
    w_r = _relayout_w_in(w_in)
    w_t = _relayout_w_in_t(w_in)
    w_out_b = w_out.astype(BF16)
    x2d = x.reshape(m, D_MODEL)
    projected = _in_proj(x2d, norm_w[0], w_r[0], w_t[0], tm)
    for i in range(depth):
        nsa_q, nsa_k2, nsa_cmp, misc, z_all, diff_qk, ret_qkv, xbc, vt_all = projected
        kc, vct = _nsa_compress(nsa_cmp, nsa_pe_k[i], nsa_pe_v[i], nsa_w_ck1[i], nsa_w_ck2[i],
                                nsa_w_cv1[i], nsa_w_cv2[i], batch, seq)
        y_nsa = _nsa_attention(nsa_q, nsa_k2, vt_all, kc, vct, misc, z_all, batch, seq, tq, tk)
        lam_p = jnp.stack([diff_lam_q1[i], diff_lam_k1[i], diff_lam_q2[i], diff_lam_k2[i]])
        y_diff = _diff_attention(diff_qk, vt_all, z_all, lam_p, diff_subln_w[i], batch, seq, i, tq, tk)
        y_ret = _retention(ret_qkv, z_all, ret_gn_w[i], batch, seq, bpb)
        y_ssm = _ssd(xbc, misc, z_all, ssm_conv_w[i], ssm_conv_b[i], ssm_dt_bias[i], ssm_A_log[i],
                     ssm_D[i], ssm_norm_w[i], batch, seq, bpb)
        ys = (y_nsa, y_diff, y_ret, y_ssm)
        if i + 1 < depth:
            x2d, *projected = _out_in_proj(ys, w_out_b[i], x2d, norm_w[i + 1], w_r[i + 1], w_t[i + 1], tm)
        else:
            x2d = _out_proj(ys, w_out_b[i], x2d, final_norm_w, tm)
    return x2d.reshape(batch, seq, D_MODEL)
```

```python
import functools
import math

import numpy as np
import jax
import jax.numpy as jnp
from jax import lax
from jax.experimental import pallas as pl
from jax.experimental.pallas import tpu as pltpu

F32 = jnp.float32
BF16 = jnp.bfloat16
NEG_INF = float("-inf")
LOG2E = 1.4426950408889634

D_MODEL = 1024
DEPTH = 4
GROUP_W = 256
HEADS = 4
HEAD_DIM = 64
EPS = 1e-6
CMP_LEN = 32
CMP_STRIDE = 16
CMP_HIDDEN = 256
SLC_BLOCK = 64
SLC_SHIFT = 6
SLC_TOPK = 16
WINDOW = 512
DIFF_QK_DIM = 32
RET_CHUNK = 128
SSM_STATE = 128
SSM_CHUNK = 128
CONV_W = 4
CONV_CH = 768
N_ALIBI_HEADS = 8
LANES = 128
BF16_ROWS = 16
ACC_ROWS = HEAD_DIM + BF16_ROWS
QK_LOOKAHEAD = 6
ALIBI_ROWS = 3
POS_RADIX = 256
MASK_BIG = 2.0 ** 100
BLK_LANE0 = 64
REC_BATCH = 4

IN_LAYOUT = (
    ("nsa_q", 256), ("nsa_k_cmp", 64), ("nsa_v_cmp", 64), ("nsa_k_slc", 64), ("nsa_v_slc", 64),
    ("nsa_k_win", 64), ("nsa_v_win", 64), ("nsa_gate", 12), ("nsa_z", 256),
    ("diff_q", 256), ("diff_k", 256), ("diff_v", 256), ("diff_z", 256),
    ("ret_q", 256), ("ret_k", 256), ("ret_v", 256), ("ret_z", 256),
    ("ssm_z", 256), ("ssm_xbc", 768), ("ssm_dt", 4),
)
IN_OFF = {}
_o = 0
for _n, _w in IN_LAYOUT:
    IN_OFF[_n] = (_o, _w)
    _o += _w
IN_W = _o

GATE_COL = 0
DT_COL = 12
IN_OUTPUTS = (
    ("nsa_q", BF16, ("nsa_q",), 256),
    ("nsa_k2", BF16, ("nsa_k_slc", "nsa_k_win"), 128),
    ("nsa_cmp", F32, ("nsa_k_cmp", "nsa_v_cmp"), 128),
    ("misc", F32, ("nsa_gate", "ssm_dt"), 128),
    ("z_all", F32, ("nsa_z", "diff_z", "ret_z", "ssm_z"), 1024),
    ("diff_qk", BF16, ("diff_q", "diff_k"), 512),
    ("ret_qkv", BF16, ("ret_q", "ret_k", "ret_v"), 768),
    ("xbc", F32, ("ssm_xbc",), 768),
)
IN_T_SRC = ("diff_v", "nsa_v_slc", "nsa_v_win")
IN_T_ROWS = 384
DIFF_VT_BLK = 0
NSA_VT_BLK = 2
IN_SEGS = []
_o = 0
for _n, _dt, _src, _w in IN_OUTPUTS:
    IN_SEGS.append((_o, _o + _w))
    _o += _w
IN_WP = _o

VMEM_LIMIT = 56 * 1024 * 1024


def _alibi_slopes():
    return [2.0 ** (-8.0 * (i + 1) / N_ALIBI_HEADS) for i in range(N_ALIBI_HEADS)]


NSA_SLOPES = _alibi_slopes()[0::2]
DIFF_SLOPES = _alibi_slopes()[1::2]


def _silu(x):
    return x * jax.nn.sigmoid(x)


def _dot(a, b):
    return jnp.dot(a, b, preferred_element_type=F32)


def _dot_nt(a, b):
    return lax.dot_general(a, b, (((1,), (1,)), ((), ())), preferred_element_type=F32)


def _dot_tn(a, b):
    return lax.dot_general(a, b, (((0,), (0,)), ((), ())), preferred_element_type=F32)


def _params(sem):
    return pltpu.CompilerParams(dimension_semantics=sem, vmem_limit_bytes=VMEM_LIMIT)


def _relayout_w_in(w_in):
    cols = []
    for _n, _dt, src, width in IN_OUTPUTS:
        used = 0
        for s in src:
            off, w = IN_OFF[s]
            cols.append(w_in[:, :, off:off + w])
            used += w
        if used < width:
            cols.append(jnp.zeros(w_in.shape[:2] + (width - used,), w_in.dtype))
    return jnp.concatenate(cols, axis=-1).astype(BF16)


def _relayout_w_in_t(w_in):
    cols = [w_in[:, :, IN_OFF[s][0]:IN_OFF[s][0] + IN_OFF[s][1]] for s in IN_T_SRC]
    return jnp.swapaxes(jnp.concatenate(cols, axis=-1), 1, 2).astype(BF16)


def _norm_project(x, nw_ref, w_ref, wt_ref, out_refs):
    ms = jnp.mean(x * x, axis=-1, keepdims=True)
    h = (x * lax.rsqrt(ms + EPS) * nw_ref[...]).astype(BF16)
    for ref, (a, b) in zip(out_refs[:-1], IN_SEGS):
        ref[...] = _dot(h, w_ref[:, a:b]).astype(ref.dtype)
    out_refs[-1][...] = _dot_nt(wt_ref[...], h).astype(BF16)


def _in_proj_kernel(x_ref, nw_ref, w_ref, wt_ref, *out_refs):
    _norm_project(x_ref[...], nw_ref, w_ref, wt_ref, out_refs)


def _in_proj_outputs(m, tm):
    out_shape = [jax.ShapeDtypeStruct((m, b - a), dt) for (_n, dt, _s, _w), (a, b) in zip(IN_OUTPUTS, IN_SEGS)]
    out_specs = [pl.BlockSpec((tm, b - a), lambda i: (i, 0)) for (a, b) in IN_SEGS]
    out_shape.append(jax.ShapeDtypeStruct((IN_T_ROWS, m), BF16))
    out_specs.append(pl.BlockSpec((IN_T_ROWS, tm), lambda i: (0, i)))
    return out_shape, out_specs


def _in_proj(x2d, norm_w, w_r, w_t, tm):
    m = x2d.shape[0]
    out_shape, out_specs = _in_proj_outputs(m, tm)
    return pl.pallas_call(
        _in_proj_kernel,
        grid=(m // tm,),
        in_specs=[pl.BlockSpec((tm, D_MODEL), lambda i: (i, 0)),
                  pl.BlockSpec((1, D_MODEL), lambda i: (0, 0)),
                  pl.BlockSpec((D_MODEL, IN_WP), lambda i: (0, 0)),
                  pl.BlockSpec((IN_T_ROWS, D_MODEL), lambda i: (0, 0))],
        out_specs=out_specs,
        out_shape=out_shape,
        compiler_params=_params(("parallel",)),
        name="in_proj",
    )(x2d, norm_w.reshape(1, D_MODEL), w_r, w_t)


def _out_proj_kernel(y0_ref, y1_ref, y2_ref, y3_ref, w_ref, x_ref, fw_ref, o_ref):
    acc = x_ref[...]
    for g, y_ref in enumerate((y0_ref, y1_ref, y2_ref, y3_ref)):
        acc = acc + _dot(y_ref[...], w_ref[g * GROUP_W:(g + 1) * GROUP_W, :])
    ms = jnp.mean(acc * acc, axis=-1, keepdims=True)
    o_ref[...] = acc * lax.rsqrt(ms + EPS) * fw_ref[...]


def _out_proj(ys, w_out_b, x2d, final_w, tm):
    m = x2d.shape[0]
    yspec = pl.BlockSpec((tm, GROUP_W), lambda i: (i, 0))
    return pl.pallas_call(
        _out_proj_kernel,
        grid=(m // tm,),
        in_specs=[yspec, yspec, yspec, yspec,
                  pl.BlockSpec((D_MODEL, D_MODEL), lambda i: (0, 0)),
                  pl.BlockSpec((tm, D_MODEL), lambda i: (i, 0)),
                  pl.BlockSpec((1, D_MODEL), lambda i: (0, 0))],
        out_specs=pl.BlockSpec((tm, D_MODEL), lambda i: (i, 0)),
        out_shape=jax.ShapeDtypeStruct((m, D_MODEL), F32),
        compiler_params=_params(("parallel",)),
        name="out_proj",
    )(*ys, w_out_b, x2d, final_w.reshape(1, D_MODEL))


def _out_in_proj_kernel(y0_ref, y1_ref, y2_ref, y3_ref, wo_ref, x_ref, nw_ref, w_ref, wt_ref, xo_ref, *out_refs):
    acc = x_ref[...]
    for g, y_ref in enumerate((y0_ref, y1_ref, y2_ref, y3_ref)):
        acc = acc + _dot(y_ref[...], wo_ref[g * GROUP_W:(g + 1) * GROUP_W, :])
    xo_ref[...] = acc
    _norm_project(acc, nw_ref, w_ref, wt_ref, out_refs)


def _out_in_proj(ys, w_out_b, x2d, norm_w, w_r, w_t, tm):
    m = x2d.shape[0]
    yspec = pl.BlockSpec((tm, GROUP_W), lambda i: (i, 0))
    xspec = pl.BlockSpec((tm, D_MODEL), lambda i: (i, 0))
    out_shape, out_specs = _in_proj_outputs(m, tm)
    return pl.pallas_call(
        _out_in_proj_kernel,
        grid=(m // tm,),
        in_specs=[yspec, yspec, yspec, yspec,
                  pl.BlockSpec((D_MODEL, D_MODEL), lambda i: (0, 0)),
                  xspec,
                  pl.BlockSpec((1, D_MODEL), lambda i: (0, 0)),
                  pl.BlockSpec((D_MODEL, IN_WP), lambda i: (0, 0)),
                  pl.BlockSpec((IN_T_ROWS, D_MODEL), lambda i: (0, 0))],
        out_specs=[xspec] + out_specs,
        out_shape=[jax.ShapeDtypeStruct((m, D_MODEL), F32)] + out_shape,
        compiler_params=_params(("parallel",)),
        name="out_in_proj",
    )(*ys, w_out_b, x2d, norm_w.reshape(1, D_MODEL), w_r, w_t)


def _flash_update_t(s, c1, shift, vt_ext, m_ref, acc_ref, idx, cols):
    m_old = m_ref[idx, :, cols]
    m_new = jnp.maximum(m_old, c1 * jnp.max(s(), axis=0, keepdims=True) + shift)
    alpha = jnp.exp2(m_old - m_new)
    p = jnp.exp2(c1 * s() - (m_new - shift))
    acc_ref[idx, :, cols] = alpha * acc_ref[idx, :, cols] + _dot(vt_ext, p.astype(BF16))
    m_ref[idx, :, cols] = m_new


def _bf16_pieces(x, n=3):
    out = []
    for _ in range(n):
        p = float(np.asarray(x, np.float32).astype(BF16).astype(np.float32))
        out.append(p)
        x = x - p
    return out


def _key_pos_features(tk):
    lane = lax.broadcasted_iota(jnp.int32, (tk, LANES), 1)
    row = lax.broadcasted_iota(jnp.int32, (tk, LANES), 0)
    out = jnp.zeros((tk, LANES), jnp.int32)
    for d in range(-(-tk // POS_RADIX)):
        digit = jnp.bitwise_and(jnp.right_shift(row, 8 * d), POS_RADIX - 1)
        out = jnp.where((lane >= ALIBI_ROWS * d) & (lane < ALIBI_ROWS * (d + 1)), digit, out)
    return out.astype(F32).astype(BF16)


def _alibi_rows(beta, tq, rows=LANES, tk=POS_RADIX):
    row = lax.broadcasted_iota(jnp.int32, (rows, tq), 0)
    out = jnp.zeros((rows, tq), F32)
    for d in range(-(-tk // POS_RADIX)):
        for r, piece in enumerate(_bf16_pieces(beta * POS_RADIX ** d, ALIBI_ROWS)):
            out = jnp.where(row == ALIBI_ROWS * d + r, piece, out)
    return out.astype(BF16)


def _normalized(acc):
    return acc[0:HEAD_DIM] / jnp.maximum(acc[HEAD_DIM:HEAD_DIM + 1], 1e-30)


def _rel_t(tk, tq):
    return lax.broadcasted_iota(jnp.int32, (tk, tq), 1) - lax.broadcasted_iota(jnp.int32, (tk, tq), 0)


def _diff_kernel(lam_ref, sw_ref, q_ref, k_ref, vt_ref, z_ref, o_ref, m_ref, acc_ref, wq_ref, s_ref,
                 *, tq, tk, lam_init):
    qi = pl.program_id(1)
    kpq = tq // tk
    scale = DIFF_QK_DIM ** -0.5
    c1 = scale * LOG2E
    m_ref[...] = jnp.full(m_ref.shape, NEG_INF, F32)
    acc_ref[...] = jnp.zeros(acc_ref.shape, F32)
    rel = _rel_t(tk, tq)

    qt = q_ref[...].astype(F32).T.astype(BF16)
    row = lax.broadcasted_iota(jnp.int32, (LANES, tq), 0)
    for j in range(2 * HEADS):
        g, r0 = divmod(j * DIFF_QK_DIM, LANES)
        qg = qt[g * LANES:(g + 1) * LANES]
        wq_ref[j, 0:LANES, :] = jnp.where((row >= r0) & (row < r0 + DIFF_QK_DIM), qg, jnp.zeros_like(qg))
        wq_ref[j, LANES:2 * LANES, :] = _alibi_rows(DIFF_SLOPES[j // 2] / scale, tq, tk=tk)

    ones = jnp.ones((BF16_ROWS, tk), BF16)
    kpos = _key_pos_features(tk)

    n_maps = 2 * HEADS
    n_slots = s_ref.shape[0]

    def tiles(kis, mask_offs):
        loaded = []
        for ki in kis:
            start = pl.multiple_of(ki * tk, tk)
            loaded.append((k_ref[pl.ds(start, tk), :],
                           vt_ref[:, pl.ds(start, tk)],
                           (ki * tk).astype(F32)))
        items = [(t, j) for t in range(len(kis)) for j in range(n_maps)]
        cols = [slice(0, tq)] * len(kis)

        def scores(i):
            t, j = items[i]
            g = (j * DIFF_QK_DIM) // LANES
            lhs = jnp.concatenate([loaded[t][0][:, g * LANES:(g + 1) * LANES], kpos], axis=1)
            s_ref[i % n_slots, :, cols[t]] = _dot(lhs, wq_ref[j, :, cols[t]])

        def staged(i):
            t = items[i][0]
            if mask_offs[t] is None:
                return lambda: s_ref[i % n_slots, :, cols[t]]
            return lambda: jnp.where(rel[:, cols[t]] >= mask_offs[t], s_ref[i % n_slots, :, cols[t]], NEG_INF)

        for i in range(QK_LOOKAHEAD):
            scores(i)
        for i, (t, j) in enumerate(items):
            h = j // 2
            if i + QK_LOOKAHEAD < len(items):
                scores(i + QK_LOOKAHEAD)
            _k, vt, key0 = loaded[t]
            vt_ext = jnp.concatenate([vt[h * HEAD_DIM:(h + 1) * HEAD_DIM], ones], axis=0)
            _flash_update_t(staged(i), c1, (DIFF_SLOPES[h] * LOG2E) * key0, vt_ext, m_ref, acc_ref, j, cols[t])

    def body(i, carry):
        tiles([i * kpq + d for d in range(kpq)], [None] * kpq)
        return carry

    lax.fori_loop(0, qi, body, 0)
    tiles([qi * kpq + d for d in range(kpq)], [d * tk for d in range(kpq)])

    lp = lam_ref[...]
    lam = (jnp.exp(jnp.sum(lp[0:1] * lp[1:2], axis=-1, keepdims=True))
           - jnp.exp(jnp.sum(lp[2:3] * lp[3:4], axis=-1, keepdims=True)) + lam_init)
    sw = sw_ref[...]
    o_t = jnp.concatenate([_normalized(acc_ref[2 * h]) - lam * _normalized(acc_ref[2 * h + 1])
                           for h in range(HEADS)], axis=0)
    o = o_t.T
    outs = []
    for h in range(HEADS):
        oh = o[:, h * HEAD_DIM:(h + 1) * HEAD_DIM]
        ms = jnp.mean(oh * oh, axis=-1, keepdims=True)
        outs.append(oh * lax.rsqrt(ms + EPS) * sw * (1.0 - lam_init))
    y = jnp.concatenate(outs, axis=-1) * _silu(z_ref[...])
    o_ref[...] = y.astype(o_ref.dtype)


def _diff_attention(qk, vt_all, z_all, lam_p, subln_w, batch, seq, layer_idx, tq, tk):
    nq = seq // tq
    lam_init = 0.8 - 0.6 * math.exp(-0.3 * layer_idx)
    return pl.pallas_call(
        functools.partial(_diff_kernel, tq=tq, tk=tk, lam_init=lam_init),
        grid=(batch, nq),
        in_specs=[pl.BlockSpec((4, DIFF_QK_DIM), lambda b, i: (0, 0)),
                  pl.BlockSpec((1, HEAD_DIM), lambda b, i: (0, 0)),
                  pl.BlockSpec((tq, GROUP_W), lambda b, i: (b * nq + i, 0)),
                  pl.BlockSpec((seq, GROUP_W), lambda b, i: (b, 1)),
                  pl.BlockSpec((GROUP_W, seq), lambda b, i: (DIFF_VT_BLK, b)),
                  pl.BlockSpec((tq, GROUP_W), lambda b, i: (b * nq + i, 1))],
        out_specs=pl.BlockSpec((tq, GROUP_W), lambda b, i: (b * nq + i, 0)),
        out_shape=jax.ShapeDtypeStruct((batch * seq, GROUP_W), BF16),
        scratch_shapes=[pltpu.VMEM((2 * HEADS, 1, tq), F32),
                        pltpu.VMEM((2 * HEADS, ACC_ROWS, tq), F32),
                        pltpu.VMEM((2 * HEADS, 2 * LANES, tq), BF16),
                        pltpu.VMEM((QK_LOOKAHEAD + 1, tk, tq), F32)],
        compiler_params=_params(("parallel", "parallel")),
        name="diff_attention",
    )(lam_p, subln_w.reshape(1, HEAD_DIM), qk, qk, vt_all, z_all)


def _nsa_compress_kernel(gk_ref, gv_ref, pek_ref, pev_ref, wk1_ref, wk2_ref, wv1_ref, wv2t_ref, kc_ref, vct_ref):
    half = CMP_STRIDE * HEAD_DIM

    def hidden(g_ref, pe_ref, w1_ref):
        g = g_ref[0]
        top = (g + pe_ref[0:1, :]).astype(BF16)
        bot = (g + pe_ref[1:2, :]).astype(BF16)
        a = _dot(top, w1_ref[0:half, :])
        b = _dot(bot, w1_ref[half:2 * half, :])
        nrow = b.shape[0]
        b = pltpu.roll(b, nrow - 1, 0)
        return _silu(a + b).astype(BF16)

    hk = hidden(gk_ref, pek_ref, wk1_ref)
    kc = _dot(hk, wk2_ref[...])
    lane = lax.broadcasted_iota(jnp.int32, kc.shape, 1)
    blk = lax.broadcasted_iota(jnp.int32, kc.shape, 0).astype(F32)
    kc = jnp.where((lane >= HEAD_DIM) & (lane < HEAD_DIM + ALIBI_ROWS), blk, kc)
    kc_ref[0] = kc.astype(kc_ref.dtype)
    hv = hidden(gv_ref, pev_ref, wv1_ref)
    vct_ref[0] = _dot_nt(wv2t_ref[...], hv).astype(vct_ref.dtype)


def _nsa_compress(cmp2d, pe_k, pe_v, w_ck1, w_ck2, w_cv1, w_cv2, batch, seq):
    ng = seq // CMP_STRIDE
    half = CMP_STRIDE * HEAD_DIM
    gk = cmp2d[:, :HEAD_DIM].reshape(batch, ng, half)
    gv = cmp2d[:, HEAD_DIM:].reshape(batch, ng, half)
    gspec = pl.BlockSpec((1, ng, half), lambda b: (b, 0, 0))

    def full(shape):
        return pl.BlockSpec(shape, lambda b: (0,) * len(shape))

    return pl.pallas_call(
        _nsa_compress_kernel,
        grid=(batch,),
        in_specs=[gspec, gspec, full((2, half)), full((2, half)),
                  full((2 * half, CMP_HIDDEN)), full((CMP_HIDDEN, LANES)),
                  full((2 * half, CMP_HIDDEN)), full((HEAD_DIM, CMP_HIDDEN))],
        out_specs=[pl.BlockSpec((1, ng, LANES), lambda b: (b, 0, 0)),
                   pl.BlockSpec((1, HEAD_DIM, ng), lambda b: (b, 0, 0))],
        out_shape=[jax.ShapeDtypeStruct((batch, ng, LANES), BF16),
                   jax.ShapeDtypeStruct((batch, HEAD_DIM, ng), BF16)],
        compiler_params=_params(("parallel",)),
        name="nsa_compress",
    )(gk, gv, pe_k.reshape(2, half), pe_v.reshape(2, half),
      w_ck1.astype(BF16), jnp.pad(w_ck2, ((0, 0), (0, LANES - HEAD_DIM))).astype(BF16),
      w_cv1.astype(BF16), w_cv2.T.astype(BF16))


def _nsa_kernel(ovl_ref, q_ref, k_ref, vt_ref, kc_ref, vct_ref, misc_ref, z_ref, o_ref,
                m_ref, acc_ref, wq_ref, s_ref, *, tq, tk, seq):
    qi = pl.program_id(1)
    kpq = tq // tk
    scale = HEAD_DIM ** -0.5
    c1 = scale * LOG2E
    ng = seq // CMP_STRIDE
    ns = seq // SLC_BLOCK
    top = min(SLC_TOPK, ns)
    m_ref[...] = jnp.full(m_ref.shape, NEG_INF, F32)
    acc_ref[...] = jnp.zeros(acc_ref.shape, F32)
    rel = _rel_t(tk, tq)

    qt = q_ref[...].astype(F32).T.astype(BF16)
    zeros_q = jnp.zeros((HEAD_DIM, tq), BF16)
    for h in range(HEADS):
        qh = qt[h * HEAD_DIM:(h + 1) * HEAD_DIM]
        alibi = _alibi_rows(NSA_SLOPES[h] / scale, tq, tk=tk)
        wq_ref[h, 0:LANES, :] = jnp.concatenate([qh, zeros_q], axis=0)
        wq_ref[h, LANES:2 * LANES, :] = alibi
        wq_ref[HEADS + h, 0:LANES, :] = jnp.concatenate([zeros_q, qh], axis=0)
        wq_ref[HEADS + h, LANES:2 * LANES, :] = alibi

    t_lane = qi * tq + lax.broadcasted_iota(jnp.int32, (1, tq), 1)
    n_sub = lax.broadcasted_iota(jnp.int32, (ng, 1), 0)
    c_valid = (n_sub * CMP_STRIDE + (CMP_LEN - 1)) <= t_lane
    kc = kc_ref[0]
    vct = vct_ref[0]
    ovl = ovl_ref[...]
    imp = jnp.zeros((ns, tq), F32)
    o_cmp = []
    for h in range(HEADS):
        rhs = jnp.concatenate([qt[h * HEAD_DIM:(h + 1) * HEAD_DIM],
                               _alibi_rows(NSA_SLOPES[h] * CMP_STRIDE / scale, tq, HEAD_DIM)], axis=0)
        s = jnp.where(c_valid, _dot(kc, rhs), NEG_INF)
        mx = jnp.max(s, axis=0, keepdims=True)
        mx = jnp.where(mx > NEG_INF, mx, 0.0)
        p = jnp.exp2(c1 * s - c1 * mx)
        p = p * (1.0 / jnp.maximum(jnp.sum(p, axis=0, keepdims=True), 1e-30))
        pb = p.astype(BF16)
        o_cmp.append(_dot(vct, pb))
        imp = imp + _dot(ovl, pb)

    def select_blocks():
        j_sub = lax.broadcasted_iota(jnp.int32, (ns, 1), 0)
        j_sub_f = j_sub.astype(F32)
        cur = jnp.right_shift(t_lane, SLC_SHIFT)
        forced = (j_sub == 0) | (j_sub == cur) | (j_sub == cur - 1)
        valid = (j_sub * SLC_BLOCK) <= t_lane
        score = jnp.where(forced, 1e30, jnp.where(valid, imp, -1.0))
        sel = jnp.zeros((ns, tq), F32)
        for _ in range(top):
            mx = jnp.max(score, axis=0, keepdims=True)
            idx = jnp.min(jnp.where(score == mx, j_sub_f, float(ns)), axis=0, keepdims=True)
            pick = j_sub_f == idx
            sel = jnp.where(pick, 1.0, sel)
            score = jnp.where(pick, -2.0, score)
        unsel = jnp.where(sel > 0.5, 0.0, -MASK_BIG).astype(BF16)
        for h in range(HEADS):
            wq_ref[h, LANES + BLK_LANE0:LANES + BLK_LANE0 + ns, :] = unsel

    aux_lane = lax.broadcasted_iota(jnp.int32, (tk, LANES), 1)
    aux_blk = jnp.right_shift(lax.broadcasted_iota(jnp.int32, (tk, LANES), 0), SLC_SHIFT) + BLK_LANE0
    kpos = _key_pos_features(tk)
    ones = jnp.ones((BF16_ROWS, tk), BF16)

    n_slots = s_ref.shape[0]

    def tiles(window, kis, cols, mask_offs=None):
        base = HEADS if window else 0
        loaded = []
        for t, ki in enumerate(kis):
            start = pl.multiple_of(ki * tk, tk)
            k = k_ref[pl.ds(start, tk), :]
            if window:
                vt = vt_ref[HEAD_DIM:2 * HEAD_DIM, pl.ds(start, tk)]
                aux = kpos
                dist = rel[:, cols[t]] + (qi * tq - ki * tk)
                keep = (dist >= 0) & (dist < WINDOW)
            else:
                vt = vt_ref[0:HEAD_DIM, pl.ds(start, tk)]
                onehot = aux_lane == aux_blk + ki * (tk // SLC_BLOCK)
                aux = jnp.where(onehot, jnp.ones_like(kpos), kpos)
                keep = None
            loaded.append((jnp.concatenate([k, aux], axis=1),
                           jnp.concatenate([vt, ones], axis=0), (ki * tk).astype(F32), keep))
        items = [(t, h) for t in range(len(kis)) for h in range(HEADS)]

        def scores(i):
            t, h = items[i]
            s_ref[i % n_slots, :, cols[t]] = _dot(loaded[t][0], wq_ref[base + h, :, cols[t]])

        def staged(i):
            t = items[i][0]
            if window:
                return lambda: jnp.where(loaded[t][3], s_ref[i % n_slots, :, cols[t]], NEG_INF)
            if mask_offs[t] is None:
                return lambda: s_ref[i % n_slots, :, cols[t]]
            return lambda: jnp.where(rel[:, cols[t]] >= mask_offs[t], s_ref[i % n_slots, :, cols[t]], NEG_INF)

        for i in range(QK_LOOKAHEAD):
            scores(i)
        for i, (t, h) in enumerate(items):
            if i + QK_LOOKAHEAD < len(items):
                scores(i + QK_LOOKAHEAD)
            _lhs, vt_ext, key0, _keep = loaded[t]
            _flash_update_t(staged(i), c1, (NSA_SLOPES[h] * LOG2E) * key0, vt_ext, m_ref, acc_ref, base + h,
                            cols[t])

    all_cols = slice(0, tq)
    diag_cols = [slice(d * tk, tq) for d in range(kpq)]

    def body(i, carry):
        tiles(False, [i * kpq + d for d in range(kpq)], [all_cols] * kpq, [None] * kpq)
        return carry

    tiles(True, [qi * kpq + d for d in range(kpq)], diag_cols)
    select_blocks()

    lax.fori_loop(0, qi, body, 0)
    tiles(False, [qi * kpq + d for d in range(kpq)], diag_cols, [d * tk for d in range(kpq)])

    n_back = (WINDOW + tk - 1) // tk
    for g in range((n_back + kpq - 1) // kpq):
        backs = list(range(g * kpq + 1, min((g + 1) * kpq, n_back) + 1))
        back_cols = [slice(0, min(tq, -(-(WINDOW - (back - 1) * tk - 1) // LANES) * LANES)) for back in backs]

        @pl.when(qi > g)
        def _():
            tiles(True, [qi * kpq - back for back in backs], back_cols)

    g_t = jax.nn.sigmoid(misc_ref[...]).T
    outs = []
    for h in range(HEADS):
        r0 = GATE_COL + 3 * h
        outs.append(g_t[r0:r0 + 1] * o_cmp[h] + g_t[r0 + 1:r0 + 2] * _normalized(acc_ref[h])
                    + g_t[r0 + 2:r0 + 3] * _normalized(acc_ref[HEADS + h]))
    y = jnp.concatenate(outs, axis=0).T * _silu(z_ref[...])
    o_ref[...] = y.astype(o_ref.dtype)


def _overlap_t(seq):
    nc = (seq - CMP_LEN) // CMP_STRIDE + 1
    ng = seq // CMP_STRIDE
    ns = seq // SLC_BLOCK
    c_start = np.arange(ng) * CMP_STRIDE
    c_end = c_start + CMP_LEN - 1
    s_start = np.arange(ns) * SLC_BLOCK
    s_end = s_start + SLC_BLOCK - 1
    ov = (c_start[None, :] <= s_end[:, None]) & (c_end[None, :] >= s_start[:, None]) & (np.arange(ng)[None, :] < nc)
    return jnp.asarray(ov.astype(np.float32), dtype=BF16)


def _nsa_attention(q, k2, vt_all, kc, vct, misc, z_all, batch, seq, tq, tk):
    nq = seq // tq
    ng = seq // CMP_STRIDE
    ns = seq // SLC_BLOCK
    assert tq - tk < WINDOW and tk % SLC_BLOCK == 0 and ns <= LANES - BLK_LANE0
    return pl.pallas_call(
        functools.partial(_nsa_kernel, tq=tq, tk=tk, seq=seq),
        grid=(batch, nq),
        in_specs=[pl.BlockSpec((ns, ng), lambda b, i: (0, 0)),
                  pl.BlockSpec((tq, GROUP_W), lambda b, i: (b * nq + i, 0)),
                  pl.BlockSpec((seq, LANES), lambda b, i: (b, 0)),
                  pl.BlockSpec((2 * HEAD_DIM, seq), lambda b, i: (NSA_VT_BLK, b)),
                  pl.BlockSpec((1, ng, LANES), lambda b, i: (b, 0, 0)),
                  pl.BlockSpec((1, HEAD_DIM, ng), lambda b, i: (b, 0, 0)),
                  pl.BlockSpec((tq, LANES), lambda b, i: (b * nq + i, 0)),
                  pl.BlockSpec((tq, GROUP_W), lambda b, i: (b * nq + i, 0))],
        out_specs=pl.BlockSpec((tq, GROUP_W), lambda b, i: (b * nq + i, 0)),
        out_shape=jax.ShapeDtypeStruct((batch * seq, GROUP_W), BF16),
        scratch_shapes=[pltpu.VMEM((2 * HEADS, 1, tq), F32),
                        pltpu.VMEM((2 * HEADS, ACC_ROWS, tq), F32),
                        pltpu.VMEM((2 * HEADS, 2 * LANES, tq), BF16),
                        pltpu.VMEM((2 * HEADS, tk, tq), F32)],
        compiler_params=_params(("parallel", "parallel")),
        name="nsa_attention",
    )(_overlap_t(seq), q, k2, vt_all, kc, vct, misc, z_all)


def _ret_tables():
    c = RET_CHUNK
    h = np.arange(HEADS, dtype=np.float32)
    log_g = jnp.log(1.0 - 2.0 ** (-5.0 - jnp.asarray(h)))
    pos = jnp.arange(c, dtype=F32)
    rel = pos[:, None] - pos[None, :]
    decay = jnp.where(rel >= 0, jnp.exp(log_g[:, None, None] * jnp.maximum(rel, 0.0)), 0.0)
    xi = jnp.exp(log_g[:, None] * (pos + 1.0))
    zeta = jnp.exp(log_g[:, None] * (c - 1.0 - pos))
    chunk_decay = jnp.exp(log_g * c)
    xi_tab = jnp.repeat(xi.T, HEAD_DIM, axis=1)
    zeta_tab = jnp.repeat(zeta.T, HEAD_DIM, axis=1)
    cd_tab = jnp.repeat(chunk_decay, HEAD_DIM)[None, :]
    return decay, xi_tab, zeta_tab, cd_tab


def _ret_kernel(decay_ref, xi_ref, zeta_ref, cd_ref, gn_ref, q_ref, k_ref, v_ref, z_ref, o_ref, st_ref):
    n = pl.program_id(1)

    @pl.when(n == 0)
    def _():
        st_ref[...] = jnp.zeros(st_ref.shape, F32)

    xi = xi_ref[...]
    cd = cd_ref[...]
    for bb in range(q_ref.shape[0]):
        q = (q_ref[bb].astype(F32) * (HEAD_DIM ** -0.5)).astype(BF16)
        k = k_ref[bb]
        v = v_ref[bb]
        kz_t = (k.astype(F32) * zeta_ref[...]).T.astype(BF16)
        outs = []
        for h in range(HEADS):
            sl = slice(h * HEAD_DIM, (h + 1) * HEAD_DIM)
            qh, kh, vh = q[:, sl], k[:, sl], v[:, sl]
            prev = st_ref[bb * HEADS + h]
            inner = (_dot_nt(qh, kh) * decay_ref[h]).astype(BF16)
            o = _dot(inner, vh) + _dot(qh, prev.astype(BF16)) * xi[:, sl]
            st_ref[bb * HEADS + h] = prev * cd[:, sl] + _dot(kz_t[sl, :], vh)
            mu = jnp.mean(o, axis=-1, keepdims=True)
            d = o - mu
            var = jnp.mean(d * d, axis=-1, keepdims=True)
            outs.append(d * lax.rsqrt(var + EPS))
        y = jnp.concatenate(outs, axis=-1) * gn_ref[...] * _silu(z_ref[bb])
        o_ref[bb] = y.astype(o_ref.dtype)


def _retention(qkv, z_all, gn_w, batch, seq, bpb):
    c = RET_CHUNK
    nch = seq // c
    decay, xi_tab, zeta_tab, cd_tab = _ret_tables()

    def full(shape):
        return pl.BlockSpec(shape, lambda b, n: (0,) * len(shape))

    def blk(col):
        return pl.BlockSpec((bpb, c, GROUP_W), lambda b, n: (b, n, col))

    out = pl.pallas_call(
        _ret_kernel,
        grid=(batch // bpb, nch),
        in_specs=[full((HEADS, c, c)), full((c, GROUP_W)), full((c, GROUP_W)), full((1, GROUP_W)),
                  full((1, GROUP_W)), blk(0), blk(1), blk(2), blk(2)],
        out_specs=blk(0),
        out_shape=jax.ShapeDtypeStruct((batch, seq, GROUP_W), BF16),
        scratch_shapes=[pltpu.VMEM((bpb * HEADS, HEAD_DIM, HEAD_DIM), F32)],
        compiler_params=_params(("parallel", "arbitrary")),
        name="retention",
    )(decay, xi_tab, zeta_tab, cd_tab, gn_w.reshape(1, GROUP_W),
      *(qkv.reshape(batch, seq, -1),) * 3, z_all.reshape(batch, seq, -1))
    return out.reshape(batch * seq, GROUP_W)


def _ssd_kernel(cw_ref, cb_ref, dtb_ref, a_ref, dsk_ref, nw_ref, xbc_ref, misc_ref, z_ref, o_ref,
                ext_ref, st_ref):
    n = pl.program_id(1)
    L = SSM_CHUNK
    hi = lax.Precision.HIGHEST

    @pl.when(n == 0)
    def _():
        st_ref[...] = jnp.zeros(st_ref.shape, F32)
        ext_ref[:, 0:8, :] = jnp.zeros((ext_ref.shape[0], 8, CONV_CH), F32)

    row = lax.broadcasted_iota(jnp.int32, (L, L), 0)
    col = lax.broadcasted_iota(jnp.int32, (L, L), 1)
    causal = row >= col
    tril = jnp.where(causal, 1.0, 0.0).astype(F32)
    dsk = dsk_ref[...]

    for bb in range(xbc_ref.shape[0]):
        raw = xbc_ref[bb]
        ext_ref[bb, 8:8 + L, :] = raw
        conv = cb_ref[...] + raw * cw_ref[CONV_W - 1:CONV_W, :]
        for w in range(CONV_W - 1):
            shift = CONV_W - 1 - w
            conv = conv + ext_ref[bb, 8 - shift:8 - shift + L, :] * cw_ref[w:w + 1, :]
        ext_ref[bb, 0:8, :] = raw[L - 8:L, :]
        xc = _silu(conv)
        x = xc[:, 0:GROUP_W]
        bm = xc[:, GROUP_W:GROUP_W + 2 * SSM_STATE].astype(BF16)
        cm = xc[:, GROUP_W + 2 * SSM_STATE:].astype(BF16)

        dt_full = jax.nn.softplus(misc_ref[bb] + dtb_ref[...])
        da = dt_full * a_ref[...]
        cs_col = jnp.dot(tril, da, precision=hi, preferred_element_type=F32)
        cs_row = lax.dot_general(da, tril, (((0,), (1,)), ((), ())), precision=hi,
                                 preferred_element_type=F32)

        outs = []
        for h in range(HEADS):
            g = h // 2
            c0 = DT_COL + h
            sl = slice(h * HEAD_DIM, (h + 1) * HEAD_DIM)
            gs = slice(g * SSM_STATE, (g + 1) * SSM_STATE)
            cs_c = cs_col[:, c0:c0 + 1]
            cs_r = cs_row[c0:c0 + 1, :]
            cs_last = cs_col[L - 1:L, c0:c0 + 1]
            xh = x[:, sl]
            xdt = xh * dt_full[:, c0:c0 + 1]
            seg = jnp.exp(jnp.where(causal, cs_c - cs_r, NEG_INF))
            cb = _dot_nt(cm[:, gs], bm[:, gs])
            y = _dot((cb * seg).astype(BF16), xdt.astype(BF16))
            prev = st_ref[bb * HEADS + h]
            y = y + _dot(cm[:, gs], prev.astype(BF16)) * jnp.exp(cs_c)
            y = y + dsk[:, sl] * xh
            dec = jnp.exp(cs_last - cs_c)
            st_ref[bb * HEADS + h] = prev * jnp.exp(cs_last) + _dot_tn(bm[:, gs], (xdt * dec).astype(BF16))
            outs.append(y)
        y = jnp.concatenate(outs, axis=-1) * _silu(z_ref[bb])
        ms = jnp.mean(y * y, axis=-1, keepdims=True)
        o_ref[bb] = (y * lax.rsqrt(ms + EPS) * nw_ref[...]).astype(o_ref.dtype)


def _ssd(xbc, misc, z_all, conv_w, conv_b, dt_bias, a_log, d_skip, norm_w, batch, seq, bpb):
    L = SSM_CHUNK
    nch = seq // L
    dtb = jnp.zeros((1, 128), F32).at[0, DT_COL:DT_COL + HEADS].set(dt_bias)
    a_full = jnp.zeros((1, 128), F32).at[0, DT_COL:DT_COL + HEADS].set(-jnp.exp(a_log))
    dsk = jnp.repeat(d_skip, HEAD_DIM)[None, :]

    def full(shape):
        return pl.BlockSpec(shape, lambda b, n: (0,) * len(shape))

    out = pl.pallas_call(
        _ssd_kernel,
        grid=(batch // bpb, nch),
        in_specs=[full((CONV_W, CONV_CH)), full((1, CONV_CH)), full((1, 128)), full((1, 128)),
                  full((1, GROUP_W)), full((1, GROUP_W)),
                  pl.BlockSpec((bpb, L, CONV_CH), lambda b, n: (b, n, 0)),
                  pl.BlockSpec((bpb, L, 128), lambda b, n: (b, n, 0)),
                  pl.BlockSpec((bpb, L, GROUP_W), lambda b, n: (b, n, 3))],
        out_specs=pl.BlockSpec((bpb, L, GROUP_W), lambda b, n: (b, n, 0)),
        out_shape=jax.ShapeDtypeStruct((batch, seq, GROUP_W), BF16),
        scratch_shapes=[pltpu.VMEM((bpb, 8 + L, CONV_CH), F32),
                        pltpu.VMEM((bpb * HEADS, SSM_STATE, HEAD_DIM), F32)],
        compiler_params=_params(("parallel", "arbitrary")),
        name="ssd",
    )(conv_w, conv_b.reshape(1, CONV_CH), dtb, a_full, dsk, norm_w.reshape(1, GROUP_W),
      xbc.reshape(batch, seq, -1), misc.reshape(batch, seq, -1), z_all.reshape(batch, seq, -1))
    return out.reshape(batch * seq, GROUP_W)


def _pick_tile(n, pref):
    t = pref
    while n % t:
        t //= 2
    return t


def kernel(x, norm_w, w_in, w_out, nsa_pe_k, nsa_pe_v, nsa_w_ck1, nsa_w_ck2, nsa_w_cv1, nsa_w_cv2,
           diff_lam_q1, diff_lam_k1, diff_lam_q2, diff_lam_k2, diff_subln_w, ret_gn_w,
           ssm_conv_w, ssm_conv_b, ssm_dt_bias, ssm_A_log, ssm_D, ssm_norm_w, final_norm_w):
    batch, seq, _ = x.shape
    depth = w_in.shape[0]
    m = batch * seq
    tm = _pick_tile(m, 512)
    tq = _pick_tile(seq, 512)
    tk = _pick_tile(seq, 256)
    tk_diff = _pick_tile(seq, 512)
    bpb = _pick_tile(batch, REC_BATCH)
    w_r = _relayout_w_in(w_in)
    w_t = _relayout_w_in_t(w_in)
    w_out_b = w_out.astype(BF16)
    x2d = x.reshape(m, D_MODEL)
    projected = _in_proj(x2d, norm_w[0], w_r[0], w_t[0], tm)
    for i in range(depth):
        nsa_q, nsa_k2, nsa_cmp, misc, z_all, diff_qk, ret_qkv, xbc, vt_all = projected
        kc, vct = _nsa_compress(nsa_cmp, nsa_pe_k[i], nsa_pe_v[i], nsa_w_ck1[i], nsa_w_ck2[i],
                                nsa_w_cv1[i], nsa_w_cv2[i], batch, seq)
        y_nsa = _nsa_attention(nsa_q, nsa_k2, vt_all, kc, vct, misc, z_all, batch, seq, tq, tk)
        lam_p = jnp.stack([diff_lam_q1[i], diff_lam_k1[i], diff_lam_q2[i], diff_lam_k2[i]])
        y_diff = _diff_attention(diff_qk, vt_all, z_all, lam_p, diff_subln_w[i], batch, seq, i, tq, tk_diff)
        y_ret = _retention(ret_qkv, z_all, ret_gn_w[i], batch, seq, bpb)
        y_ssm = _ssd(xbc, misc, z_all, ssm_conv_w[i], ssm_conv_b[i], ssm_dt_bias[i], ssm_A_log[i],
                     ssm_D[i], ssm_norm_w[i], batch, seq, bpb)
        ys = (y_nsa, y_diff, y_ret, y_ssm)
        if i + 1 < depth:
            x2d, *projected = _out_in_proj(ys, w_out_b[i], x2d, norm_w[i + 1], w_r[i + 1], w_t[i + 1], tm)
        else:
            x2d = _out_proj(ys, w_out_b[i], x2d, final_norm_w, tm)
    return x2d.reshape(batch, seq, D_MODEL)
```

```python
import functools
import math

import numpy as np
import jax
import jax.numpy as jnp
from jax import lax
from jax.experimental import pallas as pl
from jax.experimental.pallas import tpu as pltpu

F32 = jnp.float32
BF16 = jnp.bfloat16
NEG_INF = float("-inf")
LOG2E = 1.4426950408889634

D_MODEL = 1024
DEPTH = 4
GROUP_W = 256
HEADS = 4
HEAD_DIM = 64
EPS = 1e-6
CMP_LEN = 32
CMP_STRIDE = 16
CMP_HIDDEN = 256
SLC_BLOCK = 64
SLC_SHIFT = 6
SLC_TOPK = 16
WINDOW = 512
DIFF_QK_DIM = 32
RET_CHUNK = 128
SSM_STATE = 128
SSM_CHUNK = 128
CONV_W = 4
CONV_CH = 768
N_ALIBI_HEADS = 8
LANES = 128
BF16_ROWS = 16
ACC_ROWS = HEAD_DIM + BF16_ROWS
QK_LOOKAHEAD = 6
ALIBI_ROWS = 3
POS_RADIX = 256
MASK_BIG = 2.0 ** 100
BLK_LANE0 = 64
REC_BATCH = 4

IN_LAYOUT = (
    ("nsa_q", 256), ("nsa_k_cmp", 64), ("nsa_v_cmp", 64), ("nsa_k_slc", 64), ("nsa_v_slc", 64),
    ("nsa_k_win", 64), ("nsa_v_win", 64), ("nsa_gate", 12), ("nsa_z", 256),
    ("diff_q", 256), ("diff_k", 256), ("diff_v", 256), ("diff_z", 256),
    ("ret_q", 256), ("ret_k", 256), ("ret_v", 256), ("ret_z", 256),
    ("ssm_z", 256), ("ssm_xbc", 768), ("ssm_dt", 4),
)
IN_OFF = {}
_o = 0
for _n, _w in IN_LAYOUT:
    IN_OFF[_n] = (_o, _w)
    _o += _w
IN_W = _o

GATE_COL = 0
DT_COL = 12
IN_OUTPUTS = (
    ("nsa_q", BF16, ("nsa_q",), 256),
    ("nsa_k2", BF16, ("nsa_k_slc", "nsa_k_win"), 128),
    ("nsa_cmp", F32, ("nsa_k_cmp", "nsa_v_cmp"), 128),
    ("misc", F32, ("nsa_gate", "ssm_dt"), 128),
    ("z_all", F32, ("nsa_z", "diff_z", "ret_z", "ssm_z"), 1024),
    ("diff_qk", BF16, ("diff_q", "diff_k"), 512),
    ("ret_qkv", BF16, ("ret_q", "ret_k", "ret_v"), 768),
    ("xbc", F32, ("ssm_xbc",), 768),
)
IN_T_SRC = ("diff_v", "nsa_v_slc", "nsa_v_win")
IN_T_ROWS = 384
DIFF_VT_BLK = 0
NSA_VT_BLK = 2
IN_SEGS = []
_o = 0
for _n, _dt, _src, _w in IN_OUTPUTS:
    IN_SEGS.append((_o, _o + _w))
    _o += _w
IN_WP = _o

VMEM_LIMIT = 56 * 1024 * 1024


def _alibi_slopes():
    return [2.0 ** (-8.0 * (i + 1) / N_ALIBI_HEADS) for i in range(N_ALIBI_HEADS)]


NSA_SLOPES = _alibi_slopes()[0::2]
DIFF_SLOPES = _alibi_slopes()[1::2]


def _silu(x):
    return x * jax.nn.sigmoid(x)


def _dot(a, b):
    return jnp.dot(a, b, preferred_element_type=F32)


def _dot_nt(a, b):
    return lax.dot_general(a, b, (((1,), (1,)), ((), ())), preferred_element_type=F32)


def _dot_tn(a, b):
    return lax.dot_general(a, b, (((0,), (0,)), ((), ())), preferred_element_type=F32)


def _params(sem):
    return pltpu.CompilerParams(dimension_semantics=sem, vmem_limit_bytes=VMEM_LIMIT)


def _relayout_w_in(w_in):
    cols = []
    for _n, _dt, src, width in IN_OUTPUTS:
        used = 0
        for s in src:
            off, w = IN_OFF[s]
            cols.append(w_in[:, :, off:off + w])
            used += w
        if used < width:
            cols.append(jnp.zeros(w_in.shape[:2] + (width - used,), w_in.dtype))
    return jnp.concatenate(cols, axis=-1).astype(BF16)


def _relayout_w_in_t(w_in):
    cols = [w_in[:, :, IN_OFF[s][0]:IN_OFF[s][0] + IN_OFF[s][1]] for s in IN_T_SRC]
    return jnp.swapaxes(jnp.concatenate(cols, axis=-1), 1, 2).astype(BF16)


def _norm_project(x, nw_ref, w_ref, wt_ref, out_refs):
    ms = jnp.mean(x * x, axis=-1, keepdims=True)
    h = (x * lax.rsqrt(ms + EPS) * nw_ref[...]).astype(BF16)
    for ref, (a, b) in zip(out_refs[:-1], IN_SEGS):
        ref[...] = _dot(h, w_ref[:, a:b]).astype(ref.dtype)
    out_refs[-1][...] = _dot_nt(wt_ref[...], h).astype(BF16)


def _in_proj_kernel(x_ref, nw_ref, w_ref, wt_ref, *out_refs):
    _norm_project(x_ref[...], nw_ref, w_ref, wt_ref, out_refs)


def _in_proj_outputs(m, tm):
    out_shape = [jax.ShapeDtypeStruct((m, b - a), dt) for (_n, dt, _s, _w), (a, b) in zip(IN_OUTPUTS, IN_SEGS)]
    out_specs = [pl.BlockSpec((tm, b - a), lambda i: (i, 0)) for (a, b) in IN_SEGS]
    out_shape.append(jax.ShapeDtypeStruct((IN_T_ROWS, m), BF16))
    out_specs.append(pl.BlockSpec((IN_T_ROWS, tm), lambda i: (0, i)))
    return out_shape, out_specs


def _in_proj(x2d, norm_w, w_r, w_t, tm):
    m = x2d.shape[0]
    out_shape, out_specs = _in_proj_outputs(m, tm)
    return pl.pallas_call(
        _in_proj_kernel,
        grid=(m // tm,),
        in_specs=[pl.BlockSpec((tm, D_MODEL), lambda i: (i, 0)),
                  pl.BlockSpec((1, D_MODEL), lambda i: (0, 0)),
                  pl.BlockSpec((D_MODEL, IN_WP), lambda i: (0, 0)),
                  pl.BlockSpec((IN_T_ROWS, D_MODEL), lambda i: (0, 0))],
        out_specs=out_specs,
        out_shape=out_shape,
        compiler_params=_params(("parallel",)),
        name="in_proj",
    )(x2d, norm_w.reshape(1, D_MODEL), w_r, w_t)


def _out_proj_kernel(y0_ref, y1_ref, y2_ref, y3_ref, w_ref, x_ref, fw_ref, o_ref):
    acc = x_ref[...]
    for g, y_ref in enumerate((y0_ref, y1_ref, y2_ref, y3_ref)):
        acc = acc + _dot(y_ref[...], w_ref[g * GROUP_W:(g + 1) * GROUP_W, :])
    ms = jnp.mean(acc * acc, axis=-1, keepdims=True)
    o_ref[...] = acc * lax.rsqrt(ms + EPS) * fw_ref[...]


def _out_proj(ys, w_out_b, x2d, final_w, tm):
    m = x2d.shape[0]
    yspec = pl.BlockSpec((tm, GROUP_W), lambda i: (i, 0))
    return pl.pallas_call(
        _out_proj_kernel,
        grid=(m // tm,),
        in_specs=[yspec, yspec, yspec, yspec,
                  pl.BlockSpec((D_MODEL, D_MODEL), lambda i: (0, 0)),
                  pl.BlockSpec((tm, D_MODEL), lambda i: (i, 0)),
                  pl.BlockSpec((1, D_MODEL), lambda i: (0, 0))],
        out_specs=pl.BlockSpec((tm, D_MODEL), lambda i: (i, 0)),
        out_shape=jax.ShapeDtypeStruct((m, D_MODEL), F32),
        compiler_params=_params(("parallel",)),
        name="out_proj",
    )(*ys, w_out_b, x2d, final_w.reshape(1, D_MODEL))


def _out_in_proj_kernel(y0_ref, y1_ref, y2_ref, y3_ref, wo_ref, x_ref, nw_ref, w_ref, wt_ref, xo_ref, *out_refs):
    acc = x_ref[...]
    for g, y_ref in enumerate((y0_ref, y1_ref, y2_ref, y3_ref)):
        acc = acc + _dot(y_ref[...], wo_ref[g * GROUP_W:(g + 1) * GROUP_W, :])
    xo_ref[...] = acc
    _norm_project(acc, nw_ref, w_ref, wt_ref, out_refs)


def _out_in_proj(ys, w_out_b, x2d, norm_w, w_r, w_t, tm):
    m = x2d.shape[0]
    yspec = pl.BlockSpec((tm, GROUP_W), lambda i: (i, 0))
    xspec = pl.BlockSpec((tm, D_MODEL), lambda i: (i, 0))
    out_shape, out_specs = _in_proj_outputs(m, tm)
    return pl.pallas_call(
        _out_in_proj_kernel,
        grid=(m // tm,),
        in_specs=[yspec, yspec, yspec, yspec,
                  pl.BlockSpec((D_MODEL, D_MODEL), lambda i: (0, 0)),
                  xspec,
                  pl.BlockSpec((1, D_MODEL), lambda i: (0, 0)),
                  pl.BlockSpec((D_MODEL, IN_WP), lambda i: (0, 0)),
                  pl.BlockSpec((IN_T_ROWS, D_MODEL), lambda i: (0, 0))],
        out_specs=[xspec] + out_specs,
        out_shape=[jax.ShapeDtypeStruct((m, D_MODEL), F32)] + out_shape,
        compiler_params=_params(("parallel",)),
        name="out_in_proj",
    )(*ys, w_out_b, x2d, norm_w.reshape(1, D_MODEL), w_r, w_t)


def _flash_update_t(s, c1, shift, vt_ext, m_ref, acc_ref, idx, cols):
    m_old = m_ref[idx, :, cols]
    m_new = jnp.maximum(m_old, c1 * jnp.max(s(), axis=0, keepdims=True) + shift)
    alpha = jnp.exp2(m_old - m_new)
    p = jnp.exp2(c1 * s() - (m_new - shift))
    acc_ref[idx, :, cols] = alpha * acc_ref[idx, :, cols] + _dot(vt_ext, p.astype(BF16))
    m_ref[idx, :, cols] = m_new


def _bf16_pieces(x, n=3):
    out = []
    for _ in range(n):
        p = float(np.asarray(x, np.float32).astype(BF16).astype(np.float32))
        out.append(p)
        x = x - p
    return out


def _key_pos_features(tk):
    lane = lax.broadcasted_iota(jnp.int32, (tk, LANES), 1)
    row = lax.broadcasted_iota(jnp.int32, (tk, LANES), 0)
    out = jnp.zeros((tk, LANES), jnp.int32)
    for d in range(-(-tk // POS_RADIX)):
        digit = jnp.bitwise_and(jnp.right_shift(row, 8 * d), POS_RADIX - 1)
        out = jnp.where((lane >= ALIBI_ROWS * d) & (lane < ALIBI_ROWS * (d + 1)), digit, out)
    return out.astype(F32).astype(BF16)


def _alibi_rows(beta, tq, rows=LANES, tk=POS_RADIX):
    row = lax.broadcasted_iota(jnp.int32, (rows, tq), 0)
    out = jnp.zeros((rows, tq), F32)
    for d in range(-(-tk // POS_RADIX)):
        for r, piece in enumerate(_bf16_pieces(beta * POS_RADIX ** d, ALIBI_ROWS)):
            out = jnp.where(row == ALIBI_ROWS * d + r, piece, out)
    return out.astype(BF16)


def _normalized(acc):
    return acc[0:HEAD_DIM] / jnp.maximum(acc[HEAD_DIM:HEAD_DIM + 1], 1e-30)


def _rel_t(tk, tq):
    return lax.broadcasted_iota(jnp.int32, (tk, tq), 1) - lax.broadcasted_iota(jnp.int32, (tk, tq), 0)


def _diff_kernel(lam_ref, sw_ref, q_ref, k_ref, vt_ref, z_ref, o_ref, m_ref, acc_ref, wq_ref, s_ref,
                 *, tq, tk, lam_init):
    qi = pl.program_id(1)
    kpq = tq // tk
    scale = DIFF_QK_DIM ** -0.5
    c1 = scale * LOG2E
    m_ref[...] = jnp.full(m_ref.shape, NEG_INF, F32)
    acc_ref[...] = jnp.zeros(acc_ref.shape, F32)
    rel = _rel_t(tk, tq)

    qt = q_ref[...].astype(F32).T.astype(BF16)
    row = lax.broadcasted_iota(jnp.int32, (LANES, tq), 0)
    for j in range(2 * HEADS):
        g, r0 = divmod(j * DIFF_QK_DIM, LANES)
        qg = qt[g * LANES:(g + 1) * LANES]
        wq_ref[j, 0:LANES, :] = jnp.where((row >= r0) & (row < r0 + DIFF_QK_DIM), qg, jnp.zeros_like(qg))
        wq_ref[j, LANES:2 * LANES, :] = _alibi_rows(DIFF_SLOPES[j // 2] / scale, tq, tk=tk)

    ones = jnp.ones((BF16_ROWS, tk), BF16)
    kpos = _key_pos_features(tk)

    n_maps = 2 * HEADS
    n_slots = s_ref.shape[0]

    def tiles(kis, mask_offs):
        loaded = []
        for ki in kis:
            start = pl.multiple_of(ki * tk, tk)
            loaded.append((k_ref[pl.ds(start, tk), :],
                           vt_ref[:, pl.ds(start, tk)],
                           (ki * tk).astype(F32)))
        items = [(t, j) for t in range(len(kis)) for j in range(n_maps)]
        cols = [slice(0, tq)] * len(kis)

        def scores(i):
            t, j = items[i]
            g = (j * DIFF_QK_DIM) // LANES
            lhs = jnp.concatenate([loaded[t][0][:, g * LANES:(g + 1) * LANES], kpos], axis=1)
            s_ref[i % n_slots, :, cols[t]] = _dot(lhs, wq_ref[j, :, cols[t]])

        def staged(i):
            t = items[i][0]
            if mask_offs[t] is None:
                return lambda: s_ref[i % n_slots, :, cols[t]]
            return lambda: jnp.where(rel[:, cols[t]] >= mask_offs[t], s_ref[i % n_slots, :, cols[t]], NEG_INF)

        for i in range(QK_LOOKAHEAD):
            scores(i)
        for i, (t, j) in enumerate(items):
            h = j // 2
            if i + QK_LOOKAHEAD < len(items):
                scores(i + QK_LOOKAHEAD)
            _k, vt, key0 = loaded[t]
            vt_ext = jnp.concatenate([vt[h * HEAD_DIM:(h + 1) * HEAD_DIM], ones], axis=0)
            _flash_update_t(staged(i), c1, (DIFF_SLOPES[h] * LOG2E) * key0, vt_ext, m_ref, acc_ref, j, cols[t])

    def body(i, carry):
        tiles([i * kpq + d for d in range(kpq)], [None] * kpq)
        return carry

    lax.fori_loop(0, qi, body, 0)
    tiles([qi * kpq + d for d in range(kpq)], [d * tk for d in range(kpq)])

    lp = lam_ref[...]
    lam = (jnp.exp(jnp.sum(lp[0:1] * lp[1:2], axis=-1, keepdims=True))
           - jnp.exp(jnp.sum(lp[2:3] * lp[3:4], axis=-1, keepdims=True)) + lam_init)
    sw = sw_ref[...]
    o_t = jnp.concatenate([_normalized(acc_ref[2 * h]) - lam * _normalized(acc_ref[2 * h + 1])
                           for h in range(HEADS)], axis=0)
    o = o_t.T
    outs = []
    for h in range(HEADS):
        oh = o[:, h * HEAD_DIM:(h + 1) * HEAD_DIM]
        ms = jnp.mean(oh * oh, axis=-1, keepdims=True)
        outs.append(oh * lax.rsqrt(ms + EPS) * sw * (1.0 - lam_init))
    y = jnp.concatenate(outs, axis=-1) * _silu(z_ref[...])
    o_ref[...] = y.astype(o_ref.dtype)


def _diff_attention(qk, vt_all, z_all, lam_p, subln_w, batch, seq, layer_idx, tq, tk):
    nq = seq // tq
    lam_init = 0.8 - 0.6 * math.exp(-0.3 * layer_idx)
    return pl.pallas_call(
        functools.partial(_diff_kernel, tq=tq, tk=tk, lam_init=lam_init),
        grid=(batch, nq),
        in_specs=[pl.BlockSpec((4, DIFF_QK_DIM), lambda b, i: (0, 0)),
                  pl.BlockSpec((1, HEAD_DIM), lambda b, i: (0, 0)),
                  pl.BlockSpec((tq, GROUP_W), lambda b, i: (b * nq + i, 0)),
                  pl.BlockSpec((seq, GROUP_W), lambda b, i: (b, 1)),
                  pl.BlockSpec((GROUP_W, seq), lambda b, i: (DIFF_VT_BLK, b)),
                  pl.BlockSpec((tq, GROUP_W), lambda b, i: (b * nq + i, 1))],
        out_specs=pl.BlockSpec((tq, GROUP_W), lambda b, i: (b * nq + i, 0)),
        out_shape=jax.ShapeDtypeStruct((batch * seq, GROUP_W), BF16),
        scratch_shapes=[pltpu.VMEM((2 * HEADS, 1, tq), F32),
                        pltpu.VMEM((2 * HEADS, ACC_ROWS, tq), F32),
                        pltpu.VMEM((2 * HEADS, 2 * LANES, tq), BF16),
                        pltpu.VMEM((QK_LOOKAHEAD + 1, tk, tq), F32)],
        compiler_params=_params(("parallel", "parallel")),
        name="diff_attention",
    )(lam_p, subln_w.reshape(1, HEAD_DIM), qk, qk, vt_all, z_all)


def _nsa_compress_kernel(gk_ref, gv_ref, pek_ref, pev_ref, wk1_ref, wk2_ref, wv1_ref, wv2t_ref, kc_ref, vct_ref):
    half = CMP_STRIDE * HEAD_DIM

    def hidden(g_ref, pe_ref, w1_ref):
        g = g_ref[0]
        top = (g + pe_ref[0:1, :]).astype(BF16)
        bot = (g + pe_ref[1:2, :]).astype(BF16)
        a = _dot(top, w1_ref[0:half, :])
        b = _dot(bot, w1_ref[half:2 * half, :])
        nrow = b.shape[0]
        b = pltpu.roll(b, nrow - 1, 0)
        return _silu(a + b).astype(BF16)

    hk = hidden(gk_ref, pek_ref, wk1_ref)
    kc = _dot(hk, wk2_ref[...])
    lane = lax.broadcasted_iota(jnp.int32, kc.shape, 1)
    blk = lax.broadcasted_iota(jnp.int32, kc.shape, 0).astype(F32)
    kc = jnp.where((lane >= HEAD_DIM) & (lane < HEAD_DIM + ALIBI_ROWS), blk, kc)
    kc_ref[0] = kc.astype(kc_ref.dtype)
    hv = hidden(gv_ref, pev_ref, wv1_ref)
    vct_ref[0] = _dot_nt(wv2t_ref[...], hv).astype(vct_ref.dtype)


def _nsa_compress(cmp2d, pe_k, pe_v, w_ck1, w_ck2, w_cv1, w_cv2, batch, seq):
    ng = seq // CMP_STRIDE
    half = CMP_STRIDE * HEAD_DIM
    gk = cmp2d[:, :HEAD_DIM].reshape(batch, ng, half)
    gv = cmp2d[:, HEAD_DIM:].reshape(batch, ng, half)
    gspec = pl.BlockSpec((1, ng, half), lambda b: (b, 0, 0))

    def full(shape):
        return pl.BlockSpec(shape, lambda b: (0,) * len(shape))

    return pl.pallas_call(
        _nsa_compress_kernel,
        grid=(batch,),
        in_specs=[gspec, gspec, full((2, half)), full((2, half)),
                  full((2 * half, CMP_HIDDEN)), full((CMP_HIDDEN, LANES)),
                  full((2 * half, CMP_HIDDEN)), full((HEAD_DIM, CMP_HIDDEN))],
        out_specs=[pl.BlockSpec((1, ng, LANES), lambda b: (b, 0, 0)),
                   pl.BlockSpec((1, HEAD_DIM, ng), lambda b: (b, 0, 0))],
        out_shape=[jax.ShapeDtypeStruct((batch, ng, LANES), BF16),
                   jax.ShapeDtypeStruct((batch, HEAD_DIM, ng), BF16)],
        compiler_params=_params(("parallel",)),
        name="nsa_compress",
    )(gk, gv, pe_k.reshape(2, half), pe_v.reshape(2, half),
      w_ck1.astype(BF16), jnp.pad(w_ck2, ((0, 0), (0, LANES - HEAD_DIM))).astype(BF16),
      w_cv1.astype(BF16), w_cv2.T.astype(BF16))


def _nsa_kernel(ovl_ref, q_ref, k_ref, vt_ref, kc_ref, vct_ref, misc_ref, z_ref, o_ref,
                m_ref, acc_ref, wq_ref, s_ref, sc_ref, *, tq, tk, seq):
    qi = pl.program_id(1)
    kpq = tq // tk
    scale = HEAD_DIM ** -0.5
    c1 = scale * LOG2E
    ng = seq // CMP_STRIDE
    ns = seq // SLC_BLOCK
    top = min(SLC_TOPK, ns)
    m_ref[...] = jnp.full(m_ref.shape, NEG_INF, F32)
    acc_ref[...] = jnp.zeros(acc_ref.shape, F32)
    rel = _rel_t(tk, tq)

    qt = q_ref[...].astype(F32).T.astype(BF16)
    zeros_q = jnp.zeros((HEAD_DIM, tq), BF16)
    for h in range(HEADS):
        qh = qt[h * HEAD_DIM:(h + 1) * HEAD_DIM]
        alibi = _alibi_rows(NSA_SLOPES[h] / scale, tq, tk=tk)
        wq_ref[h, 0:LANES, :] = jnp.concatenate([qh, zeros_q], axis=0)
        wq_ref[h, LANES:2 * LANES, :] = alibi
        wq_ref[HEADS + h, 0:LANES, :] = jnp.concatenate([zeros_q, qh], axis=0)
        wq_ref[HEADS + h, LANES:2 * LANES, :] = alibi

    t_lane = qi * tq + lax.broadcasted_iota(jnp.int32, (1, tq), 1)
    n_sub = lax.broadcasted_iota(jnp.int32, (ng, 1), 0)
    c_valid = (n_sub * CMP_STRIDE + (CMP_LEN - 1)) <= t_lane
    kc = kc_ref[0]
    vct_ext = jnp.concatenate([vct_ref[0], jnp.ones((BF16_ROWS, ng), BF16)], axis=0)
    ovl = ovl_ref[...]
    for h in range(HEADS):
        rhs = jnp.concatenate([qt[h * HEAD_DIM:(h + 1) * HEAD_DIM],
                               _alibi_rows(NSA_SLOPES[h] * CMP_STRIDE / scale, tq, HEAD_DIM)], axis=0)
        sc_ref[h] = jnp.where(c_valid, _dot(kc, rhs), NEG_INF)
    imp = jnp.zeros((ns, tq), F32)
    o_cmp = []
    for h in range(HEADS):
        mx = jnp.max(sc_ref[h], axis=0, keepdims=True)
        mx = jnp.where(mx > NEG_INF, mx, 0.0)
        pb = jnp.exp2(c1 * sc_ref[h] - c1 * mx).astype(BF16)
        o_ext = _dot(vct_ext, pb)
        r = 1.0 / jnp.maximum(o_ext[HEAD_DIM:HEAD_DIM + 1], 1e-30)
        o_cmp.append(o_ext[0:HEAD_DIM] * r)
        imp = imp + _dot(ovl, pb) * r

    def select_blocks():
        j_sub = lax.broadcasted_iota(jnp.int32, (ns, 1), 0)
        j_sub_f = j_sub.astype(F32)
        cur = jnp.right_shift(t_lane, SLC_SHIFT)
        forced = (j_sub == 0) | (j_sub == cur) | (j_sub == cur - 1)
        valid = (j_sub * SLC_BLOCK) <= t_lane
        score = jnp.where(forced, 1e30, jnp.where(valid, imp, -1.0))
        sel = jnp.zeros((ns, tq), F32)
        for _ in range(top):
            mx = jnp.max(score, axis=0, keepdims=True)
            idx = jnp.min(jnp.where(score == mx, j_sub_f, float(ns)), axis=0, keepdims=True)
            pick = j_sub_f == idx
            sel = jnp.where(pick, 1.0, sel)
            score = jnp.where(pick, -2.0, score)
        unsel = jnp.where(sel > 0.5, 0.0, -MASK_BIG).astype(BF16)
        for h in range(HEADS):
            wq_ref[h, LANES + BLK_LANE0:LANES + BLK_LANE0 + ns, :] = unsel

    aux_lane = lax.broadcasted_iota(jnp.int32, (tk, LANES), 1)
    aux_blk = jnp.right_shift(lax.broadcasted_iota(jnp.int32, (tk, LANES), 0), SLC_SHIFT) + BLK_LANE0
    kpos = _key_pos_features(tk)
    ones = jnp.ones((BF16_ROWS, tk), BF16)

    n_slots = s_ref.shape[0]

    def tiles(window, kis, cols, mask_offs=None):
        base = HEADS if window else 0
        loaded = []
        for t, ki in enumerate(kis):
            start = pl.multiple_of(ki * tk, tk)
            k = k_ref[pl.ds(start, tk), :]
            if window:
                vt = vt_ref[HEAD_DIM:2 * HEAD_DIM, pl.ds(start, tk)]
                aux = kpos
                dist = rel[:, cols[t]] + (qi * tq - ki * tk)
                keep = (dist >= 0) & (dist < WINDOW)
            else:
                vt = vt_ref[0:HEAD_DIM, pl.ds(start, tk)]
                onehot = aux_lane == aux_blk + ki * (tk // SLC_BLOCK)
                aux = jnp.where(onehot, jnp.ones_like(kpos), kpos)
                keep = None
            loaded.append((jnp.concatenate([k, aux], axis=1),
                           jnp.concatenate([vt, ones], axis=0), (ki * tk).astype(F32), keep))
        items = [(t, h) for t in range(len(kis)) for h in range(HEADS)]

        def scores(i):
            t, h = items[i]
            s_ref[i % n_slots, :, cols[t]] = _dot(loaded[t][0], wq_ref[base + h, :, cols[t]])

        def staged(i):
            t = items[i][0]
            if window:
                return lambda: jnp.where(loaded[t][3], s_ref[i % n_slots, :, cols[t]], NEG_INF)
            if mask_offs[t] is None:
                return lambda: s_ref[i % n_slots, :, cols[t]]
            return lambda: jnp.where(rel[:, cols[t]] >= mask_offs[t], s_ref[i % n_slots, :, cols[t]], NEG_INF)

        for i in range(QK_LOOKAHEAD):
            scores(i)
        for i, (t, h) in enumerate(items):
            if i + QK_LOOKAHEAD < len(items):
                scores(i + QK_LOOKAHEAD)
            _lhs, vt_ext, key0, _keep = loaded[t]
            _flash_update_t(staged(i), c1, (NSA_SLOPES[h] * LOG2E) * key0, vt_ext, m_ref, acc_ref, base + h,
                            cols[t])

    all_cols = slice(0, tq)
    diag_cols = [slice(d * tk, tq) for d in range(kpq)]

    def body(i, carry):
        tiles(False, [i * kpq + d for d in range(kpq)], [all_cols] * kpq, [None] * kpq)
        return carry

    tiles(True, [qi * kpq + d for d in range(kpq)], diag_cols)
    select_blocks()

    lax.fori_loop(0, qi, body, 0)
    tiles(False, [qi * kpq + d for d in range(kpq)], diag_cols, [d * tk for d in range(kpq)])

    n_back = (WINDOW + tk - 1) // tk
    for g in range((n_back + kpq - 1) // kpq):
        backs = list(range(g * kpq + 1, min((g + 1) * kpq, n_back) + 1))
        back_cols = [slice(0, min(tq, -(-(WINDOW - (back - 1) * tk - 1) // LANES) * LANES)) for back in backs]

        @pl.when(qi > g)
        def _():
            tiles(True, [qi * kpq - back for back in backs], back_cols)

    g_t = jax.nn.sigmoid(misc_ref[...]).T
    outs = []
    for h in range(HEADS):
        r0 = GATE_COL + 3 * h
        outs.append(g_t[r0:r0 + 1] * o_cmp[h] + g_t[r0 + 1:r0 + 2] * _normalized(acc_ref[h])
                    + g_t[r0 + 2:r0 + 3] * _normalized(acc_ref[HEADS + h]))
    y = jnp.concatenate(outs, axis=0).T * _silu(z_ref[...])
    o_ref[...] = y.astype(o_ref.dtype)


def _overlap_t(seq):
    nc = (seq - CMP_LEN) // CMP_STRIDE + 1
    ng = seq // CMP_STRIDE
    ns = seq // SLC_BLOCK
    c_start = np.arange(ng) * CMP_STRIDE
    c_end = c_start + CMP_LEN - 1
    s_start = np.arange(ns) * SLC_BLOCK
    s_end = s_start + SLC_BLOCK - 1
    ov = (c_start[None, :] <= s_end[:, None]) & (c_end[None, :] >= s_start[:, None]) & (np.arange(ng)[None, :] < nc)
    return jnp.asarray(ov.astype(np.float32), dtype=BF16)


def _nsa_attention(q, k2, vt_all, kc, vct, misc, z_all, batch, seq, tq, tk):
    nq = seq // tq
    ng = seq // CMP_STRIDE
    ns = seq // SLC_BLOCK
    assert tq - tk < WINDOW and tk % SLC_BLOCK == 0 and ns <= LANES - BLK_LANE0
    return pl.pallas_call(
        functools.partial(_nsa_kernel, tq=tq, tk=tk, seq=seq),
        grid=(batch, nq),
        in_specs=[pl.BlockSpec((ns, ng), lambda b, i: (0, 0)),
                  pl.BlockSpec((tq, GROUP_W), lambda b, i: (b * nq + i, 0)),
                  pl.BlockSpec((seq, LANES), lambda b, i: (b, 0)),
                  pl.BlockSpec((2 * HEAD_DIM, seq), lambda b, i: (NSA_VT_BLK, b)),
                  pl.BlockSpec((1, ng, LANES), lambda b, i: (b, 0, 0)),
                  pl.BlockSpec((1, HEAD_DIM, ng), lambda b, i: (b, 0, 0)),
                  pl.BlockSpec((tq, LANES), lambda b, i: (b * nq + i, 0)),
                  pl.BlockSpec((tq, GROUP_W), lambda b, i: (b * nq + i, 0))],
        out_specs=pl.BlockSpec((tq, GROUP_W), lambda b, i: (b * nq + i, 0)),
        out_shape=jax.ShapeDtypeStruct((batch * seq, GROUP_W), BF16),
        scratch_shapes=[pltpu.VMEM((2 * HEADS, 1, tq), F32),
                        pltpu.VMEM((2 * HEADS, ACC_ROWS, tq), F32),
                        pltpu.VMEM((2 * HEADS, 2 * LANES, tq), BF16),
                        pltpu.VMEM((2 * HEADS, tk, tq), F32),
                        pltpu.VMEM((HEADS, ng, tq), F32)],
        compiler_params=_params(("parallel", "parallel")),
        name="nsa_attention",
    )(_overlap_t(seq), q, k2, vt_all, kc, vct, misc, z_all)


def _ret_tables():
    c = RET_CHUNK
    h = np.arange(HEADS, dtype=np.float32)
    log_g = jnp.log(1.0 - 2.0 ** (-5.0 - jnp.asarray(h)))
    pos = jnp.arange(c, dtype=F32)
    rel = pos[:, None] - pos[None, :]
    decay = jnp.where(rel >= 0, jnp.exp(log_g[:, None, None] * jnp.maximum(rel, 0.0)), 0.0)
    xi = jnp.exp(log_g[:, None] * (pos + 1.0))
    zeta = jnp.exp(log_g[:, None] * (c - 1.0 - pos))
    chunk_decay = jnp.exp(log_g * c)
    xi_tab = jnp.repeat(xi.T, HEAD_DIM, axis=1)
    zeta_tab = jnp.repeat(zeta.T, HEAD_DIM, axis=1)
    cd_tab = jnp.repeat(chunk_decay, HEAD_DIM)[None, :]
    return decay, xi_tab, zeta_tab, cd_tab


def _ret_kernel(decay_ref, xi_ref, zeta_ref, cd_ref, gn_ref, q_ref, k_ref, v_ref, z_ref, o_ref, st_ref):
    n = pl.program_id(1)

    @pl.when(n == 0)
    def _():
        st_ref[...] = jnp.zeros(st_ref.shape, F32)

    xi = xi_ref[...]
    cd = cd_ref[...]
    for bb in range(q_ref.shape[0]):
        q = (q_ref[bb].astype(F32) * (HEAD_DIM ** -0.5)).astype(BF16)
        k = k_ref[bb]
        v = v_ref[bb]
        kz_t = (k.astype(F32) * zeta_ref[...]).T.astype(BF16)
        outs = []
        for h in range(HEADS):
            sl = slice(h * HEAD_DIM, (h + 1) * HEAD_DIM)
            qh, kh, vh = q[:, sl], k[:, sl], v[:, sl]
            prev = st_ref[bb * HEADS + h]
            inner = (_dot_nt(qh, kh) * decay_ref[h]).astype(BF16)
            o = _dot(inner, vh) + _dot(qh, prev.astype(BF16)) * xi[:, sl]
            st_ref[bb * HEADS + h] = prev * cd[:, sl] + _dot(kz_t[sl, :], vh)
            mu = jnp.mean(o, axis=-1, keepdims=True)
            d = o - mu
            var = jnp.mean(d * d, axis=-1, keepdims=True)
            outs.append(d * lax.rsqrt(var + EPS))
        y = jnp.concatenate(outs, axis=-1) * gn_ref[...] * _silu(z_ref[bb])
        o_ref[bb] = y.astype(o_ref.dtype)


def _retention(qkv, z_all, gn_w, batch, seq, bpb):
    c = RET_CHUNK
    nch = seq // c
    decay, xi_tab, zeta_tab, cd_tab = _ret_tables()

    def full(shape):
        return pl.BlockSpec(shape, lambda b, n: (0,) * len(shape))

    def blk(col):
        return pl.BlockSpec((bpb, c, GROUP_W), lambda b, n: (b, n, col))

    out = pl.pallas_call(
        _ret_kernel,
        grid=(batch // bpb, nch),
        in_specs=[full((HEADS, c, c)), full((c, GROUP_W)), full((c, GROUP_W)), full((1, GROUP_W)),
                  full((1, GROUP_W)), blk(0), blk(1), blk(2), blk(2)],
        out_specs=blk(0),
        out_shape=jax.ShapeDtypeStruct((batch, seq, GROUP_W), BF16),
        scratch_shapes=[pltpu.VMEM((bpb * HEADS, HEAD_DIM, HEAD_DIM), F32)],
        compiler_params=_params(("parallel", "arbitrary")),
        name="retention",
    )(decay, xi_tab, zeta_tab, cd_tab, gn_w.reshape(1, GROUP_W),
      *(qkv.reshape(batch, seq, -1),) * 3, z_all.reshape(batch, seq, -1))
    return out.reshape(batch * seq, GROUP_W)


def _ssd_kernel(cw_ref, cb_ref, dtb_ref, a_ref, dsk_ref, nw_ref, xbc_ref, misc_ref, z_ref, o_ref,
                ext_ref, st_ref):
    n = pl.program_id(1)
    L = SSM_CHUNK
    hi = lax.Precision.HIGHEST

    @pl.when(n == 0)
    def _():
        st_ref[...] = jnp.zeros(st_ref.shape, F32)
        ext_ref[:, 0:8, :] = jnp.zeros((ext_ref.shape[0], 8, CONV_CH), F32)

    row = lax.broadcasted_iota(jnp.int32, (L, L), 0)
    col = lax.broadcasted_iota(jnp.int32, (L, L), 1)
    causal = row >= col
    tril = jnp.where(causal, 1.0, 0.0).astype(F32)
    dsk = dsk_ref[...]

    for bb in range(xbc_ref.shape[0]):
        raw = xbc_ref[bb]
        ext_ref[bb, 8:8 + L, :] = raw
        conv = cb_ref[...] + raw * cw_ref[CONV_W - 1:CONV_W, :]
        for w in range(CONV_W - 1):
            shift = CONV_W - 1 - w
            conv = conv + ext_ref[bb, 8 - shift:8 - shift + L, :] * cw_ref[w:w + 1, :]
        ext_ref[bb, 0:8, :] = raw[L - 8:L, :]
        xc = _silu(conv)
        x = xc[:, 0:GROUP_W]
        bm = xc[:, GROUP_W:GROUP_W + 2 * SSM_STATE].astype(BF16)
        cm = xc[:, GROUP_W + 2 * SSM_STATE:].astype(BF16)

        dt_full = jax.nn.softplus(misc_ref[bb] + dtb_ref[...])
        da = dt_full * a_ref[...]
        cs_col = jnp.dot(tril, da, precision=hi, preferred_element_type=F32)
        cs_row = lax.dot_general(da, tril, (((0,), (1,)), ((), ())), precision=hi,
                                 preferred_element_type=F32)

        outs = []
        for h in range(HEADS):
            g = h // 2
            c0 = DT_COL + h
            sl = slice(h * HEAD_DIM, (h + 1) * HEAD_DIM)
            gs = slice(g * SSM_STATE, (g + 1) * SSM_STATE)
            cs_c = cs_col[:, c0:c0 + 1]
            cs_r = cs_row[c0:c0 + 1, :]
            cs_last = cs_col[L - 1:L, c0:c0 + 1]
            xh = x[:, sl]
            xdt = xh * dt_full[:, c0:c0 + 1]
            seg = jnp.exp(jnp.where(causal, cs_c - cs_r, NEG_INF))
            cb = _dot_nt(cm[:, gs], bm[:, gs])
            y = _dot((cb * seg).astype(BF16), xdt.astype(BF16))
            prev = st_ref[bb * HEADS + h]
            y = y + _dot(cm[:, gs], prev.astype(BF16)) * jnp.exp(cs_c)
            y = y + dsk[:, sl] * xh
            dec = jnp.exp(cs_last - cs_c)
            st_ref[bb * HEADS + h] = prev * jnp.exp(cs_last) + _dot_tn(bm[:, gs], (xdt * dec).astype(BF16))
            outs.append(y)
        y = jnp.concatenate(outs, axis=-1) * _silu(z_ref[bb])
        ms = jnp.mean(y * y, axis=-1, keepdims=True)
        o_ref[bb] = (y * lax.rsqrt(ms + EPS) * nw_ref[...]).astype(o_ref.dtype)


def _ssd(xbc, misc, z_all, conv_w, conv_b, dt_bias, a_log, d_skip, norm_w, batch, seq, bpb):
    L = SSM_CHUNK
    nch = seq // L
    dtb = jnp.zeros((1, 128), F32).at[0, DT_COL:DT_COL + HEADS].set(dt_bias)
    a_full = jnp.zeros((1, 128), F32).at[0, DT_COL:DT_COL + HEADS].set(-jnp.exp(a_log))
    dsk = jnp.repeat(d_skip, HEAD_DIM)[None, :]

    def full(shape):
        return pl.BlockSpec(shape, lambda b, n: (0,) * len(shape))

    out = pl.pallas_call(
        _ssd_kernel,
        grid=(batch // bpb, nch),
        in_specs=[full((CONV_W, CONV_CH)), full((1, CONV_CH)), full((1, 128)), full((1, 128)),
                  full((1, GROUP_W)), full((1, GROUP_W)),
                  pl.BlockSpec((bpb, L, CONV_CH), lambda b, n: (b, n, 0)),
                  pl.BlockSpec((bpb, L, 128), lambda b, n: (b, n, 0)),
                  pl.BlockSpec((bpb, L, GROUP_W), lambda b, n: (b, n, 3))],
        out_specs=pl.BlockSpec((bpb, L, GROUP_W), lambda b, n: (b, n, 0)),
        out_shape=jax.ShapeDtypeStruct((batch, seq, GROUP_W), BF16),
        scratch_shapes=[pltpu.VMEM((bpb, 8 + L, CONV_CH), F32),
                        pltpu.VMEM((bpb * HEADS, SSM_STATE, HEAD_DIM), F32)],
        compiler_params=_params(("parallel", "arbitrary")),
        name="ssd",
    )(conv_w, conv_b.reshape(1, CONV_CH), dtb, a_full, dsk, norm_w.reshape(1, GROUP_W),
      xbc.reshape(batch, seq, -1), misc.reshape(batch, seq, -1), z_all.reshape(batch, seq, -1))
    return out.reshape(batch * seq, GROUP_W)


def _pick_tile(n, pref):
    t = pref
    while n % t:
        t //= 2
    return t


def kernel(x, norm_w, w_in, w_out, nsa_pe_k, nsa_pe_v, nsa_w_ck1, nsa_w_ck2, nsa_w_cv1, nsa_w_cv2,
           diff_lam_q1, diff_lam_k1, diff_lam_q2, diff_lam_k2, diff_subln_w, ret_gn_w,
           ssm_conv_w, ssm_conv_b, ssm_dt_bias, ssm_A_log, ssm_D, ssm_norm_w, final_norm_w):
    batch, seq, _ = x.shape
    depth = w_in.shape[0]
    m = batch * seq
    tm = _pick_tile(m, 512)
    tq = _pick_tile(seq, 512)
    tk = _pick_tile(seq, 256)
    tk_diff = _pick_tile(seq, 512)
    bpb = _pick_tile(batch, REC_BATCH)
    w_r = _relayout_w_in(w_in)
    w_t = _relayout_w_in_t(w_in)
    w_out_b = w_out.astype(BF16)
    x2d = x.reshape(m, D_MODEL)
    projected = _in_proj(x2d, norm_w[0], w_r[0], w_t[0], tm)
    for i in range(depth):
        nsa_q, nsa_k2, nsa_cmp, misc, z_all, diff_qk, ret_qkv, xbc, vt_all = projected
        kc, vct = _nsa_compress(nsa_cmp, nsa_pe_k[i], nsa_pe_v[i], nsa_w_ck1[i], nsa_w_ck2[i],
                                nsa_w_cv1[i], nsa_w_cv2[i], batch, seq)
        y_nsa = _nsa_attention(nsa_q, nsa_k2, vt_all, kc, vct, misc, z_all, batch, seq, tq, tk)
        lam_p = jnp.stack([diff_lam_q1[i], diff_lam_k1[i], diff_lam_q2[i], diff_lam_k2[i]])
        y_diff = _diff_attention(diff_qk, vt_all, z_all, lam_p, diff_subln_w[i], batch, seq, i, tq, tk_diff)
        y_ret = _retention(ret_qkv, z_all, ret_gn_w[i], batch, seq, bpb)
        y_ssm = _ssd(xbc, misc, z_all, ssm_conv_w[i], ssm_conv_b[i], ssm_dt_bias[i], ssm_A_log[i],
                     ssm_D[i], ssm_norm_w[i], batch, seq, bpb)
        ys = (y_nsa, y_diff, y_ret, y_ssm)
        if i + 1 < depth:
            x2d, *projected = _out_in_proj(ys, w_out_b[i], x2d, norm_w[i + 1], w_r[i + 1], w_t[i + 1], tm)
        else:
            x2d = _out_proj(ys, w_out_b[i], x2d, final_norm_w, tm)
    return x2d.reshape(batch, seq, D_MODEL)
```

```python
import functools
import math

import numpy as np
import jax
import jax.numpy as jnp
from jax import lax
from jax.experimental import pallas as pl
from jax.experimental.pallas import tpu as pltpu

F32 = jnp.float32
BF16 = jnp.bfloat16
NEG_INF = float("-inf")
LOG2E = 1.4426950408889634

D_MODEL = 1024
DEPTH = 4
GROUP_W = 256
HEADS = 4
HEAD_DIM = 64
EPS = 1e-6
CMP_LEN = 32
CMP_STRIDE = 16
CMP_HIDDEN = 256
SLC_BLOCK = 64
SLC_SHIFT = 6
SLC_TOPK = 16
WINDOW = 512
DIFF_QK_DIM = 32
RET_CHUNK = 128
SSM_STATE = 128
SSM_CHUNK = 128
CONV_W = 4
CONV_CH = 768
N_ALIBI_HEADS = 8
LANES = 128
BF16_ROWS = 16
ACC_ROWS = HEAD_DIM + BF16_ROWS
QK_LOOKAHEAD = 6
ALIBI_ROWS = 3
POS_RADIX = 256
MASK_BIG = 2.0 ** 100
BLK_LANE0 = 64
REC_BATCH = 4

IN_LAYOUT = (
    ("nsa_q", 256), ("nsa_k_cmp", 64), ("nsa_v_cmp", 64), ("nsa_k_slc", 64), ("nsa_v_slc", 64),
    ("nsa_k_win", 64), ("nsa_v_win", 64), ("nsa_gate", 12), ("nsa_z", 256),
    ("diff_q", 256), ("diff_k", 256), ("diff_v", 256), ("diff_z", 256),
    ("ret_q", 256), ("ret_k", 256), ("ret_v", 256), ("ret_z", 256),
    ("ssm_z", 256), ("ssm_xbc", 768), ("ssm_dt", 4),
)
IN_OFF = {}
_o = 0
for _n, _w in IN_LAYOUT:
    IN_OFF[_n] = (_o, _w)
    _o += _w
IN_W = _o

GATE_COL = 0
DT_COL = 12
IN_OUTPUTS = (
    ("nsa_q", BF16, ("nsa_q",), 256),
    ("nsa_k2", BF16, ("nsa_k_slc", "nsa_k_win"), 128),
    ("nsa_cmp", F32, ("nsa_k_cmp", "nsa_v_cmp"), 128),
    ("misc", F32, ("nsa_gate", "ssm_dt"), 128),
    ("z_all", F32, ("nsa_z", "diff_z", "ret_z", "ssm_z"), 1024),
    ("diff_qk", BF16, ("diff_q", "diff_k"), 512),
    ("ret_qkv", BF16, ("ret_q", "ret_k", "ret_v"), 768),
    ("xbc", F32, ("ssm_xbc",), 768),
)
IN_T_SRC = ("diff_v", "nsa_v_slc", "nsa_v_win")
IN_T_ROWS = 384
DIFF_VT_BLK = 0
NSA_VT_BLK = 2
IN_SEGS = []
_o = 0
for _n, _dt, _src, _w in IN_OUTPUTS:
    IN_SEGS.append((_o, _o + _w))
    _o += _w
IN_WP = _o

VMEM_LIMIT = 56 * 1024 * 1024


def _alibi_slopes():
    return [2.0 ** (-8.0 * (i + 1) / N_ALIBI_HEADS) for i in range(N_ALIBI_HEADS)]


NSA_SLOPES = _alibi_slopes()[0::2]
DIFF_SLOPES = _alibi_slopes()[1::2]


def _silu(x):
    return x * jax.nn.sigmoid(x)


def _dot(a, b):
    return jnp.dot(a, b, preferred_element_type=F32)


def _dot_nt(a, b):
    return lax.dot_general(a, b, (((1,), (1,)), ((), ())), preferred_element_type=F32)


def _dot_tn(a, b):
    return lax.dot_general(a, b, (((0,), (0,)), ((), ())), preferred_element_type=F32)


def _params(sem):
    return pltpu.CompilerParams(dimension_semantics=sem, vmem_limit_bytes=VMEM_LIMIT)


def _relayout_w_in(w_in):
    cols = []
    for _n, _dt, src, width in IN_OUTPUTS:
        used = 0
        for s in src:
            off, w = IN_OFF[s]
            cols.append(w_in[:, :, off:off + w])
            used += w
        if used < width:
            cols.append(jnp.zeros(w_in.shape[:2] + (width - used,), w_in.dtype))
    return jnp.concatenate(cols, axis=-1).astype(BF16)


def _relayout_w_in_t(w_in):
    cols = [w_in[:, :, IN_OFF[s][0]:IN_OFF[s][0] + IN_OFF[s][1]] for s in IN_T_SRC]
    return jnp.swapaxes(jnp.concatenate(cols, axis=-1), 1, 2).astype(BF16)


def _norm_project(x, nw_ref, w_ref, wt_ref, out_refs):
    ms = jnp.mean(x * x, axis=-1, keepdims=True)
    h = (x * lax.rsqrt(ms + EPS) * nw_ref[...]).astype(BF16)
    for ref, (a, b) in zip(out_refs[:-1], IN_SEGS):
        ref[...] = _dot(h, w_ref[:, a:b]).astype(ref.dtype)
    out_refs[-1][...] = _dot_nt(wt_ref[...], h).astype(BF16)


def _in_proj_kernel(x_ref, nw_ref, w_ref, wt_ref, *out_refs):
    _norm_project(x_ref[...], nw_ref, w_ref, wt_ref, out_refs)


def _in_proj_outputs(m, tm):
    out_shape = [jax.ShapeDtypeStruct((m, b - a), dt) for (_n, dt, _s, _w), (a, b) in zip(IN_OUTPUTS, IN_SEGS)]
    out_specs = [pl.BlockSpec((tm, b - a), lambda i: (i, 0)) for (a, b) in IN_SEGS]
    out_shape.append(jax.ShapeDtypeStruct((IN_T_ROWS, m), BF16))
    out_specs.append(pl.BlockSpec((IN_T_ROWS, tm), lambda i: (0, i)))
    return out_shape, out_specs


def _in_proj(x2d, norm_w, w_r, w_t, tm):
    m = x2d.shape[0]
    out_shape, out_specs = _in_proj_outputs(m, tm)
    return pl.pallas_call(
        _in_proj_kernel,
        grid=(m // tm,),
        in_specs=[pl.BlockSpec((tm, D_MODEL), lambda i: (i, 0)),
                  pl.BlockSpec((1, D_MODEL), lambda i: (0, 0)),
                  pl.BlockSpec((D_MODEL, IN_WP), lambda i: (0, 0)),
                  pl.BlockSpec((IN_T_ROWS, D_MODEL), lambda i: (0, 0))],
        out_specs=out_specs,
        out_shape=out_shape,
        compiler_params=_params(("parallel",)),
        name="in_proj",
    )(x2d, norm_w.reshape(1, D_MODEL), w_r, w_t)


def _out_proj_kernel(y0_ref, y1_ref, y2_ref, y3_ref, w_ref, x_ref, fw_ref, o_ref):
    acc = x_ref[...]
    for g, y_ref in enumerate((y0_ref, y1_ref, y2_ref, y3_ref)):
        acc = acc + _dot(y_ref[...], w_ref[g * GROUP_W:(g + 1) * GROUP_W, :])
    ms = jnp.mean(acc * acc, axis=-1, keepdims=True)
    o_ref[...] = acc * lax.rsqrt(ms + EPS) * fw_ref[...]


def _out_proj(ys, w_out_b, x2d, final_w, tm):
    m = x2d.shape[0]
    yspec = pl.BlockSpec((tm, GROUP_W), lambda i: (i, 0))
    return pl.pallas_call(
        _out_proj_kernel,
        grid=(m // tm,),
        in_specs=[yspec, yspec, yspec, yspec,
                  pl.BlockSpec((D_MODEL, D_MODEL), lambda i: (0, 0)),
                  pl.BlockSpec((tm, D_MODEL), lambda i: (i, 0)),
                  pl.BlockSpec((1, D_MODEL), lambda i: (0, 0))],
        out_specs=pl.BlockSpec((tm, D_MODEL), lambda i: (i, 0)),
        out_shape=jax.ShapeDtypeStruct((m, D_MODEL), F32),
        compiler_params=_params(("parallel",)),
        name="out_proj",
    )(*ys, w_out_b, x2d, final_w.reshape(1, D_MODEL))


def _out_in_proj_kernel(y0_ref, y1_ref, y2_ref, y3_ref, wo_ref, x_ref, nw_ref, w_ref, wt_ref, xo_ref, *out_refs):
    acc = x_ref[...]
    for g, y_ref in enumerate((y0_ref, y1_ref, y2_ref, y3_ref)):
        acc = acc + _dot(y_ref[...], wo_ref[g * GROUP_W:(g + 1) * GROUP_W, :])
    xo_ref[...] = acc
    _norm_project(acc, nw_ref, w_ref, wt_ref, out_refs)


def _out_in_proj(ys, w_out_b, x2d, norm_w, w_r, w_t, tm):
    m = x2d.shape[0]
    yspec = pl.BlockSpec((tm, GROUP_W), lambda i: (i, 0))
    xspec = pl.BlockSpec((tm, D_MODEL), lambda i: (i, 0))
    out_shape, out_specs = _in_proj_outputs(m, tm)
    return pl.pallas_call(
        _out_in_proj_kernel,
        grid=(m // tm,),
        in_specs=[yspec, yspec, yspec, yspec,
                  pl.BlockSpec((D_MODEL, D_MODEL), lambda i: (0, 0)),
                  xspec,
                  pl.BlockSpec((1, D_MODEL), lambda i: (0, 0)),
                  pl.BlockSpec((D_MODEL, IN_WP), lambda i: (0, 0)),
                  pl.BlockSpec((IN_T_ROWS, D_MODEL), lambda i: (0, 0))],
        out_specs=[xspec] + out_specs,
        out_shape=[jax.ShapeDtypeStruct((m, D_MODEL), F32)] + out_shape,
        compiler_params=_params(("parallel",)),
        name="out_in_proj",
    )(*ys, w_out_b, x2d, norm_w.reshape(1, D_MODEL), w_r, w_t)


def _flash_update_t(s, c1, shift, vt_ext, m_ref, acc_ref, idx, cols):
    m_old = m_ref[idx, :, cols]
    m_new = jnp.maximum(m_old, c1 * jnp.max(s(), axis=0, keepdims=True) + shift)
    alpha = jnp.exp2(m_old - m_new)
    p = jnp.exp2(c1 * s() - (m_new - shift))
    acc_ref[idx, :, cols] = alpha * acc_ref[idx, :, cols] + _dot(vt_ext, p.astype(BF16))
    m_ref[idx, :, cols] = m_new


def _bf16_pieces(x, n=3):
    out = []
    for _ in range(n):
        p = float(np.asarray(x, np.float32).astype(BF16).astype(np.float32))
        out.append(p)
        x = x - p
    return out


def _key_pos_features(tk):
    lane = lax.broadcasted_iota(jnp.int32, (tk, LANES), 1)
    row = lax.broadcasted_iota(jnp.int32, (tk, LANES), 0)
    out = jnp.zeros((tk, LANES), jnp.int32)
    for d in range(-(-tk // POS_RADIX)):
        digit = jnp.bitwise_and(jnp.right_shift(row, 8 * d), POS_RADIX - 1)
        out = jnp.where((lane >= ALIBI_ROWS * d) & (lane < ALIBI_ROWS * (d + 1)), digit, out)
    return out.astype(F32).astype(BF16)


def _alibi_rows(beta, tq, rows=LANES, tk=POS_RADIX):
    row = lax.broadcasted_iota(jnp.int32, (rows, tq), 0)
    out = jnp.zeros((rows, tq), F32)
    for d in range(-(-tk // POS_RADIX)):
        for r, piece in enumerate(_bf16_pieces(beta * POS_RADIX ** d, ALIBI_ROWS)):
            out = jnp.where(row == ALIBI_ROWS * d + r, piece, out)
    return out.astype(BF16)


def _normalized(acc):
    return acc[0:HEAD_DIM] / jnp.maximum(acc[HEAD_DIM:HEAD_DIM + 1], 1e-30)


def _rel_t(tk, tq):
    return lax.broadcasted_iota(jnp.int32, (tk, tq), 1) - lax.broadcasted_iota(jnp.int32, (tk, tq), 0)


def _diff_kernel(lam_ref, sw_ref, q_ref, k_ref, vt_ref, z_ref, o_ref, m_ref, acc_ref, wq_ref, s_ref,
                 *, tq, tk, lam_init):
    qi = pl.program_id(1)
    kpq = tq // tk
    scale = DIFF_QK_DIM ** -0.5
    c1 = scale * LOG2E
    rel = _rel_t(tk, tq)

    qt = q_ref[...].astype(F32).T.astype(BF16)
    row = lax.broadcasted_iota(jnp.int32, (LANES, tq), 0)
    for j in range(2 * HEADS):
        g, r0 = divmod(j * DIFF_QK_DIM, LANES)
        qg = qt[g * LANES:(g + 1) * LANES]
        wq_ref[j, 0:LANES, :] = jnp.where((row >= r0) & (row < r0 + DIFF_QK_DIM), qg, jnp.zeros_like(qg))
        wq_ref[j, LANES:2 * LANES, :] = _alibi_rows(DIFF_SLOPES[j // 2] / scale, tq, tk=tk)

    ones = jnp.ones((BF16_ROWS, tk), BF16)
    kpos = _key_pos_features(tk)

    n_maps = 2 * HEADS
    all_cols = slice(0, tq)

    def load_keys(ki):
        return k_ref[pl.ds(pl.multiple_of(ki * tk, tk), tk), :]

    def issue(k, j):
        g = (j * DIFF_QK_DIM) // LANES
        lhs = jnp.concatenate([k[:, g * LANES:(g + 1) * LANES], kpos], axis=1)
        s_ref[j] = _dot(lhs, wq_ref[j])

    def tile(ki, mask_off, has_next):
        k = load_keys(ki)
        k_next = load_keys(ki + 1) if has_next else None
        vt = vt_ref[:, pl.ds(pl.multiple_of(ki * tk, tk), tk)]
        key0 = (ki * tk).astype(F32)
        for j in range(n_maps):
            h = j // 2
            ahead = j + QK_LOOKAHEAD
            if ahead < n_maps:
                issue(k, ahead)
            elif has_next:
                issue(k_next, ahead - n_maps)
            if mask_off is None:
                staged = lambda j=j: s_ref[j]
            else:
                staged = lambda j=j: jnp.where(rel >= mask_off, s_ref[j], NEG_INF)
            vt_ext = jnp.concatenate([vt[h * HEAD_DIM:(h + 1) * HEAD_DIM], ones], axis=0)
            _flash_update_t(staged, c1, (DIFF_SLOPES[h] * LOG2E) * key0, vt_ext, m_ref, acc_ref, j, all_cols)

    def body(ki, carry):
        tile(ki, None, True)
        return carry

    k0 = load_keys(0)
    for j in range(QK_LOOKAHEAD):
        issue(k0, j)
    m_ref[...] = jnp.full(m_ref.shape, NEG_INF, F32)
    acc_ref[...] = jnp.zeros(acc_ref.shape, F32)
    lax.fori_loop(0, qi * kpq, body, 0)
    for d in range(kpq):
        tile(qi * kpq + d, d * tk, d + 1 < kpq)

    lp = lam_ref[...]
    lam = (jnp.exp(jnp.sum(lp[0:1] * lp[1:2], axis=-1, keepdims=True))
           - jnp.exp(jnp.sum(lp[2:3] * lp[3:4], axis=-1, keepdims=True)) + lam_init)
    sw = sw_ref[...]
    o_t = jnp.concatenate([_normalized(acc_ref[2 * h]) - lam * _normalized(acc_ref[2 * h + 1])
                           for h in range(HEADS)], axis=0)
    o = o_t.T
    outs = []
    for h in range(HEADS):
        oh = o[:, h * HEAD_DIM:(h + 1) * HEAD_DIM]
        ms = jnp.mean(oh * oh, axis=-1, keepdims=True)
        outs.append(oh * lax.rsqrt(ms + EPS) * sw * (1.0 - lam_init))
    y = jnp.concatenate(outs, axis=-1) * _silu(z_ref[...])
    o_ref[...] = y.astype(o_ref.dtype)


def _diff_attention(qk, vt_all, z_all, lam_p, subln_w, batch, seq, layer_idx, tq, tk):
    nq = seq // tq
    lam_init = 0.8 - 0.6 * math.exp(-0.3 * layer_idx)
    return pl.pallas_call(
        functools.partial(_diff_kernel, tq=tq, tk=tk, lam_init=lam_init),
        grid=(batch, nq),
        in_specs=[pl.BlockSpec((4, DIFF_QK_DIM), lambda b, i: (0, 0)),
                  pl.BlockSpec((1, HEAD_DIM), lambda b, i: (0, 0)),
                  pl.BlockSpec((tq, GROUP_W), lambda b, i: (b * nq + i, 0)),
                  pl.BlockSpec((seq, GROUP_W), lambda b, i: (b, 1)),
                  pl.BlockSpec((GROUP_W, seq), lambda b, i: (DIFF_VT_BLK, b)),
                  pl.BlockSpec((tq, GROUP_W), lambda b, i: (b * nq + i, 1))],
        out_specs=pl.BlockSpec((tq, GROUP_W), lambda b, i: (b * nq + i, 0)),
        out_shape=jax.ShapeDtypeStruct((batch * seq, GROUP_W), BF16),
        scratch_shapes=[pltpu.VMEM((2 * HEADS, 1, tq), F32),
                        pltpu.VMEM((2 * HEADS, ACC_ROWS, tq), F32),
                        pltpu.VMEM((2 * HEADS, 2 * LANES, tq), BF16),
                        pltpu.VMEM((2 * HEADS, tk, tq), F32)],
        compiler_params=_params(("parallel", "parallel")),
        name="diff_attention",
    )(lam_p, subln_w.reshape(1, HEAD_DIM), qk, qk, vt_all, z_all)


def _nsa_compress_kernel(gk_ref, gv_ref, pek_ref, pev_ref, wk1_ref, wk2_ref, wv1_ref, wv2t_ref, kc_ref, vct_ref):
    half = CMP_STRIDE * HEAD_DIM

    def hidden(g_ref, pe_ref, w1_ref):
        g = g_ref[0]
        top = (g + pe_ref[0:1, :]).astype(BF16)
        bot = (g + pe_ref[1:2, :]).astype(BF16)
        a = _dot(top, w1_ref[0:half, :])
        b = _dot(bot, w1_ref[half:2 * half, :])
        nrow = b.shape[0]
        b = pltpu.roll(b, nrow - 1, 0)
        return _silu(a + b).astype(BF16)

    hk = hidden(gk_ref, pek_ref, wk1_ref)
    kc = _dot(hk, wk2_ref[...])
    lane = lax.broadcasted_iota(jnp.int32, kc.shape, 1)
    blk = lax.broadcasted_iota(jnp.int32, kc.shape, 0).astype(F32)
    kc = jnp.where((lane >= HEAD_DIM) & (lane < HEAD_DIM + ALIBI_ROWS), blk, kc)
    kc_ref[0] = kc.astype(kc_ref.dtype)
    hv = hidden(gv_ref, pev_ref, wv1_ref)
    vct_ref[0] = _dot_nt(wv2t_ref[...], hv).astype(vct_ref.dtype)


def _nsa_compress(cmp2d, pe_k, pe_v, w_ck1, w_ck2, w_cv1, w_cv2, batch, seq):
    ng = seq // CMP_STRIDE
    half = CMP_STRIDE * HEAD_DIM
    gk = cmp2d[:, :HEAD_DIM].reshape(batch, ng, half)
    gv = cmp2d[:, HEAD_DIM:].reshape(batch, ng, half)
    gspec = pl.BlockSpec((1, ng, half), lambda b: (b, 0, 0))

    def full(shape):
        return pl.BlockSpec(shape, lambda b: (0,) * len(shape))

    return pl.pallas_call(
        _nsa_compress_kernel,
        grid=(batch,),
        in_specs=[gspec, gspec, full((2, half)), full((2, half)),
                  full((2 * half, CMP_HIDDEN)), full((CMP_HIDDEN, LANES)),
                  full((2 * half, CMP_HIDDEN)), full((HEAD_DIM, CMP_HIDDEN))],
        out_specs=[pl.BlockSpec((1, ng, LANES), lambda b: (b, 0, 0)),
                   pl.BlockSpec((1, HEAD_DIM, ng), lambda b: (b, 0, 0))],
        out_shape=[jax.ShapeDtypeStruct((batch, ng, LANES), BF16),
                   jax.ShapeDtypeStruct((batch, HEAD_DIM, ng), BF16)],
        compiler_params=_params(("parallel",)),
        name="nsa_compress",
    )(gk, gv, pe_k.reshape(2, half), pe_v.reshape(2, half),
      w_ck1.astype(BF16), jnp.pad(w_ck2, ((0, 0), (0, LANES - HEAD_DIM))).astype(BF16),
      w_cv1.astype(BF16), w_cv2.T.astype(BF16))


def _nsa_kernel(ovl_ref, q_ref, k_ref, vt_ref, kc_ref, vct_ref, misc_ref, z_ref, o_ref,
                m_ref, acc_ref, wq_ref, s_ref, sc_ref, *, tq, tk, seq):
    qi = pl.program_id(1)
    kpq = tq // tk
    scale = HEAD_DIM ** -0.5
    c1 = scale * LOG2E
    ng = seq // CMP_STRIDE
    ns = seq // SLC_BLOCK
    top = min(SLC_TOPK, ns)
    m_ref[...] = jnp.full(m_ref.shape, NEG_INF, F32)
    acc_ref[...] = jnp.zeros(acc_ref.shape, F32)
    rel = _rel_t(tk, tq)

    qt = q_ref[...].astype(F32).T.astype(BF16)
    zeros_q = jnp.zeros((HEAD_DIM, tq), BF16)
    for h in range(HEADS):
        qh = qt[h * HEAD_DIM:(h + 1) * HEAD_DIM]
        alibi = _alibi_rows(NSA_SLOPES[h] / scale, tq, tk=tk)
        wq_ref[h, 0:LANES, :] = jnp.concatenate([qh, zeros_q], axis=0)
        wq_ref[h, LANES:2 * LANES, :] = alibi
        wq_ref[HEADS + h, 0:LANES, :] = jnp.concatenate([zeros_q, qh], axis=0)
        wq_ref[HEADS + h, LANES:2 * LANES, :] = alibi

    t_lane = qi * tq + lax.broadcasted_iota(jnp.int32, (1, tq), 1)
    n_sub = lax.broadcasted_iota(jnp.int32, (ng, 1), 0)
    c_valid = (n_sub * CMP_STRIDE + (CMP_LEN - 1)) <= t_lane
    kc = kc_ref[0]
    vct_ext = jnp.concatenate([vct_ref[0], jnp.ones((BF16_ROWS, ng), BF16)], axis=0)
    ovl = ovl_ref[...]
    for h in range(HEADS):
        rhs = jnp.concatenate([qt[h * HEAD_DIM:(h + 1) * HEAD_DIM],
                               _alibi_rows(NSA_SLOPES[h] * CMP_STRIDE / scale, tq, HEAD_DIM)], axis=0)
        sc_ref[h] = jnp.where(c_valid, _dot(kc, rhs), NEG_INF)
    imp = jnp.zeros((ns, tq), F32)
    o_cmp = []
    for h in range(HEADS):
        mx = jnp.max(sc_ref[h], axis=0, keepdims=True)
        mx = jnp.where(mx > NEG_INF, mx, 0.0)
        pb = jnp.exp2(c1 * sc_ref[h] - c1 * mx).astype(BF16)
        o_ext = _dot(vct_ext, pb)
        r = 1.0 / jnp.maximum(o_ext[HEAD_DIM:HEAD_DIM + 1], 1e-30)
        o_cmp.append(o_ext[0:HEAD_DIM] * r)
        imp = imp + _dot(ovl, pb) * r

    def select_blocks():
        j_sub = lax.broadcasted_iota(jnp.int32, (ns, 1), 0)
        j_sub_f = j_sub.astype(F32)
        cur = jnp.right_shift(t_lane, SLC_SHIFT)
        forced = (j_sub == 0) | (j_sub == cur) | (j_sub == cur - 1)
        valid = (j_sub * SLC_BLOCK) <= t_lane
        score = jnp.where(forced, 1e30, jnp.where(valid, imp, -1.0))
        sel = jnp.zeros((ns, tq), F32)
        for _ in range(top):
            mx = jnp.max(score, axis=0, keepdims=True)
            idx = jnp.min(jnp.where(score == mx, j_sub_f, float(ns)), axis=0, keepdims=True)
            pick = j_sub_f == idx
            sel = jnp.where(pick, 1.0, sel)
            score = jnp.where(pick, -2.0, score)
        unsel = jnp.where(sel > 0.5, 0.0, -MASK_BIG).astype(BF16)
        for h in range(HEADS):
            wq_ref[h, LANES + BLK_LANE0:LANES + BLK_LANE0 + ns, :] = unsel

    aux_lane = lax.broadcasted_iota(jnp.int32, (tk, LANES), 1)
    aux_blk = jnp.right_shift(lax.broadcasted_iota(jnp.int32, (tk, LANES), 0), SLC_SHIFT) + BLK_LANE0
    kpos = _key_pos_features(tk)
    ones = jnp.ones((BF16_ROWS, tk), BF16)

    n_slots = s_ref.shape[0]

    def win_tiles(kis, cols):
        loaded = []
        for t, ki in enumerate(kis):
            start = pl.multiple_of(ki * tk, tk)
            k = k_ref[pl.ds(start, tk), :]
            vt = vt_ref[HEAD_DIM:2 * HEAD_DIM, pl.ds(start, tk)]
            dist = rel[:, cols[t]] + (qi * tq - ki * tk)
            loaded.append((jnp.concatenate([k, kpos], axis=1),
                           jnp.concatenate([vt, ones], axis=0), (ki * tk).astype(F32),
                           (dist >= 0) & (dist < WINDOW)))
        items = [(t, h) for t in range(len(kis)) for h in range(HEADS)]

        def scores(i):
            t, h = items[i]
            s_ref[i % n_slots, :, cols[t]] = _dot(loaded[t][0], wq_ref[HEADS + h, :, cols[t]])

        def staged(i):
            t = items[i][0]
            return lambda: jnp.where(loaded[t][3], s_ref[i % n_slots, :, cols[t]], NEG_INF)

        for i in range(QK_LOOKAHEAD):
            scores(i)
        for i, (t, h) in enumerate(items):
            if i + QK_LOOKAHEAD < len(items):
                scores(i + QK_LOOKAHEAD)
            _lhs, vt_ext, key0, _keep = loaded[t]
            _flash_update_t(staged(i), c1, (NSA_SLOPES[h] * LOG2E) * key0, vt_ext, m_ref, acc_ref, HEADS + h,
                            cols[t])

    all_cols = slice(0, tq)
    diag_cols = [slice(d * tk, tq) for d in range(kpq)]

    def slc_lhs(ki):
        k = k_ref[pl.ds(pl.multiple_of(ki * tk, tk), tk), :]
        onehot = aux_lane == aux_blk + ki * (tk // SLC_BLOCK)
        return jnp.concatenate([k, jnp.where(onehot, jnp.ones_like(kpos), kpos)], axis=1)

    def slc_issue(lhs, slot0, cols, h):
        s_ref[slot0 + h, :, cols] = _dot(lhs, wq_ref[h, :, cols])

    def slc_tile(ki, slot0, mask_off, cols, ahead):
        vt = vt_ref[0:HEAD_DIM, pl.ds(pl.multiple_of(ki * tk, tk), tk)]
        vt_ext = jnp.concatenate([vt, ones], axis=0)
        key0 = (ki * tk).astype(F32)
        lhs_ahead = slc_lhs(ahead[0]) if ahead else None
        for h in range(HEADS):
            if ahead:
                slc_issue(lhs_ahead, ahead[1], ahead[2], h)
            if mask_off is None:
                staged = lambda h=h: s_ref[slot0 + h, :, cols]
            else:
                staged = lambda h=h: jnp.where(rel[:, cols] >= mask_off, s_ref[slot0 + h, :, cols], NEG_INF)
            _flash_update_t(staged, c1, (NSA_SLOPES[h] * LOG2E) * key0, vt_ext, m_ref, acc_ref, h, cols)

    def body(i, carry):
        slc_tile(2 * i, 0, None, all_cols, (2 * i + 1, HEADS, all_cols))
        slc_tile(2 * i + 1, HEADS, None, all_cols, (2 * i + 2, 0, all_cols))
        return carry

    win_tiles([qi * kpq + d for d in range(kpq)], diag_cols)
    select_blocks()

    lhs0 = slc_lhs(0)
    for h in range(HEADS):
        slc_issue(lhs0, 0, all_cols, h)
    lax.fori_loop(0, qi, body, 0)
    slc_tile(2 * qi, 0, 0, diag_cols[0], (2 * qi + 1, HEADS, diag_cols[1]))
    slc_tile(2 * qi + 1, HEADS, tk, diag_cols[1], None)

    n_back = (WINDOW + tk - 1) // tk
    for g in range((n_back + kpq - 1) // kpq):
        backs = list(range(g * kpq + 1, min((g + 1) * kpq, n_back) + 1))
        back_cols = [slice(0, min(tq, -(-(WINDOW - (back - 1) * tk - 1) // LANES) * LANES)) for back in backs]

        @pl.when(qi > g)
        def _():
            win_tiles([qi * kpq - back for back in backs], back_cols)

    g_t = jax.nn.sigmoid(misc_ref[...]).T
    outs = []
    for h in range(HEADS):
        r0 = GATE_COL + 3 * h
        outs.append(g_t[r0:r0 + 1] * o_cmp[h] + g_t[r0 + 1:r0 + 2] * _normalized(acc_ref[h])
                    + g_t[r0 + 2:r0 + 3] * _normalized(acc_ref[HEADS + h]))
    y = jnp.concatenate(outs, axis=0).T * _silu(z_ref[...])
    o_ref[...] = y.astype(o_ref.dtype)


def _overlap_t(seq):
    nc = (seq - CMP_LEN) // CMP_STRIDE + 1
    ng = seq // CMP_STRIDE
    ns = seq // SLC_BLOCK
    c_start = np.arange(ng) * CMP_STRIDE
    c_end = c_start + CMP_LEN - 1
    s_start = np.arange(ns) * SLC_BLOCK
    s_end = s_start + SLC_BLOCK - 1
    ov = (c_start[None, :] <= s_end[:, None]) & (c_end[None, :] >= s_start[:, None]) & (np.arange(ng)[None, :] < nc)
    return jnp.asarray(ov.astype(np.float32), dtype=BF16)


def _nsa_attention(q, k2, vt_all, kc, vct, misc, z_all, batch, seq, tq, tk):
    nq = seq // tq
    ng = seq // CMP_STRIDE
    ns = seq // SLC_BLOCK
    assert tq == 2 * tk and tq - tk < WINDOW and tk % SLC_BLOCK == 0 and ns <= LANES - BLK_LANE0
    return pl.pallas_call(
        functools.partial(_nsa_kernel, tq=tq, tk=tk, seq=seq),
        grid=(batch, nq),
        in_specs=[pl.BlockSpec((ns, ng), lambda b, i: (0, 0)),
                  pl.BlockSpec((tq, GROUP_W), lambda b, i: (b * nq + i, 0)),
                  pl.BlockSpec((seq, LANES), lambda b, i: (b, 0)),
                  pl.BlockSpec((2 * HEAD_DIM, seq), lambda b, i: (NSA_VT_BLK, b)),
                  pl.BlockSpec((1, ng, LANES), lambda b, i: (b, 0, 0)),
                  pl.BlockSpec((1, HEAD_DIM, ng), lambda b, i: (b, 0, 0)),
                  pl.BlockSpec((tq, LANES), lambda b, i: (b * nq + i, 0)),
                  pl.BlockSpec((tq, GROUP_W), lambda b, i: (b * nq + i, 0))],
        out_specs=pl.BlockSpec((tq, GROUP_W), lambda b, i: (b * nq + i, 0)),
        out_shape=jax.ShapeDtypeStruct((batch * seq, GROUP_W), BF16),
        scratch_shapes=[pltpu.VMEM((2 * HEADS, 1, tq), F32),
                        pltpu.VMEM((2 * HEADS, ACC_ROWS, tq), F32),
                        pltpu.VMEM((2 * HEADS, 2 * LANES, tq), BF16),
                        pltpu.VMEM((2 * HEADS, tk, tq), F32),
                        pltpu.VMEM((HEADS, ng, tq), F32)],
        compiler_params=_params(("parallel", "parallel")),
        name="nsa_attention",
    )(_overlap_t(seq), q, k2, vt_all, kc, vct, misc, z_all)


def _ret_tables():
    c = RET_CHUNK
    h = np.arange(HEADS, dtype=np.float32)
    log_g = jnp.log(1.0 - 2.0 ** (-5.0 - jnp.asarray(h)))
    pos = jnp.arange(c, dtype=F32)
    rel = pos[:, None] - pos[None, :]
    decay = jnp.where(rel >= 0, jnp.exp(log_g[:, None, None] * jnp.maximum(rel, 0.0)), 0.0)
    xi = jnp.exp(log_g[:, None] * (pos + 1.0))
    zeta = jnp.exp(log_g[:, None] * (c - 1.0 - pos))
    chunk_decay = jnp.exp(log_g * c)
    xi_tab = jnp.repeat(xi.T, HEAD_DIM, axis=1)
    zeta_tab = jnp.repeat(zeta.T, HEAD_DIM, axis=1)
    cd_tab = jnp.repeat(chunk_decay, HEAD_DIM)[None, :]
    return decay, xi_tab, zeta_tab, cd_tab


def _ret_kernel(decay_ref, xi_ref, zeta_ref, cd_ref, gn_ref, q_ref, k_ref, v_ref, z_ref, o_ref, st_ref):
    n = pl.program_id(1)

    @pl.when(n == 0)
    def _():
        st_ref[...] = jnp.zeros(st_ref.shape, F32)

    xi = xi_ref[...]
    cd = cd_ref[...]
    for bb in range(q_ref.shape[0]):
        q = (q_ref[bb].astype(F32) * (HEAD_DIM ** -0.5)).astype(BF16)
        k = k_ref[bb]
        v = v_ref[bb]
        kz_t = (k.astype(F32) * zeta_ref[...]).T.astype(BF16)
        outs = []
        for h in range(HEADS):
            sl = slice(h * HEAD_DIM, (h + 1) * HEAD_DIM)
            qh, kh, vh = q[:, sl], k[:, sl], v[:, sl]
            prev = st_ref[bb * HEADS + h]
            inner = (_dot_nt(qh, kh) * decay_ref[h]).astype(BF16)
            o = _dot(inner, vh) + _dot(qh, prev.astype(BF16)) * xi[:, sl]
            st_ref[bb * HEADS + h] = prev * cd[:, sl] + _dot(kz_t[sl, :], vh)
            mu = jnp.mean(o, axis=-1, keepdims=True)
            d = o - mu
            var = jnp.mean(d * d, axis=-1, keepdims=True)
            outs.append(d * lax.rsqrt(var + EPS))
        y = jnp.concatenate(outs, axis=-1) * gn_ref[...] * _silu(z_ref[bb])
        o_ref[bb] = y.astype(o_ref.dtype)


def _retention(qkv, z_all, gn_w, batch, seq, bpb):
    c = RET_CHUNK
    nch = seq // c
    decay, xi_tab, zeta_tab, cd_tab = _ret_tables()

    def full(shape):
        return pl.BlockSpec(shape, lambda b, n: (0,) * len(shape))

    def blk(col):
        return pl.BlockSpec((bpb, c, GROUP_W), lambda b, n: (b, n, col))

    out = pl.pallas_call(
        _ret_kernel,
        grid=(batch // bpb, nch),
        in_specs=[full((HEADS, c, c)), full((c, GROUP_W)), full((c, GROUP_W)), full((1, GROUP_W)),
                  full((1, GROUP_W)), blk(0), blk(1), blk(2), blk(2)],
        out_specs=blk(0),
        out_shape=jax.ShapeDtypeStruct((batch, seq, GROUP_W), BF16),
        scratch_shapes=[pltpu.VMEM((bpb * HEADS, HEAD_DIM, HEAD_DIM), F32)],
        compiler_params=_params(("parallel", "arbitrary")),
        name="retention",
    )(decay, xi_tab, zeta_tab, cd_tab, gn_w.reshape(1, GROUP_W),
      *(qkv.reshape(batch, seq, -1),) * 3, z_all.reshape(batch, seq, -1))
    return out.reshape(batch * seq, GROUP_W)


def _ssd_kernel(cw_ref, cb_ref, dtb_ref, a_ref, dsk_ref, nw_ref, xbc_ref, misc_ref, z_ref, o_ref,
                ext_ref, st_ref):
    n = pl.program_id(1)
    L = SSM_CHUNK
    hi = lax.Precision.HIGHEST

    @pl.when(n == 0)
    def _():
        st_ref[...] = jnp.zeros(st_ref.shape, F32)
        ext_ref[:, 0:8, :] = jnp.zeros((ext_ref.shape[0], 8, CONV_CH), F32)

    row = lax.broadcasted_iota(jnp.int32, (L, L), 0)
    col = lax.broadcasted_iota(jnp.int32, (L, L), 1)
    causal = row >= col
    tril = jnp.where(causal, 1.0, 0.0).astype(F32)
    dsk = dsk_ref[...]

    for bb in range(xbc_ref.shape[0]):
        raw = xbc_ref[bb]
        ext_ref[bb, 8:8 + L, :] = raw
        conv = cb_ref[...] + raw * cw_ref[CONV_W - 1:CONV_W, :]
        for w in range(CONV_W - 1):
            shift = CONV_W - 1 - w
            conv = conv + ext_ref[bb, 8 - shift:8 - shift + L, :] * cw_ref[w:w + 1, :]
        ext_ref[bb, 0:8, :] = raw[L - 8:L, :]
        xc = _silu(conv)
        x = xc[:, 0:GROUP_W]
        bm = xc[:, GROUP_W:GROUP_W + 2 * SSM_STATE].astype(BF16)
        cm = xc[:, GROUP_W + 2 * SSM_STATE:].astype(BF16)

        dt_full = jax.nn.softplus(misc_ref[bb] + dtb_ref[...])
        da = dt_full * a_ref[...]
        cs_col = jnp.dot(tril, da, precision=hi, preferred_element_type=F32)
        cs_row = lax.dot_general(da, tril, (((0,), (1,)), ((), ())), precision=hi,
                                 preferred_element_type=F32)

        outs = []
        for h in range(HEADS):
            g = h // 2
            c0 = DT_COL + h
            sl = slice(h * HEAD_DIM, (h + 1) * HEAD_DIM)
            gs = slice(g * SSM_STATE, (g + 1) * SSM_STATE)
            cs_c = cs_col[:, c0:c0 + 1]
            cs_r = cs_row[c0:c0 + 1, :]
            cs_last = cs_col[L - 1:L, c0:c0 + 1]
            xh = x[:, sl]
            xdt = xh * dt_full[:, c0:c0 + 1]
            seg = jnp.exp(jnp.where(causal, cs_c - cs_r, NEG_INF))
            cb = _dot_nt(cm[:, gs], bm[:, gs])
            y = _dot((cb * seg).astype(BF16), xdt.astype(BF16))
            prev = st_ref[bb * HEADS + h]
            y = y + _dot(cm[:, gs], prev.astype(BF16)) * jnp.exp(cs_c)
            y = y + dsk[:, sl] * xh
            dec = jnp.exp(cs_last - cs_c)
            st_ref[bb * HEADS + h] = prev * jnp.exp(cs_last) + _dot_tn(bm[:, gs], (xdt * dec).astype(BF16))
            outs.append(y)
        y = jnp.concatenate(outs, axis=-1) * _silu(z_ref[bb])
        ms = jnp.mean(y * y, axis=-1, keepdims=True)
        o_ref[bb] = (y * lax.rsqrt(ms + EPS) * nw_ref[...]).astype(o_ref.dtype)


def _ssd(xbc, misc, z_all, conv_w, conv_b, dt_bias, a_log, d_skip, norm_w, batch, seq, bpb):
    L = SSM_CHUNK
    nch = seq // L
    dtb = jnp.zeros((1, 128), F32).at[0, DT_COL:DT_COL + HEADS].set(dt_bias)
    a_full = jnp.zeros((1, 128), F32).at[0, DT_COL:DT_COL + HEADS].set(-jnp.exp(a_log))
    dsk = jnp.repeat(d_skip, HEAD_DIM)[None, :]

    def full(shape):
        return pl.BlockSpec(shape, lambda b, n: (0,) * len(shape))

    out = pl.pallas_call(
        _ssd_kernel,
        grid=(batch // bpb, nch),
        in_specs=[full((CONV_W, CONV_CH)), full((1, CONV_CH)), full((1, 128)), full((1, 128)),
                  full((1, GROUP_W)), full((1, GROUP_W)),
                  pl.BlockSpec((bpb, L, CONV_CH), lambda b, n: (b, n, 0)),
                  pl.BlockSpec((bpb, L, 128), lambda b, n: (b, n, 0)),
                  pl.BlockSpec((bpb, L, GROUP_W), lambda b, n: (b, n, 3))],
        out_specs=pl.BlockSpec((bpb, L, GROUP_W), lambda b, n: (b, n, 0)),
        out_shape=jax.ShapeDtypeStruct((batch, seq, GROUP_W), BF16),
        scratch_shapes=[pltpu.VMEM((bpb, 8 + L, CONV_CH), F32),
                        pltpu.VMEM((bpb * HEADS, SSM_STATE, HEAD_DIM), F32)],
        compiler_params=_params(("parallel", "arbitrary")),
        name="ssd",
    )(conv_w, conv_b.reshape(1, CONV_CH), dtb, a_full, dsk, norm_w.reshape(1, GROUP_W),
      xbc.reshape(batch, seq, -1), misc.reshape(batch, seq, -1), z_all.reshape(batch, seq, -1))
    return out.reshape(batch * seq, GROUP_W)


def _pick_tile(n, pref):
    t = pref
    while n % t:
        t //= 2
    return t


def kernel(x, norm_w, w_in, w_out, nsa_pe_k, nsa_pe_v, nsa_w_ck1, nsa_w_ck2, nsa_w_cv1, nsa_w_cv2,
           diff_lam_q1, diff_lam_k1, diff_lam_q2, diff_lam_k2, diff_subln_w, ret_gn_w,
           ssm_conv_w, ssm_conv_b, ssm_dt_bias, ssm_A_log, ssm_D, ssm_norm_w, final_norm_w):
    batch, seq, _ = x.shape
    depth = w_in.shape[0]
    m = batch * seq
    tm = _pick_tile(m, 512)
    tq = _pick_tile(seq, 512)
    tk = _pick_tile(seq, 256)
    tk_diff = _pick_tile(seq, 512)
    bpb = _pick_tile(batch, REC_BATCH)
    w_r = _relayout_w_in(w_in)
    w_t = _relayout_w_in_t(w_in)
    w_out_b = w_out.astype(BF16)
    x2d = x.reshape(m, D_MODEL)
    projected = _in_proj(x2d, norm_w[0], w_r[0], w_t[0], tm)
    for i in range(depth):
        nsa_q, nsa_k2, nsa_cmp, misc, z_all, diff_qk, ret_qkv, xbc, vt_all = projected
        kc, vct = _nsa_compress(nsa_cmp, nsa_pe_k[i], nsa_pe_v[i], nsa_w_ck1[i], nsa_w_ck2[i],
                                nsa_w_cv1[i], nsa_w_cv2[i], batch, seq)
        y_nsa = _nsa_attention(nsa_q, nsa_k2, vt_all, kc, vct, misc, z_all, batch, seq, tq, tk)
        lam_p = jnp.stack([diff_lam_q1[i], diff_lam_k1[i], diff_lam_q2[i], diff_lam_k2[i]])
        y_diff = _diff_attention(diff_qk, vt_all, z_all, lam_p, diff_subln_w[i], batch, seq, i, tq, tk_diff)
        y_ret = _retention(ret_qkv, z_all, ret_gn_w[i], batch, seq, bpb)
        y_ssm = _ssd(xbc, misc, z_all, ssm_conv_w[i], ssm_conv_b[i], ssm_dt_bias[i], ssm_A_log[i],
                     ssm_D[i], ssm_norm_w[i], batch, seq, bpb)
        ys = (y_nsa, y_diff, y_ret, y_ssm)
        if i + 1 < depth:
            x2d, *projected = _out_in_proj(ys, w_out_b[i], x2d, norm_w[i + 1], w_r[i + 1], w_t[i + 1], tm)
        else:
            x2d = _out_proj(ys, w_out_b[i], x2d, final_norm_w, tm)
    return x2d.reshape(batch, seq, D_MODEL)
```

```python
import functools
import math

import numpy as np
import jax
import jax.numpy as jnp
from jax import lax
from jax.experimental import pallas as pl
from jax.experimental.pallas import tpu as pltpu

F32 = jnp.float32
BF16 = jnp.bfloat16
NEG_INF = float("-inf")
LOG2E = 1.4426950408889634

D_MODEL = 1024
DEPTH = 4
GROUP_W = 256
HEADS = 4
HEAD_DIM = 64
EPS = 1e-6
CMP_LEN = 32
CMP_STRIDE = 16
CMP_HIDDEN = 256
SLC_BLOCK = 64
SLC_SHIFT = 6
SLC_TOPK = 16
WINDOW = 512
DIFF_QK_DIM = 32
RET_CHUNK = 128
SSM_STATE = 128
SSM_CHUNK = 128
CONV_W = 4
CONV_CH = 768
N_ALIBI_HEADS = 8
LANES = 128
BF16_ROWS = 16
ACC_ROWS = HEAD_DIM + BF16_ROWS
QK_LOOKAHEAD = 6
ALIBI_ROWS = 3
POS_RADIX = 256
MASK_BIG = 2.0 ** 100
BLK_LANE0 = 64
REC_BATCH = 4

IN_LAYOUT = (
    ("nsa_q", 256), ("nsa_k_cmp", 64), ("nsa_v_cmp", 64), ("nsa_k_slc", 64), ("nsa_v_slc", 64),
    ("nsa_k_win", 64), ("nsa_v_win", 64), ("nsa_gate", 12), ("nsa_z", 256),
    ("diff_q", 256), ("diff_k", 256), ("diff_v", 256), ("diff_z", 256),
    ("ret_q", 256), ("ret_k", 256), ("ret_v", 256), ("ret_z", 256),
    ("ssm_z", 256), ("ssm_xbc", 768), ("ssm_dt", 4),
)
IN_OFF = {}
_o = 0
for _n, _w in IN_LAYOUT:
    IN_OFF[_n] = (_o, _w)
    _o += _w
IN_W = _o

GATE_COL = 0
DT_COL = 12
IN_OUTPUTS = (
    ("nsa_q", BF16, ("nsa_q",), 256),
    ("nsa_k2", BF16, ("nsa_k_slc", "nsa_k_win"), 128),
    ("nsa_cmp", F32, ("nsa_k_cmp", "nsa_v_cmp"), 128),
    ("misc", F32, ("nsa_gate", "ssm_dt"), 128),
    ("z_all", F32, ("nsa_z", "diff_z", "ret_z", "ssm_z"), 1024),
    ("diff_qk", BF16, ("diff_q", "diff_k"), 512),
    ("ret_qkv", BF16, ("ret_q", "ret_k", "ret_v"), 768),
    ("xbc", F32, ("ssm_xbc",), 768),
)
IN_T_SRC = ("diff_v", "nsa_v_slc", "nsa_v_win")
IN_T_ROWS = 384
DIFF_VT_BLK = 0
NSA_VT_BLK = 2
IN_SEGS = []
_o = 0
for _n, _dt, _src, _w in IN_OUTPUTS:
    IN_SEGS.append((_o, _o + _w))
    _o += _w
IN_WP = _o

VMEM_LIMIT = 56 * 1024 * 1024


def _alibi_slopes():
    return [2.0 ** (-8.0 * (i + 1) / N_ALIBI_HEADS) for i in range(N_ALIBI_HEADS)]


NSA_SLOPES = _alibi_slopes()[0::2]
DIFF_SLOPES = _alibi_slopes()[1::2]


def _silu(x):
    return x * jax.nn.sigmoid(x)


def _dot(a, b):
    return jnp.dot(a, b, preferred_element_type=F32)


def _dot_nt(a, b):
    return lax.dot_general(a, b, (((1,), (1,)), ((), ())), preferred_element_type=F32)


def _dot_tn(a, b):
    return lax.dot_general(a, b, (((0,), (0,)), ((), ())), preferred_element_type=F32)


def _params(sem):
    return pltpu.CompilerParams(dimension_semantics=sem, vmem_limit_bytes=VMEM_LIMIT)


def _relayout_w_in(w_in):
    cols = []
    for _n, _dt, src, width in IN_OUTPUTS:
        used = 0
        for s in src:
            off, w = IN_OFF[s]
            cols.append(w_in[:, :, off:off + w])
            used += w
        if used < width:
            cols.append(jnp.zeros(w_in.shape[:2] + (width - used,), w_in.dtype))
    return jnp.concatenate(cols, axis=-1).astype(BF16)


def _relayout_w_in_t(w_in):
    cols = [w_in[:, :, IN_OFF[s][0]:IN_OFF[s][0] + IN_OFF[s][1]] for s in IN_T_SRC]
    return jnp.swapaxes(jnp.concatenate(cols, axis=-1), 1, 2).astype(BF16)


def _norm_project(x, nw_ref, w_ref, wt_ref, out_refs):
    ms = jnp.mean(x * x, axis=-1, keepdims=True)
    h = (x * lax.rsqrt(ms + EPS) * nw_ref[...]).astype(BF16)
    for ref, (a, b) in zip(out_refs[:-1], IN_SEGS):
        ref[...] = _dot(h, w_ref[:, a:b]).astype(ref.dtype)
    out_refs[-1][...] = _dot_nt(wt_ref[...], h).astype(BF16)


def _in_proj_kernel(x_ref, nw_ref, w_ref, wt_ref, *out_refs):
    _norm_project(x_ref[...], nw_ref, w_ref, wt_ref, out_refs)


def _in_proj_outputs(m, tm):
    out_shape = [jax.ShapeDtypeStruct((m, b - a), dt) for (_n, dt, _s, _w), (a, b) in zip(IN_OUTPUTS, IN_SEGS)]
    out_specs = [pl.BlockSpec((tm, b - a), lambda i: (i, 0)) for (a, b) in IN_SEGS]
    out_shape.append(jax.ShapeDtypeStruct((IN_T_ROWS, m), BF16))
    out_specs.append(pl.BlockSpec((IN_T_ROWS, tm), lambda i: (0, i)))
    return out_shape, out_specs


def _in_proj(x2d, norm_w, w_r, w_t, tm):
    m = x2d.shape[0]
    out_shape, out_specs = _in_proj_outputs(m, tm)
    return pl.pallas_call(
        _in_proj_kernel,
        grid=(m // tm,),
        in_specs=[pl.BlockSpec((tm, D_MODEL), lambda i: (i, 0)),
                  pl.BlockSpec((1, D_MODEL), lambda i: (0, 0)),
                  pl.BlockSpec((D_MODEL, IN_WP), lambda i: (0, 0)),
                  pl.BlockSpec((IN_T_ROWS, D_MODEL), lambda i: (0, 0))],
        out_specs=out_specs,
        out_shape=out_shape,
        compiler_params=_params(("parallel",)),
        name="in_proj",
    )(x2d, norm_w.reshape(1, D_MODEL), w_r, w_t)


def _out_proj_kernel(y0_ref, y1_ref, y2_ref, y3_ref, w_ref, x_ref, fw_ref, o_ref):
    acc = x_ref[...]
    for g, y_ref in enumerate((y0_ref, y1_ref, y2_ref, y3_ref)):
        acc = acc + _dot(y_ref[...], w_ref[g * GROUP_W:(g + 1) * GROUP_W, :])
    ms = jnp.mean(acc * acc, axis=-1, keepdims=True)
    o_ref[...] = acc * lax.rsqrt(ms + EPS) * fw_ref[...]


def _out_proj(ys, w_out_b, x2d, final_w, tm):
    m = x2d.shape[0]
    yspec = pl.BlockSpec((tm, GROUP_W), lambda i: (i, 0))
    return pl.pallas_call(
        _out_proj_kernel,
        grid=(m // tm,),
        in_specs=[yspec, yspec, yspec, yspec,
                  pl.BlockSpec((D_MODEL, D_MODEL), lambda i: (0, 0)),
                  pl.BlockSpec((tm, D_MODEL), lambda i: (i, 0)),
                  pl.BlockSpec((1, D_MODEL), lambda i: (0, 0))],
        out_specs=pl.BlockSpec((tm, D_MODEL), lambda i: (i, 0)),
        out_shape=jax.ShapeDtypeStruct((m, D_MODEL), F32),
        compiler_params=_params(("parallel",)),
        name="out_proj",
    )(*ys, w_out_b, x2d, final_w.reshape(1, D_MODEL))


def _out_in_proj_kernel(y0_ref, y1_ref, y2_ref, y3_ref, wo_ref, x_ref, nw_ref, w_ref, wt_ref, xo_ref, *out_refs):
    acc = x_ref[...]
    for g, y_ref in enumerate((y0_ref, y1_ref, y2_ref, y3_ref)):
        acc = acc + _dot(y_ref[...], wo_ref[g * GROUP_W:(g + 1) * GROUP_W, :])
    xo_ref[...] = acc
    _norm_project(acc, nw_ref, w_ref, wt_ref, out_refs)


def _out_in_proj(ys, w_out_b, x2d, norm_w, w_r, w_t, tm):
    m = x2d.shape[0]
    yspec = pl.BlockSpec((tm, GROUP_W), lambda i: (i, 0))
    xspec = pl.BlockSpec((tm, D_MODEL), lambda i: (i, 0))
    out_shape, out_specs = _in_proj_outputs(m, tm)
    return pl.pallas_call(
        _out_in_proj_kernel,
        grid=(m // tm,),
        in_specs=[yspec, yspec, yspec, yspec,
                  pl.BlockSpec((D_MODEL, D_MODEL), lambda i: (0, 0)),
                  xspec,
                  pl.BlockSpec((1, D_MODEL), lambda i: (0, 0)),
                  pl.BlockSpec((D_MODEL, IN_WP), lambda i: (0, 0)),
                  pl.BlockSpec((IN_T_ROWS, D_MODEL), lambda i: (0, 0))],
        out_specs=[xspec] + out_specs,
        out_shape=[jax.ShapeDtypeStruct((m, D_MODEL), F32)] + out_shape,
        compiler_params=_params(("parallel",)),
        name="out_in_proj",
    )(*ys, w_out_b, x2d, norm_w.reshape(1, D_MODEL), w_r, w_t)


def _stage_scores(s_ref, slot, cols, s, keep=None):
    if keep is not None:
        s = jnp.where(keep, s, NEG_INF)
    s_ref[slot, :, cols] = s
    return jnp.max(s, axis=0, keepdims=True)


def _flash_update_t(s, s_max, c1, shift, vt_ext, m_ref, acc_ref, idx, cols):
    m_old = m_ref[idx, :, cols]
    m_new = jnp.maximum(m_old, c1 * s_max + shift)
    alpha = jnp.exp2(m_old - m_new)
    p = jnp.exp2(c1 * s() - (m_new - shift))
    acc_ref[idx, :, cols] = alpha * acc_ref[idx, :, cols] + _dot(vt_ext, p.astype(BF16))
    m_ref[idx, :, cols] = m_new


def _bf16_pieces(x, n=3):
    out = []
    for _ in range(n):
        p = float(np.asarray(x, np.float32).astype(BF16).astype(np.float32))
        out.append(p)
        x = x - p
    return out


def _key_pos_features(tk):
    lane = lax.broadcasted_iota(jnp.int32, (tk, LANES), 1)
    row = lax.broadcasted_iota(jnp.int32, (tk, LANES), 0)
    out = jnp.zeros((tk, LANES), jnp.int32)
    for d in range(-(-tk // POS_RADIX)):
        digit = jnp.bitwise_and(jnp.right_shift(row, 8 * d), POS_RADIX - 1)
        out = jnp.where((lane >= ALIBI_ROWS * d) & (lane < ALIBI_ROWS * (d + 1)), digit, out)
    return out.astype(F32).astype(BF16)


def _alibi_rows(beta, tq, rows=LANES, tk=POS_RADIX):
    row = lax.broadcasted_iota(jnp.int32, (rows, tq), 0)
    out = jnp.zeros((rows, tq), F32)
    for d in range(-(-tk // POS_RADIX)):
        for r, piece in enumerate(_bf16_pieces(beta * POS_RADIX ** d, ALIBI_ROWS)):
            out = jnp.where(row == ALIBI_ROWS * d + r, piece, out)
    return out.astype(BF16)


def _normalized(acc):
    return acc[0:HEAD_DIM] / jnp.maximum(acc[HEAD_DIM:HEAD_DIM + 1], 1e-30)


def _rel_t(tk, tq):
    return lax.broadcasted_iota(jnp.int32, (tk, tq), 1) - lax.broadcasted_iota(jnp.int32, (tk, tq), 0)


def _diff_kernel(lam_ref, sw_ref, q_ref, k_ref, vt_ref, z_ref, o_ref, m_ref, acc_ref, wq_ref, s_ref,
                 *, tq, tk, lam_init):
    qi = pl.program_id(1)
    kpq = tq // tk
    scale = DIFF_QK_DIM ** -0.5
    c1 = scale * LOG2E
    m_ref[...] = jnp.full(m_ref.shape, NEG_INF, F32)
    acc_ref[...] = jnp.zeros(acc_ref.shape, F32)
    rel = _rel_t(tk, tq)

    qt = q_ref[...].astype(F32).T.astype(BF16)
    row = lax.broadcasted_iota(jnp.int32, (LANES, tq), 0)
    for j in range(2 * HEADS):
        g, r0 = divmod(j * DIFF_QK_DIM, LANES)
        qg = qt[g * LANES:(g + 1) * LANES]
        wq_ref[j, 0:LANES, :] = jnp.where((row >= r0) & (row < r0 + DIFF_QK_DIM), qg, jnp.zeros_like(qg))
        wq_ref[j, LANES:2 * LANES, :] = _alibi_rows(DIFF_SLOPES[j // 2] / scale, tq, tk=tk)

    ones = jnp.ones((BF16_ROWS, tk), BF16)
    kpos = _key_pos_features(tk)

    n_maps = 2 * HEADS
    n_slots = s_ref.shape[0]

    def tiles(kis, mask_offs):
        loaded = []
        for ki in kis:
            start = pl.multiple_of(ki * tk, tk)
            loaded.append((k_ref[pl.ds(start, tk), :],
                           vt_ref[:, pl.ds(start, tk)],
                           (ki * tk).astype(F32)))
        items = [(t, j) for t in range(len(kis)) for j in range(n_maps)]
        cols = slice(0, tq)

        def scores(i):
            t, j = items[i]
            g = (j * DIFF_QK_DIM) // LANES
            lhs = jnp.concatenate([loaded[t][0][:, g * LANES:(g + 1) * LANES], kpos], axis=1)
            s_ref[i % n_slots] = _dot(lhs, wq_ref[j])

        def staged(i):
            t = items[i][0]
            if mask_offs[t] is None:
                return lambda: s_ref[i % n_slots]
            return lambda: jnp.where(rel >= mask_offs[t], s_ref[i % n_slots], NEG_INF)

        for i in range(QK_LOOKAHEAD):
            scores(i)
        for i, (t, j) in enumerate(items):
            h = j // 2
            if i + QK_LOOKAHEAD < len(items):
                scores(i + QK_LOOKAHEAD)
            _k, vt, key0 = loaded[t]
            vt_ext = jnp.concatenate([vt[h * HEAD_DIM:(h + 1) * HEAD_DIM], ones], axis=0)
            s = staged(i)
            _flash_update_t(s, jnp.max(s(), axis=0, keepdims=True), c1, (DIFF_SLOPES[h] * LOG2E) * key0,
                            vt_ext, m_ref, acc_ref, j, cols)

    def body(i, carry):
        tiles([i * kpq + d for d in range(kpq)], [None] * kpq)
        return carry

    lax.fori_loop(0, qi, body, 0)
    tiles([qi * kpq + d for d in range(kpq)], [d * tk for d in range(kpq)])

    lp = lam_ref[...]
    lam = (jnp.exp(jnp.sum(lp[0:1] * lp[1:2], axis=-1, keepdims=True))
           - jnp.exp(jnp.sum(lp[2:3] * lp[3:4], axis=-1, keepdims=True)) + lam_init)
    sw = sw_ref[...]
    o_t = jnp.concatenate([_normalized(acc_ref[2 * h]) - lam * _normalized(acc_ref[2 * h + 1])
                           for h in range(HEADS)], axis=0)
    o = o_t.T
    outs = []
    for h in range(HEADS):
        oh = o[:, h * HEAD_DIM:(h + 1) * HEAD_DIM]
        ms = jnp.mean(oh * oh, axis=-1, keepdims=True)
        outs.append(oh * lax.rsqrt(ms + EPS) * sw * (1.0 - lam_init))
    y = jnp.concatenate(outs, axis=-1) * _silu(z_ref[...])
    o_ref[...] = y.astype(o_ref.dtype)


def _diff_attention(qk, vt_all, z_all, lam_p, subln_w, batch, seq, layer_idx, tq, tk):
    nq = seq // tq
    lam_init = 0.8 - 0.6 * math.exp(-0.3 * layer_idx)
    return pl.pallas_call(
        functools.partial(_diff_kernel, tq=tq, tk=tk, lam_init=lam_init),
        grid=(batch, nq),
        in_specs=[pl.BlockSpec((4, DIFF_QK_DIM), lambda b, i: (0, 0)),
                  pl.BlockSpec((1, HEAD_DIM), lambda b, i: (0, 0)),
                  pl.BlockSpec((tq, GROUP_W), lambda b, i: (b * nq + i, 0)),
                  pl.BlockSpec((seq, GROUP_W), lambda b, i: (b, 1)),
                  pl.BlockSpec((GROUP_W, seq), lambda b, i: (DIFF_VT_BLK, b)),
                  pl.BlockSpec((tq, GROUP_W), lambda b, i: (b * nq + i, 1))],
        out_specs=pl.BlockSpec((tq, GROUP_W), lambda b, i: (b * nq + i, 0)),
        out_shape=jax.ShapeDtypeStruct((batch * seq, GROUP_W), BF16),
        scratch_shapes=[pltpu.VMEM((2 * HEADS, 1, tq), F32),
                        pltpu.VMEM((2 * HEADS, ACC_ROWS, tq), F32),
                        pltpu.VMEM((2 * HEADS, 2 * LANES, tq), BF16),
                        pltpu.VMEM((QK_LOOKAHEAD + 1, tk, tq), F32)],
        compiler_params=_params(("parallel", "parallel")),
        name="diff_attention",
    )(lam_p, subln_w.reshape(1, HEAD_DIM), qk, qk, vt_all, z_all)


def _nsa_compress_kernel(gk_ref, gv_ref, pek_ref, pev_ref, wk1_ref, wk2_ref, wv1_ref, wv2t_ref, kc_ref, vct_ref):
    half = CMP_STRIDE * HEAD_DIM

    def hidden(g_ref, pe_ref, w1_ref):
        g = g_ref[0]
        top = (g + pe_ref[0:1, :]).astype(BF16)
        bot = (g + pe_ref[1:2, :]).astype(BF16)
        a = _dot(top, w1_ref[0:half, :])
        b = _dot(bot, w1_ref[half:2 * half, :])
        nrow = b.shape[0]
        b = pltpu.roll(b, nrow - 1, 0)
        return _silu(a + b).astype(BF16)

    hk = hidden(gk_ref, pek_ref, wk1_ref)
    kc = _dot(hk, wk2_ref[...])
    lane = lax.broadcasted_iota(jnp.int32, kc.shape, 1)
    blk = lax.broadcasted_iota(jnp.int32, kc.shape, 0).astype(F32)
    kc = jnp.where((lane >= HEAD_DIM) & (lane < HEAD_DIM + ALIBI_ROWS), blk, kc)
    kc_ref[0] = kc.astype(kc_ref.dtype)
    hv = hidden(gv_ref, pev_ref, wv1_ref)
    vct_ref[0] = _dot_nt(wv2t_ref[...], hv).astype(vct_ref.dtype)


def _nsa_compress(cmp2d, pe_k, pe_v, w_ck1, w_ck2, w_cv1, w_cv2, batch, seq):
    ng = seq // CMP_STRIDE
    half = CMP_STRIDE * HEAD_DIM
    gk = cmp2d[:, :HEAD_DIM].reshape(batch, ng, half)
    gv = cmp2d[:, HEAD_DIM:].reshape(batch, ng, half)
    gspec = pl.BlockSpec((1, ng, half), lambda b: (b, 0, 0))

    def full(shape):
        return pl.BlockSpec(shape, lambda b: (0,) * len(shape))

    return pl.pallas_call(
        _nsa_compress_kernel,
        grid=(batch,),
        in_specs=[gspec, gspec, full((2, half)), full((2, half)),
                  full((2 * half, CMP_HIDDEN)), full((CMP_HIDDEN, LANES)),
                  full((2 * half, CMP_HIDDEN)), full((HEAD_DIM, CMP_HIDDEN))],
        out_specs=[pl.BlockSpec((1, ng, LANES), lambda b: (b, 0, 0)),
                   pl.BlockSpec((1, HEAD_DIM, ng), lambda b: (b, 0, 0))],
        out_shape=[jax.ShapeDtypeStruct((batch, ng, LANES), BF16),
                   jax.ShapeDtypeStruct((batch, HEAD_DIM, ng), BF16)],
        compiler_params=_params(("parallel",)),
        name="nsa_compress",
    )(gk, gv, pe_k.reshape(2, half), pe_v.reshape(2, half),
      w_ck1.astype(BF16), jnp.pad(w_ck2, ((0, 0), (0, LANES - HEAD_DIM))).astype(BF16),
      w_cv1.astype(BF16), w_cv2.T.astype(BF16))


def _nsa_kernel(ovl_ref, q_ref, k_ref, vt_ref, kc_ref, vct_ref, misc_ref, z_ref, o_ref,
                m_ref, acc_ref, wq_ref, s_ref, sc_ref, *, tq, tk, seq):
    qi = pl.program_id(1)
    kpq = tq // tk
    scale = HEAD_DIM ** -0.5
    c1 = scale * LOG2E
    ng = seq // CMP_STRIDE
    ns = seq // SLC_BLOCK
    top = min(SLC_TOPK, ns)
    m_ref[...] = jnp.full(m_ref.shape, NEG_INF, F32)
    acc_ref[...] = jnp.zeros(acc_ref.shape, F32)
    rel = _rel_t(tk, tq)

    qt = q_ref[...].astype(F32).T.astype(BF16)
    zeros_q = jnp.zeros((HEAD_DIM, tq), BF16)
    for h in range(HEADS):
        qh = qt[h * HEAD_DIM:(h + 1) * HEAD_DIM]
        alibi = _alibi_rows(NSA_SLOPES[h] / scale, tq, tk=tk)
        wq_ref[h, 0:LANES, :] = jnp.concatenate([qh, zeros_q], axis=0)
        wq_ref[h, LANES:2 * LANES, :] = alibi
        wq_ref[HEADS + h, 0:LANES, :] = jnp.concatenate([zeros_q, qh], axis=0)
        wq_ref[HEADS + h, LANES:2 * LANES, :] = alibi

    t_lane = qi * tq + lax.broadcasted_iota(jnp.int32, (1, tq), 1)
    n_sub = lax.broadcasted_iota(jnp.int32, (ng, 1), 0)
    c_valid = (n_sub * CMP_STRIDE + (CMP_LEN - 1)) <= t_lane
    kc = kc_ref[0]
    vct_ext = jnp.concatenate([vct_ref[0], jnp.ones((BF16_ROWS, ng), BF16)], axis=0)
    ovl = ovl_ref[...]
    cmp_max = []
    for h in range(HEADS):
        rhs = jnp.concatenate([qt[h * HEAD_DIM:(h + 1) * HEAD_DIM],
                               _alibi_rows(NSA_SLOPES[h] * CMP_STRIDE / scale, tq, HEAD_DIM)], axis=0)
        cmp_max.append(_stage_scores(sc_ref, h, slice(0, tq), _dot(kc, rhs), c_valid))
    imp = jnp.zeros((ns, tq), F32)
    o_cmp = []
    for h in range(HEADS):
        mx = jnp.where(cmp_max[h] > NEG_INF, cmp_max[h], 0.0)
        pb = jnp.exp2(c1 * sc_ref[h] - c1 * mx).astype(BF16)
        o_ext = _dot(vct_ext, pb)
        r = 1.0 / jnp.maximum(o_ext[HEAD_DIM:HEAD_DIM + 1], 1e-30)
        o_cmp.append(o_ext[0:HEAD_DIM] * r)
        imp = imp + _dot(ovl, pb) * r

    def select_blocks():
        j_sub = lax.broadcasted_iota(jnp.int32, (ns, 1), 0)
        j_sub_f = j_sub.astype(F32)
        cur = jnp.right_shift(t_lane, SLC_SHIFT)
        forced = (j_sub == 0) | (j_sub == cur) | (j_sub == cur - 1)
        valid = (j_sub * SLC_BLOCK) <= t_lane
        score = jnp.where(forced, 1e30, jnp.where(valid, imp, -1.0))
        sel = jnp.zeros((ns, tq), F32)
        for _ in range(top):
            mx = jnp.max(score, axis=0, keepdims=True)
            idx = jnp.min(jnp.where(score == mx, j_sub_f, float(ns)), axis=0, keepdims=True)
            pick = j_sub_f == idx
            sel = jnp.where(pick, 1.0, sel)
            score = jnp.where(pick, -2.0, score)
        unsel = jnp.where(sel > 0.5, 0.0, -MASK_BIG).astype(BF16)
        for h in range(HEADS):
            wq_ref[h, LANES + BLK_LANE0:LANES + BLK_LANE0 + ns, :] = unsel

    aux_lane = lax.broadcasted_iota(jnp.int32, (tk, LANES), 1)
    aux_blk = jnp.right_shift(lax.broadcasted_iota(jnp.int32, (tk, LANES), 0), SLC_SHIFT) + BLK_LANE0
    kpos = _key_pos_features(tk)
    ones = jnp.ones((BF16_ROWS, tk), BF16)

    n_slots = s_ref.shape[0]

    def tiles(window, kis, cols, mask_offs=None):
        base = HEADS if window else 0
        loaded = []
        for t, ki in enumerate(kis):
            start = pl.multiple_of(ki * tk, tk)
            k = k_ref[pl.ds(start, tk), :]
            if window:
                vt = vt_ref[HEAD_DIM:2 * HEAD_DIM, pl.ds(start, tk)]
                aux = kpos
                dist = rel[:, cols[t]] + (qi * tq - ki * tk)
                keep = (dist >= 0) & (dist < WINDOW)
            else:
                vt = vt_ref[0:HEAD_DIM, pl.ds(start, tk)]
                onehot = aux_lane == aux_blk + ki * (tk // SLC_BLOCK)
                aux = jnp.where(onehot, jnp.ones_like(kpos), kpos)
                keep = None
            loaded.append((jnp.concatenate([k, aux], axis=1),
                           jnp.concatenate([vt, ones], axis=0), (ki * tk).astype(F32), keep))
        items = [(t, h) for t in range(len(kis)) for h in range(HEADS)]
        s_max = {}

        def scores(i):
            t, h = items[i]
            if window:
                keep = loaded[t][3]
            else:
                keep = None if mask_offs[t] is None else rel[:, cols[t]] >= mask_offs[t]
            s_max[i] = _stage_scores(s_ref, i % n_slots, cols[t],
                                     _dot(loaded[t][0], wq_ref[base + h, :, cols[t]]), keep)

        for i in range(QK_LOOKAHEAD):
            scores(i)
        for i, (t, h) in enumerate(items):
            if i + QK_LOOKAHEAD < len(items):
                scores(i + QK_LOOKAHEAD)
            _lhs, vt_ext, key0, _keep = loaded[t]
            _flash_update_t(lambda i=i, t=t: s_ref[i % n_slots, :, cols[t]], s_max.pop(i), c1,
                            (NSA_SLOPES[h] * LOG2E) * key0, vt_ext, m_ref, acc_ref, base + h, cols[t])

    all_cols = slice(0, tq)
    diag_cols = [slice(d * tk, tq) for d in range(kpq)]

    def body(i, carry):
        tiles(False, [i * kpq + d for d in range(kpq)], [all_cols] * kpq, [None] * kpq)
        return carry

    tiles(True, [qi * kpq + d for d in range(kpq)], diag_cols)
    select_blocks()

    lax.fori_loop(0, qi, body, 0)
    tiles(False, [qi * kpq + d for d in range(kpq)], diag_cols, [d * tk for d in range(kpq)])

    n_back = (WINDOW + tk - 1) // tk
    for g in range((n_back + kpq - 1) // kpq):
        backs = list(range(g * kpq + 1, min((g + 1) * kpq, n_back) + 1))
        back_cols = [slice(0, min(tq, -(-(WINDOW - (back - 1) * tk - 1) // LANES) * LANES)) for back in backs]

        @pl.when(qi > g)
        def _():
            tiles(True, [qi * kpq - back for back in backs], back_cols)

    g_t = jax.nn.sigmoid(misc_ref[...]).T
    outs = []
    for h in range(HEADS):
        r0 = GATE_COL + 3 * h
        outs.append(g_t[r0:r0 + 1] * o_cmp[h] + g_t[r0 + 1:r0 + 2] * _normalized(acc_ref[h])
                    + g_t[r0 + 2:r0 + 3] * _normalized(acc_ref[HEADS + h]))
    y = jnp.concatenate(outs, axis=0).T * _silu(z_ref[...])
    o_ref[...] = y.astype(o_ref.dtype)


def _overlap_t(seq):
    nc = (seq - CMP_LEN) // CMP_STRIDE + 1
    ng = seq // CMP_STRIDE
    ns = seq // SLC_BLOCK
    c_start = np.arange(ng) * CMP_STRIDE
    c_end = c_start + CMP_LEN - 1
    s_start = np.arange(ns) * SLC_BLOCK
    s_end = s_start + SLC_BLOCK - 1
    ov = (c_start[None, :] <= s_end[:, None]) & (c_end[None, :] >= s_start[:, None]) & (np.arange(ng)[None, :] < nc)
    return jnp.asarray(ov.astype(np.float32), dtype=BF16)


def _nsa_attention(q, k2, vt_all, kc, vct, misc, z_all, batch, seq, tq, tk):
    nq = seq // tq
    ng = seq // CMP_STRIDE
    ns = seq // SLC_BLOCK
    assert tq - tk < WINDOW and tk % SLC_BLOCK == 0 and ns <= LANES - BLK_LANE0
    return pl.pallas_call(
        functools.partial(_nsa_kernel, tq=tq, tk=tk, seq=seq),
        grid=(batch, nq),
        in_specs=[pl.BlockSpec((ns, ng), lambda b, i: (0, 0)),
                  pl.BlockSpec((tq, GROUP_W), lambda b, i: (b * nq + i, 0)),
                  pl.BlockSpec((seq, LANES), lambda b, i: (b, 0)),
                  pl.BlockSpec((2 * HEAD_DIM, seq), lambda b, i: (NSA_VT_BLK, b)),
                  pl.BlockSpec((1, ng, LANES), lambda b, i: (b, 0, 0)),
                  pl.BlockSpec((1, HEAD_DIM, ng), lambda b, i: (b, 0, 0)),
                  pl.BlockSpec((tq, LANES), lambda b, i: (b * nq + i, 0)),
                  pl.BlockSpec((tq, GROUP_W), lambda b, i: (b * nq + i, 0))],
        out_specs=pl.BlockSpec((tq, GROUP_W), lambda b, i: (b * nq + i, 0)),
        out_shape=jax.ShapeDtypeStruct((batch * seq, GROUP_W), BF16),
        scratch_shapes=[pltpu.VMEM((2 * HEADS, 1, tq), F32),
                        pltpu.VMEM((2 * HEADS, ACC_ROWS, tq), F32),
                        pltpu.VMEM((2 * HEADS, 2 * LANES, tq), BF16),
                        pltpu.VMEM((2 * HEADS, tk, tq), F32),
                        pltpu.VMEM((HEADS, ng, tq), F32)],
        compiler_params=_params(("parallel", "parallel")),
        name="nsa_attention",
    )(_overlap_t(seq), q, k2, vt_all, kc, vct, misc, z_all)


def _ret_tables():
    c = RET_CHUNK
    h = np.arange(HEADS, dtype=np.float32)
    log_g = jnp.log(1.0 - 2.0 ** (-5.0 - jnp.asarray(h)))
    pos = jnp.arange(c, dtype=F32)
    rel = pos[:, None] - pos[None, :]
    decay = jnp.where(rel >= 0, jnp.exp(log_g[:, None, None] * jnp.maximum(rel, 0.0)), 0.0)
    xi = jnp.exp(log_g[:, None] * (pos + 1.0))
    zeta = jnp.exp(log_g[:, None] * (c - 1.0 - pos))
    chunk_decay = jnp.exp(log_g * c)
    xi_tab = jnp.repeat(xi.T, HEAD_DIM, axis=1)
    zeta_tab = jnp.repeat(zeta.T, HEAD_DIM, axis=1)
    cd_tab = jnp.repeat(chunk_decay, HEAD_DIM)[None, :]
    return decay, xi_tab, zeta_tab, cd_tab


def _ret_kernel(decay_ref, xi_ref, zeta_ref, cd_ref, gn_ref, q_ref, k_ref, v_ref, z_ref, o_ref, st_ref):
    n = pl.program_id(1)

    @pl.when(n == 0)
    def _():
        st_ref[...] = jnp.zeros(st_ref.shape, F32)

    xi = xi_ref[...]
    cd = cd_ref[...]
    for bb in range(q_ref.shape[0]):
        q = (q_ref[bb].astype(F32) * (HEAD_DIM ** -0.5)).astype(BF16)
        k = k_ref[bb]
        v = v_ref[bb]
        kz_t = (k.astype(F32) * zeta_ref[...]).T.astype(BF16)
        outs = []
        for h in range(HEADS):
            sl = slice(h * HEAD_DIM, (h + 1) * HEAD_DIM)
            qh, kh, vh = q[:, sl], k[:, sl], v[:, sl]
            prev = st_ref[bb * HEADS + h]
            inner = (_dot_nt(qh, kh) * decay_ref[h]).astype(BF16)
            o = _dot(inner, vh) + _dot(qh, prev.astype(BF16)) * xi[:, sl]
            st_ref[bb * HEADS + h] = prev * cd[:, sl] + _dot(kz_t[sl, :], vh)
            mu = jnp.mean(o, axis=-1, keepdims=True)
            d = o - mu
            var = jnp.mean(d * d, axis=-1, keepdims=True)
            outs.append(d * lax.rsqrt(var + EPS))
        y = jnp.concatenate(outs, axis=-1) * gn_ref[...] * _silu(z_ref[bb])
        o_ref[bb] = y.astype(o_ref.dtype)


def _retention(qkv, z_all, gn_w, batch, seq, bpb):
    c = RET_CHUNK
    nch = seq // c
    decay, xi_tab, zeta_tab, cd_tab = _ret_tables()

    def full(shape):
        return pl.BlockSpec(shape, lambda b, n: (0,) * len(shape))

    def blk(col):
        return pl.BlockSpec((bpb, c, GROUP_W), lambda b, n: (b, n, col))

    out = pl.pallas_call(
        _ret_kernel,
        grid=(batch // bpb, nch),
        in_specs=[full((HEADS, c, c)), full((c, GROUP_W)), full((c, GROUP_W)), full((1, GROUP_W)),
                  full((1, GROUP_W)), blk(0), blk(1), blk(2), blk(2)],
        out_specs=blk(0),
        out_shape=jax.ShapeDtypeStruct((batch, seq, GROUP_W), BF16),
        scratch_shapes=[pltpu.VMEM((bpb * HEADS, HEAD_DIM, HEAD_DIM), F32)],
        compiler_params=_params(("parallel", "arbitrary")),
        name="retention",
    )(decay, xi_tab, zeta_tab, cd_tab, gn_w.reshape(1, GROUP_W),
      *(qkv.reshape(batch, seq, -1),) * 3, z_all.reshape(batch, seq, -1))
    return out.reshape(batch * seq, GROUP_W)


def _ssd_kernel(cw_ref, cb_ref, dtb_ref, a_ref, dsk_ref, nw_ref, xbc_ref, misc_ref, z_ref, o_ref,
                ext_ref, st_ref):
    n = pl.program_id(1)
    L = SSM_CHUNK
    hi = lax.Precision.HIGHEST

    @pl.when(n == 0)
    def _():
        st_ref[...] = jnp.zeros(st_ref.shape, F32)
        ext_ref[:, 0:8, :] = jnp.zeros((ext_ref.shape[0], 8, CONV_CH), F32)

    row = lax.broadcasted_iota(jnp.int32, (L, L), 0)
    col = lax.broadcasted_iota(jnp.int32, (L, L), 1)
    causal = row >= col
    tril = jnp.where(causal, 1.0, 0.0).astype(F32)
    dsk = dsk_ref[...]

    for bb in range(xbc_ref.shape[0]):
        raw = xbc_ref[bb]
        ext_ref[bb, 8:8 + L, :] = raw
        conv = cb_ref[...] + raw * cw_ref[CONV_W - 1:CONV_W, :]
        for w in range(CONV_W - 1):
            shift = CONV_W - 1 - w
            conv = conv + ext_ref[bb, 8 - shift:8 - shift + L, :] * cw_ref[w:w + 1, :]
        ext_ref[bb, 0:8, :] = raw[L - 8:L, :]
        xc = _silu(conv)
        x = xc[:, 0:GROUP_W]
        bm = xc[:, GROUP_W:GROUP_W + 2 * SSM_STATE].astype(BF16)
        cm = xc[:, GROUP_W + 2 * SSM_STATE:].astype(BF16)

        dt_full = jax.nn.softplus(misc_ref[bb] + dtb_ref[...])
        da = dt_full * a_ref[...]
        cs_col = jnp.dot(tril, da, precision=hi, preferred_element_type=F32)
        cs_row = lax.dot_general(da, tril, (((0,), (1,)), ((), ())), precision=hi,
                                 preferred_element_type=F32)

        outs = []
        for h in range(HEADS):
            g = h // 2
            c0 = DT_COL + h
            sl = slice(h * HEAD_DIM, (h + 1) * HEAD_DIM)
            gs = slice(g * SSM_STATE, (g + 1) * SSM_STATE)
            cs_c = cs_col[:, c0:c0 + 1]
            cs_r = cs_row[c0:c0 + 1, :]
            cs_last = cs_col[L - 1:L, c0:c0 + 1]
            xh = x[:, sl]
            xdt = xh * dt_full[:, c0:c0 + 1]
            seg = jnp.exp(jnp.where(causal, cs_c - cs_r, NEG_INF))
            cb = _dot_nt(cm[:, gs], bm[:, gs])
            y = _dot((cb * seg).astype(BF16), xdt.astype(BF16))
            prev = st_ref[bb * HEADS + h]
            y = y + _dot(cm[:, gs], prev.astype(BF16)) * jnp.exp(cs_c)
            y = y + dsk[:, sl] * xh
            dec = jnp.exp(cs_last - cs_c)
            st_ref[bb * HEADS + h] = prev * jnp.exp(cs_last) + _dot_tn(bm[:, gs], (xdt * dec).astype(BF16))
            outs.append(y)
        y = jnp.concatenate(outs, axis=-1) * _silu(z_ref[bb])
        ms = jnp.mean(y * y, axis=-1, keepdims=True)
        o_ref[bb] = (y * lax.rsqrt(ms + EPS) * nw_ref[...]).astype(o_ref.dtype)


def _ssd(xbc, misc, z_all, conv_w, conv_b, dt_bias, a_log, d_skip, norm_w, batch, seq, bpb):
    L = SSM_CHUNK
    nch = seq // L
    dtb = jnp.zeros((1, 128), F32).at[0, DT_COL:DT_COL + HEADS].set(dt_bias)
    a_full = jnp.zeros((1, 128), F32).at[0, DT_COL:DT_COL + HEADS].set(-jnp.exp(a_log))
    dsk = jnp.repeat(d_skip, HEAD_DIM)[None, :]

    def full(shape):
        return pl.BlockSpec(shape, lambda b, n: (0,) * len(shape))

    out = pl.pallas_call(
        _ssd_kernel,
        grid=(batch // bpb, nch),
        in_specs=[full((CONV_W, CONV_CH)), full((1, CONV_CH)), full((1, 128)), full((1, 128)),
                  full((1, GROUP_W)), full((1, GROUP_W)),
                  pl.BlockSpec((bpb, L, CONV_CH), lambda b, n: (b, n, 0)),
                  pl.BlockSpec((bpb, L, 128), lambda b, n: (b, n, 0)),
                  pl.BlockSpec((bpb, L, GROUP_W), lambda b, n: (b, n, 3))],
        out_specs=pl.BlockSpec((bpb, L, GROUP_W), lambda b, n: (b, n, 0)),
        out_shape=jax.ShapeDtypeStruct((batch, seq, GROUP_W), BF16),
        scratch_shapes=[pltpu.VMEM((bpb, 8 + L, CONV_CH), F32),
                        pltpu.VMEM((bpb * HEADS, SSM_STATE, HEAD_DIM), F32)],
        compiler_params=_params(("parallel", "arbitrary")),
        name="ssd",
    )(conv_w, conv_b.reshape(1, CONV_CH), dtb, a_full, dsk, norm_w.reshape(1, GROUP_W),
      xbc.reshape(batch, seq, -1), misc.reshape(batch, seq, -1), z_all.reshape(batch, seq, -1))
    return out.reshape(batch * seq, GROUP_W)


def _pick_tile(n, pref):
    t = pref
    while n % t:
        t //= 2
    return t


def kernel(x, norm_w, w_in, w_out, nsa_pe_k, nsa_pe_v, nsa_w_ck1, nsa_w_ck2, nsa_w_cv1, nsa_w_cv2,
           diff_lam_q1, diff_lam_k1, diff_lam_q2, diff_lam_k2, diff_subln_w, ret_gn_w,
           ssm_conv_w, ssm_conv_b, ssm_dt_bias, ssm_A_log, ssm_D, ssm_norm_w, final_norm_w):
    batch, seq, _ = x.shape
    depth = w_in.shape[0]
    m = batch * seq
    tm = _pick_tile(m, 512)
    tq = _pick_tile(seq, 512)
    tk = _pick_tile(seq, 256)
    tk_diff = _pick_tile(seq, 512)
    bpb = _pick_tile(batch, REC_BATCH)
    w_r = _relayout_w_in(w_in)
    w_t = _relayout_w_in_t(w_in)
    w_out_b = w_out.astype(BF16)
    x2d = x.reshape(m, D_MODEL)
    projected = _in_proj(x2d, norm_w[0], w_r[0], w_t[0], tm)
    for i in range(depth):
        nsa_q, nsa_k2, nsa_cmp, misc, z_all, diff_qk, ret_qkv, xbc, vt_all = projected
        kc, vct = _nsa_compress(nsa_cmp, nsa_pe_k[i], nsa_pe_v[i], nsa_w_ck1[i], nsa_w_ck2[i],
                                nsa_w_cv1[i], nsa_w_cv2[i], batch, seq)
        y_nsa = _nsa_attention(nsa_q, nsa_k2, vt_all, kc, vct, misc, z_all, batch, seq, tq, tk)
        lam_p = jnp.stack([diff_lam_q1[i], diff_lam_k1[i], diff_lam_q2[i], diff_lam_k2[i]])
        y_diff = _diff_attention(diff_qk, vt_all, z_all, lam_p, diff_subln_w[i], batch, seq, i, tq, tk_diff)
        y_ret = _retention(ret_qkv, z_all, ret_gn_w[i], batch, seq, bpb)
        y_ssm = _ssd(xbc, misc, z_all, ssm_conv_w[i], ssm_conv_b[i], ssm_dt_bias[i], ssm_A_log[i],
                     ssm_D[i], ssm_norm_w[i], batch, seq, bpb)
        ys = (y_nsa, y_diff, y_ret, y_ssm)
        if i + 1 < depth:
            x2d, *projected = _out_in_proj(ys, w_out_b[i], x2d, norm_w[i + 1], w_r[i + 1], w_t[i + 1], tm)
        else:
            x2d = _out_proj(ys, w_out_b[i], x2d, final_norm_w, tm)
    return x2d.reshape(batch, seq, D_MODEL)
```

```python
import functools
import math

import numpy as np
import jax
import jax.numpy as jnp
from jax import lax
from jax.experimental import pallas as pl
from jax.experimental.pallas import tpu as pltpu

F32 = jnp.float32
BF16 = jnp.bfloat16
NEG_INF = float("-inf")
LOG2E = 1.4426950408889634

D_MODEL = 1024
DEPTH = 4
GROUP_W = 256
HEADS = 4
HEAD_DIM = 64
EPS = 1e-6
CMP_LEN = 32
CMP_STRIDE = 16
CMP_HIDDEN = 256
SLC_BLOCK = 64
SLC_SHIFT = 6
SLC_TOPK = 16
WINDOW = 512
DIFF_QK_DIM = 32
RET_CHUNK = 128
SSM_STATE = 128
SSM_CHUNK = 128
CONV_W = 4
CONV_CH = 768
N_ALIBI_HEADS = 8
LANES = 128
BF16_ROWS = 16
ACC_ROWS = HEAD_DIM + BF16_ROWS
QK_LOOKAHEAD = 6
ALIBI_ROWS = 3
POS_RADIX = 256
MASK_BIG = 2.0 ** 100
BLK_LANE0 = 64
REC_BATCH = 4

IN_LAYOUT = (
    ("nsa_q", 256), ("nsa_k_cmp", 64), ("nsa_v_cmp", 64), ("nsa_k_slc", 64), ("nsa_v_slc", 64),
    ("nsa_k_win", 64), ("nsa_v_win", 64), ("nsa_gate", 12), ("nsa_z", 256),
    ("diff_q", 256), ("diff_k", 256), ("diff_v", 256), ("diff_z", 256),
    ("ret_q", 256), ("ret_k", 256), ("ret_v", 256), ("ret_z", 256),
    ("ssm_z", 256), ("ssm_xbc", 768), ("ssm_dt", 4),
)
IN_OFF = {}
_o = 0
for _n, _w in IN_LAYOUT:
    IN_OFF[_n] = (_o, _w)
    _o += _w
IN_W = _o

GATE_COL = 0
DT_COL = 12
IN_OUTPUTS = (
    ("nsa_q", BF16, ("nsa_q",), 256),
    ("nsa_k2", BF16, ("nsa_k_slc", "nsa_k_win"), 128),
    ("nsa_cmp", F32, ("nsa_k_cmp", "nsa_v_cmp"), 128),
    ("misc", F32, ("nsa_gate", "ssm_dt"), 128),
    ("z_all", F32, ("nsa_z", "diff_z", "ret_z", "ssm_z"), 1024),
    ("diff_qk", BF16, ("diff_q", "diff_k"), 512),
    ("ret_qkv", BF16, ("ret_q", "ret_k", "ret_v"), 768),
    ("xbc", F32, ("ssm_xbc",), 768),
)
IN_T_SRC = ("diff_v", "nsa_v_slc", "nsa_v_win")
IN_T_ROWS = 384
DIFF_VT_BLK = 0
NSA_VT_BLK = 2
IN_SEGS = []
_o = 0
for _n, _dt, _src, _w in IN_OUTPUTS:
    IN_SEGS.append((_o, _o + _w))
    _o += _w
IN_WP = _o

VMEM_LIMIT = 56 * 1024 * 1024


def _alibi_slopes():
    return [2.0 ** (-8.0 * (i + 1) / N_ALIBI_HEADS) for i in range(N_ALIBI_HEADS)]


NSA_SLOPES = _alibi_slopes()[0::2]
DIFF_SLOPES = _alibi_slopes()[1::2]


def _silu(x):
    return x * jax.nn.sigmoid(x)


def _dot(a, b):
    return jnp.dot(a, b, preferred_element_type=F32)


def _dot_nt(a, b):
    return lax.dot_general(a, b, (((1,), (1,)), ((), ())), preferred_element_type=F32)


def _dot_tn(a, b):
    return lax.dot_general(a, b, (((0,), (0,)), ((), ())), preferred_element_type=F32)


def _params(sem):
    return pltpu.CompilerParams(dimension_semantics=sem, vmem_limit_bytes=VMEM_LIMIT)


def _relayout_w_in(w_in):
    cols = []
    for _n, _dt, src, width in IN_OUTPUTS:
        used = 0
        for s in src:
            off, w = IN_OFF[s]
            cols.append(w_in[:, :, off:off + w])
            used += w
        if used < width:
            cols.append(jnp.zeros(w_in.shape[:2] + (width - used,), w_in.dtype))
    return jnp.concatenate(cols, axis=-1)


def _relayout_w_in_t(w_in):
    cols = [w_in[:, :, IN_OFF[s][0]:IN_OFF[s][0] + IN_OFF[s][1]] for s in IN_T_SRC]
    return jnp.swapaxes(jnp.concatenate(cols, axis=-1), 1, 2)


def _norm_project(x, nw_ref, w_ref, wt_ref, out_refs):
    ms = jnp.mean(x * x, axis=-1, keepdims=True)
    h = (x * lax.rsqrt(ms + EPS) * nw_ref[...]).astype(BF16)
    for ref, (a, b) in zip(out_refs[:-1], IN_SEGS):
        ref[...] = _dot(h, w_ref[:, a:b]).astype(ref.dtype)
    out_refs[-1][...] = _dot_nt(wt_ref[...], h).astype(BF16)


def _in_proj_kernel(x_ref, nw_ref, w_ref, wt_ref, *out_refs):
    _norm_project(x_ref[...], nw_ref, w_ref, wt_ref, out_refs)


def _in_proj_outputs(m, tm):
    out_shape = [jax.ShapeDtypeStruct((m, b - a), dt) for (_n, dt, _s, _w), (a, b) in zip(IN_OUTPUTS, IN_SEGS)]
    out_specs = [pl.BlockSpec((tm, b - a), lambda i: (i, 0)) for (a, b) in IN_SEGS]
    out_shape.append(jax.ShapeDtypeStruct((IN_T_ROWS, m), BF16))
    out_specs.append(pl.BlockSpec((IN_T_ROWS, tm), lambda i: (0, i)))
    return out_shape, out_specs


def _in_proj(x2d, norm_w, w_r, w_t, tm):
    m = x2d.shape[0]
    out_shape, out_specs = _in_proj_outputs(m, tm)
    return pl.pallas_call(
        _in_proj_kernel,
        grid=(m // tm,),
        in_specs=[pl.BlockSpec((tm, D_MODEL), lambda i: (i, 0)),
                  pl.BlockSpec((1, D_MODEL), lambda i: (0, 0)),
                  pl.BlockSpec((D_MODEL, IN_WP), lambda i: (0, 0)),
                  pl.BlockSpec((IN_T_ROWS, D_MODEL), lambda i: (0, 0))],
        out_specs=out_specs,
        out_shape=out_shape,
        compiler_params=_params(("parallel",)),
        name="in_proj",
    )(x2d, norm_w.reshape(1, D_MODEL), w_r, w_t)


def _out_proj_kernel(y0_ref, y1_ref, y2_ref, y3_ref, w_ref, x_ref, fw_ref, o_ref):
    acc = x_ref[...]
    for g, y_ref in enumerate((y0_ref, y1_ref, y2_ref, y3_ref)):
        acc = acc + _dot(y_ref[...], w_ref[g * GROUP_W:(g + 1) * GROUP_W, :])
    ms = jnp.mean(acc * acc, axis=-1, keepdims=True)
    o_ref[...] = acc * lax.rsqrt(ms + EPS) * fw_ref[...]


def _out_proj(ys, w_out_b, x2d, final_w, tm):
    m = x2d.shape[0]
    yspec = pl.BlockSpec((tm, GROUP_W), lambda i: (i, 0))
    return pl.pallas_call(
        _out_proj_kernel,
        grid=(m // tm,),
        in_specs=[yspec, yspec, yspec, yspec,
                  pl.BlockSpec((D_MODEL, D_MODEL), lambda i: (0, 0)),
                  pl.BlockSpec((tm, D_MODEL), lambda i: (i, 0)),
                  pl.BlockSpec((1, D_MODEL), lambda i: (0, 0))],
        out_specs=pl.BlockSpec((tm, D_MODEL), lambda i: (i, 0)),
        out_shape=jax.ShapeDtypeStruct((m, D_MODEL), F32),
        compiler_params=_params(("parallel",)),
        name="out_proj",
    )(*ys, w_out_b, x2d, final_w.reshape(1, D_MODEL))


def _out_in_proj_kernel(y0_ref, y1_ref, y2_ref, y3_ref, wo_ref, x_ref, nw_ref, w_ref, wt_ref, xo_ref, *out_refs):
    acc = x_ref[...]
    for g, y_ref in enumerate((y0_ref, y1_ref, y2_ref, y3_ref)):
        acc = acc + _dot(y_ref[...], wo_ref[g * GROUP_W:(g + 1) * GROUP_W, :])
    xo_ref[...] = acc
    _norm_project(acc, nw_ref, w_ref, wt_ref, out_refs)


def _out_in_proj(ys, w_out_b, x2d, norm_w, w_r, w_t, tm):
    m = x2d.shape[0]
    yspec = pl.BlockSpec((tm, GROUP_W), lambda i: (i, 0))
    xspec = pl.BlockSpec((tm, D_MODEL), lambda i: (i, 0))
    out_shape, out_specs = _in_proj_outputs(m, tm)
    return pl.pallas_call(
        _out_in_proj_kernel,
        grid=(m // tm,),
        in_specs=[yspec, yspec, yspec, yspec,
                  pl.BlockSpec((D_MODEL, D_MODEL), lambda i: (0, 0)),
                  xspec,
                  pl.BlockSpec((1, D_MODEL), lambda i: (0, 0)),
                  pl.BlockSpec((D_MODEL, IN_WP), lambda i: (0, 0)),
                  pl.BlockSpec((IN_T_ROWS, D_MODEL), lambda i: (0, 0))],
        out_specs=[xspec] + out_specs,
        out_shape=[jax.ShapeDtypeStruct((m, D_MODEL), F32)] + out_shape,
        compiler_params=_params(("parallel",)),
        name="out_in_proj",
    )(*ys, w_out_b, x2d, norm_w.reshape(1, D_MODEL), w_r, w_t)


def _stage_scores(s_ref, slot, cols, s, keep=None):
    if keep is not None:
        s = jnp.where(keep, s, NEG_INF)
    s_ref[slot, :, cols] = s
    return jnp.max(s, axis=0, keepdims=True)


def _flash_update_t(s, s_max, c1, shift, vt_ext, m_ref, acc_ref, idx, cols):
    m_old = m_ref[idx, :, cols]
    m_new = jnp.maximum(m_old, c1 * s_max + shift)
    alpha = jnp.exp2(m_old - m_new)
    p = jnp.exp2(c1 * s() - (m_new - shift))
    acc_ref[idx, :, cols] = alpha * acc_ref[idx, :, cols] + _dot(vt_ext, p.astype(BF16))
    m_ref[idx, :, cols] = m_new


def _bf16_pieces(x, n=3):
    out = []
    for _ in range(n):
        p = float(np.asarray(x, np.float32).astype(BF16).astype(np.float32))
        out.append(p)
        x = x - p
    return out


def _key_pos_features(tk):
    lane = lax.broadcasted_iota(jnp.int32, (tk, LANES), 1)
    row = lax.broadcasted_iota(jnp.int32, (tk, LANES), 0)
    out = jnp.zeros((tk, LANES), jnp.int32)
    for d in range(-(-tk // POS_RADIX)):
        digit = jnp.bitwise_and(jnp.right_shift(row, 8 * d), POS_RADIX - 1)
        out = jnp.where((lane >= ALIBI_ROWS * d) & (lane < ALIBI_ROWS * (d + 1)), digit, out)
    return out.astype(F32).astype(BF16)


def _alibi_rows(beta, tq, rows=LANES, tk=POS_RADIX):
    row = lax.broadcasted_iota(jnp.int32, (rows, tq), 0)
    out = jnp.zeros((rows, tq), F32)
    for d in range(-(-tk // POS_RADIX)):
        for r, piece in enumerate(_bf16_pieces(beta * POS_RADIX ** d, ALIBI_ROWS)):
            out = jnp.where(row == ALIBI_ROWS * d + r, piece, out)
    return out.astype(BF16)


def _normalized(acc):
    return acc[0:HEAD_DIM] / jnp.maximum(acc[HEAD_DIM:HEAD_DIM + 1], 1e-30)


def _rel_t(tk, tq):
    return lax.broadcasted_iota(jnp.int32, (tk, tq), 1) - lax.broadcasted_iota(jnp.int32, (tk, tq), 0)


def _diff_kernel(lam_ref, sw_ref, q_ref, k_ref, vt_ref, z_ref, o_ref, m_ref, acc_ref, wq_ref, s_ref,
                 *, tq, tk, lam_init):
    qi = pl.program_id(1)
    kpq = tq // tk
    scale = DIFF_QK_DIM ** -0.5
    c1 = scale * LOG2E
    m_ref[...] = jnp.full(m_ref.shape, NEG_INF, F32)
    acc_ref[...] = jnp.zeros(acc_ref.shape, F32)
    rel = _rel_t(tk, tq)

    qt = q_ref[...].astype(F32).T.astype(BF16)
    row = lax.broadcasted_iota(jnp.int32, (LANES, tq), 0)
    for j in range(2 * HEADS):
        g, r0 = divmod(j * DIFF_QK_DIM, LANES)
        qg = qt[g * LANES:(g + 1) * LANES]
        wq_ref[j, 0:LANES, :] = jnp.where((row >= r0) & (row < r0 + DIFF_QK_DIM), qg, jnp.zeros_like(qg))
        wq_ref[j, LANES:2 * LANES, :] = _alibi_rows(DIFF_SLOPES[j // 2] / scale, tq, tk=tk)

    ones = jnp.ones((BF16_ROWS, tk), BF16)
    kpos = _key_pos_features(tk)

    n_maps = 2 * HEADS
    n_slots = s_ref.shape[0]

    def tiles(kis, mask_offs):
        loaded = []
        for ki in kis:
            start = pl.multiple_of(ki * tk, tk)
            loaded.append((k_ref[pl.ds(start, tk), :],
                           vt_ref[:, pl.ds(start, tk)],
                           (ki * tk).astype(F32)))
        items = [(t, j) for t in range(len(kis)) for j in range(n_maps)]
        cols = slice(0, tq)

        def scores(i):
            t, j = items[i]
            g = (j * DIFF_QK_DIM) // LANES
            lhs = jnp.concatenate([loaded[t][0][:, g * LANES:(g + 1) * LANES], kpos], axis=1)
            s_ref[i % n_slots] = _dot(lhs, wq_ref[j])

        def staged(i):
            t = items[i][0]
            if mask_offs[t] is None:
                return lambda: s_ref[i % n_slots]
            return lambda: jnp.where(rel >= mask_offs[t], s_ref[i % n_slots], NEG_INF)

        for i in range(QK_LOOKAHEAD):
            scores(i)
        for i, (t, j) in enumerate(items):
            h = j // 2
            if i + QK_LOOKAHEAD < len(items):
                scores(i + QK_LOOKAHEAD)
            _k, vt, key0 = loaded[t]
            vt_ext = jnp.concatenate([vt[h * HEAD_DIM:(h + 1) * HEAD_DIM], ones], axis=0)
            s = staged(i)
            _flash_update_t(s, jnp.max(s(), axis=0, keepdims=True), c1, (DIFF_SLOPES[h] * LOG2E) * key0,
                            vt_ext, m_ref, acc_ref, j, cols)

    def body(i, carry):
        tiles([i * kpq + d for d in range(kpq)], [None] * kpq)
        return carry

    lax.fori_loop(0, qi, body, 0)
    tiles([qi * kpq + d for d in range(kpq)], [d * tk for d in range(kpq)])

    lp = lam_ref[...]
    lam = (jnp.exp(jnp.sum(lp[0:1] * lp[1:2], axis=-1, keepdims=True))
           - jnp.exp(jnp.sum(lp[2:3] * lp[3:4], axis=-1, keepdims=True)) + lam_init)
    sw = sw_ref[...]
    o_t = jnp.concatenate([_normalized(acc_ref[2 * h]) - lam * _normalized(acc_ref[2 * h + 1])
                           for h in range(HEADS)], axis=0)
    o = o_t.T
    outs = []
    for h in range(HEADS):
        oh = o[:, h * HEAD_DIM:(h + 1) * HEAD_DIM]
        ms = jnp.mean(oh * oh, axis=-1, keepdims=True)
        outs.append(oh * lax.rsqrt(ms + EPS) * sw * (1.0 - lam_init))
    y = jnp.concatenate(outs, axis=-1) * _silu(z_ref[...])
    o_ref[...] = y.astype(o_ref.dtype)


def _diff_attention(qk, vt_all, z_all, lam_p, subln_w, batch, seq, layer_idx, tq, tk):
    nq = seq // tq
    lam_init = 0.8 - 0.6 * math.exp(-0.3 * layer_idx)
    return pl.pallas_call(
        functools.partial(_diff_kernel, tq=tq, tk=tk, lam_init=lam_init),
        grid=(batch, nq),
        in_specs=[pl.BlockSpec((4, DIFF_QK_DIM), lambda b, i: (0, 0)),
                  pl.BlockSpec((1, HEAD_DIM), lambda b, i: (0, 0)),
                  pl.BlockSpec((tq, GROUP_W), lambda b, i: (b * nq + i, 0)),
                  pl.BlockSpec((seq, GROUP_W), lambda b, i: (b, 1)),
                  pl.BlockSpec((GROUP_W, seq), lambda b, i: (DIFF_VT_BLK, b)),
                  pl.BlockSpec((tq, GROUP_W), lambda b, i: (b * nq + i, 1))],
        out_specs=pl.BlockSpec((tq, GROUP_W), lambda b, i: (b * nq + i, 0)),
        out_shape=jax.ShapeDtypeStruct((batch * seq, GROUP_W), BF16),
        scratch_shapes=[pltpu.VMEM((2 * HEADS, 1, tq), F32),
                        pltpu.VMEM((2 * HEADS, ACC_ROWS, tq), F32),
                        pltpu.VMEM((2 * HEADS, 2 * LANES, tq), BF16),
                        pltpu.VMEM((QK_LOOKAHEAD + 1, tk, tq), F32)],
        compiler_params=_params(("parallel", "parallel")),
        name="diff_attention",
    )(lam_p, subln_w.reshape(1, HEAD_DIM), qk, qk, vt_all, z_all)


def _nsa_compress_kernel(cmp_ref, pe_ref, w1_ref, wk2_ref, wv2t_ref, kc_ref, vct_ref):
    ng = kc_ref.shape[1]
    top = jnp.zeros((ng, 2 * CMP_HIDDEN), F32)
    bot = jnp.zeros((ng, 2 * CMP_HIDDEN), F32)
    for p in range(CMP_STRIDE):
        x = cmp_ref[pl.ds(p, ng, stride=CMP_STRIDE), :]
        top = top + _dot((x + pe_ref[p:p + 1, :]).astype(BF16), w1_ref[p])
        bot = bot + _dot((x + pe_ref[CMP_STRIDE + p:CMP_STRIDE + p + 1, :]).astype(BF16), w1_ref[CMP_STRIDE + p])
    hid = _silu(top + pltpu.roll(bot, ng - 1, 0)).astype(BF16)
    hk, hv = hid[:, 0:CMP_HIDDEN], hid[:, CMP_HIDDEN:]
    kc = _dot(hk, wk2_ref[...])
    lane = lax.broadcasted_iota(jnp.int32, kc.shape, 1)
    blk = lax.broadcasted_iota(jnp.int32, kc.shape, 0).astype(F32)
    kc = jnp.where((lane >= HEAD_DIM) & (lane < HEAD_DIM + ALIBI_ROWS), blk, kc)
    kc_ref[0] = kc.astype(kc_ref.dtype)
    vct_ref[0] = _dot_nt(wv2t_ref[...], hv).astype(vct_ref.dtype)


def _nsa_compress(cmp2d, pe_k, pe_v, w_ck1, w_ck2, w_cv1, w_cv2, batch, seq):
    ng = seq // CMP_STRIDE
    wk = w_ck1.reshape(CMP_LEN, HEAD_DIM, CMP_HIDDEN)
    wv = w_cv1.reshape(CMP_LEN, HEAD_DIM, CMP_HIDDEN)
    w1 = jnp.concatenate([jnp.pad(wk, ((0, 0), (0, 0), (0, CMP_HIDDEN))),
                          jnp.pad(wv, ((0, 0), (0, 0), (CMP_HIDDEN, 0)))], axis=1).astype(BF16)
    pe = jnp.concatenate([pe_k, pe_v], axis=1)

    def full(shape):
        return pl.BlockSpec(shape, lambda b: (0,) * len(shape))

    return pl.pallas_call(
        _nsa_compress_kernel,
        grid=(batch,),
        in_specs=[pl.BlockSpec((seq, LANES), lambda b: (b, 0)), full((CMP_LEN, LANES)),
                  full((CMP_LEN, LANES, 2 * CMP_HIDDEN)), full((CMP_HIDDEN, LANES)),
                  full((HEAD_DIM, CMP_HIDDEN))],
        out_specs=[pl.BlockSpec((1, ng, LANES), lambda b: (b, 0, 0)),
                   pl.BlockSpec((1, HEAD_DIM, ng), lambda b: (b, 0, 0))],
        out_shape=[jax.ShapeDtypeStruct((batch, ng, LANES), BF16),
                   jax.ShapeDtypeStruct((batch, HEAD_DIM, ng), BF16)],
        compiler_params=_params(("parallel",)),
        name="nsa_compress",
    )(cmp2d, pe, w1, jnp.pad(w_ck2, ((0, 0), (0, LANES - HEAD_DIM))).astype(BF16), w_cv2.T.astype(BF16))


def _nsa_kernel(ovl_ref, q_ref, k_ref, vt_ref, kc_ref, vct_ref, misc_ref, z_ref, o_ref,
                m_ref, acc_ref, wq_ref, s_ref, sc_ref, *, tq, tk, seq):
    qi = pl.program_id(1)
    kpq = tq // tk
    scale = HEAD_DIM ** -0.5
    c1 = scale * LOG2E
    ng = seq // CMP_STRIDE
    ns = seq // SLC_BLOCK
    top = min(SLC_TOPK, ns)
    m_ref[...] = jnp.full(m_ref.shape, NEG_INF, F32)
    acc_ref[...] = jnp.zeros(acc_ref.shape, F32)
    rel = _rel_t(tk, tq)

    qt = q_ref[...].astype(F32).T.astype(BF16)
    zeros_q = jnp.zeros((HEAD_DIM, tq), BF16)
    for h in range(HEADS):
        qh = qt[h * HEAD_DIM:(h + 1) * HEAD_DIM]
        alibi = _alibi_rows(NSA_SLOPES[h] / scale, tq, tk=tk)
        wq_ref[h, 0:LANES, :] = jnp.concatenate([qh, zeros_q], axis=0)
        wq_ref[h, LANES:2 * LANES, :] = alibi
        wq_ref[HEADS + h, 0:LANES, :] = jnp.concatenate([zeros_q, qh], axis=0)
        wq_ref[HEADS + h, LANES:2 * LANES, :] = alibi

    t_lane = qi * tq + lax.broadcasted_iota(jnp.int32, (1, tq), 1)
    n_sub = lax.broadcasted_iota(jnp.int32, (ng, 1), 0)
    c_valid = (n_sub * CMP_STRIDE + (CMP_LEN - 1)) <= t_lane
    kc = kc_ref[0]
    vct_ext = jnp.concatenate([vct_ref[0], jnp.ones((BF16_ROWS, ng), BF16)], axis=0)
    ovl = ovl_ref[...]
    cmp_max = []
    for h in range(HEADS):
        rhs = jnp.concatenate([qt[h * HEAD_DIM:(h + 1) * HEAD_DIM],
                               _alibi_rows(NSA_SLOPES[h] * CMP_STRIDE / scale, tq, HEAD_DIM)], axis=0)
        cmp_max.append(_stage_scores(sc_ref, h, slice(0, tq), _dot(kc, rhs), c_valid))
    imp = jnp.zeros((ns, tq), F32)
    o_cmp = []
    for h in range(HEADS):
        mx = jnp.where(cmp_max[h] > NEG_INF, cmp_max[h], 0.0)
        pb = jnp.exp2(c1 * sc_ref[h] - c1 * mx).astype(BF16)
        o_ext = _dot(vct_ext, pb)
        r = 1.0 / jnp.maximum(o_ext[HEAD_DIM:HEAD_DIM + 1], 1e-30)
        o_cmp.append(o_ext[0:HEAD_DIM] * r)
        imp = imp + _dot(ovl, pb) * r

    def select_blocks():
        j_sub = lax.broadcasted_iota(jnp.int32, (ns, 1), 0)
        j_sub_f = j_sub.astype(F32)
        cur = jnp.right_shift(t_lane, SLC_SHIFT)
        forced = (j_sub == 0) | (j_sub == cur) | (j_sub == cur - 1)
        valid = (j_sub * SLC_BLOCK) <= t_lane
        score = jnp.where(forced, 1e30, jnp.where(valid, imp, -1.0))
        sel = jnp.zeros((ns, tq), F32)
        for _ in range(top):
            mx = jnp.max(score, axis=0, keepdims=True)
            idx = jnp.min(jnp.where(score == mx, j_sub_f, float(ns)), axis=0, keepdims=True)
            pick = j_sub_f == idx
            sel = jnp.where(pick, 1.0, sel)
            score = jnp.where(pick, -2.0, score)
        unsel = jnp.where(sel > 0.5, 0.0, -MASK_BIG).astype(BF16)
        for h in range(HEADS):
            wq_ref[h, LANES + BLK_LANE0:LANES + BLK_LANE0 + ns, :] = unsel

    aux_lane = lax.broadcasted_iota(jnp.int32, (tk, LANES), 1)
    aux_blk = jnp.right_shift(lax.broadcasted_iota(jnp.int32, (tk, LANES), 0), SLC_SHIFT) + BLK_LANE0
    kpos = _key_pos_features(tk)
    ones = jnp.ones((BF16_ROWS, tk), BF16)

    n_slots = s_ref.shape[0]

    def tiles(window, kis, cols, mask_offs=None):
        base = HEADS if window else 0
        loaded = []
        for t, ki in enumerate(kis):
            start = pl.multiple_of(ki * tk, tk)
            k = k_ref[pl.ds(start, tk), :]
            if window:
                vt = vt_ref[HEAD_DIM:2 * HEAD_DIM, pl.ds(start, tk)]
                aux = kpos
                dist = rel[:, cols[t]] + (qi * tq - ki * tk)
                keep = (dist >= 0) & (dist < WINDOW)
            else:
                vt = vt_ref[0:HEAD_DIM, pl.ds(start, tk)]
                onehot = aux_lane == aux_blk + ki * (tk // SLC_BLOCK)
                aux = jnp.where(onehot, jnp.ones_like(kpos), kpos)
                keep = None
            loaded.append((jnp.concatenate([k, aux], axis=1),
                           jnp.concatenate([vt, ones], axis=0), (ki * tk).astype(F32), keep))
        items = [(t, h) for t in range(len(kis)) for h in range(HEADS)]
        s_max = {}

        def scores(i):
            t, h = items[i]
            if window:
                keep = loaded[t][3]
            else:
                keep = None if mask_offs[t] is None else rel[:, cols[t]] >= mask_offs[t]
            s_max[i] = _stage_scores(s_ref, i % n_slots, cols[t],
                                     _dot(loaded[t][0], wq_ref[base + h, :, cols[t]]), keep)

        for i in range(QK_LOOKAHEAD):
            scores(i)
        for i, (t, h) in enumerate(items):
            if i + QK_LOOKAHEAD < len(items):
                scores(i + QK_LOOKAHEAD)
            _lhs, vt_ext, key0, _keep = loaded[t]
            _flash_update_t(lambda i=i, t=t: s_ref[i % n_slots, :, cols[t]], s_max.pop(i), c1,
                            (NSA_SLOPES[h] * LOG2E) * key0, vt_ext, m_ref, acc_ref, base + h, cols[t])

    all_cols = slice(0, tq)
    diag_cols = [slice(d * tk, tq) for d in range(kpq)]

    def body(i, carry):
        tiles(False, [i * kpq + d for d in range(kpq)], [all_cols] * kpq, [None] * kpq)
        return carry

    tiles(True, [qi * kpq + d for d in range(kpq)], diag_cols)
    select_blocks()

    lax.fori_loop(0, qi, body, 0)
    tiles(False, [qi * kpq + d for d in range(kpq)], diag_cols, [d * tk for d in range(kpq)])

    n_back = (WINDOW + tk - 1) // tk
    for g in range((n_back + kpq - 1) // kpq):
        backs = list(range(g * kpq + 1, min((g + 1) * kpq, n_back) + 1))
        back_cols = [slice(0, min(tq, -(-(WINDOW - (back - 1) * tk - 1) // LANES) * LANES)) for back in backs]

        @pl.when(qi > g)
        def _():
            tiles(True, [qi * kpq - back for back in backs], back_cols)

    g_t = jax.nn.sigmoid(misc_ref[...]).T
    outs = []
    for h in range(HEADS):
        r0 = GATE_COL + 3 * h
        outs.append(g_t[r0:r0 + 1] * o_cmp[h] + g_t[r0 + 1:r0 + 2] * _normalized(acc_ref[h])
                    + g_t[r0 + 2:r0 + 3] * _normalized(acc_ref[HEADS + h]))
    y = jnp.concatenate(outs, axis=0).T * _silu(z_ref[...])
    o_ref[...] = y.astype(o_ref.dtype)


def _overlap_t(seq):
    nc = (seq - CMP_LEN) // CMP_STRIDE + 1
    ng = seq // CMP_STRIDE
    ns = seq // SLC_BLOCK
    c_start = np.arange(ng) * CMP_STRIDE
    c_end = c_start + CMP_LEN - 1
    s_start = np.arange(ns) * SLC_BLOCK
    s_end = s_start + SLC_BLOCK - 1
    ov = (c_start[None, :] <= s_end[:, None]) & (c_end[None, :] >= s_start[:, None]) & (np.arange(ng)[None, :] < nc)
    return jnp.asarray(ov.astype(np.float32), dtype=BF16)


def _nsa_attention(q, k2, vt_all, kc, vct, misc, z_all, batch, seq, tq, tk):
    nq = seq // tq
    ng = seq // CMP_STRIDE
    ns = seq // SLC_BLOCK
    assert tq - tk < WINDOW and tk % SLC_BLOCK == 0 and ns <= LANES - BLK_LANE0
    return pl.pallas_call(
        functools.partial(_nsa_kernel, tq=tq, tk=tk, seq=seq),
        grid=(batch, nq),
        in_specs=[pl.BlockSpec((ns, ng), lambda b, i: (0, 0)),
                  pl.BlockSpec((tq, GROUP_W), lambda b, i: (b * nq + i, 0)),
                  pl.BlockSpec((seq, LANES), lambda b, i: (b, 0)),
                  pl.BlockSpec((2 * HEAD_DIM, seq), lambda b, i: (NSA_VT_BLK, b)),
                  pl.BlockSpec((1, ng, LANES), lambda b, i: (b, 0, 0)),
                  pl.BlockSpec((1, HEAD_DIM, ng), lambda b, i: (b, 0, 0)),
                  pl.BlockSpec((tq, LANES), lambda b, i: (b * nq + i, 0)),
                  pl.BlockSpec((tq, GROUP_W), lambda b, i: (b * nq + i, 0))],
        out_specs=pl.BlockSpec((tq, GROUP_W), lambda b, i: (b * nq + i, 0)),
        out_shape=jax.ShapeDtypeStruct((batch * seq, GROUP_W), BF16),
        scratch_shapes=[pltpu.VMEM((2 * HEADS, 1, tq), F32),
                        pltpu.VMEM((2 * HEADS, ACC_ROWS, tq), F32),
                        pltpu.VMEM((2 * HEADS, 2 * LANES, tq), BF16),
                        pltpu.VMEM((2 * HEADS, tk, tq), F32),
                        pltpu.VMEM((HEADS, ng, tq), F32)],
        compiler_params=_params(("parallel", "parallel")),
        name="nsa_attention",
    )(_overlap_t(seq), q, k2, vt_all, kc, vct, misc, z_all)


def _ret_tables():
    c = RET_CHUNK
    h = np.arange(HEADS, dtype=np.float32)
    log_g = jnp.log(1.0 - 2.0 ** (-5.0 - jnp.asarray(h)))
    pos = jnp.arange(c, dtype=F32)
    rel = pos[:, None] - pos[None, :]
    decay = jnp.where(rel >= 0, jnp.exp(log_g[:, None, None] * jnp.maximum(rel, 0.0)), 0.0)
    xi = jnp.exp(log_g[:, None] * (pos + 1.0))
    zeta = jnp.exp(log_g[:, None] * (c - 1.0 - pos))
    chunk_decay = jnp.exp(log_g * c)
    xi_tab = jnp.repeat(xi.T, HEAD_DIM, axis=1)
    zeta_tab = jnp.repeat(zeta.T, HEAD_DIM, axis=1)
    cd_tab = jnp.repeat(chunk_decay, HEAD_DIM)[None, :]
    return decay, xi_tab, zeta_tab, cd_tab


def _ret_kernel(decay_ref, xi_ref, zeta_ref, cd_ref, gn_ref, q_ref, k_ref, v_ref, z_ref, o_ref, st_ref):
    n = pl.program_id(1)

    @pl.when(n == 0)
    def _():
        st_ref[...] = jnp.zeros(st_ref.shape, F32)

    xi = xi_ref[...]
    cd = cd_ref[...]
    for bb in range(q_ref.shape[0]):
        q = (q_ref[bb].astype(F32) * (HEAD_DIM ** -0.5)).astype(BF16)
        k = k_ref[bb]
        v = v_ref[bb]
        kz_t = (k.astype(F32) * zeta_ref[...]).T.astype(BF16)
        outs = []
        for h in range(HEADS):
            sl = slice(h * HEAD_DIM, (h + 1) * HEAD_DIM)
            qh, kh, vh = q[:, sl], k[:, sl], v[:, sl]
            prev = st_ref[bb * HEADS + h]
            inner = (_dot_nt(qh, kh) * decay_ref[h]).astype(BF16)
            o = _dot(inner, vh) + _dot(qh, prev.astype(BF16)) * xi[:, sl]
            st_ref[bb * HEADS + h] = prev * cd[:, sl] + _dot(kz_t[sl, :], vh)
            mu = jnp.mean(o, axis=-1, keepdims=True)
            d = o - mu
            var = jnp.mean(d * d, axis=-1, keepdims=True)
            outs.append(d * lax.rsqrt(var + EPS))
        y = jnp.concatenate(outs, axis=-1) * gn_ref[...] * _silu(z_ref[bb])
        o_ref[bb] = y.astype(o_ref.dtype)


def _retention(qkv, z_all, gn_w, batch, seq, bpb):
    c = RET_CHUNK
    nch = seq // c
    decay, xi_tab, zeta_tab, cd_tab = _ret_tables()

    def full(shape):
        return pl.BlockSpec(shape, lambda b, n: (0,) * len(shape))

    def blk(col):
        return pl.BlockSpec((bpb, c, GROUP_W), lambda b, n: (b, n, col))

    out = pl.pallas_call(
        _ret_kernel,
        grid=(batch // bpb, nch),
        in_specs=[full((HEADS, c, c)), full((c, GROUP_W)), full((c, GROUP_W)), full((1, GROUP_W)),
                  full((1, GROUP_W)), blk(0), blk(1), blk(2), blk(2)],
        out_specs=blk(0),
        out_shape=jax.ShapeDtypeStruct((batch, seq, GROUP_W), BF16),
        scratch_shapes=[pltpu.VMEM((bpb * HEADS, HEAD_DIM, HEAD_DIM), F32)],
        compiler_params=_params(("parallel", "arbitrary")),
        name="retention",
    )(decay, xi_tab, zeta_tab, cd_tab, gn_w.reshape(1, GROUP_W),
      *(qkv.reshape(batch, seq, -1),) * 3, z_all.reshape(batch, seq, -1))
    return out.reshape(batch * seq, GROUP_W)


def _ssd_kernel(cw_ref, cb_ref, dtb_ref, a_ref, dsk_ref, nw_ref, xbc_ref, misc_ref, z_ref, o_ref,
                ext_ref, st_ref):
    n = pl.program_id(1)
    L = SSM_CHUNK
    hi = lax.Precision.HIGHEST

    @pl.when(n == 0)
    def _():
        st_ref[...] = jnp.zeros(st_ref.shape, F32)
        ext_ref[:, 0:8, :] = jnp.zeros((ext_ref.shape[0], 8, CONV_CH), F32)

    row = lax.broadcasted_iota(jnp.int32, (L, L), 0)
    col = lax.broadcasted_iota(jnp.int32, (L, L), 1)
    causal = row >= col
    tril = jnp.where(causal, 1.0, 0.0).astype(F32)
    dsk = dsk_ref[...]

    for bb in range(xbc_ref.shape[0]):
        raw = xbc_ref[bb]
        ext_ref[bb, 8:8 + L, :] = raw
        conv = cb_ref[...] + raw * cw_ref[CONV_W - 1:CONV_W, :]
        for w in range(CONV_W - 1):
            shift = CONV_W - 1 - w
            conv = conv + ext_ref[bb, 8 - shift:8 - shift + L, :] * cw_ref[w:w + 1, :]
        ext_ref[bb, 0:8, :] = raw[L - 8:L, :]
        xc = _silu(conv)
        x = xc[:, 0:GROUP_W]
        bm = xc[:, GROUP_W:GROUP_W + 2 * SSM_STATE].astype(BF16)
        cm = xc[:, GROUP_W + 2 * SSM_STATE:].astype(BF16)

        dt_full = jax.nn.softplus(misc_ref[bb] + dtb_ref[...])
        da = dt_full * a_ref[...]
        cs_col = jnp.dot(tril, da, precision=hi, preferred_element_type=F32)
        cs_row = lax.dot_general(da, tril, (((0,), (1,)), ((), ())), precision=hi,
                                 preferred_element_type=F32)

        outs = []
        for h in range(HEADS):
            g = h // 2
            c0 = DT_COL + h
            sl = slice(h * HEAD_DIM, (h + 1) * HEAD_DIM)
            gs = slice(g * SSM_STATE, (g + 1) * SSM_STATE)
            cs_c = cs_col[:, c0:c0 + 1]
            cs_r = cs_row[c0:c0 + 1, :]
            cs_last = cs_col[L - 1:L, c0:c0 + 1]
            xh = x[:, sl]
            xdt = xh * dt_full[:, c0:c0 + 1]
            seg = jnp.exp(jnp.where(causal, cs_c - cs_r, NEG_INF))
            cb = _dot_nt(cm[:, gs], bm[:, gs])
            y = _dot((cb * seg).astype(BF16), xdt.astype(BF16))
            prev = st_ref[bb * HEADS + h]
            y = y + _dot(cm[:, gs], prev.astype(BF16)) * jnp.exp(cs_c)
            y = y + dsk[:, sl] * xh
            dec = jnp.exp(cs_last - cs_c)
            st_ref[bb * HEADS + h] = prev * jnp.exp(cs_last) + _dot_tn(bm[:, gs], (xdt * dec).astype(BF16))
            outs.append(y)
        y = jnp.concatenate(outs, axis=-1) * _silu(z_ref[bb])
        ms = jnp.mean(y * y, axis=-1, keepdims=True)
        o_ref[bb] = (y * lax.rsqrt(ms + EPS) * nw_ref[...]).astype(o_ref.dtype)


def _ssd(xbc, misc, z_all, conv_w, conv_b, dt_bias, a_log, d_skip, norm_w, batch, seq, bpb):
    L = SSM_CHUNK
    nch = seq // L
    dtb = jnp.zeros((1, 128), F32).at[0, DT_COL:DT_COL + HEADS].set(dt_bias)
    a_full = jnp.zeros((1, 128), F32).at[0, DT_COL:DT_COL + HEADS].set(-jnp.exp(a_log))
    dsk = jnp.repeat(d_skip, HEAD_DIM)[None, :]

    def full(shape):
        return pl.BlockSpec(shape, lambda b, n: (0,) * len(shape))

    out = pl.pallas_call(
        _ssd_kernel,
        grid=(batch // bpb, nch),
        in_specs=[full((CONV_W, CONV_CH)), full((1, CONV_CH)), full((1, 128)), full((1, 128)),
                  full((1, GROUP_W)), full((1, GROUP_W)),
                  pl.BlockSpec((bpb, L, CONV_CH), lambda b, n: (b, n, 0)),
                  pl.BlockSpec((bpb, L, 128), lambda b, n: (b, n, 0)),
                  pl.BlockSpec((bpb, L, GROUP_W), lambda b, n: (b, n, 3))],
        out_specs=pl.BlockSpec((bpb, L, GROUP_W), lambda b, n: (b, n, 0)),
        out_shape=jax.ShapeDtypeStruct((batch, seq, GROUP_W), BF16),
        scratch_shapes=[pltpu.VMEM((bpb, 8 + L, CONV_CH), F32),
                        pltpu.VMEM((bpb * HEADS, SSM_STATE, HEAD_DIM), F32)],
        compiler_params=_params(("parallel", "arbitrary")),
        name="ssd",
    )(conv_w, conv_b.reshape(1, CONV_CH), dtb, a_full, dsk, norm_w.reshape(1, GROUP_W),
      xbc.reshape(batch, seq, -1), misc.reshape(batch, seq, -1), z_all.reshape(batch, seq, -1))
    return out.reshape(batch * seq, GROUP_W)


def _pick_tile(n, pref):
    t = pref
    while n % t:
        t //= 2
    return t


def kernel(x, norm_w, w_in, w_out, nsa_pe_k, nsa_pe_v, nsa_w_ck1, nsa_w_ck2, nsa_w_cv1, nsa_w_cv2,
           diff_lam_q1, diff_lam_k1, diff_lam_q2, diff_lam_k2, diff_subln_w, ret_gn_w,
           ssm_conv_w, ssm_conv_b, ssm_dt_bias, ssm_A_log, ssm_D, ssm_norm_w, final_norm_w):
    batch, seq, _ = x.shape
    depth = w_in.shape[0]
    m = batch * seq
    tm = _pick_tile(m, 512)
    tq = _pick_tile(seq, 512)
    tk = _pick_tile(seq, 256)
    tk_diff = _pick_tile(seq, 512)
    bpb = _pick_tile(batch, REC_BATCH)
    w_in_b = w_in.astype(BF16)
    w_r = _relayout_w_in(w_in_b)
    w_t = _relayout_w_in_t(w_in_b)
    w_out_b = w_out.astype(BF16)
    x2d = x.reshape(m, D_MODEL)
    projected = _in_proj(x2d, norm_w[0], w_r[0], w_t[0], tm)
    for i in range(depth):
        nsa_q, nsa_k2, nsa_cmp, misc, z_all, diff_qk, ret_qkv, xbc, vt_all = projected
        kc, vct = _nsa_compress(nsa_cmp, nsa_pe_k[i], nsa_pe_v[i], nsa_w_ck1[i], nsa_w_ck2[i],
                                nsa_w_cv1[i], nsa_w_cv2[i], batch, seq)
        y_nsa = _nsa_attention(nsa_q, nsa_k2, vt_all, kc, vct, misc, z_all, batch, seq, tq, tk)
        lam_p = jnp.stack([diff_lam_q1[i], diff_lam_k1[i], diff_lam_q2[i], diff_lam_k2[i]])
        y_diff = _diff_attention(diff_qk, vt_all, z_all, lam_p, diff_subln_w[i], batch, seq, i, tq, tk_diff)
        y_ret = _retention(ret_qkv, z_all, ret_gn_w[i], batch, seq, bpb)
        y_ssm = _ssd(xbc, misc, z_all, ssm_conv_w[i], ssm_conv_b[i], ssm_dt_bias[i], ssm_A_log[i],
                     ssm_D[i], ssm_norm_w[i], batch, seq, bpb)
        ys = (y_nsa, y_diff, y_ret, y_ssm)
        if i + 1 < depth:
            x2d, *projected = _out_in_proj(ys, w_out_b[i], x2d, norm_w[i + 1], w_r[i + 1], w_t[i + 1], tm)
        else:
            x2d = _out_proj(ys, w_out_b[i], x2d, final_norm_w, tm)
    return x2d.reshape(batch, seq, D_MODEL)
```

```python
import functools
import math

import numpy as np
import jax
import jax.numpy as jnp
from jax import lax
from jax.experimental import pallas as pl
from jax.experimental.pallas import tpu as pltpu

F32 = jnp.float32
BF16 = jnp.bfloat16
NEG_INF = float("-inf")
LOG2E = 1.4426950408889634

D_MODEL = 1024
DEPTH = 4
GROUP_W = 256
HEADS = 4
HEAD_DIM = 64
EPS = 1e-6
CMP_LEN = 32
CMP_STRIDE = 16
CMP_HIDDEN = 256
SLC_BLOCK = 64
SLC_SHIFT = 6
SLC_TOPK = 16
WINDOW = 512
DIFF_QK_DIM = 32
RET_CHUNK = 128
SSM_STATE = 128
SSM_CHUNK = 128
CONV_W = 4
CONV_CH = 768
N_ALIBI_HEADS = 8
LANES = 128
BF16_ROWS = 16
ACC_ROWS = HEAD_DIM + BF16_ROWS
QK_LOOKAHEAD = 6
ALIBI_ROWS = 3
POS_RADIX = 256
MASK_BIG = 2.0 ** 100
BLK_LANE0 = 64
REC_BATCH = 4

IN_LAYOUT = (
    ("nsa_q", 256), ("nsa_k_cmp", 64), ("nsa_v_cmp", 64), ("nsa_k_slc", 64), ("nsa_v_slc", 64),
    ("nsa_k_win", 64), ("nsa_v_win", 64), ("nsa_gate", 12), ("nsa_z", 256),
    ("diff_q", 256), ("diff_k", 256), ("diff_v", 256), ("diff_z", 256),
    ("ret_q", 256), ("ret_k", 256), ("ret_v", 256), ("ret_z", 256),
    ("ssm_z", 256), ("ssm_xbc", 768), ("ssm_dt", 4),
)
IN_OFF = {}
_o = 0
for _n, _w in IN_LAYOUT:
    IN_OFF[_n] = (_o, _w)
    _o += _w
IN_W = _o

GATE_COL = 0
DT_COL = 12
IN_OUTPUTS = (
    ("nsa_q", BF16, ("nsa_q",), 256),
    ("nsa_k2", BF16, ("nsa_k_slc", "nsa_k_win"), 128),
    ("nsa_cmp", F32, ("nsa_k_cmp", "nsa_v_cmp"), 128),
    ("misc", F32, ("nsa_gate", "ssm_dt"), 128),
    ("z_all", F32, ("nsa_z", "diff_z", "ret_z", "ssm_z"), 1024),
    ("diff_qk", BF16, ("diff_q", "diff_k"), 512),
    ("ret_qkv", BF16, ("ret_q", "ret_k", "ret_v"), 768),
    ("xbc", F32, ("ssm_xbc",), 768),
)
IN_T_SRC = ("diff_v", "nsa_v_slc", "nsa_v_win")
IN_T_ROWS = 384
DIFF_VT_BLK = 0
NSA_VT_BLK = 2
IN_SEGS = []
_o = 0
for _n, _dt, _src, _w in IN_OUTPUTS:
    IN_SEGS.append((_o, _o + _w))
    _o += _w
IN_WP = _o

VMEM_LIMIT = 56 * 1024 * 1024


def _alibi_slopes():
    return [2.0 ** (-8.0 * (i + 1) / N_ALIBI_HEADS) for i in range(N_ALIBI_HEADS)]


NSA_SLOPES = _alibi_slopes()[0::2]
DIFF_SLOPES = _alibi_slopes()[1::2]


def _silu(x):
    return x * jax.nn.sigmoid(x)


def _dot(a, b):
    return jnp.dot(a, b, preferred_element_type=F32)


def _dot_nt(a, b):
    return lax.dot_general(a, b, (((1,), (1,)), ((), ())), preferred_element_type=F32)


def _dot_tn(a, b):
    return lax.dot_general(a, b, (((0,), (0,)), ((), ())), preferred_element_type=F32)


def _params(sem):
    return pltpu.CompilerParams(dimension_semantics=sem, vmem_limit_bytes=VMEM_LIMIT)


def _relayout_w_in(w_in):
    cols = []
    for _n, _dt, src, width in IN_OUTPUTS:
        used = 0
        for s in src:
            off, w = IN_OFF[s]
            cols.append(w_in[:, :, off:off + w])
            used += w
        if used < width:
            cols.append(jnp.zeros(w_in.shape[:2] + (width - used,), w_in.dtype))
    return jnp.concatenate(cols, axis=-1)


def _relayout_w_in_t(w_in):
    cols = [w_in[:, :, IN_OFF[s][0]:IN_OFF[s][0] + IN_OFF[s][1]] for s in IN_T_SRC]
    return jnp.swapaxes(jnp.concatenate(cols, axis=-1), 1, 2)


def _norm_project(x, nw_ref, w_ref, wt_ref, out_refs):
    ms = jnp.mean(x * x, axis=-1, keepdims=True)
    h = (x * lax.rsqrt(ms + EPS) * nw_ref[...]).astype(BF16)
    for ref, (a, b) in zip(out_refs[:-1], IN_SEGS):
        ref[...] = _dot(h, w_ref[:, a:b]).astype(ref.dtype)
    out_refs[-1][...] = _dot_nt(wt_ref[...], h).astype(BF16)


def _in_proj_kernel(x_ref, nw_ref, w_ref, wt_ref, *out_refs):
    _norm_project(x_ref[...], nw_ref, w_ref, wt_ref, out_refs)


def _in_proj_outputs(m, tm):
    out_shape = [jax.ShapeDtypeStruct((m, b - a), dt) for (_n, dt, _s, _w), (a, b) in zip(IN_OUTPUTS, IN_SEGS)]
    out_specs = [pl.BlockSpec((tm, b - a), lambda i: (i, 0)) for (a, b) in IN_SEGS]
    out_shape.append(jax.ShapeDtypeStruct((IN_T_ROWS, m), BF16))
    out_specs.append(pl.BlockSpec((IN_T_ROWS, tm), lambda i: (0, i)))
    return out_shape, out_specs


def _in_proj(x2d, norm_w, w_r, w_t, tm):
    m = x2d.shape[0]
    out_shape, out_specs = _in_proj_outputs(m, tm)
    return pl.pallas_call(
        _in_proj_kernel,
        grid=(m // tm,),
        in_specs=[pl.BlockSpec((tm, D_MODEL), lambda i: (i, 0)),
                  pl.BlockSpec((1, D_MODEL), lambda i: (0, 0)),
                  pl.BlockSpec((D_MODEL, IN_WP), lambda i: (0, 0)),
                  pl.BlockSpec((IN_T_ROWS, D_MODEL), lambda i: (0, 0))],
        out_specs=out_specs,
        out_shape=out_shape,
        compiler_params=_params(("parallel",)),
        name="in_proj",
    )(x2d, norm_w.reshape(1, D_MODEL), w_r, w_t)


def _out_proj_kernel(y0_ref, y1_ref, y2_ref, y3_ref, w_ref, x_ref, fw_ref, o_ref):
    acc = x_ref[...]
    for g, y_ref in enumerate((y0_ref, y1_ref, y2_ref, y3_ref)):
        acc = acc + _dot(y_ref[...], w_ref[g * GROUP_W:(g + 1) * GROUP_W, :])
    ms = jnp.mean(acc * acc, axis=-1, keepdims=True)
    o_ref[...] = acc * lax.rsqrt(ms + EPS) * fw_ref[...]


def _out_proj(ys, w_out_b, x2d, final_w, tm):
    m = x2d.shape[0]
    yspec = pl.BlockSpec((tm, GROUP_W), lambda i: (i, 0))
    return pl.pallas_call(
        _out_proj_kernel,
        grid=(m // tm,),
        in_specs=[yspec, yspec, yspec, yspec,
                  pl.BlockSpec((D_MODEL, D_MODEL), lambda i: (0, 0)),
                  pl.BlockSpec((tm, D_MODEL), lambda i: (i, 0)),
                  pl.BlockSpec((1, D_MODEL), lambda i: (0, 0))],
        out_specs=pl.BlockSpec((tm, D_MODEL), lambda i: (i, 0)),
        out_shape=jax.ShapeDtypeStruct((m, D_MODEL), F32),
        compiler_params=_params(("parallel",)),
        name="out_proj",
    )(*ys, w_out_b, x2d, final_w.reshape(1, D_MODEL))


def _out_in_proj_kernel(y0_ref, y1_ref, y2_ref, y3_ref, wo_ref, x_ref, nw_ref, w_ref, wt_ref, xo_ref, *out_refs):
    acc = x_ref[...]
    for g, y_ref in enumerate((y0_ref, y1_ref, y2_ref, y3_ref)):
        acc = acc + _dot(y_ref[...], wo_ref[g * GROUP_W:(g + 1) * GROUP_W, :])
    xo_ref[...] = acc
    _norm_project(acc, nw_ref, w_ref, wt_ref, out_refs)


def _out_in_proj(ys, w_out_b, x2d, norm_w, w_r, w_t, tm):
    m = x2d.shape[0]
    yspec = pl.BlockSpec((tm, GROUP_W), lambda i: (i, 0))
    xspec = pl.BlockSpec((tm, D_MODEL), lambda i: (i, 0))
    out_shape, out_specs = _in_proj_outputs(m, tm)
    return pl.pallas_call(
        _out_in_proj_kernel,
        grid=(m // tm,),
        in_specs=[yspec, yspec, yspec, yspec,
                  pl.BlockSpec((D_MODEL, D_MODEL), lambda i: (0, 0)),
                  xspec,
                  pl.BlockSpec((1, D_MODEL), lambda i: (0, 0)),
                  pl.BlockSpec((D_MODEL, IN_WP), lambda i: (0, 0)),
                  pl.BlockSpec((IN_T_ROWS, D_MODEL), lambda i: (0, 0))],
        out_specs=[xspec] + out_specs,
        out_shape=[jax.ShapeDtypeStruct((m, D_MODEL), F32)] + out_shape,
        compiler_params=_params(("parallel",)),
        name="out_in_proj",
    )(*ys, w_out_b, x2d, norm_w.reshape(1, D_MODEL), w_r, w_t)


def _stage_scores(s_ref, slot, cols, s, keep=None):
    if keep is not None:
        s = jnp.where(keep, s, NEG_INF)
    s_ref[slot, :, cols] = s
    return jnp.max(s, axis=0, keepdims=True)


def _flash_update_t(s, s_max, c1, shift, vt_ext, m_ref, acc_ref, idx, cols):
    m_old = m_ref[idx, :, cols]
    m_new = jnp.maximum(m_old, c1 * s_max + shift)
    alpha = jnp.exp2(m_old - m_new)
    p = jnp.exp2(c1 * s() - (m_new - shift))
    acc_ref[idx, :, cols] = alpha * acc_ref[idx, :, cols] + _dot(vt_ext, p.astype(BF16))
    m_ref[idx, :, cols] = m_new


def _bf16_pieces(x, n=3):
    out = []
    for _ in range(n):
        p = float(np.asarray(x, np.float32).astype(BF16).astype(np.float32))
        out.append(p)
        x = x - p
    return out


def _key_pos_features(tk):
    lane = lax.broadcasted_iota(jnp.int32, (tk, LANES), 1)
    row = lax.broadcasted_iota(jnp.int32, (tk, LANES), 0)
    out = jnp.zeros((tk, LANES), jnp.int32)
    for d in range(-(-tk // POS_RADIX)):
        digit = jnp.bitwise_and(jnp.right_shift(row, 8 * d), POS_RADIX - 1)
        out = jnp.where((lane >= ALIBI_ROWS * d) & (lane < ALIBI_ROWS * (d + 1)), digit, out)
    return out.astype(F32).astype(BF16)


def _alibi_rows(beta, tq, rows=LANES, tk=POS_RADIX):
    row = lax.broadcasted_iota(jnp.int32, (rows, tq), 0)
    out = jnp.zeros((rows, tq), F32)
    for d in range(-(-tk // POS_RADIX)):
        for r, piece in enumerate(_bf16_pieces(beta * POS_RADIX ** d, ALIBI_ROWS)):
            out = jnp.where(row == ALIBI_ROWS * d + r, piece, out)
    return out.astype(BF16)


def _normalized(acc):
    return acc[0:HEAD_DIM] / jnp.maximum(acc[HEAD_DIM:HEAD_DIM + 1], 1e-30)


def _rel_t(tk, tq):
    return lax.broadcasted_iota(jnp.int32, (tk, tq), 1) - lax.broadcasted_iota(jnp.int32, (tk, tq), 0)


def _diff_kernel(lam_ref, sw_ref, q_ref, k_ref, vt_ref, z_ref, o_ref, m_ref, acc_ref, wq_ref, s_ref,
                 *, tq, tk, lam_init):
    qi = pl.program_id(1)
    kpq = tq // tk
    scale = DIFF_QK_DIM ** -0.5
    c1 = scale * LOG2E
    m_ref[...] = jnp.full(m_ref.shape, NEG_INF, F32)
    acc_ref[...] = jnp.zeros(acc_ref.shape, F32)
    rel = _rel_t(tk, tq)

    qt = q_ref[...].astype(F32).T.astype(BF16)
    row = lax.broadcasted_iota(jnp.int32, (LANES, tq), 0)
    for j in range(2 * HEADS):
        g, r0 = divmod(j * DIFF_QK_DIM, LANES)
        qg = qt[g * LANES:(g + 1) * LANES]
        wq_ref[j, 0:LANES, :] = jnp.where((row >= r0) & (row < r0 + DIFF_QK_DIM), qg, jnp.zeros_like(qg))
        wq_ref[j, LANES:2 * LANES, :] = _alibi_rows(DIFF_SLOPES[j // 2] / scale, tq, tk=tk)

    ones = jnp.ones((BF16_ROWS, tk), BF16)
    kpos = _key_pos_features(tk)

    n_maps = 2 * HEADS
    n_slots = s_ref.shape[0]

    def tiles(kis, mask_offs):
        loaded = []
        for ki in kis:
            start = pl.multiple_of(ki * tk, tk)
            loaded.append((k_ref[pl.ds(start, tk), :],
                           vt_ref[:, pl.ds(start, tk)],
                           (ki * tk).astype(F32)))
        items = [(t, j) for t in range(len(kis)) for j in range(n_maps)]
        cols = slice(0, tq)

        def scores(i):
            t, j = items[i]
            g = (j * DIFF_QK_DIM) // LANES
            lhs = jnp.concatenate([loaded[t][0][:, g * LANES:(g + 1) * LANES], kpos], axis=1)
            s_ref[i % n_slots] = _dot(lhs, wq_ref[j])

        def staged(i):
            t = items[i][0]
            if mask_offs[t] is None:
                return lambda: s_ref[i % n_slots]
            return lambda: jnp.where(rel >= mask_offs[t], s_ref[i % n_slots], NEG_INF)

        for i in range(QK_LOOKAHEAD):
            scores(i)
        for i, (t, j) in enumerate(items):
            h = j // 2
            if i + QK_LOOKAHEAD < len(items):
                scores(i + QK_LOOKAHEAD)
            _k, vt, key0 = loaded[t]
            vt_ext = jnp.concatenate([vt[h * HEAD_DIM:(h + 1) * HEAD_DIM], ones], axis=0)
            s = staged(i)
            _flash_update_t(s, jnp.max(s(), axis=0, keepdims=True), c1, (DIFF_SLOPES[h] * LOG2E) * key0,
                            vt_ext, m_ref, acc_ref, j, cols)

    def body(i, carry):
        tiles([i * kpq + d for d in range(kpq)], [None] * kpq)
        return carry

    lax.fori_loop(0, qi, body, 0)
    tiles([qi * kpq + d for d in range(kpq)], [d * tk for d in range(kpq)])

    lp = lam_ref[...]
    lam = (jnp.exp(jnp.sum(lp[0:1] * lp[1:2], axis=-1, keepdims=True))
           - jnp.exp(jnp.sum(lp[2:3] * lp[3:4], axis=-1, keepdims=True)) + lam_init)
    sw = sw_ref[...]
    o_t = jnp.concatenate([_normalized(acc_ref[2 * h]) - lam * _normalized(acc_ref[2 * h + 1])
                           for h in range(HEADS)], axis=0)
    o = o_t.T
    outs = []
    for h in range(HEADS):
        oh = o[:, h * HEAD_DIM:(h + 1) * HEAD_DIM]
        ms = jnp.mean(oh * oh, axis=-1, keepdims=True)
        outs.append(oh * lax.rsqrt(ms + EPS) * sw * (1.0 - lam_init))
    y = jnp.concatenate(outs, axis=-1) * _silu(z_ref[...])
    o_ref[...] = y.astype(o_ref.dtype)


def _diff_attention(qk, vt_all, z_all, lam_p, subln_w, batch, seq, layer_idx, tq, tk):
    nq = seq // tq
    lam_init = 0.8 - 0.6 * math.exp(-0.3 * layer_idx)
    return pl.pallas_call(
        functools.partial(_diff_kernel, tq=tq, tk=tk, lam_init=lam_init),
        grid=(batch, nq),
        in_specs=[pl.BlockSpec((4, DIFF_QK_DIM), lambda b, i: (0, 0)),
                  pl.BlockSpec((1, HEAD_DIM), lambda b, i: (0, 0)),
                  pl.BlockSpec((tq, GROUP_W), lambda b, i: (b * nq + i, 0)),
                  pl.BlockSpec((seq, GROUP_W), lambda b, i: (b, 1)),
                  pl.BlockSpec((GROUP_W, seq), lambda b, i: (DIFF_VT_BLK, b)),
                  pl.BlockSpec((tq, GROUP_W), lambda b, i: (b * nq + i, 1))],
        out_specs=pl.BlockSpec((tq, GROUP_W), lambda b, i: (b * nq + i, 0)),
        out_shape=jax.ShapeDtypeStruct((batch * seq, GROUP_W), BF16),
        scratch_shapes=[pltpu.VMEM((2 * HEADS, 1, tq), F32),
                        pltpu.VMEM((2 * HEADS, ACC_ROWS, tq), F32),
                        pltpu.VMEM((2 * HEADS, 2 * LANES, tq), BF16),
                        pltpu.VMEM((QK_LOOKAHEAD + 1, tk, tq), F32)],
        compiler_params=_params(("parallel", "parallel")),
        name="diff_attention",
    )(lam_p, subln_w.reshape(1, HEAD_DIM), qk, qk, vt_all, z_all)


def _nsa_compress_kernel(cmp_ref, pe_ref, w1_ref, wk2_ref, wv2t_ref, kc_ref, vct_ref):
    ng = kc_ref.shape[1]
    top = jnp.zeros((ng, 2 * CMP_HIDDEN), F32)
    bot = jnp.zeros((ng, 2 * CMP_HIDDEN), F32)
    for p in range(CMP_STRIDE):
        x = cmp_ref[pl.ds(p, ng, stride=CMP_STRIDE), :]
        top = top + _dot((x + pe_ref[p:p + 1, :]).astype(BF16), w1_ref[p])
        bot = bot + _dot((x + pe_ref[CMP_STRIDE + p:CMP_STRIDE + p + 1, :]).astype(BF16), w1_ref[CMP_STRIDE + p])
    hid = _silu(top + pltpu.roll(bot, ng - 1, 0)).astype(BF16)
    hk, hv = hid[:, 0:CMP_HIDDEN], hid[:, CMP_HIDDEN:]
    kc = _dot(hk, wk2_ref[...])
    lane = lax.broadcasted_iota(jnp.int32, kc.shape, 1)
    blk = lax.broadcasted_iota(jnp.int32, kc.shape, 0).astype(F32)
    kc = jnp.where((lane >= HEAD_DIM) & (lane < HEAD_DIM + ALIBI_ROWS), blk, kc)
    kc_ref[0] = kc.astype(kc_ref.dtype)
    vct_ref[0] = _dot_nt(wv2t_ref[...], hv).astype(vct_ref.dtype)


def _nsa_compress(cmp2d, pe_k, pe_v, w_ck1, w_ck2, w_cv1, w_cv2, batch, seq):
    ng = seq // CMP_STRIDE
    wk = w_ck1.reshape(CMP_LEN, HEAD_DIM, CMP_HIDDEN)
    wv = w_cv1.reshape(CMP_LEN, HEAD_DIM, CMP_HIDDEN)
    w1 = jnp.concatenate([jnp.pad(wk, ((0, 0), (0, 0), (0, CMP_HIDDEN))),
                          jnp.pad(wv, ((0, 0), (0, 0), (CMP_HIDDEN, 0)))], axis=1).astype(BF16)
    pe = jnp.concatenate([pe_k, pe_v], axis=1)

    def full(shape):
        return pl.BlockSpec(shape, lambda b: (0,) * len(shape))

    return pl.pallas_call(
        _nsa_compress_kernel,
        grid=(batch,),
        in_specs=[pl.BlockSpec((seq, LANES), lambda b: (b, 0)), full((CMP_LEN, LANES)),
                  full((CMP_LEN, LANES, 2 * CMP_HIDDEN)), full((CMP_HIDDEN, LANES)),
                  full((HEAD_DIM, CMP_HIDDEN))],
        out_specs=[pl.BlockSpec((1, ng, LANES), lambda b: (b, 0, 0)),
                   pl.BlockSpec((1, HEAD_DIM, ng), lambda b: (b, 0, 0))],
        out_shape=[jax.ShapeDtypeStruct((batch, ng, LANES), BF16),
                   jax.ShapeDtypeStruct((batch, HEAD_DIM, ng), BF16)],
        compiler_params=_params(("parallel",)),
        name="nsa_compress",
    )(cmp2d, pe, w1, jnp.pad(w_ck2, ((0, 0), (0, LANES - HEAD_DIM))).astype(BF16), w_cv2.T.astype(BF16))


def _nsa_kernel(ovl_ref, q_ref, k_ref, vt_ref, kc_ref, vct_ref, misc_ref, z_ref, o_ref,
                m_ref, acc_ref, wq_ref, s_ref, sc_ref, *, tq, tk, seq):
    qi = pl.program_id(1)
    kpq = tq // tk
    scale = HEAD_DIM ** -0.5
    c1 = scale * LOG2E
    ng = seq // CMP_STRIDE
    ns = seq // SLC_BLOCK
    top = min(SLC_TOPK, ns)
    m_ref[...] = jnp.full(m_ref.shape, NEG_INF, F32)
    acc_ref[...] = jnp.zeros(acc_ref.shape, F32)
    rel = _rel_t(tk, tq)

    qt = q_ref[...].astype(F32).T.astype(BF16)
    zeros_q = jnp.zeros((HEAD_DIM, tq), BF16)
    for h in range(HEADS):
        qh = qt[h * HEAD_DIM:(h + 1) * HEAD_DIM]
        alibi = _alibi_rows(NSA_SLOPES[h] / scale, tq, tk=tk)
        wq_ref[h, 0:LANES, :] = jnp.concatenate([qh, zeros_q], axis=0)
        wq_ref[h, LANES:2 * LANES, :] = alibi
        wq_ref[HEADS + h, 0:LANES, :] = jnp.concatenate([zeros_q, qh], axis=0)
        wq_ref[HEADS + h, LANES:2 * LANES, :] = alibi

    t_lane = qi * tq + lax.broadcasted_iota(jnp.int32, (1, tq), 1)
    n_sub = lax.broadcasted_iota(jnp.int32, (ng, 1), 0)
    c_valid = (n_sub * CMP_STRIDE + (CMP_LEN - 1)) <= t_lane
    kc = kc_ref[0]
    vct_ext = jnp.concatenate([vct_ref[0], jnp.ones((BF16_ROWS, ng), BF16)], axis=0)
    ovl = ovl_ref[...]
    cmp_max = []
    for h in range(HEADS):
        rhs = jnp.concatenate([qt[h * HEAD_DIM:(h + 1) * HEAD_DIM],
                               _alibi_rows(NSA_SLOPES[h] * CMP_STRIDE / scale, tq, HEAD_DIM)], axis=0)
        cmp_max.append(_stage_scores(sc_ref, h, slice(0, tq), _dot(kc, rhs), c_valid))
    imp = jnp.zeros((ns, tq), F32)
    o_cmp = []
    for h in range(HEADS):
        mx = jnp.where(cmp_max[h] > NEG_INF, cmp_max[h], 0.0)
        pb = jnp.exp2(c1 * sc_ref[h] - c1 * mx).astype(BF16)
        o_ext = _dot(vct_ext, pb)
        r = 1.0 / jnp.maximum(o_ext[HEAD_DIM:HEAD_DIM + 1], 1e-30)
        o_cmp.append(o_ext[0:HEAD_DIM] * r)
        imp = imp + _dot(ovl, pb) * r

    def select_blocks():
        j_sub = lax.broadcasted_iota(jnp.int32, (ns, 1), 0)
        j_sub_f = j_sub.astype(F32)
        cur = jnp.right_shift(t_lane, SLC_SHIFT)
        forced = (j_sub == 0) | (j_sub == cur) | (j_sub == cur - 1)
        valid = (j_sub * SLC_BLOCK) <= t_lane
        score = jnp.where(forced, 1e30, jnp.where(valid, imp, -1.0))
        sel = jnp.zeros((ns, tq), F32)
        for _ in range(top):
            mx = jnp.max(score, axis=0, keepdims=True)
            idx = jnp.min(jnp.where(score == mx, j_sub_f, float(ns)), axis=0, keepdims=True)
            pick = j_sub_f == idx
            sel = jnp.where(pick, 1.0, sel)
            score = jnp.where(pick, -2.0, score)
        unsel = jnp.where(sel > 0.5, 0.0, -MASK_BIG).astype(BF16)
        for h in range(HEADS):
            wq_ref[h, LANES + BLK_LANE0:LANES + BLK_LANE0 + ns, :] = unsel

    aux_lane = lax.broadcasted_iota(jnp.int32, (tk, LANES), 1)
    aux_blk = jnp.right_shift(lax.broadcasted_iota(jnp.int32, (tk, LANES), 0), SLC_SHIFT) + BLK_LANE0
    kpos = _key_pos_features(tk)
    ones = jnp.ones((BF16_ROWS, tk), BF16)

    n_slots = s_ref.shape[0]

    def tiles(window, kis, cols, mask_offs=None):
        base = HEADS if window else 0
        loaded = []
        for t, ki in enumerate(kis):
            start = pl.multiple_of(ki * tk, tk)
            k = k_ref[pl.ds(start, tk), :]
            if window:
                vt = vt_ref[HEAD_DIM:2 * HEAD_DIM, pl.ds(start, tk)]
                aux = kpos
                dist = rel[:, cols[t]] + (qi * tq - ki * tk)
                keep = (dist >= 0) & (dist < WINDOW)
            else:
                vt = vt_ref[0:HEAD_DIM, pl.ds(start, tk)]
                onehot = aux_lane == aux_blk + ki * (tk // SLC_BLOCK)
                aux = jnp.where(onehot, jnp.ones_like(kpos), kpos)
                keep = None
            loaded.append((jnp.concatenate([k, aux], axis=1),
                           jnp.concatenate([vt, ones], axis=0), (ki * tk).astype(F32), keep))
        items = [(t, h) for t in range(len(kis)) for h in range(HEADS)]
        s_max = {}

        def scores(i):
            t, h = items[i]
            if window:
                keep = loaded[t][3]
            else:
                keep = None if mask_offs[t] is None else rel[:, cols[t]] >= mask_offs[t]
            s_max[i] = _stage_scores(s_ref, i % n_slots, cols[t],
                                     _dot(loaded[t][0], wq_ref[base + h, :, cols[t]]), keep)

        for i in range(QK_LOOKAHEAD):
            scores(i)
        for i, (t, h) in enumerate(items):
            if i + QK_LOOKAHEAD < len(items):
                scores(i + QK_LOOKAHEAD)
            _lhs, vt_ext, key0, _keep = loaded[t]
            _flash_update_t(lambda i=i, t=t: s_ref[i % n_slots, :, cols[t]], s_max.pop(i), c1,
                            (NSA_SLOPES[h] * LOG2E) * key0, vt_ext, m_ref, acc_ref, base + h, cols[t])

    all_cols = slice(0, tq)
    diag_cols = [slice(d * tk, tq) for d in range(kpq)]

    def body(i, carry):
        tiles(False, [i * kpq + d for d in range(kpq)], [all_cols] * kpq, [None] * kpq)
        return carry

    tiles(True, [qi * kpq + d for d in range(kpq)], diag_cols)
    select_blocks()

    lax.fori_loop(0, qi, body, 0)
    tiles(False, [qi * kpq + d for d in range(kpq)], diag_cols, [d * tk for d in range(kpq)])

    n_back = (WINDOW + tk - 1) // tk
    for g in range((n_back + kpq - 1) // kpq):
        backs = list(range(g * kpq + 1, min((g + 1) * kpq, n_back) + 1))
        back_cols = [slice(0, min(tq, -(-(WINDOW - (back - 1) * tk - 1) // LANES) * LANES)) for back in backs]

        @pl.when(qi > g)
        def _():
            tiles(True, [qi * kpq - back for back in backs], back_cols)

    g_t = jax.nn.sigmoid(misc_ref[...]).T
    outs = []
    for h in range(HEADS):
        r0 = GATE_COL + 3 * h
        outs.append(g_t[r0:r0 + 1] * o_cmp[h] + g_t[r0 + 1:r0 + 2] * _normalized(acc_ref[h])
                    + g_t[r0 + 2:r0 + 3] * _normalized(acc_ref[HEADS + h]))
    y = jnp.concatenate(outs, axis=0).T * _silu(z_ref[...])
    o_ref[...] = y.astype(o_ref.dtype)


def _overlap_t(seq):
    nc = (seq - CMP_LEN) // CMP_STRIDE + 1
    ng = seq // CMP_STRIDE
    ns = seq // SLC_BLOCK
    c_start = np.arange(ng) * CMP_STRIDE
    c_end = c_start + CMP_LEN - 1
    s_start = np.arange(ns) * SLC_BLOCK
    s_end = s_start + SLC_BLOCK - 1
    ov = (c_start[None, :] <= s_end[:, None]) & (c_end[None, :] >= s_start[:, None]) & (np.arange(ng)[None, :] < nc)
    return jnp.asarray(ov.astype(np.float32), dtype=BF16)


def _nsa_attention(q, k2, vt_all, kc, vct, misc, z_all, batch, seq, tq, tk):
    nq = seq // tq
    ng = seq // CMP_STRIDE
    ns = seq // SLC_BLOCK
    assert tq - tk < WINDOW and tk % SLC_BLOCK == 0 and ns <= LANES - BLK_LANE0
    return pl.pallas_call(
        functools.partial(_nsa_kernel, tq=tq, tk=tk, seq=seq),
        grid=(batch, nq),
        in_specs=[pl.BlockSpec((ns, ng), lambda b, i: (0, 0)),
                  pl.BlockSpec((tq, GROUP_W), lambda b, i: (b * nq + i, 0)),
                  pl.BlockSpec((seq, LANES), lambda b, i: (b, 0)),
                  pl.BlockSpec((2 * HEAD_DIM, seq), lambda b, i: (NSA_VT_BLK, b)),
                  pl.BlockSpec((1, ng, LANES), lambda b, i: (b, 0, 0)),
                  pl.BlockSpec((1, HEAD_DIM, ng), lambda b, i: (b, 0, 0)),
                  pl.BlockSpec((tq, LANES), lambda b, i: (b * nq + i, 0)),
                  pl.BlockSpec((tq, GROUP_W), lambda b, i: (b * nq + i, 0))],
        out_specs=pl.BlockSpec((tq, GROUP_W), lambda b, i: (b * nq + i, 0)),
        out_shape=jax.ShapeDtypeStruct((batch * seq, GROUP_W), BF16),
        scratch_shapes=[pltpu.VMEM((2 * HEADS, 1, tq), F32),
                        pltpu.VMEM((2 * HEADS, ACC_ROWS, tq), F32),
                        pltpu.VMEM((2 * HEADS, 2 * LANES, tq), BF16),
                        pltpu.VMEM((2 * HEADS, tk, tq), F32),
                        pltpu.VMEM((HEADS, ng, tq), F32)],
        compiler_params=_params(("parallel", "parallel")),
        name="nsa_attention",
    )(_overlap_t(seq), q, k2, vt_all, kc, vct, misc, z_all)


def _ret_tables():
    c = RET_CHUNK
    h = np.arange(HEADS, dtype=np.float32)
    log_g = jnp.log(1.0 - 2.0 ** (-5.0 - jnp.asarray(h)))
    pos = jnp.arange(c, dtype=F32)
    rel = pos[:, None] - pos[None, :]
    decay = jnp.where(rel >= 0, jnp.exp(log_g[:, None, None] * jnp.maximum(rel, 0.0)), 0.0)
    xi = jnp.exp(log_g[:, None] * (pos + 1.0))
    zeta = jnp.exp(log_g[:, None] * (c - 1.0 - pos))
    chunk_decay = jnp.exp(log_g * c)
    xi_tab = jnp.repeat(xi.T, HEAD_DIM, axis=1)
    zeta_tab = jnp.repeat(zeta.T, HEAD_DIM, axis=1)
    cd_tab = jnp.repeat(chunk_decay, HEAD_DIM)[None, :]
    return decay, xi_tab, zeta_tab, cd_tab


def _ret_body(decay_ref, xi_ref, zeta_ref, cd_ref, gn_ref, q_ref, k_ref, v_ref, z_ref, o_ref, st_ref):
    xi = xi_ref[...]
    cd = cd_ref[...]
    for bb in range(q_ref.shape[0]):
        q = (q_ref[bb].astype(F32) * (HEAD_DIM ** -0.5)).astype(BF16)
        k = k_ref[bb]
        v = v_ref[bb]
        kz_t = (k.astype(F32) * zeta_ref[...]).T.astype(BF16)
        outs = []
        for h in range(HEADS):
            sl = slice(h * HEAD_DIM, (h + 1) * HEAD_DIM)
            qh, kh, vh = q[:, sl], k[:, sl], v[:, sl]
            prev = st_ref[bb * HEADS + h]
            inner = (_dot_nt(qh, kh) * decay_ref[h]).astype(BF16)
            o = _dot(inner, vh) + _dot(qh, prev.astype(BF16)) * xi[:, sl]
            st_ref[bb * HEADS + h] = prev * cd[:, sl] + _dot(kz_t[sl, :], vh)
            mu = jnp.mean(o, axis=-1, keepdims=True)
            d = o - mu
            var = jnp.mean(d * d, axis=-1, keepdims=True)
            outs.append(d * lax.rsqrt(var + EPS))
        y = jnp.concatenate(outs, axis=-1) * gn_ref[...] * _silu(z_ref[bb])
        o_ref[bb] = y.astype(o_ref.dtype)


def _ssd_body(cw_ref, cb_ref, dtb_ref, a_ref, dsk_ref, nw_ref, xbc_ref, misc_ref, z_ref, o_ref,
              ext_ref, st_ref):
    L = SSM_CHUNK
    hi = lax.Precision.HIGHEST
    row = lax.broadcasted_iota(jnp.int32, (L, L), 0)
    col = lax.broadcasted_iota(jnp.int32, (L, L), 1)
    causal = row >= col
    tril = jnp.where(causal, 1.0, 0.0).astype(F32)
    dsk = dsk_ref[...]

    for bb in range(xbc_ref.shape[0]):
        raw = xbc_ref[bb]
        ext_ref[bb, 8:8 + L, :] = raw
        conv = cb_ref[...] + raw * cw_ref[CONV_W - 1:CONV_W, :]
        for w in range(CONV_W - 1):
            shift = CONV_W - 1 - w
            conv = conv + ext_ref[bb, 8 - shift:8 - shift + L, :] * cw_ref[w:w + 1, :]
        ext_ref[bb, 0:8, :] = raw[L - 8:L, :]
        xc = _silu(conv)
        x = xc[:, 0:GROUP_W]
        bm = xc[:, GROUP_W:GROUP_W + 2 * SSM_STATE].astype(BF16)
        cm = xc[:, GROUP_W + 2 * SSM_STATE:].astype(BF16)

        dt_full = jax.nn.softplus(misc_ref[bb] + dtb_ref[...])
        da = dt_full * a_ref[...]
        cs_col = jnp.dot(tril, da, precision=hi, preferred_element_type=F32)
        cs_row = lax.dot_general(da, tril, (((0,), (1,)), ((), ())), precision=hi,
                                 preferred_element_type=F32)

        outs = []
        for h in range(HEADS):
            g = h // 2
            c0 = DT_COL + h
            sl = slice(h * HEAD_DIM, (h + 1) * HEAD_DIM)
            gs = slice(g * SSM_STATE, (g + 1) * SSM_STATE)
            cs_c = cs_col[:, c0:c0 + 1]
            cs_r = cs_row[c0:c0 + 1, :]
            cs_last = cs_col[L - 1:L, c0:c0 + 1]
            xh = x[:, sl]
            xdt = xh * dt_full[:, c0:c0 + 1]
            seg = jnp.exp(jnp.where(causal, cs_c - cs_r, NEG_INF))
            cb = _dot_nt(cm[:, gs], bm[:, gs])
            y = _dot((cb * seg).astype(BF16), xdt.astype(BF16))
            prev = st_ref[bb * HEADS + h]
            y = y + _dot(cm[:, gs], prev.astype(BF16)) * jnp.exp(cs_c)
            y = y + dsk[:, sl] * xh
            dec = jnp.exp(cs_last - cs_c)
            st_ref[bb * HEADS + h] = prev * jnp.exp(cs_last) + _dot_tn(bm[:, gs], (xdt * dec).astype(BF16))
            outs.append(y)
        y = jnp.concatenate(outs, axis=-1) * _silu(z_ref[bb])
        ms = jnp.mean(y * y, axis=-1, keepdims=True)
        o_ref[bb] = (y * lax.rsqrt(ms + EPS) * nw_ref[...]).astype(o_ref.dtype)


N_RET_IN = 9
N_SSD_IN = 9


def _recurrent_kernel(*refs):
    ret_in, ssd_in = refs[:N_RET_IN], refs[N_RET_IN:N_RET_IN + N_SSD_IN]
    ret_out, ssd_out, ret_st, ssd_ext, ssd_st = refs[N_RET_IN + N_SSD_IN:]

    @pl.when(pl.program_id(1) == 0)
    def _():
        ret_st[...] = jnp.zeros(ret_st.shape, F32)
        ssd_st[...] = jnp.zeros(ssd_st.shape, F32)
        ssd_ext[:, 0:8, :] = jnp.zeros((ssd_ext.shape[0], 8, CONV_CH), F32)

    _ret_body(*ret_in, ret_out, ret_st)
    _ssd_body(*ssd_in, ssd_out, ssd_ext, ssd_st)


def _recurrent_mixers(ret_qkv, xbc, misc, z_all, gn_w, conv_w, conv_b, dt_bias, a_log, d_skip, norm_w,
                      batch, seq, bpb):
    L = SSM_CHUNK
    assert RET_CHUNK == L
    nch = seq // L
    decay, xi_tab, zeta_tab, cd_tab = _ret_tables()
    dtb = jnp.zeros((1, 128), F32).at[0, DT_COL:DT_COL + HEADS].set(dt_bias)
    a_full = jnp.zeros((1, 128), F32).at[0, DT_COL:DT_COL + HEADS].set(-jnp.exp(a_log))
    dsk = jnp.repeat(d_skip, HEAD_DIM)[None, :]

    def full(shape):
        return pl.BlockSpec(shape, lambda b, n: (0,) * len(shape))

    def blk(width, col):
        return pl.BlockSpec((bpb, L, width), lambda b, n: (b, n, col))

    qkv3 = ret_qkv.reshape(batch, seq, -1)
    z3 = z_all.reshape(batch, seq, -1)
    ret_in = [decay, xi_tab, zeta_tab, cd_tab, gn_w.reshape(1, GROUP_W), qkv3, qkv3, qkv3, z3]
    ret_specs = [full((HEADS, L, L)), full((L, GROUP_W)), full((L, GROUP_W)), full((1, GROUP_W)),
                 full((1, GROUP_W)), blk(GROUP_W, 0), blk(GROUP_W, 1), blk(GROUP_W, 2), blk(GROUP_W, 2)]
    ssd_in = [conv_w, conv_b.reshape(1, CONV_CH), dtb, a_full, dsk, norm_w.reshape(1, GROUP_W),
              xbc.reshape(batch, seq, -1), misc.reshape(batch, seq, -1), z3]
    ssd_specs = [full((CONV_W, CONV_CH)), full((1, CONV_CH)), full((1, 128)), full((1, 128)),
                 full((1, GROUP_W)), full((1, GROUP_W)), blk(CONV_CH, 0), blk(128, 0), blk(GROUP_W, 3)]
    assert len(ret_in) == N_RET_IN and len(ssd_in) == N_SSD_IN
    y_ret, y_ssm = pl.pallas_call(
        _recurrent_kernel,
        grid=(batch // bpb, nch),
        in_specs=ret_specs + ssd_specs,
        out_specs=[blk(GROUP_W, 0), blk(GROUP_W, 0)],
        out_shape=[jax.ShapeDtypeStruct((batch, seq, GROUP_W), BF16)] * 2,
        scratch_shapes=[pltpu.VMEM((bpb * HEADS, HEAD_DIM, HEAD_DIM), F32),
                        pltpu.VMEM((bpb, 8 + L, CONV_CH), F32),
                        pltpu.VMEM((bpb * HEADS, SSM_STATE, HEAD_DIM), F32)],
        compiler_params=_params(("parallel", "arbitrary")),
        name="retention_ssd",
    )(*ret_in, *ssd_in)
    return y_ret.reshape(batch * seq, GROUP_W), y_ssm.reshape(batch * seq, GROUP_W)


def _pick_tile(n, pref):
    t = pref
    while n % t:
        t //= 2
    return t


def kernel(x, norm_w, w_in, w_out, nsa_pe_k, nsa_pe_v, nsa_w_ck1, nsa_w_ck2, nsa_w_cv1, nsa_w_cv2,
           diff_lam_q1, diff_lam_k1, diff_lam_q2, diff_lam_k2, diff_subln_w, ret_gn_w,
           ssm_conv_w, ssm_conv_b, ssm_dt_bias, ssm_A_log, ssm_D, ssm_norm_w, final_norm_w):
    batch, seq, _ = x.shape
    depth = w_in.shape[0]
    m = batch * seq
    tm = _pick_tile(m, 512)
    tq = _pick_tile(seq, 512)
    tk = _pick_tile(seq, 256)
    tk_diff = _pick_tile(seq, 512)
    bpb = _pick_tile(batch, REC_BATCH)
    w_in_b = w_in.astype(BF16)
    w_r = _relayout_w_in(w_in_b)
    w_t = _relayout_w_in_t(w_in_b)
    w_out_b = w_out.astype(BF16)
    x2d = x.reshape(m, D_MODEL)
    projected = _in_proj(x2d, norm_w[0], w_r[0], w_t[0], tm)
    for i in range(depth):
        nsa_q, nsa_k2, nsa_cmp, misc, z_all, diff_qk, ret_qkv, xbc, vt_all = projected
        kc, vct = _nsa_compress(nsa_cmp, nsa_pe_k[i], nsa_pe_v[i], nsa_w_ck1[i], nsa_w_ck2[i],
                                nsa_w_cv1[i], nsa_w_cv2[i], batch, seq)
        y_nsa = _nsa_attention(nsa_q, nsa_k2, vt_all, kc, vct, misc, z_all, batch, seq, tq, tk)
        lam_p = jnp.stack([diff_lam_q1[i], diff_lam_k1[i], diff_lam_q2[i], diff_lam_k2[i]])
        y_diff = _diff_attention(diff_qk, vt_all, z_all, lam_p, diff_subln_w[i], batch, seq, i, tq, tk_diff)
        y_ret, y_ssm = _recurrent_mixers(ret_qkv, xbc, misc, z_all, ret_gn_w[i], ssm_conv_w[i], ssm_conv_b[i],
                                         ssm_dt_bias[i], ssm_A_log[i], ssm_D[i], ssm_norm_w[i], batch, seq, bpb)
        ys = (y_nsa, y_diff, y_ret, y_ssm)
        if i + 1 < depth:
            x2d, *projected = _out_in_proj(ys, w_out_b[i], x2d, norm_w[i + 1], w_r[i + 1], w_t[i + 1], tm)
        else:
            x2d = _out_proj(ys, w_out_b[i], x2d, final_norm_w, tm)
    return x2d.reshape(batch, seq, D_MODEL)
```

```python
import functools
import math

import numpy as np
import jax
import jax.numpy as jnp
from jax import lax
from jax.experimental import pallas as pl
from jax.experimental.pallas import tpu as pltpu

F32 = jnp.float32
BF16 = jnp.bfloat16
NEG_INF = float("-inf")
LOG2E = 1.4426950408889634

D_MODEL = 1024
DEPTH = 4
GROUP_W = 256
HEADS = 4
HEAD_DIM = 64
EPS = 1e-6
CMP_LEN = 32
CMP_STRIDE = 16
CMP_HIDDEN = 256
SLC_BLOCK = 64
SLC_SHIFT = 6
SLC_TOPK = 16
WINDOW = 512
DIFF_QK_DIM = 32
RET_CHUNK = 128
SSM_STATE = 128
SSM_CHUNK = 128
CONV_W = 4
CONV_CH = 768
N_ALIBI_HEADS = 8
LANES = 128
BF16_ROWS = 16
ACC_ROWS = HEAD_DIM + BF16_ROWS
QK_LOOKAHEAD = 6
ALIBI_ROWS = 3
POS_RADIX = 256
MASK_BIG = 2.0 ** 100
BLK_LANE0 = 64
REC_BATCH = 4

IN_LAYOUT = (
    ("nsa_q", 256), ("nsa_k_cmp", 64), ("nsa_v_cmp", 64), ("nsa_k_slc", 64), ("nsa_v_slc", 64),
    ("nsa_k_win", 64), ("nsa_v_win", 64), ("nsa_gate", 12), ("nsa_z", 256),
    ("diff_q", 256), ("diff_k", 256), ("diff_v", 256), ("diff_z", 256),
    ("ret_q", 256), ("ret_k", 256), ("ret_v", 256), ("ret_z", 256),
    ("ssm_z", 256), ("ssm_xbc", 768), ("ssm_dt", 4),
)
IN_OFF = {}
_o = 0
for _n, _w in IN_LAYOUT:
    IN_OFF[_n] = (_o, _w)
    _o += _w
IN_W = _o

GATE_COL = 0
DT_COL = 12
IN_OUTPUTS = (
    ("nsa_q", BF16, ("nsa_q",), 256),
    ("nsa_k2", BF16, ("nsa_k_slc", "nsa_k_win"), 128),
    ("nsa_cmp", F32, ("nsa_k_cmp", "nsa_v_cmp"), 128),
    ("misc", F32, ("nsa_gate", "ssm_dt"), 128),
    ("z_all", F32, ("nsa_z", "diff_z", "ret_z", "ssm_z"), 1024),
    ("diff_qk", BF16, ("diff_q", "diff_k"), 512),
    ("ret_qkv", BF16, ("ret_q", "ret_k", "ret_v"), 768),
    ("xbc", F32, ("ssm_xbc",), 768),
)
IN_T_SRC = ("diff_v", "nsa_v_slc", "nsa_v_win")
IN_T_ROWS = 384
DIFF_VT_BLK = 0
NSA_VT_BLK = 2
IN_SEGS = []
_o = 0
for _n, _dt, _src, _w in IN_OUTPUTS:
    IN_SEGS.append((_o, _o + _w))
    _o += _w
IN_WP = _o

VMEM_LIMIT = 56 * 1024 * 1024


def _alibi_slopes():
    return [2.0 ** (-8.0 * (i + 1) / N_ALIBI_HEADS) for i in range(N_ALIBI_HEADS)]


NSA_SLOPES = _alibi_slopes()[0::2]
DIFF_SLOPES = _alibi_slopes()[1::2]


def _silu(x):
    return x * jax.nn.sigmoid(x)


def _dot(a, b):
    return jnp.dot(a, b, preferred_element_type=F32)


def _dot_nt(a, b):
    return lax.dot_general(a, b, (((1,), (1,)), ((), ())), preferred_element_type=F32)


def _dot_tn(a, b):
    return lax.dot_general(a, b, (((0,), (0,)), ((), ())), preferred_element_type=F32)


def _params(sem):
    return pltpu.CompilerParams(dimension_semantics=sem, vmem_limit_bytes=VMEM_LIMIT)


def _relayout_w_in(w_in):
    cols = []
    for _n, _dt, src, width in IN_OUTPUTS:
        used = 0
        for s in src:
            off, w = IN_OFF[s]
            cols.append(w_in[:, :, off:off + w])
            used += w
        if used < width:
            cols.append(jnp.zeros(w_in.shape[:2] + (width - used,), w_in.dtype))
    return jnp.concatenate(cols, axis=-1)


def _relayout_w_in_t(w_in):
    cols = [w_in[:, :, IN_OFF[s][0]:IN_OFF[s][0] + IN_OFF[s][1]] for s in IN_T_SRC]
    return jnp.swapaxes(jnp.concatenate(cols, axis=-1), 1, 2)


def _norm_project(x, nw_ref, w_ref, wt_ref, out_refs):
    ms = jnp.mean(x * x, axis=-1, keepdims=True)
    h = (x * lax.rsqrt(ms + EPS) * nw_ref[...]).astype(BF16)
    for ref, (a, b) in zip(out_refs[:-1], IN_SEGS):
        ref[...] = _dot(h, w_ref[:, a:b]).astype(ref.dtype)
    out_refs[-1][...] = _dot_nt(wt_ref[...], h).astype(BF16)


def _in_proj_kernel(x_ref, nw_ref, w_ref, wt_ref, *out_refs):
    _norm_project(x_ref[...], nw_ref, w_ref, wt_ref, out_refs)


def _in_proj_outputs(m, tm):
    out_shape = [jax.ShapeDtypeStruct((m, b - a), dt) for (_n, dt, _s, _w), (a, b) in zip(IN_OUTPUTS, IN_SEGS)]
    out_specs = [pl.BlockSpec((tm, b - a), lambda i: (i, 0)) for (a, b) in IN_SEGS]
    out_shape.append(jax.ShapeDtypeStruct((IN_T_ROWS, m), BF16))
    out_specs.append(pl.BlockSpec((IN_T_ROWS, tm), lambda i: (0, i)))
    return out_shape, out_specs


def _in_proj(x2d, norm_w, w_r, w_t, tm):
    m = x2d.shape[0]
    out_shape, out_specs = _in_proj_outputs(m, tm)
    return pl.pallas_call(
        _in_proj_kernel,
        grid=(m // tm,),
        in_specs=[pl.BlockSpec((tm, D_MODEL), lambda i: (i, 0)),
                  pl.BlockSpec((1, D_MODEL), lambda i: (0, 0)),
                  pl.BlockSpec((D_MODEL, IN_WP), lambda i: (0, 0)),
                  pl.BlockSpec((IN_T_ROWS, D_MODEL), lambda i: (0, 0))],
        out_specs=out_specs,
        out_shape=out_shape,
        compiler_params=_params(("parallel",)),
        name="in_proj",
    )(x2d, norm_w.reshape(1, D_MODEL), w_r, w_t)


def _out_proj_kernel(y0_ref, y1_ref, y2_ref, y3_ref, w_ref, x_ref, fw_ref, o_ref):
    acc = x_ref[...]
    for g, y_ref in enumerate((y0_ref, y1_ref, y2_ref, y3_ref)):
        acc = acc + _dot(y_ref[...], w_ref[g * GROUP_W:(g + 1) * GROUP_W, :])
    ms = jnp.mean(acc * acc, axis=-1, keepdims=True)
    o_ref[...] = acc * lax.rsqrt(ms + EPS) * fw_ref[...]


def _out_proj(ys, w_out_b, x2d, final_w, tm):
    m = x2d.shape[0]
    yspec = pl.BlockSpec((tm, GROUP_W), lambda i: (i, 0))
    return pl.pallas_call(
        _out_proj_kernel,
        grid=(m // tm,),
        in_specs=[yspec, yspec, yspec, yspec,
                  pl.BlockSpec((D_MODEL, D_MODEL), lambda i: (0, 0)),
                  pl.BlockSpec((tm, D_MODEL), lambda i: (i, 0)),
                  pl.BlockSpec((1, D_MODEL), lambda i: (0, 0))],
        out_specs=pl.BlockSpec((tm, D_MODEL), lambda i: (i, 0)),
        out_shape=jax.ShapeDtypeStruct((m, D_MODEL), F32),
        compiler_params=_params(("parallel",)),
        name="out_proj",
    )(*ys, w_out_b, x2d, final_w.reshape(1, D_MODEL))


def _out_in_proj_kernel(y0_ref, y1_ref, y2_ref, y3_ref, wo_ref, x_ref, nw_ref, w_ref, wt_ref, xo_ref, *out_refs):
    acc = x_ref[...]
    for g, y_ref in enumerate((y0_ref, y1_ref, y2_ref, y3_ref)):
        acc = acc + _dot(y_ref[...], wo_ref[g * GROUP_W:(g + 1) * GROUP_W, :])
    xo_ref[...] = acc
    _norm_project(acc, nw_ref, w_ref, wt_ref, out_refs)


def _out_in_proj(ys, w_out_b, x2d, norm_w, w_r, w_t, tm):
    m = x2d.shape[0]
    yspec = pl.BlockSpec((tm, GROUP_W), lambda i: (i, 0))
    xspec = pl.BlockSpec((tm, D_MODEL), lambda i: (i, 0))
    out_shape, out_specs = _in_proj_outputs(m, tm)
    return pl.pallas_call(
        _out_in_proj_kernel,
        grid=(m // tm,),
        in_specs=[yspec, yspec, yspec, yspec,
                  pl.BlockSpec((D_MODEL, D_MODEL), lambda i: (0, 0)),
                  xspec,
                  pl.BlockSpec((1, D_MODEL), lambda i: (0, 0)),
                  pl.BlockSpec((D_MODEL, IN_WP), lambda i: (0, 0)),
                  pl.BlockSpec((IN_T_ROWS, D_MODEL), lambda i: (0, 0))],
        out_specs=[xspec] + out_specs,
        out_shape=[jax.ShapeDtypeStruct((m, D_MODEL), F32)] + out_shape,
        compiler_params=_params(("parallel",)),
        name="out_in_proj",
    )(*ys, w_out_b, x2d, norm_w.reshape(1, D_MODEL), w_r, w_t)


def _stage_scores(s_ref, slot, cols, s, keep=None):
    if keep is not None:
        s = jnp.where(keep, s, NEG_INF)
    s_ref[slot, :, cols] = s
    return jnp.max(s, axis=0, keepdims=True)


def _flash_update_t(s, s_max, c1, shift, vt_ext, m_ref, acc_ref, idx, cols):
    m_old = m_ref[idx, :, cols]
    m_new = jnp.maximum(m_old, c1 * s_max + shift)
    alpha = jnp.exp2(m_old - m_new)
    p = jnp.exp2(c1 * s() - (m_new - shift))
    acc_ref[idx, :, cols] = alpha * acc_ref[idx, :, cols] + _dot(vt_ext, p.astype(BF16))
    m_ref[idx, :, cols] = m_new


def _bf16_pieces(x, n=3):
    out = []
    for _ in range(n):
        p = float(np.asarray(x, np.float32).astype(BF16).astype(np.float32))
        out.append(p)
        x = x - p
    return out


def _key_pos_features(tk):
    lane = lax.broadcasted_iota(jnp.int32, (tk, LANES), 1)
    row = lax.broadcasted_iota(jnp.int32, (tk, LANES), 0)
    out = jnp.zeros((tk, LANES), jnp.int32)
    for d in range(-(-tk // POS_RADIX)):
        digit = jnp.bitwise_and(jnp.right_shift(row, 8 * d), POS_RADIX - 1)
        out = jnp.where((lane >= ALIBI_ROWS * d) & (lane < ALIBI_ROWS * (d + 1)), digit, out)
    return out.astype(F32).astype(BF16)


def _alibi_rows(beta, tq, rows=LANES, tk=POS_RADIX):
    row = lax.broadcasted_iota(jnp.int32, (rows, tq), 0)
    out = jnp.zeros((rows, tq), F32)
    for d in range(-(-tk // POS_RADIX)):
        for r, piece in enumerate(_bf16_pieces(beta * POS_RADIX ** d, ALIBI_ROWS)):
            out = jnp.where(row == ALIBI_ROWS * d + r, piece, out)
    return out.astype(BF16)


def _normalized(acc):
    return acc[0:HEAD_DIM] / jnp.maximum(acc[HEAD_DIM:HEAD_DIM + 1], 1e-30)


def _rel_t(tk, tq):
    return lax.broadcasted_iota(jnp.int32, (tk, tq), 1) - lax.broadcasted_iota(jnp.int32, (tk, tq), 0)


def _diff_phases(lam_ref, sw_ref, q_ref, k_ref, vt_ref, z_ref, o_ref, m_ref, acc_ref, wq_ref, s_ref,
                 *, tq, tk, lam_init):
    qi = pl.program_id(1)
    kpq = tq // tk
    scale = DIFF_QK_DIM ** -0.5
    c1 = scale * LOG2E
    m_ref[...] = jnp.full(m_ref.shape, NEG_INF, F32)
    acc_ref[...] = jnp.zeros(acc_ref.shape, F32)
    rel = _rel_t(tk, tq)

    qt = q_ref[...].astype(F32).T.astype(BF16)
    row = lax.broadcasted_iota(jnp.int32, (LANES, tq), 0)
    for j in range(2 * HEADS):
        g, r0 = divmod(j * DIFF_QK_DIM, LANES)
        qg = qt[g * LANES:(g + 1) * LANES]
        wq_ref[j, 0:LANES, :] = jnp.where((row >= r0) & (row < r0 + DIFF_QK_DIM), qg, jnp.zeros_like(qg))
        wq_ref[j, LANES:2 * LANES, :] = _alibi_rows(DIFF_SLOPES[j // 2] / scale, tq, tk=tk)

    ones = jnp.ones((BF16_ROWS, tk), BF16)
    kpos = _key_pos_features(tk)

    n_maps = 2 * HEADS
    n_slots = s_ref.shape[0]

    def tiles(kis, mask_offs):
        loaded = []
        for ki in kis:
            start = pl.multiple_of(ki * tk, tk)
            loaded.append((k_ref[pl.ds(start, tk), :],
                           vt_ref[:, pl.ds(start, tk)],
                           (ki * tk).astype(F32)))
        items = [(t, j) for t in range(len(kis)) for j in range(n_maps)]
        cols = slice(0, tq)

        def scores(i):
            t, j = items[i]
            g = (j * DIFF_QK_DIM) // LANES
            lhs = jnp.concatenate([loaded[t][0][:, g * LANES:(g + 1) * LANES], kpos], axis=1)
            s_ref[i % n_slots] = _dot(lhs, wq_ref[j])

        def staged(i):
            t = items[i][0]
            if mask_offs[t] is None:
                return lambda: s_ref[i % n_slots]
            return lambda: jnp.where(rel >= mask_offs[t], s_ref[i % n_slots], NEG_INF)

        for i in range(QK_LOOKAHEAD):
            scores(i)
        for i, (t, j) in enumerate(items):
            h = j // 2
            if i + QK_LOOKAHEAD < len(items):
                scores(i + QK_LOOKAHEAD)
            _k, vt, key0 = loaded[t]
            vt_ext = jnp.concatenate([vt[h * HEAD_DIM:(h + 1) * HEAD_DIM], ones], axis=0)
            s = staged(i)
            _flash_update_t(s, jnp.max(s(), axis=0, keepdims=True), c1, (DIFF_SLOPES[h] * LOG2E) * key0,
                            vt_ext, m_ref, acc_ref, j, cols)

    def loop_body(i):
        tiles([i * kpq + d for d in range(kpq)], [None] * kpq)

    def finish():
        tiles([qi * kpq + d for d in range(kpq)], [d * tk for d in range(kpq)])
        lp = lam_ref[...]
        lam = (jnp.exp(jnp.sum(lp[0:1] * lp[1:2], axis=-1, keepdims=True))
               - jnp.exp(jnp.sum(lp[2:3] * lp[3:4], axis=-1, keepdims=True)) + lam_init)
        sw = sw_ref[...]
        o_t = jnp.concatenate([_normalized(acc_ref[2 * h]) - lam * _normalized(acc_ref[2 * h + 1])
                               for h in range(HEADS)], axis=0)
        o = o_t.T
        outs = []
        for h in range(HEADS):
            oh = o[:, h * HEAD_DIM:(h + 1) * HEAD_DIM]
            ms = jnp.mean(oh * oh, axis=-1, keepdims=True)
            outs.append(oh * lax.rsqrt(ms + EPS) * sw * (1.0 - lam_init))
        y = jnp.concatenate(outs, axis=-1) * _silu(z_ref[...])
        o_ref[...] = y.astype(o_ref.dtype)

    return loop_body, finish


def _diff_operands(qk, vt_all, z_all, lam_p, subln_w, seq, nq, tq, tk):
    inputs = [lam_p, subln_w.reshape(1, HEAD_DIM), qk, qk, vt_all, z_all]
    in_specs = [pl.BlockSpec((4, DIFF_QK_DIM), lambda b, i: (0, 0)),
                pl.BlockSpec((1, HEAD_DIM), lambda b, i: (0, 0)),
                pl.BlockSpec((tq, GROUP_W), lambda b, i: (b * nq + i, 0)),
                pl.BlockSpec((seq, GROUP_W), lambda b, i: (b, 1)),
                pl.BlockSpec((GROUP_W, seq), lambda b, i: (DIFF_VT_BLK, b)),
                pl.BlockSpec((tq, GROUP_W), lambda b, i: (b * nq + i, 1))]
    scratch = [pltpu.VMEM((2 * HEADS, 1, tq), F32),
               pltpu.VMEM((2 * HEADS, ACC_ROWS, tq), F32),
               pltpu.VMEM((2 * HEADS, 2 * LANES, tq), BF16),
               pltpu.VMEM((QK_LOOKAHEAD + 1, tk, tq), F32)]
    return inputs, in_specs, scratch


def _nsa_compress_kernel(cmp_ref, pe_ref, w1_ref, wk2_ref, wv2t_ref, kc_ref, vct_ref):
    ng = kc_ref.shape[1]
    top = jnp.zeros((ng, 2 * CMP_HIDDEN), F32)
    bot = jnp.zeros((ng, 2 * CMP_HIDDEN), F32)
    for p in range(CMP_STRIDE):
        x = cmp_ref[pl.ds(p, ng, stride=CMP_STRIDE), :]
        top = top + _dot((x + pe_ref[p:p + 1, :]).astype(BF16), w1_ref[p])
        bot = bot + _dot((x + pe_ref[CMP_STRIDE + p:CMP_STRIDE + p + 1, :]).astype(BF16), w1_ref[CMP_STRIDE + p])
    hid = _silu(top + pltpu.roll(bot, ng - 1, 0)).astype(BF16)
    hk, hv = hid[:, 0:CMP_HIDDEN], hid[:, CMP_HIDDEN:]
    kc = _dot(hk, wk2_ref[...])
    lane = lax.broadcasted_iota(jnp.int32, kc.shape, 1)
    blk = lax.broadcasted_iota(jnp.int32, kc.shape, 0).astype(F32)
    kc = jnp.where((lane >= HEAD_DIM) & (lane < HEAD_DIM + ALIBI_ROWS), blk, kc)
    kc_ref[0] = kc.astype(kc_ref.dtype)
    vct_ref[0] = _dot_nt(wv2t_ref[...], hv).astype(vct_ref.dtype)


def _nsa_compress(cmp2d, pe_k, pe_v, w_ck1, w_ck2, w_cv1, w_cv2, batch, seq):
    ng = seq // CMP_STRIDE
    wk = w_ck1.reshape(CMP_LEN, HEAD_DIM, CMP_HIDDEN)
    wv = w_cv1.reshape(CMP_LEN, HEAD_DIM, CMP_HIDDEN)
    w1 = jnp.concatenate([jnp.pad(wk, ((0, 0), (0, 0), (0, CMP_HIDDEN))),
                          jnp.pad(wv, ((0, 0), (0, 0), (CMP_HIDDEN, 0)))], axis=1).astype(BF16)
    pe = jnp.concatenate([pe_k, pe_v], axis=1)

    def full(shape):
        return pl.BlockSpec(shape, lambda b: (0,) * len(shape))

    return pl.pallas_call(
        _nsa_compress_kernel,
        grid=(batch,),
        in_specs=[pl.BlockSpec((seq, LANES), lambda b: (b, 0)), full((CMP_LEN, LANES)),
                  full((CMP_LEN, LANES, 2 * CMP_HIDDEN)), full((CMP_HIDDEN, LANES)),
                  full((HEAD_DIM, CMP_HIDDEN))],
        out_specs=[pl.BlockSpec((1, ng, LANES), lambda b: (b, 0, 0)),
                   pl.BlockSpec((1, HEAD_DIM, ng), lambda b: (b, 0, 0))],
        out_shape=[jax.ShapeDtypeStruct((batch, ng, LANES), BF16),
                   jax.ShapeDtypeStruct((batch, HEAD_DIM, ng), BF16)],
        compiler_params=_params(("parallel",)),
        name="nsa_compress",
    )(cmp2d, pe, w1, jnp.pad(w_ck2, ((0, 0), (0, LANES - HEAD_DIM))).astype(BF16), w_cv2.T.astype(BF16))


def _nsa_phases(ovl_ref, q_ref, k_ref, vt_ref, kc_ref, vct_ref, misc_ref, z_ref, o_ref,
                m_ref, acc_ref, wq_ref, s_ref, sc_ref, *, tq, tk, seq):
    qi = pl.program_id(1)
    kpq = tq // tk
    scale = HEAD_DIM ** -0.5
    c1 = scale * LOG2E
    ng = seq // CMP_STRIDE
    ns = seq // SLC_BLOCK
    top = min(SLC_TOPK, ns)
    m_ref[...] = jnp.full(m_ref.shape, NEG_INF, F32)
    acc_ref[...] = jnp.zeros(acc_ref.shape, F32)
    rel = _rel_t(tk, tq)

    qt = q_ref[...].astype(F32).T.astype(BF16)
    zeros_q = jnp.zeros((HEAD_DIM, tq), BF16)
    for h in range(HEADS):
        qh = qt[h * HEAD_DIM:(h + 1) * HEAD_DIM]
        alibi = _alibi_rows(NSA_SLOPES[h] / scale, tq, tk=tk)
        wq_ref[h, 0:LANES, :] = jnp.concatenate([qh, zeros_q], axis=0)
        wq_ref[h, LANES:2 * LANES, :] = alibi
        wq_ref[HEADS + h, 0:LANES, :] = jnp.concatenate([zeros_q, qh], axis=0)
        wq_ref[HEADS + h, LANES:2 * LANES, :] = alibi

    t_lane = qi * tq + lax.broadcasted_iota(jnp.int32, (1, tq), 1)
    n_sub = lax.broadcasted_iota(jnp.int32, (ng, 1), 0)
    c_valid = (n_sub * CMP_STRIDE + (CMP_LEN - 1)) <= t_lane
    kc = kc_ref[0]
    vct_ext = jnp.concatenate([vct_ref[0], jnp.ones((BF16_ROWS, ng), BF16)], axis=0)
    ovl = ovl_ref[...]
    cmp_max = []
    for h in range(HEADS):
        rhs = jnp.concatenate([qt[h * HEAD_DIM:(h + 1) * HEAD_DIM],
                               _alibi_rows(NSA_SLOPES[h] * CMP_STRIDE / scale, tq, HEAD_DIM)], axis=0)
        cmp_max.append(_stage_scores(sc_ref, h, slice(0, tq), _dot(kc, rhs), c_valid))
    imp = jnp.zeros((ns, tq), F32)
    o_cmp = []
    for h in range(HEADS):
        mx = jnp.where(cmp_max[h] > NEG_INF, cmp_max[h], 0.0)
        pb = jnp.exp2(c1 * sc_ref[h] - c1 * mx).astype(BF16)
        o_ext = _dot(vct_ext, pb)
        r = 1.0 / jnp.maximum(o_ext[HEAD_DIM:HEAD_DIM + 1], 1e-30)
        o_cmp.append(o_ext[0:HEAD_DIM] * r)
        imp = imp + _dot(ovl, pb) * r

    def select_blocks():
        j_sub = lax.broadcasted_iota(jnp.int32, (ns, 1), 0)
        j_sub_f = j_sub.astype(F32)
        cur = jnp.right_shift(t_lane, SLC_SHIFT)
        forced = (j_sub == 0) | (j_sub == cur) | (j_sub == cur - 1)
        valid = (j_sub * SLC_BLOCK) <= t_lane
        score = jnp.where(forced, 1e30, jnp.where(valid, imp, -1.0))
        sel = jnp.zeros((ns, tq), F32)
        for _ in range(top):
            mx = jnp.max(score, axis=0, keepdims=True)
            idx = jnp.min(jnp.where(score == mx, j_sub_f, float(ns)), axis=0, keepdims=True)
            pick = j_sub_f == idx
            sel = jnp.where(pick, 1.0, sel)
            score = jnp.where(pick, -2.0, score)
        unsel = jnp.where(sel > 0.5, 0.0, -MASK_BIG).astype(BF16)
        for h in range(HEADS):
            wq_ref[h, LANES + BLK_LANE0:LANES + BLK_LANE0 + ns, :] = unsel

    aux_lane = lax.broadcasted_iota(jnp.int32, (tk, LANES), 1)
    aux_blk = jnp.right_shift(lax.broadcasted_iota(jnp.int32, (tk, LANES), 0), SLC_SHIFT) + BLK_LANE0
    kpos = _key_pos_features(tk)
    ones = jnp.ones((BF16_ROWS, tk), BF16)

    n_slots = s_ref.shape[0]

    def tiles(window, kis, cols, mask_offs=None):
        base = HEADS if window else 0
        loaded = []
        for t, ki in enumerate(kis):
            start = pl.multiple_of(ki * tk, tk)
            k = k_ref[pl.ds(start, tk), :]
            if window:
                vt = vt_ref[HEAD_DIM:2 * HEAD_DIM, pl.ds(start, tk)]
                aux = kpos
                dist = rel[:, cols[t]] + (qi * tq - ki * tk)
                keep = (dist >= 0) & (dist < WINDOW)
            else:
                vt = vt_ref[0:HEAD_DIM, pl.ds(start, tk)]
                onehot = aux_lane == aux_blk + ki * (tk // SLC_BLOCK)
                aux = jnp.where(onehot, jnp.ones_like(kpos), kpos)
                keep = None
            loaded.append((jnp.concatenate([k, aux], axis=1),
                           jnp.concatenate([vt, ones], axis=0), (ki * tk).astype(F32), keep))
        items = [(t, h) for t in range(len(kis)) for h in range(HEADS)]
        s_max = {}

        def scores(i):
            t, h = items[i]
            if window:
                keep = loaded[t][3]
            else:
                keep = None if mask_offs[t] is None else rel[:, cols[t]] >= mask_offs[t]
            s_max[i] = _stage_scores(s_ref, i % n_slots, cols[t],
                                     _dot(loaded[t][0], wq_ref[base + h, :, cols[t]]), keep)

        for i in range(QK_LOOKAHEAD):
            scores(i)
        for i, (t, h) in enumerate(items):
            if i + QK_LOOKAHEAD < len(items):
                scores(i + QK_LOOKAHEAD)
            _lhs, vt_ext, key0, _keep = loaded[t]
            _flash_update_t(lambda i=i, t=t: s_ref[i % n_slots, :, cols[t]], s_max.pop(i), c1,
                            (NSA_SLOPES[h] * LOG2E) * key0, vt_ext, m_ref, acc_ref, base + h, cols[t])

    all_cols = slice(0, tq)
    diag_cols = [slice(d * tk, tq) for d in range(kpq)]

    tiles(True, [qi * kpq + d for d in range(kpq)], diag_cols)
    select_blocks()

    def loop_body(i):
        tiles(False, [i * kpq + d for d in range(kpq)], [all_cols] * kpq, [None] * kpq)

    def back_window():
        n_back = (WINDOW + tk - 1) // tk
        for g in range((n_back + kpq - 1) // kpq):
            backs = list(range(g * kpq + 1, min((g + 1) * kpq, n_back) + 1))
            back_cols = [slice(0, min(tq, -(-(WINDOW - (back - 1) * tk - 1) // LANES) * LANES)) for back in backs]

            @pl.when(qi > g)
            def _():
                tiles(True, [qi * kpq - back for back in backs], back_cols)

    def finish():
        tiles(False, [qi * kpq + d for d in range(kpq)], diag_cols, [d * tk for d in range(kpq)])
        g_t = jax.nn.sigmoid(misc_ref[...]).T
        outs = []
        for h in range(HEADS):
            r0 = GATE_COL + 3 * h
            outs.append(g_t[r0:r0 + 1] * o_cmp[h] + g_t[r0 + 1:r0 + 2] * _normalized(acc_ref[h])
                        + g_t[r0 + 2:r0 + 3] * _normalized(acc_ref[HEADS + h]))
        y = jnp.concatenate(outs, axis=0).T * _silu(z_ref[...])
        o_ref[...] = y.astype(o_ref.dtype)

    return loop_body, back_window, finish


def _overlap_t(seq):
    nc = (seq - CMP_LEN) // CMP_STRIDE + 1
    ng = seq // CMP_STRIDE
    ns = seq // SLC_BLOCK
    c_start = np.arange(ng) * CMP_STRIDE
    c_end = c_start + CMP_LEN - 1
    s_start = np.arange(ns) * SLC_BLOCK
    s_end = s_start + SLC_BLOCK - 1
    ov = (c_start[None, :] <= s_end[:, None]) & (c_end[None, :] >= s_start[:, None]) & (np.arange(ng)[None, :] < nc)
    return jnp.asarray(ov.astype(np.float32), dtype=BF16)


def _nsa_operands(q, k2, vt_all, kc, vct, misc, z_all, seq, nq, tq, tk):
    ng = seq // CMP_STRIDE
    ns = seq // SLC_BLOCK
    assert tq - tk < WINDOW and tk % SLC_BLOCK == 0 and ns <= LANES - BLK_LANE0
    inputs = [_overlap_t(seq), q, k2, vt_all, kc, vct, misc, z_all]
    in_specs = [pl.BlockSpec((ns, ng), lambda b, i: (0, 0)),
                pl.BlockSpec((tq, GROUP_W), lambda b, i: (b * nq + i, 0)),
                pl.BlockSpec((seq, LANES), lambda b, i: (b, 0)),
                pl.BlockSpec((2 * HEAD_DIM, seq), lambda b, i: (NSA_VT_BLK, b)),
                pl.BlockSpec((1, ng, LANES), lambda b, i: (b, 0, 0)),
                pl.BlockSpec((1, HEAD_DIM, ng), lambda b, i: (b, 0, 0)),
                pl.BlockSpec((tq, LANES), lambda b, i: (b * nq + i, 0)),
                pl.BlockSpec((tq, GROUP_W), lambda b, i: (b * nq + i, 0))]
    scratch = [pltpu.VMEM((2 * HEADS, 1, tq), F32),
               pltpu.VMEM((2 * HEADS, ACC_ROWS, tq), F32),
               pltpu.VMEM((2 * HEADS, 2 * LANES, tq), BF16),
               pltpu.VMEM((2 * HEADS, tk, tq), F32),
               pltpu.VMEM((HEADS, ng, tq), F32)]
    return inputs, in_specs, scratch


def _attention_kernel(*refs, n_nsa_in, n_diff_in, n_nsa_scratch, tq, tk_nsa, tk_diff, seq, lam_init):
    nsa_in, diff_in = refs[:n_nsa_in], refs[n_nsa_in:n_nsa_in + n_diff_in]
    nsa_out, diff_out = refs[n_nsa_in + n_diff_in:n_nsa_in + n_diff_in + 2]
    scratch = refs[n_nsa_in + n_diff_in + 2:]
    nsa_loop, nsa_back_window, nsa_finish = _nsa_phases(*nsa_in, nsa_out, *scratch[:n_nsa_scratch],
                                                        tq=tq, tk=tk_nsa, seq=seq)
    diff_loop, diff_finish = _diff_phases(*diff_in, diff_out, *scratch[n_nsa_scratch:],
                                          tq=tq, tk=tk_diff, lam_init=lam_init)

    def body(i, carry):
        nsa_loop(i)
        diff_loop(i)
        return carry

    lax.fori_loop(0, pl.program_id(1), body, 0)
    nsa_back_window()
    nsa_finish()
    diff_finish()


def _attention(nsa_args, diff_args, batch, seq, layer_idx, tq, tk_nsa, tk_diff):
    nq = seq // tq
    nsa_in, nsa_specs, nsa_scratch = _nsa_operands(*nsa_args, seq, nq, tq, tk_nsa)
    diff_in, diff_specs, diff_scratch = _diff_operands(*diff_args, seq, nq, tq, tk_diff)
    out_spec = pl.BlockSpec((tq, GROUP_W), lambda b, i: (b * nq + i, 0))
    return pl.pallas_call(
        functools.partial(_attention_kernel, n_nsa_in=len(nsa_in), n_diff_in=len(diff_in),
                          n_nsa_scratch=len(nsa_scratch), tq=tq, tk_nsa=tk_nsa, tk_diff=tk_diff, seq=seq,
                          lam_init=0.8 - 0.6 * math.exp(-0.3 * layer_idx)),
        grid=(batch, nq),
        in_specs=nsa_specs + diff_specs,
        out_specs=[out_spec, out_spec],
        out_shape=[jax.ShapeDtypeStruct((batch * seq, GROUP_W), BF16)] * 2,
        scratch_shapes=nsa_scratch + diff_scratch,
        compiler_params=_params(("parallel", "parallel")),
        name="attention",
    )(*nsa_in, *diff_in)


def _ret_tables():
    c = RET_CHUNK
    h = np.arange(HEADS, dtype=np.float32)
    log_g = jnp.log(1.0 - 2.0 ** (-5.0 - jnp.asarray(h)))
    pos = jnp.arange(c, dtype=F32)
    rel = pos[:, None] - pos[None, :]
    decay = jnp.where(rel >= 0, jnp.exp(log_g[:, None, None] * jnp.maximum(rel, 0.0)), 0.0)
    xi = jnp.exp(log_g[:, None] * (pos + 1.0))
    zeta = jnp.exp(log_g[:, None] * (c - 1.0 - pos))
    chunk_decay = jnp.exp(log_g * c)
    xi_tab = jnp.repeat(xi.T, HEAD_DIM, axis=1)
    zeta_tab = jnp.repeat(zeta.T, HEAD_DIM, axis=1)
    cd_tab = jnp.repeat(chunk_decay, HEAD_DIM)[None, :]
    return decay, xi_tab, zeta_tab, cd_tab


def _ret_body(decay_ref, xi_ref, zeta_ref, cd_ref, gn_ref, q_ref, k_ref, v_ref, z_ref, o_ref, st_ref):
    xi = xi_ref[...]
    cd = cd_ref[...]
    for bb in range(q_ref.shape[0]):
        q = (q_ref[bb].astype(F32) * (HEAD_DIM ** -0.5)).astype(BF16)
        k = k_ref[bb]
        v = v_ref[bb]
        kz_t = (k.astype(F32) * zeta_ref[...]).T.astype(BF16)
        outs = []
        for h in range(HEADS):
            sl = slice(h * HEAD_DIM, (h + 1) * HEAD_DIM)
            qh, kh, vh = q[:, sl], k[:, sl], v[:, sl]
            prev = st_ref[bb * HEADS + h]
            inner = (_dot_nt(qh, kh) * decay_ref[h]).astype(BF16)
            o = _dot(inner, vh) + _dot(qh, prev.astype(BF16)) * xi[:, sl]
            st_ref[bb * HEADS + h] = prev * cd[:, sl] + _dot(kz_t[sl, :], vh)
            mu = jnp.mean(o, axis=-1, keepdims=True)
            d = o - mu
            var = jnp.mean(d * d, axis=-1, keepdims=True)
            outs.append(d * lax.rsqrt(var + EPS))
        y = jnp.concatenate(outs, axis=-1) * gn_ref[...] * _silu(z_ref[bb])
        o_ref[bb] = y.astype(o_ref.dtype)


def _ssd_body(cw_ref, cb_ref, dtb_ref, a_ref, dsk_ref, nw_ref, xbc_ref, misc_ref, z_ref, o_ref,
              ext_ref, st_ref):
    L = SSM_CHUNK
    hi = lax.Precision.HIGHEST
    row = lax.broadcasted_iota(jnp.int32, (L, L), 0)
    col = lax.broadcasted_iota(jnp.int32, (L, L), 1)
    causal = row >= col
    tril = jnp.where(causal, 1.0, 0.0).astype(F32)
    dsk = dsk_ref[...]

    for bb in range(xbc_ref.shape[0]):
        raw = xbc_ref[bb]
        ext_ref[bb, 8:8 + L, :] = raw
        conv = cb_ref[...] + raw * cw_ref[CONV_W - 1:CONV_W, :]
        for w in range(CONV_W - 1):
            shift = CONV_W - 1 - w
            conv = conv + ext_ref[bb, 8 - shift:8 - shift + L, :] * cw_ref[w:w + 1, :]
        ext_ref[bb, 0:8, :] = raw[L - 8:L, :]
        xc = _silu(conv)
        x = xc[:, 0:GROUP_W]
        bm = xc[:, GROUP_W:GROUP_W + 2 * SSM_STATE].astype(BF16)
        cm = xc[:, GROUP_W + 2 * SSM_STATE:].astype(BF16)

        dt_full = jax.nn.softplus(misc_ref[bb] + dtb_ref[...])
        da = dt_full * a_ref[...]
        cs_col = jnp.dot(tril, da, precision=hi, preferred_element_type=F32)
        cs_row = lax.dot_general(da, tril, (((0,), (1,)), ((), ())), precision=hi,
                                 preferred_element_type=F32)

        outs = []
        for h in range(HEADS):
            g = h // 2
            c0 = DT_COL + h
            sl = slice(h * HEAD_DIM, (h + 1) * HEAD_DIM)
            gs = slice(g * SSM_STATE, (g + 1) * SSM_STATE)
            cs_c = cs_col[:, c0:c0 + 1]
            cs_r = cs_row[c0:c0 + 1, :]
            cs_last = cs_col[L - 1:L, c0:c0 + 1]
            xh = x[:, sl]
            xdt = xh * dt_full[:, c0:c0 + 1]
            seg = jnp.exp(jnp.where(causal, cs_c - cs_r, NEG_INF))
            cb = _dot_nt(cm[:, gs], bm[:, gs])
            y = _dot((cb * seg).astype(BF16), xdt.astype(BF16))
            prev = st_ref[bb * HEADS + h]
            y = y + _dot(cm[:, gs], prev.astype(BF16)) * jnp.exp(cs_c)
            y = y + dsk[:, sl] * xh
            dec = jnp.exp(cs_last - cs_c)
            st_ref[bb * HEADS + h] = prev * jnp.exp(cs_last) + _dot_tn(bm[:, gs], (xdt * dec).astype(BF16))
            outs.append(y)
        y = jnp.concatenate(outs, axis=-1) * _silu(z_ref[bb])
        ms = jnp.mean(y * y, axis=-1, keepdims=True)
        o_ref[bb] = (y * lax.rsqrt(ms + EPS) * nw_ref[...]).astype(o_ref.dtype)


N_RET_IN = 9
N_SSD_IN = 9


def _recurrent_kernel(*refs):
    ret_in, ssd_in = refs[:N_RET_IN], refs[N_RET_IN:N_RET_IN + N_SSD_IN]
    ret_out, ssd_out, ret_st, ssd_ext, ssd_st = refs[N_RET_IN + N_SSD_IN:]

    @pl.when(pl.program_id(1) == 0)
    def _():
        ret_st[...] = jnp.zeros(ret_st.shape, F32)
        ssd_st[...] = jnp.zeros(ssd_st.shape, F32)
        ssd_ext[:, 0:8, :] = jnp.zeros((ssd_ext.shape[0], 8, CONV_CH), F32)

    _ret_body(*ret_in, ret_out, ret_st)
    _ssd_body(*ssd_in, ssd_out, ssd_ext, ssd_st)


def _recurrent_mixers(ret_qkv, xbc, misc, z_all, gn_w, conv_w, conv_b, dt_bias, a_log, d_skip, norm_w,
                      batch, seq, bpb):
    L = SSM_CHUNK
    assert RET_CHUNK == L
    nch = seq // L
    decay, xi_tab, zeta_tab, cd_tab = _ret_tables()
    dtb = jnp.zeros((1, 128), F32).at[0, DT_COL:DT_COL + HEADS].set(dt_bias)
    a_full = jnp.zeros((1, 128), F32).at[0, DT_COL:DT_COL + HEADS].set(-jnp.exp(a_log))
    dsk = jnp.repeat(d_skip, HEAD_DIM)[None, :]

    def full(shape):
        return pl.BlockSpec(shape, lambda b, n: (0,) * len(shape))

    def blk(width, col):
        return pl.BlockSpec((bpb, L, width), lambda b, n: (b, n, col))

    qkv3 = ret_qkv.reshape(batch, seq, -1)
    z3 = z_all.reshape(batch, seq, -1)
    ret_in = [decay, xi_tab, zeta_tab, cd_tab, gn_w.reshape(1, GROUP_W), qkv3, qkv3, qkv3, z3]
    ret_specs = [full((HEADS, L, L)), full((L, GROUP_W)), full((L, GROUP_W)), full((1, GROUP_W)),
                 full((1, GROUP_W)), blk(GROUP_W, 0), blk(GROUP_W, 1), blk(GROUP_W, 2), blk(GROUP_W, 2)]
    ssd_in = [conv_w, conv_b.reshape(1, CONV_CH), dtb, a_full, dsk, norm_w.reshape(1, GROUP_W),
              xbc.reshape(batch, seq, -1), misc.reshape(batch, seq, -1), z3]
    ssd_specs = [full((CONV_W, CONV_CH)), full((1, CONV_CH)), full((1, 128)), full((1, 128)),
                 full((1, GROUP_W)), full((1, GROUP_W)), blk(CONV_CH, 0), blk(128, 0), blk(GROUP_W, 3)]
    assert len(ret_in) == N_RET_IN and len(ssd_in) == N_SSD_IN
    y_ret, y_ssm = pl.pallas_call(
        _recurrent_kernel,
        grid=(batch // bpb, nch),
        in_specs=ret_specs + ssd_specs,
        out_specs=[blk(GROUP_W, 0), blk(GROUP_W, 0)],
        out_shape=[jax.ShapeDtypeStruct((batch, seq, GROUP_W), BF16)] * 2,
        scratch_shapes=[pltpu.VMEM((bpb * HEADS, HEAD_DIM, HEAD_DIM), F32),
                        pltpu.VMEM((bpb, 8 + L, CONV_CH), F32),
                        pltpu.VMEM((bpb * HEADS, SSM_STATE, HEAD_DIM), F32)],
        compiler_params=_params(("parallel", "arbitrary")),
        name="retention_ssd",
    )(*ret_in, *ssd_in)
    return y_ret.reshape(batch * seq, GROUP_W), y_ssm.reshape(batch * seq, GROUP_W)


def _pick_tile(n, pref):
    t = pref
    while n % t:
        t //= 2
    return t


def kernel(x, norm_w, w_in, w_out, nsa_pe_k, nsa_pe_v, nsa_w_ck1, nsa_w_ck2, nsa_w_cv1, nsa_w_cv2,
           diff_lam_q1, diff_lam_k1, diff_lam_q2, diff_lam_k2, diff_subln_w, ret_gn_w,
           ssm_conv_w, ssm_conv_b, ssm_dt_bias, ssm_A_log, ssm_D, ssm_norm_w, final_norm_w):
    batch, seq, _ = x.shape
    depth = w_in.shape[0]
    m = batch * seq
    tm = _pick_tile(m, 512)
    tq = _pick_tile(seq, 512)
    tk = _pick_tile(seq, 256)
    tk_diff = _pick_tile(seq, 512)
    bpb = _pick_tile(batch, REC_BATCH)
    w_in_b = w_in.astype(BF16)
    w_r = _relayout_w_in(w_in_b)
    w_t = _relayout_w_in_t(w_in_b)
    w_out_b = w_out.astype(BF16)
    x2d = x.reshape(m, D_MODEL)
    projected = _in_proj(x2d, norm_w[0], w_r[0], w_t[0], tm)
    for i in range(depth):
        nsa_q, nsa_k2, nsa_cmp, misc, z_all, diff_qk, ret_qkv, xbc, vt_all = projected
        kc, vct = _nsa_compress(nsa_cmp, nsa_pe_k[i], nsa_pe_v[i], nsa_w_ck1[i], nsa_w_ck2[i],
                                nsa_w_cv1[i], nsa_w_cv2[i], batch, seq)
        lam_p = jnp.stack([diff_lam_q1[i], diff_lam_k1[i], diff_lam_q2[i], diff_lam_k2[i]])
        y_nsa, y_diff = _attention((nsa_q, nsa_k2, vt_all, kc, vct, misc, z_all),
                                   (diff_qk, vt_all, z_all, lam_p, diff_subln_w[i]),
                                   batch, seq, i, tq, tk, tk_diff)
        y_ret, y_ssm = _recurrent_mixers(ret_qkv, xbc, misc, z_all, ret_gn_w[i], ssm_conv_w[i], ssm_conv_b[i],
                                         ssm_dt_bias[i], ssm_A_log[i], ssm_D[i], ssm_norm_w[i], batch, seq, bpb)
        ys = (y_nsa, y_diff, y_ret, y_ssm)
        if i + 1 < depth:
            x2d, *projected = _out_in_proj(ys, w_out_b[i], x2d, norm_w[i + 1], w_r[i + 1], w_t[i + 1], tm)
        else:
            x2d = _out_proj(ys, w_out_b[i], x2d, final_norm_w, tm)
    return x2d.reshape(batch, seq, D_MODEL)
```

```python
import functools
import math

import numpy as np
import jax
import jax.numpy as jnp
from jax import lax
from jax.experimental import pallas as pl
from jax.experimental.pallas import tpu as pltpu

F32 = jnp.float32
BF16 = jnp.bfloat16
NEG_INF = float("-inf")
LOG2E = 1.4426950408889634

D_MODEL = 1024
DEPTH = 4
GROUP_W = 256
HEADS = 4
HEAD_DIM = 64
EPS = 1e-6
CMP_LEN = 32
CMP_STRIDE = 16
CMP_HIDDEN = 256
SLC_BLOCK = 64
SLC_SHIFT = 6
SLC_TOPK = 16
WINDOW = 512
DIFF_QK_DIM = 32
RET_CHUNK = 128
SSM_STATE = 128
SSM_CHUNK = 128
CONV_W = 4
CONV_CH = 768
N_ALIBI_HEADS = 8
LANES = 128
BF16_ROWS = 16
ACC_ROWS = HEAD_DIM + BF16_ROWS
QK_LOOKAHEAD = 6
ALIBI_ROWS = 3
POS_RADIX = 256
MASK_BIG = 2.0 ** 100
BLK_LANE0 = 64

IN_LAYOUT = (
    ("nsa_q", 256), ("nsa_k_cmp", 64), ("nsa_v_cmp", 64), ("nsa_k_slc", 64), ("nsa_v_slc", 64),
    ("nsa_k_win", 64), ("nsa_v_win", 64), ("nsa_gate", 12), ("nsa_z", 256),
    ("diff_q", 256), ("diff_k", 256), ("diff_v", 256), ("diff_z", 256),
    ("ret_q", 256), ("ret_k", 256), ("ret_v", 256), ("ret_z", 256),
    ("ssm_z", 256), ("ssm_xbc", 768), ("ssm_dt", 4),
)
IN_OFF = {}
_o = 0
for _n, _w in IN_LAYOUT:
    IN_OFF[_n] = (_o, _w)
    _o += _w
IN_W = _o

GATE_COL = 0
DT_COL = 12
IN_OUTPUTS = (
    ("nsa_q", BF16, ("nsa_q",), 256),
    ("nsa_k2", BF16, ("nsa_k_slc", "nsa_k_win"), 128),
    ("nsa_cmp", F32, ("nsa_k_cmp", "nsa_v_cmp"), 128),
    ("misc", F32, ("nsa_gate", "ssm_dt"), 128),
    ("z_all", F32, ("nsa_z", "diff_z", "ret_z", "ssm_z"), 1024),
    ("diff_qk", BF16, ("diff_q", "diff_k"), 512),
    ("ret_qkv", BF16, ("ret_q", "ret_k", "ret_v"), 768),
    ("xbc", F32, ("ssm_xbc",), 768),
)
IN_T_SRC = ("diff_v", "nsa_v_slc", "nsa_v_win")
IN_T_ROWS = 384
DIFF_VT_BLK = 0
NSA_VT_BLK = 2
IN_SEGS = []
_o = 0
for _n, _dt, _src, _w in IN_OUTPUTS:
    IN_SEGS.append((_o, _o + _w))
    _o += _w
IN_WP = _o

VMEM_LIMIT = 56 * 1024 * 1024


def _alibi_slopes():
    return [2.0 ** (-8.0 * (i + 1) / N_ALIBI_HEADS) for i in range(N_ALIBI_HEADS)]


NSA_SLOPES = _alibi_slopes()[0::2]
DIFF_SLOPES = _alibi_slopes()[1::2]


def _silu(x):
    return x * jax.nn.sigmoid(x)


def _dot(a, b):
    return jnp.dot(a, b, preferred_element_type=F32)


def _dot_nt(a, b):
    return lax.dot_general(a, b, (((1,), (1,)), ((), ())), preferred_element_type=F32)


def _dot_tn(a, b):
    return lax.dot_general(a, b, (((0,), (0,)), ((), ())), preferred_element_type=F32)


def _params(sem):
    return pltpu.CompilerParams(dimension_semantics=sem, vmem_limit_bytes=VMEM_LIMIT)


def _relayout_w_in(w_in):
    cols = []
    for _n, _dt, src, width in IN_OUTPUTS:
        used = 0
        for s in src:
            off, w = IN_OFF[s]
            cols.append(w_in[:, :, off:off + w])
            used += w
        if used < width:
            cols.append(jnp.zeros(w_in.shape[:2] + (width - used,), w_in.dtype))
    return jnp.concatenate(cols, axis=-1)


def _relayout_w_in_t(w_in):
    cols = [w_in[:, :, IN_OFF[s][0]:IN_OFF[s][0] + IN_OFF[s][1]] for s in IN_T_SRC]
    return jnp.swapaxes(jnp.concatenate(cols, axis=-1), 1, 2)


def _norm_project(x, nw_ref, w_ref, wt_ref, out_refs):
    ms = jnp.mean(x * x, axis=-1, keepdims=True)
    h = (x * lax.rsqrt(ms + EPS) * nw_ref[...]).astype(BF16)
    for ref, (a, b) in zip(out_refs[:-1], IN_SEGS):
        ref[...] = _dot(h, w_ref[:, a:b]).astype(ref.dtype)
    out_refs[-1][...] = _dot_nt(wt_ref[...], h).astype(BF16)


def _in_proj_kernel(x_ref, nw_ref, w_ref, wt_ref, *out_refs):
    _norm_project(x_ref[...], nw_ref, w_ref, wt_ref, out_refs)


def _in_proj_outputs(m, tm):
    out_shape = [jax.ShapeDtypeStruct((m, b - a), dt) for (_n, dt, _s, _w), (a, b) in zip(IN_OUTPUTS, IN_SEGS)]
    out_specs = [pl.BlockSpec((tm, b - a), lambda i: (i, 0)) for (a, b) in IN_SEGS]
    out_shape.append(jax.ShapeDtypeStruct((IN_T_ROWS, m), BF16))
    out_specs.append(pl.BlockSpec((IN_T_ROWS, tm), lambda i: (0, i)))
    return out_shape, out_specs


def _in_proj(x2d, norm_w, w_r, w_t, tm):
    m = x2d.shape[0]
    out_shape, out_specs = _in_proj_outputs(m, tm)
    return pl.pallas_call(
        _in_proj_kernel,
        grid=(m // tm,),
        in_specs=[pl.BlockSpec((tm, D_MODEL), lambda i: (i, 0)),
                  pl.BlockSpec((1, D_MODEL), lambda i: (0, 0)),
                  pl.BlockSpec((D_MODEL, IN_WP), lambda i: (0, 0)),
                  pl.BlockSpec((IN_T_ROWS, D_MODEL), lambda i: (0, 0))],
        out_specs=out_specs,
        out_shape=out_shape,
        compiler_params=_params(("parallel",)),
        name="in_proj",
    )(x2d, norm_w.reshape(1, D_MODEL), w_r, w_t)


def _out_proj_kernel(y0_ref, y1_ref, y2_ref, y3_ref, w_ref, x_ref, fw_ref, o_ref):
    acc = x_ref[...]
    for g, y_ref in enumerate((y0_ref, y1_ref, y2_ref, y3_ref)):
        acc = acc + _dot(y_ref[...], w_ref[g * GROUP_W:(g + 1) * GROUP_W, :])
    ms = jnp.mean(acc * acc, axis=-1, keepdims=True)
    o_ref[...] = acc * lax.rsqrt(ms + EPS) * fw_ref[...]


def _out_proj(ys, w_out_b, x2d, final_w, tm):
    m = x2d.shape[0]
    yspec = pl.BlockSpec((tm, GROUP_W), lambda i: (i, 0))
    return pl.pallas_call(
        _out_proj_kernel,
        grid=(m // tm,),
        in_specs=[yspec, yspec, yspec, yspec,
                  pl.BlockSpec((D_MODEL, D_MODEL), lambda i: (0, 0)),
                  pl.BlockSpec((tm, D_MODEL), lambda i: (i, 0)),
                  pl.BlockSpec((1, D_MODEL), lambda i: (0, 0))],
        out_specs=pl.BlockSpec((tm, D_MODEL), lambda i: (i, 0)),
        out_shape=jax.ShapeDtypeStruct((m, D_MODEL), F32),
        compiler_params=_params(("parallel",)),
        name="out_proj",
    )(*ys, w_out_b, x2d, final_w.reshape(1, D_MODEL))


def _out_in_proj_kernel(y0_ref, y1_ref, y2_ref, y3_ref, wo_ref, x_ref, nw_ref, w_ref, wt_ref, xo_ref, *out_refs):
    acc = x_ref[...]
    for g, y_ref in enumerate((y0_ref, y1_ref, y2_ref, y3_ref)):
        acc = acc + _dot(y_ref[...], wo_ref[g * GROUP_W:(g + 1) * GROUP_W, :])
    xo_ref[...] = acc
    _norm_project(acc, nw_ref, w_ref, wt_ref, out_refs)


def _out_in_proj(ys, w_out_b, x2d, norm_w, w_r, w_t, tm):
    m = x2d.shape[0]
    yspec = pl.BlockSpec((tm, GROUP_W), lambda i: (i, 0))
    xspec = pl.BlockSpec((tm, D_MODEL), lambda i: (i, 0))
    out_shape, out_specs = _in_proj_outputs(m, tm)
    return pl.pallas_call(
        _out_in_proj_kernel,
        grid=(m // tm,),
        in_specs=[yspec, yspec, yspec, yspec,
                  pl.BlockSpec((D_MODEL, D_MODEL), lambda i: (0, 0)),
                  xspec,
                  pl.BlockSpec((1, D_MODEL), lambda i: (0, 0)),
                  pl.BlockSpec((D_MODEL, IN_WP), lambda i: (0, 0)),
                  pl.BlockSpec((IN_T_ROWS, D_MODEL), lambda i: (0, 0))],
        out_specs=[xspec] + out_specs,
        out_shape=[jax.ShapeDtypeStruct((m, D_MODEL), F32)] + out_shape,
        compiler_params=_params(("parallel",)),
        name="out_in_proj",
    )(*ys, w_out_b, x2d, norm_w.reshape(1, D_MODEL), w_r, w_t)


def _run_interleaved(*pipelines):
    live = list(pipelines)
    while live:
        for g in list(live):
            if next(g, _DONE) is _DONE:
                live.remove(g)


_DONE = object()


def _stage_scores(s_ref, slot, cols, s, keep=None):
    if keep is not None:
        s = jnp.where(keep, s, NEG_INF)
    s_ref[slot, :, cols] = s
    return jnp.max(s, axis=0, keepdims=True)


def _flash_update_t(s, s_max, c1, shift, vt_ext, m_ref, acc_ref, idx, cols):
    m_old = m_ref[idx, :, cols]
    m_new = jnp.maximum(m_old, c1 * s_max + shift)
    alpha = jnp.exp2(m_old - m_new)
    p = jnp.exp2(c1 * s() - (m_new - shift))
    acc_ref[idx, :, cols] = alpha * acc_ref[idx, :, cols] + _dot(vt_ext, p.astype(BF16))
    m_ref[idx, :, cols] = m_new


def _bf16_pieces(x, n=3):
    out = []
    for _ in range(n):
        p = float(np.asarray(x, np.float32).astype(BF16).astype(np.float32))
        out.append(p)
        x = x - p
    return out


def _key_pos_features(tk):
    lane = lax.broadcasted_iota(jnp.int32, (tk, LANES), 1)
    row = lax.broadcasted_iota(jnp.int32, (tk, LANES), 0)
    out = jnp.zeros((tk, LANES), jnp.int32)
    for d in range(-(-tk // POS_RADIX)):
        digit = jnp.bitwise_and(jnp.right_shift(row, 8 * d), POS_RADIX - 1)
        out = jnp.where((lane >= ALIBI_ROWS * d) & (lane < ALIBI_ROWS * (d + 1)), digit, out)
    return out.astype(F32).astype(BF16)


def _alibi_rows(beta, tq, rows=LANES, tk=POS_RADIX):
    row = lax.broadcasted_iota(jnp.int32, (rows, tq), 0)
    out = jnp.zeros((rows, tq), F32)
    for d in range(-(-tk // POS_RADIX)):
        for r, piece in enumerate(_bf16_pieces(beta * POS_RADIX ** d, ALIBI_ROWS)):
            out = jnp.where(row == ALIBI_ROWS * d + r, piece, out)
    return out.astype(BF16)


def _normalized(acc):
    return acc[0:HEAD_DIM] / jnp.maximum(acc[HEAD_DIM:HEAD_DIM + 1], 1e-30)


def _rel_t(tk, tq):
    return lax.broadcasted_iota(jnp.int32, (tk, tq), 1) - lax.broadcasted_iota(jnp.int32, (tk, tq), 0)


def _diff_phases(lam_ref, sw_ref, q_ref, k_ref, vt_ref, z_ref, o_ref, m_ref, acc_ref, wq_ref, s_ref,
                 *, tq, tk, lam_init):
    qi = pl.program_id(1)
    kpq = tq // tk
    scale = DIFF_QK_DIM ** -0.5
    c1 = scale * LOG2E
    m_ref[...] = jnp.full(m_ref.shape, NEG_INF, F32)
    acc_ref[...] = jnp.zeros(acc_ref.shape, F32)
    rel = _rel_t(tk, tq)

    qt = q_ref[...].astype(F32).T.astype(BF16)
    row = lax.broadcasted_iota(jnp.int32, (LANES, tq), 0)
    for j in range(2 * HEADS):
        g, r0 = divmod(j * DIFF_QK_DIM, LANES)
        qg = qt[g * LANES:(g + 1) * LANES]
        wq_ref[j, 0:LANES, :] = jnp.where((row >= r0) & (row < r0 + DIFF_QK_DIM), qg, jnp.zeros_like(qg))
        wq_ref[j, LANES:2 * LANES, :] = _alibi_rows(DIFF_SLOPES[j // 2] / scale, tq, tk=tk)

    ones = jnp.ones((BF16_ROWS, tk), BF16)
    kpos = _key_pos_features(tk)

    n_maps = 2 * HEADS
    n_slots = s_ref.shape[0]

    def tiles(kis, mask_offs):
        loaded = []
        for ki in kis:
            start = pl.multiple_of(ki * tk, tk)
            loaded.append((k_ref[pl.ds(start, tk), :],
                           vt_ref[:, pl.ds(start, tk)],
                           (ki * tk).astype(F32)))
        items = [(t, j) for t in range(len(kis)) for j in range(n_maps)]
        cols = slice(0, tq)

        def scores(i):
            t, j = items[i]
            g = (j * DIFF_QK_DIM) // LANES
            lhs = jnp.concatenate([loaded[t][0][:, g * LANES:(g + 1) * LANES], kpos], axis=1)
            s_ref[i % n_slots] = _dot(lhs, wq_ref[j])

        def staged(i):
            t = items[i][0]
            if mask_offs[t] is None:
                return lambda: s_ref[i % n_slots]
            return lambda: jnp.where(rel >= mask_offs[t], s_ref[i % n_slots], NEG_INF)

        for i in range(QK_LOOKAHEAD):
            scores(i)
        for i, (t, j) in enumerate(items):
            h = j // 2
            if i + QK_LOOKAHEAD < len(items):
                scores(i + QK_LOOKAHEAD)
            _k, vt, key0 = loaded[t]
            vt_ext = jnp.concatenate([vt[h * HEAD_DIM:(h + 1) * HEAD_DIM], ones], axis=0)
            s = staged(i)
            _flash_update_t(s, jnp.max(s(), axis=0, keepdims=True), c1, (DIFF_SLOPES[h] * LOG2E) * key0,
                            vt_ext, m_ref, acc_ref, j, cols)
            yield

    def loop_steps(i):
        return tiles([i * kpq + d for d in range(kpq)], [None] * kpq)

    def diag_steps():
        return tiles([qi * kpq + d for d in range(kpq)], [d * tk for d in range(kpq)])

    def finish():
        lp = lam_ref[...]
        lam = (jnp.exp(jnp.sum(lp[0:1] * lp[1:2], axis=-1, keepdims=True))
               - jnp.exp(jnp.sum(lp[2:3] * lp[3:4], axis=-1, keepdims=True)) + lam_init)
        sw = sw_ref[...]
        o_t = jnp.concatenate([_normalized(acc_ref[2 * h]) - lam * _normalized(acc_ref[2 * h + 1])
                               for h in range(HEADS)], axis=0)
        o = o_t.T
        outs = []
        for h in range(HEADS):
            oh = o[:, h * HEAD_DIM:(h + 1) * HEAD_DIM]
            ms = jnp.mean(oh * oh, axis=-1, keepdims=True)
            outs.append(oh * lax.rsqrt(ms + EPS) * sw * (1.0 - lam_init))
        y = jnp.concatenate(outs, axis=-1) * _silu(z_ref[...])
        o_ref[...] = y.astype(o_ref.dtype)

    return loop_steps, diag_steps, finish


def _diff_operands(qk, vt_all, z_all, lam_p, subln_w, seq, nq, tq, tk):
    inputs = [lam_p, subln_w.reshape(1, HEAD_DIM), qk, qk, vt_all, z_all]
    in_specs = [pl.BlockSpec((4, DIFF_QK_DIM), lambda b, i: (0, 0)),
                pl.BlockSpec((1, HEAD_DIM), lambda b, i: (0, 0)),
                pl.BlockSpec((tq, GROUP_W), lambda b, i: (b * nq + i, 0)),
                pl.BlockSpec((seq, GROUP_W), lambda b, i: (b, 1)),
                pl.BlockSpec((GROUP_W, seq), lambda b, i: (DIFF_VT_BLK, b)),
                pl.BlockSpec((tq, GROUP_W), lambda b, i: (b * nq + i, 1))]
    scratch = [pltpu.VMEM((2 * HEADS, 1, tq), F32),
               pltpu.VMEM((2 * HEADS, ACC_ROWS, tq), F32),
               pltpu.VMEM((2 * HEADS, 2 * LANES, tq), BF16),
               pltpu.VMEM((QK_LOOKAHEAD + 1, tk, tq), F32)]
    return inputs, in_specs, scratch


def _nsa_compress_kernel(cmp_ref, pe_ref, w1_ref, wk2_ref, wv2t_ref, kc_ref, vct_ref):
    ng = kc_ref.shape[1]
    top = jnp.zeros((ng, 2 * CMP_HIDDEN), F32)
    bot = jnp.zeros((ng, 2 * CMP_HIDDEN), F32)
    for p in range(CMP_STRIDE):
        x = cmp_ref[pl.ds(p, ng, stride=CMP_STRIDE), :]
        top = top + _dot((x + pe_ref[p:p + 1, :]).astype(BF16), w1_ref[p])
        bot = bot + _dot((x + pe_ref[CMP_STRIDE + p:CMP_STRIDE + p + 1, :]).astype(BF16), w1_ref[CMP_STRIDE + p])
    hid = _silu(top + pltpu.roll(bot, ng - 1, 0)).astype(BF16)
    hk, hv = hid[:, 0:CMP_HIDDEN], hid[:, CMP_HIDDEN:]
    kc = _dot(hk, wk2_ref[...])
    lane = lax.broadcasted_iota(jnp.int32, kc.shape, 1)
    blk = lax.broadcasted_iota(jnp.int32, kc.shape, 0).astype(F32)
    kc = jnp.where((lane >= HEAD_DIM) & (lane < HEAD_DIM + ALIBI_ROWS), blk, kc)
    kc_ref[0] = kc.astype(kc_ref.dtype)
    vct_ref[0] = _dot_nt(wv2t_ref[...], hv).astype(vct_ref.dtype)


def _nsa_compress(cmp2d, pe_k, pe_v, w_ck1, w_ck2, w_cv1, w_cv2, batch, seq):
    ng = seq // CMP_STRIDE
    wk = w_ck1.reshape(CMP_LEN, HEAD_DIM, CMP_HIDDEN)
    wv = w_cv1.reshape(CMP_LEN, HEAD_DIM, CMP_HIDDEN)
    w1 = jnp.concatenate([jnp.pad(wk, ((0, 0), (0, 0), (0, CMP_HIDDEN))),
                          jnp.pad(wv, ((0, 0), (0, 0), (CMP_HIDDEN, 0)))], axis=1).astype(BF16)
    pe = jnp.concatenate([pe_k, pe_v], axis=1)

    def full(shape):
        return pl.BlockSpec(shape, lambda b: (0,) * len(shape))

    return pl.pallas_call(
        _nsa_compress_kernel,
        grid=(batch,),
        in_specs=[pl.BlockSpec((seq, LANES), lambda b: (b, 0)), full((CMP_LEN, LANES)),
                  full((CMP_LEN, LANES, 2 * CMP_HIDDEN)), full((CMP_HIDDEN, LANES)),
                  full((HEAD_DIM, CMP_HIDDEN))],
        out_specs=[pl.BlockSpec((1, ng, LANES), lambda b: (b, 0, 0)),
                   pl.BlockSpec((1, HEAD_DIM, ng), lambda b: (b, 0, 0))],
        out_shape=[jax.ShapeDtypeStruct((batch, ng, LANES), BF16),
                   jax.ShapeDtypeStruct((batch, HEAD_DIM, ng), BF16)],
        compiler_params=_params(("parallel",)),
        name="nsa_compress",
    )(cmp2d, pe, w1, jnp.pad(w_ck2, ((0, 0), (0, LANES - HEAD_DIM))).astype(BF16), w_cv2.T.astype(BF16))


def _nsa_phases(ovl_ref, q_ref, k_ref, vt_ref, kc_ref, vct_ref, misc_ref, z_ref, o_ref,
                m_ref, acc_ref, wq_ref, s_ref, sc_ref, *, tq, tk, seq):
    qi = pl.program_id(1)
    kpq = tq // tk
    scale = HEAD_DIM ** -0.5
    c1 = scale * LOG2E
    ng = seq // CMP_STRIDE
    ns = seq // SLC_BLOCK
    top = min(SLC_TOPK, ns)
    m_ref[...] = jnp.full(m_ref.shape, NEG_INF, F32)
    acc_ref[...] = jnp.zeros(acc_ref.shape, F32)
    rel = _rel_t(tk, tq)

    qt = q_ref[...].astype(F32).T.astype(BF16)
    zeros_q = jnp.zeros((HEAD_DIM, tq), BF16)
    for h in range(HEADS):
        qh = qt[h * HEAD_DIM:(h + 1) * HEAD_DIM]
        alibi = _alibi_rows(NSA_SLOPES[h] / scale, tq, tk=tk)
        wq_ref[h, 0:LANES, :] = jnp.concatenate([qh, zeros_q], axis=0)
        wq_ref[h, LANES:2 * LANES, :] = alibi
        wq_ref[HEADS + h, 0:LANES, :] = jnp.concatenate([zeros_q, qh], axis=0)
        wq_ref[HEADS + h, LANES:2 * LANES, :] = alibi

    t_lane = qi * tq + lax.broadcasted_iota(jnp.int32, (1, tq), 1)
    n_sub = lax.broadcasted_iota(jnp.int32, (ng, 1), 0)
    c_valid = (n_sub * CMP_STRIDE + (CMP_LEN - 1)) <= t_lane
    kc = kc_ref[0]
    vct_ext = jnp.concatenate([vct_ref[0], jnp.ones((BF16_ROWS, ng), BF16)], axis=0)
    ovl = ovl_ref[...]
    cmp_max = []
    for h in range(HEADS):
        rhs = jnp.concatenate([qt[h * HEAD_DIM:(h + 1) * HEAD_DIM],
                               _alibi_rows(NSA_SLOPES[h] * CMP_STRIDE / scale, tq, HEAD_DIM)], axis=0)
        cmp_max.append(_stage_scores(sc_ref, h, slice(0, tq), _dot(kc, rhs), c_valid))
    imp = jnp.zeros((ns, tq), F32)
    o_cmp = []
    for h in range(HEADS):
        mx = jnp.where(cmp_max[h] > NEG_INF, cmp_max[h], 0.0)
        pb = jnp.exp2(c1 * sc_ref[h] - c1 * mx).astype(BF16)
        o_ext = _dot(vct_ext, pb)
        r = 1.0 / jnp.maximum(o_ext[HEAD_DIM:HEAD_DIM + 1], 1e-30)
        o_cmp.append(o_ext[0:HEAD_DIM] * r)
        imp = imp + _dot(ovl, pb) * r

    def select_blocks():
        j_sub = lax.broadcasted_iota(jnp.int32, (ns, 1), 0)
        j_sub_f = j_sub.astype(F32)
        cur = jnp.right_shift(t_lane, SLC_SHIFT)
        forced = (j_sub == 0) | (j_sub == cur) | (j_sub == cur - 1)
        valid = (j_sub * SLC_BLOCK) <= t_lane
        score = jnp.where(forced, 1e30, jnp.where(valid, imp, -1.0))
        sel = jnp.zeros((ns, tq), F32)
        for _ in range(top):
            mx = jnp.max(score, axis=0, keepdims=True)
            idx = jnp.min(jnp.where(score == mx, j_sub_f, float(ns)), axis=0, keepdims=True)
            pick = j_sub_f == idx
            sel = jnp.where(pick, 1.0, sel)
            score = jnp.where(pick, -2.0, score)
        unsel = jnp.where(sel > 0.5, 0.0, -MASK_BIG).astype(BF16)
        for h in range(HEADS):
            wq_ref[h, LANES + BLK_LANE0:LANES + BLK_LANE0 + ns, :] = unsel

    aux_lane = lax.broadcasted_iota(jnp.int32, (tk, LANES), 1)
    aux_blk = jnp.right_shift(lax.broadcasted_iota(jnp.int32, (tk, LANES), 0), SLC_SHIFT) + BLK_LANE0
    kpos = _key_pos_features(tk)
    ones = jnp.ones((BF16_ROWS, tk), BF16)

    n_slots = s_ref.shape[0]

    def tiles(window, kis, cols, mask_offs=None):
        base = HEADS if window else 0
        loaded = []
        for t, ki in enumerate(kis):
            start = pl.multiple_of(ki * tk, tk)
            k = k_ref[pl.ds(start, tk), :]
            if window:
                vt = vt_ref[HEAD_DIM:2 * HEAD_DIM, pl.ds(start, tk)]
                aux = kpos
                dist = rel[:, cols[t]] + (qi * tq - ki * tk)
                keep = (dist >= 0) & (dist < WINDOW)
            else:
                vt = vt_ref[0:HEAD_DIM, pl.ds(start, tk)]
                onehot = aux_lane == aux_blk + ki * (tk // SLC_BLOCK)
                aux = jnp.where(onehot, jnp.ones_like(kpos), kpos)
                keep = None
            loaded.append((jnp.concatenate([k, aux], axis=1),
                           jnp.concatenate([vt, ones], axis=0), (ki * tk).astype(F32), keep))
        items = [(t, h) for t in range(len(kis)) for h in range(HEADS)]
        s_max = {}

        def scores(i):
            t, h = items[i]
            if window:
                keep = loaded[t][3]
            else:
                keep = None if mask_offs[t] is None else rel[:, cols[t]] >= mask_offs[t]
            s_max[i] = _stage_scores(s_ref, i % n_slots, cols[t],
                                     _dot(loaded[t][0], wq_ref[base + h, :, cols[t]]), keep)

        for i in range(QK_LOOKAHEAD):
            scores(i)
        for i, (t, h) in enumerate(items):
            if i + QK_LOOKAHEAD < len(items):
                scores(i + QK_LOOKAHEAD)
            _lhs, vt_ext, key0, _keep = loaded[t]
            _flash_update_t(lambda i=i, t=t: s_ref[i % n_slots, :, cols[t]], s_max.pop(i), c1,
                            (NSA_SLOPES[h] * LOG2E) * key0, vt_ext, m_ref, acc_ref, base + h, cols[t])
            yield

    all_cols = slice(0, tq)
    diag_cols = [slice(d * tk, tq) for d in range(kpq)]

    _run_interleaved(tiles(True, [qi * kpq + d for d in range(kpq)], diag_cols))
    select_blocks()

    def loop_steps(i):
        return tiles(False, [i * kpq + d for d in range(kpq)], [all_cols] * kpq, [None] * kpq)

    def back_window():
        n_back = (WINDOW + tk - 1) // tk
        for g in range((n_back + kpq - 1) // kpq):
            backs = list(range(g * kpq + 1, min((g + 1) * kpq, n_back) + 1))
            back_cols = [slice(0, min(tq, -(-(WINDOW - (back - 1) * tk - 1) // LANES) * LANES)) for back in backs]

            @pl.when(qi > g)
            def _():
                _run_interleaved(tiles(True, [qi * kpq - back for back in backs], back_cols))

    def diag_steps():
        return tiles(False, [qi * kpq + d for d in range(kpq)], diag_cols, [d * tk for d in range(kpq)])

    def finish():
        g_t = jax.nn.sigmoid(misc_ref[...]).T
        outs = []
        for h in range(HEADS):
            r0 = GATE_COL + 3 * h
            outs.append(g_t[r0:r0 + 1] * o_cmp[h] + g_t[r0 + 1:r0 + 2] * _normalized(acc_ref[h])
                        + g_t[r0 + 2:r0 + 3] * _normalized(acc_ref[HEADS + h]))
        y = jnp.concatenate(outs, axis=0).T * _silu(z_ref[...])
        o_ref[...] = y.astype(o_ref.dtype)

    return loop_steps, back_window, diag_steps, finish


def _overlap_t(seq):
    nc = (seq - CMP_LEN) // CMP_STRIDE + 1
    ng = seq // CMP_STRIDE
    ns = seq // SLC_BLOCK
    c_start = np.arange(ng) * CMP_STRIDE
    c_end = c_start + CMP_LEN - 1
    s_start = np.arange(ns) * SLC_BLOCK
    s_end = s_start + SLC_BLOCK - 1
    ov = (c_start[None, :] <= s_end[:, None]) & (c_end[None, :] >= s_start[:, None]) & (np.arange(ng)[None, :] < nc)
    return jnp.asarray(ov.astype(np.float32), dtype=BF16)


def _nsa_operands(q, k2, vt_all, kc, vct, misc, z_all, seq, nq, tq, tk):
    ng = seq // CMP_STRIDE
    ns = seq // SLC_BLOCK
    assert tq - tk < WINDOW and tk % SLC_BLOCK == 0 and ns <= LANES - BLK_LANE0
    inputs = [_overlap_t(seq), q, k2, vt_all, kc, vct, misc, z_all]
    in_specs = [pl.BlockSpec((ns, ng), lambda b, i: (0, 0)),
                pl.BlockSpec((tq, GROUP_W), lambda b, i: (b * nq + i, 0)),
                pl.BlockSpec((seq, LANES), lambda b, i: (b, 0)),
                pl.BlockSpec((2 * HEAD_DIM, seq), lambda b, i: (NSA_VT_BLK, b)),
                pl.BlockSpec((1, ng, LANES), lambda b, i: (b, 0, 0)),
                pl.BlockSpec((1, HEAD_DIM, ng), lambda b, i: (b, 0, 0)),
                pl.BlockSpec((tq, LANES), lambda b, i: (b * nq + i, 0)),
                pl.BlockSpec((tq, GROUP_W), lambda b, i: (b * nq + i, 0))]
    scratch = [pltpu.VMEM((2 * HEADS, 1, tq), F32),
               pltpu.VMEM((2 * HEADS, ACC_ROWS, tq), F32),
               pltpu.VMEM((2 * HEADS, 2 * LANES, tq), BF16),
               pltpu.VMEM((2 * HEADS, tk, tq), F32),
               pltpu.VMEM((HEADS, ng, tq), F32)]
    return inputs, in_specs, scratch


def _mixers_kernel(*refs, n_in, n_scratch, tq, tk_nsa, tk_diff, seq, lam_init):
    bounds = np.cumsum([0] + list(n_in))
    nsa_in, diff_in, ret_in, ssd_in = (refs[a:b] for a, b in zip(bounds[:-1], bounds[1:]))
    nsa_out, diff_out, ret_out, ssd_out = refs[bounds[-1]:bounds[-1] + 4]
    sb = np.cumsum([0] + list(n_scratch)) + bounds[-1] + 4
    nsa_scr, diff_scr, rec_scr = (refs[a:b] for a, b in zip(sb[:-1], sb[1:]))
    ret_st, ssd_ext, ssd_st = rec_scr
    qi = pl.program_id(1)

    @pl.when(qi == 0)
    def _():
        ret_st[...] = jnp.zeros(ret_st.shape, F32)
        ssd_st[...] = jnp.zeros(ssd_st.shape, F32)
        ssd_ext[0:8, :] = jnp.zeros((8, CONV_CH), F32)

    chunks = list(range(ret_out.shape[0]))
    first, second = chunks[:len(chunks) // 2], chunks[len(chunks) // 2:]
    nsa_loop, nsa_back_window, nsa_diag, nsa_finish = _nsa_phases(
        *nsa_in, nsa_out, *nsa_scr, tq=tq, tk=tk_nsa, seq=seq)
    diff_loop, diff_diag, diff_finish = _diff_phases(
        *diff_in, diff_out, *diff_scr, tq=tq, tk=tk_diff, lam_init=lam_init)
    for c in first:
        _ret_body(*ret_in, ret_out, ret_st, [c])
        _ssd_body(*ssd_in, ssd_out, ssd_ext, ssd_st, [c])

    def body(i, carry):
        _run_interleaved(nsa_loop(i), diff_loop(i))
        return carry

    lax.fori_loop(0, qi, body, 0)
    nsa_back_window()
    _run_interleaved(nsa_diag(), diff_diag())
    for c in second:
        _ret_body(*ret_in, ret_out, ret_st, [c])
        _ssd_body(*ssd_in, ssd_out, ssd_ext, ssd_st, [c])
    nsa_finish()
    diff_finish()


def _mixers(nsa_args, diff_args, rec_args, batch, seq, layer_idx, tq, tk_nsa, tk_diff):
    nq = seq // tq
    nsa_in, nsa_specs, nsa_scratch = _nsa_operands(*nsa_args, seq, nq, tq, tk_nsa)
    diff_in, diff_specs, diff_scratch = _diff_operands(*diff_args, seq, nq, tq, tk_diff)
    ret_in, ssd_in, rec_specs, rec_out_spec, rec_scratch = _recurrent_operands(*rec_args, nq, tq)
    out_spec = pl.BlockSpec((tq, GROUP_W), lambda b, i: (b * nq + i, 0))
    m = batch * seq
    y_nsa, y_diff, y_ret, y_ssm = pl.pallas_call(
        functools.partial(_mixers_kernel, n_in=(len(nsa_in), len(diff_in), len(ret_in), len(ssd_in)),
                          n_scratch=(len(nsa_scratch), len(diff_scratch), len(rec_scratch)),
                          tq=tq, tk_nsa=tk_nsa, tk_diff=tk_diff, seq=seq,
                          lam_init=0.8 - 0.6 * math.exp(-0.3 * layer_idx)),
        grid=(batch, nq),
        in_specs=nsa_specs + diff_specs + rec_specs,
        out_specs=[out_spec, out_spec, rec_out_spec, rec_out_spec],
        out_shape=[jax.ShapeDtypeStruct((m, GROUP_W), BF16)] * 2
                  + [jax.ShapeDtypeStruct((m // SSM_CHUNK, SSM_CHUNK, GROUP_W), BF16)] * 2,
        scratch_shapes=nsa_scratch + diff_scratch + rec_scratch,
        compiler_params=_params(("parallel", "arbitrary")),
        name="mixers",
    )(*nsa_in, *diff_in, *ret_in, *ssd_in)
    return y_nsa, y_diff, y_ret.reshape(m, GROUP_W), y_ssm.reshape(m, GROUP_W)


def _ret_tables():
    c = RET_CHUNK
    h = np.arange(HEADS, dtype=np.float32)
    log_g = jnp.log(1.0 - 2.0 ** (-5.0 - jnp.asarray(h)))
    pos = jnp.arange(c, dtype=F32)
    rel = pos[:, None] - pos[None, :]
    decay = jnp.where(rel >= 0, jnp.exp(log_g[:, None, None] * jnp.maximum(rel, 0.0)), 0.0)
    xi = jnp.exp(log_g[:, None] * (pos + 1.0))
    zeta = jnp.exp(log_g[:, None] * (c - 1.0 - pos))
    chunk_decay = jnp.exp(log_g * c)
    xi_tab = jnp.repeat(xi.T, HEAD_DIM, axis=1)
    zeta_tab = jnp.repeat(zeta.T, HEAD_DIM, axis=1)
    cd_tab = jnp.repeat(chunk_decay, HEAD_DIM)[None, :]
    return decay, xi_tab, zeta_tab, cd_tab


def _ret_body(decay_ref, xi_ref, zeta_ref, cd_ref, gn_ref, q_ref, k_ref, v_ref, z_ref, o_ref, st_ref, blocks):
    xi = xi_ref[...]
    cd = cd_ref[...]
    for bb in blocks:
        q = (q_ref[bb].astype(F32) * (HEAD_DIM ** -0.5)).astype(BF16)
        k = k_ref[bb]
        v = v_ref[bb]
        kz_t = (k.astype(F32) * zeta_ref[...]).T.astype(BF16)
        outs = []
        for h in range(HEADS):
            sl = slice(h * HEAD_DIM, (h + 1) * HEAD_DIM)
            qh, kh, vh = q[:, sl], k[:, sl], v[:, sl]
            prev = st_ref[h]
            inner = (_dot_nt(qh, kh) * decay_ref[h]).astype(BF16)
            o = _dot(inner, vh) + _dot(qh, prev.astype(BF16)) * xi[:, sl]
            st_ref[h] = prev * cd[:, sl] + _dot(kz_t[sl, :], vh)
            mu = jnp.mean(o, axis=-1, keepdims=True)
            d = o - mu
            var = jnp.mean(d * d, axis=-1, keepdims=True)
            outs.append(d * lax.rsqrt(var + EPS))
        y = jnp.concatenate(outs, axis=-1) * gn_ref[...] * _silu(z_ref[bb])
        o_ref[bb] = y.astype(o_ref.dtype)


def _ssd_body(cw_ref, cb_ref, dtb_ref, a_ref, dsk_ref, nw_ref, xbc_ref, misc_ref, z_ref, o_ref,
              ext_ref, st_ref, blocks):
    L = SSM_CHUNK
    hi = lax.Precision.HIGHEST
    row = lax.broadcasted_iota(jnp.int32, (L, L), 0)
    col = lax.broadcasted_iota(jnp.int32, (L, L), 1)
    causal = row >= col
    tril = jnp.where(causal, 1.0, 0.0).astype(F32)
    dsk = dsk_ref[...]

    for bb in blocks:
        raw = xbc_ref[bb]
        ext_ref[8:8 + L, :] = raw
        conv = cb_ref[...] + raw * cw_ref[CONV_W - 1:CONV_W, :]
        for w in range(CONV_W - 1):
            shift = CONV_W - 1 - w
            conv = conv + ext_ref[8 - shift:8 - shift + L, :] * cw_ref[w:w + 1, :]
        ext_ref[0:8, :] = raw[L - 8:L, :]
        xc = _silu(conv)
        x = xc[:, 0:GROUP_W]
        bm = xc[:, GROUP_W:GROUP_W + 2 * SSM_STATE].astype(BF16)
        cm = xc[:, GROUP_W + 2 * SSM_STATE:].astype(BF16)

        dt_full = jax.nn.softplus(misc_ref[bb] + dtb_ref[...])
        da = dt_full * a_ref[...]
        cs_col = jnp.dot(tril, da, precision=hi, preferred_element_type=F32)
        cs_row = lax.dot_general(da, tril, (((0,), (1,)), ((), ())), precision=hi,
                                 preferred_element_type=F32)

        outs = []
        for h in range(HEADS):
            g = h // 2
            c0 = DT_COL + h
            sl = slice(h * HEAD_DIM, (h + 1) * HEAD_DIM)
            gs = slice(g * SSM_STATE, (g + 1) * SSM_STATE)
            cs_c = cs_col[:, c0:c0 + 1]
            cs_r = cs_row[c0:c0 + 1, :]
            cs_last = cs_col[L - 1:L, c0:c0 + 1]
            xh = x[:, sl]
            xdt = xh * dt_full[:, c0:c0 + 1]
            seg = jnp.exp(jnp.where(causal, cs_c - cs_r, NEG_INF))
            cb = _dot_nt(cm[:, gs], bm[:, gs])
            y = _dot((cb * seg).astype(BF16), xdt.astype(BF16))
            prev = st_ref[h]
            y = y + _dot(cm[:, gs], prev.astype(BF16)) * jnp.exp(cs_c)
            y = y + dsk[:, sl] * xh
            dec = jnp.exp(cs_last - cs_c)
            st_ref[h] = prev * jnp.exp(cs_last) + _dot_tn(bm[:, gs], (xdt * dec).astype(BF16))
            outs.append(y)
        y = jnp.concatenate(outs, axis=-1) * _silu(z_ref[bb])
        ms = jnp.mean(y * y, axis=-1, keepdims=True)
        o_ref[bb] = (y * lax.rsqrt(ms + EPS) * nw_ref[...]).astype(o_ref.dtype)


def _recurrent_operands(ret_qkv, xbc, misc, z_all, gn_w, conv_w, conv_b, dt_bias, a_log, d_skip, norm_w, nq, tq):
    L = SSM_CHUNK
    assert RET_CHUNK == L and tq % L == 0
    cpt = tq // L
    decay, xi_tab, zeta_tab, cd_tab = _ret_tables()
    dtb = jnp.zeros((1, 128), F32).at[0, DT_COL:DT_COL + HEADS].set(dt_bias)
    a_full = jnp.zeros((1, 128), F32).at[0, DT_COL:DT_COL + HEADS].set(-jnp.exp(a_log))
    dsk = jnp.repeat(d_skip, HEAD_DIM)[None, :]

    def full(shape):
        return pl.BlockSpec(shape, lambda b, i: (0,) * len(shape))

    def blk(width, col):
        return pl.BlockSpec((cpt, L, width), lambda b, i: (b * nq + i, 0, col))

    def chunked(a):
        return a.reshape(-1, L, a.shape[-1])

    qkv3, z3 = chunked(ret_qkv), chunked(z_all)
    ret_in = [decay, xi_tab, zeta_tab, cd_tab, gn_w.reshape(1, GROUP_W), qkv3, qkv3, qkv3, z3]
    ret_specs = [full((HEADS, L, L)), full((L, GROUP_W)), full((L, GROUP_W)), full((1, GROUP_W)),
                 full((1, GROUP_W)), blk(GROUP_W, 0), blk(GROUP_W, 1), blk(GROUP_W, 2), blk(GROUP_W, 2)]
    ssd_in = [conv_w, conv_b.reshape(1, CONV_CH), dtb, a_full, dsk, norm_w.reshape(1, GROUP_W),
              chunked(xbc), chunked(misc), z3]
    ssd_specs = [full((CONV_W, CONV_CH)), full((1, CONV_CH)), full((1, 128)), full((1, 128)),
                 full((1, GROUP_W)), full((1, GROUP_W)), blk(CONV_CH, 0), blk(128, 0), blk(GROUP_W, 3)]
    scratch = [pltpu.VMEM((HEADS, HEAD_DIM, HEAD_DIM), F32),
               pltpu.VMEM((8 + L, CONV_CH), F32),
               pltpu.VMEM((HEADS, SSM_STATE, HEAD_DIM), F32)]
    return ret_in, ssd_in, ret_specs + ssd_specs, blk(GROUP_W, 0), scratch


def _pick_tile(n, pref):
    t = pref
    while n % t:
        t //= 2
    return t


def kernel(x, norm_w, w_in, w_out, nsa_pe_k, nsa_pe_v, nsa_w_ck1, nsa_w_ck2, nsa_w_cv1, nsa_w_cv2,
           diff_lam_q1, diff_lam_k1, diff_lam_q2, diff_lam_k2, diff_subln_w, ret_gn_w,
           ssm_conv_w, ssm_conv_b, ssm_dt_bias, ssm_A_log, ssm_D, ssm_norm_w, final_norm_w):
    batch, seq, _ = x.shape
    depth = w_in.shape[0]
    m = batch * seq
    tm = _pick_tile(m, 512)
    tq = _pick_tile(seq, 512)
    tk = _pick_tile(seq, 256)
    tk_diff = _pick_tile(seq, 512)
    w_in_b = w_in.astype(BF16)
    w_r = _relayout_w_in(w_in_b)
    w_t = _relayout_w_in_t(w_in_b)
    w_out_b = w_out.astype(BF16)
    x2d = x.reshape(m, D_MODEL)
    projected = _in_proj(x2d, norm_w[0], w_r[0], w_t[0], tm)
    for i in range(depth):
        nsa_q, nsa_k2, nsa_cmp, misc, z_all, diff_qk, ret_qkv, xbc, vt_all = projected
        kc, vct = _nsa_compress(nsa_cmp, nsa_pe_k[i], nsa_pe_v[i], nsa_w_ck1[i], nsa_w_ck2[i],
                                nsa_w_cv1[i], nsa_w_cv2[i], batch, seq)
        lam_p = jnp.stack([diff_lam_q1[i], diff_lam_k1[i], diff_lam_q2[i], diff_lam_k2[i]])
        ys = _mixers((nsa_q, nsa_k2, vt_all, kc, vct, misc, z_all),
                     (diff_qk, vt_all, z_all, lam_p, diff_subln_w[i]),
                     (ret_qkv, xbc, misc, z_all, ret_gn_w[i], ssm_conv_w[i], ssm_conv_b[i], ssm_dt_bias[i],
                      ssm_A_log[i], ssm_D[i], ssm_norm_w[i]),
                     batch, seq, i, tq, tk, tk_diff)
        if i + 1 < depth:
            x2d, *projected = _out_in_proj(ys, w_out_b[i], x2d, norm_w[i + 1], w_r[i + 1], w_t[i + 1], tm)
        else:
            x2d = _out_proj(ys, w_out_b[i], x2d, final_norm_w, tm)
    return x2d.reshape(batch, seq, D_MODEL)
```

```python
import functools
import math

import numpy as np
import jax
import jax.numpy as jnp
from jax import lax
from jax.experimental import pallas as pl
from jax.experimental.pallas import tpu as pltpu

F32 = jnp.float32
BF16 = jnp.bfloat16
NEG_INF = float("-inf")
LOG2E = 1.4426950408889634

D_MODEL = 1024
DEPTH = 4
GROUP_W = 256
HEADS = 4
HEAD_DIM = 64
EPS = 1e-6
CMP_LEN = 32
CMP_STRIDE = 16
CMP_HIDDEN = 256
SLC_BLOCK = 64
SLC_SHIFT = 6
SLC_TOPK = 16
WINDOW = 512
DIFF_QK_DIM = 32
RET_CHUNK = 128
SSM_STATE = 128
SSM_CHUNK = 128
CONV_W = 4
CONV_CH = 768
N_ALIBI_HEADS = 8
LANES = 128
BF16_ROWS = 16
ACC_ROWS = HEAD_DIM + BF16_ROWS
QK_LOOKAHEAD = 6
ALIBI_ROWS = 3
POS_RADIX = 256
MASK_BIG = 2.0 ** 100
BLK_LANE0 = 64

IN_LAYOUT = (
    ("nsa_q", 256), ("nsa_k_cmp", 64), ("nsa_v_cmp", 64), ("nsa_k_slc", 64), ("nsa_v_slc", 64),
    ("nsa_k_win", 64), ("nsa_v_win", 64), ("nsa_gate", 12), ("nsa_z", 256),
    ("diff_q", 256), ("diff_k", 256), ("diff_v", 256), ("diff_z", 256),
    ("ret_q", 256), ("ret_k", 256), ("ret_v", 256), ("ret_z", 256),
    ("ssm_z", 256), ("ssm_xbc", 768), ("ssm_dt", 4),
)
IN_OFF = {}
_o = 0
for _n, _w in IN_LAYOUT:
    IN_OFF[_n] = (_o, _w)
    _o += _w
IN_W = _o

GATE_COL = 0
DT_COL = 12
IN_OUTPUTS = (
    ("nsa_q", BF16, ("nsa_q",), 256),
    ("nsa_k2", BF16, ("nsa_k_slc", "nsa_k_win"), 128),
    ("nsa_cmp", F32, ("nsa_k_cmp", "nsa_v_cmp"), 128),
    ("misc", F32, ("nsa_gate", "ssm_dt"), 128),
    ("z_all", F32, ("nsa_z", "diff_z", "ret_z", "ssm_z"), 1024),
    ("diff_qk", BF16, ("diff_q", "diff_k"), 512),
    ("ret_qkv", BF16, ("ret_q", "ret_k", "ret_v"), 768),
    ("xbc", F32, ("ssm_xbc",), 768),
)
IN_T_SRC = ("diff_v", "nsa_v_slc", "nsa_v_win")
IN_T_ROWS = 384
DIFF_VT_BLK = 0
NSA_VT_BLK = 2
IN_SEGS = []
_o = 0
for _n, _dt, _src, _w in IN_OUTPUTS:
    IN_SEGS.append((_o, _o + _w))
    _o += _w
IN_WP = _o

VMEM_LIMIT = 56 * 1024 * 1024


def _alibi_slopes():
    return [2.0 ** (-8.0 * (i + 1) / N_ALIBI_HEADS) for i in range(N_ALIBI_HEADS)]


NSA_SLOPES = _alibi_slopes()[0::2]
DIFF_SLOPES = _alibi_slopes()[1::2]


def _silu(x):
    return x * jax.nn.sigmoid(x)


def _dot(a, b):
    return jnp.dot(a, b, preferred_element_type=F32)


def _dot_nt(a, b):
    return lax.dot_general(a, b, (((1,), (1,)), ((), ())), preferred_element_type=F32)


def _dot_tn(a, b):
    return lax.dot_general(a, b, (((0,), (0,)), ((), ())), preferred_element_type=F32)


def _params(sem):
    return pltpu.CompilerParams(dimension_semantics=sem, vmem_limit_bytes=VMEM_LIMIT)


def _relayout_w_in(w_in):
    cols = []
    for _n, _dt, src, width in IN_OUTPUTS:
        used = 0
        for s in src:
            off, w = IN_OFF[s]
            cols.append(w_in[:, :, off:off + w])
            used += w
        if used < width:
            cols.append(jnp.zeros(w_in.shape[:2] + (width - used,), w_in.dtype))
    return jnp.concatenate(cols, axis=-1)


def _relayout_w_in_t(w_in):
    cols = [w_in[:, :, IN_OFF[s][0]:IN_OFF[s][0] + IN_OFF[s][1]] for s in IN_T_SRC]
    return jnp.swapaxes(jnp.concatenate(cols, axis=-1), 1, 2)


def _norm_project(x, nw_ref, w_ref, wt_ref, out_refs):
    ms = jnp.mean(x * x, axis=-1, keepdims=True)
    h = (x * lax.rsqrt(ms + EPS) * nw_ref[...]).astype(BF16)
    for ref, (a, b) in zip(out_refs[:-1], IN_SEGS):
        ref[...] = _dot(h, w_ref[:, a:b]).astype(ref.dtype)
    out_refs[-1][...] = _dot_nt(wt_ref[...], h).astype(BF16)


def _in_proj_kernel(x_ref, nw_ref, w_ref, wt_ref, *out_refs):
    _norm_project(x_ref[...], nw_ref, w_ref, wt_ref, out_refs)


def _in_proj_outputs(m, tm):
    out_shape = [jax.ShapeDtypeStruct((m, b - a), dt) for (_n, dt, _s, _w), (a, b) in zip(IN_OUTPUTS, IN_SEGS)]
    out_specs = [pl.BlockSpec((tm, b - a), lambda i: (i, 0)) for (a, b) in IN_SEGS]
    out_shape.append(jax.ShapeDtypeStruct((IN_T_ROWS, m), BF16))
    out_specs.append(pl.BlockSpec((IN_T_ROWS, tm), lambda i: (0, i)))
    return out_shape, out_specs


def _in_proj(x2d, norm_w, w_r, w_t, tm):
    m = x2d.shape[0]
    out_shape, out_specs = _in_proj_outputs(m, tm)
    return pl.pallas_call(
        _in_proj_kernel,
        grid=(m // tm,),
        in_specs=[pl.BlockSpec((tm, D_MODEL), lambda i: (i, 0)),
                  pl.BlockSpec((1, D_MODEL), lambda i: (0, 0)),
                  pl.BlockSpec((D_MODEL, IN_WP), lambda i: (0, 0)),
                  pl.BlockSpec((IN_T_ROWS, D_MODEL), lambda i: (0, 0))],
        out_specs=out_specs,
        out_shape=out_shape,
        compiler_params=_params(("parallel",)),
        name="in_proj",
    )(x2d, norm_w.reshape(1, D_MODEL), w_r, w_t)


def _out_proj_kernel(y0_ref, y1_ref, y2_ref, y3_ref, w_ref, x_ref, fw_ref, o_ref):
    acc = x_ref[...]
    for g, y_ref in enumerate((y0_ref, y1_ref, y2_ref, y3_ref)):
        acc = acc + _dot(y_ref[...], w_ref[g * GROUP_W:(g + 1) * GROUP_W, :])
    ms = jnp.mean(acc * acc, axis=-1, keepdims=True)
    o_ref[...] = acc * lax.rsqrt(ms + EPS) * fw_ref[...]


def _out_proj(ys, w_out_b, x2d, final_w, tm):
    m = x2d.shape[0]
    yspec = pl.BlockSpec((tm, GROUP_W), lambda i: (i, 0))
    return pl.pallas_call(
        _out_proj_kernel,
        grid=(m // tm,),
        in_specs=[yspec, yspec, yspec, yspec,
                  pl.BlockSpec((D_MODEL, D_MODEL), lambda i: (0, 0)),
                  pl.BlockSpec((tm, D_MODEL), lambda i: (i, 0)),
                  pl.BlockSpec((1, D_MODEL), lambda i: (0, 0))],
        out_specs=pl.BlockSpec((tm, D_MODEL), lambda i: (i, 0)),
        out_shape=jax.ShapeDtypeStruct((m, D_MODEL), F32),
        compiler_params=_params(("parallel",)),
        name="out_proj",
    )(*ys, w_out_b, x2d, final_w.reshape(1, D_MODEL))


def _out_in_proj_kernel(y0_ref, y1_ref, y2_ref, y3_ref, wo_ref, x_ref, nw_ref, w_ref, wt_ref, xo_ref, *out_refs):
    acc = x_ref[...]
    for g, y_ref in enumerate((y0_ref, y1_ref, y2_ref, y3_ref)):
        acc = acc + _dot(y_ref[...], wo_ref[g * GROUP_W:(g + 1) * GROUP_W, :])
    xo_ref[...] = acc
    _norm_project(acc, nw_ref, w_ref, wt_ref, out_refs)


def _out_in_proj(ys, w_out_b, x2d, norm_w, w_r, w_t, tm):
    m = x2d.shape[0]
    yspec = pl.BlockSpec((tm, GROUP_W), lambda i: (i, 0))
    xspec = pl.BlockSpec((tm, D_MODEL), lambda i: (i, 0))
    out_shape, out_specs = _in_proj_outputs(m, tm)
    return pl.pallas_call(
        _out_in_proj_kernel,
        grid=(m // tm,),
        in_specs=[yspec, yspec, yspec, yspec,
                  pl.BlockSpec((D_MODEL, D_MODEL), lambda i: (0, 0)),
                  xspec,
                  pl.BlockSpec((1, D_MODEL), lambda i: (0, 0)),
                  pl.BlockSpec((D_MODEL, IN_WP), lambda i: (0, 0)),
                  pl.BlockSpec((IN_T_ROWS, D_MODEL), lambda i: (0, 0))],
        out_specs=[xspec] + out_specs,
        out_shape=[jax.ShapeDtypeStruct((m, D_MODEL), F32)] + out_shape,
        compiler_params=_params(("parallel",)),
        name="out_in_proj",
    )(*ys, w_out_b, x2d, norm_w.reshape(1, D_MODEL), w_r, w_t)


def _run_interleaved(*pipelines):
    live = list(pipelines)
    while live:
        for g in list(live):
            if next(g, _DONE) is _DONE:
                live.remove(g)


_DONE = object()


def _stage_scores(s_ref, slot, cols, s, keep=None):
    if keep is not None:
        s = jnp.where(keep, s, NEG_INF)
    s_ref[slot, :, cols] = s
    return jnp.max(s, axis=0, keepdims=True)


def _flash_update_t(s, s_max, c1, shift, vt_ext, m_ref, acc_ref, idx, cols):
    m_old = m_ref[idx, :, cols]
    m_new = jnp.maximum(m_old, c1 * s_max + shift)
    alpha = jnp.exp2(m_old - m_new)
    p = jnp.exp2(c1 * s() - (m_new - shift))
    acc_ref[idx, :, cols] = alpha * acc_ref[idx, :, cols] + _dot(vt_ext, p.astype(BF16))
    m_ref[idx, :, cols] = m_new


def _bf16_pieces(x, n=3):
    out = []
    for _ in range(n):
        p = float(np.asarray(x, np.float32).astype(BF16).astype(np.float32))
        out.append(p)
        x = x - p
    return out


def _key_pos_features(tk):
    lane = lax.broadcasted_iota(jnp.int32, (tk, LANES), 1)
    row = lax.broadcasted_iota(jnp.int32, (tk, LANES), 0)
    out = jnp.zeros((tk, LANES), jnp.int32)
    for d in range(-(-tk // POS_RADIX)):
        digit = jnp.bitwise_and(jnp.right_shift(row, 8 * d), POS_RADIX - 1)
        out = jnp.where((lane >= ALIBI_ROWS * d) & (lane < ALIBI_ROWS * (d + 1)), digit, out)
    return out.astype(F32).astype(BF16)


def _alibi_rows(beta, tq, rows=LANES, tk=POS_RADIX):
    row = lax.broadcasted_iota(jnp.int32, (rows, tq), 0)
    out = jnp.zeros((rows, tq), F32)
    for d in range(-(-tk // POS_RADIX)):
        for r, piece in enumerate(_bf16_pieces(beta * POS_RADIX ** d, ALIBI_ROWS)):
            out = jnp.where(row == ALIBI_ROWS * d + r, piece, out)
    return out.astype(BF16)


def _normalized(acc):
    return acc[0:HEAD_DIM] / jnp.maximum(acc[HEAD_DIM:HEAD_DIM + 1], 1e-30)


def _rel_t(tk, tq):
    return lax.broadcasted_iota(jnp.int32, (tk, tq), 1) - lax.broadcasted_iota(jnp.int32, (tk, tq), 0)


def _diff_phases(lam_ref, sw_ref, q_ref, k_ref, vt_ref, z_ref, o_ref, m_ref, acc_ref, wq_ref, s_ref,
                 *, tq, tk, lam_init):
    qi = pl.program_id(1)
    kpq = tq // tk
    scale = DIFF_QK_DIM ** -0.5
    c1 = scale * LOG2E
    m_ref[...] = jnp.full(m_ref.shape, NEG_INF, F32)
    acc_ref[...] = jnp.zeros(acc_ref.shape, F32)
    rel = _rel_t(tk, tq)

    qt = q_ref[...].astype(F32).T.astype(BF16)
    row = lax.broadcasted_iota(jnp.int32, (LANES, tq), 0)
    for j in range(2 * HEADS):
        g, r0 = divmod(j * DIFF_QK_DIM, LANES)
        qg = qt[g * LANES:(g + 1) * LANES]
        wq_ref[j, 0:LANES, :] = jnp.where((row >= r0) & (row < r0 + DIFF_QK_DIM), qg, jnp.zeros_like(qg))
        wq_ref[j, LANES:2 * LANES, :] = _alibi_rows(DIFF_SLOPES[j // 2] / scale, tq, tk=tk)

    ones = jnp.ones((BF16_ROWS, tk), BF16)
    kpos = _key_pos_features(tk)

    n_maps = 2 * HEADS
    n_slots = s_ref.shape[0]

    def tiles(kis, mask_offs):
        loaded = []
        for ki in kis:
            start = pl.multiple_of(ki * tk, tk)
            loaded.append((k_ref[pl.ds(start, tk), :],
                           vt_ref[:, pl.ds(start, tk)],
                           (ki * tk).astype(F32)))
        items = [(t, j) for t in range(len(kis)) for j in range(n_maps)]
        cols = slice(0, tq)

        def scores(i):
            t, j = items[i]
            g = (j * DIFF_QK_DIM) // LANES
            lhs = jnp.concatenate([loaded[t][0][:, g * LANES:(g + 1) * LANES], kpos], axis=1)
            s_ref[i % n_slots] = _dot(lhs, wq_ref[j])

        def staged(i):
            t = items[i][0]
            if mask_offs[t] is None:
                return lambda: s_ref[i % n_slots]
            return lambda: jnp.where(rel >= mask_offs[t], s_ref[i % n_slots], NEG_INF)

        for i in range(QK_LOOKAHEAD):
            scores(i)
        for i, (t, j) in enumerate(items):
            h = j // 2
            if i + QK_LOOKAHEAD < len(items):
                scores(i + QK_LOOKAHEAD)
            _k, vt, key0 = loaded[t]
            vt_ext = jnp.concatenate([vt[h * HEAD_DIM:(h + 1) * HEAD_DIM], ones], axis=0)
            s = staged(i)
            _flash_update_t(s, jnp.max(s(), axis=0, keepdims=True), c1, (DIFF_SLOPES[h] * LOG2E) * key0,
                            vt_ext, m_ref, acc_ref, j, cols)
            yield

    def loop_steps(i):
        return tiles([i * kpq + d for d in range(kpq)], [None] * kpq)

    def diag_steps():
        return tiles([qi * kpq + d for d in range(kpq)], [d * tk for d in range(kpq)])

    def finish():
        lp = lam_ref[...]
        lam = (jnp.exp(jnp.sum(lp[0:1] * lp[1:2], axis=-1, keepdims=True))
               - jnp.exp(jnp.sum(lp[2:3] * lp[3:4], axis=-1, keepdims=True)) + lam_init)
        sw = sw_ref[...]
        o_t = jnp.concatenate([_normalized(acc_ref[2 * h]) - lam * _normalized(acc_ref[2 * h + 1])
                               for h in range(HEADS)], axis=0)
        o = o_t.T
        outs = []
        for h in range(HEADS):
            oh = o[:, h * HEAD_DIM:(h + 1) * HEAD_DIM]
            ms = jnp.mean(oh * oh, axis=-1, keepdims=True)
            outs.append(oh * lax.rsqrt(ms + EPS) * sw * (1.0 - lam_init))
        y = jnp.concatenate(outs, axis=-1) * _silu(z_ref[...])
        o_ref[...] = y.astype(o_ref.dtype)

    return loop_steps, diag_steps, finish


def _diff_operands(qk, vt_all, z_all, lam_p, subln_w, seq, nq, tq, tk):
    inputs = [lam_p, subln_w.reshape(1, HEAD_DIM), qk, qk, vt_all, z_all]
    in_specs = [pl.BlockSpec((4, DIFF_QK_DIM), lambda b, i: (0, 0)),
                pl.BlockSpec((1, HEAD_DIM), lambda b, i: (0, 0)),
                pl.BlockSpec((tq, GROUP_W), lambda b, i: (b * nq + i, 0)),
                pl.BlockSpec((seq, GROUP_W), lambda b, i: (b, 1)),
                pl.BlockSpec((GROUP_W, seq), lambda b, i: (DIFF_VT_BLK, b)),
                pl.BlockSpec((tq, GROUP_W), lambda b, i: (b * nq + i, 1))]
    scratch = [pltpu.VMEM((2 * HEADS, 1, tq), F32),
               pltpu.VMEM((2 * HEADS, ACC_ROWS, tq), F32),
               pltpu.VMEM((2 * HEADS, 2 * LANES, tq), BF16),
               pltpu.VMEM((QK_LOOKAHEAD + 1, tk, tq), F32)]
    return inputs, in_specs, scratch


def _nsa_compress_kernel(cmp_ref, pe_ref, w1_ref, wk2_ref, wv2t_ref, kc_ref, vct_ref):
    ng = kc_ref.shape[1]
    top = jnp.zeros((ng, 2 * CMP_HIDDEN), F32)
    bot = jnp.zeros((ng, 2 * CMP_HIDDEN), F32)
    for p in range(CMP_STRIDE):
        x = cmp_ref[pl.ds(p, ng, stride=CMP_STRIDE), :]
        top = top + _dot((x + pe_ref[p:p + 1, :]).astype(BF16), w1_ref[p])
        bot = bot + _dot((x + pe_ref[CMP_STRIDE + p:CMP_STRIDE + p + 1, :]).astype(BF16), w1_ref[CMP_STRIDE + p])
    hid = _silu(top + pltpu.roll(bot, ng - 1, 0)).astype(BF16)
    hk, hv = hid[:, 0:CMP_HIDDEN], hid[:, CMP_HIDDEN:]
    kc = _dot(hk, wk2_ref[...])
    lane = lax.broadcasted_iota(jnp.int32, kc.shape, 1)
    blk = lax.broadcasted_iota(jnp.int32, kc.shape, 0).astype(F32)
    kc = jnp.where((lane >= HEAD_DIM) & (lane < HEAD_DIM + ALIBI_ROWS), blk, kc)
    kc_ref[0] = kc.astype(kc_ref.dtype)
    vct_ref[0] = _dot_nt(wv2t_ref[...], hv).astype(vct_ref.dtype)


def _nsa_compress(cmp2d, pe_k, pe_v, w_ck1, w_ck2, w_cv1, w_cv2, batch, seq):
    ng = seq // CMP_STRIDE
    wk = w_ck1.reshape(CMP_LEN, HEAD_DIM, CMP_HIDDEN)
    wv = w_cv1.reshape(CMP_LEN, HEAD_DIM, CMP_HIDDEN)
    w1 = jnp.concatenate([jnp.pad(wk, ((0, 0), (0, 0), (0, CMP_HIDDEN))),
                          jnp.pad(wv, ((0, 0), (0, 0), (CMP_HIDDEN, 0)))], axis=1).astype(BF16)
    pe = jnp.concatenate([pe_k, pe_v], axis=1)

    def full(shape):
        return pl.BlockSpec(shape, lambda b: (0,) * len(shape))

    return pl.pallas_call(
        _nsa_compress_kernel,
        grid=(batch,),
        in_specs=[pl.BlockSpec((seq, LANES), lambda b: (b, 0)), full((CMP_LEN, LANES)),
                  full((CMP_LEN, LANES, 2 * CMP_HIDDEN)), full((CMP_HIDDEN, LANES)),
                  full((HEAD_DIM, CMP_HIDDEN))],
        out_specs=[pl.BlockSpec((1, ng, LANES), lambda b: (b, 0, 0)),
                   pl.BlockSpec((1, HEAD_DIM, ng), lambda b: (b, 0, 0))],
        out_shape=[jax.ShapeDtypeStruct((batch, ng, LANES), BF16),
                   jax.ShapeDtypeStruct((batch, HEAD_DIM, ng), BF16)],
        compiler_params=_params(("parallel",)),
        name="nsa_compress",
    )(cmp2d, pe, w1, jnp.pad(w_ck2, ((0, 0), (0, LANES - HEAD_DIM))).astype(BF16), w_cv2.T.astype(BF16))


def _nsa_phases(ovl_ref, q_ref, k_ref, vt_ref, kc_ref, vct_ref, misc_ref, z_ref, o_ref,
                m_ref, acc_ref, wq_ref, s_ref, sc_ref, *, tq, tk, seq):
    qi = pl.program_id(1)
    kpq = tq // tk
    scale = HEAD_DIM ** -0.5
    c1 = scale * LOG2E
    ng = seq // CMP_STRIDE
    ns = seq // SLC_BLOCK
    top = min(SLC_TOPK, ns)
    m_ref[...] = jnp.full(m_ref.shape, NEG_INF, F32)
    acc_ref[...] = jnp.zeros(acc_ref.shape, F32)
    rel = _rel_t(tk, tq)

    qt = q_ref[...].astype(F32).T.astype(BF16)
    zeros_q = jnp.zeros((HEAD_DIM, tq), BF16)
    for h in range(HEADS):
        qh = qt[h * HEAD_DIM:(h + 1) * HEAD_DIM]
        alibi = _alibi_rows(NSA_SLOPES[h] / scale, tq, tk=tk)
        wq_ref[h, 0:LANES, :] = jnp.concatenate([qh, zeros_q], axis=0)
        wq_ref[h, LANES:2 * LANES, :] = alibi
        wq_ref[HEADS + h, 0:LANES, :] = jnp.concatenate([zeros_q, qh], axis=0)
        wq_ref[HEADS + h, LANES:2 * LANES, :] = alibi

    t_lane = qi * tq + lax.broadcasted_iota(jnp.int32, (1, tq), 1)
    n_sub = lax.broadcasted_iota(jnp.int32, (ng, 1), 0)
    c_valid = (n_sub * CMP_STRIDE + (CMP_LEN - 1)) <= t_lane
    kc = kc_ref[0]
    vct_ext = jnp.concatenate([vct_ref[0], jnp.ones((BF16_ROWS, ng), BF16)], axis=0)
    ovl = ovl_ref[...]
    cmp_max = []
    for h in range(HEADS):
        rhs = jnp.concatenate([qt[h * HEAD_DIM:(h + 1) * HEAD_DIM],
                               _alibi_rows(NSA_SLOPES[h] * CMP_STRIDE / scale, tq, HEAD_DIM)], axis=0)
        cmp_max.append(_stage_scores(sc_ref, h, slice(0, tq), _dot(kc, rhs), c_valid))
    imp = jnp.zeros((ns, tq), F32)
    o_cmp = []
    for h in range(HEADS):
        mx = jnp.where(cmp_max[h] > NEG_INF, cmp_max[h], 0.0)
        pb = jnp.exp2(c1 * sc_ref[h] - c1 * mx).astype(BF16)
        o_ext = _dot(vct_ext, pb)
        r = 1.0 / jnp.maximum(o_ext[HEAD_DIM:HEAD_DIM + 1], 1e-30)
        o_cmp.append(o_ext[0:HEAD_DIM] * r)
        imp = imp + _dot(ovl, pb) * r

    def select_blocks():
        j_sub = lax.broadcasted_iota(jnp.int32, (ns, 1), 0)
        j_sub_f = j_sub.astype(F32)
        cur = jnp.right_shift(t_lane, SLC_SHIFT)
        forced = (j_sub == 0) | (j_sub == cur) | (j_sub == cur - 1)
        valid = (j_sub * SLC_BLOCK) <= t_lane
        score = jnp.where(forced, 1e30, jnp.where(valid, imp, -1.0))
        sel = jnp.zeros((ns, tq), F32)
        for _ in range(top):
            mx = jnp.max(score, axis=0, keepdims=True)
            idx = jnp.min(jnp.where(score == mx, j_sub_f, float(ns)), axis=0, keepdims=True)
            pick = j_sub_f == idx
            sel = jnp.where(pick, 1.0, sel)
            score = jnp.where(pick, -2.0, score)
        unsel = jnp.where(sel > 0.5, 0.0, -MASK_BIG).astype(BF16)
        for h in range(HEADS):
            wq_ref[h, LANES + BLK_LANE0:LANES + BLK_LANE0 + ns, :] = unsel

    aux_lane = lax.broadcasted_iota(jnp.int32, (tk, LANES), 1)
    aux_blk = jnp.right_shift(lax.broadcasted_iota(jnp.int32, (tk, LANES), 0), SLC_SHIFT) + BLK_LANE0
    kpos = _key_pos_features(tk)
    ones = jnp.ones((BF16_ROWS, tk), BF16)

    n_slots = s_ref.shape[0]

    def tiles(window, kis, cols, mask_offs=None):
        base = HEADS if window else 0
        loaded = []
        for t, ki in enumerate(kis):
            start = pl.multiple_of(ki * tk, tk)
            k = k_ref[pl.ds(start, tk), :]
            if window:
                vt = vt_ref[HEAD_DIM:2 * HEAD_DIM, pl.ds(start, tk)]
                aux = kpos
                dist = rel[:, cols[t]] + (qi * tq - ki * tk)
                keep = (dist >= 0) & (dist < WINDOW)
            else:
                vt = vt_ref[0:HEAD_DIM, pl.ds(start, tk)]
                onehot = aux_lane == aux_blk + ki * (tk // SLC_BLOCK)
                aux = jnp.where(onehot, jnp.ones_like(kpos), kpos)
                keep = None
            loaded.append((jnp.concatenate([k, aux], axis=1),
                           jnp.concatenate([vt, ones], axis=0), (ki * tk).astype(F32), keep))
        items = [(t, h) for t in range(len(kis)) for h in range(HEADS)]
        s_max = {}

        def scores(i):
            t, h = items[i]
            if window:
                keep = loaded[t][3]
            else:
                keep = None if mask_offs[t] is None else rel[:, cols[t]] >= mask_offs[t]
            s_max[i] = _stage_scores(s_ref, i % n_slots, cols[t],
                                     _dot(loaded[t][0], wq_ref[base + h, :, cols[t]]), keep)

        for i in range(QK_LOOKAHEAD):
            scores(i)
        for i, (t, h) in enumerate(items):
            if i + QK_LOOKAHEAD < len(items):
                scores(i + QK_LOOKAHEAD)
            _lhs, vt_ext, key0, _keep = loaded[t]
            _flash_update_t(lambda i=i, t=t: s_ref[i % n_slots, :, cols[t]], s_max.pop(i), c1,
                            (NSA_SLOPES[h] * LOG2E) * key0, vt_ext, m_ref, acc_ref, base + h, cols[t])
            yield

    all_cols = slice(0, tq)
    diag_cols = [slice(d * tk, tq) for d in range(kpq)]

    _run_interleaved(tiles(True, [qi * kpq + d for d in range(kpq)], diag_cols))
    select_blocks()

    def loop_steps(i):
        return tiles(False, [i * kpq + d for d in range(kpq)], [all_cols] * kpq, [None] * kpq)

    def back_window():
        n_back = (WINDOW + tk - 1) // tk
        for g in range((n_back + kpq - 1) // kpq):
            backs = list(range(g * kpq + 1, min((g + 1) * kpq, n_back) + 1))
            back_cols = [slice(0, min(tq, -(-(WINDOW - (back - 1) * tk - 1) // LANES) * LANES)) for back in backs]

            @pl.when(qi > g)
            def _():
                _run_interleaved(tiles(True, [qi * kpq - back for back in backs], back_cols))

    def diag_steps():
        return tiles(False, [qi * kpq + d for d in range(kpq)], diag_cols, [d * tk for d in range(kpq)])

    def finish():
        g_t = jax.nn.sigmoid(misc_ref[...]).T
        outs = []
        for h in range(HEADS):
            r0 = GATE_COL + 3 * h
            outs.append(g_t[r0:r0 + 1] * o_cmp[h] + g_t[r0 + 1:r0 + 2] * _normalized(acc_ref[h])
                        + g_t[r0 + 2:r0 + 3] * _normalized(acc_ref[HEADS + h]))
        y = jnp.concatenate(outs, axis=0).T * _silu(z_ref[...])
        o_ref[...] = y.astype(o_ref.dtype)

    return loop_steps, back_window, diag_steps, finish


def _overlap_t(seq):
    nc = (seq - CMP_LEN) // CMP_STRIDE + 1
    ng = seq // CMP_STRIDE
    ns = seq // SLC_BLOCK
    c_start = np.arange(ng) * CMP_STRIDE
    c_end = c_start + CMP_LEN - 1
    s_start = np.arange(ns) * SLC_BLOCK
    s_end = s_start + SLC_BLOCK - 1
    ov = (c_start[None, :] <= s_end[:, None]) & (c_end[None, :] >= s_start[:, None]) & (np.arange(ng)[None, :] < nc)
    return jnp.asarray(ov.astype(np.float32), dtype=BF16)


def _nsa_operands(q, k2, vt_all, kc, vct, misc, z_all, seq, nq, tq, tk):
    ng = seq // CMP_STRIDE
    ns = seq // SLC_BLOCK
    assert tq - tk < WINDOW and tk % SLC_BLOCK == 0 and ns <= LANES - BLK_LANE0
    inputs = [_overlap_t(seq), q, k2, vt_all, kc, vct, misc, z_all]
    in_specs = [pl.BlockSpec((ns, ng), lambda b, i: (0, 0)),
                pl.BlockSpec((tq, GROUP_W), lambda b, i: (b * nq + i, 0)),
                pl.BlockSpec((seq, LANES), lambda b, i: (b, 0)),
                pl.BlockSpec((2 * HEAD_DIM, seq), lambda b, i: (NSA_VT_BLK, b)),
                pl.BlockSpec((1, ng, LANES), lambda b, i: (b, 0, 0)),
                pl.BlockSpec((1, HEAD_DIM, ng), lambda b, i: (b, 0, 0)),
                pl.BlockSpec((tq, LANES), lambda b, i: (b * nq + i, 0)),
                pl.BlockSpec((tq, GROUP_W), lambda b, i: (b * nq + i, 0))]
    scratch = [pltpu.VMEM((2 * HEADS, 1, tq), F32),
               pltpu.VMEM((2 * HEADS, ACC_ROWS, tq), F32),
               pltpu.VMEM((2 * HEADS, 2 * LANES, tq), BF16),
               pltpu.VMEM((2 * HEADS, tk, tq), F32),
               pltpu.VMEM((HEADS, ng, tq), F32)]
    return inputs, in_specs, scratch


def _mixers_kernel(*refs, n_in, n_scratch, tq, tk_nsa, tk_diff, seq, lam_init):
    bounds = np.cumsum([0] + list(n_in))
    nsa_in, diff_in, ret_in, ssd_in = (refs[a:b] for a, b in zip(bounds[:-1], bounds[1:]))
    nsa_out, diff_out, ret_out, ssd_out = refs[bounds[-1]:bounds[-1] + 4]
    sb = np.cumsum([0] + list(n_scratch)) + bounds[-1] + 4
    nsa_scr, diff_scr, rec_scr = (refs[a:b] for a, b in zip(sb[:-1], sb[1:]))
    ret_st, ssd_ext, ssd_st = rec_scr
    qi = pl.program_id(1)

    @pl.when(qi == 0)
    def _():
        ret_st[...] = jnp.zeros(ret_st.shape, F32)
        ssd_st[...] = jnp.zeros(ssd_st.shape, F32)
        ssd_ext[0:8, :] = jnp.zeros((8, CONV_CH), F32)

    chunks = list(range(ret_out.shape[0]))
    first, second = chunks[:len(chunks) // 2], chunks[len(chunks) // 2:]
    nsa_loop, nsa_back_window, nsa_diag, nsa_finish = _nsa_phases(
        *nsa_in, nsa_out, *nsa_scr, tq=tq, tk=tk_nsa, seq=seq)
    diff_loop, diff_diag, diff_finish = _diff_phases(
        *diff_in, diff_out, *diff_scr, tq=tq, tk=tk_diff, lam_init=lam_init)
    _ret_body(*ret_in, ret_out, ret_st, first)
    _ssd_body(*ssd_in, ssd_out, ssd_ext, ssd_st, first)

    def body(i, carry):
        _run_interleaved(nsa_loop(i), diff_loop(i))
        return carry

    lax.fori_loop(0, qi, body, 0)
    nsa_back_window()
    _run_interleaved(nsa_diag(), diff_diag())
    _ret_body(*ret_in, ret_out, ret_st, second)
    _ssd_body(*ssd_in, ssd_out, ssd_ext, ssd_st, second)
    nsa_finish()
    diff_finish()


def _mixers(nsa_args, diff_args, rec_args, batch, seq, layer_idx, tq, tk_nsa, tk_diff):
    nq = seq // tq
    nsa_in, nsa_specs, nsa_scratch = _nsa_operands(*nsa_args, seq, nq, tq, tk_nsa)
    diff_in, diff_specs, diff_scratch = _diff_operands(*diff_args, seq, nq, tq, tk_diff)
    ret_in, ssd_in, rec_specs, rec_out_spec, rec_scratch = _recurrent_operands(*rec_args, nq, tq)
    out_spec = pl.BlockSpec((tq, GROUP_W), lambda b, i: (b * nq + i, 0))
    m = batch * seq
    y_nsa, y_diff, y_ret, y_ssm = pl.pallas_call(
        functools.partial(_mixers_kernel, n_in=(len(nsa_in), len(diff_in), len(ret_in), len(ssd_in)),
                          n_scratch=(len(nsa_scratch), len(diff_scratch), len(rec_scratch)),
                          tq=tq, tk_nsa=tk_nsa, tk_diff=tk_diff, seq=seq,
                          lam_init=0.8 - 0.6 * math.exp(-0.3 * layer_idx)),
        grid=(batch, nq),
        in_specs=nsa_specs + diff_specs + rec_specs,
        out_specs=[out_spec, out_spec, rec_out_spec, rec_out_spec],
        out_shape=[jax.ShapeDtypeStruct((m, GROUP_W), BF16)] * 2
                  + [jax.ShapeDtypeStruct((m // SSM_CHUNK, SSM_CHUNK, GROUP_W), BF16)] * 2,
        scratch_shapes=nsa_scratch + diff_scratch + rec_scratch,
        compiler_params=_params(("parallel", "arbitrary")),
        name="mixers",
    )(*nsa_in, *diff_in, *ret_in, *ssd_in)
    return y_nsa, y_diff, y_ret.reshape(m, GROUP_W), y_ssm.reshape(m, GROUP_W)


def _ret_tables():
    c = RET_CHUNK
    h = np.arange(HEADS, dtype=np.float32)
    log_g = jnp.log(1.0 - 2.0 ** (-5.0 - jnp.asarray(h)))
    pos = jnp.arange(c, dtype=F32)
    rel = pos[:, None] - pos[None, :]
    decay = jnp.where(rel >= 0, jnp.exp(log_g[:, None, None] * jnp.maximum(rel, 0.0)), 0.0)
    xi = jnp.exp(log_g[:, None] * (pos + 1.0))
    zeta = jnp.exp(log_g[:, None] * (c - 1.0 - pos))
    chunk_decay = jnp.exp(log_g * c)
    xi_tab = jnp.repeat(xi.T, HEAD_DIM, axis=1)
    zeta_tab = jnp.repeat(zeta.T, HEAD_DIM, axis=1)
    cd_tab = jnp.repeat(chunk_decay, HEAD_DIM)[None, :]
    return decay, xi_tab, zeta_tab, cd_tab


def _ret_body(decay_ref, xi_ref, zeta_ref, cd_ref, gn_ref, q_ref, k_ref, v_ref, z_ref, o_ref, st_ref, blocks):
    xi = xi_ref[...]
    cd = cd_ref[...]
    for bb in blocks:
        q = (q_ref[bb].astype(F32) * (HEAD_DIM ** -0.5)).astype(BF16)
        k = k_ref[bb]
        v = v_ref[bb]
        kz_t = (k.astype(F32) * zeta_ref[...]).T.astype(BF16)
        outs = []
        for h in range(HEADS):
            sl = slice(h * HEAD_DIM, (h + 1) * HEAD_DIM)
            qh, kh, vh = q[:, sl], k[:, sl], v[:, sl]
            prev = st_ref[h]
            inner = (_dot_nt(qh, kh) * decay_ref[h]).astype(BF16)
            o = _dot(inner, vh) + _dot(qh, prev.astype(BF16)) * xi[:, sl]
            st_ref[h] = prev * cd[:, sl] + _dot(kz_t[sl, :], vh)
            mu = jnp.mean(o, axis=-1, keepdims=True)
            d = o - mu
            var = jnp.mean(d * d, axis=-1, keepdims=True)
            outs.append(d * lax.rsqrt(var + EPS))
        y = jnp.concatenate(outs, axis=-1) * gn_ref[...] * _silu(z_ref[bb])
        o_ref[bb] = y.astype(o_ref.dtype)


def _ssd_body(cw_ref, cb_ref, dtb_ref, a_ref, dsk_ref, nw_ref, xbc_ref, misc_ref, z_ref, o_ref,
              ext_ref, st_ref, blocks):
    L = SSM_CHUNK
    hi = lax.Precision.HIGHEST
    row = lax.broadcasted_iota(jnp.int32, (L, L), 0)
    col = lax.broadcasted_iota(jnp.int32, (L, L), 1)
    causal = row >= col
    tril = jnp.where(causal, 1.0, 0.0).astype(F32)
    dsk = dsk_ref[...]

    for bb in blocks:
        raw = xbc_ref[bb]
        ext_ref[8:8 + L, :] = raw
        conv = cb_ref[...] + raw * cw_ref[CONV_W - 1:CONV_W, :]
        for w in range(CONV_W - 1):
            shift = CONV_W - 1 - w
            conv = conv + ext_ref[8 - shift:8 - shift + L, :] * cw_ref[w:w + 1, :]
        ext_ref[0:8, :] = raw[L - 8:L, :]
        xc = _silu(conv)
        x = xc[:, 0:GROUP_W]
        bm = xc[:, GROUP_W:GROUP_W + 2 * SSM_STATE].astype(BF16)
        cm = xc[:, GROUP_W + 2 * SSM_STATE:].astype(BF16)

        dt_full = jax.nn.softplus(misc_ref[bb] + dtb_ref[...])
        da = dt_full * a_ref[...]
        cs_col = jnp.dot(tril, da, precision=hi, preferred_element_type=F32)
        cs_row = lax.dot_general(da, tril, (((0,), (1,)), ((), ())), precision=hi,
                                 preferred_element_type=F32)

        outs = []
        for h in range(HEADS):
            g = h // 2
            c0 = DT_COL + h
            sl = slice(h * HEAD_DIM, (h + 1) * HEAD_DIM)
            gs = slice(g * SSM_STATE, (g + 1) * SSM_STATE)
            cs_c = cs_col[:, c0:c0 + 1]
            cs_r = cs_row[c0:c0 + 1, :]
            cs_last = cs_col[L - 1:L, c0:c0 + 1]
            xh = x[:, sl]
            xdt = xh * dt_full[:, c0:c0 + 1]
            seg = jnp.exp(jnp.where(causal, cs_c - cs_r, NEG_INF))
            cb = _dot_nt(cm[:, gs], bm[:, gs])
            y = _dot((cb * seg).astype(BF16), xdt.astype(BF16))
            prev = st_ref[h]
            y = y + _dot(cm[:, gs], prev.astype(BF16)) * jnp.exp(cs_c)
            y = y + dsk[:, sl] * xh
            dec = jnp.exp(cs_last - cs_c)
            st_ref[h] = prev * jnp.exp(cs_last) + _dot_tn(bm[:, gs], (xdt * dec).astype(BF16))
            outs.append(y)
        y = jnp.concatenate(outs, axis=-1) * _silu(z_ref[bb])
        ms = jnp.mean(y * y, axis=-1, keepdims=True)
        o_ref[bb] = (y * lax.rsqrt(ms + EPS) * nw_ref[...]).astype(o_ref.dtype)


def _recurrent_operands(ret_qkv, xbc, misc, z_all, gn_w, conv_w, conv_b, dt_bias, a_log, d_skip, norm_w, nq, tq):
    L = SSM_CHUNK
    assert RET_CHUNK == L and tq % L == 0
    cpt = tq // L
    decay, xi_tab, zeta_tab, cd_tab = _ret_tables()
    dtb = jnp.zeros((1, 128), F32).at[0, DT_COL:DT_COL + HEADS].set(dt_bias)
    a_full = jnp.zeros((1, 128), F32).at[0, DT_COL:DT_COL + HEADS].set(-jnp.exp(a_log))
    dsk = jnp.repeat(d_skip, HEAD_DIM)[None, :]

    def full(shape):
        return pl.BlockSpec(shape, lambda b, i: (0,) * len(shape))

    def blk(width, col):
        return pl.BlockSpec((cpt, L, width), lambda b, i: (b * nq + i, 0, col))

    def chunked(a):
        return a.reshape(-1, L, a.shape[-1])

    qkv3, z3 = chunked(ret_qkv), chunked(z_all)
    ret_in = [decay, xi_tab, zeta_tab, cd_tab, gn_w.reshape(1, GROUP_W), qkv3, qkv3, qkv3, z3]
    ret_specs = [full((HEADS, L, L)), full((L, GROUP_W)), full((L, GROUP_W)), full((1, GROUP_W)),
                 full((1, GROUP_W)), blk(GROUP_W, 0), blk(GROUP_W, 1), blk(GROUP_W, 2), blk(GROUP_W, 2)]
    ssd_in = [conv_w, conv_b.reshape(1, CONV_CH), dtb, a_full, dsk, norm_w.reshape(1, GROUP_W),
              chunked(xbc), chunked(misc), z3]
    ssd_specs = [full((CONV_W, CONV_CH)), full((1, CONV_CH)), full((1, 128)), full((1, 128)),
                 full((1, GROUP_W)), full((1, GROUP_W)), blk(CONV_CH, 0), blk(128, 0), blk(GROUP_W, 3)]
    scratch = [pltpu.VMEM((HEADS, HEAD_DIM, HEAD_DIM), F32),
               pltpu.VMEM((8 + L, CONV_CH), F32),
               pltpu.VMEM((HEADS, SSM_STATE, HEAD_DIM), F32)]
    return ret_in, ssd_in, ret_specs + ssd_specs, blk(GROUP_W, 0), scratch


def _pick_tile(n, pref):
    t = pref
    while n % t:
        t //= 2
    return t


def kernel(x, norm_w, w_in, w_out, nsa_pe_k, nsa_pe_v, nsa_w_ck1, nsa_w_ck2, nsa_w_cv1, nsa_w_cv2,
           diff_lam_q1, diff_lam_k1, diff_lam_q2, diff_lam_k2, diff_subln_w, ret_gn_w,
           ssm_conv_w, ssm_conv_b, ssm_dt_bias, ssm_A_log, ssm_D, ssm_norm_w, final_norm_w):
    batch, seq, _ = x.shape
    depth = w_in.shape[0]
    m = batch * seq
    tm = _pick_tile(m, 512)
    tq = _pick_tile(seq, 512)
    tk = _pick_tile(seq, 256)
    tk_diff = _pick_tile(seq, 512)
    w_in_b = w_in.astype(BF16)
    w_r = _relayout_w_in(w_in_b)
    w_t = _relayout_w_in_t(w_in_b)
    w_out_b = w_out.astype(BF16)
    x2d = x.reshape(m, D_MODEL)
    projected = _in_proj(x2d, norm_w[0], w_r[0], w_t[0], tm)
    for i in range(depth):
        nsa_q, nsa_k2, nsa_cmp, misc, z_all, diff_qk, ret_qkv, xbc, vt_all = projected
        kc, vct = _nsa_compress(nsa_cmp, nsa_pe_k[i], nsa_pe_v[i], nsa_w_ck1[i], nsa_w_ck2[i],
                                nsa_w_cv1[i], nsa_w_cv2[i], batch, seq)
        lam_p = jnp.stack([diff_lam_q1[i], diff_lam_k1[i], diff_lam_q2[i], diff_lam_k2[i]])
        ys = _mixers((nsa_q, nsa_k2, vt_all, kc, vct, misc, z_all),
                     (diff_qk, vt_all, z_all, lam_p, diff_subln_w[i]),
                     (ret_qkv, xbc, misc, z_all, ret_gn_w[i], ssm_conv_w[i], ssm_conv_b[i], ssm_dt_bias[i],
                      ssm_A_log[i], ssm_D[i], ssm_norm_w[i]),
                     batch, seq, i, tq, tk, tk_diff)
        if i + 1 < depth:
            x2d, *projected = _out_in_proj(ys, w_out_b[i], x2d, norm_w[i + 1], w_r[i + 1], w_t[i + 1], tm)
        else:
            x2d = _out_proj(ys, w_out_b[i], x2d, final_norm_w, tm)
    return x2d.reshape(batch, seq, D_MODEL)
```

```python
import functools
import math

import numpy as np
import jax
import jax.numpy as jnp
from jax import lax
from jax.experimental import pallas as pl
from jax.experimental.pallas import tpu as pltpu

F32 = jnp.float32
BF16 = jnp.bfloat16
NEG_INF = float("-inf")
LOG2E = 1.4426950408889634

D_MODEL = 1024
DEPTH = 4
GROUP_W = 256
HEADS = 4
HEAD_DIM = 64
EPS = 1e-6
CMP_LEN = 32
CMP_STRIDE = 16
CMP_HIDDEN = 256
SLC_BLOCK = 64
SLC_SHIFT = 6
SLC_TOPK = 16
WINDOW = 512
DIFF_QK_DIM = 32
RET_CHUNK = 128
SSM_STATE = 128
SSM_CHUNK = 128
CONV_W = 4
CONV_CH = 768
N_ALIBI_HEADS = 8
LANES = 128
BF16_ROWS = 16
ACC_ROWS = HEAD_DIM + BF16_ROWS
QK_LOOKAHEAD = 4
ALIBI_ROWS = 3
POS_RADIX = 256
MASK_BIG = 2.0 ** 100
BLK_LANE0 = 64

IN_LAYOUT = (
    ("nsa_q", 256), ("nsa_k_cmp", 64), ("nsa_v_cmp", 64), ("nsa_k_slc", 64), ("nsa_v_slc", 64),
    ("nsa_k_win", 64), ("nsa_v_win", 64), ("nsa_gate", 12), ("nsa_z", 256),
    ("diff_q", 256), ("diff_k", 256), ("diff_v", 256), ("diff_z", 256),
    ("ret_q", 256), ("ret_k", 256), ("ret_v", 256), ("ret_z", 256),
    ("ssm_z", 256), ("ssm_xbc", 768), ("ssm_dt", 4),
)
IN_OFF = {}
_o = 0
for _n, _w in IN_LAYOUT:
    IN_OFF[_n] = (_o, _w)
    _o += _w
IN_W = _o

GATE_COL = 0
DT_COL = 12
IN_OUTPUTS = (
    ("nsa_q", BF16, ("nsa_q",), 256),
    ("nsa_k2", BF16, ("nsa_k_slc", "nsa_k_win"), 128),
    ("nsa_cmp", F32, ("nsa_k_cmp", "nsa_v_cmp"), 128),
    ("misc", F32, ("nsa_gate", "ssm_dt"), 128),
    ("z_all", F32, ("nsa_z", "diff_z", "ret_z", "ssm_z"), 1024),
    ("diff_qk", BF16, ("diff_q", "diff_k"), 512),
    ("ret_qkv", BF16, ("ret_q", "ret_k", "ret_v"), 768),
    ("xbc", F32, ("ssm_xbc",), 768),
)
IN_T_SRC = ("diff_v", "nsa_v_slc", "nsa_v_win")
IN_T_ROWS = 384
DIFF_VT_BLK = 0
NSA_VT_BLK = 2
IN_SEGS = []
_o = 0
for _n, _dt, _src, _w in IN_OUTPUTS:
    IN_SEGS.append((_o, _o + _w))
    _o += _w
IN_WP = _o

VMEM_LIMIT = 56 * 1024 * 1024


def _alibi_slopes():
    return [2.0 ** (-8.0 * (i + 1) / N_ALIBI_HEADS) for i in range(N_ALIBI_HEADS)]


NSA_SLOPES = _alibi_slopes()[0::2]
DIFF_SLOPES = _alibi_slopes()[1::2]


def _silu(x):
    return x * jax.nn.sigmoid(x)


def _dot(a, b):
    return jnp.dot(a, b, preferred_element_type=F32)


def _dot_nt(a, b):
    return lax.dot_general(a, b, (((1,), (1,)), ((), ())), preferred_element_type=F32)


def _dot_tn(a, b):
    return lax.dot_general(a, b, (((0,), (0,)), ((), ())), preferred_element_type=F32)


def _params(sem):
    return pltpu.CompilerParams(dimension_semantics=sem, vmem_limit_bytes=VMEM_LIMIT)


def _relayout_w_in(w_in):
    cols = []
    for _n, _dt, src, width in IN_OUTPUTS:
        used = 0
        for s in src:
            off, w = IN_OFF[s]
            cols.append(w_in[:, :, off:off + w])
            used += w
        if used < width:
            cols.append(jnp.zeros(w_in.shape[:2] + (width - used,), w_in.dtype))
    return jnp.concatenate(cols, axis=-1)


def _relayout_w_in_t(w_in):
    cols = [w_in[:, :, IN_OFF[s][0]:IN_OFF[s][0] + IN_OFF[s][1]] for s in IN_T_SRC]
    return jnp.swapaxes(jnp.concatenate(cols, axis=-1), 1, 2)


def _norm_project(x, nw_ref, w_ref, wt_ref, out_refs):
    ms = jnp.mean(x * x, axis=-1, keepdims=True)
    h = (x * lax.rsqrt(ms + EPS) * nw_ref[...]).astype(BF16)
    for ref, (a, b) in zip(out_refs[:-1], IN_SEGS):
        ref[...] = _dot(h, w_ref[:, a:b]).astype(ref.dtype)
    out_refs[-1][...] = _dot_nt(wt_ref[...], h).astype(BF16)


def _in_proj_kernel(x_ref, nw_ref, w_ref, wt_ref, *out_refs):
    _norm_project(x_ref[...], nw_ref, w_ref, wt_ref, out_refs)


def _in_proj_outputs(m, tm):
    out_shape = [jax.ShapeDtypeStruct((m, b - a), dt) for (_n, dt, _s, _w), (a, b) in zip(IN_OUTPUTS, IN_SEGS)]
    out_specs = [pl.BlockSpec((tm, b - a), lambda i: (i, 0)) for (a, b) in IN_SEGS]
    out_shape.append(jax.ShapeDtypeStruct((IN_T_ROWS, m), BF16))
    out_specs.append(pl.BlockSpec((IN_T_ROWS, tm), lambda i: (0, i)))
    return out_shape, out_specs


def _in_proj(x2d, norm_w, w_r, w_t, tm):
    m = x2d.shape[0]
    out_shape, out_specs = _in_proj_outputs(m, tm)
    return pl.pallas_call(
        _in_proj_kernel,
        grid=(m // tm,),
        in_specs=[pl.BlockSpec((tm, D_MODEL), lambda i: (i, 0)),
                  pl.BlockSpec((1, D_MODEL), lambda i: (0, 0)),
                  pl.BlockSpec((D_MODEL, IN_WP), lambda i: (0, 0)),
                  pl.BlockSpec((IN_T_ROWS, D_MODEL), lambda i: (0, 0))],
        out_specs=out_specs,
        out_shape=out_shape,
        compiler_params=_params(("parallel",)),
        name="in_proj",
    )(x2d, norm_w.reshape(1, D_MODEL), w_r, w_t)


def _out_proj_kernel(y0_ref, y1_ref, y2_ref, y3_ref, w_ref, x_ref, fw_ref, o_ref):
    acc = x_ref[...]
    for g, y_ref in enumerate((y0_ref, y1_ref, y2_ref, y3_ref)):
        acc = acc + _dot(y_ref[...], w_ref[g * GROUP_W:(g + 1) * GROUP_W, :])
    ms = jnp.mean(acc * acc, axis=-1, keepdims=True)
    o_ref[...] = acc * lax.rsqrt(ms + EPS) * fw_ref[...]


def _out_proj(ys, w_out_b, x2d, final_w, tm):
    m = x2d.shape[0]
    yspec = pl.BlockSpec((tm, GROUP_W), lambda i: (i, 0))
    return pl.pallas_call(
        _out_proj_kernel,
        grid=(m // tm,),
        in_specs=[yspec, yspec, yspec, yspec,
                  pl.BlockSpec((D_MODEL, D_MODEL), lambda i: (0, 0)),
                  pl.BlockSpec((tm, D_MODEL), lambda i: (i, 0)),
                  pl.BlockSpec((1, D_MODEL), lambda i: (0, 0))],
        out_specs=pl.BlockSpec((tm, D_MODEL), lambda i: (i, 0)),
        out_shape=jax.ShapeDtypeStruct((m, D_MODEL), F32),
        compiler_params=_params(("parallel",)),
        name="out_proj",
    )(*ys, w_out_b, x2d, final_w.reshape(1, D_MODEL))


def _out_in_proj_kernel(y0_ref, y1_ref, y2_ref, y3_ref, wo_ref, x_ref, nw_ref, w_ref, wt_ref, xo_ref, *out_refs):
    acc = x_ref[...]
    for g, y_ref in enumerate((y0_ref, y1_ref, y2_ref, y3_ref)):
        acc = acc + _dot(y_ref[...], wo_ref[g * GROUP_W:(g + 1) * GROUP_W, :])
    xo_ref[...] = acc
    _norm_project(acc, nw_ref, w_ref, wt_ref, out_refs)


def _out_in_proj(ys, w_out_b, x2d, norm_w, w_r, w_t, tm):
    m = x2d.shape[0]
    yspec = pl.BlockSpec((tm, GROUP_W), lambda i: (i, 0))
    xspec = pl.BlockSpec((tm, D_MODEL), lambda i: (i, 0))
    out_shape, out_specs = _in_proj_outputs(m, tm)
    return pl.pallas_call(
        _out_in_proj_kernel,
        grid=(m // tm,),
        in_specs=[yspec, yspec, yspec, yspec,
                  pl.BlockSpec((D_MODEL, D_MODEL), lambda i: (0, 0)),
                  xspec,
                  pl.BlockSpec((1, D_MODEL), lambda i: (0, 0)),
                  pl.BlockSpec((D_MODEL, IN_WP), lambda i: (0, 0)),
                  pl.BlockSpec((IN_T_ROWS, D_MODEL), lambda i: (0, 0))],
        out_specs=[xspec] + out_specs,
        out_shape=[jax.ShapeDtypeStruct((m, D_MODEL), F32)] + out_shape,
        compiler_params=_params(("parallel",)),
        name="out_in_proj",
    )(*ys, w_out_b, x2d, norm_w.reshape(1, D_MODEL), w_r, w_t)


def _run_interleaved(*pipelines):
    live = list(pipelines)
    while live:
        for g in list(live):
            if next(g, _DONE) is _DONE:
                live.remove(g)


_DONE = object()


def _stage_scores(s_ref, slot, cols, s, keep=None):
    if keep is not None:
        s = jnp.where(keep, s, NEG_INF)
    s_ref[slot, :, cols] = s
    return jnp.max(s, axis=0, keepdims=True)


def _flash_update_t(s, s_max, c1, shift, vt_ext, m_ref, acc_ref, idx, cols):
    m_old = m_ref[idx, :, cols]
    m_new = jnp.maximum(m_old, c1 * s_max + shift)
    alpha = jnp.exp2(m_old - m_new)
    p = jnp.exp2(c1 * s() - (m_new - shift))
    acc_ref[idx, :, cols] = alpha * acc_ref[idx, :, cols] + _dot(vt_ext, p.astype(BF16))
    m_ref[idx, :, cols] = m_new


def _bf16_pieces(x, n=3):
    out = []
    for _ in range(n):
        p = float(np.asarray(x, np.float32).astype(BF16).astype(np.float32))
        out.append(p)
        x = x - p
    return out


def _key_pos_features(tk):
    lane = lax.broadcasted_iota(jnp.int32, (tk, LANES), 1)
    row = lax.broadcasted_iota(jnp.int32, (tk, LANES), 0)
    out = jnp.zeros((tk, LANES), jnp.int32)
    for d in range(-(-tk // POS_RADIX)):
        digit = jnp.bitwise_and(jnp.right_shift(row, 8 * d), POS_RADIX - 1)
        out = jnp.where((lane >= ALIBI_ROWS * d) & (lane < ALIBI_ROWS * (d + 1)), digit, out)
    return out.astype(F32).astype(BF16)


def _alibi_rows(beta, tq, rows=LANES, tk=POS_RADIX):
    row = lax.broadcasted_iota(jnp.int32, (rows, tq), 0)
    out = jnp.zeros((rows, tq), F32)
    for d in range(-(-tk // POS_RADIX)):
        for r, piece in enumerate(_bf16_pieces(beta * POS_RADIX ** d, ALIBI_ROWS)):
            out = jnp.where(row == ALIBI_ROWS * d + r, piece, out)
    return out.astype(BF16)


def _normalized(acc):
    return acc[0:HEAD_DIM] / jnp.maximum(acc[HEAD_DIM:HEAD_DIM + 1], 1e-30)


def _rel_t(tk, tq):
    return lax.broadcasted_iota(jnp.int32, (tk, tq), 1) - lax.broadcasted_iota(jnp.int32, (tk, tq), 0)


def _diff_phases(lam_ref, sw_ref, q_ref, k_ref, vt_ref, z_ref, o_ref, m_ref, acc_ref, wq_ref, s_ref,
                 *, tq, tk, lam_init):
    qi = pl.program_id(1)
    kpq = tq // tk
    scale = DIFF_QK_DIM ** -0.5
    c1 = scale * LOG2E
    m_ref[...] = jnp.full(m_ref.shape, NEG_INF, F32)
    acc_ref[...] = jnp.zeros(acc_ref.shape, F32)
    rel = _rel_t(tk, tq)

    qt = q_ref[...].astype(F32).T.astype(BF16)
    row = lax.broadcasted_iota(jnp.int32, (LANES, tq), 0)
    for j in range(2 * HEADS):
        g, r0 = divmod(j * DIFF_QK_DIM, LANES)
        qg = qt[g * LANES:(g + 1) * LANES]
        wq_ref[j, 0:LANES, :] = jnp.where((row >= r0) & (row < r0 + DIFF_QK_DIM), qg, jnp.zeros_like(qg))
        wq_ref[j, LANES:2 * LANES, :] = _alibi_rows(DIFF_SLOPES[j // 2] / scale, tq, tk=tk)

    ones = jnp.ones((BF16_ROWS, tk), BF16)
    kpos = _key_pos_features(tk)

    n_maps = 2 * HEADS
    n_slots = s_ref.shape[0]

    def tiles(kis, mask_offs):
        loaded = []
        for ki in kis:
            start = pl.multiple_of(ki * tk, tk)
            loaded.append((k_ref[pl.ds(start, tk), :],
                           vt_ref[:, pl.ds(start, tk)],
                           (ki * tk).astype(F32)))
        items = [(t, j) for t in range(len(kis)) for j in range(n_maps)]
        cols = slice(0, tq)

        def scores(i):
            t, j = items[i]
            g = (j * DIFF_QK_DIM) // LANES
            lhs = jnp.concatenate([loaded[t][0][:, g * LANES:(g + 1) * LANES], kpos], axis=1)
            s_ref[i % n_slots] = _dot(lhs, wq_ref[j])

        def staged(i):
            t = items[i][0]
            if mask_offs[t] is None:
                return lambda: s_ref[i % n_slots]
            return lambda: jnp.where(rel >= mask_offs[t], s_ref[i % n_slots], NEG_INF)

        for i in range(QK_LOOKAHEAD):
            scores(i)
        for i, (t, j) in enumerate(items):
            h = j // 2
            if i + QK_LOOKAHEAD < len(items):
                scores(i + QK_LOOKAHEAD)
            _k, vt, key0 = loaded[t]
            vt_ext = jnp.concatenate([vt[h * HEAD_DIM:(h + 1) * HEAD_DIM], ones], axis=0)
            s = staged(i)
            _flash_update_t(s, jnp.max(s(), axis=0, keepdims=True), c1, (DIFF_SLOPES[h] * LOG2E) * key0,
                            vt_ext, m_ref, acc_ref, j, cols)
            yield

    def loop_steps(i):
        return tiles([i * kpq + d for d in range(kpq)], [None] * kpq)

    def diag_steps():
        return tiles([qi * kpq + d for d in range(kpq)], [d * tk for d in range(kpq)])

    def finish():
        lp = lam_ref[...]
        lam = (jnp.exp(jnp.sum(lp[0:1] * lp[1:2], axis=-1, keepdims=True))
               - jnp.exp(jnp.sum(lp[2:3] * lp[3:4], axis=-1, keepdims=True)) + lam_init)
        sw = sw_ref[...]
        o_t = jnp.concatenate([_normalized(acc_ref[2 * h]) - lam * _normalized(acc_ref[2 * h + 1])
                               for h in range(HEADS)], axis=0)
        o = o_t.T
        outs = []
        for h in range(HEADS):
            oh = o[:, h * HEAD_DIM:(h + 1) * HEAD_DIM]
            ms = jnp.mean(oh * oh, axis=-1, keepdims=True)
            outs.append(oh * lax.rsqrt(ms + EPS) * sw * (1.0 - lam_init))
        y = jnp.concatenate(outs, axis=-1) * _silu(z_ref[...])
        o_ref[...] = y.astype(o_ref.dtype)

    return loop_steps, diag_steps, finish


def _diff_operands(qk, vt_all, z_all, lam_p, subln_w, seq, nq, tq, tk):
    inputs = [lam_p, subln_w.reshape(1, HEAD_DIM), qk, qk, vt_all, z_all]
    in_specs = [pl.BlockSpec((4, DIFF_QK_DIM), lambda b, i: (0, 0)),
                pl.BlockSpec((1, HEAD_DIM), lambda b, i: (0, 0)),
                pl.BlockSpec((tq, GROUP_W), lambda b, i: (b * nq + i, 0)),
                pl.BlockSpec((seq, GROUP_W), lambda b, i: (b, 1)),
                pl.BlockSpec((GROUP_W, seq), lambda b, i: (DIFF_VT_BLK, b)),
                pl.BlockSpec((tq, GROUP_W), lambda b, i: (b * nq + i, 1))]
    scratch = [pltpu.VMEM((2 * HEADS, 1, tq), F32),
               pltpu.VMEM((2 * HEADS, ACC_ROWS, tq), F32),
               pltpu.VMEM((2 * HEADS, 2 * LANES, tq), BF16),
               pltpu.VMEM((QK_LOOKAHEAD + 1, tk, tq), F32)]
    return inputs, in_specs, scratch


def _nsa_compress_kernel(cmp_ref, pe_ref, w1_ref, wk2_ref, wv2t_ref, kc_ref, vct_ref):
    ng = kc_ref.shape[1]
    top = jnp.zeros((ng, 2 * CMP_HIDDEN), F32)
    bot = jnp.zeros((ng, 2 * CMP_HIDDEN), F32)
    for p in range(CMP_STRIDE):
        x = cmp_ref[pl.ds(p, ng, stride=CMP_STRIDE), :]
        top = top + _dot((x + pe_ref[p:p + 1, :]).astype(BF16), w1_ref[p])
        bot = bot + _dot((x + pe_ref[CMP_STRIDE + p:CMP_STRIDE + p + 1, :]).astype(BF16), w1_ref[CMP_STRIDE + p])
    hid = _silu(top + pltpu.roll(bot, ng - 1, 0)).astype(BF16)
    hk, hv = hid[:, 0:CMP_HIDDEN], hid[:, CMP_HIDDEN:]
    kc = _dot(hk, wk2_ref[...])
    lane = lax.broadcasted_iota(jnp.int32, kc.shape, 1)
    blk = lax.broadcasted_iota(jnp.int32, kc.shape, 0).astype(F32)
    kc = jnp.where((lane >= HEAD_DIM) & (lane < HEAD_DIM + ALIBI_ROWS), blk, kc)
    kc_ref[0] = kc.astype(kc_ref.dtype)
    vct_ref[0] = _dot_nt(wv2t_ref[...], hv).astype(vct_ref.dtype)


def _nsa_compress(cmp2d, pe_k, pe_v, w_ck1, w_ck2, w_cv1, w_cv2, batch, seq):
    ng = seq // CMP_STRIDE
    wk = w_ck1.reshape(CMP_LEN, HEAD_DIM, CMP_HIDDEN)
    wv = w_cv1.reshape(CMP_LEN, HEAD_DIM, CMP_HIDDEN)
    w1 = jnp.concatenate([jnp.pad(wk, ((0, 0), (0, 0), (0, CMP_HIDDEN))),
                          jnp.pad(wv, ((0, 0), (0, 0), (CMP_HIDDEN, 0)))], axis=1).astype(BF16)
    pe = jnp.concatenate([pe_k, pe_v], axis=1)

    def full(shape):
        return pl.BlockSpec(shape, lambda b: (0,) * len(shape))

    return pl.pallas_call(
        _nsa_compress_kernel,
        grid=(batch,),
        in_specs=[pl.BlockSpec((seq, LANES), lambda b: (b, 0)), full((CMP_LEN, LANES)),
                  full((CMP_LEN, LANES, 2 * CMP_HIDDEN)), full((CMP_HIDDEN, LANES)),
                  full((HEAD_DIM, CMP_HIDDEN))],
        out_specs=[pl.BlockSpec((1, ng, LANES), lambda b: (b, 0, 0)),
                   pl.BlockSpec((1, HEAD_DIM, ng), lambda b: (b, 0, 0))],
        out_shape=[jax.ShapeDtypeStruct((batch, ng, LANES), BF16),
                   jax.ShapeDtypeStruct((batch, HEAD_DIM, ng), BF16)],
        compiler_params=_params(("parallel",)),
        name="nsa_compress",
    )(cmp2d, pe, w1, jnp.pad(w_ck2, ((0, 0), (0, LANES - HEAD_DIM))).astype(BF16), w_cv2.T.astype(BF16))


def _nsa_phases(ovl_ref, q_ref, k_ref, vt_ref, kc_ref, vct_ref, misc_ref, z_ref, o_ref,
                m_ref, acc_ref, wq_ref, s_ref, sc_ref, *, tq, tk, seq):
    qi = pl.program_id(1)
    kpq = tq // tk
    scale = HEAD_DIM ** -0.5
    c1 = scale * LOG2E
    ng = seq // CMP_STRIDE
    ns = seq // SLC_BLOCK
    top = min(SLC_TOPK, ns)
    m_ref[...] = jnp.full(m_ref.shape, NEG_INF, F32)
    acc_ref[...] = jnp.zeros(acc_ref.shape, F32)
    rel = _rel_t(tk, tq)

    qt = q_ref[...].astype(F32).T.astype(BF16)
    zeros_q = jnp.zeros((HEAD_DIM, tq), BF16)
    for h in range(HEADS):
        qh = qt[h * HEAD_DIM:(h + 1) * HEAD_DIM]
        alibi = _alibi_rows(NSA_SLOPES[h] / scale, tq, tk=tk)
        wq_ref[h, 0:LANES, :] = jnp.concatenate([qh, zeros_q], axis=0)
        wq_ref[h, LANES:2 * LANES, :] = alibi
        wq_ref[HEADS + h, 0:LANES, :] = jnp.concatenate([zeros_q, qh], axis=0)
        wq_ref[HEADS + h, LANES:2 * LANES, :] = alibi

    t_lane = qi * tq + lax.broadcasted_iota(jnp.int32, (1, tq), 1)
    n_sub = lax.broadcasted_iota(jnp.int32, (ng, 1), 0)
    c_valid = (n_sub * CMP_STRIDE + (CMP_LEN - 1)) <= t_lane
    kc = kc_ref[0]
    vct_ext = jnp.concatenate([vct_ref[0], jnp.ones((BF16_ROWS, ng), BF16)], axis=0)
    ovl = ovl_ref[...]
    cmp_max = []
    for h in range(HEADS):
        rhs = jnp.concatenate([qt[h * HEAD_DIM:(h + 1) * HEAD_DIM],
                               _alibi_rows(NSA_SLOPES[h] * CMP_STRIDE / scale, tq, HEAD_DIM)], axis=0)
        cmp_max.append(_stage_scores(sc_ref, h, slice(0, tq), _dot(kc, rhs), c_valid))
    imp = jnp.zeros((ns, tq), F32)
    o_cmp = []
    for h in range(HEADS):
        mx = jnp.where(cmp_max[h] > NEG_INF, cmp_max[h], 0.0)
        pb = jnp.exp2(c1 * sc_ref[h] - c1 * mx).astype(BF16)
        o_ext = _dot(vct_ext, pb)
        r = 1.0 / jnp.maximum(o_ext[HEAD_DIM:HEAD_DIM + 1], 1e-30)
        o_cmp.append(o_ext[0:HEAD_DIM] * r)
        imp = imp + _dot(ovl, pb) * r

    def select_blocks():
        j_sub = lax.broadcasted_iota(jnp.int32, (ns, 1), 0)
        j_sub_f = j_sub.astype(F32)
        cur = jnp.right_shift(t_lane, SLC_SHIFT)
        forced = (j_sub == 0) | (j_sub == cur) | (j_sub == cur - 1)
        valid = (j_sub * SLC_BLOCK) <= t_lane
        score = jnp.where(forced, 1e30, jnp.where(valid, imp, -1.0))
        sel = jnp.zeros((ns, tq), F32)
        for _ in range(top):
            mx = jnp.max(score, axis=0, keepdims=True)
            idx = jnp.min(jnp.where(score == mx, j_sub_f, float(ns)), axis=0, keepdims=True)
            pick = j_sub_f == idx
            sel = jnp.where(pick, 1.0, sel)
            score = jnp.where(pick, -2.0, score)
        unsel = jnp.where(sel > 0.5, 0.0, -MASK_BIG).astype(BF16)
        for h in range(HEADS):
            wq_ref[h, LANES + BLK_LANE0:LANES + BLK_LANE0 + ns, :] = unsel

    aux_lane = lax.broadcasted_iota(jnp.int32, (tk, LANES), 1)
    aux_blk = jnp.right_shift(lax.broadcasted_iota(jnp.int32, (tk, LANES), 0), SLC_SHIFT) + BLK_LANE0
    kpos = _key_pos_features(tk)
    ones = jnp.ones((BF16_ROWS, tk), BF16)

    n_slots = s_ref.shape[0]

    def tiles(window, kis, cols, mask_offs=None):
        base = HEADS if window else 0
        loaded = []
        for t, ki in enumerate(kis):
            start = pl.multiple_of(ki * tk, tk)
            k = k_ref[pl.ds(start, tk), :]
            if window:
                vt = vt_ref[HEAD_DIM:2 * HEAD_DIM, pl.ds(start, tk)]
                aux = kpos
                dist = rel[:, cols[t]] + (qi * tq - ki * tk)
                keep = (dist >= 0) & (dist < WINDOW)
            else:
                vt = vt_ref[0:HEAD_DIM, pl.ds(start, tk)]
                onehot = aux_lane == aux_blk + ki * (tk // SLC_BLOCK)
                aux = jnp.where(onehot, jnp.ones_like(kpos), kpos)
                keep = None
            loaded.append((jnp.concatenate([k, aux], axis=1),
                           jnp.concatenate([vt, ones], axis=0), (ki * tk).astype(F32), keep))
        items = [(t, h) for t in range(len(kis)) for h in range(HEADS)]
        s_max = {}

        def scores(i):
            t, h = items[i]
            if window:
                keep = loaded[t][3]
            else:
                keep = None if mask_offs[t] is None else rel[:, cols[t]] >= mask_offs[t]
            s_max[i] = _stage_scores(s_ref, i % n_slots, cols[t],
                                     _dot(loaded[t][0], wq_ref[base + h, :, cols[t]]), keep)

        for i in range(QK_LOOKAHEAD):
            scores(i)
        for i, (t, h) in enumerate(items):
            if i + QK_LOOKAHEAD < len(items):
                scores(i + QK_LOOKAHEAD)
            _lhs, vt_ext, key0, _keep = loaded[t]
            _flash_update_t(lambda i=i, t=t: s_ref[i % n_slots, :, cols[t]], s_max.pop(i), c1,
                            (NSA_SLOPES[h] * LOG2E) * key0, vt_ext, m_ref, acc_ref, base + h, cols[t])
            yield

    all_cols = slice(0, tq)
    diag_cols = [slice(d * tk, tq) for d in range(kpq)]

    _run_interleaved(tiles(True, [qi * kpq + d for d in range(kpq)], diag_cols))
    select_blocks()

    def loop_steps(i):
        return tiles(False, [i * kpq + d for d in range(kpq)], [all_cols] * kpq, [None] * kpq)

    def back_window():
        n_back = (WINDOW + tk - 1) // tk
        for g in range((n_back + kpq - 1) // kpq):
            backs = list(range(g * kpq + 1, min((g + 1) * kpq, n_back) + 1))
            back_cols = [slice(0, min(tq, -(-(WINDOW - (back - 1) * tk - 1) // LANES) * LANES)) for back in backs]

            @pl.when(qi > g)
            def _():
                _run_interleaved(tiles(True, [qi * kpq - back for back in backs], back_cols))

    def diag_steps():
        return tiles(False, [qi * kpq + d for d in range(kpq)], diag_cols, [d * tk for d in range(kpq)])

    def finish():
        g_t = jax.nn.sigmoid(misc_ref[...]).T
        outs = []
        for h in range(HEADS):
            r0 = GATE_COL + 3 * h
            outs.append(g_t[r0:r0 + 1] * o_cmp[h] + g_t[r0 + 1:r0 + 2] * _normalized(acc_ref[h])
                        + g_t[r0 + 2:r0 + 3] * _normalized(acc_ref[HEADS + h]))
        y = jnp.concatenate(outs, axis=0).T * _silu(z_ref[...])
        o_ref[...] = y.astype(o_ref.dtype)

    return loop_steps, back_window, diag_steps, finish


def _overlap_t(seq):
    nc = (seq - CMP_LEN) // CMP_STRIDE + 1
    ng = seq // CMP_STRIDE
    ns = seq // SLC_BLOCK
    c_start = np.arange(ng) * CMP_STRIDE
    c_end = c_start + CMP_LEN - 1
    s_start = np.arange(ns) * SLC_BLOCK
    s_end = s_start + SLC_BLOCK - 1
    ov = (c_start[None, :] <= s_end[:, None]) & (c_end[None, :] >= s_start[:, None]) & (np.arange(ng)[None, :] < nc)
    return jnp.asarray(ov.astype(np.float32), dtype=BF16)


def _nsa_operands(q, k2, vt_all, kc, vct, misc, z_all, seq, nq, tq, tk):
    ng = seq // CMP_STRIDE
    ns = seq // SLC_BLOCK
    assert tq - tk < WINDOW and tk % SLC_BLOCK == 0 and ns <= LANES - BLK_LANE0
    inputs = [_overlap_t(seq), q, k2, vt_all, kc, vct, misc, z_all]
    in_specs = [pl.BlockSpec((ns, ng), lambda b, i: (0, 0)),
                pl.BlockSpec((tq, GROUP_W), lambda b, i: (b * nq + i, 0)),
                pl.BlockSpec((seq, LANES), lambda b, i: (b, 0)),
                pl.BlockSpec((2 * HEAD_DIM, seq), lambda b, i: (NSA_VT_BLK, b)),
                pl.BlockSpec((1, ng, LANES), lambda b, i: (b, 0, 0)),
                pl.BlockSpec((1, HEAD_DIM, ng), lambda b, i: (b, 0, 0)),
                pl.BlockSpec((tq, LANES), lambda b, i: (b * nq + i, 0)),
                pl.BlockSpec((tq, GROUP_W), lambda b, i: (b * nq + i, 0))]
    scratch = [pltpu.VMEM((2 * HEADS, 1, tq), F32),
               pltpu.VMEM((2 * HEADS, ACC_ROWS, tq), F32),
               pltpu.VMEM((2 * HEADS, 2 * LANES, tq), BF16),
               pltpu.VMEM((2 * HEADS, tk, tq), F32),
               pltpu.VMEM((HEADS, ng, tq), F32)]
    return inputs, in_specs, scratch


def _mixers_kernel(*refs, n_in, n_scratch, tq, tk_nsa, tk_diff, seq, lam_init):
    bounds = np.cumsum([0] + list(n_in))
    nsa_in, diff_in, ret_in, ssd_in = (refs[a:b] for a, b in zip(bounds[:-1], bounds[1:]))
    nsa_out, diff_out, ret_out, ssd_out = refs[bounds[-1]:bounds[-1] + 4]
    sb = np.cumsum([0] + list(n_scratch)) + bounds[-1] + 4
    nsa_scr, diff_scr, rec_scr = (refs[a:b] for a, b in zip(sb[:-1], sb[1:]))
    ret_st, ssd_ext, ssd_st = rec_scr
    qi = pl.program_id(1)

    @pl.when(qi == 0)
    def _():
        ret_st[...] = jnp.zeros(ret_st.shape, F32)
        ssd_st[...] = jnp.zeros(ssd_st.shape, F32)
        ssd_ext[0:8, :] = jnp.zeros((8, CONV_CH), F32)

    chunks = list(range(ret_out.shape[0]))
    first, second = chunks[:len(chunks) // 2], chunks[len(chunks) // 2:]
    nsa_loop, nsa_back_window, nsa_diag, nsa_finish = _nsa_phases(
        *nsa_in, nsa_out, *nsa_scr, tq=tq, tk=tk_nsa, seq=seq)
    diff_loop, diff_diag, diff_finish = _diff_phases(
        *diff_in, diff_out, *diff_scr, tq=tq, tk=tk_diff, lam_init=lam_init)
    _ret_body(*ret_in, ret_out, ret_st, first)
    _ssd_body(*ssd_in, ssd_out, ssd_ext, ssd_st, first)

    def body(i, carry):
        _run_interleaved(nsa_loop(i), diff_loop(i))
        return carry

    lax.fori_loop(0, qi, body, 0)
    nsa_back_window()
    _run_interleaved(nsa_diag(), diff_diag())
    _ret_body(*ret_in, ret_out, ret_st, second)
    _ssd_body(*ssd_in, ssd_out, ssd_ext, ssd_st, second)
    nsa_finish()
    diff_finish()


def _mixers(nsa_args, diff_args, rec_args, batch, seq, layer_idx, tq, tk_nsa, tk_diff):
    nq = seq // tq
    nsa_in, nsa_specs, nsa_scratch = _nsa_operands(*nsa_args, seq, nq, tq, tk_nsa)
    diff_in, diff_specs, diff_scratch = _diff_operands(*diff_args, seq, nq, tq, tk_diff)
    ret_in, ssd_in, rec_specs, rec_out_spec, rec_scratch = _recurrent_operands(*rec_args, nq, tq)
    out_spec = pl.BlockSpec((tq, GROUP_W), lambda b, i: (b * nq + i, 0))
    m = batch * seq
    y_nsa, y_diff, y_ret, y_ssm = pl.pallas_call(
        functools.partial(_mixers_kernel, n_in=(len(nsa_in), len(diff_in), len(ret_in), len(ssd_in)),
                          n_scratch=(len(nsa_scratch), len(diff_scratch), len(rec_scratch)),
                          tq=tq, tk_nsa=tk_nsa, tk_diff=tk_diff, seq=seq,
                          lam_init=0.8 - 0.6 * math.exp(-0.3 * layer_idx)),
        grid=(batch, nq),
        in_specs=nsa_specs + diff_specs + rec_specs,
        out_specs=[out_spec, out_spec, rec_out_spec, rec_out_spec],
        out_shape=[jax.ShapeDtypeStruct((m, GROUP_W), BF16)] * 2
                  + [jax.ShapeDtypeStruct((m // SSM_CHUNK, SSM_CHUNK, GROUP_W), BF16)] * 2,
        scratch_shapes=nsa_scratch + diff_scratch + rec_scratch,
        compiler_params=_params(("parallel", "arbitrary")),
        name="mixers",
    )(*nsa_in, *diff_in, *ret_in, *ssd_in)
    return y_nsa, y_diff, y_ret.reshape(m, GROUP_W), y_ssm.reshape(m, GROUP_W)


def _ret_tables():
    c = RET_CHUNK
    h = np.arange(HEADS, dtype=np.float32)
    log_g = jnp.log(1.0 - 2.0 ** (-5.0 - jnp.asarray(h)))
    pos = jnp.arange(c, dtype=F32)
    rel = pos[:, None] - pos[None, :]
    decay = jnp.where(rel >= 0, jnp.exp(log_g[:, None, None] * jnp.maximum(rel, 0.0)), 0.0)
    xi = jnp.exp(log_g[:, None] * (pos + 1.0))
    zeta = jnp.exp(log_g[:, None] * (c - 1.0 - pos))
    chunk_decay = jnp.exp(log_g * c)
    xi_tab = jnp.repeat(xi.T, HEAD_DIM, axis=1)
    zeta_tab = jnp.repeat(zeta.T, HEAD_DIM, axis=1)
    cd_tab = jnp.repeat(chunk_decay, HEAD_DIM)[None, :]
    return decay, xi_tab, zeta_tab, cd_tab


def _ret_body(decay_ref, xi_ref, zeta_ref, cd_ref, gn_ref, q_ref, k_ref, v_ref, z_ref, o_ref, st_ref, blocks):
    xi = xi_ref[...]
    cd = cd_ref[...]
    for bb in blocks:
        q = (q_ref[bb].astype(F32) * (HEAD_DIM ** -0.5)).astype(BF16)
        k = k_ref[bb]
        v = v_ref[bb]
        kz_t = (k.astype(F32) * zeta_ref[...]).T.astype(BF16)
        outs = []
        for h in range(HEADS):
            sl = slice(h * HEAD_DIM, (h + 1) * HEAD_DIM)
            qh, kh, vh = q[:, sl], k[:, sl], v[:, sl]
            prev = st_ref[h]
            inner = (_dot_nt(qh, kh) * decay_ref[h]).astype(BF16)
            o = _dot(inner, vh) + _dot(qh, prev.astype(BF16)) * xi[:, sl]
            st_ref[h] = prev * cd[:, sl] + _dot(kz_t[sl, :], vh)
            mu = jnp.mean(o, axis=-1, keepdims=True)
            d = o - mu
            var = jnp.mean(d * d, axis=-1, keepdims=True)
            outs.append(d * lax.rsqrt(var + EPS))
        y = jnp.concatenate(outs, axis=-1) * gn_ref[...] * _silu(z_ref[bb])
        o_ref[bb] = y.astype(o_ref.dtype)


def _ssd_body(cw_ref, cb_ref, dtb_ref, a_ref, dsk_ref, nw_ref, xbc_ref, misc_ref, z_ref, o_ref,
              ext_ref, st_ref, blocks):
    L = SSM_CHUNK
    hi = lax.Precision.HIGHEST
    row = lax.broadcasted_iota(jnp.int32, (L, L), 0)
    col = lax.broadcasted_iota(jnp.int32, (L, L), 1)
    causal = row >= col
    tril = jnp.where(causal, 1.0, 0.0).astype(F32)
    dsk = dsk_ref[...]

    for bb in blocks:
        raw = xbc_ref[bb]
        ext_ref[8:8 + L, :] = raw
        conv = cb_ref[...] + raw * cw_ref[CONV_W - 1:CONV_W, :]
        for w in range(CONV_W - 1):
            shift = CONV_W - 1 - w
            conv = conv + ext_ref[8 - shift:8 - shift + L, :] * cw_ref[w:w + 1, :]
        ext_ref[0:8, :] = raw[L - 8:L, :]
        xc = _silu(conv)
        x = xc[:, 0:GROUP_W]
        bm = xc[:, GROUP_W:GROUP_W + 2 * SSM_STATE].astype(BF16)
        cm = xc[:, GROUP_W + 2 * SSM_STATE:].astype(BF16)

        dt_full = jax.nn.softplus(misc_ref[bb] + dtb_ref[...])
        da = dt_full * a_ref[...]
        cs_col = jnp.dot(tril, da, precision=hi, preferred_element_type=F32)
        cs_row = lax.dot_general(da, tril, (((0,), (1,)), ((), ())), precision=hi,
                                 preferred_element_type=F32)

        outs = []
        for h in range(HEADS):
            g = h // 2
            c0 = DT_COL + h
            sl = slice(h * HEAD_DIM, (h + 1) * HEAD_DIM)
            gs = slice(g * SSM_STATE, (g + 1) * SSM_STATE)
            cs_c = cs_col[:, c0:c0 + 1]
            cs_r = cs_row[c0:c0 + 1, :]
            cs_last = cs_col[L - 1:L, c0:c0 + 1]
            xh = x[:, sl]
            xdt = xh * dt_full[:, c0:c0 + 1]
            seg = jnp.exp(jnp.where(causal, cs_c - cs_r, NEG_INF))
            cb = _dot_nt(cm[:, gs], bm[:, gs])
            y = _dot((cb * seg).astype(BF16), xdt.astype(BF16))
            prev = st_ref[h]
            y = y + _dot(cm[:, gs], prev.astype(BF16)) * jnp.exp(cs_c)
            y = y + dsk[:, sl] * xh
            dec = jnp.exp(cs_last - cs_c)
            st_ref[h] = prev * jnp.exp(cs_last) + _dot_tn(bm[:, gs], (xdt * dec).astype(BF16))
            outs.append(y)
        y = jnp.concatenate(outs, axis=-1) * _silu(z_ref[bb])
        ms = jnp.mean(y * y, axis=-1, keepdims=True)
        o_ref[bb] = (y * lax.rsqrt(ms + EPS) * nw_ref[...]).astype(o_ref.dtype)


def _recurrent_operands(ret_qkv, xbc, misc, z_all, gn_w, conv_w, conv_b, dt_bias, a_log, d_skip, norm_w, nq, tq):
    L = SSM_CHUNK
    assert RET_CHUNK == L and tq % L == 0
    cpt = tq // L
    decay, xi_tab, zeta_tab, cd_tab = _ret_tables()
    dtb = jnp.zeros((1, 128), F32).at[0, DT_COL:DT_COL + HEADS].set(dt_bias)
    a_full = jnp.zeros((1, 128), F32).at[0, DT_COL:DT_COL + HEADS].set(-jnp.exp(a_log))
    dsk = jnp.repeat(d_skip, HEAD_DIM)[None, :]

    def full(shape):
        return pl.BlockSpec(shape, lambda b, i: (0,) * len(shape))

    def blk(width, col):
        return pl.BlockSpec((cpt, L, width), lambda b, i: (b * nq + i, 0, col))

    def chunked(a):
        return a.reshape(-1, L, a.shape[-1])

    qkv3, z3 = chunked(ret_qkv), chunked(z_all)
    ret_in = [decay, xi_tab, zeta_tab, cd_tab, gn_w.reshape(1, GROUP_W), qkv3, qkv3, qkv3, z3]
    ret_specs = [full((HEADS, L, L)), full((L, GROUP_W)), full((L, GROUP_W)), full((1, GROUP_W)),
                 full((1, GROUP_W)), blk(GROUP_W, 0), blk(GROUP_W, 1), blk(GROUP_W, 2), blk(GROUP_W, 2)]
    ssd_in = [conv_w, conv_b.reshape(1, CONV_CH), dtb, a_full, dsk, norm_w.reshape(1, GROUP_W),
              chunked(xbc), chunked(misc), z3]
    ssd_specs = [full((CONV_W, CONV_CH)), full((1, CONV_CH)), full((1, 128)), full((1, 128)),
                 full((1, GROUP_W)), full((1, GROUP_W)), blk(CONV_CH, 0), blk(128, 0), blk(GROUP_W, 3)]
    scratch = [pltpu.VMEM((HEADS, HEAD_DIM, HEAD_DIM), F32),
               pltpu.VMEM((8 + L, CONV_CH), F32),
               pltpu.VMEM((HEADS, SSM_STATE, HEAD_DIM), F32)]
    return ret_in, ssd_in, ret_specs + ssd_specs, blk(GROUP_W, 0), scratch


def _pick_tile(n, pref):
    t = pref
    while n % t:
        t //= 2
    return t


def kernel(x, norm_w, w_in, w_out, nsa_pe_k, nsa_pe_v, nsa_w_ck1, nsa_w_ck2, nsa_w_cv1, nsa_w_cv2,
           diff_lam_q1, diff_lam_k1, diff_lam_q2, diff_lam_k2, diff_subln_w, ret_gn_w,
           ssm_conv_w, ssm_conv_b, ssm_dt_bias, ssm_A_log, ssm_D, ssm_norm_w, final_norm_w):
    batch, seq, _ = x.shape
    depth = w_in.shape[0]
    m = batch * seq
    tm = _pick_tile(m, 512)
    tq = _pick_tile(seq, 512)
    tk = _pick_tile(seq, 256)
    tk_diff = _pick_tile(seq, 512)
    w_in_b = w_in.astype(BF16)
    w_r = _relayout_w_in(w_in_b)
    w_t = _relayout_w_in_t(w_in_b)
    w_out_b = w_out.astype(BF16)
    x2d = x.reshape(m, D_MODEL)
    projected = _in_proj(x2d, norm_w[0], w_r[0], w_t[0], tm)
    for i in range(depth):
        nsa_q, nsa_k2, nsa_cmp, misc, z_all, diff_qk, ret_qkv, xbc, vt_all = projected
        kc, vct = _nsa_compress(nsa_cmp, nsa_pe_k[i], nsa_pe_v[i], nsa_w_ck1[i], nsa_w_ck2[i],
                                nsa_w_cv1[i], nsa_w_cv2[i], batch, seq)
        lam_p = jnp.stack([diff_lam_q1[i], diff_lam_k1[i], diff_lam_q2[i], diff_lam_k2[i]])
        ys = _mixers((nsa_q, nsa_k2, vt_all, kc, vct, misc, z_all),
                     (diff_qk, vt_all, z_all, lam_p, diff_subln_w[i]),
                     (ret_qkv, xbc, misc, z_all, ret_gn_w[i], ssm_conv_w[i], ssm_conv_b[i], ssm_dt_bias[i],
                      ssm_A_log[i], ssm_D[i], ssm_norm_w[i]),
                     batch, seq, i, tq, tk, tk_diff)
        if i + 1 < depth:
            x2d, *projected = _out_in_proj(ys, w_out_b[i], x2d, norm_w[i + 1], w_r[i + 1], w_t[i + 1], tm)
        else:
            x2d = _out_proj(ys, w_out_b[i], x2d, final_norm_w, tm)
    return x2d.reshape(batch, seq, D_MODEL)
```

```python
import functools
import math

import numpy as np
import jax
import jax.numpy as jnp
from jax import lax
from jax.experimental import pallas as pl
from jax.experimental.pallas import tpu as pltpu

F32 = jnp.float32
BF16 = jnp.bfloat16
NEG_INF = float("-inf")
LOG2E = 1.4426950408889634

D_MODEL = 1024
DEPTH = 4
GROUP_W = 256
HEADS = 4
HEAD_DIM = 64
EPS = 1e-6
CMP_LEN = 32
CMP_STRIDE = 16
CMP_HIDDEN = 256
SLC_BLOCK = 64
SLC_SHIFT = 6
SLC_TOPK = 16
WINDOW = 512
DIFF_QK_DIM = 32
RET_CHUNK = 128
SSM_STATE = 128
SSM_CHUNK = 128
CONV_W = 4
CONV_CH = 768
N_ALIBI_HEADS = 8
LANES = 128
BF16_ROWS = 16
ACC_ROWS = HEAD_DIM + BF16_ROWS
QK_LOOKAHEAD = 3
ALIBI_ROWS = 3
POS_RADIX = 256
MASK_BIG = 2.0 ** 100
BLK_LANE0 = 64

IN_LAYOUT = (
    ("nsa_q", 256), ("nsa_k_cmp", 64), ("nsa_v_cmp", 64), ("nsa_k_slc", 64), ("nsa_v_slc", 64),
    ("nsa_k_win", 64), ("nsa_v_win", 64), ("nsa_gate", 12), ("nsa_z", 256),
    ("diff_q", 256), ("diff_k", 256), ("diff_v", 256), ("diff_z", 256),
    ("ret_q", 256), ("ret_k", 256), ("ret_v", 256), ("ret_z", 256),
    ("ssm_z", 256), ("ssm_xbc", 768), ("ssm_dt", 4),
)
IN_OFF = {}
_o = 0
for _n, _w in IN_LAYOUT:
    IN_OFF[_n] = (_o, _w)
    _o += _w
IN_W = _o

GATE_COL = 0
DT_COL = 12
IN_OUTPUTS = (
    ("nsa_q", BF16, ("nsa_q",), 256),
    ("nsa_k2", BF16, ("nsa_k_slc", "nsa_k_win"), 128),
    ("nsa_cmp", F32, ("nsa_k_cmp", "nsa_v_cmp"), 128),
    ("misc", F32, ("nsa_gate", "ssm_dt"), 128),
    ("z_all", F32, ("nsa_z", "diff_z", "ret_z", "ssm_z"), 1024),
    ("diff_qk", BF16, ("diff_q", "diff_k"), 512),
    ("ret_qkv", BF16, ("ret_q", "ret_k", "ret_v"), 768),
    ("xbc", F32, ("ssm_xbc",), 768),
)
IN_T_SRC = ("diff_v", "nsa_v_slc", "nsa_v_win")
IN_T_ROWS = 384
DIFF_VT_BLK = 0
NSA_VT_BLK = 2
IN_SEGS = []
_o = 0
for _n, _dt, _src, _w in IN_OUTPUTS:
    IN_SEGS.append((_o, _o + _w))
    _o += _w
IN_WP = _o

VMEM_LIMIT = 56 * 1024 * 1024


def _alibi_slopes():
    return [2.0 ** (-8.0 * (i + 1) / N_ALIBI_HEADS) for i in range(N_ALIBI_HEADS)]


NSA_SLOPES = _alibi_slopes()[0::2]
DIFF_SLOPES = _alibi_slopes()[1::2]


def _silu(x):
    return x * jax.nn.sigmoid(x)


def _dot(a, b):
    return jnp.dot(a, b, preferred_element_type=F32)


def _dot_nt(a, b):
    return lax.dot_general(a, b, (((1,), (1,)), ((), ())), preferred_element_type=F32)


def _dot_tn(a, b):
    return lax.dot_general(a, b, (((0,), (0,)), ((), ())), preferred_element_type=F32)


def _params(sem):
    return pltpu.CompilerParams(dimension_semantics=sem, vmem_limit_bytes=VMEM_LIMIT)


def _relayout_w_in(w_in):
    cols = []
    for _n, _dt, src, width in IN_OUTPUTS:
        used = 0
        for s in src:
            off, w = IN_OFF[s]
            cols.append(w_in[:, :, off:off + w])
            used += w
        if used < width:
            cols.append(jnp.zeros(w_in.shape[:2] + (width - used,), w_in.dtype))
    return jnp.concatenate(cols, axis=-1)


def _relayout_w_in_t(w_in):
    cols = [w_in[:, :, IN_OFF[s][0]:IN_OFF[s][0] + IN_OFF[s][1]] for s in IN_T_SRC]
    return jnp.swapaxes(jnp.concatenate(cols, axis=-1), 1, 2)


def _norm_project(x, nw_ref, w_ref, wt_ref, out_refs):
    ms = jnp.mean(x * x, axis=-1, keepdims=True)
    h = (x * lax.rsqrt(ms + EPS) * nw_ref[...]).astype(BF16)
    for ref, (a, b) in zip(out_refs[:-1], IN_SEGS):
        ref[...] = _dot(h, w_ref[:, a:b]).astype(ref.dtype)
    out_refs[-1][...] = _dot_nt(wt_ref[...], h).astype(BF16)


def _in_proj_kernel(x_ref, nw_ref, w_ref, wt_ref, *out_refs):
    _norm_project(x_ref[...], nw_ref, w_ref, wt_ref, out_refs)


def _in_proj_outputs(m, tm):
    out_shape = [jax.ShapeDtypeStruct((m, b - a), dt) for (_n, dt, _s, _w), (a, b) in zip(IN_OUTPUTS, IN_SEGS)]
    out_specs = [pl.BlockSpec((tm, b - a), lambda i: (i, 0)) for (a, b) in IN_SEGS]
    out_shape.append(jax.ShapeDtypeStruct((IN_T_ROWS, m), BF16))
    out_specs.append(pl.BlockSpec((IN_T_ROWS, tm), lambda i: (0, i)))
    return out_shape, out_specs


def _in_proj(x2d, norm_w, w_r, w_t, tm):
    m = x2d.shape[0]
    out_shape, out_specs = _in_proj_outputs(m, tm)
    return pl.pallas_call(
        _in_proj_kernel,
        grid=(m // tm,),
        in_specs=[pl.BlockSpec((tm, D_MODEL), lambda i: (i, 0)),
                  pl.BlockSpec((1, D_MODEL), lambda i: (0, 0)),
                  pl.BlockSpec((D_MODEL, IN_WP), lambda i: (0, 0)),
                  pl.BlockSpec((IN_T_ROWS, D_MODEL), lambda i: (0, 0))],
        out_specs=out_specs,
        out_shape=out_shape,
        compiler_params=_params(("parallel",)),
        name="in_proj",
    )(x2d, norm_w.reshape(1, D_MODEL), w_r, w_t)


def _out_proj_kernel(y0_ref, y1_ref, y2_ref, y3_ref, w_ref, x_ref, fw_ref, o_ref):
    acc = x_ref[...]
    for g, y_ref in enumerate((y0_ref, y1_ref, y2_ref, y3_ref)):
        acc = acc + _dot(y_ref[...], w_ref[g * GROUP_W:(g + 1) * GROUP_W, :])
    ms = jnp.mean(acc * acc, axis=-1, keepdims=True)
    o_ref[...] = acc * lax.rsqrt(ms + EPS) * fw_ref[...]


def _out_proj(ys, w_out_b, x2d, final_w, tm):
    m = x2d.shape[0]
    yspec = pl.BlockSpec((tm, GROUP_W), lambda i: (i, 0))
    return pl.pallas_call(
        _out_proj_kernel,
        grid=(m // tm,),
        in_specs=[yspec, yspec, yspec, yspec,
                  pl.BlockSpec((D_MODEL, D_MODEL), lambda i: (0, 0)),
                  pl.BlockSpec((tm, D_MODEL), lambda i: (i, 0)),
                  pl.BlockSpec((1, D_MODEL), lambda i: (0, 0))],
        out_specs=pl.BlockSpec((tm, D_MODEL), lambda i: (i, 0)),
        out_shape=jax.ShapeDtypeStruct((m, D_MODEL), F32),
        compiler_params=_params(("parallel",)),
        name="out_proj",
    )(*ys, w_out_b, x2d, final_w.reshape(1, D_MODEL))


def _out_in_proj_kernel(y0_ref, y1_ref, y2_ref, y3_ref, wo_ref, x_ref, nw_ref, w_ref, wt_ref, xo_ref, *out_refs):
    acc = x_ref[...]
    for g, y_ref in enumerate((y0_ref, y1_ref, y2_ref, y3_ref)):
        acc = acc + _dot(y_ref[...], wo_ref[g * GROUP_W:(g + 1) * GROUP_W, :])
    xo_ref[...] = acc
    _norm_project(acc, nw_ref, w_ref, wt_ref, out_refs)


def _out_in_proj(ys, w_out_b, x2d, norm_w, w_r, w_t, tm):
    m = x2d.shape[0]
    yspec = pl.BlockSpec((tm, GROUP_W), lambda i: (i, 0))
    xspec = pl.BlockSpec((tm, D_MODEL), lambda i: (i, 0))
    out_shape, out_specs = _in_proj_outputs(m, tm)
    return pl.pallas_call(
        _out_in_proj_kernel,
        grid=(m // tm,),
        in_specs=[yspec, yspec, yspec, yspec,
                  pl.BlockSpec((D_MODEL, D_MODEL), lambda i: (0, 0)),
                  xspec,
                  pl.BlockSpec((1, D_MODEL), lambda i: (0, 0)),
                  pl.BlockSpec((D_MODEL, IN_WP), lambda i: (0, 0)),
                  pl.BlockSpec((IN_T_ROWS, D_MODEL), lambda i: (0, 0))],
        out_specs=[xspec] + out_specs,
        out_shape=[jax.ShapeDtypeStruct((m, D_MODEL), F32)] + out_shape,
        compiler_params=_params(("parallel",)),
        name="out_in_proj",
    )(*ys, w_out_b, x2d, norm_w.reshape(1, D_MODEL), w_r, w_t)


def _run_interleaved(*pipelines):
    live = list(pipelines)
    while live:
        for g in list(live):
            if next(g, _DONE) is _DONE:
                live.remove(g)


_DONE = object()


def _stage_scores(s_ref, slot, cols, s, keep=None):
    if keep is not None:
        s = jnp.where(keep, s, NEG_INF)
    s_ref[slot, :, cols] = s
    return jnp.max(s, axis=0, keepdims=True)


def _flash_update_t(s, s_max, c1, shift, vt_ext, m_ref, acc_ref, idx, cols):
    m_old = m_ref[idx, :, cols]
    m_new = jnp.maximum(m_old, c1 * s_max + shift)
    alpha = jnp.exp2(m_old - m_new)
    p = jnp.exp2(c1 * s() - (m_new - shift))
    acc_ref[idx, :, cols] = alpha * acc_ref[idx, :, cols] + _dot(vt_ext, p.astype(BF16))
    m_ref[idx, :, cols] = m_new


def _bf16_pieces(x, n=3):
    out = []
    for _ in range(n):
        p = float(np.asarray(x, np.float32).astype(BF16).astype(np.float32))
        out.append(p)
        x = x - p
    return out


def _key_pos_features(tk):
    lane = lax.broadcasted_iota(jnp.int32, (tk, LANES), 1)
    row = lax.broadcasted_iota(jnp.int32, (tk, LANES), 0)
    out = jnp.zeros((tk, LANES), jnp.int32)
    for d in range(-(-tk // POS_RADIX)):
        digit = jnp.bitwise_and(jnp.right_shift(row, 8 * d), POS_RADIX - 1)
        out = jnp.where((lane >= ALIBI_ROWS * d) & (lane < ALIBI_ROWS * (d + 1)), digit, out)
    return out.astype(F32).astype(BF16)


def _alibi_rows(beta, tq, rows=LANES, tk=POS_RADIX):
    row = lax.broadcasted_iota(jnp.int32, (rows, tq), 0)
    out = jnp.zeros((rows, tq), F32)
    for d in range(-(-tk // POS_RADIX)):
        for r, piece in enumerate(_bf16_pieces(beta * POS_RADIX ** d, ALIBI_ROWS)):
            out = jnp.where(row == ALIBI_ROWS * d + r, piece, out)
    return out.astype(BF16)


def _normalized(acc):
    return acc[0:HEAD_DIM] / jnp.maximum(acc[HEAD_DIM:HEAD_DIM + 1], 1e-30)


def _rel_t(tk, tq):
    return lax.broadcasted_iota(jnp.int32, (tk, tq), 1) - lax.broadcasted_iota(jnp.int32, (tk, tq), 0)


def _diff_phases(lam_ref, sw_ref, q_ref, k_ref, vt_ref, z_ref, o_ref, m_ref, acc_ref, wq_ref, s_ref,
                 *, tq, tk, lam_init):
    qi = pl.program_id(1)
    kpq = tq // tk
    scale = DIFF_QK_DIM ** -0.5
    c1 = scale * LOG2E
    m_ref[...] = jnp.full(m_ref.shape, NEG_INF, F32)
    acc_ref[...] = jnp.zeros(acc_ref.shape, F32)
    rel = _rel_t(tk, tq)

    qt = q_ref[...].astype(F32).T.astype(BF16)
    row = lax.broadcasted_iota(jnp.int32, (LANES, tq), 0)
    for j in range(2 * HEADS):
        g, r0 = divmod(j * DIFF_QK_DIM, LANES)
        qg = qt[g * LANES:(g + 1) * LANES]
        wq_ref[j, 0:LANES, :] = jnp.where((row >= r0) & (row < r0 + DIFF_QK_DIM), qg, jnp.zeros_like(qg))
        wq_ref[j, LANES:2 * LANES, :] = _alibi_rows(DIFF_SLOPES[j // 2] / scale, tq, tk=tk)

    ones = jnp.ones((BF16_ROWS, tk), BF16)
    kpos = _key_pos_features(tk)

    n_maps = 2 * HEADS
    n_slots = s_ref.shape[0]

    def tiles(kis, mask_offs):
        loaded = []
        for ki in kis:
            start = pl.multiple_of(ki * tk, tk)
            loaded.append((k_ref[pl.ds(start, tk), :],
                           vt_ref[:, pl.ds(start, tk)],
                           (ki * tk).astype(F32)))
        items = [(t, j) for t in range(len(kis)) for j in range(n_maps)]
        cols = slice(0, tq)

        def scores(i):
            t, j = items[i]
            g = (j * DIFF_QK_DIM) // LANES
            lhs = jnp.concatenate([loaded[t][0][:, g * LANES:(g + 1) * LANES], kpos], axis=1)
            s_ref[i % n_slots] = _dot(lhs, wq_ref[j])

        def staged(i):
            t = items[i][0]
            if mask_offs[t] is None:
                return lambda: s_ref[i % n_slots]
            return lambda: jnp.where(rel >= mask_offs[t], s_ref[i % n_slots], NEG_INF)

        for i in range(QK_LOOKAHEAD):
            scores(i)
        for i, (t, j) in enumerate(items):
            h = j // 2
            if i + QK_LOOKAHEAD < len(items):
                scores(i + QK_LOOKAHEAD)
            _k, vt, key0 = loaded[t]
            vt_ext = jnp.concatenate([vt[h * HEAD_DIM:(h + 1) * HEAD_DIM], ones], axis=0)
            s = staged(i)
            _flash_update_t(s, jnp.max(s(), axis=0, keepdims=True), c1, (DIFF_SLOPES[h] * LOG2E) * key0,
                            vt_ext, m_ref, acc_ref, j, cols)
            yield

    def loop_steps(i):
        return tiles([i * kpq + d for d in range(kpq)], [None] * kpq)

    def diag_steps():
        return tiles([qi * kpq + d for d in range(kpq)], [d * tk for d in range(kpq)])

    def finish():
        lp = lam_ref[...]
        lam = (jnp.exp(jnp.sum(lp[0:1] * lp[1:2], axis=-1, keepdims=True))
               - jnp.exp(jnp.sum(lp[2:3] * lp[3:4], axis=-1, keepdims=True)) + lam_init)
        sw = sw_ref[...]
        o_t = jnp.concatenate([_normalized(acc_ref[2 * h]) - lam * _normalized(acc_ref[2 * h + 1])
                               for h in range(HEADS)], axis=0)
        o = o_t.T
        outs = []
        for h in range(HEADS):
            oh = o[:, h * HEAD_DIM:(h + 1) * HEAD_DIM]
            ms = jnp.mean(oh * oh, axis=-1, keepdims=True)
            outs.append(oh * lax.rsqrt(ms + EPS) * sw * (1.0 - lam_init))
        y = jnp.concatenate(outs, axis=-1) * _silu(z_ref[...])
        o_ref[...] = y.astype(o_ref.dtype)

    return loop_steps, diag_steps, finish


def _diff_operands(qk, vt_all, z_all, lam_p, subln_w, seq, nq, tq, tk):
    inputs = [lam_p, subln_w.reshape(1, HEAD_DIM), qk, qk, vt_all, z_all]
    in_specs = [pl.BlockSpec((4, DIFF_QK_DIM), lambda b, i: (0, 0)),
                pl.BlockSpec((1, HEAD_DIM), lambda b, i: (0, 0)),
                pl.BlockSpec((tq, GROUP_W), lambda b, i: (b * nq + i, 0)),
                pl.BlockSpec((seq, GROUP_W), lambda b, i: (b, 1)),
                pl.BlockSpec((GROUP_W, seq), lambda b, i: (DIFF_VT_BLK, b)),
                pl.BlockSpec((tq, GROUP_W), lambda b, i: (b * nq + i, 1))]
    scratch = [pltpu.VMEM((2 * HEADS, 1, tq), F32),
               pltpu.VMEM((2 * HEADS, ACC_ROWS, tq), F32),
               pltpu.VMEM((2 * HEADS, 2 * LANES, tq), BF16),
               pltpu.VMEM((QK_LOOKAHEAD + 1, tk, tq), F32)]
    return inputs, in_specs, scratch


def _nsa_compress_kernel(cmp_ref, pe_ref, w1_ref, wk2_ref, wv2t_ref, kc_ref, vct_ref):
    ng = kc_ref.shape[1]
    top = jnp.zeros((ng, 2 * CMP_HIDDEN), F32)
    bot = jnp.zeros((ng, 2 * CMP_HIDDEN), F32)
    for p in range(CMP_STRIDE):
        x = cmp_ref[pl.ds(p, ng, stride=CMP_STRIDE), :]
        top = top + _dot((x + pe_ref[p:p + 1, :]).astype(BF16), w1_ref[p])
        bot = bot + _dot((x + pe_ref[CMP_STRIDE + p:CMP_STRIDE + p + 1, :]).astype(BF16), w1_ref[CMP_STRIDE + p])
    hid = _silu(top + pltpu.roll(bot, ng - 1, 0)).astype(BF16)
    hk, hv = hid[:, 0:CMP_HIDDEN], hid[:, CMP_HIDDEN:]
    kc = _dot(hk, wk2_ref[...])
    lane = lax.broadcasted_iota(jnp.int32, kc.shape, 1)
    blk = lax.broadcasted_iota(jnp.int32, kc.shape, 0).astype(F32)
    kc = jnp.where((lane >= HEAD_DIM) & (lane < HEAD_DIM + ALIBI_ROWS), blk, kc)
    kc_ref[0] = kc.astype(kc_ref.dtype)
    vct_ref[0] = _dot_nt(wv2t_ref[...], hv).astype(vct_ref.dtype)


def _nsa_compress(cmp2d, pe_k, pe_v, w_ck1, w_ck2, w_cv1, w_cv2, batch, seq):
    ng = seq // CMP_STRIDE
    wk = w_ck1.reshape(CMP_LEN, HEAD_DIM, CMP_HIDDEN)
    wv = w_cv1.reshape(CMP_LEN, HEAD_DIM, CMP_HIDDEN)
    w1 = jnp.concatenate([jnp.pad(wk, ((0, 0), (0, 0), (0, CMP_HIDDEN))),
                          jnp.pad(wv, ((0, 0), (0, 0), (CMP_HIDDEN, 0)))], axis=1).astype(BF16)
    pe = jnp.concatenate([pe_k, pe_v], axis=1)

    def full(shape):
        return pl.BlockSpec(shape, lambda b: (0,) * len(shape))

    return pl.pallas_call(
        _nsa_compress_kernel,
        grid=(batch,),
        in_specs=[pl.BlockSpec((seq, LANES), lambda b: (b, 0)), full((CMP_LEN, LANES)),
                  full((CMP_LEN, LANES, 2 * CMP_HIDDEN)), full((CMP_HIDDEN, LANES)),
                  full((HEAD_DIM, CMP_HIDDEN))],
        out_specs=[pl.BlockSpec((1, ng, LANES), lambda b: (b, 0, 0)),
                   pl.BlockSpec((1, HEAD_DIM, ng), lambda b: (b, 0, 0))],
        out_shape=[jax.ShapeDtypeStruct((batch, ng, LANES), BF16),
                   jax.ShapeDtypeStruct((batch, HEAD_DIM, ng), BF16)],
        compiler_params=_params(("parallel",)),
        name="nsa_compress",
    )(cmp2d, pe, w1, jnp.pad(w_ck2, ((0, 0), (0, LANES - HEAD_DIM))).astype(BF16), w_cv2.T.astype(BF16))


def _nsa_phases(ovl_ref, q_ref, k_ref, vt_ref, kc_ref, vct_ref, misc_ref, z_ref, o_ref,
                m_ref, acc_ref, wq_ref, s_ref, sc_ref, *, tq, tk, seq):
    qi = pl.program_id(1)
    kpq = tq // tk
    scale = HEAD_DIM ** -0.5
    c1 = scale * LOG2E
    ng = seq // CMP_STRIDE
    ns = seq // SLC_BLOCK
    top = min(SLC_TOPK, ns)
    m_ref[...] = jnp.full(m_ref.shape, NEG_INF, F32)
    acc_ref[...] = jnp.zeros(acc_ref.shape, F32)
    rel = _rel_t(tk, tq)

    qt = q_ref[...].astype(F32).T.astype(BF16)
    zeros_q = jnp.zeros((HEAD_DIM, tq), BF16)
    for h in range(HEADS):
        qh = qt[h * HEAD_DIM:(h + 1) * HEAD_DIM]
        alibi = _alibi_rows(NSA_SLOPES[h] / scale, tq, tk=tk)
        wq_ref[h, 0:LANES, :] = jnp.concatenate([qh, zeros_q], axis=0)
        wq_ref[h, LANES:2 * LANES, :] = alibi
        wq_ref[HEADS + h, 0:LANES, :] = jnp.concatenate([zeros_q, qh], axis=0)
        wq_ref[HEADS + h, LANES:2 * LANES, :] = alibi

    t_lane = qi * tq + lax.broadcasted_iota(jnp.int32, (1, tq), 1)
    n_sub = lax.broadcasted_iota(jnp.int32, (ng, 1), 0)
    c_valid = (n_sub * CMP_STRIDE + (CMP_LEN - 1)) <= t_lane
    kc = kc_ref[0]
    vct_ext = jnp.concatenate([vct_ref[0], jnp.ones((BF16_ROWS, ng), BF16)], axis=0)
    ovl = ovl_ref[...]
    cmp_max = []
    for h in range(HEADS):
        rhs = jnp.concatenate([qt[h * HEAD_DIM:(h + 1) * HEAD_DIM],
                               _alibi_rows(NSA_SLOPES[h] * CMP_STRIDE / scale, tq, HEAD_DIM)], axis=0)
        cmp_max.append(_stage_scores(sc_ref, h, slice(0, tq), _dot(kc, rhs), c_valid))
    imp = jnp.zeros((ns, tq), F32)
    o_cmp = []
    for h in range(HEADS):
        mx = jnp.where(cmp_max[h] > NEG_INF, cmp_max[h], 0.0)
        pb = jnp.exp2(c1 * sc_ref[h] - c1 * mx).astype(BF16)
        o_ext = _dot(vct_ext, pb)
        r = 1.0 / jnp.maximum(o_ext[HEAD_DIM:HEAD_DIM + 1], 1e-30)
        o_cmp.append(o_ext[0:HEAD_DIM] * r)
        imp = imp + _dot(ovl, pb) * r

    def select_blocks():
        j_sub = lax.broadcasted_iota(jnp.int32, (ns, 1), 0)
        j_sub_f = j_sub.astype(F32)
        cur = jnp.right_shift(t_lane, SLC_SHIFT)
        forced = (j_sub == 0) | (j_sub == cur) | (j_sub == cur - 1)
        valid = (j_sub * SLC_BLOCK) <= t_lane
        score = jnp.where(forced, 1e30, jnp.where(valid, imp, -1.0))
        sel = jnp.zeros((ns, tq), F32)
        for _ in range(top):
            mx = jnp.max(score, axis=0, keepdims=True)
            idx = jnp.min(jnp.where(score == mx, j_sub_f, float(ns)), axis=0, keepdims=True)
            pick = j_sub_f == idx
            sel = jnp.where(pick, 1.0, sel)
            score = jnp.where(pick, -2.0, score)
        unsel = jnp.where(sel > 0.5, 0.0, -MASK_BIG).astype(BF16)
        for h in range(HEADS):
            wq_ref[h, LANES + BLK_LANE0:LANES + BLK_LANE0 + ns, :] = unsel

    aux_lane = lax.broadcasted_iota(jnp.int32, (tk, LANES), 1)
    aux_blk = jnp.right_shift(lax.broadcasted_iota(jnp.int32, (tk, LANES), 0), SLC_SHIFT) + BLK_LANE0
    kpos = _key_pos_features(tk)
    ones = jnp.ones((BF16_ROWS, tk), BF16)

    n_slots = s_ref.shape[0]

    def tiles(window, kis, cols, mask_offs=None):
        base = HEADS if window else 0
        loaded = []
        for t, ki in enumerate(kis):
            start = pl.multiple_of(ki * tk, tk)
            k = k_ref[pl.ds(start, tk), :]
            if window:
                vt = vt_ref[HEAD_DIM:2 * HEAD_DIM, pl.ds(start, tk)]
                aux = kpos
                dist = rel[:, cols[t]] + (qi * tq - ki * tk)
                keep = (dist >= 0) & (dist < WINDOW)
            else:
                vt = vt_ref[0:HEAD_DIM, pl.ds(start, tk)]
                onehot = aux_lane == aux_blk + ki * (tk // SLC_BLOCK)
                aux = jnp.where(onehot, jnp.ones_like(kpos), kpos)
                keep = None
            loaded.append((jnp.concatenate([k, aux], axis=1),
                           jnp.concatenate([vt, ones], axis=0), (ki * tk).astype(F32), keep))
        items = [(t, h) for t in range(len(kis)) for h in range(HEADS)]
        s_max = {}

        def scores(i):
            t, h = items[i]
            if window:
                keep = loaded[t][3]
            else:
                keep = None if mask_offs[t] is None else rel[:, cols[t]] >= mask_offs[t]
            s_max[i] = _stage_scores(s_ref, i % n_slots, cols[t],
                                     _dot(loaded[t][0], wq_ref[base + h, :, cols[t]]), keep)

        for i in range(QK_LOOKAHEAD):
            scores(i)
        for i, (t, h) in enumerate(items):
            if i + QK_LOOKAHEAD < len(items):
                scores(i + QK_LOOKAHEAD)
            _lhs, vt_ext, key0, _keep = loaded[t]
            _flash_update_t(lambda i=i, t=t: s_ref[i % n_slots, :, cols[t]], s_max.pop(i), c1,
                            (NSA_SLOPES[h] * LOG2E) * key0, vt_ext, m_ref, acc_ref, base + h, cols[t])
            yield

    all_cols = slice(0, tq)
    diag_cols = [slice(d * tk, tq) for d in range(kpq)]

    _run_interleaved(tiles(True, [qi * kpq + d for d in range(kpq)], diag_cols))
    select_blocks()

    def loop_steps(i):
        return tiles(False, [i * kpq + d for d in range(kpq)], [all_cols] * kpq, [None] * kpq)

    def back_window():
        n_back = (WINDOW + tk - 1) // tk
        for g in range((n_back + kpq - 1) // kpq):
            backs = list(range(g * kpq + 1, min((g + 1) * kpq, n_back) + 1))
            back_cols = [slice(0, min(tq, -(-(WINDOW - (back - 1) * tk - 1) // LANES) * LANES)) for back in backs]

            @pl.when(qi > g)
            def _():
                _run_interleaved(tiles(True, [qi * kpq - back for back in backs], back_cols))

    def diag_steps():
        return tiles(False, [qi * kpq + d for d in range(kpq)], diag_cols, [d * tk for d in range(kpq)])

    def finish():
        g_t = jax.nn.sigmoid(misc_ref[...]).T
        outs = []
        for h in range(HEADS):
            r0 = GATE_COL + 3 * h
            outs.append(g_t[r0:r0 + 1] * o_cmp[h] + g_t[r0 + 1:r0 + 2] * _normalized(acc_ref[h])
                        + g_t[r0 + 2:r0 + 3] * _normalized(acc_ref[HEADS + h]))
        y = jnp.concatenate(outs, axis=0).T * _silu(z_ref[...])
        o_ref[...] = y.astype(o_ref.dtype)

    return loop_steps, back_window, diag_steps, finish


def _overlap_t(seq):
    nc = (seq - CMP_LEN) // CMP_STRIDE + 1
    ng = seq // CMP_STRIDE
    ns = seq // SLC_BLOCK
    c_start = np.arange(ng) * CMP_STRIDE
    c_end = c_start + CMP_LEN - 1
    s_start = np.arange(ns) * SLC_BLOCK
    s_end = s_start + SLC_BLOCK - 1
    ov = (c_start[None, :] <= s_end[:, None]) & (c_end[None, :] >= s_start[:, None]) & (np.arange(ng)[None, :] < nc)
    return jnp.asarray(ov.astype(np.float32), dtype=BF16)


def _nsa_operands(q, k2, vt_all, kc, vct, misc, z_all, seq, nq, tq, tk):
    ng = seq // CMP_STRIDE
    ns = seq // SLC_BLOCK
    assert tq - tk < WINDOW and tk % SLC_BLOCK == 0 and ns <= LANES - BLK_LANE0
    inputs = [_overlap_t(seq), q, k2, vt_all, kc, vct, misc, z_all]
    in_specs = [pl.BlockSpec((ns, ng), lambda b, i: (0, 0)),
                pl.BlockSpec((tq, GROUP_W), lambda b, i: (b * nq + i, 0)),
                pl.BlockSpec((seq, LANES), lambda b, i: (b, 0)),
                pl.BlockSpec((2 * HEAD_DIM, seq), lambda b, i: (NSA_VT_BLK, b)),
                pl.BlockSpec((1, ng, LANES), lambda b, i: (b, 0, 0)),
                pl.BlockSpec((1, HEAD_DIM, ng), lambda b, i: (b, 0, 0)),
                pl.BlockSpec((tq, LANES), lambda b, i: (b * nq + i, 0)),
                pl.BlockSpec((tq, GROUP_W), lambda b, i: (b * nq + i, 0))]
    scratch = [pltpu.VMEM((2 * HEADS, 1, tq), F32),
               pltpu.VMEM((2 * HEADS, ACC_ROWS, tq), F32),
               pltpu.VMEM((2 * HEADS, 2 * LANES, tq), BF16),
               pltpu.VMEM((2 * HEADS, tk, tq), F32),
               pltpu.VMEM((HEADS, ng, tq), F32)]
    return inputs, in_specs, scratch


def _mixers_kernel(*refs, n_in, n_scratch, tq, tk_nsa, tk_diff, seq, lam_init):
    bounds = np.cumsum([0] + list(n_in))
    nsa_in, diff_in, ret_in, ssd_in = (refs[a:b] for a, b in zip(bounds[:-1], bounds[1:]))
    nsa_out, diff_out, ret_out, ssd_out = refs[bounds[-1]:bounds[-1] + 4]
    sb = np.cumsum([0] + list(n_scratch)) + bounds[-1] + 4
    nsa_scr, diff_scr, rec_scr = (refs[a:b] for a, b in zip(sb[:-1], sb[1:]))
    ret_st, ssd_ext, ssd_st = rec_scr
    qi = pl.program_id(1)

    @pl.when(qi == 0)
    def _():
        ret_st[...] = jnp.zeros(ret_st.shape, F32)
        ssd_st[...] = jnp.zeros(ssd_st.shape, F32)
        ssd_ext[0:8, :] = jnp.zeros((8, CONV_CH), F32)

    chunks = list(range(ret_out.shape[0]))
    first, second = chunks[:len(chunks) // 2], chunks[len(chunks) // 2:]
    nsa_loop, nsa_back_window, nsa_diag, nsa_finish = _nsa_phases(
        *nsa_in, nsa_out, *nsa_scr, tq=tq, tk=tk_nsa, seq=seq)
    diff_loop, diff_diag, diff_finish = _diff_phases(
        *diff_in, diff_out, *diff_scr, tq=tq, tk=tk_diff, lam_init=lam_init)
    _ret_body(*ret_in, ret_out, ret_st, first)
    _ssd_body(*ssd_in, ssd_out, ssd_ext, ssd_st, first)

    def body(i, carry):
        _run_interleaved(nsa_loop(i), diff_loop(i))
        return carry

    lax.fori_loop(0, qi, body, 0)
    nsa_back_window()
    _run_interleaved(nsa_diag(), diff_diag())
    _ret_body(*ret_in, ret_out, ret_st, second)
    _ssd_body(*ssd_in, ssd_out, ssd_ext, ssd_st, second)
    nsa_finish()
    diff_finish()


def _mixers(nsa_args, diff_args, rec_args, batch, seq, layer_idx, tq, tk_nsa, tk_diff):
    nq = seq // tq
    nsa_in, nsa_specs, nsa_scratch = _nsa_operands(*nsa_args, seq, nq, tq, tk_nsa)
    diff_in, diff_specs, diff_scratch = _diff_operands(*diff_args, seq, nq, tq, tk_diff)
    ret_in, ssd_in, rec_specs, rec_out_spec, rec_scratch = _recurrent_operands(*rec_args, nq, tq)
    out_spec = pl.BlockSpec((tq, GROUP_W), lambda b, i: (b * nq + i, 0))
    m = batch * seq
    y_nsa, y_diff, y_ret, y_ssm = pl.pallas_call(
        functools.partial(_mixers_kernel, n_in=(len(nsa_in), len(diff_in), len(ret_in), len(ssd_in)),
                          n_scratch=(len(nsa_scratch), len(diff_scratch), len(rec_scratch)),
                          tq=tq, tk_nsa=tk_nsa, tk_diff=tk_diff, seq=seq,
                          lam_init=0.8 - 0.6 * math.exp(-0.3 * layer_idx)),
        grid=(batch, nq),
        in_specs=nsa_specs + diff_specs + rec_specs,
        out_specs=[out_spec, out_spec, rec_out_spec, rec_out_spec],
        out_shape=[jax.ShapeDtypeStruct((m, GROUP_W), BF16)] * 2
                  + [jax.ShapeDtypeStruct((m // SSM_CHUNK, SSM_CHUNK, GROUP_W), BF16)] * 2,
        scratch_shapes=nsa_scratch + diff_scratch + rec_scratch,
        compiler_params=_params(("parallel", "arbitrary")),
        name="mixers",
    )(*nsa_in, *diff_in, *ret_in, *ssd_in)
    return y_nsa, y_diff, y_ret.reshape(m, GROUP_W), y_ssm.reshape(m, GROUP_W)


def _ret_tables():
    c = RET_CHUNK
    h = np.arange(HEADS, dtype=np.float32)
    log_g = jnp.log(1.0 - 2.0 ** (-5.0 - jnp.asarray(h)))
    pos = jnp.arange(c, dtype=F32)
    rel = pos[:, None] - pos[None, :]
    decay = jnp.where(rel >= 0, jnp.exp(log_g[:, None, None] * jnp.maximum(rel, 0.0)), 0.0)
    xi = jnp.exp(log_g[:, None] * (pos + 1.0))
    zeta = jnp.exp(log_g[:, None] * (c - 1.0 - pos))
    chunk_decay = jnp.exp(log_g * c)
    xi_tab = jnp.repeat(xi.T, HEAD_DIM, axis=1)
    zeta_tab = jnp.repeat(zeta.T, HEAD_DIM, axis=1)
    cd_tab = jnp.repeat(chunk_decay, HEAD_DIM)[None, :]
    return decay, xi_tab, zeta_tab, cd_tab


def _ret_body(decay_ref, xi_ref, zeta_ref, cd_ref, gn_ref, q_ref, k_ref, v_ref, z_ref, o_ref, st_ref, blocks):
    xi = xi_ref[...]
    cd = cd_ref[...]
    for bb in blocks:
        q = (q_ref[bb].astype(F32) * (HEAD_DIM ** -0.5)).astype(BF16)
        k = k_ref[bb]
        v = v_ref[bb]
        kz_t = (k.astype(F32) * zeta_ref[...]).T.astype(BF16)
        outs = []
        for h in range(HEADS):
            sl = slice(h * HEAD_DIM, (h + 1) * HEAD_DIM)
            qh, kh, vh = q[:, sl], k[:, sl], v[:, sl]
            prev = st_ref[h]
            inner = (_dot_nt(qh, kh) * decay_ref[h]).astype(BF16)
            o = _dot(inner, vh) + _dot(qh, prev.astype(BF16)) * xi[:, sl]
            st_ref[h] = prev * cd[:, sl] + _dot(kz_t[sl, :], vh)
            mu = jnp.mean(o, axis=-1, keepdims=True)
            d = o - mu
            var = jnp.mean(d * d, axis=-1, keepdims=True)
            outs.append(d * lax.rsqrt(var + EPS))
        y = jnp.concatenate(outs, axis=-1) * gn_ref[...] * _silu(z_ref[bb])
        o_ref[bb] = y.astype(o_ref.dtype)


def _ssd_body(cw_ref, cb_ref, dtb_ref, a_ref, dsk_ref, nw_ref, xbc_ref, misc_ref, z_ref, o_ref,
              ext_ref, st_ref, blocks):
    L = SSM_CHUNK
    hi = lax.Precision.HIGHEST
    row = lax.broadcasted_iota(jnp.int32, (L, L), 0)
    col = lax.broadcasted_iota(jnp.int32, (L, L), 1)
    causal = row >= col
    tril = jnp.where(causal, 1.0, 0.0).astype(F32)
    dsk = dsk_ref[...]

    for bb in blocks:
        raw = xbc_ref[bb]
        ext_ref[8:8 + L, :] = raw
        conv = cb_ref[...] + raw * cw_ref[CONV_W - 1:CONV_W, :]
        for w in range(CONV_W - 1):
            shift = CONV_W - 1 - w
            conv = conv + ext_ref[8 - shift:8 - shift + L, :] * cw_ref[w:w + 1, :]
        ext_ref[0:8, :] = raw[L - 8:L, :]
        xc = _silu(conv)
        x = xc[:, 0:GROUP_W]
        bm = xc[:, GROUP_W:GROUP_W + 2 * SSM_STATE].astype(BF16)
        cm = xc[:, GROUP_W + 2 * SSM_STATE:].astype(BF16)

        dt_full = jax.nn.softplus(misc_ref[bb] + dtb_ref[...])
        da = dt_full * a_ref[...]
        cs_col = jnp.dot(tril, da, precision=hi, preferred_element_type=F32)
        cs_row = lax.dot_general(da, tril, (((0,), (1,)), ((), ())), precision=hi,
                                 preferred_element_type=F32)

        outs = []
        for h in range(HEADS):
            g = h // 2
            c0 = DT_COL + h
            sl = slice(h * HEAD_DIM, (h + 1) * HEAD_DIM)
            gs = slice(g * SSM_STATE, (g + 1) * SSM_STATE)
            cs_c = cs_col[:, c0:c0 + 1]
            cs_r = cs_row[c0:c0 + 1, :]
            cs_last = cs_col[L - 1:L, c0:c0 + 1]
            xh = x[:, sl]
            xdt = xh * dt_full[:, c0:c0 + 1]
            seg = jnp.exp(jnp.where(causal, cs_c - cs_r, NEG_INF))
            cb = _dot_nt(cm[:, gs], bm[:, gs])
            y = _dot((cb * seg).astype(BF16), xdt.astype(BF16))
            prev = st_ref[h]
            y = y + _dot(cm[:, gs], prev.astype(BF16)) * jnp.exp(cs_c)
            y = y + dsk[:, sl] * xh
            dec = jnp.exp(cs_last - cs_c)
            st_ref[h] = prev * jnp.exp(cs_last) + _dot_tn(bm[:, gs], (xdt * dec).astype(BF16))
            outs.append(y)
        y = jnp.concatenate(outs, axis=-1) * _silu(z_ref[bb])
        ms = jnp.mean(y * y, axis=-1, keepdims=True)
        o_ref[bb] = (y * lax.rsqrt(ms + EPS) * nw_ref[...]).astype(o_ref.dtype)


def _recurrent_operands(ret_qkv, xbc, misc, z_all, gn_w, conv_w, conv_b, dt_bias, a_log, d_skip, norm_w, nq, tq):
    L = SSM_CHUNK
    assert RET_CHUNK == L and tq % L == 0
    cpt = tq // L
    decay, xi_tab, zeta_tab, cd_tab = _ret_tables()
    dtb = jnp.zeros((1, 128), F32).at[0, DT_COL:DT_COL + HEADS].set(dt_bias)
    a_full = jnp.zeros((1, 128), F32).at[0, DT_COL:DT_COL + HEADS].set(-jnp.exp(a_log))
    dsk = jnp.repeat(d_skip, HEAD_DIM)[None, :]

    def full(shape):
        return pl.BlockSpec(shape, lambda b, i: (0,) * len(shape))

    def blk(width, col):
        return pl.BlockSpec((cpt, L, width), lambda b, i: (b * nq + i, 0, col))

    def chunked(a):
        return a.reshape(-1, L, a.shape[-1])

    qkv3, z3 = chunked(ret_qkv), chunked(z_all)
    ret_in = [decay, xi_tab, zeta_tab, cd_tab, gn_w.reshape(1, GROUP_W), qkv3, qkv3, qkv3, z3]
    ret_specs = [full((HEADS, L, L)), full((L, GROUP_W)), full((L, GROUP_W)), full((1, GROUP_W)),
                 full((1, GROUP_W)), blk(GROUP_W, 0), blk(GROUP_W, 1), blk(GROUP_W, 2), blk(GROUP_W, 2)]
    ssd_in = [conv_w, conv_b.reshape(1, CONV_CH), dtb, a_full, dsk, norm_w.reshape(1, GROUP_W),
              chunked(xbc), chunked(misc), z3]
    ssd_specs = [full((CONV_W, CONV_CH)), full((1, CONV_CH)), full((1, 128)), full((1, 128)),
                 full((1, GROUP_W)), full((1, GROUP_W)), blk(CONV_CH, 0), blk(128, 0), blk(GROUP_W, 3)]
    scratch = [pltpu.VMEM((HEADS, HEAD_DIM, HEAD_DIM), F32),
               pltpu.VMEM((8 + L, CONV_CH), F32),
               pltpu.VMEM((HEADS, SSM_STATE, HEAD_DIM), F32)]
    return ret_in, ssd_in, ret_specs + ssd_specs, blk(GROUP_W, 0), scratch


def _pick_tile(n, pref):
    t = pref
    while n % t:
        t //= 2
    return t


def kernel(x, norm_w, w_in, w_out, nsa_pe_k, nsa_pe_v, nsa_w_ck1, nsa_w_ck2, nsa_w_cv1, nsa_w_cv2,
           diff_lam_q1, diff_lam_k1, diff_lam_q2, diff_lam_k2, diff_subln_w, ret_gn_w,
           ssm_conv_w, ssm_conv_b, ssm_dt_bias, ssm_A_log, ssm_D, ssm_norm_w, final_norm_w):
    batch, seq, _ = x.shape
    depth = w_in.shape[0]
    m = batch * seq
    tm = _pick_tile(m, 512)
    tq = _pick_tile(seq, 512)
    tk = _pick_tile(seq, 256)
    tk_diff = _pick_tile(seq, 512)
    w_in_b = w_in.astype(BF16)
    w_r = _relayout_w_in(w_in_b)
    w_t = _relayout_w_in_t(w_in_b)
    w_out_b = w_out.astype(BF16)
    x2d = x.reshape(m, D_MODEL)
    projected = _in_proj(x2d, norm_w[0], w_r[0], w_t[0], tm)
    for i in range(depth):
        nsa_q, nsa_k2, nsa_cmp, misc, z_all, diff_qk, ret_qkv, xbc, vt_all = projected
        kc, vct = _nsa_compress(nsa_cmp, nsa_pe_k[i], nsa_pe_v[i], nsa_w_ck1[i], nsa_w_ck2[i],
                                nsa_w_cv1[i], nsa_w_cv2[i], batch, seq)
        lam_p = jnp.stack([diff_lam_q1[i], diff_lam_k1[i], diff_lam_q2[i], diff_lam_k2[i]])
        ys = _mixers((nsa_q, nsa_k2, vt_all, kc, vct, misc, z_all),
                     (diff_qk, vt_all, z_all, lam_p, diff_subln_w[i]),
                     (ret_qkv, xbc, misc, z_all, ret_gn_w[i], ssm_conv_w[i], ssm_conv_b[i], ssm_dt_bias[i],
                      ssm_A_log[i], ssm_D[i], ssm_norm_w[i]),
                     batch, seq, i, tq, tk, tk_diff)
        if i + 1 < depth:
            x2d, *projected = _out_in_proj(ys, w_out_b[i], x2d, norm_w[i + 1], w_r[i + 1], w_t[i + 1], tm)
        else:
            x2d = _out_proj(ys, w_out_b[i], x2d, final_norm_w, tm)
    return x2d.reshape(batch, seq, D_MODEL)
```

```python
import functools
import math

import numpy as np
import jax
import jax.numpy as jnp
from jax import lax
from jax.experimental import pallas as pl
from jax.experimental.pallas import tpu as pltpu

F32 = jnp.float32
BF16 = jnp.bfloat16
NEG_INF = float("-inf")
LOG2E = 1.4426950408889634

D_MODEL = 1024
DEPTH = 4
GROUP_W = 256
HEADS = 4
HEAD_DIM = 64
EPS = 1e-6
CMP_LEN = 32
CMP_STRIDE = 16
CMP_HIDDEN = 256
SLC_BLOCK = 64
SLC_SHIFT = 6
SLC_TOPK = 16
WINDOW = 512
DIFF_QK_DIM = 32
RET_CHUNK = 128
SSM_STATE = 128
SSM_CHUNK = 128
CONV_W = 4
CONV_CH = 768
N_ALIBI_HEADS = 8
LANES = 128
BF16_ROWS = 16
ACC_ROWS = HEAD_DIM + BF16_ROWS
QK_LOOKAHEAD = 2
ALIBI_ROWS = 3
POS_RADIX = 256
MASK_BIG = 2.0 ** 100
BLK_LANE0 = 64

IN_LAYOUT = (
    ("nsa_q", 256), ("nsa_k_cmp", 64), ("nsa_v_cmp", 64), ("nsa_k_slc", 64), ("nsa_v_slc", 64),
    ("nsa_k_win", 64), ("nsa_v_win", 64), ("nsa_gate", 12), ("nsa_z", 256),
    ("diff_q", 256), ("diff_k", 256), ("diff_v", 256), ("diff_z", 256),
    ("ret_q", 256), ("ret_k", 256), ("ret_v", 256), ("ret_z", 256),
    ("ssm_z", 256), ("ssm_xbc", 768), ("ssm_dt", 4),
)
IN_OFF = {}
_o = 0
for _n, _w in IN_LAYOUT:
    IN_OFF[_n] = (_o, _w)
    _o += _w
IN_W = _o

GATE_COL = 0
DT_COL = 12
IN_OUTPUTS = (
    ("nsa_q", BF16, ("nsa_q",), 256),
    ("nsa_k2", BF16, ("nsa_k_slc", "nsa_k_win"), 128),
    ("nsa_cmp", F32, ("nsa_k_cmp", "nsa_v_cmp"), 128),
    ("misc", F32, ("nsa_gate", "ssm_dt"), 128),
    ("z_all", F32, ("nsa_z", "diff_z", "ret_z", "ssm_z"), 1024),
    ("diff_qk", BF16, ("diff_q", "diff_k"), 512),
    ("ret_qkv", BF16, ("ret_q", "ret_k", "ret_v"), 768),
    ("xbc", F32, ("ssm_xbc",), 768),
)
IN_T_SRC = ("diff_v", "nsa_v_slc", "nsa_v_win")
IN_T_ROWS = 384
DIFF_VT_BLK = 0
NSA_VT_BLK = 2
IN_SEGS = []
_o = 0
for _n, _dt, _src, _w in IN_OUTPUTS:
    IN_SEGS.append((_o, _o + _w))
    _o += _w
IN_WP = _o

VMEM_LIMIT = 56 * 1024 * 1024


def _alibi_slopes():
    return [2.0 ** (-8.0 * (i + 1) / N_ALIBI_HEADS) for i in range(N_ALIBI_HEADS)]


NSA_SLOPES = _alibi_slopes()[0::2]
DIFF_SLOPES = _alibi_slopes()[1::2]


def _silu(x):
    return x * jax.nn.sigmoid(x)


def _dot(a, b):
    return jnp.dot(a, b, preferred_element_type=F32)


def _dot_nt(a, b):
    return lax.dot_general(a, b, (((1,), (1,)), ((), ())), preferred_element_type=F32)


def _dot_tn(a, b):
    return lax.dot_general(a, b, (((0,), (0,)), ((), ())), preferred_element_type=F32)


def _params(sem):
    return pltpu.CompilerParams(dimension_semantics=sem, vmem_limit_bytes=VMEM_LIMIT)


def _relayout_w_in(w_in):
    cols = []
    for _n, _dt, src, width in IN_OUTPUTS:
        used = 0
        for s in src:
            off, w = IN_OFF[s]
            cols.append(w_in[:, :, off:off + w])
            used += w
        if used < width:
            cols.append(jnp.zeros(w_in.shape[:2] + (width - used,), w_in.dtype))
    return jnp.concatenate(cols, axis=-1)


def _relayout_w_in_t(w_in):
    cols = [w_in[:, :, IN_OFF[s][0]:IN_OFF[s][0] + IN_OFF[s][1]] for s in IN_T_SRC]
    return jnp.swapaxes(jnp.concatenate(cols, axis=-1), 1, 2)


def _norm_project(x, nw_ref, w_ref, wt_ref, out_refs):
    ms = jnp.mean(x * x, axis=-1, keepdims=True)
    h = (x * lax.rsqrt(ms + EPS) * nw_ref[...]).astype(BF16)
    for ref, (a, b) in zip(out_refs[:-1], IN_SEGS):
        ref[...] = _dot(h, w_ref[:, a:b]).astype(ref.dtype)
    out_refs[-1][...] = _dot_nt(wt_ref[...], h).astype(BF16)


def _in_proj_kernel(x_ref, nw_ref, w_ref, wt_ref, *out_refs):
    _norm_project(x_ref[...], nw_ref, w_ref, wt_ref, out_refs)


def _in_proj_outputs(m, tm):
    out_shape = [jax.ShapeDtypeStruct((m, b - a), dt) for (_n, dt, _s, _w), (a, b) in zip(IN_OUTPUTS, IN_SEGS)]
    out_specs = [pl.BlockSpec((tm, b - a), lambda i: (i, 0)) for (a, b) in IN_SEGS]
    out_shape.append(jax.ShapeDtypeStruct((IN_T_ROWS, m), BF16))
    out_specs.append(pl.BlockSpec((IN_T_ROWS, tm), lambda i: (0, i)))
    return out_shape, out_specs


def _in_proj(x2d, norm_w, w_r, w_t, tm):
    m = x2d.shape[0]
    out_shape, out_specs = _in_proj_outputs(m, tm)
    return pl.pallas_call(
        _in_proj_kernel,
        grid=(m // tm,),
        in_specs=[pl.BlockSpec((tm, D_MODEL), lambda i: (i, 0)),
                  pl.BlockSpec((1, D_MODEL), lambda i: (0, 0)),
                  pl.BlockSpec((D_MODEL, IN_WP), lambda i: (0, 0)),
                  pl.BlockSpec((IN_T_ROWS, D_MODEL), lambda i: (0, 0))],
        out_specs=out_specs,
        out_shape=out_shape,
        compiler_params=_params(("parallel",)),
        name="in_proj",
    )(x2d, norm_w.reshape(1, D_MODEL), w_r, w_t)


def _out_proj_kernel(y0_ref, y1_ref, y2_ref, y3_ref, w_ref, x_ref, fw_ref, o_ref):
    acc = x_ref[...]
    for g, y_ref in enumerate((y0_ref, y1_ref, y2_ref, y3_ref)):
        acc = acc + _dot(y_ref[...], w_ref[g * GROUP_W:(g + 1) * GROUP_W, :])
    ms = jnp.mean(acc * acc, axis=-1, keepdims=True)
    o_ref[...] = acc * lax.rsqrt(ms + EPS) * fw_ref[...]


def _out_proj(ys, w_out_b, x2d, final_w, tm):
    m = x2d.shape[0]
    yspec = pl.BlockSpec((tm, GROUP_W), lambda i: (i, 0))
    return pl.pallas_call(
        _out_proj_kernel,
        grid=(m // tm,),
        in_specs=[yspec, yspec, yspec, yspec,
                  pl.BlockSpec((D_MODEL, D_MODEL), lambda i: (0, 0)),
                  pl.BlockSpec((tm, D_MODEL), lambda i: (i, 0)),
                  pl.BlockSpec((1, D_MODEL), lambda i: (0, 0))],
        out_specs=pl.BlockSpec((tm, D_MODEL), lambda i: (i, 0)),
        out_shape=jax.ShapeDtypeStruct((m, D_MODEL), F32),
        compiler_params=_params(("parallel",)),
        name="out_proj",
    )(*ys, w_out_b, x2d, final_w.reshape(1, D_MODEL))


def _out_in_proj_kernel(y0_ref, y1_ref, y2_ref, y3_ref, wo_ref, x_ref, nw_ref, w_ref, wt_ref, xo_ref, *out_refs):
    acc = x_ref[...]
    for g, y_ref in enumerate((y0_ref, y1_ref, y2_ref, y3_ref)):
        acc = acc + _dot(y_ref[...], wo_ref[g * GROUP_W:(g + 1) * GROUP_W, :])
    xo_ref[...] = acc
    _norm_project(acc, nw_ref, w_ref, wt_ref, out_refs)


def _out_in_proj(ys, w_out_b, x2d, norm_w, w_r, w_t, tm):
    m = x2d.shape[0]
    yspec = pl.BlockSpec((tm, GROUP_W), lambda i: (i, 0))
    xspec = pl.BlockSpec((tm, D_MODEL), lambda i: (i, 0))
    out_shape, out_specs = _in_proj_outputs(m, tm)
    return pl.pallas_call(
        _out_in_proj_kernel,
        grid=(m // tm,),
        in_specs=[yspec, yspec, yspec, yspec,
                  pl.BlockSpec((D_MODEL, D_MODEL), lambda i: (0, 0)),
                  xspec,
                  pl.BlockSpec((1, D_MODEL), lambda i: (0, 0)),
                  pl.BlockSpec((D_MODEL, IN_WP), lambda i: (0, 0)),
                  pl.BlockSpec((IN_T_ROWS, D_MODEL), lambda i: (0, 0))],
        out_specs=[xspec] + out_specs,
        out_shape=[jax.ShapeDtypeStruct((m, D_MODEL), F32)] + out_shape,
        compiler_params=_params(("parallel",)),
        name="out_in_proj",
    )(*ys, w_out_b, x2d, norm_w.reshape(1, D_MODEL), w_r, w_t)


def _run_interleaved(*pipelines):
    live = list(pipelines)
    while live:
        for g in list(live):
            if next(g, _DONE) is _DONE:
                live.remove(g)


_DONE = object()


def _stage_scores(s_ref, slot, cols, s, keep=None):
    if keep is not None:
        s = jnp.where(keep, s, NEG_INF)
    s_ref[slot, :, cols] = s
    return jnp.max(s, axis=0, keepdims=True)


def _flash_update_t(s, s_max, c1, shift, vt_ext, m_ref, acc_ref, idx, cols):
    m_old = m_ref[idx, :, cols]
    m_new = jnp.maximum(m_old, c1 * s_max + shift)
    alpha = jnp.exp2(m_old - m_new)
    p = jnp.exp2(c1 * s() - (m_new - shift))
    acc_ref[idx, :, cols] = alpha * acc_ref[idx, :, cols] + _dot(vt_ext, p.astype(BF16))
    m_ref[idx, :, cols] = m_new


def _bf16_pieces(x, n=3):
    out = []
    for _ in range(n):
        p = float(np.asarray(x, np.float32).astype(BF16).astype(np.float32))
        out.append(p)
        x = x - p
    return out


def _key_pos_features(tk):
    lane = lax.broadcasted_iota(jnp.int32, (tk, LANES), 1)
    row = lax.broadcasted_iota(jnp.int32, (tk, LANES), 0)
    out = jnp.zeros((tk, LANES), jnp.int32)
    for d in range(-(-tk // POS_RADIX)):
        digit = jnp.bitwise_and(jnp.right_shift(row, 8 * d), POS_RADIX - 1)
        out = jnp.where((lane >= ALIBI_ROWS * d) & (lane < ALIBI_ROWS * (d + 1)), digit, out)
    return out.astype(F32).astype(BF16)


def _alibi_rows(beta, tq, rows=LANES, tk=POS_RADIX):
    row = lax.broadcasted_iota(jnp.int32, (rows, tq), 0)
    out = jnp.zeros((rows, tq), F32)
    for d in range(-(-tk // POS_RADIX)):
        for r, piece in enumerate(_bf16_pieces(beta * POS_RADIX ** d, ALIBI_ROWS)):
            out = jnp.where(row == ALIBI_ROWS * d + r, piece, out)
    return out.astype(BF16)


def _normalized(acc):
    return acc[0:HEAD_DIM] / jnp.maximum(acc[HEAD_DIM:HEAD_DIM + 1], 1e-30)


def _rel_t(tk, tq):
    return lax.broadcasted_iota(jnp.int32, (tk, tq), 1) - lax.broadcasted_iota(jnp.int32, (tk, tq), 0)


def _diff_phases(lam_ref, sw_ref, q_ref, k_ref, vt_ref, z_ref, o_ref, m_ref, acc_ref, wq_ref, s_ref,
                 *, tq, tk, lam_init):
    qi = pl.program_id(1)
    kpq = tq // tk
    scale = DIFF_QK_DIM ** -0.5
    c1 = scale * LOG2E
    m_ref[...] = jnp.full(m_ref.shape, NEG_INF, F32)
    acc_ref[...] = jnp.zeros(acc_ref.shape, F32)
    rel = _rel_t(tk, tq)

    qt = q_ref[...].astype(F32).T.astype(BF16)
    row = lax.broadcasted_iota(jnp.int32, (LANES, tq), 0)
    for j in range(2 * HEADS):
        g, r0 = divmod(j * DIFF_QK_DIM, LANES)
        qg = qt[g * LANES:(g + 1) * LANES]
        wq_ref[j, 0:LANES, :] = jnp.where((row >= r0) & (row < r0 + DIFF_QK_DIM), qg, jnp.zeros_like(qg))
        wq_ref[j, LANES:2 * LANES, :] = _alibi_rows(DIFF_SLOPES[j // 2] / scale, tq, tk=tk)

    ones = jnp.ones((BF16_ROWS, tk), BF16)
    kpos = _key_pos_features(tk)

    n_maps = 2 * HEADS
    n_slots = s_ref.shape[0]

    def tiles(kis, mask_offs):
        loaded = []
        for ki in kis:
            start = pl.multiple_of(ki * tk, tk)
            loaded.append((k_ref[pl.ds(start, tk), :],
                           vt_ref[:, pl.ds(start, tk)],
                           (ki * tk).astype(F32)))
        items = [(t, j) for t in range(len(kis)) for j in range(n_maps)]
        cols = slice(0, tq)

        def scores(i):
            t, j = items[i]
            g = (j * DIFF_QK_DIM) // LANES
            lhs = jnp.concatenate([loaded[t][0][:, g * LANES:(g + 1) * LANES], kpos], axis=1)
            s_ref[i % n_slots] = _dot(lhs, wq_ref[j])

        def staged(i):
            t = items[i][0]
            if mask_offs[t] is None:
                return lambda: s_ref[i % n_slots]
            return lambda: jnp.where(rel >= mask_offs[t], s_ref[i % n_slots], NEG_INF)

        for i in range(QK_LOOKAHEAD):
            scores(i)
        for i, (t, j) in enumerate(items):
            h = j // 2
            if i + QK_LOOKAHEAD < len(items):
                scores(i + QK_LOOKAHEAD)
            _k, vt, key0 = loaded[t]
            vt_ext = jnp.concatenate([vt[h * HEAD_DIM:(h + 1) * HEAD_DIM], ones], axis=0)
            s = staged(i)
            _flash_update_t(s, jnp.max(s(), axis=0, keepdims=True), c1, (DIFF_SLOPES[h] * LOG2E) * key0,
                            vt_ext, m_ref, acc_ref, j, cols)
            yield

    def loop_steps(i):
        return tiles([i * kpq + d for d in range(kpq)], [None] * kpq)

    def diag_steps():
        return tiles([qi * kpq + d for d in range(kpq)], [d * tk for d in range(kpq)])

    def finish():
        lp = lam_ref[...]
        lam = (jnp.exp(jnp.sum(lp[0:1] * lp[1:2], axis=-1, keepdims=True))
               - jnp.exp(jnp.sum(lp[2:3] * lp[3:4], axis=-1, keepdims=True)) + lam_init)
        sw = sw_ref[...]
        o_t = jnp.concatenate([_normalized(acc_ref[2 * h]) - lam * _normalized(acc_ref[2 * h + 1])
                               for h in range(HEADS)], axis=0)
        o = o_t.T
        outs = []
        for h in range(HEADS):
            oh = o[:, h * HEAD_DIM:(h + 1) * HEAD_DIM]
            ms = jnp.mean(oh * oh, axis=-1, keepdims=True)
            outs.append(oh * lax.rsqrt(ms + EPS) * sw * (1.0 - lam_init))
        y = jnp.concatenate(outs, axis=-1) * _silu(z_ref[...])
        o_ref[...] = y.astype(o_ref.dtype)

    return loop_steps, diag_steps, finish


def _diff_operands(qk, vt_all, z_all, lam_p, subln_w, seq, nq, tq, tk):
    inputs = [lam_p, subln_w.reshape(1, HEAD_DIM), qk, qk, vt_all, z_all]
    in_specs = [pl.BlockSpec((4, DIFF_QK_DIM), lambda b, i: (0, 0)),
                pl.BlockSpec((1, HEAD_DIM), lambda b, i: (0, 0)),
                pl.BlockSpec((tq, GROUP_W), lambda b, i: (b * nq + i, 0)),
                pl.BlockSpec((seq, GROUP_W), lambda b, i: (b, 1)),
                pl.BlockSpec((GROUP_W, seq), lambda b, i: (DIFF_VT_BLK, b)),
                pl.BlockSpec((tq, GROUP_W), lambda b, i: (b * nq + i, 1))]
    scratch = [pltpu.VMEM((2 * HEADS, 1, tq), F32),
               pltpu.VMEM((2 * HEADS, ACC_ROWS, tq), F32),
               pltpu.VMEM((2 * HEADS, 2 * LANES, tq), BF16),
               pltpu.VMEM((QK_LOOKAHEAD + 1, tk, tq), F32)]
    return inputs, in_specs, scratch


def _nsa_compress_kernel(cmp_ref, pe_ref, w1_ref, wk2_ref, wv2t_ref, kc_ref, vct_ref):
    ng = kc_ref.shape[1]
    top = jnp.zeros((ng, 2 * CMP_HIDDEN), F32)
    bot = jnp.zeros((ng, 2 * CMP_HIDDEN), F32)
    for p in range(CMP_STRIDE):
        x = cmp_ref[pl.ds(p, ng, stride=CMP_STRIDE), :]
        top = top + _dot((x + pe_ref[p:p + 1, :]).astype(BF16), w1_ref[p])
        bot = bot + _dot((x + pe_ref[CMP_STRIDE + p:CMP_STRIDE + p + 1, :]).astype(BF16), w1_ref[CMP_STRIDE + p])
    hid = _silu(top + pltpu.roll(bot, ng - 1, 0)).astype(BF16)
    hk, hv = hid[:, 0:CMP_HIDDEN], hid[:, CMP_HIDDEN:]
    kc = _dot(hk, wk2_ref[...])
    lane = lax.broadcasted_iota(jnp.int32, kc.shape, 1)
    blk = lax.broadcasted_iota(jnp.int32, kc.shape, 0).astype(F32)
    kc = jnp.where((lane >= HEAD_DIM) & (lane < HEAD_DIM + ALIBI_ROWS), blk, kc)
    kc_ref[0] = kc.astype(kc_ref.dtype)
    vct_ref[0] = _dot_nt(wv2t_ref[...], hv).astype(vct_ref.dtype)


def _nsa_compress(cmp2d, pe_k, pe_v, w_ck1, w_ck2, w_cv1, w_cv2, batch, seq):
    ng = seq // CMP_STRIDE
    wk = w_ck1.reshape(CMP_LEN, HEAD_DIM, CMP_HIDDEN)
    wv = w_cv1.reshape(CMP_LEN, HEAD_DIM, CMP_HIDDEN)
    w1 = jnp.concatenate([jnp.pad(wk, ((0, 0), (0, 0), (0, CMP_HIDDEN))),
                          jnp.pad(wv, ((0, 0), (0, 0), (CMP_HIDDEN, 0)))], axis=1).astype(BF16)
    pe = jnp.concatenate([pe_k, pe_v], axis=1)

    def full(shape):
        return pl.BlockSpec(shape, lambda b: (0,) * len(shape))

    return pl.pallas_call(
        _nsa_compress_kernel,
        grid=(batch,),
        in_specs=[pl.BlockSpec((seq, LANES), lambda b: (b, 0)), full((CMP_LEN, LANES)),
                  full((CMP_LEN, LANES, 2 * CMP_HIDDEN)), full((CMP_HIDDEN, LANES)),
                  full((HEAD_DIM, CMP_HIDDEN))],
        out_specs=[pl.BlockSpec((1, ng, LANES), lambda b: (b, 0, 0)),
                   pl.BlockSpec((1, HEAD_DIM, ng), lambda b: (b, 0, 0))],
        out_shape=[jax.ShapeDtypeStruct((batch, ng, LANES), BF16),
                   jax.ShapeDtypeStruct((batch, HEAD_DIM, ng), BF16)],
        compiler_params=_params(("parallel",)),
        name="nsa_compress",
    )(cmp2d, pe, w1, jnp.pad(w_ck2, ((0, 0), (0, LANES - HEAD_DIM))).astype(BF16), w_cv2.T.astype(BF16))


def _nsa_phases(ovl_ref, q_ref, k_ref, vt_ref, kc_ref, vct_ref, misc_ref, z_ref, o_ref,
                m_ref, acc_ref, wq_ref, s_ref, sc_ref, *, tq, tk, seq):
    qi = pl.program_id(1)
    kpq = tq // tk
    scale = HEAD_DIM ** -0.5
    c1 = scale * LOG2E
    ng = seq // CMP_STRIDE
    ns = seq // SLC_BLOCK
    top = min(SLC_TOPK, ns)
    m_ref[...] = jnp.full(m_ref.shape, NEG_INF, F32)
    acc_ref[...] = jnp.zeros(acc_ref.shape, F32)
    rel = _rel_t(tk, tq)

    qt = q_ref[...].astype(F32).T.astype(BF16)
    zeros_q = jnp.zeros((HEAD_DIM, tq), BF16)
    for h in range(HEADS):
        qh = qt[h * HEAD_DIM:(h + 1) * HEAD_DIM]
        alibi = _alibi_rows(NSA_SLOPES[h] / scale, tq, tk=tk)
        wq_ref[h, 0:LANES, :] = jnp.concatenate([qh, zeros_q], axis=0)
        wq_ref[h, LANES:2 * LANES, :] = alibi
        wq_ref[HEADS + h, 0:LANES, :] = jnp.concatenate([zeros_q, qh], axis=0)
        wq_ref[HEADS + h, LANES:2 * LANES, :] = alibi

    t_lane = qi * tq + lax.broadcasted_iota(jnp.int32, (1, tq), 1)
    n_sub = lax.broadcasted_iota(jnp.int32, (ng, 1), 0)
    c_valid = (n_sub * CMP_STRIDE + (CMP_LEN - 1)) <= t_lane
    kc = kc_ref[0]
    vct_ext = jnp.concatenate([vct_ref[0], jnp.ones((BF16_ROWS, ng), BF16)], axis=0)
    ovl = ovl_ref[...]
    cmp_max = []
    for h in range(HEADS):
        rhs = jnp.concatenate([qt[h * HEAD_DIM:(h + 1) * HEAD_DIM],
                               _alibi_rows(NSA_SLOPES[h] * CMP_STRIDE / scale, tq, HEAD_DIM)], axis=0)
        cmp_max.append(_stage_scores(sc_ref, h, slice(0, tq), _dot(kc, rhs), c_valid))
    imp = jnp.zeros((ns, tq), F32)
    o_cmp = []
    for h in range(HEADS):
        mx = jnp.where(cmp_max[h] > NEG_INF, cmp_max[h], 0.0)
        pb = jnp.exp2(c1 * sc_ref[h] - c1 * mx).astype(BF16)
        o_ext = _dot(vct_ext, pb)
        r = 1.0 / jnp.maximum(o_ext[HEAD_DIM:HEAD_DIM + 1], 1e-30)
        o_cmp.append(o_ext[0:HEAD_DIM] * r)
        imp = imp + _dot(ovl, pb) * r

    def select_blocks():
        j_sub = lax.broadcasted_iota(jnp.int32, (ns, 1), 0)
        j_sub_f = j_sub.astype(F32)
        cur = jnp.right_shift(t_lane, SLC_SHIFT)
        forced = (j_sub == 0) | (j_sub == cur) | (j_sub == cur - 1)
        valid = (j_sub * SLC_BLOCK) <= t_lane
        score = jnp.where(forced, 1e30, jnp.where(valid, imp, -1.0))
        sel = jnp.zeros((ns, tq), F32)
        for _ in range(top):
            mx = jnp.max(score, axis=0, keepdims=True)
            idx = jnp.min(jnp.where(score == mx, j_sub_f, float(ns)), axis=0, keepdims=True)
            pick = j_sub_f == idx
            sel = jnp.where(pick, 1.0, sel)
            score = jnp.where(pick, -2.0, score)
        unsel = jnp.where(sel > 0.5, 0.0, -MASK_BIG).astype(BF16)
        for h in range(HEADS):
            wq_ref[h, LANES + BLK_LANE0:LANES + BLK_LANE0 + ns, :] = unsel

    aux_lane = lax.broadcasted_iota(jnp.int32, (tk, LANES), 1)
    aux_blk = jnp.right_shift(lax.broadcasted_iota(jnp.int32, (tk, LANES), 0), SLC_SHIFT) + BLK_LANE0
    kpos = _key_pos_features(tk)
    ones = jnp.ones((BF16_ROWS, tk), BF16)

    n_slots = s_ref.shape[0]

    def tiles(window, kis, cols, mask_offs=None):
        base = HEADS if window else 0
        loaded = []
        for t, ki in enumerate(kis):
            start = pl.multiple_of(ki * tk, tk)
            k = k_ref[pl.ds(start, tk), :]
            if window:
                vt = vt_ref[HEAD_DIM:2 * HEAD_DIM, pl.ds(start, tk)]
                aux = kpos
                dist = rel[:, cols[t]] + (qi * tq - ki * tk)
                keep = (dist >= 0) & (dist < WINDOW)
            else:
                vt = vt_ref[0:HEAD_DIM, pl.ds(start, tk)]
                onehot = aux_lane == aux_blk + ki * (tk // SLC_BLOCK)
                aux = jnp.where(onehot, jnp.ones_like(kpos), kpos)
                keep = None
            loaded.append((jnp.concatenate([k, aux], axis=1),
                           jnp.concatenate([vt, ones], axis=0), (ki * tk).astype(F32), keep))
        items = [(t, h) for t in range(len(kis)) for h in range(HEADS)]
        s_max = {}

        def scores(i):
            t, h = items[i]
            if window:
                keep = loaded[t][3]
            else:
                keep = None if mask_offs[t] is None else rel[:, cols[t]] >= mask_offs[t]
            s_max[i] = _stage_scores(s_ref, i % n_slots, cols[t],
                                     _dot(loaded[t][0], wq_ref[base + h, :, cols[t]]), keep)

        for i in range(QK_LOOKAHEAD):
            scores(i)
        for i, (t, h) in enumerate(items):
            if i + QK_LOOKAHEAD < len(items):
                scores(i + QK_LOOKAHEAD)
            _lhs, vt_ext, key0, _keep = loaded[t]
            _flash_update_t(lambda i=i, t=t: s_ref[i % n_slots, :, cols[t]], s_max.pop(i), c1,
                            (NSA_SLOPES[h] * LOG2E) * key0, vt_ext, m_ref, acc_ref, base + h, cols[t])
            yield

    all_cols = slice(0, tq)
    diag_cols = [slice(d * tk, tq) for d in range(kpq)]

    _run_interleaved(tiles(True, [qi * kpq + d for d in range(kpq)], diag_cols))
    select_blocks()

    def loop_steps(i):
        return tiles(False, [i * kpq + d for d in range(kpq)], [all_cols] * kpq, [None] * kpq)

    def back_window():
        n_back = (WINDOW + tk - 1) // tk
        for g in range((n_back + kpq - 1) // kpq):
            backs = list(range(g * kpq + 1, min((g + 1) * kpq, n_back) + 1))
            back_cols = [slice(0, min(tq, -(-(WINDOW - (back - 1) * tk - 1) // LANES) * LANES)) for back in backs]

            @pl.when(qi > g)
            def _():
                _run_interleaved(tiles(True, [qi * kpq - back for back in backs], back_cols))

    def diag_steps():
        return tiles(False, [qi * kpq + d for d in range(kpq)], diag_cols, [d * tk for d in range(kpq)])

    def finish():
        g_t = jax.nn.sigmoid(misc_ref[...]).T
        outs = []
        for h in range(HEADS):
            r0 = GATE_COL + 3 * h
            outs.append(g_t[r0:r0 + 1] * o_cmp[h] + g_t[r0 + 1:r0 + 2] * _normalized(acc_ref[h])
                        + g_t[r0 + 2:r0 + 3] * _normalized(acc_ref[HEADS + h]))
        y = jnp.concatenate(outs, axis=0).T * _silu(z_ref[...])
        o_ref[...] = y.astype(o_ref.dtype)

    return loop_steps, back_window, diag_steps, finish


def _overlap_t(seq):
    nc = (seq - CMP_LEN) // CMP_STRIDE + 1
    ng = seq // CMP_STRIDE
    ns = seq // SLC_BLOCK
    c_start = np.arange(ng) * CMP_STRIDE
    c_end = c_start + CMP_LEN - 1
    s_start = np.arange(ns) * SLC_BLOCK
    s_end = s_start + SLC_BLOCK - 1
    ov = (c_start[None, :] <= s_end[:, None]) & (c_end[None, :] >= s_start[:, None]) & (np.arange(ng)[None, :] < nc)
    return jnp.asarray(ov.astype(np.float32), dtype=BF16)


def _nsa_operands(q, k2, vt_all, kc, vct, misc, z_all, seq, nq, tq, tk):
    ng = seq // CMP_STRIDE
    ns = seq // SLC_BLOCK
    assert tq - tk < WINDOW and tk % SLC_BLOCK == 0 and ns <= LANES - BLK_LANE0
    inputs = [_overlap_t(seq), q, k2, vt_all, kc, vct, misc, z_all]
    in_specs = [pl.BlockSpec((ns, ng), lambda b, i: (0, 0)),
                pl.BlockSpec((tq, GROUP_W), lambda b, i: (b * nq + i, 0)),
                pl.BlockSpec((seq, LANES), lambda b, i: (b, 0)),
                pl.BlockSpec((2 * HEAD_DIM, seq), lambda b, i: (NSA_VT_BLK, b)),
                pl.BlockSpec((1, ng, LANES), lambda b, i: (b, 0, 0)),
                pl.BlockSpec((1, HEAD_DIM, ng), lambda b, i: (b, 0, 0)),
                pl.BlockSpec((tq, LANES), lambda b, i: (b * nq + i, 0)),
                pl.BlockSpec((tq, GROUP_W), lambda b, i: (b * nq + i, 0))]
    scratch = [pltpu.VMEM((2 * HEADS, 1, tq), F32),
               pltpu.VMEM((2 * HEADS, ACC_ROWS, tq), F32),
               pltpu.VMEM((2 * HEADS, 2 * LANES, tq), BF16),
               pltpu.VMEM((2 * HEADS, tk, tq), F32),
               pltpu.VMEM((HEADS, ng, tq), F32)]
    return inputs, in_specs, scratch


def _mixers_kernel(*refs, n_in, n_scratch, tq, tk_nsa, tk_diff, seq, lam_init):
    bounds = np.cumsum([0] + list(n_in))
    nsa_in, diff_in, ret_in, ssd_in = (refs[a:b] for a, b in zip(bounds[:-1], bounds[1:]))
    nsa_out, diff_out, ret_out, ssd_out = refs[bounds[-1]:bounds[-1] + 4]
    sb = np.cumsum([0] + list(n_scratch)) + bounds[-1] + 4
    nsa_scr, diff_scr, rec_scr = (refs[a:b] for a, b in zip(sb[:-1], sb[1:]))
    ret_st, ssd_ext, ssd_st = rec_scr
    qi = pl.program_id(1)

    @pl.when(qi == 0)
    def _():
        ret_st[...] = jnp.zeros(ret_st.shape, F32)
        ssd_st[...] = jnp.zeros(ssd_st.shape, F32)
        ssd_ext[0:8, :] = jnp.zeros((8, CONV_CH), F32)

    chunks = list(range(ret_out.shape[0]))
    first, second = chunks[:len(chunks) // 2], chunks[len(chunks) // 2:]
    nsa_loop, nsa_back_window, nsa_diag, nsa_finish = _nsa_phases(
        *nsa_in, nsa_out, *nsa_scr, tq=tq, tk=tk_nsa, seq=seq)
    diff_loop, diff_diag, diff_finish = _diff_phases(
        *diff_in, diff_out, *diff_scr, tq=tq, tk=tk_diff, lam_init=lam_init)
    _ret_body(*ret_in, ret_out, ret_st, first)
    _ssd_body(*ssd_in, ssd_out, ssd_ext, ssd_st, first)

    def body(i, carry):
        _run_interleaved(nsa_loop(i), diff_loop(i))
        return carry

    lax.fori_loop(0, qi, body, 0)
    nsa_back_window()
    _run_interleaved(nsa_diag(), diff_diag())
    _ret_body(*ret_in, ret_out, ret_st, second)
    _ssd_body(*ssd_in, ssd_out, ssd_ext, ssd_st, second)
    nsa_finish()
    diff_finish()


def _mixers(nsa_args, diff_args, rec_args, batch, seq, layer_idx, tq, tk_nsa, tk_diff):
    nq = seq // tq
    nsa_in, nsa_specs, nsa_scratch = _nsa_operands(*nsa_args, seq, nq, tq, tk_nsa)
    diff_in, diff_specs, diff_scratch = _diff_operands(*diff_args, seq, nq, tq, tk_diff)
    ret_in, ssd_in, rec_specs, rec_out_spec, rec_scratch = _recurrent_operands(*rec_args, nq, tq)
    out_spec = pl.BlockSpec((tq, GROUP_W), lambda b, i: (b * nq + i, 0))
    m = batch * seq
    y_nsa, y_diff, y_ret, y_ssm = pl.pallas_call(
        functools.partial(_mixers_kernel, n_in=(len(nsa_in), len(diff_in), len(ret_in), len(ssd_in)),
                          n_scratch=(len(nsa_scratch), len(diff_scratch), len(rec_scratch)),
                          tq=tq, tk_nsa=tk_nsa, tk_diff=tk_diff, seq=seq,
                          lam_init=0.8 - 0.6 * math.exp(-0.3 * layer_idx)),
        grid=(batch, nq),
        in_specs=nsa_specs + diff_specs + rec_specs,
        out_specs=[out_spec, out_spec, rec_out_spec, rec_out_spec],
        out_shape=[jax.ShapeDtypeStruct((m, GROUP_W), BF16)] * 2
                  + [jax.ShapeDtypeStruct((m // SSM_CHUNK, SSM_CHUNK, GROUP_W), BF16)] * 2,
        scratch_shapes=nsa_scratch + diff_scratch + rec_scratch,
        compiler_params=_params(("parallel", "arbitrary")),
        name="mixers",
    )(*nsa_in, *diff_in, *ret_in, *ssd_in)
    return y_nsa, y_diff, y_ret.reshape(m, GROUP_W), y_ssm.reshape(m, GROUP_W)


def _ret_tables():
    c = RET_CHUNK
    h = np.arange(HEADS, dtype=np.float32)
    log_g = jnp.log(1.0 - 2.0 ** (-5.0 - jnp.asarray(h)))
    pos = jnp.arange(c, dtype=F32)
    rel = pos[:, None] - pos[None, :]
    decay = jnp.where(rel >= 0, jnp.exp(log_g[:, None, None] * jnp.maximum(rel, 0.0)), 0.0)
    xi = jnp.exp(log_g[:, None] * (pos + 1.0))
    zeta = jnp.exp(log_g[:, None] * (c - 1.0 - pos))
    chunk_decay = jnp.exp(log_g * c)
    xi_tab = jnp.repeat(xi.T, HEAD_DIM, axis=1)
    zeta_tab = jnp.repeat(zeta.T, HEAD_DIM, axis=1)
    cd_tab = jnp.repeat(chunk_decay, HEAD_DIM)[None, :]
    return decay, xi_tab, zeta_tab, cd_tab


def _ret_body(decay_ref, xi_ref, zeta_ref, cd_ref, gn_ref, q_ref, k_ref, v_ref, z_ref, o_ref, st_ref, blocks):
    xi = xi_ref[...]
    cd = cd_ref[...]
    for bb in blocks:
        q = (q_ref[bb].astype(F32) * (HEAD_DIM ** -0.5)).astype(BF16)
        k = k_ref[bb]
        v = v_ref[bb]
        kz_t = (k.astype(F32) * zeta_ref[...]).T.astype(BF16)
        outs = []
        for h in range(HEADS):
            sl = slice(h * HEAD_DIM, (h + 1) * HEAD_DIM)
            qh, kh, vh = q[:, sl], k[:, sl], v[:, sl]
            prev = st_ref[h]
            inner = (_dot_nt(qh, kh) * decay_ref[h]).astype(BF16)
            o = _dot(inner, vh) + _dot(qh, prev.astype(BF16)) * xi[:, sl]
            st_ref[h] = prev * cd[:, sl] + _dot(kz_t[sl, :], vh)
            mu = jnp.mean(o, axis=-1, keepdims=True)
            d = o - mu
            var = jnp.mean(d * d, axis=-1, keepdims=True)
            outs.append(d * lax.rsqrt(var + EPS))
        y = jnp.concatenate(outs, axis=-1) * gn_ref[...] * _silu(z_ref[bb])
        o_ref[bb] = y.astype(o_ref.dtype)


def _ssd_body(cw_ref, cb_ref, dtb_ref, a_ref, dsk_ref, nw_ref, xbc_ref, misc_ref, z_ref, o_ref,
              ext_ref, st_ref, blocks):
    L = SSM_CHUNK
    hi = lax.Precision.HIGHEST
    row = lax.broadcasted_iota(jnp.int32, (L, L), 0)
    col = lax.broadcasted_iota(jnp.int32, (L, L), 1)
    causal = row >= col
    tril = jnp.where(causal, 1.0, 0.0).astype(F32)
    dsk = dsk_ref[...]

    for bb in blocks:
        raw = xbc_ref[bb]
        ext_ref[8:8 + L, :] = raw
        conv = cb_ref[...] + raw * cw_ref[CONV_W - 1:CONV_W, :]
        for w in range(CONV_W - 1):
            shift = CONV_W - 1 - w
            conv = conv + ext_ref[8 - shift:8 - shift + L, :] * cw_ref[w:w + 1, :]
        ext_ref[0:8, :] = raw[L - 8:L, :]
        xc = _silu(conv)
        x = xc[:, 0:GROUP_W]
        bm = xc[:, GROUP_W:GROUP_W + 2 * SSM_STATE].astype(BF16)
        cm = xc[:, GROUP_W + 2 * SSM_STATE:].astype(BF16)

        dt_full = jax.nn.softplus(misc_ref[bb] + dtb_ref[...])
        da = dt_full * a_ref[...]
        cs_col = jnp.dot(tril, da, precision=hi, preferred_element_type=F32)
        cs_row = lax.dot_general(da, tril, (((0,), (1,)), ((), ())), precision=hi,
                                 preferred_element_type=F32)

        outs = []
        for h in range(HEADS):
            g = h // 2
            c0 = DT_COL + h
            sl = slice(h * HEAD_DIM, (h + 1) * HEAD_DIM)
            gs = slice(g * SSM_STATE, (g + 1) * SSM_STATE)
            cs_c = cs_col[:, c0:c0 + 1]
            cs_r = cs_row[c0:c0 + 1, :]
            cs_last = cs_col[L - 1:L, c0:c0 + 1]
            xh = x[:, sl]
            xdt = xh * dt_full[:, c0:c0 + 1]
            seg = jnp.exp(jnp.where(causal, cs_c - cs_r, NEG_INF))
            cb = _dot_nt(cm[:, gs], bm[:, gs])
            y = _dot((cb * seg).astype(BF16), xdt.astype(BF16))
            prev = st_ref[h]
            y = y + _dot(cm[:, gs], prev.astype(BF16)) * jnp.exp(cs_c)
            y = y + dsk[:, sl] * xh
            dec = jnp.exp(cs_last - cs_c)
            st_ref[h] = prev * jnp.exp(cs_last) + _dot_tn(bm[:, gs], (xdt * dec).astype(BF16))
            outs.append(y)
        y = jnp.concatenate(outs, axis=-1) * _silu(z_ref[bb])
        ms = jnp.mean(y * y, axis=-1, keepdims=True)
        o_ref[bb] = (y * lax.rsqrt(ms + EPS) * nw_ref[...]).astype(o_ref.dtype)


def _recurrent_operands(ret_qkv, xbc, misc, z_all, gn_w, conv_w, conv_b, dt_bias, a_log, d_skip, norm_w, nq, tq):
    L = SSM_CHUNK
    assert RET_CHUNK == L and tq % L == 0
    cpt = tq // L
    decay, xi_tab, zeta_tab, cd_tab = _ret_tables()
    dtb = jnp.zeros((1, 128), F32).at[0, DT_COL:DT_COL + HEADS].set(dt_bias)
    a_full = jnp.zeros((1, 128), F32).at[0, DT_COL:DT_COL + HEADS].set(-jnp.exp(a_log))
    dsk = jnp.repeat(d_skip, HEAD_DIM)[None, :]

    def full(shape):
        return pl.BlockSpec(shape, lambda b, i: (0,) * len(shape))

    def blk(width, col):
        return pl.BlockSpec((cpt, L, width), lambda b, i: (b * nq + i, 0, col))

    def chunked(a):
        return a.reshape(-1, L, a.shape[-1])

    qkv3, z3 = chunked(ret_qkv), chunked(z_all)
    ret_in = [decay, xi_tab, zeta_tab, cd_tab, gn_w.reshape(1, GROUP_W), qkv3, qkv3, qkv3, z3]
    ret_specs = [full((HEADS, L, L)), full((L, GROUP_W)), full((L, GROUP_W)), full((1, GROUP_W)),
                 full((1, GROUP_W)), blk(GROUP_W, 0), blk(GROUP_W, 1), blk(GROUP_W, 2), blk(GROUP_W, 2)]
    ssd_in = [conv_w, conv_b.reshape(1, CONV_CH), dtb, a_full, dsk, norm_w.reshape(1, GROUP_W),
              chunked(xbc), chunked(misc), z3]
    ssd_specs = [full((CONV_W, CONV_CH)), full((1, CONV_CH)), full((1, 128)), full((1, 128)),
                 full((1, GROUP_W)), full((1, GROUP_W)), blk(CONV_CH, 0), blk(128, 0), blk(GROUP_W, 3)]
    scratch = [pltpu.VMEM((HEADS, HEAD_DIM, HEAD_DIM), F32),
               pltpu.VMEM((8 + L, CONV_CH), F32),
               pltpu.VMEM((HEADS, SSM_STATE, HEAD_DIM), F32)]
    return ret_in, ssd_in, ret_specs + ssd_specs, blk(GROUP_W, 0), scratch


def _pick_tile(n, pref):
    t = pref
    while n % t:
        t //= 2
    return t


def kernel(x, norm_w, w_in, w_out, nsa_pe_k, nsa_pe_v, nsa_w_ck1, nsa_w_ck2, nsa_w_cv1, nsa_w_cv2,
           diff_lam_q1, diff_lam_k1, diff_lam_q2, diff_lam_k2, diff_subln_w, ret_gn_w,
           ssm_conv_w, ssm_conv_b, ssm_dt_bias, ssm_A_log, ssm_D, ssm_norm_w, final_norm_w):
    batch, seq, _ = x.shape
    depth = w_in.shape[0]
    m = batch * seq
    tm = _pick_tile(m, 512)
    tq = _pick_tile(seq, 512)
    tk = _pick_tile(seq, 256)
    tk_diff = _pick_tile(seq, 512)
    w_in_b = w_in.astype(BF16)
    w_r = _relayout_w_in(w_in_b)
    w_t = _relayout_w_in_t(w_in_b)
    w_out_b = w_out.astype(BF16)
    x2d = x.reshape(m, D_MODEL)
    projected = _in_proj(x2d, norm_w[0], w_r[0], w_t[0], tm)
    for i in range(depth):
        nsa_q, nsa_k2, nsa_cmp, misc, z_all, diff_qk, ret_qkv, xbc, vt_all = projected
        kc, vct = _nsa_compress(nsa_cmp, nsa_pe_k[i], nsa_pe_v[i], nsa_w_ck1[i], nsa_w_ck2[i],
                                nsa_w_cv1[i], nsa_w_cv2[i], batch, seq)
        lam_p = jnp.stack([diff_lam_q1[i], diff_lam_k1[i], diff_lam_q2[i], diff_lam_k2[i]])
        ys = _mixers((nsa_q, nsa_k2, vt_all, kc, vct, misc, z_all),
                     (diff_qk, vt_all, z_all, lam_p, diff_subln_w[i]),
                     (ret_qkv, xbc, misc, z_all, ret_gn_w[i], ssm_conv_w[i], ssm_conv_b[i], ssm_dt_bias[i],
                      ssm_A_log[i], ssm_D[i], ssm_norm_w[i]),
                     batch, seq, i, tq, tk, tk_diff)
        if i + 1 < depth:
            x2d, *projected = _out_in_proj(ys, w_out_b[i], x2d, norm_w[i + 1], w_r[i + 1], w_t[i + 1], tm)
        else:
            x2d = _out_proj(ys, w_out_b[i], x2d, final_norm_w, tm)
    return x2d.reshape(batch, seq, D_MODEL)
```

```python
import functools
import math

import numpy as np
import jax
import jax.numpy as jnp
from jax import lax
from jax.experimental import pallas as pl
from jax.experimental.pallas import tpu as pltpu

F32 = jnp.float32
BF16 = jnp.bfloat16
NEG_INF = float("-inf")
LOG2E = 1.4426950408889634

D_MODEL = 1024
DEPTH = 4
GROUP_W = 256
HEADS = 4
HEAD_DIM = 64
EPS = 1e-6
CMP_LEN = 32
CMP_STRIDE = 16
CMP_HIDDEN = 256
SLC_BLOCK = 64
SLC_SHIFT = 6
SLC_TOPK = 16
WINDOW = 512
DIFF_QK_DIM = 32
RET_CHUNK = 128
SSM_STATE = 128
SSM_CHUNK = 128
CONV_W = 4
CONV_CH = 768
N_ALIBI_HEADS = 8
LANES = 128
BF16_ROWS = 16
ACC_ROWS = HEAD_DIM + BF16_ROWS
QK_LOOKAHEAD = 3
ALIBI_ROWS = 3
POS_RADIX = 256
MASK_BIG = 2.0 ** 100
BLK_LANE0 = 64
DIAG_ROWS = 256

IN_LAYOUT = (
    ("nsa_q", 256), ("nsa_k_cmp", 64), ("nsa_v_cmp", 64), ("nsa_k_slc", 64), ("nsa_v_slc", 64),
    ("nsa_k_win", 64), ("nsa_v_win", 64), ("nsa_gate", 12), ("nsa_z", 256),
    ("diff_q", 256), ("diff_k", 256), ("diff_v", 256), ("diff_z", 256),
    ("ret_q", 256), ("ret_k", 256), ("ret_v", 256), ("ret_z", 256),
    ("ssm_z", 256), ("ssm_xbc", 768), ("ssm_dt", 4),
)
IN_OFF = {}
_o = 0
for _n, _w in IN_LAYOUT:
    IN_OFF[_n] = (_o, _w)
    _o += _w
IN_W = _o

GATE_COL = 0
DT_COL = 12
IN_OUTPUTS = (
    ("nsa_q", BF16, ("nsa_q",), 256),
    ("nsa_k2", BF16, ("nsa_k_slc", "nsa_k_win"), 128),
    ("nsa_cmp", F32, ("nsa_k_cmp", "nsa_v_cmp"), 128),
    ("misc", F32, ("nsa_gate", "ssm_dt"), 128),
    ("z_all", F32, ("nsa_z", "diff_z", "ret_z", "ssm_z"), 1024),
    ("diff_qk", BF16, ("diff_q", "diff_k"), 512),
    ("ret_qkv", BF16, ("ret_q", "ret_k", "ret_v"), 768),
    ("xbc", F32, ("ssm_xbc",), 768),
)
IN_T_SRC = ("diff_v", "nsa_v_slc", "nsa_v_win")
IN_T_ROWS = 384
DIFF_VT_BLK = 0
NSA_VT_BLK = 2
IN_SEGS = []
_o = 0
for _n, _dt, _src, _w in IN_OUTPUTS:
    IN_SEGS.append((_o, _o + _w))
    _o += _w
IN_WP = _o

VMEM_LIMIT = 56 * 1024 * 1024


def _alibi_slopes():
    return [2.0 ** (-8.0 * (i + 1) / N_ALIBI_HEADS) for i in range(N_ALIBI_HEADS)]


NSA_SLOPES = _alibi_slopes()[0::2]
DIFF_SLOPES = _alibi_slopes()[1::2]


def _silu(x):
    return x * jax.nn.sigmoid(x)


def _dot(a, b):
    return jnp.dot(a, b, preferred_element_type=F32)


def _dot_nt(a, b):
    return lax.dot_general(a, b, (((1,), (1,)), ((), ())), preferred_element_type=F32)


def _dot_tn(a, b):
    return lax.dot_general(a, b, (((0,), (0,)), ((), ())), preferred_element_type=F32)


def _params(sem):
    return pltpu.CompilerParams(dimension_semantics=sem, vmem_limit_bytes=VMEM_LIMIT)


def _relayout_w_in(w_in):
    cols = []
    for _n, _dt, src, width in IN_OUTPUTS:
        used = 0
        for s in src:
            off, w = IN_OFF[s]
            cols.append(w_in[:, :, off:off + w])
            used += w
        if used < width:
            cols.append(jnp.zeros(w_in.shape[:2] + (width - used,), w_in.dtype))
    return jnp.concatenate(cols, axis=-1)


def _relayout_w_in_t(w_in):
    cols = [w_in[:, :, IN_OFF[s][0]:IN_OFF[s][0] + IN_OFF[s][1]] for s in IN_T_SRC]
    return jnp.swapaxes(jnp.concatenate(cols, axis=-1), 1, 2)


def _norm_project(x, nw_ref, w_ref, wt_ref, out_refs):
    ms = jnp.mean(x * x, axis=-1, keepdims=True)
    h = (x * lax.rsqrt(ms + EPS) * nw_ref[...]).astype(BF16)
    for ref, (a, b) in zip(out_refs[:-1], IN_SEGS):
        ref[...] = _dot(h, w_ref[:, a:b]).astype(ref.dtype)
    out_refs[-1][...] = _dot_nt(wt_ref[...], h).astype(BF16)


def _in_proj_kernel(x_ref, nw_ref, w_ref, wt_ref, *out_refs):
    _norm_project(x_ref[...], nw_ref, w_ref, wt_ref, out_refs)


def _in_proj_outputs(m, tm):
    out_shape = [jax.ShapeDtypeStruct((m, b - a), dt) for (_n, dt, _s, _w), (a, b) in zip(IN_OUTPUTS, IN_SEGS)]
    out_specs = [pl.BlockSpec((tm, b - a), lambda i: (i, 0)) for (a, b) in IN_SEGS]
    out_shape.append(jax.ShapeDtypeStruct((IN_T_ROWS, m), BF16))
    out_specs.append(pl.BlockSpec((IN_T_ROWS, tm), lambda i: (0, i)))
    return out_shape, out_specs


def _in_proj(x2d, norm_w, w_r, w_t, tm):
    m = x2d.shape[0]
    out_shape, out_specs = _in_proj_outputs(m, tm)
    return pl.pallas_call(
        _in_proj_kernel,
        grid=(m // tm,),
        in_specs=[pl.BlockSpec((tm, D_MODEL), lambda i: (i, 0)),
                  pl.BlockSpec((1, D_MODEL), lambda i: (0, 0)),
                  pl.BlockSpec((D_MODEL, IN_WP), lambda i: (0, 0)),
                  pl.BlockSpec((IN_T_ROWS, D_MODEL), lambda i: (0, 0))],
        out_specs=out_specs,
        out_shape=out_shape,
        compiler_params=_params(("parallel",)),
        name="in_proj",
    )(x2d, norm_w.reshape(1, D_MODEL), w_r, w_t)


def _out_proj_kernel(y0_ref, y1_ref, y2_ref, y3_ref, w_ref, x_ref, fw_ref, o_ref):
    acc = x_ref[...]
    for g, y_ref in enumerate((y0_ref, y1_ref, y2_ref, y3_ref)):
        acc = acc + _dot(y_ref[...], w_ref[g * GROUP_W:(g + 1) * GROUP_W, :])
    ms = jnp.mean(acc * acc, axis=-1, keepdims=True)
    o_ref[...] = acc * lax.rsqrt(ms + EPS) * fw_ref[...]


def _out_proj(ys, w_out_b, x2d, final_w, tm):
    m = x2d.shape[0]
    yspec = pl.BlockSpec((tm, GROUP_W), lambda i: (i, 0))
    return pl.pallas_call(
        _out_proj_kernel,
        grid=(m // tm,),
        in_specs=[yspec, yspec, yspec, yspec,
                  pl.BlockSpec((D_MODEL, D_MODEL), lambda i: (0, 0)),
                  pl.BlockSpec((tm, D_MODEL), lambda i: (i, 0)),
                  pl.BlockSpec((1, D_MODEL), lambda i: (0, 0))],
        out_specs=pl.BlockSpec((tm, D_MODEL), lambda i: (i, 0)),
        out_shape=jax.ShapeDtypeStruct((m, D_MODEL), F32),
        compiler_params=_params(("parallel",)),
        name="out_proj",
    )(*ys, w_out_b, x2d, final_w.reshape(1, D_MODEL))


def _out_in_proj_kernel(y0_ref, y1_ref, y2_ref, y3_ref, wo_ref, x_ref, nw_ref, w_ref, wt_ref, xo_ref, *out_refs):
    acc = x_ref[...]
    for g, y_ref in enumerate((y0_ref, y1_ref, y2_ref, y3_ref)):
        acc = acc + _dot(y_ref[...], wo_ref[g * GROUP_W:(g + 1) * GROUP_W, :])
    xo_ref[...] = acc
    _norm_project(acc, nw_ref, w_ref, wt_ref, out_refs)


def _out_in_proj(ys, w_out_b, x2d, norm_w, w_r, w_t, tm):
    m = x2d.shape[0]
    yspec = pl.BlockSpec((tm, GROUP_W), lambda i: (i, 0))
    xspec = pl.BlockSpec((tm, D_MODEL), lambda i: (i, 0))
    out_shape, out_specs = _in_proj_outputs(m, tm)
    return pl.pallas_call(
        _out_in_proj_kernel,
        grid=(m // tm,),
        in_specs=[yspec, yspec, yspec, yspec,
                  pl.BlockSpec((D_MODEL, D_MODEL), lambda i: (0, 0)),
                  xspec,
                  pl.BlockSpec((1, D_MODEL), lambda i: (0, 0)),
                  pl.BlockSpec((D_MODEL, IN_WP), lambda i: (0, 0)),
                  pl.BlockSpec((IN_T_ROWS, D_MODEL), lambda i: (0, 0))],
        out_specs=[xspec] + out_specs,
        out_shape=[jax.ShapeDtypeStruct((m, D_MODEL), F32)] + out_shape,
        compiler_params=_params(("parallel",)),
        name="out_in_proj",
    )(*ys, w_out_b, x2d, norm_w.reshape(1, D_MODEL), w_r, w_t)


def _run_interleaved(*pipelines):
    live = list(pipelines)
    while live:
        for g in list(live):
            if next(g, _DONE) is _DONE:
                live.remove(g)


_DONE = object()


def _stage_scores(s_ref, slot, cols, s, keep=None):
    if keep is not None:
        s = jnp.where(keep, s, NEG_INF)
    s_ref[slot, :, cols] = s
    return jnp.max(s, axis=0, keepdims=True)


def _flash_update_t(s, s_max, c1, shift, vt_ext, m_ref, acc_ref, idx, cols):
    m_old = m_ref[idx, :, cols]
    m_new = jnp.maximum(m_old, c1 * s_max + shift)
    alpha = jnp.exp2(m_old - m_new)
    p = jnp.exp2(c1 * s() - (m_new - shift))
    acc_ref[idx, :, cols] = alpha * acc_ref[idx, :, cols] + _dot(vt_ext, p.astype(BF16))
    m_ref[idx, :, cols] = m_new


def _bf16_pieces(x, n=3):
    out = []
    for _ in range(n):
        p = float(np.asarray(x, np.float32).astype(BF16).astype(np.float32))
        out.append(p)
        x = x - p
    return out


def _key_pos_features(tk):
    lane = lax.broadcasted_iota(jnp.int32, (tk, LANES), 1)
    row = lax.broadcasted_iota(jnp.int32, (tk, LANES), 0)
    out = jnp.zeros((tk, LANES), jnp.int32)
    for d in range(-(-tk // POS_RADIX)):
        digit = jnp.bitwise_and(jnp.right_shift(row, 8 * d), POS_RADIX - 1)
        out = jnp.where((lane >= ALIBI_ROWS * d) & (lane < ALIBI_ROWS * (d + 1)), digit, out)
    return out.astype(F32).astype(BF16)


def _alibi_rows(beta, tq, rows=LANES, tk=POS_RADIX):
    row = lax.broadcasted_iota(jnp.int32, (rows, tq), 0)
    out = jnp.zeros((rows, tq), F32)
    for d in range(-(-tk // POS_RADIX)):
        for r, piece in enumerate(_bf16_pieces(beta * POS_RADIX ** d, ALIBI_ROWS)):
            out = jnp.where(row == ALIBI_ROWS * d + r, piece, out)
    return out.astype(BF16)


def _normalized(acc):
    return acc[0:HEAD_DIM] / jnp.maximum(acc[HEAD_DIM:HEAD_DIM + 1], 1e-30)


def _rel_t(tk, tq):
    return lax.broadcasted_iota(jnp.int32, (tk, tq), 1) - lax.broadcasted_iota(jnp.int32, (tk, tq), 0)


def _diff_phases(lam_ref, sw_ref, q_ref, k_ref, vt_ref, z_ref, o_ref, m_ref, acc_ref, wq_ref, s_ref,
                 *, tq, tk, lam_init):
    qi = pl.program_id(1)
    kpq = tq // tk
    scale = DIFF_QK_DIM ** -0.5
    c1 = scale * LOG2E
    m_ref[...] = jnp.full(m_ref.shape, NEG_INF, F32)
    acc_ref[...] = jnp.zeros(acc_ref.shape, F32)
    rel = _rel_t(tk, tq)

    qt = q_ref[...].astype(F32).T.astype(BF16)
    row = lax.broadcasted_iota(jnp.int32, (LANES, tq), 0)
    for j in range(2 * HEADS):
        g, r0 = divmod(j * DIFF_QK_DIM, LANES)
        qg = qt[g * LANES:(g + 1) * LANES]
        wq_ref[j, 0:LANES, :] = jnp.where((row >= r0) & (row < r0 + DIFF_QK_DIM), qg, jnp.zeros_like(qg))
        wq_ref[j, LANES:2 * LANES, :] = _alibi_rows(DIFF_SLOPES[j // 2] / scale, tq, tk=tk)

    ones = jnp.ones((BF16_ROWS, tk), BF16)
    kpos = _key_pos_features(tk)

    n_maps = 2 * HEADS
    n_slots = s_ref.shape[0]

    def tiles(kis, mask_offs):
        loaded = []
        for ki in kis:
            start = pl.multiple_of(ki * tk, tk)
            loaded.append((k_ref[pl.ds(start, tk), :],
                           vt_ref[:, pl.ds(start, tk)],
                           (ki * tk).astype(F32)))
        parts = []
        for t, off in enumerate(mask_offs):
            if off is None:
                parts.append([(slice(0, tk), slice(0, tq))])
            else:
                parts.append([(slice(r0, min(r0 + DIAG_ROWS, tk)), slice((off + r0) // LANES * LANES, tq))
                              for r0 in range(0, tk, DIAG_ROWS) if off + r0 < tq])
        items = [(t, rows, cols, j) for t in range(len(kis)) for rows, cols in parts[t] for j in range(n_maps)]

        def scores(i):
            t, rows, cols, j = items[i]
            g = (j * DIFF_QK_DIM) // LANES
            lhs = jnp.concatenate([loaded[t][0][rows, g * LANES:(g + 1) * LANES], kpos[rows]], axis=1)
            s_ref[i % n_slots, rows, cols] = _dot(lhs, wq_ref[j, :, cols])

        def staged(i):
            t, rows, cols, _j = items[i]
            if mask_offs[t] is None:
                return lambda: s_ref[i % n_slots, rows, cols]
            return lambda: jnp.where(rel[rows, cols] >= mask_offs[t], s_ref[i % n_slots, rows, cols], NEG_INF)

        for i in range(QK_LOOKAHEAD):
            scores(i)
        for i, (t, rows, cols, j) in enumerate(items):
            h = j // 2
            if i + QK_LOOKAHEAD < len(items):
                scores(i + QK_LOOKAHEAD)
            _k, vt, key0 = loaded[t]
            vt_ext = jnp.concatenate([vt[h * HEAD_DIM:(h + 1) * HEAD_DIM, rows], ones[:, rows]], axis=0)
            s = staged(i)
            _flash_update_t(s, jnp.max(s(), axis=0, keepdims=True), c1, (DIFF_SLOPES[h] * LOG2E) * key0,
                            vt_ext, m_ref, acc_ref, j, cols)
            yield

    def loop_steps(i):
        return tiles([i * kpq + d for d in range(kpq)], [None] * kpq)

    def diag_steps():
        return tiles([qi * kpq + d for d in range(kpq)], [d * tk for d in range(kpq)])

    def finish():
        lp = lam_ref[...]
        lam = (jnp.exp(jnp.sum(lp[0:1] * lp[1:2], axis=-1, keepdims=True))
               - jnp.exp(jnp.sum(lp[2:3] * lp[3:4], axis=-1, keepdims=True)) + lam_init)
        sw = sw_ref[...]
        o_t = jnp.concatenate([_normalized(acc_ref[2 * h]) - lam * _normalized(acc_ref[2 * h + 1])
                               for h in range(HEADS)], axis=0)
        o = o_t.T
        outs = []
        for h in range(HEADS):
            oh = o[:, h * HEAD_DIM:(h + 1) * HEAD_DIM]
            ms = jnp.mean(oh * oh, axis=-1, keepdims=True)
            outs.append(oh * lax.rsqrt(ms + EPS) * sw * (1.0 - lam_init))
        y = jnp.concatenate(outs, axis=-1) * _silu(z_ref[...])
        o_ref[...] = y.astype(o_ref.dtype)

    return loop_steps, diag_steps, finish


def _diff_operands(qk, vt_all, z_all, lam_p, subln_w, seq, nq, tq, tk):
    inputs = [lam_p, subln_w.reshape(1, HEAD_DIM), qk, qk, vt_all, z_all]
    in_specs = [pl.BlockSpec((4, DIFF_QK_DIM), lambda b, i: (0, 0)),
                pl.BlockSpec((1, HEAD_DIM), lambda b, i: (0, 0)),
                pl.BlockSpec((tq, GROUP_W), lambda b, i: (b * nq + i, 0)),
                pl.BlockSpec((seq, GROUP_W), lambda b, i: (b, 1)),
                pl.BlockSpec((GROUP_W, seq), lambda b, i: (DIFF_VT_BLK, b)),
                pl.BlockSpec((tq, GROUP_W), lambda b, i: (b * nq + i, 1))]
    scratch = [pltpu.VMEM((2 * HEADS, 1, tq), F32),
               pltpu.VMEM((2 * HEADS, ACC_ROWS, tq), F32),
               pltpu.VMEM((2 * HEADS, 2 * LANES, tq), BF16),
               pltpu.VMEM((QK_LOOKAHEAD + 1, tk, tq), F32)]
    return inputs, in_specs, scratch


def _nsa_compress_kernel(cmp_ref, pe_ref, w1_ref, wk2_ref, wv2t_ref, kc_ref, vct_ref):
    ng = kc_ref.shape[1]
    top = jnp.zeros((ng, 2 * CMP_HIDDEN), F32)
    bot = jnp.zeros((ng, 2 * CMP_HIDDEN), F32)
    for p in range(CMP_STRIDE):
        x = cmp_ref[pl.ds(p, ng, stride=CMP_STRIDE), :]
        top = top + _dot((x + pe_ref[p:p + 1, :]).astype(BF16), w1_ref[p])
        bot = bot + _dot((x + pe_ref[CMP_STRIDE + p:CMP_STRIDE + p + 1, :]).astype(BF16), w1_ref[CMP_STRIDE + p])
    hid = _silu(top + pltpu.roll(bot, ng - 1, 0)).astype(BF16)
    hk, hv = hid[:, 0:CMP_HIDDEN], hid[:, CMP_HIDDEN:]
    kc = _dot(hk, wk2_ref[...])
    lane = lax.broadcasted_iota(jnp.int32, kc.shape, 1)
    blk = lax.broadcasted_iota(jnp.int32, kc.shape, 0).astype(F32)
    kc = jnp.where((lane >= HEAD_DIM) & (lane < HEAD_DIM + ALIBI_ROWS), blk, kc)
    kc_ref[0] = kc.astype(kc_ref.dtype)
    vct_ref[0] = _dot_nt(wv2t_ref[...], hv).astype(vct_ref.dtype)


def _nsa_compress(cmp2d, pe_k, pe_v, w_ck1, w_ck2, w_cv1, w_cv2, batch, seq):
    ng = seq // CMP_STRIDE
    wk = w_ck1.reshape(CMP_LEN, HEAD_DIM, CMP_HIDDEN)
    wv = w_cv1.reshape(CMP_LEN, HEAD_DIM, CMP_HIDDEN)
    w1 = jnp.concatenate([jnp.pad(wk, ((0, 0), (0, 0), (0, CMP_HIDDEN))),
                          jnp.pad(wv, ((0, 0), (0, 0), (CMP_HIDDEN, 0)))], axis=1).astype(BF16)
    pe = jnp.concatenate([pe_k, pe_v], axis=1)

    def full(shape):
        return pl.BlockSpec(shape, lambda b: (0,) * len(shape))

    return pl.pallas_call(
        _nsa_compress_kernel,
        grid=(batch,),
        in_specs=[pl.BlockSpec((seq, LANES), lambda b: (b, 0)), full((CMP_LEN, LANES)),
                  full((CMP_LEN, LANES, 2 * CMP_HIDDEN)), full((CMP_HIDDEN, LANES)),
                  full((HEAD_DIM, CMP_HIDDEN))],
        out_specs=[pl.BlockSpec((1, ng, LANES), lambda b: (b, 0, 0)),
                   pl.BlockSpec((1, HEAD_DIM, ng), lambda b: (b, 0, 0))],
        out_shape=[jax.ShapeDtypeStruct((batch, ng, LANES), BF16),
                   jax.ShapeDtypeStruct((batch, HEAD_DIM, ng), BF16)],
        compiler_params=_params(("parallel",)),
        name="nsa_compress",
    )(cmp2d, pe, w1, jnp.pad(w_ck2, ((0, 0), (0, LANES - HEAD_DIM))).astype(BF16), w_cv2.T.astype(BF16))


def _nsa_phases(ovl_ref, q_ref, k_ref, vt_ref, kc_ref, vct_ref, misc_ref, z_ref, o_ref,
                m_ref, acc_ref, wq_ref, s_ref, sc_ref, *, tq, tk, seq):
    qi = pl.program_id(1)
    kpq = tq // tk
    scale = HEAD_DIM ** -0.5
    c1 = scale * LOG2E
    ng = seq // CMP_STRIDE
    ns = seq // SLC_BLOCK
    top = min(SLC_TOPK, ns)
    m_ref[...] = jnp.full(m_ref.shape, NEG_INF, F32)
    acc_ref[...] = jnp.zeros(acc_ref.shape, F32)
    rel = _rel_t(tk, tq)

    qt = q_ref[...].astype(F32).T.astype(BF16)
    zeros_q = jnp.zeros((HEAD_DIM, tq), BF16)
    for h in range(HEADS):
        qh = qt[h * HEAD_DIM:(h + 1) * HEAD_DIM]
        alibi = _alibi_rows(NSA_SLOPES[h] / scale, tq, tk=tk)
        wq_ref[h, 0:LANES, :] = jnp.concatenate([qh, zeros_q], axis=0)
        wq_ref[h, LANES:2 * LANES, :] = alibi
        wq_ref[HEADS + h, 0:LANES, :] = jnp.concatenate([zeros_q, qh], axis=0)
        wq_ref[HEADS + h, LANES:2 * LANES, :] = alibi

    t_lane = qi * tq + lax.broadcasted_iota(jnp.int32, (1, tq), 1)
    n_sub = lax.broadcasted_iota(jnp.int32, (ng, 1), 0)
    c_valid = (n_sub * CMP_STRIDE + (CMP_LEN - 1)) <= t_lane
    kc = kc_ref[0]
    vct_ext = jnp.concatenate([vct_ref[0], jnp.ones((BF16_ROWS, ng), BF16)], axis=0)
    ovl = ovl_ref[...]
    cmp_max = []
    for h in range(HEADS):
        rhs = jnp.concatenate([qt[h * HEAD_DIM:(h + 1) * HEAD_DIM],
                               _alibi_rows(NSA_SLOPES[h] * CMP_STRIDE / scale, tq, HEAD_DIM)], axis=0)
        cmp_max.append(_stage_scores(sc_ref, h, slice(0, tq), _dot(kc, rhs), c_valid))
    imp = jnp.zeros((ns, tq), F32)
    o_cmp = []
    for h in range(HEADS):
        mx = jnp.where(cmp_max[h] > NEG_INF, cmp_max[h], 0.0)
        pb = jnp.exp2(c1 * sc_ref[h] - c1 * mx).astype(BF16)
        o_ext = _dot(vct_ext, pb)
        r = 1.0 / jnp.maximum(o_ext[HEAD_DIM:HEAD_DIM + 1], 1e-30)
        o_cmp.append(o_ext[0:HEAD_DIM] * r)
        imp = imp + _dot(ovl, pb) * r

    def select_blocks():
        j_sub = lax.broadcasted_iota(jnp.int32, (ns, 1), 0)
        j_sub_f = j_sub.astype(F32)
        cur = jnp.right_shift(t_lane, SLC_SHIFT)
        forced = (j_sub == 0) | (j_sub == cur) | (j_sub == cur - 1)
        valid = (j_sub * SLC_BLOCK) <= t_lane
        score = jnp.where(forced, 1e30, jnp.where(valid, imp, -1.0))
        sel = jnp.zeros((ns, tq), F32)
        for _ in range(top):
            mx = jnp.max(score, axis=0, keepdims=True)
            idx = jnp.min(jnp.where(score == mx, j_sub_f, float(ns)), axis=0, keepdims=True)
            pick = j_sub_f == idx
            sel = jnp.where(pick, 1.0, sel)
            score = jnp.where(pick, -2.0, score)
        unsel = jnp.where(sel > 0.5, 0.0, -MASK_BIG).astype(BF16)
        for h in range(HEADS):
            wq_ref[h, LANES + BLK_LANE0:LANES + BLK_LANE0 + ns, :] = unsel

    aux_lane = lax.broadcasted_iota(jnp.int32, (tk, LANES), 1)
    aux_blk = jnp.right_shift(lax.broadcasted_iota(jnp.int32, (tk, LANES), 0), SLC_SHIFT) + BLK_LANE0
    kpos = _key_pos_features(tk)
    ones = jnp.ones((BF16_ROWS, tk), BF16)

    n_slots = s_ref.shape[0]

    def tiles(window, kis, cols, mask_offs=None):
        base = HEADS if window else 0
        loaded = []
        for t, ki in enumerate(kis):
            start = pl.multiple_of(ki * tk, tk)
            k = k_ref[pl.ds(start, tk), :]
            if window:
                vt = vt_ref[HEAD_DIM:2 * HEAD_DIM, pl.ds(start, tk)]
                aux = kpos
                dist = rel[:, cols[t]] + (qi * tq - ki * tk)
                keep = (dist >= 0) & (dist < WINDOW)
            else:
                vt = vt_ref[0:HEAD_DIM, pl.ds(start, tk)]
                onehot = aux_lane == aux_blk + ki * (tk // SLC_BLOCK)
                aux = jnp.where(onehot, jnp.ones_like(kpos), kpos)
                keep = None
            loaded.append((jnp.concatenate([k, aux], axis=1),
                           jnp.concatenate([vt, ones], axis=0), (ki * tk).astype(F32), keep))
        items = [(t, h) for t in range(len(kis)) for h in range(HEADS)]
        s_max = {}

        def scores(i):
            t, h = items[i]
            if window:
                keep = loaded[t][3]
            else:
                keep = None if mask_offs[t] is None else rel[:, cols[t]] >= mask_offs[t]
            s_max[i] = _stage_scores(s_ref, i % n_slots, cols[t],
                                     _dot(loaded[t][0], wq_ref[base + h, :, cols[t]]), keep)

        for i in range(QK_LOOKAHEAD):
            scores(i)
        for i, (t, h) in enumerate(items):
            if i + QK_LOOKAHEAD < len(items):
                scores(i + QK_LOOKAHEAD)
            _lhs, vt_ext, key0, _keep = loaded[t]
            _flash_update_t(lambda i=i, t=t: s_ref[i % n_slots, :, cols[t]], s_max.pop(i), c1,
                            (NSA_SLOPES[h] * LOG2E) * key0, vt_ext, m_ref, acc_ref, base + h, cols[t])
            yield

    all_cols = slice(0, tq)
    diag_cols = [slice(d * tk, tq) for d in range(kpq)]

    _run_interleaved(tiles(True, [qi * kpq + d for d in range(kpq)], diag_cols))
    select_blocks()

    def loop_steps(i):
        return tiles(False, [i * kpq + d for d in range(kpq)], [all_cols] * kpq, [None] * kpq)

    def back_window():
        n_back = (WINDOW + tk - 1) // tk
        for g in range((n_back + kpq - 1) // kpq):
            backs = list(range(g * kpq + 1, min((g + 1) * kpq, n_back) + 1))
            back_cols = [slice(0, min(tq, -(-(WINDOW - (back - 1) * tk - 1) // LANES) * LANES)) for back in backs]

            @pl.when(qi > g)
            def _():
                _run_interleaved(tiles(True, [qi * kpq - back for back in backs], back_cols))

    def diag_steps():
        return tiles(False, [qi * kpq + d for d in range(kpq)], diag_cols, [d * tk for d in range(kpq)])

    def finish():
        g_t = jax.nn.sigmoid(misc_ref[...]).T
        outs = []
        for h in range(HEADS):
            r0 = GATE_COL + 3 * h
            outs.append(g_t[r0:r0 + 1] * o_cmp[h] + g_t[r0 + 1:r0 + 2] * _normalized(acc_ref[h])
                        + g_t[r0 + 2:r0 + 3] * _normalized(acc_ref[HEADS + h]))
        y = jnp.concatenate(outs, axis=0).T * _silu(z_ref[...])
        o_ref[...] = y.astype(o_ref.dtype)

    return loop_steps, back_window, diag_steps, finish


def _overlap_t(seq):
    nc = (seq - CMP_LEN) // CMP_STRIDE + 1
    ng = seq // CMP_STRIDE
    ns = seq // SLC_BLOCK
    c_start = np.arange(ng) * CMP_STRIDE
    c_end = c_start + CMP_LEN - 1
    s_start = np.arange(ns) * SLC_BLOCK
    s_end = s_start + SLC_BLOCK - 1
    ov = (c_start[None, :] <= s_end[:, None]) & (c_end[None, :] >= s_start[:, None]) & (np.arange(ng)[None, :] < nc)
    return jnp.asarray(ov.astype(np.float32), dtype=BF16)


def _nsa_operands(q, k2, vt_all, kc, vct, misc, z_all, seq, nq, tq, tk):
    ng = seq // CMP_STRIDE
    ns = seq // SLC_BLOCK
    assert tq - tk < WINDOW and tk % SLC_BLOCK == 0 and ns <= LANES - BLK_LANE0
    inputs = [_overlap_t(seq), q, k2, vt_all, kc, vct, misc, z_all]
    in_specs = [pl.BlockSpec((ns, ng), lambda b, i: (0, 0)),
                pl.BlockSpec((tq, GROUP_W), lambda b, i: (b * nq + i, 0)),
                pl.BlockSpec((seq, LANES), lambda b, i: (b, 0)),
                pl.BlockSpec((2 * HEAD_DIM, seq), lambda b, i: (NSA_VT_BLK, b)),
                pl.BlockSpec((1, ng, LANES), lambda b, i: (b, 0, 0)),
                pl.BlockSpec((1, HEAD_DIM, ng), lambda b, i: (b, 0, 0)),
                pl.BlockSpec((tq, LANES), lambda b, i: (b * nq + i, 0)),
                pl.BlockSpec((tq, GROUP_W), lambda b, i: (b * nq + i, 0))]
    scratch = [pltpu.VMEM((2 * HEADS, 1, tq), F32),
               pltpu.VMEM((2 * HEADS, ACC_ROWS, tq), F32),
               pltpu.VMEM((2 * HEADS, 2 * LANES, tq), BF16),
               pltpu.VMEM((2 * HEADS, tk, tq), F32),
               pltpu.VMEM((HEADS, ng, tq), F32)]
    return inputs, in_specs, scratch


def _mixers_kernel(*refs, n_in, n_scratch, tq, tk_nsa, tk_diff, seq, lam_init):
    bounds = np.cumsum([0] + list(n_in))
    nsa_in, diff_in, ret_in, ssd_in = (refs[a:b] for a, b in zip(bounds[:-1], bounds[1:]))
    nsa_out, diff_out, ret_out, ssd_out = refs[bounds[-1]:bounds[-1] + 4]
    sb = np.cumsum([0] + list(n_scratch)) + bounds[-1] + 4
    nsa_scr, diff_scr, rec_scr = (refs[a:b] for a, b in zip(sb[:-1], sb[1:]))
    ret_st, ssd_ext, ssd_st = rec_scr
    qi = pl.program_id(1)

    @pl.when(qi == 0)
    def _():
        ret_st[...] = jnp.zeros(ret_st.shape, F32)
        ssd_st[...] = jnp.zeros(ssd_st.shape, F32)
        ssd_ext[0:8, :] = jnp.zeros((8, CONV_CH), F32)

    chunks = list(range(ret_out.shape[0]))
    first, second = chunks[:len(chunks) // 2], chunks[len(chunks) // 2:]
    nsa_loop, nsa_back_window, nsa_diag, nsa_finish = _nsa_phases(
        *nsa_in, nsa_out, *nsa_scr, tq=tq, tk=tk_nsa, seq=seq)
    diff_loop, diff_diag, diff_finish = _diff_phases(
        *diff_in, diff_out, *diff_scr, tq=tq, tk=tk_diff, lam_init=lam_init)
    _ret_body(*ret_in, ret_out, ret_st, first)
    _ssd_body(*ssd_in, ssd_out, ssd_ext, ssd_st, first)

    def body(i, carry):
        _run_interleaved(nsa_loop(i), diff_loop(i))
        return carry

    lax.fori_loop(0, qi, body, 0)
    nsa_back_window()
    _run_interleaved(nsa_diag(), diff_diag())
    _ret_body(*ret_in, ret_out, ret_st, second)
    _ssd_body(*ssd_in, ssd_out, ssd_ext, ssd_st, second)
    nsa_finish()
    diff_finish()


def _mixers(nsa_args, diff_args, rec_args, batch, seq, layer_idx, tq, tk_nsa, tk_diff):
    nq = seq // tq
    nsa_in, nsa_specs, nsa_scratch = _nsa_operands(*nsa_args, seq, nq, tq, tk_nsa)
    diff_in, diff_specs, diff_scratch = _diff_operands(*diff_args, seq, nq, tq, tk_diff)
    ret_in, ssd_in, rec_specs, rec_out_spec, rec_scratch = _recurrent_operands(*rec_args, nq, tq)
    out_spec = pl.BlockSpec((tq, GROUP_W), lambda b, i: (b * nq + i, 0))
    m = batch * seq
    y_nsa, y_diff, y_ret, y_ssm = pl.pallas_call(
        functools.partial(_mixers_kernel, n_in=(len(nsa_in), len(diff_in), len(ret_in), len(ssd_in)),
                          n_scratch=(len(nsa_scratch), len(diff_scratch), len(rec_scratch)),
                          tq=tq, tk_nsa=tk_nsa, tk_diff=tk_diff, seq=seq,
                          lam_init=0.8 - 0.6 * math.exp(-0.3 * layer_idx)),
        grid=(batch, nq),
        in_specs=nsa_specs + diff_specs + rec_specs,
        out_specs=[out_spec, out_spec, rec_out_spec, rec_out_spec],
        out_shape=[jax.ShapeDtypeStruct((m, GROUP_W), BF16)] * 2
                  + [jax.ShapeDtypeStruct((m // SSM_CHUNK, SSM_CHUNK, GROUP_W), BF16)] * 2,
        scratch_shapes=nsa_scratch + diff_scratch + rec_scratch,
        compiler_params=_params(("parallel", "arbitrary")),
        name="mixers",
    )(*nsa_in, *diff_in, *ret_in, *ssd_in)
    return y_nsa, y_diff, y_ret.reshape(m, GROUP_W), y_ssm.reshape(m, GROUP_W)


def _ret_tables():
    c = RET_CHUNK
    h = np.arange(HEADS, dtype=np.float32)
    log_g = jnp.log(1.0 - 2.0 ** (-5.0 - jnp.asarray(h)))
    pos = jnp.arange(c, dtype=F32)
    rel = pos[:, None] - pos[None, :]
    decay = jnp.where(rel >= 0, jnp.exp(log_g[:, None, None] * jnp.maximum(rel, 0.0)), 0.0)
    xi = jnp.exp(log_g[:, None] * (pos + 1.0))
    zeta = jnp.exp(log_g[:, None] * (c - 1.0 - pos))
    chunk_decay = jnp.exp(log_g * c)
    xi_tab = jnp.repeat(xi.T, HEAD_DIM, axis=1)
    zeta_tab = jnp.repeat(zeta.T, HEAD_DIM, axis=1)
    cd_tab = jnp.repeat(chunk_decay, HEAD_DIM)[None, :]
    return decay, xi_tab, zeta_tab, cd_tab


def _ret_body(decay_ref, xi_ref, zeta_ref, cd_ref, gn_ref, q_ref, k_ref, v_ref, z_ref, o_ref, st_ref, blocks):
    xi = xi_ref[...]
    cd = cd_ref[...]
    for bb in blocks:
        q = (q_ref[bb].astype(F32) * (HEAD_DIM ** -0.5)).astype(BF16)
        k = k_ref[bb]
        v = v_ref[bb]
        kz_t = (k.astype(F32) * zeta_ref[...]).T.astype(BF16)
        outs = []
        for h in range(HEADS):
            sl = slice(h * HEAD_DIM, (h + 1) * HEAD_DIM)
            qh, kh, vh = q[:, sl], k[:, sl], v[:, sl]
            prev = st_ref[h]
            inner = (_dot_nt(qh, kh) * decay_ref[h]).astype(BF16)
            o = _dot(inner, vh) + _dot(qh, prev.astype(BF16)) * xi[:, sl]
            st_ref[h] = prev * cd[:, sl] + _dot(kz_t[sl, :], vh)
            mu = jnp.mean(o, axis=-1, keepdims=True)
            d = o - mu
            var = jnp.mean(d * d, axis=-1, keepdims=True)
            outs.append(d * lax.rsqrt(var + EPS))
        y = jnp.concatenate(outs, axis=-1) * gn_ref[...] * _silu(z_ref[bb])
        o_ref[bb] = y.astype(o_ref.dtype)


def _ssd_body(cw_ref, cb_ref, dtb_ref, a_ref, dsk_ref, nw_ref, xbc_ref, misc_ref, z_ref, o_ref,
              ext_ref, st_ref, blocks):
    L = SSM_CHUNK
    hi = lax.Precision.HIGHEST
    row = lax.broadcasted_iota(jnp.int32, (L, L), 0)
    col = lax.broadcasted_iota(jnp.int32, (L, L), 1)
    causal = row >= col
    tril = jnp.where(causal, 1.0, 0.0).astype(F32)
    dsk = dsk_ref[...]

    for bb in blocks:
        raw = xbc_ref[bb]
        ext_ref[8:8 + L, :] = raw
        conv = cb_ref[...] + raw * cw_ref[CONV_W - 1:CONV_W, :]
        for w in range(CONV_W - 1):
            shift = CONV_W - 1 - w
            conv = conv + ext_ref[8 - shift:8 - shift + L, :] * cw_ref[w:w + 1, :]
        ext_ref[0:8, :] = raw[L - 8:L, :]
        xc = _silu(conv)
        x = xc[:, 0:GROUP_W]
        bm = xc[:, GROUP_W:GROUP_W + 2 * SSM_STATE].astype(BF16)
        cm = xc[:, GROUP_W + 2 * SSM_STATE:].astype(BF16)

        dt_full = jax.nn.softplus(misc_ref[bb] + dtb_ref[...])
        da = dt_full * a_ref[...]
        cs_col = jnp.dot(tril, da, precision=hi, preferred_element_type=F32)
        cs_row = lax.dot_general(da, tril, (((0,), (1,)), ((), ())), precision=hi,
                                 preferred_element_type=F32)

        outs = []
        for h in range(HEADS):
            g = h // 2
            c0 = DT_COL + h
            sl = slice(h * HEAD_DIM, (h + 1) * HEAD_DIM)
            gs = slice(g * SSM_STATE, (g + 1) * SSM_STATE)
            cs_c = cs_col[:, c0:c0 + 1]
            cs_r = cs_row[c0:c0 + 1, :]
            cs_last = cs_col[L - 1:L, c0:c0 + 1]
            xh = x[:, sl]
            xdt = xh * dt_full[:, c0:c0 + 1]
            seg = jnp.exp(jnp.where(causal, cs_c - cs_r, NEG_INF))
            cb = _dot_nt(cm[:, gs], bm[:, gs])
            y = _dot((cb * seg).astype(BF16), xdt.astype(BF16))
            prev = st_ref[h]
            y = y + _dot(cm[:, gs], prev.astype(BF16)) * jnp.exp(cs_c)
            y = y + dsk[:, sl] * xh
            dec = jnp.exp(cs_last - cs_c)
            st_ref[h] = prev * jnp.exp(cs_last) + _dot_tn(bm[:, gs], (xdt * dec).astype(BF16))
            outs.append(y)
        y = jnp.concatenate(outs, axis=-1) * _silu(z_ref[bb])
        ms = jnp.mean(y * y, axis=-1, keepdims=True)
        o_ref[bb] = (y * lax.rsqrt(ms + EPS) * nw_ref[...]).astype(o_ref.dtype)


def _recurrent_operands(ret_qkv, xbc, misc, z_all, gn_w, conv_w, conv_b, dt_bias, a_log, d_skip, norm_w, nq, tq):
    L = SSM_CHUNK
    assert RET_CHUNK == L and tq % L == 0
    cpt = tq // L
    decay, xi_tab, zeta_tab, cd_tab = _ret_tables()
    dtb = jnp.zeros((1, 128), F32).at[0, DT_COL:DT_COL + HEADS].set(dt_bias)
    a_full = jnp.zeros((1, 128), F32).at[0, DT_COL:DT_COL + HEADS].set(-jnp.exp(a_log))
    dsk = jnp.repeat(d_skip, HEAD_DIM)[None, :]

    def full(shape):
        return pl.BlockSpec(shape, lambda b, i: (0,) * len(shape))

    def blk(width, col):
        return pl.BlockSpec((cpt, L, width), lambda b, i: (b * nq + i, 0, col))

    def chunked(a):
        return a.reshape(-1, L, a.shape[-1])

    qkv3, z3 = chunked(ret_qkv), chunked(z_all)
    ret_in = [decay, xi_tab, zeta_tab, cd_tab, gn_w.reshape(1, GROUP_W), qkv3, qkv3, qkv3, z3]
    ret_specs = [full((HEADS, L, L)), full((L, GROUP_W)), full((L, GROUP_W)), full((1, GROUP_W)),
                 full((1, GROUP_W)), blk(GROUP_W, 0), blk(GROUP_W, 1), blk(GROUP_W, 2), blk(GROUP_W, 2)]
    ssd_in = [conv_w, conv_b.reshape(1, CONV_CH), dtb, a_full, dsk, norm_w.reshape(1, GROUP_W),
              chunked(xbc), chunked(misc), z3]
    ssd_specs = [full((CONV_W, CONV_CH)), full((1, CONV_CH)), full((1, 128)), full((1, 128)),
                 full((1, GROUP_W)), full((1, GROUP_W)), blk(CONV_CH, 0), blk(128, 0), blk(GROUP_W, 3)]
    scratch = [pltpu.VMEM((HEADS, HEAD_DIM, HEAD_DIM), F32),
               pltpu.VMEM((8 + L, CONV_CH), F32),
               pltpu.VMEM((HEADS, SSM_STATE, HEAD_DIM), F32)]
    return ret_in, ssd_in, ret_specs + ssd_specs, blk(GROUP_W, 0), scratch


def _pick_tile(n, pref):
    t = pref
    while n % t:
        t //= 2
    return t


def kernel(x, norm_w, w_in, w_out, nsa_pe_k, nsa_pe_v, nsa_w_ck1, nsa_w_ck2, nsa_w_cv1, nsa_w_cv2,
           diff_lam_q1, diff_lam_k1, diff_lam_q2, diff_lam_k2, diff_subln_w, ret_gn_w,
           ssm_conv_w, ssm_conv_b, ssm_dt_bias, ssm_A_log, ssm_D, ssm_norm_w, final_norm_w):
    batch, seq, _ = x.shape
    depth = w_in.shape[0]
    m = batch * seq
    tm = _pick_tile(m, 512)
    tq = _pick_tile(seq, 512)
    tk = _pick_tile(seq, 256)
    tk_diff = _pick_tile(seq, 512)
    w_in_b = w_in.astype(BF16)
    w_r = _relayout_w_in(w_in_b)
    w_t = _relayout_w_in_t(w_in_b)
    w_out_b = w_out.astype(BF16)
    x2d = x.reshape(m, D_MODEL)
    projected = _in_proj(x2d, norm_w[0], w_r[0], w_t[0], tm)
    for i in range(depth):
        nsa_q, nsa_k2, nsa_cmp, misc, z_all, diff_qk, ret_qkv, xbc, vt_all = projected
        kc, vct = _nsa_compress(nsa_cmp, nsa_pe_k[i], nsa_pe_v[i], nsa_w_ck1[i], nsa_w_ck2[i],
                                nsa_w_cv1[i], nsa_w_cv2[i], batch, seq)
        lam_p = jnp.stack([diff_lam_q1[i], diff_lam_k1[i], diff_lam_q2[i], diff_lam_k2[i]])
        ys = _mixers((nsa_q, nsa_k2, vt_all, kc, vct, misc, z_all),
                     (diff_qk, vt_all, z_all, lam_p, diff_subln_w[i]),
                     (ret_qkv, xbc, misc, z_all, ret_gn_w[i], ssm_conv_w[i], ssm_conv_b[i], ssm_dt_bias[i],
                      ssm_A_log[i], ssm_D[i], ssm_norm_w[i]),
                     batch, seq, i, tq, tk, tk_diff)
        if i + 1 < depth:
            x2d, *projected = _out_in_proj(ys, w_out_b[i], x2d, norm_w[i + 1], w_r[i + 1], w_t[i + 1], tm)
        else:
            x2d = _out_proj(ys, w_out_b[i], x2d, final_norm_w, tm)
    return x2d.reshape(batch, seq, D_MODEL)
```

```python
import functools
import math

import numpy as np
import jax
import jax.numpy as jnp
from jax import lax
from jax.experimental import pallas as pl
from jax.experimental.pallas import tpu as pltpu

F32 = jnp.float32
BF16 = jnp.bfloat16
NEG_INF = float("-inf")
LOG2E = 1.4426950408889634

D_MODEL = 1024
DEPTH = 4
GROUP_W = 256
HEADS = 4
HEAD_DIM = 64
EPS = 1e-6
CMP_LEN = 32
CMP_STRIDE = 16
CMP_HIDDEN = 256
SLC_BLOCK = 64
SLC_SHIFT = 6
SLC_TOPK = 16
WINDOW = 512
DIFF_QK_DIM = 32
RET_CHUNK = 128
SSM_STATE = 128
SSM_CHUNK = 128
CONV_W = 4
CONV_CH = 768
N_ALIBI_HEADS = 8
LANES = 128
BF16_ROWS = 16
ACC_ROWS = HEAD_DIM + BF16_ROWS
QK_LOOKAHEAD = 3
ALIBI_ROWS = 3
POS_RADIX = 256
MASK_BIG = 2.0 ** 100
BLK_LANE0 = 64

IN_LAYOUT = (
    ("nsa_q", 256), ("nsa_k_cmp", 64), ("nsa_v_cmp", 64), ("nsa_k_slc", 64), ("nsa_v_slc", 64),
    ("nsa_k_win", 64), ("nsa_v_win", 64), ("nsa_gate", 12), ("nsa_z", 256),
    ("diff_q", 256), ("diff_k", 256), ("diff_v", 256), ("diff_z", 256),
    ("ret_q", 256), ("ret_k", 256), ("ret_v", 256), ("ret_z", 256),
    ("ssm_z", 256), ("ssm_xbc", 768), ("ssm_dt", 4),
)
IN_OFF = {}
_o = 0
for _n, _w in IN_LAYOUT:
    IN_OFF[_n] = (_o, _w)
    _o += _w
IN_W = _o

GATE_COL = 0
DT_COL = 12
IN_OUTPUTS = (
    ("nsa_q", BF16, ("nsa_q",), 256),
    ("nsa_k2", BF16, ("nsa_k_slc", "nsa_k_win"), 128),
    ("nsa_cmp", F32, ("nsa_k_cmp", "nsa_v_cmp"), 128),
    ("misc", F32, ("nsa_gate", "ssm_dt"), 128),
    ("z_all", F32, ("nsa_z", "diff_z", "ret_z", "ssm_z"), 1024),
    ("diff_qk", BF16, ("diff_q", "diff_k"), 512),
    ("ret_qkv", BF16, ("ret_q", "ret_k", "ret_v"), 768),
    ("xbc", F32, ("ssm_xbc",), 768),
)
IN_T_SRC = ("diff_v", "nsa_v_slc", "nsa_v_win")
IN_T_ROWS = 384
DIFF_VT_BLK = 0
NSA_VT_BLK = 2
IN_SEGS = []
_o = 0
for _n, _dt, _src, _w in IN_OUTPUTS:
    IN_SEGS.append((_o, _o + _w))
    _o += _w
IN_WP = _o

VMEM_LIMIT = 56 * 1024 * 1024


def _alibi_slopes():
    return [2.0 ** (-8.0 * (i + 1) / N_ALIBI_HEADS) for i in range(N_ALIBI_HEADS)]


NSA_SLOPES = _alibi_slopes()[0::2]
DIFF_SLOPES = _alibi_slopes()[1::2]


def _silu(x):
    return x * jax.nn.sigmoid(x)


def _dot(a, b):
    return jnp.dot(a, b, preferred_element_type=F32)


def _dot_nt(a, b):
    return lax.dot_general(a, b, (((1,), (1,)), ((), ())), preferred_element_type=F32)


def _dot_tn(a, b):
    return lax.dot_general(a, b, (((0,), (0,)), ((), ())), preferred_element_type=F32)


def _params(sem):
    return pltpu.CompilerParams(dimension_semantics=sem, vmem_limit_bytes=VMEM_LIMIT)


def _relayout_w_in(w_in):
    cols = []
    for _n, _dt, src, width in IN_OUTPUTS:
        used = 0
        for s in src:
            off, w = IN_OFF[s]
            cols.append(w_in[:, :, off:off + w])
            used += w
        if used < width:
            cols.append(jnp.zeros(w_in.shape[:2] + (width - used,), w_in.dtype))
    return jnp.concatenate(cols, axis=-1)


def _relayout_w_in_t(w_in):
    cols = [w_in[:, :, IN_OFF[s][0]:IN_OFF[s][0] + IN_OFF[s][1]] for s in IN_T_SRC]
    return jnp.swapaxes(jnp.concatenate(cols, axis=-1), 1, 2)


def _norm_project(x, nw_ref, w_ref, wt_ref, out_refs):
    ms = jnp.mean(x * x, axis=-1, keepdims=True)
    h = (x * lax.rsqrt(ms + EPS) * nw_ref[...]).astype(BF16)
    for ref, (a, b) in zip(out_refs[:-1], IN_SEGS):
        ref[...] = _dot(h, w_ref[:, a:b]).astype(ref.dtype)
    out_refs[-1][...] = _dot_nt(wt_ref[...], h).astype(BF16)


def _in_proj_kernel(x_ref, nw_ref, w_ref, wt_ref, *out_refs):
    _norm_project(x_ref[...], nw_ref, w_ref, wt_ref, out_refs)


def _in_proj_outputs(m, tm):
    out_shape = [jax.ShapeDtypeStruct((m, b - a), dt) for (_n, dt, _s, _w), (a, b) in zip(IN_OUTPUTS, IN_SEGS)]
    out_specs = [pl.BlockSpec((tm, b - a), lambda i: (i, 0)) for (a, b) in IN_SEGS]
    out_shape.append(jax.ShapeDtypeStruct((IN_T_ROWS, m), BF16))
    out_specs.append(pl.BlockSpec((IN_T_ROWS, tm), lambda i: (0, i)))
    return out_shape, out_specs


def _in_proj(x2d, norm_w, w_r, w_t, tm):
    m = x2d.shape[0]
    out_shape, out_specs = _in_proj_outputs(m, tm)
    return pl.pallas_call(
        _in_proj_kernel,
        grid=(m // tm,),
        in_specs=[pl.BlockSpec((tm, D_MODEL), lambda i: (i, 0)),
                  pl.BlockSpec((1, D_MODEL), lambda i: (0, 0)),
                  pl.BlockSpec((D_MODEL, IN_WP), lambda i: (0, 0)),
                  pl.BlockSpec((IN_T_ROWS, D_MODEL), lambda i: (0, 0))],
        out_specs=out_specs,
        out_shape=out_shape,
        compiler_params=_params(("parallel",)),
        name="in_proj",
    )(x2d, norm_w.reshape(1, D_MODEL), w_r, w_t)


def _out_proj_kernel(y0_ref, y1_ref, y2_ref, y3_ref, w_ref, x_ref, fw_ref, o_ref):
    acc = x_ref[...]
    for g, y_ref in enumerate((y0_ref, y1_ref, y2_ref, y3_ref)):
        acc = acc + _dot(y_ref[...], w_ref[g * GROUP_W:(g + 1) * GROUP_W, :])
    ms = jnp.mean(acc * acc, axis=-1, keepdims=True)
    o_ref[...] = acc * lax.rsqrt(ms + EPS) * fw_ref[...]


def _out_proj(ys, w_out_b, x2d, final_w, tm):
    m = x2d.shape[0]
    yspec = pl.BlockSpec((tm, GROUP_W), lambda i: (i, 0))
    return pl.pallas_call(
        _out_proj_kernel,
        grid=(m // tm,),
        in_specs=[yspec, yspec, yspec, yspec,
                  pl.BlockSpec((D_MODEL, D_MODEL), lambda i: (0, 0)),
                  pl.BlockSpec((tm, D_MODEL), lambda i: (i, 0)),
                  pl.BlockSpec((1, D_MODEL), lambda i: (0, 0))],
        out_specs=pl.BlockSpec((tm, D_MODEL), lambda i: (i, 0)),
        out_shape=jax.ShapeDtypeStruct((m, D_MODEL), F32),
        compiler_params=_params(("parallel",)),
        name="out_proj",
    )(*ys, w_out_b, x2d, final_w.reshape(1, D_MODEL))


def _out_in_proj_kernel(y0_ref, y1_ref, y2_ref, y3_ref, wo_ref, x_ref, nw_ref, w_ref, wt_ref, xo_ref, *out_refs):
    acc = x_ref[...]
    for g, y_ref in enumerate((y0_ref, y1_ref, y2_ref, y3_ref)):
        acc = acc + _dot(y_ref[...], wo_ref[g * GROUP_W:(g + 1) * GROUP_W, :])
    xo_ref[...] = acc
    _norm_project(acc, nw_ref, w_ref, wt_ref, out_refs)


def _out_in_proj(ys, w_out_b, x2d, norm_w, w_r, w_t, tm):
    m = x2d.shape[0]
    yspec = pl.BlockSpec((tm, GROUP_W), lambda i: (i, 0))
    xspec = pl.BlockSpec((tm, D_MODEL), lambda i: (i, 0))
    out_shape, out_specs = _in_proj_outputs(m, tm)
    return pl.pallas_call(
        _out_in_proj_kernel,
        grid=(m // tm,),
        in_specs=[yspec, yspec, yspec, yspec,
                  pl.BlockSpec((D_MODEL, D_MODEL), lambda i: (0, 0)),
                  xspec,
                  pl.BlockSpec((1, D_MODEL), lambda i: (0, 0)),
                  pl.BlockSpec((D_MODEL, IN_WP), lambda i: (0, 0)),
                  pl.BlockSpec((IN_T_ROWS, D_MODEL), lambda i: (0, 0))],
        out_specs=[xspec] + out_specs,
        out_shape=[jax.ShapeDtypeStruct((m, D_MODEL), F32)] + out_shape,
        compiler_params=_params(("parallel",)),
        name="out_in_proj",
    )(*ys, w_out_b, x2d, norm_w.reshape(1, D_MODEL), w_r, w_t)


def _run_interleaved(*pipelines):
    live = list(pipelines)
    while live:
        for g in list(live):
            if next(g, _DONE) is _DONE:
                live.remove(g)


_DONE = object()


def _stage_scores(s_ref, slot, cols, s, keep=None):
    if keep is not None:
        s = jnp.where(keep, s, NEG_INF)
    s_ref[slot, :, cols] = s
    return jnp.max(s, axis=0, keepdims=True)


def _flash_update_t(s, s_max, c1, shift, vt_ext, m_ref, acc_ref, idx, cols):
    m_old = m_ref[idx, :, cols]
    m_new = jnp.maximum(m_old, c1 * s_max + shift)
    alpha = jnp.exp2(m_old - m_new)
    p = jnp.exp2(c1 * s() - (m_new - shift))
    acc_ref[idx, :, cols] = alpha * acc_ref[idx, :, cols] + _dot(vt_ext, p.astype(BF16))
    m_ref[idx, :, cols] = m_new


def _bf16_pieces(x, n=3):
    out = []
    for _ in range(n):
        p = float(np.asarray(x, np.float32).astype(BF16).astype(np.float32))
        out.append(p)
        x = x - p
    return out


def _key_pos_features(tk):
    lane = lax.broadcasted_iota(jnp.int32, (tk, LANES), 1)
    row = lax.broadcasted_iota(jnp.int32, (tk, LANES), 0)
    out = jnp.zeros((tk, LANES), jnp.int32)
    for d in range(-(-tk // POS_RADIX)):
        digit = jnp.bitwise_and(jnp.right_shift(row, 8 * d), POS_RADIX - 1)
        out = jnp.where((lane >= ALIBI_ROWS * d) & (lane < ALIBI_ROWS * (d + 1)), digit, out)
    return out.astype(F32).astype(BF16)


def _alibi_rows(beta, tq, rows=LANES, tk=POS_RADIX):
    row = lax.broadcasted_iota(jnp.int32, (rows, tq), 0)
    out = jnp.zeros((rows, tq), F32)
    for d in range(-(-tk // POS_RADIX)):
        for r, piece in enumerate(_bf16_pieces(beta * POS_RADIX ** d, ALIBI_ROWS)):
            out = jnp.where(row == ALIBI_ROWS * d + r, piece, out)
    return out.astype(BF16)


def _normalized(acc):
    return acc[0:HEAD_DIM] / jnp.maximum(acc[HEAD_DIM:HEAD_DIM + 1], 1e-30)


def _rel_t(tk, tq):
    return lax.broadcasted_iota(jnp.int32, (tk, tq), 1) - lax.broadcasted_iota(jnp.int32, (tk, tq), 0)


def _diff_phases(lam_ref, sw_ref, q_ref, k_ref, vt_ref, z_ref, o_ref, m_ref, acc_ref, wq_ref, s_ref,
                 *, tq, tk, lam_init):
    qi = pl.program_id(1)
    kpq = tq // tk
    scale = DIFF_QK_DIM ** -0.5
    c1 = scale * LOG2E
    m_ref[...] = jnp.full(m_ref.shape, NEG_INF, F32)
    acc_ref[...] = jnp.zeros(acc_ref.shape, F32)
    rel = _rel_t(tk, tq)

    qt = q_ref[...].astype(F32).T.astype(BF16)
    row = lax.broadcasted_iota(jnp.int32, (LANES, tq), 0)
    for j in range(2 * HEADS):
        g, r0 = divmod(j * DIFF_QK_DIM, LANES)
        qg = qt[g * LANES:(g + 1) * LANES]
        wq_ref[j, 0:LANES, :] = jnp.where((row >= r0) & (row < r0 + DIFF_QK_DIM), qg, jnp.zeros_like(qg))
        wq_ref[j, LANES:2 * LANES, :] = _alibi_rows(DIFF_SLOPES[j // 2] / scale, tq, tk=tk)

    ones = jnp.ones((BF16_ROWS, tk), BF16)
    kpos = _key_pos_features(tk)

    n_maps = 2 * HEADS
    n_slots = s_ref.shape[0]

    def tiles(kis, mask_offs):
        loaded = []
        for ki in kis:
            start = pl.multiple_of(ki * tk, tk)
            loaded.append((k_ref[pl.ds(start, tk), :],
                           vt_ref[:, pl.ds(start, tk)],
                           (ki * tk).astype(F32)))
        items = [(t, j) for t in range(len(kis)) for j in range(n_maps)]
        cols = slice(0, tq)

        def scores(i):
            t, j = items[i]
            g = (j * DIFF_QK_DIM) // LANES
            lhs = jnp.concatenate([loaded[t][0][:, g * LANES:(g + 1) * LANES], kpos], axis=1)
            s_ref[i % n_slots] = _dot(lhs, wq_ref[j])

        def staged(i):
            t = items[i][0]
            if mask_offs[t] is None:
                return lambda: s_ref[i % n_slots]
            return lambda: jnp.where(rel >= mask_offs[t], s_ref[i % n_slots], NEG_INF)

        for i in range(QK_LOOKAHEAD):
            scores(i)
        for i, (t, j) in enumerate(items):
            h = j // 2
            if i + QK_LOOKAHEAD < len(items):
                scores(i + QK_LOOKAHEAD)
            _k, vt, key0 = loaded[t]
            vt_ext = jnp.concatenate([vt[h * HEAD_DIM:(h + 1) * HEAD_DIM], ones], axis=0)
            s = staged(i)
            _flash_update_t(s, jnp.max(s(), axis=0, keepdims=True), c1, (DIFF_SLOPES[h] * LOG2E) * key0,
                            vt_ext, m_ref, acc_ref, j, cols)
            yield

    def loop_steps(i):
        return tiles([i * kpq + d for d in range(kpq)], [None] * kpq)

    def diag_steps():
        return tiles([qi * kpq + d for d in range(kpq)], [d * tk for d in range(kpq)])

    def finish():
        lp = lam_ref[...]
        lam = (jnp.exp(jnp.sum(lp[0:1] * lp[1:2], axis=-1, keepdims=True))
               - jnp.exp(jnp.sum(lp[2:3] * lp[3:4], axis=-1, keepdims=True)) + lam_init)
        sw = sw_ref[...]
        o_t = jnp.concatenate([_normalized(acc_ref[2 * h]) - lam * _normalized(acc_ref[2 * h + 1])
                               for h in range(HEADS)], axis=0)
        o = o_t.T
        outs = []
        for h in range(HEADS):
            oh = o[:, h * HEAD_DIM:(h + 1) * HEAD_DIM]
            ms = jnp.mean(oh * oh, axis=-1, keepdims=True)
            outs.append(oh * lax.rsqrt(ms + EPS) * sw * (1.0 - lam_init))
        y = jnp.concatenate(outs, axis=-1) * _silu(z_ref[...])
        o_ref[...] = y.astype(o_ref.dtype)

    return loop_steps, diag_steps, finish


def _diff_operands(qk, vt_all, z_all, lam_p, subln_w, seq, nq, tq, tk):
    inputs = [lam_p, subln_w.reshape(1, HEAD_DIM), qk, qk, vt_all, z_all]
    in_specs = [pl.BlockSpec((4, DIFF_QK_DIM), lambda b, i: (0, 0)),
                pl.BlockSpec((1, HEAD_DIM), lambda b, i: (0, 0)),
                pl.BlockSpec((tq, GROUP_W), lambda b, i: (b * nq + i, 0)),
                pl.BlockSpec((seq, GROUP_W), lambda b, i: (b, 1)),
                pl.BlockSpec((GROUP_W, seq), lambda b, i: (DIFF_VT_BLK, b)),
                pl.BlockSpec((tq, GROUP_W), lambda b, i: (b * nq + i, 1))]
    scratch = [pltpu.VMEM((2 * HEADS, 1, tq), F32),
               pltpu.VMEM((2 * HEADS, ACC_ROWS, tq), F32),
               pltpu.VMEM((2 * HEADS, 2 * LANES, tq), BF16),
               pltpu.VMEM((QK_LOOKAHEAD + 1, tk, tq), F32)]
    return inputs, in_specs, scratch


def _nsa_compress_kernel(cmp_ref, pe_ref, w1_ref, wk2_ref, wv2t_ref, kc_ref, vct_ref):
    ng = kc_ref.shape[1]
    top = jnp.zeros((ng, 2 * CMP_HIDDEN), F32)
    bot = jnp.zeros((ng, 2 * CMP_HIDDEN), F32)
    for p in range(CMP_STRIDE):
        x = cmp_ref[pl.ds(p, ng, stride=CMP_STRIDE), :]
        top = top + _dot((x + pe_ref[p:p + 1, :]).astype(BF16), w1_ref[p])
        bot = bot + _dot((x + pe_ref[CMP_STRIDE + p:CMP_STRIDE + p + 1, :]).astype(BF16), w1_ref[CMP_STRIDE + p])
    hid = _silu(top + pltpu.roll(bot, ng - 1, 0)).astype(BF16)
    hk, hv = hid[:, 0:CMP_HIDDEN], hid[:, CMP_HIDDEN:]
    kc = _dot(hk, wk2_ref[...])
    lane = lax.broadcasted_iota(jnp.int32, kc.shape, 1)
    blk = lax.broadcasted_iota(jnp.int32, kc.shape, 0).astype(F32)
    kc = jnp.where((lane >= HEAD_DIM) & (lane < HEAD_DIM + ALIBI_ROWS), blk, kc)
    kc_ref[0] = kc.astype(kc_ref.dtype)
    vct_ref[0] = _dot_nt(wv2t_ref[...], hv).astype(vct_ref.dtype)


def _nsa_compress(cmp2d, pe_k, pe_v, w_ck1, w_ck2, w_cv1, w_cv2, batch, seq):
    ng = seq // CMP_STRIDE
    wk = w_ck1.reshape(CMP_LEN, HEAD_DIM, CMP_HIDDEN)
    wv = w_cv1.reshape(CMP_LEN, HEAD_DIM, CMP_HIDDEN)
    w1 = jnp.concatenate([jnp.pad(wk, ((0, 0), (0, 0), (0, CMP_HIDDEN))),
                          jnp.pad(wv, ((0, 0), (0, 0), (CMP_HIDDEN, 0)))], axis=1).astype(BF16)
    pe = jnp.concatenate([pe_k, pe_v], axis=1)

    def full(shape):
        return pl.BlockSpec(shape, lambda b: (0,) * len(shape))

    return pl.pallas_call(
        _nsa_compress_kernel,
        grid=(batch,),
        in_specs=[pl.BlockSpec((seq, LANES), lambda b: (b, 0)), full((CMP_LEN, LANES)),
                  full((CMP_LEN, LANES, 2 * CMP_HIDDEN)), full((CMP_HIDDEN, LANES)),
                  full((HEAD_DIM, CMP_HIDDEN))],
        out_specs=[pl.BlockSpec((1, ng, LANES), lambda b: (b, 0, 0)),
                   pl.BlockSpec((1, HEAD_DIM, ng), lambda b: (b, 0, 0))],
        out_shape=[jax.ShapeDtypeStruct((batch, ng, LANES), BF16),
                   jax.ShapeDtypeStruct((batch, HEAD_DIM, ng), BF16)],
        compiler_params=_params(("parallel",)),
        name="nsa_compress",
    )(cmp2d, pe, w1, jnp.pad(w_ck2, ((0, 0), (0, LANES - HEAD_DIM))).astype(BF16), w_cv2.T.astype(BF16))


def _nsa_phases(ovl_ref, q_ref, k_ref, vt_ref, kc_ref, vct_ref, misc_ref, z_ref, o_ref,
                m_ref, acc_ref, wq_ref, s_ref, sc_ref, *, tq, tk, seq):
    qi = pl.program_id(1)
    kpq = tq // tk
    scale = HEAD_DIM ** -0.5
    c1 = scale * LOG2E
    ng = seq // CMP_STRIDE
    ns = seq // SLC_BLOCK
    top = min(SLC_TOPK, ns)
    m_ref[...] = jnp.full(m_ref.shape, NEG_INF, F32)
    acc_ref[...] = jnp.zeros(acc_ref.shape, F32)
    rel = _rel_t(tk, tq)

    qt = q_ref[...].astype(F32).T.astype(BF16)
    zeros_q = jnp.zeros((HEAD_DIM, tq), BF16)
    for h in range(HEADS):
        qh = qt[h * HEAD_DIM:(h + 1) * HEAD_DIM]
        alibi = _alibi_rows(NSA_SLOPES[h] / scale, tq, tk=tk)
        wq_ref[h, 0:LANES, :] = jnp.concatenate([qh, zeros_q], axis=0)
        wq_ref[h, LANES:2 * LANES, :] = alibi
        wq_ref[HEADS + h, 0:LANES, :] = jnp.concatenate([zeros_q, qh], axis=0)
        wq_ref[HEADS + h, LANES:2 * LANES, :] = alibi

    t_lane = qi * tq + lax.broadcasted_iota(jnp.int32, (1, tq), 1)
    n_sub = lax.broadcasted_iota(jnp.int32, (ng, 1), 0)
    c_valid = (n_sub * CMP_STRIDE + (CMP_LEN - 1)) <= t_lane
    kc = kc_ref[0]
    vct_ext = jnp.concatenate([vct_ref[0], jnp.ones((BF16_ROWS, ng), BF16)], axis=0)
    ovl = ovl_ref[...]
    cmp_max = []
    for h in range(HEADS):
        rhs = jnp.concatenate([qt[h * HEAD_DIM:(h + 1) * HEAD_DIM],
                               _alibi_rows(NSA_SLOPES[h] * CMP_STRIDE / scale, tq, HEAD_DIM)], axis=0)
        cmp_max.append(_stage_scores(sc_ref, h, slice(0, tq), _dot(kc, rhs), c_valid))
    imp = jnp.zeros((ns, tq), F32)
    o_cmp = []
    for h in range(HEADS):
        mx = jnp.where(cmp_max[h] > NEG_INF, cmp_max[h], 0.0)
        pb = jnp.exp2(c1 * sc_ref[h] - c1 * mx).astype(BF16)
        o_ext = _dot(vct_ext, pb)
        r = 1.0 / jnp.maximum(o_ext[HEAD_DIM:HEAD_DIM + 1], 1e-30)
        o_cmp.append(o_ext[0:HEAD_DIM] * r)
        imp = imp + _dot(ovl, pb) * r

    def select_blocks():
        j_sub = lax.broadcasted_iota(jnp.int32, (ns, 1), 0)
        j_sub_f = j_sub.astype(F32)
        cur = jnp.right_shift(t_lane, SLC_SHIFT)
        forced = (j_sub == 0) | (j_sub == cur) | (j_sub == cur - 1)
        valid = (j_sub * SLC_BLOCK) <= t_lane
        score = jnp.where(forced, 1e30, jnp.where(valid, imp, -1.0))
        sel = jnp.zeros((ns, tq), F32)
        for _ in range(top):
            mx = jnp.max(score, axis=0, keepdims=True)
            idx = jnp.min(jnp.where(score == mx, j_sub_f, float(ns)), axis=0, keepdims=True)
            pick = j_sub_f == idx
            sel = jnp.where(pick, 1.0, sel)
            score = jnp.where(pick, -2.0, score)
        unsel = jnp.where(sel > 0.5, 0.0, -MASK_BIG).astype(BF16)
        for h in range(HEADS):
            wq_ref[h, LANES + BLK_LANE0:LANES + BLK_LANE0 + ns, :] = unsel

    aux_lane = lax.broadcasted_iota(jnp.int32, (tk, LANES), 1)
    aux_blk = jnp.right_shift(lax.broadcasted_iota(jnp.int32, (tk, LANES), 0), SLC_SHIFT) + BLK_LANE0
    kpos = _key_pos_features(tk)
    ones = jnp.ones((BF16_ROWS, tk), BF16)

    n_slots = s_ref.shape[0]

    def tiles(window, kis, cols, mask_offs=None):
        base = HEADS if window else 0
        loaded = []
        for t, ki in enumerate(kis):
            start = pl.multiple_of(ki * tk, tk)
            k = k_ref[pl.ds(start, tk), :]
            if window:
                vt = vt_ref[HEAD_DIM:2 * HEAD_DIM, pl.ds(start, tk)]
                aux = kpos
                dist = rel[:, cols[t]] + (qi * tq - ki * tk)
                keep = (dist >= 0) & (dist < WINDOW)
            else:
                vt = vt_ref[0:HEAD_DIM, pl.ds(start, tk)]
                onehot = aux_lane == aux_blk + ki * (tk // SLC_BLOCK)
                aux = jnp.where(onehot, jnp.ones_like(kpos), kpos)
                keep = None
            loaded.append((jnp.concatenate([k, aux], axis=1),
                           jnp.concatenate([vt, ones], axis=0), (ki * tk).astype(F32), keep))
        items = [(t, h) for t in range(len(kis)) for h in range(HEADS)]
        s_max = {}

        def scores(i):
            t, h = items[i]
            if window:
                keep = loaded[t][3]
            else:
                keep = None if mask_offs[t] is None else rel[:, cols[t]] >= mask_offs[t]
            s_max[i] = _stage_scores(s_ref, i % n_slots, cols[t],
                                     _dot(loaded[t][0], wq_ref[base + h, :, cols[t]]), keep)

        for i in range(QK_LOOKAHEAD):
            scores(i)
        for i, (t, h) in enumerate(items):
            if i + QK_LOOKAHEAD < len(items):
                scores(i + QK_LOOKAHEAD)
            _lhs, vt_ext, key0, _keep = loaded[t]
            _flash_update_t(lambda i=i, t=t: s_ref[i % n_slots, :, cols[t]], s_max.pop(i), c1,
                            (NSA_SLOPES[h] * LOG2E) * key0, vt_ext, m_ref, acc_ref, base + h, cols[t])
            yield

    all_cols = slice(0, tq)
    diag_cols = [slice(d * tk, tq) for d in range(kpq)]

    _run_interleaved(tiles(True, [qi * kpq + d for d in range(kpq)], diag_cols))
    select_blocks()

    def loop_steps(i):
        return tiles(False, [i * kpq + d for d in range(kpq)], [all_cols] * kpq, [None] * kpq)

    def back_window():
        n_back = (WINDOW + tk - 1) // tk
        for g in range((n_back + kpq - 1) // kpq):
            backs = list(range(g * kpq + 1, min((g + 1) * kpq, n_back) + 1))
            back_cols = [slice(0, min(tq, -(-(WINDOW - (back - 1) * tk - 1) // LANES) * LANES)) for back in backs]

            @pl.when(qi > g)
            def _():
                _run_interleaved(tiles(True, [qi * kpq - back for back in backs], back_cols))

    def diag_steps():
        return tiles(False, [qi * kpq + d for d in range(kpq)], diag_cols, [d * tk for d in range(kpq)])

    def finish():
        g_t = jax.nn.sigmoid(misc_ref[...]).T
        outs = []
        for h in range(HEADS):
            r0 = GATE_COL + 3 * h
            outs.append(g_t[r0:r0 + 1] * o_cmp[h] + g_t[r0 + 1:r0 + 2] * _normalized(acc_ref[h])
                        + g_t[r0 + 2:r0 + 3] * _normalized(acc_ref[HEADS + h]))
        y = jnp.concatenate(outs, axis=0).T * _silu(z_ref[...])
        o_ref[...] = y.astype(o_ref.dtype)

    return loop_steps, back_window, diag_steps, finish


def _overlap_t(seq):
    nc = (seq - CMP_LEN) // CMP_STRIDE + 1
    ng = seq // CMP_STRIDE
    ns = seq // SLC_BLOCK
    c_start = np.arange(ng) * CMP_STRIDE
    c_end = c_start + CMP_LEN - 1
    s_start = np.arange(ns) * SLC_BLOCK
    s_end = s_start + SLC_BLOCK - 1
    ov = (c_start[None, :] <= s_end[:, None]) & (c_end[None, :] >= s_start[:, None]) & (np.arange(ng)[None, :] < nc)
    return jnp.asarray(ov.astype(np.float32), dtype=BF16)


def _nsa_operands(q, k2, vt_all, kc, vct, misc, z_all, seq, nq, tq, tk):
    ng = seq // CMP_STRIDE
    ns = seq // SLC_BLOCK
    assert tq - tk < WINDOW and tk % SLC_BLOCK == 0 and ns <= LANES - BLK_LANE0
    inputs = [_overlap_t(seq), q, k2, vt_all, kc, vct, misc, z_all]
    in_specs = [pl.BlockSpec((ns, ng), lambda b, i: (0, 0)),
                pl.BlockSpec((tq, GROUP_W), lambda b, i: (b * nq + i, 0)),
                pl.BlockSpec((seq, LANES), lambda b, i: (b, 0)),
                pl.BlockSpec((2 * HEAD_DIM, seq), lambda b, i: (NSA_VT_BLK, b)),
                pl.BlockSpec((1, ng, LANES), lambda b, i: (b, 0, 0)),
                pl.BlockSpec((1, HEAD_DIM, ng), lambda b, i: (b, 0, 0)),
                pl.BlockSpec((tq, LANES), lambda b, i: (b * nq + i, 0)),
                pl.BlockSpec((tq, GROUP_W), lambda b, i: (b * nq + i, 0))]
    scratch = [pltpu.VMEM((2 * HEADS, 1, tq), F32),
               pltpu.VMEM((2 * HEADS, ACC_ROWS, tq), F32),
               pltpu.VMEM((2 * HEADS, 2 * LANES, tq), BF16),
               pltpu.VMEM((2 * HEADS, tk, tq), F32),
               pltpu.VMEM((HEADS, ng, tq), F32)]
    return inputs, in_specs, scratch


def _mixers_kernel(*refs, n_in, n_scratch, tq, tk_nsa, tk_diff, seq, lam_init):
    bounds = np.cumsum([0] + list(n_in))
    nsa_in, diff_in, ret_in, ssd_in = (refs[a:b] for a, b in zip(bounds[:-1], bounds[1:]))
    nsa_out, diff_out, ret_out, ssd_out = refs[bounds[-1]:bounds[-1] + 4]
    sb = np.cumsum([0] + list(n_scratch)) + bounds[-1] + 4
    nsa_scr, diff_scr, rec_scr = (refs[a:b] for a, b in zip(sb[:-1], sb[1:]))
    ret_st, ssd_ext, ssd_st = rec_scr
    qi = pl.program_id(1)

    @pl.when(qi == 0)
    def _():
        ret_st[...] = jnp.zeros(ret_st.shape, F32)
        ssd_st[...] = jnp.zeros(ssd_st.shape, F32)
        ssd_ext[0:8, :] = jnp.zeros((8, CONV_CH), F32)

    chunks = list(range(ret_out.shape[0]))
    first, second = chunks[:len(chunks) // 2], chunks[len(chunks) // 2:]
    nsa_loop, nsa_back_window, nsa_diag, nsa_finish = _nsa_phases(
        *nsa_in, nsa_out, *nsa_scr, tq=tq, tk=tk_nsa, seq=seq)
    diff_loop, diff_diag, diff_finish = _diff_phases(
        *diff_in, diff_out, *diff_scr, tq=tq, tk=tk_diff, lam_init=lam_init)
    _ret_body(*ret_in, ret_out, ret_st, first)
    _ssd_body(*ssd_in, ssd_out, ssd_ext, ssd_st, first)

    def body(i, carry):
        _run_interleaved(diff_loop(i), nsa_loop(i))
        return carry

    lax.fori_loop(0, qi, body, 0)
    nsa_back_window()
    _run_interleaved(nsa_diag(), diff_diag())
    _ret_body(*ret_in, ret_out, ret_st, second)
    _ssd_body(*ssd_in, ssd_out, ssd_ext, ssd_st, second)
    nsa_finish()
    diff_finish()


def _mixers(nsa_args, diff_args, rec_args, batch, seq, layer_idx, tq, tk_nsa, tk_diff):
    nq = seq // tq
    nsa_in, nsa_specs, nsa_scratch = _nsa_operands(*nsa_args, seq, nq, tq, tk_nsa)
    diff_in, diff_specs, diff_scratch = _diff_operands(*diff_args, seq, nq, tq, tk_diff)
    ret_in, ssd_in, rec_specs, rec_out_spec, rec_scratch = _recurrent_operands(*rec_args, nq, tq)
    out_spec = pl.BlockSpec((tq, GROUP_W), lambda b, i: (b * nq + i, 0))
    m = batch * seq
    y_nsa, y_diff, y_ret, y_ssm = pl.pallas_call(
        functools.partial(_mixers_kernel, n_in=(len(nsa_in), len(diff_in), len(ret_in), len(ssd_in)),
                          n_scratch=(len(nsa_scratch), len(diff_scratch), len(rec_scratch)),
                          tq=tq, tk_nsa=tk_nsa, tk_diff=tk_diff, seq=seq,
                          lam_init=0.8 - 0.6 * math.exp(-0.3 * layer_idx)),
        grid=(batch, nq),
        in_specs=nsa_specs + diff_specs + rec_specs,
        out_specs=[out_spec, out_spec, rec_out_spec, rec_out_spec],
        out_shape=[jax.ShapeDtypeStruct((m, GROUP_W), BF16)] * 2
                  + [jax.ShapeDtypeStruct((m // SSM_CHUNK, SSM_CHUNK, GROUP_W), BF16)] * 2,
        scratch_shapes=nsa_scratch + diff_scratch + rec_scratch,
        compiler_params=_params(("parallel", "arbitrary")),
        name="mixers",
    )(*nsa_in, *diff_in, *ret_in, *ssd_in)
    return y_nsa, y_diff, y_ret.reshape(m, GROUP_W), y_ssm.reshape(m, GROUP_W)


def _ret_tables():
    c = RET_CHUNK
    h = np.arange(HEADS, dtype=np.float32)
    log_g = jnp.log(1.0 - 2.0 ** (-5.0 - jnp.asarray(h)))
    pos = jnp.arange(c, dtype=F32)
    rel = pos[:, None] - pos[None, :]
    decay = jnp.where(rel >= 0, jnp.exp(log_g[:, None, None] * jnp.maximum(rel, 0.0)), 0.0)
    xi = jnp.exp(log_g[:, None] * (pos + 1.0))
    zeta = jnp.exp(log_g[:, None] * (c - 1.0 - pos))
    chunk_decay = jnp.exp(log_g * c)
    xi_tab = jnp.repeat(xi.T, HEAD_DIM, axis=1)
    zeta_tab = jnp.repeat(zeta.T, HEAD_DIM, axis=1)
    cd_tab = jnp.repeat(chunk_decay, HEAD_DIM)[None, :]
    return decay, xi_tab, zeta_tab, cd_tab


def _ret_body(decay_ref, xi_ref, zeta_ref, cd_ref, gn_ref, q_ref, k_ref, v_ref, z_ref, o_ref, st_ref, blocks):
    xi = xi_ref[...]
    cd = cd_ref[...]
    for bb in blocks:
        q = (q_ref[bb].astype(F32) * (HEAD_DIM ** -0.5)).astype(BF16)
        k = k_ref[bb]
        v = v_ref[bb]
        kz_t = (k.astype(F32) * zeta_ref[...]).T.astype(BF16)
        outs = []
        for h in range(HEADS):
            sl = slice(h * HEAD_DIM, (h + 1) * HEAD_DIM)
            qh, kh, vh = q[:, sl], k[:, sl], v[:, sl]
            prev = st_ref[h]
            inner = (_dot_nt(qh, kh) * decay_ref[h]).astype(BF16)
            o = _dot(inner, vh) + _dot(qh, prev.astype(BF16)) * xi[:, sl]
            st_ref[h] = prev * cd[:, sl] + _dot(kz_t[sl, :], vh)
            mu = jnp.mean(o, axis=-1, keepdims=True)
            d = o - mu
            var = jnp.mean(d * d, axis=-1, keepdims=True)
            outs.append(d * lax.rsqrt(var + EPS))
        y = jnp.concatenate(outs, axis=-1) * gn_ref[...] * _silu(z_ref[bb])
        o_ref[bb] = y.astype(o_ref.dtype)


def _ssd_body(cw_ref, cb_ref, dtb_ref, a_ref, dsk_ref, nw_ref, xbc_ref, misc_ref, z_ref, o_ref,
              ext_ref, st_ref, blocks):
    L = SSM_CHUNK
    hi = lax.Precision.HIGHEST
    row = lax.broadcasted_iota(jnp.int32, (L, L), 0)
    col = lax.broadcasted_iota(jnp.int32, (L, L), 1)
    causal = row >= col
    tril = jnp.where(causal, 1.0, 0.0).astype(F32)
    dsk = dsk_ref[...]

    for bb in blocks:
        raw = xbc_ref[bb]
        ext_ref[8:8 + L, :] = raw
        conv = cb_ref[...] + raw * cw_ref[CONV_W - 1:CONV_W, :]
        for w in range(CONV_W - 1):
            shift = CONV_W - 1 - w
            conv = conv + ext_ref[8 - shift:8 - shift + L, :] * cw_ref[w:w + 1, :]
        ext_ref[0:8, :] = raw[L - 8:L, :]
        xc = _silu(conv)
        x = xc[:, 0:GROUP_W]
        bm = xc[:, GROUP_W:GROUP_W + 2 * SSM_STATE].astype(BF16)
        cm = xc[:, GROUP_W + 2 * SSM_STATE:].astype(BF16)

        dt_full = jax.nn.softplus(misc_ref[bb] + dtb_ref[...])
        da = dt_full * a_ref[...]
        cs_col = jnp.dot(tril, da, precision=hi, preferred_element_type=F32)
        cs_row = lax.dot_general(da, tril, (((0,), (1,)), ((), ())), precision=hi,
                                 preferred_element_type=F32)

        outs = []
        for h in range(HEADS):
            g = h // 2
            c0 = DT_COL + h
            sl = slice(h * HEAD_DIM, (h + 1) * HEAD_DIM)
            gs = slice(g * SSM_STATE, (g + 1) * SSM_STATE)
            cs_c = cs_col[:, c0:c0 + 1]
            cs_r = cs_row[c0:c0 + 1, :]
            cs_last = cs_col[L - 1:L, c0:c0 + 1]
            xh = x[:, sl]
            xdt = xh * dt_full[:, c0:c0 + 1]
            seg = jnp.exp(jnp.where(causal, cs_c - cs_r, NEG_INF))
            cb = _dot_nt(cm[:, gs], bm[:, gs])
            y = _dot((cb * seg).astype(BF16), xdt.astype(BF16))
            prev = st_ref[h]
            y = y + _dot(cm[:, gs], prev.astype(BF16)) * jnp.exp(cs_c)
            y = y + dsk[:, sl] * xh
            dec = jnp.exp(cs_last - cs_c)
            st_ref[h] = prev * jnp.exp(cs_last) + _dot_tn(bm[:, gs], (xdt * dec).astype(BF16))
            outs.append(y)
        y = jnp.concatenate(outs, axis=-1) * _silu(z_ref[bb])
        ms = jnp.mean(y * y, axis=-1, keepdims=True)
        o_ref[bb] = (y * lax.rsqrt(ms + EPS) * nw_ref[...]).astype(o_ref.dtype)


def _recurrent_operands(ret_qkv, xbc, misc, z_all, gn_w, conv_w, conv_b, dt_bias, a_log, d_skip, norm_w, nq, tq):
    L = SSM_CHUNK
    assert RET_CHUNK == L and tq % L == 0
    cpt = tq // L
    decay, xi_tab, zeta_tab, cd_tab = _ret_tables()
    dtb = jnp.zeros((1, 128), F32).at[0, DT_COL:DT_COL + HEADS].set(dt_bias)
    a_full = jnp.zeros((1, 128), F32).at[0, DT_COL:DT_COL + HEADS].set(-jnp.exp(a_log))
    dsk = jnp.repeat(d_skip, HEAD_DIM)[None, :]

    def full(shape):
        return pl.BlockSpec(shape, lambda b, i: (0,) * len(shape))

    def blk(width, col):
        return pl.BlockSpec((cpt, L, width), lambda b, i: (b * nq + i, 0, col))

    def chunked(a):
        return a.reshape(-1, L, a.shape[-1])

    qkv3, z3 = chunked(ret_qkv), chunked(z_all)
    ret_in = [decay, xi_tab, zeta_tab, cd_tab, gn_w.reshape(1, GROUP_W), qkv3, qkv3, qkv3, z3]
    ret_specs = [full((HEADS, L, L)), full((L, GROUP_W)), full((L, GROUP_W)), full((1, GROUP_W)),
                 full((1, GROUP_W)), blk(GROUP_W, 0), blk(GROUP_W, 1), blk(GROUP_W, 2), blk(GROUP_W, 2)]
    ssd_in = [conv_w, conv_b.reshape(1, CONV_CH), dtb, a_full, dsk, norm_w.reshape(1, GROUP_W),
              chunked(xbc), chunked(misc), z3]
    ssd_specs = [full((CONV_W, CONV_CH)), full((1, CONV_CH)), full((1, 128)), full((1, 128)),
                 full((1, GROUP_W)), full((1, GROUP_W)), blk(CONV_CH, 0), blk(128, 0), blk(GROUP_W, 3)]
    scratch = [pltpu.VMEM((HEADS, HEAD_DIM, HEAD_DIM), F32),
               pltpu.VMEM((8 + L, CONV_CH), F32),
               pltpu.VMEM((HEADS, SSM_STATE, HEAD_DIM), F32)]
    return ret_in, ssd_in, ret_specs + ssd_specs, blk(GROUP_W, 0), scratch


def _pick_tile(n, pref):
    t = pref
    while n % t:
        t //= 2
    return t


def kernel(x, norm_w, w_in, w_out, nsa_pe_k, nsa_pe_v, nsa_w_ck1, nsa_w_ck2, nsa_w_cv1, nsa_w_cv2,
           diff_lam_q1, diff_lam_k1, diff_lam_q2, diff_lam_k2, diff_subln_w, ret_gn_w,
           ssm_conv_w, ssm_conv_b, ssm_dt_bias, ssm_A_log, ssm_D, ssm_norm_w, final_norm_w):
    batch, seq, _ = x.shape
    depth = w_in.shape[0]
    m = batch * seq
    tm = _pick_tile(m, 512)
    tq = _pick_tile(seq, 512)
    tk = _pick_tile(seq, 256)
    tk_diff = _pick_tile(seq, 512)
    w_in_b = w_in.astype(BF16)
    w_r = _relayout_w_in(w_in_b)
    w_t = _relayout_w_in_t(w_in_b)
    w_out_b = w_out.astype(BF16)
    x2d = x.reshape(m, D_MODEL)
    projected = _in_proj(x2d, norm_w[0], w_r[0], w_t[0], tm)
    for i in range(depth):
        nsa_q, nsa_k2, nsa_cmp, misc, z_all, diff_qk, ret_qkv, xbc, vt_all = projected
        kc, vct = _nsa_compress(nsa_cmp, nsa_pe_k[i], nsa_pe_v[i], nsa_w_ck1[i], nsa_w_ck2[i],
                                nsa_w_cv1[i], nsa_w_cv2[i], batch, seq)
        lam_p = jnp.stack([diff_lam_q1[i], diff_lam_k1[i], diff_lam_q2[i], diff_lam_k2[i]])
        ys = _mixers((nsa_q, nsa_k2, vt_all, kc, vct, misc, z_all),
                     (diff_qk, vt_all, z_all, lam_p, diff_subln_w[i]),
                     (ret_qkv, xbc, misc, z_all, ret_gn_w[i], ssm_conv_w[i], ssm_conv_b[i], ssm_dt_bias[i],
                      ssm_A_log[i], ssm_D[i], ssm_norm_w[i]),
                     batch, seq, i, tq, tk, tk_diff)
        if i + 1 < depth:
            x2d, *projected = _out_in_proj(ys, w_out_b[i], x2d, norm_w[i + 1], w_r[i + 1], w_t[i + 1], tm)
        else:
            x2d = _out_proj(ys, w_out_b[i], x2d, final_norm_w, tm)
    return x2d.reshape(batch, seq, D_MODEL)
```

```python
import functools
import math

import numpy as np
import jax
import jax.numpy as jnp
from jax import lax
from jax.experimental import pallas as pl
from jax.experimental.pallas import tpu as pltpu

F32 = jnp.float32
BF16 = jnp.bfloat16
NEG_INF = float("-inf")
LOG2E = 1.4426950408889634

D_MODEL = 1024
DEPTH = 4
GROUP_W = 256
HEADS = 4
HEAD_DIM = 64
EPS = 1e-6
CMP_LEN = 32
CMP_STRIDE = 16
CMP_HIDDEN = 256
SLC_BLOCK = 64
SLC_SHIFT = 6
SLC_TOPK = 16
WINDOW = 512
DIFF_QK_DIM = 32
RET_CHUNK = 128
SSM_STATE = 128
SSM_CHUNK = 128
CONV_W = 4
CONV_CH = 768
N_ALIBI_HEADS = 8
LANES = 128
BF16_ROWS = 16
ACC_ROWS = HEAD_DIM + BF16_ROWS
QK_LOOKAHEAD = 3
ALIBI_ROWS = 3
POS_RADIX = 256
MASK_BIG = 2.0 ** 100
BLK_LANE0 = 64

IN_LAYOUT = (
    ("nsa_q", 256), ("nsa_k_cmp", 64), ("nsa_v_cmp", 64), ("nsa_k_slc", 64), ("nsa_v_slc", 64),
    ("nsa_k_win", 64), ("nsa_v_win", 64), ("nsa_gate", 12), ("nsa_z", 256),
    ("diff_q", 256), ("diff_k", 256), ("diff_v", 256), ("diff_z", 256),
    ("ret_q", 256), ("ret_k", 256), ("ret_v", 256), ("ret_z", 256),
    ("ssm_z", 256), ("ssm_xbc", 768), ("ssm_dt", 4),
)
IN_OFF = {}
_o = 0
for _n, _w in IN_LAYOUT:
    IN_OFF[_n] = (_o, _w)
    _o += _w
IN_W = _o

GATE_COL = 0
DT_COL = 12
IN_OUTPUTS = (
    ("nsa_q", BF16, ("nsa_q",), 256),
    ("nsa_k2", BF16, ("nsa_k_slc", "nsa_k_win"), 128),
    ("nsa_cmp", F32, ("nsa_k_cmp", "nsa_v_cmp"), 128),
    ("misc", F32, ("nsa_gate", "ssm_dt"), 128),
    ("z_all", F32, ("nsa_z", "diff_z", "ret_z", "ssm_z"), 1024),
    ("diff_qk", BF16, ("diff_q", "diff_k"), 512),
    ("ret_qkv", BF16, ("ret_q", "ret_k", "ret_v"), 768),
    ("xbc", F32, ("ssm_xbc",), 768),
)
IN_T_SRC = ("diff_v", "nsa_v_slc", "nsa_v_win")
IN_T_ROWS = 384
DIFF_VT_BLK = 0
NSA_VT_BLK = 2
IN_SEGS = []
_o = 0
for _n, _dt, _src, _w in IN_OUTPUTS:
    IN_SEGS.append((_o, _o + _w))
    _o += _w
IN_WP = _o

VMEM_LIMIT = 56 * 1024 * 1024


def _alibi_slopes():
    return [2.0 ** (-8.0 * (i + 1) / N_ALIBI_HEADS) for i in range(N_ALIBI_HEADS)]


NSA_SLOPES = _alibi_slopes()[0::2]
DIFF_SLOPES = _alibi_slopes()[1::2]


def _silu(x):
    return x * jax.nn.sigmoid(x)


def _dot(a, b):
    return jnp.dot(a, b, preferred_element_type=F32)


def _dot_nt(a, b):
    return lax.dot_general(a, b, (((1,), (1,)), ((), ())), preferred_element_type=F32)


def _dot_tn(a, b):
    return lax.dot_general(a, b, (((0,), (0,)), ((), ())), preferred_element_type=F32)


def _params(sem):
    return pltpu.CompilerParams(dimension_semantics=sem, vmem_limit_bytes=VMEM_LIMIT)


def _relayout_w_in(w_in):
    cols = []
    for _n, _dt, src, width in IN_OUTPUTS:
        used = 0
        for s in src:
            off, w = IN_OFF[s]
            cols.append(w_in[:, :, off:off + w])
            used += w
        if used < width:
            cols.append(jnp.zeros(w_in.shape[:2] + (width - used,), w_in.dtype))
    return jnp.concatenate(cols, axis=-1)


def _relayout_w_in_t(w_in):
    cols = [w_in[:, :, IN_OFF[s][0]:IN_OFF[s][0] + IN_OFF[s][1]] for s in IN_T_SRC]
    return jnp.swapaxes(jnp.concatenate(cols, axis=-1), 1, 2)


def _norm_project(x, nw_ref, w_ref, wt_ref, out_refs):
    ms = jnp.mean(x * x, axis=-1, keepdims=True)
    h = (x * lax.rsqrt(ms + EPS) * nw_ref[...]).astype(BF16)
    for ref, (a, b) in zip(out_refs[:-1], IN_SEGS):
        ref[...] = _dot(h, w_ref[:, a:b]).astype(ref.dtype)
    out_refs[-1][...] = _dot_nt(wt_ref[...], h).astype(BF16)


def _in_proj_kernel(x_ref, nw_ref, w_ref, wt_ref, *out_refs):
    _norm_project(x_ref[...], nw_ref, w_ref, wt_ref, out_refs)


def _in_proj_outputs(m, tm):
    out_shape = [jax.ShapeDtypeStruct((m, b - a), dt) for (_n, dt, _s, _w), (a, b) in zip(IN_OUTPUTS, IN_SEGS)]
    out_specs = [pl.BlockSpec((tm, b - a), lambda i: (i, 0)) for (a, b) in IN_SEGS]
    out_shape.append(jax.ShapeDtypeStruct((IN_T_ROWS, m), BF16))
    out_specs.append(pl.BlockSpec((IN_T_ROWS, tm), lambda i: (0, i)))
    return out_shape, out_specs


def _in_proj(x2d, norm_w, w_r, w_t, tm):
    m = x2d.shape[0]
    out_shape, out_specs = _in_proj_outputs(m, tm)
    return pl.pallas_call(
        _in_proj_kernel,
        grid=(m // tm,),
        in_specs=[pl.BlockSpec((tm, D_MODEL), lambda i: (i, 0)),
                  pl.BlockSpec((1, D_MODEL), lambda i: (0, 0)),
                  pl.BlockSpec((D_MODEL, IN_WP), lambda i: (0, 0)),
                  pl.BlockSpec((IN_T_ROWS, D_MODEL), lambda i: (0, 0))],
        out_specs=out_specs,
        out_shape=out_shape,
        compiler_params=_params(("parallel",)),
        name="in_proj",
    )(x2d, norm_w.reshape(1, D_MODEL), w_r, w_t)


def _out_proj_kernel(y0_ref, y1_ref, y2_ref, y3_ref, w_ref, x_ref, fw_ref, o_ref):
    acc = x_ref[...]
    for g, y_ref in enumerate((y0_ref, y1_ref, y2_ref, y3_ref)):
        acc = acc + _dot(y_ref[...], w_ref[g * GROUP_W:(g + 1) * GROUP_W, :])
    ms = jnp.mean(acc * acc, axis=-1, keepdims=True)
    o_ref[...] = acc * lax.rsqrt(ms + EPS) * fw_ref[...]


def _out_proj(ys, w_out_b, x2d, final_w, tm):
    m = x2d.shape[0]
    yspec = pl.BlockSpec((tm, GROUP_W), lambda i: (i, 0))
    return pl.pallas_call(
        _out_proj_kernel,
        grid=(m // tm,),
        in_specs=[yspec, yspec, yspec, yspec,
                  pl.BlockSpec((D_MODEL, D_MODEL), lambda i: (0, 0)),
                  pl.BlockSpec((tm, D_MODEL), lambda i: (i, 0)),
                  pl.BlockSpec((1, D_MODEL), lambda i: (0, 0))],
        out_specs=pl.BlockSpec((tm, D_MODEL), lambda i: (i, 0)),
        out_shape=jax.ShapeDtypeStruct((m, D_MODEL), F32),
        compiler_params=_params(("parallel",)),
        name="out_proj",
    )(*ys, w_out_b, x2d, final_w.reshape(1, D_MODEL))


def _out_in_proj_kernel(y0_ref, y1_ref, y2_ref, y3_ref, wo_ref, x_ref, nw_ref, w_ref, wt_ref, xo_ref, *out_refs):
    acc = x_ref[...]
    for g, y_ref in enumerate((y0_ref, y1_ref, y2_ref, y3_ref)):
        acc = acc + _dot(y_ref[...], wo_ref[g * GROUP_W:(g + 1) * GROUP_W, :])
    xo_ref[...] = acc
    _norm_project(acc, nw_ref, w_ref, wt_ref, out_refs)


def _out_in_proj(ys, w_out_b, x2d, norm_w, w_r, w_t, tm):
    m = x2d.shape[0]
    yspec = pl.BlockSpec((tm, GROUP_W), lambda i: (i, 0))
    xspec = pl.BlockSpec((tm, D_MODEL), lambda i: (i, 0))
    out_shape, out_specs = _in_proj_outputs(m, tm)
    return pl.pallas_call(
        _out_in_proj_kernel,
        grid=(m // tm,),
        in_specs=[yspec, yspec, yspec, yspec,
                  pl.BlockSpec((D_MODEL, D_MODEL), lambda i: (0, 0)),
                  xspec,
                  pl.BlockSpec((1, D_MODEL), lambda i: (0, 0)),
                  pl.BlockSpec((D_MODEL, IN_WP), lambda i: (0, 0)),
                  pl.BlockSpec((IN_T_ROWS, D_MODEL), lambda i: (0, 0))],
        out_specs=[xspec] + out_specs,
        out_shape=[jax.ShapeDtypeStruct((m, D_MODEL), F32)] + out_shape,
        compiler_params=_params(("parallel",)),
        name="out_in_proj",
    )(*ys, w_out_b, x2d, norm_w.reshape(1, D_MODEL), w_r, w_t)


def _run_interleaved(*pipelines):
    live = list(pipelines)
    while live:
        for g in list(live):
            if next(g, _DONE) is _DONE:
                live.remove(g)


_DONE = object()


def _stage_scores(s_ref, slot, cols, s, keep=None):
    if keep is not None:
        s = jnp.where(keep, s, NEG_INF)
    s_ref[slot, :, cols] = s
    return jnp.max(s, axis=0, keepdims=True)


def _flash_update_t(s, s_max, c1, shift, vt_ext, m_ref, acc_ref, idx, cols):
    m_old = m_ref[idx, :, cols]
    m_new = jnp.maximum(m_old, c1 * s_max + shift)
    alpha = jnp.exp2(m_old - m_new)
    p = jnp.exp2(c1 * s() - (m_new - shift))
    acc_ref[idx, :, cols] = alpha * acc_ref[idx, :, cols] + _dot(vt_ext, p.astype(BF16))
    m_ref[idx, :, cols] = m_new


def _bf16_pieces(x, n=3):
    out = []
    for _ in range(n):
        p = float(np.asarray(x, np.float32).astype(BF16).astype(np.float32))
        out.append(p)
        x = x - p
    return out


def _key_pos_features(tk):
    lane = lax.broadcasted_iota(jnp.int32, (tk, LANES), 1)
    row = lax.broadcasted_iota(jnp.int32, (tk, LANES), 0)
    out = jnp.zeros((tk, LANES), jnp.int32)
    for d in range(-(-tk // POS_RADIX)):
        digit = jnp.bitwise_and(jnp.right_shift(row, 8 * d), POS_RADIX - 1)
        out = jnp.where((lane >= ALIBI_ROWS * d) & (lane < ALIBI_ROWS * (d + 1)), digit, out)
    return out.astype(F32).astype(BF16)


def _alibi_rows(beta, tq, rows=LANES, tk=POS_RADIX):
    row = lax.broadcasted_iota(jnp.int32, (rows, tq), 0)
    out = jnp.zeros((rows, tq), F32)
    for d in range(-(-tk // POS_RADIX)):
        for r, piece in enumerate(_bf16_pieces(beta * POS_RADIX ** d, ALIBI_ROWS)):
            out = jnp.where(row == ALIBI_ROWS * d + r, piece, out)
    return out.astype(BF16)


def _normalized(acc):
    return acc[0:HEAD_DIM] / jnp.maximum(acc[HEAD_DIM:HEAD_DIM + 1], 1e-30)


def _rel_t(tk, tq):
    return lax.broadcasted_iota(jnp.int32, (tk, tq), 1) - lax.broadcasted_iota(jnp.int32, (tk, tq), 0)


def _diff_phases(lam_ref, sw_ref, q_ref, k_ref, vt_ref, z_ref, o_ref, m_ref, acc_ref, wq_ref, s_ref,
                 *, tq, tk, lam_init):
    qi = pl.program_id(1)
    kpq = tq // tk
    scale = DIFF_QK_DIM ** -0.5
    c1 = scale * LOG2E
    m_ref[...] = jnp.full(m_ref.shape, NEG_INF, F32)
    acc_ref[...] = jnp.zeros(acc_ref.shape, F32)
    rel = _rel_t(tk, tq)

    qt = q_ref[...].astype(F32).T.astype(BF16)
    row = lax.broadcasted_iota(jnp.int32, (LANES, tq), 0)
    for j in range(2 * HEADS):
        g, r0 = divmod(j * DIFF_QK_DIM, LANES)
        qg = qt[g * LANES:(g + 1) * LANES]
        wq_ref[j, 0:LANES, :] = jnp.where((row >= r0) & (row < r0 + DIFF_QK_DIM), qg, jnp.zeros_like(qg))
        wq_ref[j, LANES:2 * LANES, :] = _alibi_rows(DIFF_SLOPES[j // 2] / scale, tq, tk=tk)

    ones = jnp.ones((BF16_ROWS, tk), BF16)
    kpos = _key_pos_features(tk)

    n_maps = 2 * HEADS
    n_slots = s_ref.shape[0]

    def tiles(kis, mask_offs):
        loaded = []
        for ki in kis:
            start = pl.multiple_of(ki * tk, tk)
            loaded.append((k_ref[pl.ds(start, tk), :],
                           vt_ref[:, pl.ds(start, tk)],
                           (ki * tk).astype(F32)))
        items = [(t, j) for t in range(len(kis)) for j in range(n_maps)]
        cols = slice(0, tq)

        def scores(i):
            t, j = items[i]
            g = (j * DIFF_QK_DIM) // LANES
            lhs = jnp.concatenate([loaded[t][0][:, g * LANES:(g + 1) * LANES], kpos], axis=1)
            s_ref[i % n_slots] = _dot(lhs, wq_ref[j])

        def staged(i):
            t = items[i][0]
            if mask_offs[t] is None:
                return lambda: s_ref[i % n_slots]
            return lambda: jnp.where(rel >= mask_offs[t], s_ref[i % n_slots], NEG_INF)

        for i in range(QK_LOOKAHEAD):
            scores(i)
        for i, (t, j) in enumerate(items):
            h = j // 2
            if i + QK_LOOKAHEAD < len(items):
                scores(i + QK_LOOKAHEAD)
            _k, vt, key0 = loaded[t]
            vt_ext = jnp.concatenate([vt[h * HEAD_DIM:(h + 1) * HEAD_DIM], ones], axis=0)
            s = staged(i)
            _flash_update_t(s, jnp.max(s(), axis=0, keepdims=True), c1, (DIFF_SLOPES[h] * LOG2E) * key0,
                            vt_ext, m_ref, acc_ref, j, cols)
            yield

    def loop_steps(i):
        return tiles([i * kpq + d for d in range(kpq)], [None] * kpq)

    def diag_steps():
        return tiles([qi * kpq + d for d in range(kpq)], [d * tk for d in range(kpq)])

    def finish():
        lp = lam_ref[...]
        lam = (jnp.exp(jnp.sum(lp[0:1] * lp[1:2], axis=-1, keepdims=True))
               - jnp.exp(jnp.sum(lp[2:3] * lp[3:4], axis=-1, keepdims=True)) + lam_init)
        sw = sw_ref[...]
        o_t = jnp.concatenate([_normalized(acc_ref[2 * h]) - lam * _normalized(acc_ref[2 * h + 1])
                               for h in range(HEADS)], axis=0)
        o = o_t.T
        outs = []
        for h in range(HEADS):
            oh = o[:, h * HEAD_DIM:(h + 1) * HEAD_DIM]
            ms = jnp.mean(oh * oh, axis=-1, keepdims=True)
            outs.append(oh * lax.rsqrt(ms + EPS) * sw * (1.0 - lam_init))
        y = jnp.concatenate(outs, axis=-1) * _silu(z_ref[...])
        o_ref[...] = y.astype(o_ref.dtype)

    return loop_steps, diag_steps, finish


def _diff_operands(qk, vt_all, z_all, lam_p, subln_w, seq, nq, tq, tk):
    inputs = [lam_p, subln_w.reshape(1, HEAD_DIM), qk, qk, vt_all, z_all]
    in_specs = [pl.BlockSpec((4, DIFF_QK_DIM), lambda b, i: (0, 0)),
                pl.BlockSpec((1, HEAD_DIM), lambda b, i: (0, 0)),
                pl.BlockSpec((tq, GROUP_W), lambda b, i: (b * nq + i, 0)),
                pl.BlockSpec((seq, GROUP_W), lambda b, i: (b, 1)),
                pl.BlockSpec((GROUP_W, seq), lambda b, i: (DIFF_VT_BLK, b)),
                pl.BlockSpec((tq, GROUP_W), lambda b, i: (b * nq + i, 1))]
    scratch = [pltpu.VMEM((2 * HEADS, 1, tq), F32),
               pltpu.VMEM((2 * HEADS, ACC_ROWS, tq), F32),
               pltpu.VMEM((2 * HEADS, 2 * LANES, tq), BF16),
               pltpu.VMEM((QK_LOOKAHEAD + 1, tk, tq), F32)]
    return inputs, in_specs, scratch


def _nsa_compress_kernel(cmp_ref, pe_ref, w1_ref, wk2_ref, wv2t_ref, kc_ref, vct_ref):
    ng = kc_ref.shape[1]
    top = jnp.zeros((ng, 2 * CMP_HIDDEN), F32)
    bot = jnp.zeros((ng, 2 * CMP_HIDDEN), F32)
    for p in range(CMP_STRIDE):
        x = cmp_ref[pl.ds(p, ng, stride=CMP_STRIDE), :]
        top = top + _dot((x + pe_ref[p:p + 1, :]).astype(BF16), w1_ref[p])
        bot = bot + _dot((x + pe_ref[CMP_STRIDE + p:CMP_STRIDE + p + 1, :]).astype(BF16), w1_ref[CMP_STRIDE + p])
    hid = _silu(top + pltpu.roll(bot, ng - 1, 0)).astype(BF16)
    hk, hv = hid[:, 0:CMP_HIDDEN], hid[:, CMP_HIDDEN:]
    kc = _dot(hk, wk2_ref[...])
    lane = lax.broadcasted_iota(jnp.int32, kc.shape, 1)
    blk = lax.broadcasted_iota(jnp.int32, kc.shape, 0).astype(F32)
    kc = jnp.where((lane >= HEAD_DIM) & (lane < HEAD_DIM + ALIBI_ROWS), blk, kc)
    kc_ref[0] = kc.astype(kc_ref.dtype)
    vct_ref[0] = _dot_nt(wv2t_ref[...], hv).astype(vct_ref.dtype)


def _nsa_compress(cmp2d, pe_k, pe_v, w_ck1, w_ck2, w_cv1, w_cv2, batch, seq):
    ng = seq // CMP_STRIDE
    wk = w_ck1.reshape(CMP_LEN, HEAD_DIM, CMP_HIDDEN)
    wv = w_cv1.reshape(CMP_LEN, HEAD_DIM, CMP_HIDDEN)
    w1 = jnp.concatenate([jnp.pad(wk, ((0, 0), (0, 0), (0, CMP_HIDDEN))),
                          jnp.pad(wv, ((0, 0), (0, 0), (CMP_HIDDEN, 0)))], axis=1).astype(BF16)
    pe = jnp.concatenate([pe_k, pe_v], axis=1)

    def full(shape):
        return pl.BlockSpec(shape, lambda b: (0,) * len(shape))

    return pl.pallas_call(
        _nsa_compress_kernel,
        grid=(batch,),
        in_specs=[pl.BlockSpec((seq, LANES), lambda b: (b, 0)), full((CMP_LEN, LANES)),
                  full((CMP_LEN, LANES, 2 * CMP_HIDDEN)), full((CMP_HIDDEN, LANES)),
                  full((HEAD_DIM, CMP_HIDDEN))],
        out_specs=[pl.BlockSpec((1, ng, LANES), lambda b: (b, 0, 0)),
                   pl.BlockSpec((1, HEAD_DIM, ng), lambda b: (b, 0, 0))],
        out_shape=[jax.ShapeDtypeStruct((batch, ng, LANES), BF16),
                   jax.ShapeDtypeStruct((batch, HEAD_DIM, ng), BF16)],
        compiler_params=_params(("parallel",)),
        name="nsa_compress",
    )(cmp2d, pe, w1, jnp.pad(w_ck2, ((0, 0), (0, LANES - HEAD_DIM))).astype(BF16), w_cv2.T.astype(BF16))


def _nsa_phases(ovl_ref, q_ref, k_ref, vt_ref, kc_ref, vct_ref, misc_ref, z_ref, o_ref,
                m_ref, acc_ref, wq_ref, s_ref, sc_ref, *, tq, tk, seq):
    qi = pl.program_id(1)
    kpq = tq // tk
    scale = HEAD_DIM ** -0.5
    c1 = scale * LOG2E
    ng = seq // CMP_STRIDE
    ns = seq // SLC_BLOCK
    top = min(SLC_TOPK, ns)
    m_ref[...] = jnp.full(m_ref.shape, NEG_INF, F32)
    acc_ref[...] = jnp.zeros(acc_ref.shape, F32)
    rel = _rel_t(tk, tq)

    qt = q_ref[...].astype(F32).T.astype(BF16)
    zeros_q = jnp.zeros((HEAD_DIM, tq), BF16)
    for h in range(HEADS):
        qh = qt[h * HEAD_DIM:(h + 1) * HEAD_DIM]
        alibi = _alibi_rows(NSA_SLOPES[h] / scale, tq, tk=tk)
        wq_ref[h, 0:LANES, :] = jnp.concatenate([qh, zeros_q], axis=0)
        wq_ref[h, LANES:2 * LANES, :] = alibi
        wq_ref[HEADS + h, 0:LANES, :] = jnp.concatenate([zeros_q, qh], axis=0)
        wq_ref[HEADS + h, LANES:2 * LANES, :] = alibi

    t_lane = qi * tq + lax.broadcasted_iota(jnp.int32, (1, tq), 1)
    n_sub = lax.broadcasted_iota(jnp.int32, (ng, 1), 0)
    c_valid = (n_sub * CMP_STRIDE + (CMP_LEN - 1)) <= t_lane
    kc = kc_ref[0]
    vct_ext = jnp.concatenate([vct_ref[0], jnp.ones((BF16_ROWS, ng), BF16)], axis=0)
    ovl = ovl_ref[...]
    cmp_max = []
    for h in range(HEADS):
        rhs = jnp.concatenate([qt[h * HEAD_DIM:(h + 1) * HEAD_DIM],
                               _alibi_rows(NSA_SLOPES[h] * CMP_STRIDE / scale, tq, HEAD_DIM)], axis=0)
        cmp_max.append(_stage_scores(sc_ref, h, slice(0, tq), _dot(kc, rhs), c_valid))
    imp = jnp.zeros((ns, tq), F32)
    o_cmp = []
    for h in range(HEADS):
        mx = jnp.where(cmp_max[h] > NEG_INF, cmp_max[h], 0.0)
        pb = jnp.exp2(c1 * sc_ref[h] - c1 * mx).astype(BF16)
        o_ext = _dot(vct_ext, pb)
        r = 1.0 / jnp.maximum(o_ext[HEAD_DIM:HEAD_DIM + 1], 1e-30)
        o_cmp.append(o_ext[0:HEAD_DIM] * r)
        imp = imp + _dot(ovl, pb) * r

    def select_blocks():
        j_sub = lax.broadcasted_iota(jnp.int32, (ns, 1), 0)
        j_sub_f = j_sub.astype(F32)
        cur = jnp.right_shift(t_lane, SLC_SHIFT)
        forced = (j_sub == 0) | (j_sub == cur) | (j_sub == cur - 1)
        valid = (j_sub * SLC_BLOCK) <= t_lane
        score = jnp.where(forced, 1e30, jnp.where(valid, imp, -1.0))
        sel = jnp.zeros((ns, tq), F32)
        for _ in range(top):
            mx = jnp.max(score, axis=0, keepdims=True)
            idx = jnp.min(jnp.where(score == mx, j_sub_f, float(ns)), axis=0, keepdims=True)
            pick = j_sub_f == idx
            sel = jnp.where(pick, 1.0, sel)
            score = jnp.where(pick, -2.0, score)
        unsel = jnp.where(sel > 0.5, 0.0, -MASK_BIG).astype(BF16)
        for h in range(HEADS):
            wq_ref[h, LANES + BLK_LANE0:LANES + BLK_LANE0 + ns, :] = unsel

    aux_lane = lax.broadcasted_iota(jnp.int32, (tk, LANES), 1)
    aux_blk = jnp.right_shift(lax.broadcasted_iota(jnp.int32, (tk, LANES), 0), SLC_SHIFT) + BLK_LANE0
    kpos = _key_pos_features(tk)
    ones = jnp.ones((BF16_ROWS, tk), BF16)

    n_slots = s_ref.shape[0]

    def tiles(window, kis, cols, mask_offs=None):
        base = HEADS if window else 0
        loaded = []
        for t, ki in enumerate(kis):
            start = pl.multiple_of(ki * tk, tk)
            k = k_ref[pl.ds(start, tk), :]
            if window:
                vt = vt_ref[HEAD_DIM:2 * HEAD_DIM, pl.ds(start, tk)]
                aux = kpos
                dist = rel[:, cols[t]] + (qi * tq - ki * tk)
                keep = (dist >= 0) & (dist < WINDOW)
            else:
                vt = vt_ref[0:HEAD_DIM, pl.ds(start, tk)]
                onehot = aux_lane == aux_blk + ki * (tk // SLC_BLOCK)
                aux = jnp.where(onehot, jnp.ones_like(kpos), kpos)
                keep = None
            loaded.append((jnp.concatenate([k, aux], axis=1),
                           jnp.concatenate([vt, ones], axis=0), (ki * tk).astype(F32), keep))
        items = [(t, h) for t in range(len(kis)) for h in range(HEADS)]
        s_max = {}

        def scores(i):
            t, h = items[i]
            if window:
                keep = loaded[t][3]
            else:
                keep = None if mask_offs[t] is None else rel[:, cols[t]] >= mask_offs[t]
            s_max[i] = _stage_scores(s_ref, i % n_slots, cols[t],
                                     _dot(loaded[t][0], wq_ref[base + h, :, cols[t]]), keep)

        for i in range(QK_LOOKAHEAD):
            scores(i)
        for i, (t, h) in enumerate(items):
            if i + QK_LOOKAHEAD < len(items):
                scores(i + QK_LOOKAHEAD)
            _lhs, vt_ext, key0, _keep = loaded[t]
            _flash_update_t(lambda i=i, t=t: s_ref[i % n_slots, :, cols[t]], s_max.pop(i), c1,
                            (NSA_SLOPES[h] * LOG2E) * key0, vt_ext, m_ref, acc_ref, base + h, cols[t])
            yield

    all_cols = slice(0, tq)
    diag_cols = [slice(d * tk, tq) for d in range(kpq)]

    _run_interleaved(tiles(True, [qi * kpq + d for d in range(kpq)], diag_cols))
    select_blocks()

    def loop_steps(i):
        return tiles(False, [i * kpq + d for d in range(kpq)], [all_cols] * kpq, [None] * kpq)

    def back_window():
        n_back = (WINDOW + tk - 1) // tk
        for g in range((n_back + kpq - 1) // kpq):
            backs = list(range(g * kpq + 1, min((g + 1) * kpq, n_back) + 1))
            back_cols = [slice(0, min(tq, -(-(WINDOW - (back - 1) * tk - 1) // LANES) * LANES)) for back in backs]

            @pl.when(qi > g)
            def _():
                _run_interleaved(tiles(True, [qi * kpq - back for back in backs], back_cols))

    def diag_steps():
        return tiles(False, [qi * kpq + d for d in range(kpq)], diag_cols, [d * tk for d in range(kpq)])

    def finish():
        g_t = jax.nn.sigmoid(misc_ref[...]).T
        outs = []
        for h in range(HEADS):
            r0 = GATE_COL + 3 * h
            outs.append(g_t[r0:r0 + 1] * o_cmp[h] + g_t[r0 + 1:r0 + 2] * _normalized(acc_ref[h])
                        + g_t[r0 + 2:r0 + 3] * _normalized(acc_ref[HEADS + h]))
        y = jnp.concatenate(outs, axis=0).T * _silu(z_ref[...])
        o_ref[...] = y.astype(o_ref.dtype)

    return loop_steps, back_window, diag_steps, finish


def _overlap_t(seq):
    nc = (seq - CMP_LEN) // CMP_STRIDE + 1
    ng = seq // CMP_STRIDE
    ns = seq // SLC_BLOCK
    c_start = np.arange(ng) * CMP_STRIDE
    c_end = c_start + CMP_LEN - 1
    s_start = np.arange(ns) * SLC_BLOCK
    s_end = s_start + SLC_BLOCK - 1
    ov = (c_start[None, :] <= s_end[:, None]) & (c_end[None, :] >= s_start[:, None]) & (np.arange(ng)[None, :] < nc)
    return jnp.asarray(ov.astype(np.float32), dtype=BF16)


def _nsa_operands(q, k2, vt_all, kc, vct, misc, z_all, seq, nq, tq, tk):
    ng = seq // CMP_STRIDE
    ns = seq // SLC_BLOCK
    assert tq - tk < WINDOW and tk % SLC_BLOCK == 0 and ns <= LANES - BLK_LANE0
    inputs = [_overlap_t(seq), q, k2, vt_all, kc, vct, misc, z_all]
    in_specs = [pl.BlockSpec((ns, ng), lambda b, i: (0, 0)),
                pl.BlockSpec((tq, GROUP_W), lambda b, i: (b * nq + i, 0)),
                pl.BlockSpec((seq, LANES), lambda b, i: (b, 0)),
                pl.BlockSpec((2 * HEAD_DIM, seq), lambda b, i: (NSA_VT_BLK, b)),
                pl.BlockSpec((1, ng, LANES), lambda b, i: (b, 0, 0)),
                pl.BlockSpec((1, HEAD_DIM, ng), lambda b, i: (b, 0, 0)),
                pl.BlockSpec((tq, LANES), lambda b, i: (b * nq + i, 0)),
                pl.BlockSpec((tq, GROUP_W), lambda b, i: (b * nq + i, 0))]
    scratch = [pltpu.VMEM((2 * HEADS, 1, tq), F32),
               pltpu.VMEM((2 * HEADS, ACC_ROWS, tq), F32),
               pltpu.VMEM((2 * HEADS, 2 * LANES, tq), BF16),
               pltpu.VMEM((2 * HEADS, tk, tq), F32),
               pltpu.VMEM((HEADS, ng, tq), F32)]
    return inputs, in_specs, scratch


def _mixers_kernel(*refs, n_in, n_scratch, tq, tk_nsa, tk_diff, seq, lam_init):
    bounds = np.cumsum([0] + list(n_in))
    nsa_in, diff_in, ret_in, ssd_in = (refs[a:b] for a, b in zip(bounds[:-1], bounds[1:]))
    nsa_out, diff_out, ret_out, ssd_out = refs[bounds[-1]:bounds[-1] + 4]
    sb = np.cumsum([0] + list(n_scratch)) + bounds[-1] + 4
    nsa_scr, diff_scr, rec_scr = (refs[a:b] for a, b in zip(sb[:-1], sb[1:]))
    ret_st, ssd_ext, ssd_st = rec_scr
    qi = pl.program_id(1)

    @pl.when(qi == 0)
    def _():
        ret_st[...] = jnp.zeros(ret_st.shape, F32)
        ssd_st[...] = jnp.zeros(ssd_st.shape, F32)
        ssd_ext[0:8, :] = jnp.zeros((8, CONV_CH), F32)

    chunks = list(range(ret_out.shape[0]))
    first, second = chunks[:len(chunks) // 2], chunks[len(chunks) // 2:]
    nsa_loop, nsa_back_window, nsa_diag, nsa_finish = _nsa_phases(
        *nsa_in, nsa_out, *nsa_scr, tq=tq, tk=tk_nsa, seq=seq)
    diff_loop, diff_diag, diff_finish = _diff_phases(
        *diff_in, diff_out, *diff_scr, tq=tq, tk=tk_diff, lam_init=lam_init)
    _ret_body(*ret_in, ret_out, ret_st, first)
    _ssd_body(*ssd_in, ssd_out, ssd_ext, ssd_st, first)

    def body(i, carry):
        _run_interleaved(nsa_loop(i))
        _run_interleaved(diff_loop(i))
        return carry

    lax.fori_loop(0, qi, body, 0)
    nsa_back_window()
    _run_interleaved(nsa_diag(), diff_diag())
    _ret_body(*ret_in, ret_out, ret_st, second)
    _ssd_body(*ssd_in, ssd_out, ssd_ext, ssd_st, second)
    nsa_finish()
    diff_finish()


def _mixers(nsa_args, diff_args, rec_args, batch, seq, layer_idx, tq, tk_nsa, tk_diff):
    nq = seq // tq
    nsa_in, nsa_specs, nsa_scratch = _nsa_operands(*nsa_args, seq, nq, tq, tk_nsa)
    diff_in, diff_specs, diff_scratch = _diff_operands(*diff_args, seq, nq, tq, tk_diff)
    ret_in, ssd_in, rec_specs, rec_out_spec, rec_scratch = _recurrent_operands(*rec_args, nq, tq)
    out_spec = pl.BlockSpec((tq, GROUP_W), lambda b, i: (b * nq + i, 0))
    m = batch * seq
    y_nsa, y_diff, y_ret, y_ssm = pl.pallas_call(
        functools.partial(_mixers_kernel, n_in=(len(nsa_in), len(diff_in), len(ret_in), len(ssd_in)),
                          n_scratch=(len(nsa_scratch), len(diff_scratch), len(rec_scratch)),
                          tq=tq, tk_nsa=tk_nsa, tk_diff=tk_diff, seq=seq,
                          lam_init=0.8 - 0.6 * math.exp(-0.3 * layer_idx)),
        grid=(batch, nq),
        in_specs=nsa_specs + diff_specs + rec_specs,
        out_specs=[out_spec, out_spec, rec_out_spec, rec_out_spec],
        out_shape=[jax.ShapeDtypeStruct((m, GROUP_W), BF16)] * 2
                  + [jax.ShapeDtypeStruct((m // SSM_CHUNK, SSM_CHUNK, GROUP_W), BF16)] * 2,
        scratch_shapes=nsa_scratch + diff_scratch + rec_scratch,
        compiler_params=_params(("parallel", "arbitrary")),
        name="mixers",
    )(*nsa_in, *diff_in, *ret_in, *ssd_in)
    return y_nsa, y_diff, y_ret.reshape(m, GROUP_W), y_ssm.reshape(m, GROUP_W)


def _ret_tables():
    c = RET_CHUNK
    h = np.arange(HEADS, dtype=np.float32)
    log_g = jnp.log(1.0 - 2.0 ** (-5.0 - jnp.asarray(h)))
    pos = jnp.arange(c, dtype=F32)
    rel = pos[:, None] - pos[None, :]
    decay = jnp.where(rel >= 0, jnp.exp(log_g[:, None, None] * jnp.maximum(rel, 0.0)), 0.0)
    xi = jnp.exp(log_g[:, None] * (pos + 1.0))
    zeta = jnp.exp(log_g[:, None] * (c - 1.0 - pos))
    chunk_decay = jnp.exp(log_g * c)
    xi_tab = jnp.repeat(xi.T, HEAD_DIM, axis=1)
    zeta_tab = jnp.repeat(zeta.T, HEAD_DIM, axis=1)
    cd_tab = jnp.repeat(chunk_decay, HEAD_DIM)[None, :]
    return decay, xi_tab, zeta_tab, cd_tab


def _ret_body(decay_ref, xi_ref, zeta_ref, cd_ref, gn_ref, q_ref, k_ref, v_ref, z_ref, o_ref, st_ref, blocks):
    xi = xi_ref[...]
    cd = cd_ref[...]
    for bb in blocks:
        q = (q_ref[bb].astype(F32) * (HEAD_DIM ** -0.5)).astype(BF16)
        k = k_ref[bb]
        v = v_ref[bb]
        kz_t = (k.astype(F32) * zeta_ref[...]).T.astype(BF16)
        outs = []
        for h in range(HEADS):
            sl = slice(h * HEAD_DIM, (h + 1) * HEAD_DIM)
            qh, kh, vh = q[:, sl], k[:, sl], v[:, sl]
            prev = st_ref[h]
            inner = (_dot_nt(qh, kh) * decay_ref[h]).astype(BF16)
            o = _dot(inner, vh) + _dot(qh, prev.astype(BF16)) * xi[:, sl]
            st_ref[h] = prev * cd[:, sl] + _dot(kz_t[sl, :], vh)
            mu = jnp.mean(o, axis=-1, keepdims=True)
            d = o - mu
            var = jnp.mean(d * d, axis=-1, keepdims=True)
            outs.append(d * lax.rsqrt(var + EPS))
        y = jnp.concatenate(outs, axis=-1) * gn_ref[...] * _silu(z_ref[bb])
        o_ref[bb] = y.astype(o_ref.dtype)


def _ssd_body(cw_ref, cb_ref, dtb_ref, a_ref, dsk_ref, nw_ref, xbc_ref, misc_ref, z_ref, o_ref,
              ext_ref, st_ref, blocks):
    L = SSM_CHUNK
    hi = lax.Precision.HIGHEST
    row = lax.broadcasted_iota(jnp.int32, (L, L), 0)
    col = lax.broadcasted_iota(jnp.int32, (L, L), 1)
    causal = row >= col
    tril = jnp.where(causal, 1.0, 0.0).astype(F32)
    dsk = dsk_ref[...]

    for bb in blocks:
        raw = xbc_ref[bb]
        ext_ref[8:8 + L, :] = raw
        conv = cb_ref[...] + raw * cw_ref[CONV_W - 1:CONV_W, :]
        for w in range(CONV_W - 1):
            shift = CONV_W - 1 - w
            conv = conv + ext_ref[8 - shift:8 - shift + L, :] * cw_ref[w:w + 1, :]
        ext_ref[0:8, :] = raw[L - 8:L, :]
        xc = _silu(conv)
        x = xc[:, 0:GROUP_W]
        bm = xc[:, GROUP_W:GROUP_W + 2 * SSM_STATE].astype(BF16)
        cm = xc[:, GROUP_W + 2 * SSM_STATE:].astype(BF16)

        dt_full = jax.nn.softplus(misc_ref[bb] + dtb_ref[...])
        da = dt_full * a_ref[...]
        cs_col = jnp.dot(tril, da, precision=hi, preferred_element_type=F32)
        cs_row = lax.dot_general(da, tril, (((0,), (1,)), ((), ())), precision=hi,
                                 preferred_element_type=F32)

        outs = []
        for h in range(HEADS):
            g = h // 2
            c0 = DT_COL + h
            sl = slice(h * HEAD_DIM, (h + 1) * HEAD_DIM)
            gs = slice(g * SSM_STATE, (g + 1) * SSM_STATE)
            cs_c = cs_col[:, c0:c0 + 1]
            cs_r = cs_row[c0:c0 + 1, :]
            cs_last = cs_col[L - 1:L, c0:c0 + 1]
            xh = x[:, sl]
            xdt = xh * dt_full[:, c0:c0 + 1]
            seg = jnp.exp(jnp.where(causal, cs_c - cs_r, NEG_INF))
            cb = _dot_nt(cm[:, gs], bm[:, gs])
            y = _dot((cb * seg).astype(BF16), xdt.astype(BF16))
            prev = st_ref[h]
            y = y + _dot(cm[:, gs], prev.astype(BF16)) * jnp.exp(cs_c)
            y = y + dsk[:, sl] * xh
            dec = jnp.exp(cs_last - cs_c)
            st_ref[h] = prev * jnp.exp(cs_last) + _dot_tn(bm[:, gs], (xdt * dec).astype(BF16))
            outs.append(y)
        y = jnp.concatenate(outs, axis=-1) * _silu(z_ref[bb])
        ms = jnp.mean(y * y, axis=-1, keepdims=True)
        o_ref[bb] = (y * lax.rsqrt(ms + EPS) * nw_ref[...]).astype(o_ref.dtype)


def _recurrent_operands(ret_qkv, xbc, misc, z_all, gn_w, conv_w, conv_b, dt_bias, a_log, d_skip, norm_w, nq, tq):
    L = SSM_CHUNK
    assert RET_CHUNK == L and tq % L == 0
    cpt = tq // L
    decay, xi_tab, zeta_tab, cd_tab = _ret_tables()
    dtb = jnp.zeros((1, 128), F32).at[0, DT_COL:DT_COL + HEADS].set(dt_bias)
    a_full = jnp.zeros((1, 128), F32).at[0, DT_COL:DT_COL + HEADS].set(-jnp.exp(a_log))
    dsk = jnp.repeat(d_skip, HEAD_DIM)[None, :]

    def full(shape):
        return pl.BlockSpec(shape, lambda b, i: (0,) * len(shape))

    def blk(width, col):
        return pl.BlockSpec((cpt, L, width), lambda b, i: (b * nq + i, 0, col))

    def chunked(a):
        return a.reshape(-1, L, a.shape[-1])

    qkv3, z3 = chunked(ret_qkv), chunked(z_all)
    ret_in = [decay, xi_tab, zeta_tab, cd_tab, gn_w.reshape(1, GROUP_W), qkv3, qkv3, qkv3, z3]
    ret_specs = [full((HEADS, L, L)), full((L, GROUP_W)), full((L, GROUP_W)), full((1, GROUP_W)),
                 full((1, GROUP_W)), blk(GROUP_W, 0), blk(GROUP_W, 1), blk(GROUP_W, 2), blk(GROUP_W, 2)]
    ssd_in = [conv_w, conv_b.reshape(1, CONV_CH), dtb, a_full, dsk, norm_w.reshape(1, GROUP_W),
              chunked(xbc), chunked(misc), z3]
    ssd_specs = [full((CONV_W, CONV_CH)), full((1, CONV_CH)), full((1, 128)), full((1, 128)),
                 full((1, GROUP_W)), full((1, GROUP_W)), blk(CONV_CH, 0), blk(128, 0), blk(GROUP_W, 3)]
    scratch = [pltpu.VMEM((HEADS, HEAD_DIM, HEAD_DIM), F32),
               pltpu.VMEM((8 + L, CONV_CH), F32),
               pltpu.VMEM((HEADS, SSM_STATE, HEAD_DIM), F32)]
    return ret_in, ssd_in, ret_specs + ssd_specs, blk(GROUP_W, 0), scratch


def _pick_tile(n, pref):
    t = pref
    while n % t:
        t //= 2
    return t


def kernel(x, norm_w, w_in, w_out, nsa_pe_k, nsa_pe_v, nsa_w_ck1, nsa_w_ck2, nsa_w_cv1, nsa_w_cv2,
           diff_lam_q1, diff_lam_k1, diff_lam_q2, diff_lam_k2, diff_subln_w, ret_gn_w,
           ssm_conv_w, ssm_conv_b, ssm_dt_bias, ssm_A_log, ssm_D, ssm_norm_w, final_norm_w):
    batch, seq, _ = x.shape
    depth = w_in.shape[0]
    m = batch * seq
    tm = _pick_tile(m, 512)
    tq = _pick_tile(seq, 512)
    tk = _pick_tile(seq, 256)
    tk_diff = _pick_tile(seq, 512)
    w_in_b = w_in.astype(BF16)
    w_r = _relayout_w_in(w_in_b)
    w_t = _relayout_w_in_t(w_in_b)
    w_out_b = w_out.astype(BF16)
    x2d = x.reshape(m, D_MODEL)
    projected = _in_proj(x2d, norm_w[0], w_r[0], w_t[0], tm)
    for i in range(depth):
        nsa_q, nsa_k2, nsa_cmp, misc, z_all, diff_qk, ret_qkv, xbc, vt_all = projected
        kc, vct = _nsa_compress(nsa_cmp, nsa_pe_k[i], nsa_pe_v[i], nsa_w_ck1[i], nsa_w_ck2[i],
                                nsa_w_cv1[i], nsa_w_cv2[i], batch, seq)
        lam_p = jnp.stack([diff_lam_q1[i], diff_lam_k1[i], diff_lam_q2[i], diff_lam_k2[i]])
        ys = _mixers((nsa_q, nsa_k2, vt_all, kc, vct, misc, z_all),
                     (diff_qk, vt_all, z_all, lam_p, diff_subln_w[i]),
                     (ret_qkv, xbc, misc, z_all, ret_gn_w[i], ssm_conv_w[i], ssm_conv_b[i], ssm_dt_bias[i],
                      ssm_A_log[i], ssm_D[i], ssm_norm_w[i]),
                     batch, seq, i, tq, tk, tk_diff)
        if i + 1 < depth:
            x2d, *projected = _out_in_proj(ys, w_out_b[i], x2d, norm_w[i + 1], w_r[i + 1], w_t[i + 1], tm)
        else:
            x2d = _out_proj(ys, w_out_b[i], x2d, final_norm_w, tm)
    return x2d.reshape(batch, seq, D_MODEL)
```

```python
import functools
import math

import numpy as np
import jax
import jax.numpy as jnp
from jax import lax
from jax.experimental import pallas as pl
from jax.experimental.pallas import tpu as pltpu

F32 = jnp.float32
BF16 = jnp.bfloat16
NEG_INF = float("-inf")
LOG2E = 1.4426950408889634

D_MODEL = 1024
DEPTH = 4
GROUP_W = 256
HEADS = 4
HEAD_DIM = 64
EPS = 1e-6
CMP_LEN = 32
CMP_STRIDE = 16
CMP_HIDDEN = 256
SLC_BLOCK = 64
SLC_SHIFT = 6
SLC_TOPK = 16
WINDOW = 512
DIFF_QK_DIM = 32
RET_CHUNK = 128
SSM_STATE = 128
SSM_CHUNK = 128
CONV_W = 4
CONV_CH = 768
N_ALIBI_HEADS = 8
LANES = 128
BF16_ROWS = 16
ACC_ROWS = HEAD_DIM + BF16_ROWS
QK_LOOKAHEAD = 3
ALIBI_ROWS = 3
POS_RADIX = 256
MASK_BIG = 2.0 ** 100
BLK_LANE0 = 64

IN_LAYOUT = (
    ("nsa_q", 256), ("nsa_k_cmp", 64), ("nsa_v_cmp", 64), ("nsa_k_slc", 64), ("nsa_v_slc", 64),
    ("nsa_k_win", 64), ("nsa_v_win", 64), ("nsa_gate", 12), ("nsa_z", 256),
    ("diff_q", 256), ("diff_k", 256), ("diff_v", 256), ("diff_z", 256),
    ("ret_q", 256), ("ret_k", 256), ("ret_v", 256), ("ret_z", 256),
    ("ssm_z", 256), ("ssm_xbc", 768), ("ssm_dt", 4),
)
IN_OFF = {}
_o = 0
for _n, _w in IN_LAYOUT:
    IN_OFF[_n] = (_o, _w)
    _o += _w
IN_W = _o

GATE_COL = 0
DT_COL = 12
IN_OUTPUTS = (
    ("nsa_q", BF16, ("nsa_q",), 256),
    ("nsa_k2", BF16, ("nsa_k_slc", "nsa_k_win"), 128),
    ("nsa_cmp", F32, ("nsa_k_cmp", "nsa_v_cmp"), 128),
    ("misc", F32, ("nsa_gate", "ssm_dt"), 128),
    ("z_all", F32, ("nsa_z", "diff_z", "ret_z", "ssm_z"), 1024),
    ("diff_qk", BF16, ("diff_q", "diff_k"), 512),
    ("ret_qkv", BF16, ("ret_q", "ret_k", "ret_v"), 768),
    ("xbc", F32, ("ssm_xbc",), 768),
)
IN_T_SRC = ("diff_v", "nsa_v_slc", "nsa_v_win")
IN_T_ROWS = 384
DIFF_VT_BLK = 0
NSA_VT_BLK = 2
IN_SEGS = []
_o = 0
for _n, _dt, _src, _w in IN_OUTPUTS:
    IN_SEGS.append((_o, _o + _w))
    _o += _w
IN_WP = _o

VMEM_LIMIT = 56 * 1024 * 1024


def _alibi_slopes():
    return [2.0 ** (-8.0 * (i + 1) / N_ALIBI_HEADS) for i in range(N_ALIBI_HEADS)]


NSA_SLOPES = _alibi_slopes()[0::2]
DIFF_SLOPES = _alibi_slopes()[1::2]


def _silu(x):
    return x * jax.nn.sigmoid(x)


def _dot(a, b):
    return jnp.dot(a, b, preferred_element_type=F32)


def _dot_nt(a, b):
    return lax.dot_general(a, b, (((1,), (1,)), ((), ())), preferred_element_type=F32)


def _dot_tn(a, b):
    return lax.dot_general(a, b, (((0,), (0,)), ((), ())), preferred_element_type=F32)


def _params(sem):
    return pltpu.CompilerParams(dimension_semantics=sem, vmem_limit_bytes=VMEM_LIMIT)


def _relayout_w_in(w_in):
    cols = []
    for _n, _dt, src, width in IN_OUTPUTS:
        used = 0
        for s in src:
            off, w = IN_OFF[s]
            cols.append(w_in[:, :, off:off + w])
            used += w
        if used < width:
            cols.append(jnp.zeros(w_in.shape[:2] + (width - used,), w_in.dtype))
    return jnp.concatenate(cols, axis=-1)


def _relayout_w_in_t(w_in):
    cols = [w_in[:, :, IN_OFF[s][0]:IN_OFF[s][0] + IN_OFF[s][1]] for s in IN_T_SRC]
    return jnp.swapaxes(jnp.concatenate(cols, axis=-1), 1, 2)


def _norm_project(x, nw_ref, w_ref, wt_ref, out_refs):
    ms = jnp.mean(x * x, axis=-1, keepdims=True)
    h = (x * lax.rsqrt(ms + EPS) * nw_ref[...]).astype(BF16)
    for ref, (a, b) in zip(out_refs[:-1], IN_SEGS):
        ref[...] = _dot(h, w_ref[:, a:b]).astype(ref.dtype)
    out_refs[-1][...] = _dot_nt(wt_ref[...], h).astype(BF16)


def _in_proj_kernel(x_ref, nw_ref, w_ref, wt_ref, *out_refs):
    _norm_project(x_ref[...], nw_ref, w_ref, wt_ref, out_refs)


def _in_proj_outputs(m, tm):
    out_shape = [jax.ShapeDtypeStruct((m, b - a), dt) for (_n, dt, _s, _w), (a, b) in zip(IN_OUTPUTS, IN_SEGS)]
    out_specs = [pl.BlockSpec((tm, b - a), lambda i: (i, 0)) for (a, b) in IN_SEGS]
    out_shape.append(jax.ShapeDtypeStruct((IN_T_ROWS, m), BF16))
    out_specs.append(pl.BlockSpec((IN_T_ROWS, tm), lambda i: (0, i)))
    return out_shape, out_specs


def _in_proj(x2d, norm_w, w_r, w_t, tm):
    m = x2d.shape[0]
    out_shape, out_specs = _in_proj_outputs(m, tm)
    return pl.pallas_call(
        _in_proj_kernel,
        grid=(m // tm,),
        in_specs=[pl.BlockSpec((tm, D_MODEL), lambda i: (i, 0)),
                  pl.BlockSpec((1, D_MODEL), lambda i: (0, 0)),
                  pl.BlockSpec((D_MODEL, IN_WP), lambda i: (0, 0)),
                  pl.BlockSpec((IN_T_ROWS, D_MODEL), lambda i: (0, 0))],
        out_specs=out_specs,
        out_shape=out_shape,
        compiler_params=_params(("parallel",)),
        name="in_proj",
    )(x2d, norm_w.reshape(1, D_MODEL), w_r, w_t)


def _out_proj_kernel(y0_ref, y1_ref, y2_ref, y3_ref, w_ref, x_ref, fw_ref, o_ref):
    acc = x_ref[...]
    for g, y_ref in enumerate((y0_ref, y1_ref, y2_ref, y3_ref)):
        acc = acc + _dot(y_ref[...], w_ref[g * GROUP_W:(g + 1) * GROUP_W, :])
    ms = jnp.mean(acc * acc, axis=-1, keepdims=True)
    o_ref[...] = acc * lax.rsqrt(ms + EPS) * fw_ref[...]


def _out_proj(ys, w_out_b, x2d, final_w, tm):
    m = x2d.shape[0]
    yspec = pl.BlockSpec((tm, GROUP_W), lambda i: (i, 0))
    return pl.pallas_call(
        _out_proj_kernel,
        grid=(m // tm,),
        in_specs=[yspec, yspec, yspec, yspec,
                  pl.BlockSpec((D_MODEL, D_MODEL), lambda i: (0, 0)),
                  pl.BlockSpec((tm, D_MODEL), lambda i: (i, 0)),
                  pl.BlockSpec((1, D_MODEL), lambda i: (0, 0))],
        out_specs=pl.BlockSpec((tm, D_MODEL), lambda i: (i, 0)),
        out_shape=jax.ShapeDtypeStruct((m, D_MODEL), F32),
        compiler_params=_params(("parallel",)),
        name="out_proj",
    )(*ys, w_out_b, x2d, final_w.reshape(1, D_MODEL))


def _out_in_proj_kernel(y0_ref, y1_ref, y2_ref, y3_ref, wo_ref, x_ref, nw_ref, w_ref, wt_ref, xo_ref, *out_refs):
    acc = x_ref[...]
    for g, y_ref in enumerate((y0_ref, y1_ref, y2_ref, y3_ref)):
        acc = acc + _dot(y_ref[...], wo_ref[g * GROUP_W:(g + 1) * GROUP_W, :])
    xo_ref[...] = acc
    _norm_project(acc, nw_ref, w_ref, wt_ref, out_refs)


def _out_in_proj(ys, w_out_b, x2d, norm_w, w_r, w_t, tm):
    m = x2d.shape[0]
    yspec = pl.BlockSpec((tm, GROUP_W), lambda i: (i, 0))
    xspec = pl.BlockSpec((tm, D_MODEL), lambda i: (i, 0))
    out_shape, out_specs = _in_proj_outputs(m, tm)
    return pl.pallas_call(
        _out_in_proj_kernel,
        grid=(m // tm,),
        in_specs=[yspec, yspec, yspec, yspec,
                  pl.BlockSpec((D_MODEL, D_MODEL), lambda i: (0, 0)),
                  xspec,
                  pl.BlockSpec((1, D_MODEL), lambda i: (0, 0)),
                  pl.BlockSpec((D_MODEL, IN_WP), lambda i: (0, 0)),
                  pl.BlockSpec((IN_T_ROWS, D_MODEL), lambda i: (0, 0))],
        out_specs=[xspec] + out_specs,
        out_shape=[jax.ShapeDtypeStruct((m, D_MODEL), F32)] + out_shape,
        compiler_params=_params(("parallel",)),
        name="out_in_proj",
    )(*ys, w_out_b, x2d, norm_w.reshape(1, D_MODEL), w_r, w_t)


def _run_interleaved(*pipelines):
    live = list(pipelines)
    while live:
        for g in list(live):
            if next(g, _DONE) is _DONE:
                live.remove(g)


_DONE = object()


def _stage_scores(s_ref, slot, cols, s, keep=None):
    if keep is not None:
        s = jnp.where(keep, s, NEG_INF)
    s_ref[slot, :, cols] = s
    return jnp.max(s, axis=0, keepdims=True)


def _flash_update_t(s, s_max, c1, shift, vt_ext, m_ref, acc_ref, idx, cols):
    m_old = m_ref[idx, :, cols]
    m_new = jnp.maximum(m_old, c1 * s_max + shift)
    alpha = jnp.exp2(m_old - m_new)
    p = jnp.exp2(c1 * s() - (m_new - shift))
    acc_ref[idx, :, cols] = alpha * acc_ref[idx, :, cols] + _dot(vt_ext, p.astype(BF16))
    m_ref[idx, :, cols] = m_new


def _bf16_pieces(x, n=3):
    out = []
    for _ in range(n):
        p = float(np.asarray(x, np.float32).astype(BF16).astype(np.float32))
        out.append(p)
        x = x - p
    return out


def _key_pos_features(tk):
    lane = lax.broadcasted_iota(jnp.int32, (tk, LANES), 1)
    row = lax.broadcasted_iota(jnp.int32, (tk, LANES), 0)
    out = jnp.zeros((tk, LANES), jnp.int32)
    for d in range(-(-tk // POS_RADIX)):
        digit = jnp.bitwise_and(jnp.right_shift(row, 8 * d), POS_RADIX - 1)
        out = jnp.where((lane >= ALIBI_ROWS * d) & (lane < ALIBI_ROWS * (d + 1)), digit, out)
    return out.astype(F32).astype(BF16)


def _alibi_rows(beta, tq, rows=LANES, tk=POS_RADIX):
    row = lax.broadcasted_iota(jnp.int32, (rows, tq), 0)
    out = jnp.zeros((rows, tq), F32)
    for d in range(-(-tk // POS_RADIX)):
        for r, piece in enumerate(_bf16_pieces(beta * POS_RADIX ** d, ALIBI_ROWS)):
            out = jnp.where(row == ALIBI_ROWS * d + r, piece, out)
    return out.astype(BF16)


def _normalized(acc):
    return acc[0:HEAD_DIM] / jnp.maximum(acc[HEAD_DIM:HEAD_DIM + 1], 1e-30)


def _rel_t(tk, tq):
    return lax.broadcasted_iota(jnp.int32, (tk, tq), 1) - lax.broadcasted_iota(jnp.int32, (tk, tq), 0)


def _diff_phases(lam_ref, sw_ref, q_ref, k_ref, vt_ref, z_ref, o_ref, m_ref, acc_ref, wq_ref, s_ref,
                 *, tq, tk, lam_init):
    qi = pl.program_id(1)
    kpq = tq // tk
    scale = DIFF_QK_DIM ** -0.5
    c1 = scale * LOG2E
    m_ref[...] = jnp.full(m_ref.shape, NEG_INF, F32)
    acc_ref[...] = jnp.zeros(acc_ref.shape, F32)
    rel = _rel_t(tk, tq)

    qt = q_ref[...].astype(F32).T.astype(BF16)
    row = lax.broadcasted_iota(jnp.int32, (LANES, tq), 0)
    for j in range(2 * HEADS):
        g, r0 = divmod(j * DIFF_QK_DIM, LANES)
        qg = qt[g * LANES:(g + 1) * LANES]
        wq_ref[j, 0:LANES, :] = jnp.where((row >= r0) & (row < r0 + DIFF_QK_DIM), qg, jnp.zeros_like(qg))
        wq_ref[j, LANES:2 * LANES, :] = _alibi_rows(DIFF_SLOPES[j // 2] / scale, tq, tk=tk)

    ones = jnp.ones((BF16_ROWS, tk), BF16)
    kpos = _key_pos_features(tk)

    n_maps = 2 * HEADS
    n_slots = s_ref.shape[0]

    def tiles(kis, mask_offs):
        loaded = []
        for ki in kis:
            start = pl.multiple_of(ki * tk, tk)
            loaded.append((k_ref[pl.ds(start, tk), :],
                           vt_ref[:, pl.ds(start, tk)],
                           (ki * tk).astype(F32)))
        items = [(t, j) for t in range(len(kis)) for j in range(n_maps)]
        cols = slice(0, tq)

        def scores(i):
            t, j = items[i]
            g = (j * DIFF_QK_DIM) // LANES
            lhs = jnp.concatenate([loaded[t][0][:, g * LANES:(g + 1) * LANES], kpos], axis=1)
            s_ref[i % n_slots] = _dot(lhs, wq_ref[j])

        def staged(i):
            t = items[i][0]
            if mask_offs[t] is None:
                return lambda: s_ref[i % n_slots]
            return lambda: jnp.where(rel >= mask_offs[t], s_ref[i % n_slots], NEG_INF)

        for i in range(QK_LOOKAHEAD):
            scores(i)
        for i, (t, j) in enumerate(items):
            h = j // 2
            if i + QK_LOOKAHEAD < len(items):
                scores(i + QK_LOOKAHEAD)
            _k, vt, key0 = loaded[t]
            vt_ext = jnp.concatenate([vt[h * HEAD_DIM:(h + 1) * HEAD_DIM], ones], axis=0)
            s = staged(i)
            _flash_update_t(s, jnp.max(s(), axis=0, keepdims=True), c1, (DIFF_SLOPES[h] * LOG2E) * key0,
                            vt_ext, m_ref, acc_ref, j, cols)
            yield

    def loop_steps(i):
        return tiles([i * kpq + d for d in range(kpq)], [None] * kpq)

    def diag_steps():
        return tiles([qi * kpq + d for d in range(kpq)], [d * tk for d in range(kpq)])

    def finish():
        lp = lam_ref[...]
        lam = (jnp.exp(jnp.sum(lp[0:1] * lp[1:2], axis=-1, keepdims=True))
               - jnp.exp(jnp.sum(lp[2:3] * lp[3:4], axis=-1, keepdims=True)) + lam_init)
        sw = sw_ref[...]
        o_t = jnp.concatenate([_normalized(acc_ref[2 * h]) - lam * _normalized(acc_ref[2 * h + 1])
                               for h in range(HEADS)], axis=0)
        o = o_t.T
        outs = []
        for h in range(HEADS):
            oh = o[:, h * HEAD_DIM:(h + 1) * HEAD_DIM]
            ms = jnp.mean(oh * oh, axis=-1, keepdims=True)
            outs.append(oh * lax.rsqrt(ms + EPS) * sw * (1.0 - lam_init))
        y = jnp.concatenate(outs, axis=-1) * _silu(z_ref[...])
        o_ref[...] = y.astype(o_ref.dtype)

    return loop_steps, diag_steps, finish


def _diff_operands(qk, vt_all, z_all, lam_p, subln_w, seq, nq, tq, tk):
    inputs = [lam_p, subln_w.reshape(1, HEAD_DIM), qk, qk, vt_all, z_all]
    in_specs = [pl.BlockSpec((4, DIFF_QK_DIM), lambda b, i: (0, 0)),
                pl.BlockSpec((1, HEAD_DIM), lambda b, i: (0, 0)),
                pl.BlockSpec((tq, GROUP_W), lambda b, i: (b * nq + i, 0)),
                pl.BlockSpec((seq, GROUP_W), lambda b, i: (b, 1)),
                pl.BlockSpec((GROUP_W, seq), lambda b, i: (DIFF_VT_BLK, b)),
                pl.BlockSpec((tq, GROUP_W), lambda b, i: (b * nq + i, 1))]
    scratch = [pltpu.VMEM((2 * HEADS, 1, tq), F32),
               pltpu.VMEM((2 * HEADS, ACC_ROWS, tq), F32),
               pltpu.VMEM((2 * HEADS, 2 * LANES, tq), BF16),
               pltpu.VMEM((QK_LOOKAHEAD + 1, tk, tq), F32)]
    return inputs, in_specs, scratch


def _nsa_compress_kernel(cmp_ref, pe_ref, w1_ref, wk2_ref, wv2t_ref, kc_ref, vct_ref):
    ng = kc_ref.shape[1]
    top = jnp.zeros((ng, 2 * CMP_HIDDEN), F32)
    bot = jnp.zeros((ng, 2 * CMP_HIDDEN), F32)
    for p in range(CMP_STRIDE):
        x = cmp_ref[pl.ds(p, ng, stride=CMP_STRIDE), :]
        top = top + _dot((x + pe_ref[p:p + 1, :]).astype(BF16), w1_ref[p])
        bot = bot + _dot((x + pe_ref[CMP_STRIDE + p:CMP_STRIDE + p + 1, :]).astype(BF16), w1_ref[CMP_STRIDE + p])
    hid = _silu(top + pltpu.roll(bot, ng - 1, 0)).astype(BF16)
    hk, hv = hid[:, 0:CMP_HIDDEN], hid[:, CMP_HIDDEN:]
    kc = _dot(hk, wk2_ref[...])
    lane = lax.broadcasted_iota(jnp.int32, kc.shape, 1)
    blk = lax.broadcasted_iota(jnp.int32, kc.shape, 0).astype(F32)
    kc = jnp.where((lane >= HEAD_DIM) & (lane < HEAD_DIM + ALIBI_ROWS), blk, kc)
    kc_ref[0] = kc.astype(kc_ref.dtype)
    vct_ref[0] = _dot_nt(wv2t_ref[...], hv).astype(vct_ref.dtype)


def _nsa_compress(cmp2d, pe_k, pe_v, w_ck1, w_ck2, w_cv1, w_cv2, batch, seq):
    ng = seq // CMP_STRIDE
    wk = w_ck1.reshape(CMP_LEN, HEAD_DIM, CMP_HIDDEN)
    wv = w_cv1.reshape(CMP_LEN, HEAD_DIM, CMP_HIDDEN)
    w1 = jnp.concatenate([jnp.pad(wk, ((0, 0), (0, 0), (0, CMP_HIDDEN))),
                          jnp.pad(wv, ((0, 0), (0, 0), (CMP_HIDDEN, 0)))], axis=1).astype(BF16)
    pe = jnp.concatenate([pe_k, pe_v], axis=1)

    def full(shape):
        return pl.BlockSpec(shape, lambda b: (0,) * len(shape))

    return pl.pallas_call(
        _nsa_compress_kernel,
        grid=(batch,),
        in_specs=[pl.BlockSpec((seq, LANES), lambda b: (b, 0)), full((CMP_LEN, LANES)),
                  full((CMP_LEN, LANES, 2 * CMP_HIDDEN)), full((CMP_HIDDEN, LANES)),
                  full((HEAD_DIM, CMP_HIDDEN))],
        out_specs=[pl.BlockSpec((1, ng, LANES), lambda b: (b, 0, 0)),
                   pl.BlockSpec((1, HEAD_DIM, ng), lambda b: (b, 0, 0))],
        out_shape=[jax.ShapeDtypeStruct((batch, ng, LANES), BF16),
                   jax.ShapeDtypeStruct((batch, HEAD_DIM, ng), BF16)],
        compiler_params=_params(("parallel",)),
        name="nsa_compress",
    )(cmp2d, pe, w1, jnp.pad(w_ck2, ((0, 0), (0, LANES - HEAD_DIM))).astype(BF16), w_cv2.T.astype(BF16))


def _nsa_phases(ovl_ref, q_ref, k_ref, vt_ref, kc_ref, vct_ref, misc_ref, z_ref, o_ref,
                m_ref, acc_ref, wq_ref, s_ref, sc_ref, *, tq, tk, seq):
    qi = pl.program_id(1)
    kpq = tq // tk
    scale = HEAD_DIM ** -0.5
    c1 = scale * LOG2E
    ng = seq // CMP_STRIDE
    ns = seq // SLC_BLOCK
    top = min(SLC_TOPK, ns)
    m_ref[...] = jnp.full(m_ref.shape, NEG_INF, F32)
    acc_ref[...] = jnp.zeros(acc_ref.shape, F32)
    rel = _rel_t(tk, tq)

    qt = q_ref[...].astype(F32).T.astype(BF16)
    zeros_q = jnp.zeros((HEAD_DIM, tq), BF16)
    for h in range(HEADS):
        qh = qt[h * HEAD_DIM:(h + 1) * HEAD_DIM]
        alibi = _alibi_rows(NSA_SLOPES[h] / scale, tq, tk=tk)
        wq_ref[h, 0:LANES, :] = jnp.concatenate([qh, zeros_q], axis=0)
        wq_ref[h, LANES:2 * LANES, :] = alibi
        wq_ref[HEADS + h, 0:LANES, :] = jnp.concatenate([zeros_q, qh], axis=0)
        wq_ref[HEADS + h, LANES:2 * LANES, :] = alibi

    t_lane = qi * tq + lax.broadcasted_iota(jnp.int32, (1, tq), 1)
    n_sub = lax.broadcasted_iota(jnp.int32, (ng, 1), 0)
    c_valid = (n_sub * CMP_STRIDE + (CMP_LEN - 1)) <= t_lane
    kc = kc_ref[0]
    vct_ext = jnp.concatenate([vct_ref[0], jnp.ones((BF16_ROWS, ng), BF16)], axis=0)
    ovl = ovl_ref[...]
    cmp_max = []
    for h in range(HEADS):
        rhs = jnp.concatenate([qt[h * HEAD_DIM:(h + 1) * HEAD_DIM],
                               _alibi_rows(NSA_SLOPES[h] * CMP_STRIDE / scale, tq, HEAD_DIM)], axis=0)
        cmp_max.append(_stage_scores(sc_ref, h, slice(0, tq), _dot(kc, rhs), c_valid))
    imp = jnp.zeros((ns, tq), F32)
    o_cmp = []
    for h in range(HEADS):
        mx = jnp.where(cmp_max[h] > NEG_INF, cmp_max[h], 0.0)
        pb = jnp.exp2(c1 * sc_ref[h] - c1 * mx).astype(BF16)
        o_ext = _dot(vct_ext, pb)
        r = 1.0 / jnp.maximum(o_ext[HEAD_DIM:HEAD_DIM + 1], 1e-30)
        o_cmp.append(o_ext[0:HEAD_DIM] * r)
        imp = imp + _dot(ovl, pb) * r

    def select_blocks():
        j_sub = lax.broadcasted_iota(jnp.int32, (ns, 1), 0)
        j_sub_f = j_sub.astype(F32)
        cur = jnp.right_shift(t_lane, SLC_SHIFT)
        forced = (j_sub == 0) | (j_sub == cur) | (j_sub == cur - 1)
        valid = (j_sub * SLC_BLOCK) <= t_lane
        score = jnp.where(forced, 1e30, jnp.where(valid, imp, -1.0))
        sel = jnp.zeros((ns, tq), F32)
        for _ in range(top):
            mx = jnp.max(score, axis=0, keepdims=True)
            idx = jnp.min(jnp.where(score == mx, j_sub_f, float(ns)), axis=0, keepdims=True)
            pick = j_sub_f == idx
            sel = jnp.where(pick, 1.0, sel)
            score = jnp.where(pick, -2.0, score)
        unsel = jnp.where(sel > 0.5, 0.0, -MASK_BIG).astype(BF16)
        for h in range(HEADS):
            wq_ref[h, LANES + BLK_LANE0:LANES + BLK_LANE0 + ns, :] = unsel

    aux_lane = lax.broadcasted_iota(jnp.int32, (tk, LANES), 1)
    aux_blk = jnp.right_shift(lax.broadcasted_iota(jnp.int32, (tk, LANES), 0), SLC_SHIFT) + BLK_LANE0
    kpos = _key_pos_features(tk)
    ones = jnp.ones((BF16_ROWS, tk), BF16)

    n_slots = s_ref.shape[0]

    def tiles(window, kis, cols, mask_offs=None):
        base = HEADS if window else 0
        loaded = []
        for t, ki in enumerate(kis):
            start = pl.multiple_of(ki * tk, tk)
            k = k_ref[pl.ds(start, tk), :]
            if window:
                vt = vt_ref[HEAD_DIM:2 * HEAD_DIM, pl.ds(start, tk)]
                aux = kpos
                dist = rel[:, cols[t]] + (qi * tq - ki * tk)
                keep = (dist >= 0) & (dist < WINDOW)
            else:
                vt = vt_ref[0:HEAD_DIM, pl.ds(start, tk)]
                onehot = aux_lane == aux_blk + ki * (tk // SLC_BLOCK)
                aux = jnp.where(onehot, jnp.ones_like(kpos), kpos)
                keep = None
            loaded.append((jnp.concatenate([k, aux], axis=1),
                           jnp.concatenate([vt, ones], axis=0), (ki * tk).astype(F32), keep))
        items = [(t, h) for t in range(len(kis)) for h in range(HEADS)]
        s_max = {}

        def scores(i):
            t, h = items[i]
            if window:
                keep = loaded[t][3]
            else:
                keep = None if mask_offs[t] is None else rel[:, cols[t]] >= mask_offs[t]
            s_max[i] = _stage_scores(s_ref, i % n_slots, cols[t],
                                     _dot(loaded[t][0], wq_ref[base + h, :, cols[t]]), keep)

        for i in range(QK_LOOKAHEAD):
            scores(i)
        for i, (t, h) in enumerate(items):
            if i + QK_LOOKAHEAD < len(items):
                scores(i + QK_LOOKAHEAD)
            _lhs, vt_ext, key0, _keep = loaded[t]
            _flash_update_t(lambda i=i, t=t: s_ref[i % n_slots, :, cols[t]], s_max.pop(i), c1,
                            (NSA_SLOPES[h] * LOG2E) * key0, vt_ext, m_ref, acc_ref, base + h, cols[t])
            yield

    all_cols = slice(0, tq)
    diag_cols = [slice(d * tk, tq) for d in range(kpq)]

    _run_interleaved(tiles(True, [qi * kpq + d for d in range(kpq)], diag_cols))
    select_blocks()

    def loop_steps(i):
        return tiles(False, [i * kpq + d for d in range(kpq)], [all_cols] * kpq, [None] * kpq)

    def back_window():
        n_back = (WINDOW + tk - 1) // tk
        for g in range((n_back + kpq - 1) // kpq):
            backs = list(range(g * kpq + 1, min((g + 1) * kpq, n_back) + 1))
            back_cols = [slice(0, min(tq, -(-(WINDOW - (back - 1) * tk - 1) // LANES) * LANES)) for back in backs]

            @pl.when(qi > g)
            def _():
                _run_interleaved(tiles(True, [qi * kpq - back for back in backs], back_cols))

    def diag_steps():
        return tiles(False, [qi * kpq + d for d in range(kpq)], diag_cols, [d * tk for d in range(kpq)])

    def finish():
        g_t = jax.nn.sigmoid(misc_ref[...]).T
        outs = []
        for h in range(HEADS):
            r0 = GATE_COL + 3 * h
            outs.append(g_t[r0:r0 + 1] * o_cmp[h] + g_t[r0 + 1:r0 + 2] * _normalized(acc_ref[h])
                        + g_t[r0 + 2:r0 + 3] * _normalized(acc_ref[HEADS + h]))
        y = jnp.concatenate(outs, axis=0).T * _silu(z_ref[...])
        o_ref[...] = y.astype(o_ref.dtype)

    return loop_steps, back_window, diag_steps, finish


def _overlap_t(seq):
    nc = (seq - CMP_LEN) // CMP_STRIDE + 1
    ng = seq // CMP_STRIDE
    ns = seq // SLC_BLOCK
    c_start = np.arange(ng) * CMP_STRIDE
    c_end = c_start + CMP_LEN - 1
    s_start = np.arange(ns) * SLC_BLOCK
    s_end = s_start + SLC_BLOCK - 1
    ov = (c_start[None, :] <= s_end[:, None]) & (c_end[None, :] >= s_start[:, None]) & (np.arange(ng)[None, :] < nc)
    return jnp.asarray(ov.astype(np.float32), dtype=BF16)


def _nsa_operands(q, k2, vt_all, kc, vct, misc, z_all, seq, nq, tq, tk):
    ng = seq // CMP_STRIDE
    ns = seq // SLC_BLOCK
    assert tq - tk < WINDOW and tk % SLC_BLOCK == 0 and ns <= LANES - BLK_LANE0
    inputs = [_overlap_t(seq), q, k2, vt_all, kc, vct, misc, z_all]
    in_specs = [pl.BlockSpec((ns, ng), lambda b, i: (0, 0)),
                pl.BlockSpec((tq, GROUP_W), lambda b, i: (b * nq + i, 0)),
                pl.BlockSpec((seq, LANES), lambda b, i: (b, 0)),
                pl.BlockSpec((2 * HEAD_DIM, seq), lambda b, i: (NSA_VT_BLK, b)),
                pl.BlockSpec((1, ng, LANES), lambda b, i: (b, 0, 0)),
                pl.BlockSpec((1, HEAD_DIM, ng), lambda b, i: (b, 0, 0)),
                pl.BlockSpec((tq, LANES), lambda b, i: (b * nq + i, 0)),
                pl.BlockSpec((tq, GROUP_W), lambda b, i: (b * nq + i, 0))]
    scratch = [pltpu.VMEM((2 * HEADS, 1, tq), F32),
               pltpu.VMEM((2 * HEADS, ACC_ROWS, tq), F32),
               pltpu.VMEM((2 * HEADS, 2 * LANES, tq), BF16),
               pltpu.VMEM((2 * HEADS, tk, tq), F32),
               pltpu.VMEM((HEADS, ng, tq), F32)]
    return inputs, in_specs, scratch


def _mixers_kernel(*refs, n_in, n_scratch, tq, tk_nsa, tk_diff, seq, lam_init):
    bounds = np.cumsum([0] + list(n_in))
    nsa_in, diff_in, ret_in, ssd_in = (refs[a:b] for a, b in zip(bounds[:-1], bounds[1:]))
    nsa_out, diff_out, ret_out, ssd_out = refs[bounds[-1]:bounds[-1] + 4]
    sb = np.cumsum([0] + list(n_scratch)) + bounds[-1] + 4
    nsa_scr, diff_scr, rec_scr = (refs[a:b] for a, b in zip(sb[:-1], sb[1:]))
    ret_st, ssd_ext, ssd_st = rec_scr
    qi = pl.program_id(1)

    @pl.when(qi == 0)
    def _():
        ret_st[...] = jnp.zeros(ret_st.shape, F32)
        ssd_st[...] = jnp.zeros(ssd_st.shape, F32)
        ssd_ext[0:8, :] = jnp.zeros((8, CONV_CH), F32)

    chunks = list(range(ret_out.shape[0]))
    first, second = chunks[:len(chunks) // 2], chunks[len(chunks) // 2:]
    nsa_loop, nsa_back_window, nsa_diag, nsa_finish = _nsa_phases(
        *nsa_in, nsa_out, *nsa_scr, tq=tq, tk=tk_nsa, seq=seq)
    diff_loop, diff_diag, diff_finish = _diff_phases(
        *diff_in, diff_out, *diff_scr, tq=tq, tk=tk_diff, lam_init=lam_init)
    _ret_body(*ret_in, ret_out, ret_st, first)
    _ssd_body(*ssd_in, ssd_out, ssd_ext, ssd_st, first)

    def body(i, carry):
        _run_interleaved(nsa_loop(i))
        _run_interleaved(diff_loop(i))
        return carry

    lax.fori_loop(0, qi, body, 0)
    nsa_back_window()
    _run_interleaved(nsa_diag())
    _run_interleaved(diff_diag())
    _ret_body(*ret_in, ret_out, ret_st, second)
    _ssd_body(*ssd_in, ssd_out, ssd_ext, ssd_st, second)
    nsa_finish()
    diff_finish()


def _mixers(nsa_args, diff_args, rec_args, batch, seq, layer_idx, tq, tk_nsa, tk_diff):
    nq = seq // tq
    nsa_in, nsa_specs, nsa_scratch = _nsa_operands(*nsa_args, seq, nq, tq, tk_nsa)
    diff_in, diff_specs, diff_scratch = _diff_operands(*diff_args, seq, nq, tq, tk_diff)
    ret_in, ssd_in, rec_specs, rec_out_spec, rec_scratch = _recurrent_operands(*rec_args, nq, tq)
    out_spec = pl.BlockSpec((tq, GROUP_W), lambda b, i: (b * nq + i, 0))
    m = batch * seq
    y_nsa, y_diff, y_ret, y_ssm = pl.pallas_call(
        functools.partial(_mixers_kernel, n_in=(len(nsa_in), len(diff_in), len(ret_in), len(ssd_in)),
                          n_scratch=(len(nsa_scratch), len(diff_scratch), len(rec_scratch)),
                          tq=tq, tk_nsa=tk_nsa, tk_diff=tk_diff, seq=seq,
                          lam_init=0.8 - 0.6 * math.exp(-0.3 * layer_idx)),
        grid=(batch, nq),
        in_specs=nsa_specs + diff_specs + rec_specs,
        out_specs=[out_spec, out_spec, rec_out_spec, rec_out_spec],
        out_shape=[jax.ShapeDtypeStruct((m, GROUP_W), BF16)] * 2
                  + [jax.ShapeDtypeStruct((m // SSM_CHUNK, SSM_CHUNK, GROUP_W), BF16)] * 2,
        scratch_shapes=nsa_scratch + diff_scratch + rec_scratch,
        compiler_params=_params(("parallel", "arbitrary")),
        name="mixers",
    )(*nsa_in, *diff_in, *ret_in, *ssd_in)
    return y_nsa, y_diff, y_ret.reshape(m, GROUP_W), y_ssm.reshape(m, GROUP_W)


def _ret_tables():
    c = RET_CHUNK
    h = np.arange(HEADS, dtype=np.float32)
    log_g = jnp.log(1.0 - 2.0 ** (-5.0 - jnp.asarray(h)))
    pos = jnp.arange(c, dtype=F32)
    rel = pos[:, None] - pos[None, :]
    decay = jnp.where(rel >= 0, jnp.exp(log_g[:, None, None] * jnp.maximum(rel, 0.0)), 0.0)
    xi = jnp.exp(log_g[:, None] * (pos + 1.0))
    zeta = jnp.exp(log_g[:, None] * (c - 1.0 - pos))
    chunk_decay = jnp.exp(log_g * c)
    xi_tab = jnp.repeat(xi.T, HEAD_DIM, axis=1)
    zeta_tab = jnp.repeat(zeta.T, HEAD_DIM, axis=1)
    cd_tab = jnp.repeat(chunk_decay, HEAD_DIM)[None, :]
    return decay, xi_tab, zeta_tab, cd_tab


def _ret_body(decay_ref, xi_ref, zeta_ref, cd_ref, gn_ref, q_ref, k_ref, v_ref, z_ref, o_ref, st_ref, blocks):
    xi = xi_ref[...]
    cd = cd_ref[...]
    for bb in blocks:
        q = (q_ref[bb].astype(F32) * (HEAD_DIM ** -0.5)).astype(BF16)
        k = k_ref[bb]
        v = v_ref[bb]
        kz_t = (k.astype(F32) * zeta_ref[...]).T.astype(BF16)
        outs = []
        for h in range(HEADS):
            sl = slice(h * HEAD_DIM, (h + 1) * HEAD_DIM)
            qh, kh, vh = q[:, sl], k[:, sl], v[:, sl]
            prev = st_ref[h]
            inner = (_dot_nt(qh, kh) * decay_ref[h]).astype(BF16)
            o = _dot(inner, vh) + _dot(qh, prev.astype(BF16)) * xi[:, sl]
            st_ref[h] = prev * cd[:, sl] + _dot(kz_t[sl, :], vh)
            mu = jnp.mean(o, axis=-1, keepdims=True)
            d = o - mu
            var = jnp.mean(d * d, axis=-1, keepdims=True)
            outs.append(d * lax.rsqrt(var + EPS))
        y = jnp.concatenate(outs, axis=-1) * gn_ref[...] * _silu(z_ref[bb])
        o_ref[bb] = y.astype(o_ref.dtype)


def _ssd_body(cw_ref, cb_ref, dtb_ref, a_ref, dsk_ref, nw_ref, xbc_ref, misc_ref, z_ref, o_ref,
              ext_ref, st_ref, blocks):
    L = SSM_CHUNK
    hi = lax.Precision.HIGHEST
    row = lax.broadcasted_iota(jnp.int32, (L, L), 0)
    col = lax.broadcasted_iota(jnp.int32, (L, L), 1)
    causal = row >= col
    tril = jnp.where(causal, 1.0, 0.0).astype(F32)
    dsk = dsk_ref[...]

    for bb in blocks:
        raw = xbc_ref[bb]
        ext_ref[8:8 + L, :] = raw
        conv = cb_ref[...] + raw * cw_ref[CONV_W - 1:CONV_W, :]
        for w in range(CONV_W - 1):
            shift = CONV_W - 1 - w
            conv = conv + ext_ref[8 - shift:8 - shift + L, :] * cw_ref[w:w + 1, :]
        ext_ref[0:8, :] = raw[L - 8:L, :]
        xc = _silu(conv)
        x = xc[:, 0:GROUP_W]
        bm = xc[:, GROUP_W:GROUP_W + 2 * SSM_STATE].astype(BF16)
        cm = xc[:, GROUP_W + 2 * SSM_STATE:].astype(BF16)

        dt_full = jax.nn.softplus(misc_ref[bb] + dtb_ref[...])
        da = dt_full * a_ref[...]
        cs_col = jnp.dot(tril, da, precision=hi, preferred_element_type=F32)
        cs_row = lax.dot_general(da, tril, (((0,), (1,)), ((), ())), precision=hi,
                                 preferred_element_type=F32)

        outs = []
        for h in range(HEADS):
            g = h // 2
            c0 = DT_COL + h
            sl = slice(h * HEAD_DIM, (h + 1) * HEAD_DIM)
            gs = slice(g * SSM_STATE, (g + 1) * SSM_STATE)
            cs_c = cs_col[:, c0:c0 + 1]
            cs_r = cs_row[c0:c0 + 1, :]
            cs_last = cs_col[L - 1:L, c0:c0 + 1]
            xh = x[:, sl]
            xdt = xh * dt_full[:, c0:c0 + 1]
            seg = jnp.exp(jnp.where(causal, cs_c - cs_r, NEG_INF))
            cb = _dot_nt(cm[:, gs], bm[:, gs])
            y = _dot((cb * seg).astype(BF16), xdt.astype(BF16))
            prev = st_ref[h]
            y = y + _dot(cm[:, gs], prev.astype(BF16)) * jnp.exp(cs_c)
            y = y + dsk[:, sl] * xh
            dec = jnp.exp(cs_last - cs_c)
            st_ref[h] = prev * jnp.exp(cs_last) + _dot_tn(bm[:, gs], (xdt * dec).astype(BF16))
            outs.append(y)
        y = jnp.concatenate(outs, axis=-1) * _silu(z_ref[bb])
        ms = jnp.mean(y * y, axis=-1, keepdims=True)
        o_ref[bb] = (y * lax.rsqrt(ms + EPS) * nw_ref[...]).astype(o_ref.dtype)


def _recurrent_operands(ret_qkv, xbc, misc, z_all, gn_w, conv_w, conv_b, dt_bias, a_log, d_skip, norm_w, nq, tq):
    L = SSM_CHUNK
    assert RET_CHUNK == L and tq % L == 0
    cpt = tq // L
    decay, xi_tab, zeta_tab, cd_tab = _ret_tables()
    dtb = jnp.zeros((1, 128), F32).at[0, DT_COL:DT_COL + HEADS].set(dt_bias)
    a_full = jnp.zeros((1, 128), F32).at[0, DT_COL:DT_COL + HEADS].set(-jnp.exp(a_log))
    dsk = jnp.repeat(d_skip, HEAD_DIM)[None, :]

    def full(shape):
        return pl.BlockSpec(shape, lambda b, i: (0,) * len(shape))

    def blk(width, col):
        return pl.BlockSpec((cpt, L, width), lambda b, i: (b * nq + i, 0, col))

    def chunked(a):
        return a.reshape(-1, L, a.shape[-1])

    qkv3, z3 = chunked(ret_qkv), chunked(z_all)
    ret_in = [decay, xi_tab, zeta_tab, cd_tab, gn_w.reshape(1, GROUP_W), qkv3, qkv3, qkv3, z3]
    ret_specs = [full((HEADS, L, L)), full((L, GROUP_W)), full((L, GROUP_W)), full((1, GROUP_W)),
                 full((1, GROUP_W)), blk(GROUP_W, 0), blk(GROUP_W, 1), blk(GROUP_W, 2), blk(GROUP_W, 2)]
    ssd_in = [conv_w, conv_b.reshape(1, CONV_CH), dtb, a_full, dsk, norm_w.reshape(1, GROUP_W),
              chunked(xbc), chunked(misc), z3]
    ssd_specs = [full((CONV_W, CONV_CH)), full((1, CONV_CH)), full((1, 128)), full((1, 128)),
                 full((1, GROUP_W)), full((1, GROUP_W)), blk(CONV_CH, 0), blk(128, 0), blk(GROUP_W, 3)]
    scratch = [pltpu.VMEM((HEADS, HEAD_DIM, HEAD_DIM), F32),
               pltpu.VMEM((8 + L, CONV_CH), F32),
               pltpu.VMEM((HEADS, SSM_STATE, HEAD_DIM), F32)]
    return ret_in, ssd_in, ret_specs + ssd_specs, blk(GROUP_W, 0), scratch


def _pick_tile(n, pref):
    t = pref
    while n % t:
        t //= 2
    return t


def kernel(x, norm_w, w_in, w_out, nsa_pe_k, nsa_pe_v, nsa_w_ck1, nsa_w_ck2, nsa_w_cv1, nsa_w_cv2,
           diff_lam_q1, diff_lam_k1, diff_lam_q2, diff_lam_k2, diff_subln_w, ret_gn_w,
           ssm_conv_w, ssm_conv_b, ssm_dt_bias, ssm_A_log, ssm_D, ssm_norm_w, final_norm_w):
    batch, seq, _ = x.shape
    depth = w_in.shape[0]
    m = batch * seq
    tm = _pick_tile(m, 512)
    tq = _pick_tile(seq, 512)
    tk = _pick_tile(seq, 256)
    tk_diff = _pick_tile(seq, 512)
    w_in_b = w_in.astype(BF16)
    w_r = _relayout_w_in(w_in_b)
    w_t = _relayout_w_in_t(w_in_b)
    w_out_b = w_out.astype(BF16)
    x2d = x.reshape(m, D_MODEL)
    projected = _in_proj(x2d, norm_w[0], w_r[0], w_t[0], tm)
    for i in range(depth):
        nsa_q, nsa_k2, nsa_cmp, misc, z_all, diff_qk, ret_qkv, xbc, vt_all = projected
        kc, vct = _nsa_compress(nsa_cmp, nsa_pe_k[i], nsa_pe_v[i], nsa_w_ck1[i], nsa_w_ck2[i],
                                nsa_w_cv1[i], nsa_w_cv2[i], batch, seq)
        lam_p = jnp.stack([diff_lam_q1[i], diff_lam_k1[i], diff_lam_q2[i], diff_lam_k2[i]])
        ys = _mixers((nsa_q, nsa_k2, vt_all, kc, vct, misc, z_all),
                     (diff_qk, vt_all, z_all, lam_p, diff_subln_w[i]),
                     (ret_qkv, xbc, misc, z_all, ret_gn_w[i], ssm_conv_w[i], ssm_conv_b[i], ssm_dt_bias[i],
                      ssm_A_log[i], ssm_D[i], ssm_norm_w[i]),
                     batch, seq, i, tq, tk, tk_diff)
        if i + 1 < depth:
            x2d, *projected = _out_in_proj(ys, w_out_b[i], x2d, norm_w[i + 1], w_r[i + 1], w_t[i + 1], tm)
        else:
            x2d = _out_proj(ys, w_out_b[i], x2d, final_norm_w, tm)
    return x2d.reshape(batch, seq, D_MODEL)
```

```python
import functools
import math

import numpy as np
import jax
import jax.numpy as jnp
from jax import lax
from jax.experimental import pallas as pl
from jax.experimental.pallas import tpu as pltpu

F32 = jnp.float32
BF16 = jnp.bfloat16
NEG_INF = float("-inf")
LOG2E = 1.4426950408889634

D_MODEL = 1024
DEPTH = 4
GROUP_W = 256
HEADS = 4
HEAD_DIM = 64
EPS = 1e-6
CMP_LEN = 32
CMP_STRIDE = 16
CMP_HIDDEN = 256
SLC_BLOCK = 64
SLC_SHIFT = 6
SLC_TOPK = 16
WINDOW = 512
DIFF_QK_DIM = 32
RET_CHUNK = 128
SSM_STATE = 128
SSM_CHUNK = 128
CONV_W = 4
CONV_CH = 768
N_ALIBI_HEADS = 8
LANES = 128
BF16_ROWS = 16
ACC_ROWS = HEAD_DIM + BF16_ROWS
QK_LOOKAHEAD = 3
ALIBI_ROWS = 3
POS_RADIX = 256
MASK_BIG = 2.0 ** 100
BLK_LANE0 = 64

IN_LAYOUT = (
    ("nsa_q", 256), ("nsa_k_cmp", 64), ("nsa_v_cmp", 64), ("nsa_k_slc", 64), ("nsa_v_slc", 64),
    ("nsa_k_win", 64), ("nsa_v_win", 64), ("nsa_gate", 12), ("nsa_z", 256),
    ("diff_q", 256), ("diff_k", 256), ("diff_v", 256), ("diff_z", 256),
    ("ret_q", 256), ("ret_k", 256), ("ret_v", 256), ("ret_z", 256),
    ("ssm_z", 256), ("ssm_xbc", 768), ("ssm_dt", 4),
)
IN_OFF = {}
_o = 0
for _n, _w in IN_LAYOUT:
    IN_OFF[_n] = (_o, _w)
    _o += _w
IN_W = _o

GATE_COL = 0
DT_COL = 12
IN_OUTPUTS = (
    ("nsa_q", BF16, ("nsa_q",), 256),
    ("nsa_k2", BF16, ("nsa_k_slc", "nsa_k_win"), 128),
    ("nsa_cmp", F32, ("nsa_k_cmp", "nsa_v_cmp"), 128),
    ("misc", F32, ("nsa_gate", "ssm_dt"), 128),
    ("z_all", F32, ("nsa_z", "diff_z", "ret_z", "ssm_z"), 1024),
    ("diff_qk", BF16, ("diff_q", "diff_k"), 512),
    ("ret_qkv", BF16, ("ret_q", "ret_k", "ret_v"), 768),
    ("xbc", F32, ("ssm_xbc",), 768),
)
IN_T_SRC = ("diff_v", "nsa_v_slc", "nsa_v_win")
IN_T_ROWS = 384
DIFF_VT_BLK = 0
NSA_VT_BLK = 2
IN_SEGS = []
_o = 0
for _n, _dt, _src, _w in IN_OUTPUTS:
    IN_SEGS.append((_o, _o + _w))
    _o += _w
IN_WP = _o

VMEM_LIMIT = 56 * 1024 * 1024


def _alibi_slopes():
    return [2.0 ** (-8.0 * (i + 1) / N_ALIBI_HEADS) for i in range(N_ALIBI_HEADS)]


NSA_SLOPES = _alibi_slopes()[0::2]
DIFF_SLOPES = _alibi_slopes()[1::2]


def _silu(x):
    return x * jax.nn.sigmoid(x)


def _dot(a, b):
    return jnp.dot(a, b, preferred_element_type=F32)


def _dot_nt(a, b):
    return lax.dot_general(a, b, (((1,), (1,)), ((), ())), preferred_element_type=F32)


def _dot_tn(a, b):
    return lax.dot_general(a, b, (((0,), (0,)), ((), ())), preferred_element_type=F32)


def _params(sem):
    return pltpu.CompilerParams(dimension_semantics=sem, vmem_limit_bytes=VMEM_LIMIT)


def _relayout_w_in(w_in):
    cols = []
    for _n, _dt, src, width in IN_OUTPUTS:
        used = 0
        for s in src:
            off, w = IN_OFF[s]
            cols.append(w_in[:, :, off:off + w])
            used += w
        if used < width:
            cols.append(jnp.zeros(w_in.shape[:2] + (width - used,), w_in.dtype))
    return jnp.concatenate(cols, axis=-1)


def _relayout_w_in_t(w_in):
    cols = [w_in[:, :, IN_OFF[s][0]:IN_OFF[s][0] + IN_OFF[s][1]] for s in IN_T_SRC]
    return jnp.swapaxes(jnp.concatenate(cols, axis=-1), 1, 2)


def _norm_project(x, nw_ref, w_ref, wt_ref, out_refs):
    ms = jnp.mean(x * x, axis=-1, keepdims=True)
    h = (x * lax.rsqrt(ms + EPS) * nw_ref[...]).astype(BF16)
    for ref, (a, b) in zip(out_refs[:-1], IN_SEGS):
        ref[...] = _dot(h, w_ref[:, a:b]).astype(ref.dtype)
    out_refs[-1][...] = _dot_nt(wt_ref[...], h).astype(BF16)


def _in_proj_kernel(x_ref, nw_ref, w_ref, wt_ref, *out_refs):
    _norm_project(x_ref[...], nw_ref, w_ref, wt_ref, out_refs)


def _in_proj_outputs(m, tm):
    out_shape = [jax.ShapeDtypeStruct((m, b - a), dt) for (_n, dt, _s, _w), (a, b) in zip(IN_OUTPUTS, IN_SEGS)]
    out_specs = [pl.BlockSpec((tm, b - a), lambda i: (i, 0)) for (a, b) in IN_SEGS]
    out_shape.append(jax.ShapeDtypeStruct((IN_T_ROWS, m), BF16))
    out_specs.append(pl.BlockSpec((IN_T_ROWS, tm), lambda i: (0, i)))
    return out_shape, out_specs


def _in_proj(x2d, norm_w, w_r, w_t, tm):
    m = x2d.shape[0]
    out_shape, out_specs = _in_proj_outputs(m, tm)
    return pl.pallas_call(
        _in_proj_kernel,
        grid=(m // tm,),
        in_specs=[pl.BlockSpec((tm, D_MODEL), lambda i: (i, 0)),
                  pl.BlockSpec((1, D_MODEL), lambda i: (0, 0)),
                  pl.BlockSpec((D_MODEL, IN_WP), lambda i: (0, 0)),
                  pl.BlockSpec((IN_T_ROWS, D_MODEL), lambda i: (0, 0))],
        out_specs=out_specs,
        out_shape=out_shape,
        compiler_params=_params(("parallel",)),
        name="in_proj",
    )(x2d, norm_w.reshape(1, D_MODEL), w_r, w_t)


def _out_proj_kernel(y0_ref, y1_ref, y2_ref, y3_ref, w_ref, x_ref, fw_ref, o_ref):
    acc = x_ref[...]
    for g, y_ref in enumerate((y0_ref, y1_ref, y2_ref, y3_ref)):
        acc = acc + _dot(y_ref[...], w_ref[g * GROUP_W:(g + 1) * GROUP_W, :])
    ms = jnp.mean(acc * acc, axis=-1, keepdims=True)
    o_ref[...] = acc * lax.rsqrt(ms + EPS) * fw_ref[...]


def _out_proj(ys, w_out_b, x2d, final_w, tm):
    m = x2d.shape[0]
    yspec = pl.BlockSpec((tm, GROUP_W), lambda i: (i, 0))
    return pl.pallas_call(
        _out_proj_kernel,
        grid=(m // tm,),
        in_specs=[yspec, yspec, yspec, yspec,
                  pl.BlockSpec((D_MODEL, D_MODEL), lambda i: (0, 0)),
                  pl.BlockSpec((tm, D_MODEL), lambda i: (i, 0)),
                  pl.BlockSpec((1, D_MODEL), lambda i: (0, 0))],
        out_specs=pl.BlockSpec((tm, D_MODEL), lambda i: (i, 0)),
        out_shape=jax.ShapeDtypeStruct((m, D_MODEL), F32),
        compiler_params=_params(("parallel",)),
        name="out_proj",
    )(*ys, w_out_b, x2d, final_w.reshape(1, D_MODEL))


def _out_in_proj_kernel(y0_ref, y1_ref, y2_ref, y3_ref, wo_ref, x_ref, nw_ref, w_ref, wt_ref, xo_ref, *out_refs):
    acc = x_ref[...]
    for g, y_ref in enumerate((y0_ref, y1_ref, y2_ref, y3_ref)):
        acc = acc + _dot(y_ref[...], wo_ref[g * GROUP_W:(g + 1) * GROUP_W, :])
    xo_ref[...] = acc
    _norm_project(acc, nw_ref, w_ref, wt_ref, out_refs)


def _out_in_proj(ys, w_out_b, x2d, norm_w, w_r, w_t, tm):
    m = x2d.shape[0]
    yspec = pl.BlockSpec((tm, GROUP_W), lambda i: (i, 0))
    xspec = pl.BlockSpec((tm, D_MODEL), lambda i: (i, 0))
    out_shape, out_specs = _in_proj_outputs(m, tm)
    return pl.pallas_call(
        _out_in_proj_kernel,
        grid=(m // tm,),
        in_specs=[yspec, yspec, yspec, yspec,
                  pl.BlockSpec((D_MODEL, D_MODEL), lambda i: (0, 0)),
                  xspec,
                  pl.BlockSpec((1, D_MODEL), lambda i: (0, 0)),
                  pl.BlockSpec((D_MODEL, IN_WP), lambda i: (0, 0)),
                  pl.BlockSpec((IN_T_ROWS, D_MODEL), lambda i: (0, 0))],
        out_specs=[xspec] + out_specs,
        out_shape=[jax.ShapeDtypeStruct((m, D_MODEL), F32)] + out_shape,
        compiler_params=_params(("parallel",)),
        name="out_in_proj",
    )(*ys, w_out_b, x2d, norm_w.reshape(1, D_MODEL), w_r, w_t)


def _stage_scores(s_ref, slot, cols, s, keep=None):
    if keep is not None:
        s = jnp.where(keep, s, NEG_INF)
    s_ref[slot, :, cols] = s
    return jnp.max(s, axis=0, keepdims=True)


def _flash_update_t(s, s_max, c1, shift, vt_ext, m_ref, acc_ref, idx, cols):
    m_old = m_ref[idx, :, cols]
    m_new = jnp.maximum(m_old, c1 * s_max + shift)
    alpha = jnp.exp2(m_old - m_new)
    p = jnp.exp2(c1 * s() - (m_new - shift))
    acc_ref[idx, :, cols] = alpha * acc_ref[idx, :, cols] + _dot(vt_ext, p.astype(BF16))
    m_ref[idx, :, cols] = m_new


def _bf16_pieces(x, n=3):
    out = []
    for _ in range(n):
        p = float(np.asarray(x, np.float32).astype(BF16).astype(np.float32))
        out.append(p)
        x = x - p
    return out


def _key_pos_features(tk):
    lane = lax.broadcasted_iota(jnp.int32, (tk, LANES), 1)
    row = lax.broadcasted_iota(jnp.int32, (tk, LANES), 0)
    out = jnp.zeros((tk, LANES), jnp.int32)
    for d in range(-(-tk // POS_RADIX)):
        digit = jnp.bitwise_and(jnp.right_shift(row, 8 * d), POS_RADIX - 1)
        out = jnp.where((lane >= ALIBI_ROWS * d) & (lane < ALIBI_ROWS * (d + 1)), digit, out)
    return out.astype(F32).astype(BF16)


def _alibi_rows(beta, tq, rows=LANES, tk=POS_RADIX):
    row = lax.broadcasted_iota(jnp.int32, (rows, tq), 0)
    out = jnp.zeros((rows, tq), F32)
    for d in range(-(-tk // POS_RADIX)):
        for r, piece in enumerate(_bf16_pieces(beta * POS_RADIX ** d, ALIBI_ROWS)):
            out = jnp.where(row == ALIBI_ROWS * d + r, piece, out)
    return out.astype(BF16)


def _normalized(acc):
    return acc[0:HEAD_DIM] / jnp.maximum(acc[HEAD_DIM:HEAD_DIM + 1], 1e-30)


def _rel_t(tk, tq):
    return lax.broadcasted_iota(jnp.int32, (tk, tq), 1) - lax.broadcasted_iota(jnp.int32, (tk, tq), 0)


def _diff_phases(lam_ref, sw_ref, q_ref, k_ref, vt_ref, z_ref, o_ref, m_ref, acc_ref, wq_ref, s_ref,
                 *, tq, tk, lam_init):
    qi = pl.program_id(1)
    kpq = tq // tk
    scale = DIFF_QK_DIM ** -0.5
    c1 = scale * LOG2E
    m_ref[...] = jnp.full(m_ref.shape, NEG_INF, F32)
    acc_ref[...] = jnp.zeros(acc_ref.shape, F32)
    rel = _rel_t(tk, tq)

    qt = q_ref[...].astype(F32).T.astype(BF16)
    row = lax.broadcasted_iota(jnp.int32, (LANES, tq), 0)
    for j in range(2 * HEADS):
        g, r0 = divmod(j * DIFF_QK_DIM, LANES)
        qg = qt[g * LANES:(g + 1) * LANES]
        wq_ref[j, 0:LANES, :] = jnp.where((row >= r0) & (row < r0 + DIFF_QK_DIM), qg, jnp.zeros_like(qg))
        wq_ref[j, LANES:2 * LANES, :] = _alibi_rows(DIFF_SLOPES[j // 2] / scale, tq, tk=tk)

    ones = jnp.ones((BF16_ROWS, tk), BF16)
    kpos = _key_pos_features(tk)

    n_maps = 2 * HEADS
    n_slots = s_ref.shape[0]

    def tiles(kis, mask_offs):
        loaded = []
        for ki in kis:
            start = pl.multiple_of(ki * tk, tk)
            loaded.append((k_ref[pl.ds(start, tk), :],
                           vt_ref[:, pl.ds(start, tk)],
                           (ki * tk).astype(F32)))
        items = [(t, j) for t in range(len(kis)) for j in range(n_maps)]
        cols = slice(0, tq)

        def scores(i):
            t, j = items[i]
            g = (j * DIFF_QK_DIM) // LANES
            lhs = jnp.concatenate([loaded[t][0][:, g * LANES:(g + 1) * LANES], kpos], axis=1)
            s_ref[i % n_slots] = _dot(lhs, wq_ref[j])

        def staged(i):
            t = items[i][0]
            if mask_offs[t] is None:
                return lambda: s_ref[i % n_slots]
            return lambda: jnp.where(rel >= mask_offs[t], s_ref[i % n_slots], NEG_INF)

        for i in range(QK_LOOKAHEAD):
            scores(i)
        for i, (t, j) in enumerate(items):
            h = j // 2
            if i + QK_LOOKAHEAD < len(items):
                scores(i + QK_LOOKAHEAD)
            _k, vt, key0 = loaded[t]
            vt_ext = jnp.concatenate([vt[h * HEAD_DIM:(h + 1) * HEAD_DIM], ones], axis=0)
            s = staged(i)
            _flash_update_t(s, jnp.max(s(), axis=0, keepdims=True), c1, (DIFF_SLOPES[h] * LOG2E) * key0,
                            vt_ext, m_ref, acc_ref, j, cols)

    def loop_body(i):
        tiles([i * kpq + d for d in range(kpq)], [None] * kpq)

    def diag_tiles():
        tiles([qi * kpq + d for d in range(kpq)], [d * tk for d in range(kpq)])

    def finish():
        lp = lam_ref[...]
        lam = (jnp.exp(jnp.sum(lp[0:1] * lp[1:2], axis=-1, keepdims=True))
               - jnp.exp(jnp.sum(lp[2:3] * lp[3:4], axis=-1, keepdims=True)) + lam_init)
        sw = sw_ref[...]
        o_t = jnp.concatenate([_normalized(acc_ref[2 * h]) - lam * _normalized(acc_ref[2 * h + 1])
                               for h in range(HEADS)], axis=0)
        o = o_t.T
        outs = []
        for h in range(HEADS):
            oh = o[:, h * HEAD_DIM:(h + 1) * HEAD_DIM]
            ms = jnp.mean(oh * oh, axis=-1, keepdims=True)
            outs.append(oh * lax.rsqrt(ms + EPS) * sw * (1.0 - lam_init))
        y = jnp.concatenate(outs, axis=-1) * _silu(z_ref[...])
        o_ref[...] = y.astype(o_ref.dtype)

    return loop_body, diag_tiles, finish


def _diff_operands(qk, vt_all, z_all, lam_p, subln_w, seq, nq, tq, tk):
    inputs = [lam_p, subln_w.reshape(1, HEAD_DIM), qk, qk, vt_all, z_all]
    in_specs = [pl.BlockSpec((4, DIFF_QK_DIM), lambda b, i: (0, 0)),
                pl.BlockSpec((1, HEAD_DIM), lambda b, i: (0, 0)),
                pl.BlockSpec((tq, GROUP_W), lambda b, i: (b * nq + i, 0)),
                pl.BlockSpec((seq, GROUP_W), lambda b, i: (b, 1)),
                pl.BlockSpec((GROUP_W, seq), lambda b, i: (DIFF_VT_BLK, b)),
                pl.BlockSpec((tq, GROUP_W), lambda b, i: (b * nq + i, 1))]
    scratch = [pltpu.VMEM((2 * HEADS, 1, tq), F32),
               pltpu.VMEM((2 * HEADS, ACC_ROWS, tq), F32),
               pltpu.VMEM((2 * HEADS, 2 * LANES, tq), BF16),
               pltpu.VMEM((QK_LOOKAHEAD + 1, tk, tq), F32)]
    return inputs, in_specs, scratch


def _nsa_compress_kernel(cmp_ref, pe_ref, w1_ref, wk2_ref, wv2t_ref, kc_ref, vct_ref):
    ng = kc_ref.shape[1]
    top = jnp.zeros((ng, 2 * CMP_HIDDEN), F32)
    bot = jnp.zeros((ng, 2 * CMP_HIDDEN), F32)
    for p in range(CMP_STRIDE):
        x = cmp_ref[pl.ds(p, ng, stride=CMP_STRIDE), :]
        top = top + _dot((x + pe_ref[p:p + 1, :]).astype(BF16), w1_ref[p])
        bot = bot + _dot((x + pe_ref[CMP_STRIDE + p:CMP_STRIDE + p + 1, :]).astype(BF16), w1_ref[CMP_STRIDE + p])
    hid = _silu(top + pltpu.roll(bot, ng - 1, 0)).astype(BF16)
    hk, hv = hid[:, 0:CMP_HIDDEN], hid[:, CMP_HIDDEN:]
    kc = _dot(hk, wk2_ref[...])
    lane = lax.broadcasted_iota(jnp.int32, kc.shape, 1)
    blk = lax.broadcasted_iota(jnp.int32, kc.shape, 0).astype(F32)
    kc = jnp.where((lane >= HEAD_DIM) & (lane < HEAD_DIM + ALIBI_ROWS), blk, kc)
    kc_ref[0] = kc.astype(kc_ref.dtype)
    vct_ref[0] = _dot_nt(wv2t_ref[...], hv).astype(vct_ref.dtype)


def _nsa_compress(cmp2d, pe_k, pe_v, w_ck1, w_ck2, w_cv1, w_cv2, batch, seq):
    ng = seq // CMP_STRIDE
    wk = w_ck1.reshape(CMP_LEN, HEAD_DIM, CMP_HIDDEN)
    wv = w_cv1.reshape(CMP_LEN, HEAD_DIM, CMP_HIDDEN)
    w1 = jnp.concatenate([jnp.pad(wk, ((0, 0), (0, 0), (0, CMP_HIDDEN))),
                          jnp.pad(wv, ((0, 0), (0, 0), (CMP_HIDDEN, 0)))], axis=1).astype(BF16)
    pe = jnp.concatenate([pe_k, pe_v], axis=1)

    def full(shape):
        return pl.BlockSpec(shape, lambda b: (0,) * len(shape))

    return pl.pallas_call(
        _nsa_compress_kernel,
        grid=(batch,),
        in_specs=[pl.BlockSpec((seq, LANES), lambda b: (b, 0)), full((CMP_LEN, LANES)),
                  full((CMP_LEN, LANES, 2 * CMP_HIDDEN)), full((CMP_HIDDEN, LANES)),
                  full((HEAD_DIM, CMP_HIDDEN))],
        out_specs=[pl.BlockSpec((1, ng, LANES), lambda b: (b, 0, 0)),
                   pl.BlockSpec((1, HEAD_DIM, ng), lambda b: (b, 0, 0))],
        out_shape=[jax.ShapeDtypeStruct((batch, ng, LANES), BF16),
                   jax.ShapeDtypeStruct((batch, HEAD_DIM, ng), BF16)],
        compiler_params=_params(("parallel",)),
        name="nsa_compress",
    )(cmp2d, pe, w1, jnp.pad(w_ck2, ((0, 0), (0, LANES - HEAD_DIM))).astype(BF16), w_cv2.T.astype(BF16))


def _nsa_phases(ovl_ref, q_ref, k_ref, vt_ref, kc_ref, vct_ref, misc_ref, z_ref, o_ref,
                m_ref, acc_ref, wq_ref, s_ref, sc_ref, *, tq, tk, seq):
    qi = pl.program_id(1)
    kpq = tq // tk
    scale = HEAD_DIM ** -0.5
    c1 = scale * LOG2E
    ng = seq // CMP_STRIDE
    ns = seq // SLC_BLOCK
    top = min(SLC_TOPK, ns)
    m_ref[...] = jnp.full(m_ref.shape, NEG_INF, F32)
    acc_ref[...] = jnp.zeros(acc_ref.shape, F32)
    rel = _rel_t(tk, tq)

    qt = q_ref[...].astype(F32).T.astype(BF16)
    zeros_q = jnp.zeros((HEAD_DIM, tq), BF16)
    for h in range(HEADS):
        qh = qt[h * HEAD_DIM:(h + 1) * HEAD_DIM]
        alibi = _alibi_rows(NSA_SLOPES[h] / scale, tq, tk=tk)
        wq_ref[h, 0:LANES, :] = jnp.concatenate([qh, zeros_q], axis=0)
        wq_ref[h, LANES:2 * LANES, :] = alibi
        wq_ref[HEADS + h, 0:LANES, :] = jnp.concatenate([zeros_q, qh], axis=0)
        wq_ref[HEADS + h, LANES:2 * LANES, :] = alibi

    t_lane = qi * tq + lax.broadcasted_iota(jnp.int32, (1, tq), 1)
    n_sub = lax.broadcasted_iota(jnp.int32, (ng, 1), 0)
    c_valid = (n_sub * CMP_STRIDE + (CMP_LEN - 1)) <= t_lane
    kc = kc_ref[0]
    vct_ext = jnp.concatenate([vct_ref[0], jnp.ones((BF16_ROWS, ng), BF16)], axis=0)
    ovl = ovl_ref[...]
    cmp_max = []
    for h in range(HEADS):
        rhs = jnp.concatenate([qt[h * HEAD_DIM:(h + 1) * HEAD_DIM],
                               _alibi_rows(NSA_SLOPES[h] * CMP_STRIDE / scale, tq, HEAD_DIM)], axis=0)
        cmp_max.append(_stage_scores(sc_ref, h, slice(0, tq), _dot(kc, rhs), c_valid))
    imp = jnp.zeros((ns, tq), F32)
    o_cmp = []
    for h in range(HEADS):
        mx = jnp.where(cmp_max[h] > NEG_INF, cmp_max[h], 0.0)
        pb = jnp.exp2(c1 * sc_ref[h] - c1 * mx).astype(BF16)
        o_ext = _dot(vct_ext, pb)
        r = 1.0 / jnp.maximum(o_ext[HEAD_DIM:HEAD_DIM + 1], 1e-30)
        o_cmp.append(o_ext[0:HEAD_DIM] * r)
        imp = imp + _dot(ovl, pb) * r

    def select_blocks():
        j_sub = lax.broadcasted_iota(jnp.int32, (ns, 1), 0)
        j_sub_f = j_sub.astype(F32)
        cur = jnp.right_shift(t_lane, SLC_SHIFT)
        forced = (j_sub == 0) | (j_sub == cur) | (j_sub == cur - 1)
        valid = (j_sub * SLC_BLOCK) <= t_lane
        score = jnp.where(forced, 1e30, jnp.where(valid, imp, -1.0))
        sel = jnp.zeros((ns, tq), F32)
        for _ in range(top):
            mx = jnp.max(score, axis=0, keepdims=True)
            idx = jnp.min(jnp.where(score == mx, j_sub_f, float(ns)), axis=0, keepdims=True)
            pick = j_sub_f == idx
            sel = jnp.where(pick, 1.0, sel)
            score = jnp.where(pick, -2.0, score)
        unsel = jnp.where(sel > 0.5, 0.0, -MASK_BIG).astype(BF16)
        for h in range(HEADS):
            wq_ref[h, LANES + BLK_LANE0:LANES + BLK_LANE0 + ns, :] = unsel

    aux_lane = lax.broadcasted_iota(jnp.int32, (tk, LANES), 1)
    aux_blk = jnp.right_shift(lax.broadcasted_iota(jnp.int32, (tk, LANES), 0), SLC_SHIFT) + BLK_LANE0
    kpos = _key_pos_features(tk)
    ones = jnp.ones((BF16_ROWS, tk), BF16)

    n_slots = s_ref.shape[0]

    def tiles(window, kis, cols, mask_offs=None):
        base = HEADS if window else 0
        loaded = []
        for t, ki in enumerate(kis):
            start = pl.multiple_of(ki * tk, tk)
            k = k_ref[pl.ds(start, tk), :]
            if window:
                vt = vt_ref[HEAD_DIM:2 * HEAD_DIM, pl.ds(start, tk)]
                aux = kpos
                dist = rel[:, cols[t]] + (qi * tq - ki * tk)
                keep = (dist >= 0) & (dist < WINDOW)
            else:
                vt = vt_ref[0:HEAD_DIM, pl.ds(start, tk)]
                onehot = aux_lane == aux_blk + ki * (tk // SLC_BLOCK)
                aux = jnp.where(onehot, jnp.ones_like(kpos), kpos)
                keep = None
            loaded.append((jnp.concatenate([k, aux], axis=1),
                           jnp.concatenate([vt, ones], axis=0), (ki * tk).astype(F32), keep))
        items = [(t, h) for t in range(len(kis)) for h in range(HEADS)]
        s_max = {}

        def scores(i):
            t, h = items[i]
            if window:
                keep = loaded[t][3]
            else:
                keep = None if mask_offs[t] is None else rel[:, cols[t]] >= mask_offs[t]
            s_max[i] = _stage_scores(s_ref, i % n_slots, cols[t],
                                     _dot(loaded[t][0], wq_ref[base + h, :, cols[t]]), keep)

        for i in range(QK_LOOKAHEAD):
            scores(i)
        for i, (t, h) in enumerate(items):
            if i + QK_LOOKAHEAD < len(items):
                scores(i + QK_LOOKAHEAD)
            _lhs, vt_ext, key0, _keep = loaded[t]
            _flash_update_t(lambda i=i, t=t: s_ref[i % n_slots, :, cols[t]], s_max.pop(i), c1,
                            (NSA_SLOPES[h] * LOG2E) * key0, vt_ext, m_ref, acc_ref, base + h, cols[t])

    all_cols = slice(0, tq)
    diag_cols = [slice(d * tk, tq) for d in range(kpq)]

    tiles(True, [qi * kpq + d for d in range(kpq)], diag_cols)
    select_blocks()

    def loop_body(i):
        tiles(False, [i * kpq + d for d in range(kpq)], [all_cols] * kpq, [None] * kpq)

    def back_window():
        n_back = (WINDOW + tk - 1) // tk
        for g in range((n_back + kpq - 1) // kpq):
            backs = list(range(g * kpq + 1, min((g + 1) * kpq, n_back) + 1))
            back_cols = [slice(0, min(tq, -(-(WINDOW - (back - 1) * tk - 1) // LANES) * LANES)) for back in backs]

            @pl.when(qi > g)
            def _():
                tiles(True, [qi * kpq - back for back in backs], back_cols)

    def diag_tiles():
        tiles(False, [qi * kpq + d for d in range(kpq)], diag_cols, [d * tk for d in range(kpq)])

    def finish():
        g_t = jax.nn.sigmoid(misc_ref[...]).T
        outs = []
        for h in range(HEADS):
            r0 = GATE_COL + 3 * h
            outs.append(g_t[r0:r0 + 1] * o_cmp[h] + g_t[r0 + 1:r0 + 2] * _normalized(acc_ref[h])
                        + g_t[r0 + 2:r0 + 3] * _normalized(acc_ref[HEADS + h]))
        y = jnp.concatenate(outs, axis=0).T * _silu(z_ref[...])
        o_ref[...] = y.astype(o_ref.dtype)

    return loop_body, back_window, diag_tiles, finish


def _overlap_t(seq):
    nc = (seq - CMP_LEN) // CMP_STRIDE + 1
    ng = seq // CMP_STRIDE
    ns = seq // SLC_BLOCK
    c_start = np.arange(ng) * CMP_STRIDE
    c_end = c_start + CMP_LEN - 1
    s_start = np.arange(ns) * SLC_BLOCK
    s_end = s_start + SLC_BLOCK - 1
    ov = (c_start[None, :] <= s_end[:, None]) & (c_end[None, :] >= s_start[:, None]) & (np.arange(ng)[None, :] < nc)
    return jnp.asarray(ov.astype(np.float32), dtype=BF16)


def _nsa_operands(q, k2, vt_all, kc, vct, misc, z_all, seq, nq, tq, tk):
    ng = seq // CMP_STRIDE
    ns = seq // SLC_BLOCK
    assert tq - tk < WINDOW and tk % SLC_BLOCK == 0 and ns <= LANES - BLK_LANE0
    inputs = [_overlap_t(seq), q, k2, vt_all, kc, vct, misc, z_all]
    in_specs = [pl.BlockSpec((ns, ng), lambda b, i: (0, 0)),
                pl.BlockSpec((tq, GROUP_W), lambda b, i: (b * nq + i, 0)),
                pl.BlockSpec((seq, LANES), lambda b, i: (b, 0)),
                pl.BlockSpec((2 * HEAD_DIM, seq), lambda b, i: (NSA_VT_BLK, b)),
                pl.BlockSpec((1, ng, LANES), lambda b, i: (b, 0, 0)),
                pl.BlockSpec((1, HEAD_DIM, ng), lambda b, i: (b, 0, 0)),
                pl.BlockSpec((tq, LANES), lambda b, i: (b * nq + i, 0)),
                pl.BlockSpec((tq, GROUP_W), lambda b, i: (b * nq + i, 0))]
    scratch = [pltpu.VMEM((2 * HEADS, 1, tq), F32),
               pltpu.VMEM((2 * HEADS, ACC_ROWS, tq), F32),
               pltpu.VMEM((2 * HEADS, 2 * LANES, tq), BF16),
               pltpu.VMEM((2 * HEADS, tk, tq), F32),
               pltpu.VMEM((HEADS, ng, tq), F32)]
    return inputs, in_specs, scratch


def _mixers_kernel(*refs, n_in, n_scratch, tq, tk_nsa, tk_diff, seq, lam_init):
    bounds = np.cumsum([0] + list(n_in))
    nsa_in, diff_in, ret_in, ssd_in = (refs[a:b] for a, b in zip(bounds[:-1], bounds[1:]))
    nsa_out, diff_out, ret_out, ssd_out = refs[bounds[-1]:bounds[-1] + 4]
    sb = np.cumsum([0] + list(n_scratch)) + bounds[-1] + 4
    nsa_scr, diff_scr, rec_scr = (refs[a:b] for a, b in zip(sb[:-1], sb[1:]))
    ret_st, ssd_ext, ssd_st = rec_scr
    qi = pl.program_id(1)

    @pl.when(qi == 0)
    def _():
        ret_st[...] = jnp.zeros(ret_st.shape, F32)
        ssd_st[...] = jnp.zeros(ssd_st.shape, F32)
        ssd_ext[0:8, :] = jnp.zeros((8, CONV_CH), F32)

    chunks = list(range(ret_out.shape[0]))
    first, second = chunks[:len(chunks) // 2], chunks[len(chunks) // 2:]
    nsa_loop, nsa_back_window, nsa_diag, nsa_finish = _nsa_phases(
        *nsa_in, nsa_out, *nsa_scr, tq=tq, tk=tk_nsa, seq=seq)
    diff_loop, diff_diag, diff_finish = _diff_phases(
        *diff_in, diff_out, *diff_scr, tq=tq, tk=tk_diff, lam_init=lam_init)
    _ret_body(*ret_in, ret_out, ret_st, first)
    _ssd_body(*ssd_in, ssd_out, ssd_ext, ssd_st, first)

    def body(i, carry):
        nsa_loop(i)
        diff_loop(i)
        return carry

    lax.fori_loop(0, qi, body, 0)
    nsa_back_window()
    nsa_diag()
    diff_diag()
    _ret_body(*ret_in, ret_out, ret_st, second)
    _ssd_body(*ssd_in, ssd_out, ssd_ext, ssd_st, second)
    nsa_finish()
    diff_finish()


def _mixers(nsa_args, diff_args, rec_args, batch, seq, layer_idx, tq, tk_nsa, tk_diff):
    nq = seq // tq
    nsa_in, nsa_specs, nsa_scratch = _nsa_operands(*nsa_args, seq, nq, tq, tk_nsa)
    diff_in, diff_specs, diff_scratch = _diff_operands(*diff_args, seq, nq, tq, tk_diff)
    ret_in, ssd_in, rec_specs, rec_out_spec, rec_scratch = _recurrent_operands(*rec_args, nq, tq)
    out_spec = pl.BlockSpec((tq, GROUP_W), lambda b, i: (b * nq + i, 0))
    m = batch * seq
    y_nsa, y_diff, y_ret, y_ssm = pl.pallas_call(
        functools.partial(_mixers_kernel, n_in=(len(nsa_in), len(diff_in), len(ret_in), len(ssd_in)),
                          n_scratch=(len(nsa_scratch), len(diff_scratch), len(rec_scratch)),
                          tq=tq, tk_nsa=tk_nsa, tk_diff=tk_diff, seq=seq,
                          lam_init=0.8 - 0.6 * math.exp(-0.3 * layer_idx)),
        grid=(batch, nq),
        in_specs=nsa_specs + diff_specs + rec_specs,
        out_specs=[out_spec, out_spec, rec_out_spec, rec_out_spec],
        out_shape=[jax.ShapeDtypeStruct((m, GROUP_W), BF16)] * 2
                  + [jax.ShapeDtypeStruct((m // SSM_CHUNK, SSM_CHUNK, GROUP_W), BF16)] * 2,
        scratch_shapes=nsa_scratch + diff_scratch + rec_scratch,
        compiler_params=_params(("parallel", "arbitrary")),
        name="mixers",
    )(*nsa_in, *diff_in, *ret_in, *ssd_in)
    return y_nsa, y_diff, y_ret.reshape(m, GROUP_W), y_ssm.reshape(m, GROUP_W)


def _ret_tables():
    c = RET_CHUNK
    h = np.arange(HEADS, dtype=np.float32)
    log_g = jnp.log(1.0 - 2.0 ** (-5.0 - jnp.asarray(h)))
    pos = jnp.arange(c, dtype=F32)
    rel = pos[:, None] - pos[None, :]
    decay = jnp.where(rel >= 0, jnp.exp(log_g[:, None, None] * jnp.maximum(rel, 0.0)), 0.0)
    xi = jnp.exp(log_g[:, None] * (pos + 1.0))
    zeta = jnp.exp(log_g[:, None] * (c - 1.0 - pos))
    chunk_decay = jnp.exp(log_g * c)
    xi_tab = jnp.repeat(xi.T, HEAD_DIM, axis=1)
    zeta_tab = jnp.repeat(zeta.T, HEAD_DIM, axis=1)
    cd_tab = jnp.repeat(chunk_decay, HEAD_DIM)[None, :]
    return decay, xi_tab, zeta_tab, cd_tab


def _ret_body(decay_ref, xi_ref, zeta_ref, cd_ref, gn_ref, q_ref, k_ref, v_ref, z_ref, o_ref, st_ref, blocks):
    xi = xi_ref[...]
    cd = cd_ref[...]
    for bb in blocks:
        q = (q_ref[bb].astype(F32) * (HEAD_DIM ** -0.5)).astype(BF16)
        k = k_ref[bb]
        v = v_ref[bb]
        kz_t = (k.astype(F32) * zeta_ref[...]).T.astype(BF16)
        outs = []
        for h in range(HEADS):
            sl = slice(h * HEAD_DIM, (h + 1) * HEAD_DIM)
            qh, kh, vh = q[:, sl], k[:, sl], v[:, sl]
            prev = st_ref[h]
            inner = (_dot_nt(qh, kh) * decay_ref[h]).astype(BF16)
            o = _dot(inner, vh) + _dot(qh, prev.astype(BF16)) * xi[:, sl]
            st_ref[h] = prev * cd[:, sl] + _dot(kz_t[sl, :], vh)
            mu = jnp.mean(o, axis=-1, keepdims=True)
            d = o - mu
            var = jnp.mean(d * d, axis=-1, keepdims=True)
            outs.append(d * lax.rsqrt(var + EPS))
        y = jnp.concatenate(outs, axis=-1) * gn_ref[...] * _silu(z_ref[bb])
        o_ref[bb] = y.astype(o_ref.dtype)


def _ssd_body(cw_ref, cb_ref, dtb_ref, a_ref, dsk_ref, nw_ref, xbc_ref, misc_ref, z_ref, o_ref,
              ext_ref, st_ref, blocks):
    L = SSM_CHUNK
    hi = lax.Precision.HIGHEST
    row = lax.broadcasted_iota(jnp.int32, (L, L), 0)
    col = lax.broadcasted_iota(jnp.int32, (L, L), 1)
    causal = row >= col
    tril = jnp.where(causal, 1.0, 0.0).astype(F32)
    dsk = dsk_ref[...]

    for bb in blocks:
        raw = xbc_ref[bb]
        ext_ref[8:8 + L, :] = raw
        conv = cb_ref[...] + raw * cw_ref[CONV_W - 1:CONV_W, :]
        for w in range(CONV_W - 1):
            shift = CONV_W - 1 - w
            conv = conv + ext_ref[8 - shift:8 - shift + L, :] * cw_ref[w:w + 1, :]
        ext_ref[0:8, :] = raw[L - 8:L, :]
        xc = _silu(conv)
        x = xc[:, 0:GROUP_W]
        bm = xc[:, GROUP_W:GROUP_W + 2 * SSM_STATE].astype(BF16)
        cm = xc[:, GROUP_W + 2 * SSM_STATE:].astype(BF16)

        dt_full = jax.nn.softplus(misc_ref[bb] + dtb_ref[...])
        da = dt_full * a_ref[...]
        cs_col = jnp.dot(tril, da, precision=hi, preferred_element_type=F32)
        cs_row = lax.dot_general(da, tril, (((0,), (1,)), ((), ())), precision=hi,
                                 preferred_element_type=F32)

        outs = []
        for h in range(HEADS):
            g = h // 2
            c0 = DT_COL + h
            sl = slice(h * HEAD_DIM, (h + 1) * HEAD_DIM)
            gs = slice(g * SSM_STATE, (g + 1) * SSM_STATE)
            cs_c = cs_col[:, c0:c0 + 1]
            cs_r = cs_row[c0:c0 + 1, :]
            cs_last = cs_col[L - 1:L, c0:c0 + 1]
            xh = x[:, sl]
            xdt = xh * dt_full[:, c0:c0 + 1]
            seg = jnp.exp(jnp.where(causal, cs_c - cs_r, NEG_INF))
            cb = _dot_nt(cm[:, gs], bm[:, gs])
            y = _dot((cb * seg).astype(BF16), xdt.astype(BF16))
            prev = st_ref[h]
            y = y + _dot(cm[:, gs], prev.astype(BF16)) * jnp.exp(cs_c)
            y = y + dsk[:, sl] * xh
            dec = jnp.exp(cs_last - cs_c)
            st_ref[h] = prev * jnp.exp(cs_last) + _dot_tn(bm[:, gs], (xdt * dec).astype(BF16))
            outs.append(y)
        y = jnp.concatenate(outs, axis=-1) * _silu(z_ref[bb])
        ms = jnp.mean(y * y, axis=-1, keepdims=True)
        o_ref[bb] = (y * lax.rsqrt(ms + EPS) * nw_ref[...]).astype(o_ref.dtype)


def _recurrent_operands(ret_qkv, xbc, misc, z_all, gn_w, conv_w, conv_b, dt_bias, a_log, d_skip, norm_w, nq, tq):
    L = SSM_CHUNK
    assert RET_CHUNK == L and tq % L == 0
    cpt = tq // L
    decay, xi_tab, zeta_tab, cd_tab = _ret_tables()
    dtb = jnp.zeros((1, 128), F32).at[0, DT_COL:DT_COL + HEADS].set(dt_bias)
    a_full = jnp.zeros((1, 128), F32).at[0, DT_COL:DT_COL + HEADS].set(-jnp.exp(a_log))
    dsk = jnp.repeat(d_skip, HEAD_DIM)[None, :]

    def full(shape):
        return pl.BlockSpec(shape, lambda b, i: (0,) * len(shape))

    def blk(width, col):
        return pl.BlockSpec((cpt, L, width), lambda b, i: (b * nq + i, 0, col))

    def chunked(a):
        return a.reshape(-1, L, a.shape[-1])

    qkv3, z3 = chunked(ret_qkv), chunked(z_all)
    ret_in = [decay, xi_tab, zeta_tab, cd_tab, gn_w.reshape(1, GROUP_W), qkv3, qkv3, qkv3, z3]
    ret_specs = [full((HEADS, L, L)), full((L, GROUP_W)), full((L, GROUP_W)), full((1, GROUP_W)),
                 full((1, GROUP_W)), blk(GROUP_W, 0), blk(GROUP_W, 1), blk(GROUP_W, 2), blk(GROUP_W, 2)]
    ssd_in = [conv_w, conv_b.reshape(1, CONV_CH), dtb, a_full, dsk, norm_w.reshape(1, GROUP_W),
              chunked(xbc), chunked(misc), z3]
    ssd_specs = [full((CONV_W, CONV_CH)), full((1, CONV_CH)), full((1, 128)), full((1, 128)),
                 full((1, GROUP_W)), full((1, GROUP_W)), blk(CONV_CH, 0), blk(128, 0), blk(GROUP_W, 3)]
    scratch = [pltpu.VMEM((HEADS, HEAD_DIM, HEAD_DIM), F32),
               pltpu.VMEM((8 + L, CONV_CH), F32),
               pltpu.VMEM((HEADS, SSM_STATE, HEAD_DIM), F32)]
    return ret_in, ssd_in, ret_specs + ssd_specs, blk(GROUP_W, 0), scratch


def _pick_tile(n, pref):
    t = pref
    while n % t:
        t //= 2
    return t


def kernel(x, norm_w, w_in, w_out, nsa_pe_k, nsa_pe_v, nsa_w_ck1, nsa_w_ck2, nsa_w_cv1, nsa_w_cv2,
           diff_lam_q1, diff_lam_k1, diff_lam_q2, diff_lam_k2, diff_subln_w, ret_gn_w,
           ssm_conv_w, ssm_conv_b, ssm_dt_bias, ssm_A_log, ssm_D, ssm_norm_w, final_norm_w):
    batch, seq, _ = x.shape
    depth = w_in.shape[0]
    m = batch * seq
    tm = _pick_tile(m, 512)
    tq = _pick_tile(seq, 512)
    tk = _pick_tile(seq, 256)
    tk_diff = _pick_tile(seq, 512)
    w_in_b = w_in.astype(BF16)
    w_r = _relayout_w_in(w_in_b)
    w_t = _relayout_w_in_t(w_in_b)
    w_out_b = w_out.astype(BF16)
    x2d = x.reshape(m, D_MODEL)
    projected = _in_proj(x2d, norm_w[0], w_r[0], w_t[0], tm)
    for i in range(depth):
        nsa_q, nsa_k2, nsa_cmp, misc, z_all, diff_qk, ret_qkv, xbc, vt_all = projected
        kc, vct = _nsa_compress(nsa_cmp, nsa_pe_k[i], nsa_pe_v[i], nsa_w_ck1[i], nsa_w_ck2[i],
                                nsa_w_cv1[i], nsa_w_cv2[i], batch, seq)
        lam_p = jnp.stack([diff_lam_q1[i], diff_lam_k1[i], diff_lam_q2[i], diff_lam_k2[i]])
        ys = _mixers((nsa_q, nsa_k2, vt_all, kc, vct, misc, z_all),
                     (diff_qk, vt_all, z_all, lam_p, diff_subln_w[i]),
                     (ret_qkv, xbc, misc, z_all, ret_gn_w[i], ssm_conv_w[i], ssm_conv_b[i], ssm_dt_bias[i],
                      ssm_A_log[i], ssm_D[i], ssm_norm_w[i]),
                     batch, seq, i, tq, tk, tk_diff)
        if i + 1 < depth:
            x2d, *projected = _out_in_proj(ys, w_out_b[i], x2d, norm_w[i + 1], w_r[i + 1], w_t[i + 1], tm)
        else:
            x2d = _out_proj(ys, w_out_b[i], x2d, final_norm_w, tm)
    return x2d.reshape(batch, seq, D_MODEL)
```

```python
import functools
import math

import numpy as np
import jax
import jax.numpy as jnp
from jax import lax
from jax.experimental import pallas as pl
from jax.experimental.pallas import tpu as pltpu

F32 = jnp.float32
BF16 = jnp.bfloat16
NEG_INF = float("-inf")
LOG2E = 1.4426950408889634

D_MODEL = 1024
DEPTH = 4
GROUP_W = 256
HEADS = 4
HEAD_DIM = 64
EPS = 1e-6
CMP_LEN = 32
CMP_STRIDE = 16
CMP_HIDDEN = 256
SLC_BLOCK = 64
SLC_SHIFT = 6
SLC_TOPK = 16
WINDOW = 512
DIFF_QK_DIM = 32
RET_CHUNK = 128
SSM_STATE = 128
SSM_CHUNK = 128
CONV_W = 4
CONV_CH = 768
N_ALIBI_HEADS = 8
LANES = 128
BF16_ROWS = 16
ACC_ROWS = HEAD_DIM + BF16_ROWS
QK_LOOKAHEAD = 3
ALIBI_ROWS = 3
POS_RADIX = 256
MASK_BIG = 2.0 ** 100
BLK_LANE0 = 64

IN_LAYOUT = (
    ("nsa_q", 256), ("nsa_k_cmp", 64), ("nsa_v_cmp", 64), ("nsa_k_slc", 64), ("nsa_v_slc", 64),
    ("nsa_k_win", 64), ("nsa_v_win", 64), ("nsa_gate", 12), ("nsa_z", 256),
    ("diff_q", 256), ("diff_k", 256), ("diff_v", 256), ("diff_z", 256),
    ("ret_q", 256), ("ret_k", 256), ("ret_v", 256), ("ret_z", 256),
    ("ssm_z", 256), ("ssm_xbc", 768), ("ssm_dt", 4),
)
IN_OFF = {}
_o = 0
for _n, _w in IN_LAYOUT:
    IN_OFF[_n] = (_o, _w)
    _o += _w
IN_W = _o

GATE_COL = 0
DT_COL = 12
IN_OUTPUTS = (
    ("nsa_q", BF16, ("nsa_q",), 256),
    ("nsa_k2", BF16, ("nsa_k_slc", "nsa_k_win"), 128),
    ("nsa_cmp", F32, ("nsa_k_cmp", "nsa_v_cmp"), 128),
    ("misc", F32, ("nsa_gate", "ssm_dt"), 128),
    ("z_all", F32, ("nsa_z", "diff_z", "ret_z", "ssm_z"), 1024),
    ("diff_qk", BF16, ("diff_q", "diff_k"), 512),
    ("ret_qkv", BF16, ("ret_q", "ret_k", "ret_v"), 768),
    ("xbc", F32, ("ssm_xbc",), 768),
)
IN_T_SRC = ("diff_v", "nsa_v_slc", "nsa_v_win")
IN_T_ROWS = 384
DIFF_VT_BLK = 0
NSA_VT_BLK = 2
IN_SEGS = []
_o = 0
for _n, _dt, _src, _w in IN_OUTPUTS:
    IN_SEGS.append((_o, _o + _w))
    _o += _w
IN_WP = _o

VMEM_LIMIT = 56 * 1024 * 1024


def _alibi_slopes():
    return [2.0 ** (-8.0 * (i + 1) / N_ALIBI_HEADS) for i in range(N_ALIBI_HEADS)]


NSA_SLOPES = _alibi_slopes()[0::2]
DIFF_SLOPES = _alibi_slopes()[1::2]


def _silu(x):
    return x * jax.nn.sigmoid(x)


def _dot(a, b):
    return jnp.dot(a, b, preferred_element_type=F32)


def _dot_nt(a, b):
    return lax.dot_general(a, b, (((1,), (1,)), ((), ())), preferred_element_type=F32)


def _dot_tn(a, b):
    return lax.dot_general(a, b, (((0,), (0,)), ((), ())), preferred_element_type=F32)


def _params(sem):
    return pltpu.CompilerParams(dimension_semantics=sem, vmem_limit_bytes=VMEM_LIMIT)


def _relayout_w_in(w_in):
    cols = []
    for _n, _dt, src, width in IN_OUTPUTS:
        used = 0
        for s in src:
            off, w = IN_OFF[s]
            cols.append(w_in[:, :, off:off + w])
            used += w
        if used < width:
            cols.append(jnp.zeros(w_in.shape[:2] + (width - used,), w_in.dtype))
    return jnp.concatenate(cols, axis=-1)


def _relayout_w_in_t(w_in):
    cols = [w_in[:, :, IN_OFF[s][0]:IN_OFF[s][0] + IN_OFF[s][1]] for s in IN_T_SRC]
    return jnp.swapaxes(jnp.concatenate(cols, axis=-1), 1, 2)


def _norm_project(x, nw_ref, w_ref, wt_ref, out_refs):
    ms = jnp.mean(x * x, axis=-1, keepdims=True)
    h = (x * lax.rsqrt(ms + EPS) * nw_ref[...]).astype(BF16)
    for ref, (a, b) in zip(out_refs[:-1], IN_SEGS):
        ref[...] = _dot(h, w_ref[:, a:b]).astype(ref.dtype)
    out_refs[-1][...] = _dot_nt(wt_ref[...], h).astype(BF16)


def _in_proj_kernel(x_ref, nw_ref, w_ref, wt_ref, *out_refs):
    _norm_project(x_ref[...], nw_ref, w_ref, wt_ref, out_refs)


def _in_proj_outputs(m, tm):
    out_shape = [jax.ShapeDtypeStruct((m, b - a), dt) for (_n, dt, _s, _w), (a, b) in zip(IN_OUTPUTS, IN_SEGS)]
    out_specs = [pl.BlockSpec((tm, b - a), lambda i: (i, 0)) for (a, b) in IN_SEGS]
    out_shape.append(jax.ShapeDtypeStruct((IN_T_ROWS, m), BF16))
    out_specs.append(pl.BlockSpec((IN_T_ROWS, tm), lambda i: (0, i)))
    return out_shape, out_specs


def _in_proj(x2d, norm_w, w_r, w_t, tm):
    m = x2d.shape[0]
    out_shape, out_specs = _in_proj_outputs(m, tm)
    return pl.pallas_call(
        _in_proj_kernel,
        grid=(m // tm,),
        in_specs=[pl.BlockSpec((tm, D_MODEL), lambda i: (i, 0)),
                  pl.BlockSpec((1, D_MODEL), lambda i: (0, 0)),
                  pl.BlockSpec((D_MODEL, IN_WP), lambda i: (0, 0)),
                  pl.BlockSpec((IN_T_ROWS, D_MODEL), lambda i: (0, 0))],
        out_specs=out_specs,
        out_shape=out_shape,
        compiler_params=_params(("parallel",)),
        name="in_proj",
    )(x2d, norm_w.reshape(1, D_MODEL), w_r, w_t)


def _out_proj_kernel(y0_ref, y1_ref, y2_ref, y3_ref, w_ref, x_ref, fw_ref, o_ref):
    acc = x_ref[...]
    for g, y_ref in enumerate((y0_ref, y1_ref, y2_ref, y3_ref)):
        acc = acc + _dot(y_ref[...], w_ref[g * GROUP_W:(g + 1) * GROUP_W, :])
    ms = jnp.mean(acc * acc, axis=-1, keepdims=True)
    o_ref[...] = acc * lax.rsqrt(ms + EPS) * fw_ref[...]


def _out_proj(ys, w_out_b, x2d, final_w, tm):
    m = x2d.shape[0]
    yspec = pl.BlockSpec((tm, GROUP_W), lambda i: (i, 0))
    return pl.pallas_call(
        _out_proj_kernel,
        grid=(m // tm,),
        in_specs=[yspec, yspec, yspec, yspec,
                  pl.BlockSpec((D_MODEL, D_MODEL), lambda i: (0, 0)),
                  pl.BlockSpec((tm, D_MODEL), lambda i: (i, 0)),
                  pl.BlockSpec((1, D_MODEL), lambda i: (0, 0))],
        out_specs=pl.BlockSpec((tm, D_MODEL), lambda i: (i, 0)),
        out_shape=jax.ShapeDtypeStruct((m, D_MODEL), F32),
        compiler_params=_params(("parallel",)),
        name="out_proj",
    )(*ys, w_out_b, x2d, final_w.reshape(1, D_MODEL))


def _out_in_proj_kernel(y0_ref, y1_ref, y2_ref, y3_ref, wo_ref, x_ref, nw_ref, w_ref, wt_ref, xo_ref, *out_refs):
    acc = x_ref[...]
    for g, y_ref in enumerate((y0_ref, y1_ref, y2_ref, y3_ref)):
        acc = acc + _dot(y_ref[...], wo_ref[g * GROUP_W:(g + 1) * GROUP_W, :])
    xo_ref[...] = acc
    _norm_project(acc, nw_ref, w_ref, wt_ref, out_refs)


def _out_in_proj(ys, w_out_b, x2d, norm_w, w_r, w_t, tm):
    m = x2d.shape[0]
    yspec = pl.BlockSpec((tm, GROUP_W), lambda i: (i, 0))
    xspec = pl.BlockSpec((tm, D_MODEL), lambda i: (i, 0))
    out_shape, out_specs = _in_proj_outputs(m, tm)
    return pl.pallas_call(
        _out_in_proj_kernel,
        grid=(m // tm,),
        in_specs=[yspec, yspec, yspec, yspec,
                  pl.BlockSpec((D_MODEL, D_MODEL), lambda i: (0, 0)),
                  xspec,
                  pl.BlockSpec((1, D_MODEL), lambda i: (0, 0)),
                  pl.BlockSpec((D_MODEL, IN_WP), lambda i: (0, 0)),
                  pl.BlockSpec((IN_T_ROWS, D_MODEL), lambda i: (0, 0))],
        out_specs=[xspec] + out_specs,
        out_shape=[jax.ShapeDtypeStruct((m, D_MODEL), F32)] + out_shape,
        compiler_params=_params(("parallel",)),
        name="out_in_proj",
    )(*ys, w_out_b, x2d, norm_w.reshape(1, D_MODEL), w_r, w_t)


def _stage_scores(s_ref, slot, cols, s, keep=None):
    if keep is not None:
        s = jnp.where(keep, s, NEG_INF)
    s_ref[slot, :, cols] = s
    return jnp.max(s, axis=0, keepdims=True)


def _flash_update_t(s, s_max, c1, shift, vt_ext, m_ref, acc_ref, idx, cols):
    m_old = m_ref[idx, :, cols]
    m_new = jnp.maximum(m_old, c1 * s_max + shift)
    alpha = jnp.exp2(m_old - m_new)
    p = jnp.exp2(c1 * s() - (m_new - shift))
    acc_ref[idx, :, cols] = alpha * acc_ref[idx, :, cols] + _dot(vt_ext, p.astype(BF16))
    m_ref[idx, :, cols] = m_new


def _bf16_pieces(x, n=3):
    out = []
    for _ in range(n):
        p = float(np.asarray(x, np.float32).astype(BF16).astype(np.float32))
        out.append(p)
        x = x - p
    return out


def _key_pos_features(tk):
    lane = lax.broadcasted_iota(jnp.int32, (tk, LANES), 1)
    row = lax.broadcasted_iota(jnp.int32, (tk, LANES), 0)
    out = jnp.zeros((tk, LANES), jnp.int32)
    for d in range(-(-tk // POS_RADIX)):
        digit = jnp.bitwise_and(jnp.right_shift(row, 8 * d), POS_RADIX - 1)
        out = jnp.where((lane >= ALIBI_ROWS * d) & (lane < ALIBI_ROWS * (d + 1)), digit, out)
    return out.astype(F32).astype(BF16)


def _alibi_rows(beta, tq, rows=LANES, tk=POS_RADIX):
    row = lax.broadcasted_iota(jnp.int32, (rows, tq), 0)
    out = jnp.zeros((rows, tq), F32)
    for d in range(-(-tk // POS_RADIX)):
        for r, piece in enumerate(_bf16_pieces(beta * POS_RADIX ** d, ALIBI_ROWS)):
            out = jnp.where(row == ALIBI_ROWS * d + r, piece, out)
    return out.astype(BF16)


def _normalized(acc):
    return acc[0:HEAD_DIM] / jnp.maximum(acc[HEAD_DIM:HEAD_DIM + 1], 1e-30)


def _rel_t(tk, tq):
    return lax.broadcasted_iota(jnp.int32, (tk, tq), 1) - lax.broadcasted_iota(jnp.int32, (tk, tq), 0)


def _diff_phases(lam_ref, sw_ref, q_ref, k_ref, vt_ref, z_ref, o_ref, m_ref, acc_ref, wq_ref, s_ref,
                 *, tq, tk, lam_init):
    qi = pl.program_id(1)
    kpq = tq // tk
    scale = DIFF_QK_DIM ** -0.5
    c1 = scale * LOG2E
    m_ref[...] = jnp.full(m_ref.shape, NEG_INF, F32)
    acc_ref[...] = jnp.zeros(acc_ref.shape, F32)
    rel = _rel_t(tk, tq)

    qt = q_ref[...].astype(F32).T.astype(BF16)
    row = lax.broadcasted_iota(jnp.int32, (LANES, tq), 0)
    for j in range(2 * HEADS):
        g, r0 = divmod(j * DIFF_QK_DIM, LANES)
        qg = qt[g * LANES:(g + 1) * LANES]
        wq_ref[j, 0:LANES, :] = jnp.where((row >= r0) & (row < r0 + DIFF_QK_DIM), qg, jnp.zeros_like(qg))
        wq_ref[j, LANES:2 * LANES, :] = _alibi_rows(DIFF_SLOPES[j // 2] / scale, tq, tk=tk)

    ones = jnp.ones((BF16_ROWS, tk), BF16)
    kpos = _key_pos_features(tk)

    n_maps = 2 * HEADS
    n_slots = s_ref.shape[0]

    def tiles(kis, mask_offs):
        loaded = []
        for ki in kis:
            start = pl.multiple_of(ki * tk, tk)
            loaded.append((k_ref[pl.ds(start, tk), :],
                           vt_ref[:, pl.ds(start, tk)],
                           (ki * tk).astype(F32)))
        items = [(t, j) for t in range(len(kis)) for j in range(n_maps)]
        cols = slice(0, tq)

        def scores(i):
            t, j = items[i]
            g = (j * DIFF_QK_DIM) // LANES
            lhs = jnp.concatenate([loaded[t][0][:, g * LANES:(g + 1) * LANES], kpos], axis=1)
            s_ref[i % n_slots] = _dot(lhs, wq_ref[j])

        def staged(i):
            t = items[i][0]
            if mask_offs[t] is None:
                return lambda: s_ref[i % n_slots]
            return lambda: jnp.where(rel >= mask_offs[t], s_ref[i % n_slots], NEG_INF)

        for i in range(QK_LOOKAHEAD):
            scores(i)
        for i, (t, j) in enumerate(items):
            h = j // 2
            if i + QK_LOOKAHEAD < len(items):
                scores(i + QK_LOOKAHEAD)
            _k, vt, key0 = loaded[t]
            vt_ext = jnp.concatenate([vt[h * HEAD_DIM:(h + 1) * HEAD_DIM], ones], axis=0)
            s = staged(i)
            _flash_update_t(s, jnp.max(s(), axis=0, keepdims=True), c1, (DIFF_SLOPES[h] * LOG2E) * key0,
                            vt_ext, m_ref, acc_ref, j, cols)

    def loop_body(i):
        tiles([i * kpq + d for d in range(kpq)], [None] * kpq)

    def diag_tiles():
        tiles([qi * kpq + d for d in range(kpq)], [d * tk for d in range(kpq)])

    def finish():
        lp = lam_ref[...]
        lam = (jnp.exp(jnp.sum(lp[0:1] * lp[1:2], axis=-1, keepdims=True))
               - jnp.exp(jnp.sum(lp[2:3] * lp[3:4], axis=-1, keepdims=True)) + lam_init)
        sw = sw_ref[...]
        o_t = jnp.concatenate([_normalized(acc_ref[2 * h]) - lam * _normalized(acc_ref[2 * h + 1])
                               for h in range(HEADS)], axis=0)
        o = o_t.T
        outs = []
        for h in range(HEADS):
            oh = o[:, h * HEAD_DIM:(h + 1) * HEAD_DIM]
            ms = jnp.mean(oh * oh, axis=-1, keepdims=True)
            outs.append(oh * lax.rsqrt(ms + EPS) * sw * (1.0 - lam_init))
        y = jnp.concatenate(outs, axis=-1) * _silu(z_ref[...])
        o_ref[...] = y.astype(o_ref.dtype)

    return loop_body, diag_tiles, finish


def _diff_operands(qk, vt_all, z_all, lam_p, subln_w, seq, nq, tq, tk):
    inputs = [lam_p, subln_w.reshape(1, HEAD_DIM), qk, qk, vt_all, z_all]
    in_specs = [pl.BlockSpec((4, DIFF_QK_DIM), lambda b, i: (0, 0)),
                pl.BlockSpec((1, HEAD_DIM), lambda b, i: (0, 0)),
                pl.BlockSpec((tq, GROUP_W), lambda b, i: (b * nq + i, 0)),
                pl.BlockSpec((seq, GROUP_W), lambda b, i: (b, 1)),
                pl.BlockSpec((GROUP_W, seq), lambda b, i: (DIFF_VT_BLK, b)),
                pl.BlockSpec((tq, GROUP_W), lambda b, i: (b * nq + i, 1))]
    scratch = [pltpu.VMEM((2 * HEADS, 1, tq), F32),
               pltpu.VMEM((2 * HEADS, ACC_ROWS, tq), F32),
               pltpu.VMEM((2 * HEADS, 2 * LANES, tq), BF16),
               pltpu.VMEM((QK_LOOKAHEAD + 1, tk, tq), F32)]
    return inputs, in_specs, scratch


def _nsa_compress_kernel(cmp_ref, pe_ref, w1_ref, wk2_ref, wv2t_ref, kc_ref, vct_ref):
    ng = kc_ref.shape[1]
    top = jnp.zeros((ng, 2 * CMP_HIDDEN), F32)
    bot = jnp.zeros((ng, 2 * CMP_HIDDEN), F32)
    for p in range(CMP_STRIDE):
        x = cmp_ref[pl.ds(p, ng, stride=CMP_STRIDE), :]
        top = top + _dot((x + pe_ref[p:p + 1, :]).astype(BF16), w1_ref[p])
        bot = bot + _dot((x + pe_ref[CMP_STRIDE + p:CMP_STRIDE + p + 1, :]).astype(BF16), w1_ref[CMP_STRIDE + p])
    hid = _silu(top + pltpu.roll(bot, ng - 1, 0)).astype(BF16)
    hk, hv = hid[:, 0:CMP_HIDDEN], hid[:, CMP_HIDDEN:]
    kc = _dot(hk, wk2_ref[...])
    lane = lax.broadcasted_iota(jnp.int32, kc.shape, 1)
    blk = lax.broadcasted_iota(jnp.int32, kc.shape, 0).astype(F32)
    kc = jnp.where((lane >= HEAD_DIM) & (lane < HEAD_DIM + ALIBI_ROWS), blk, kc)
    kc_ref[0] = kc.astype(kc_ref.dtype)
    vct_ref[0] = _dot_nt(wv2t_ref[...], hv).astype(vct_ref.dtype)


def _nsa_compress(cmp2d, pe_k, pe_v, w_ck1, w_ck2, w_cv1, w_cv2, batch, seq):
    ng = seq // CMP_STRIDE
    wk = w_ck1.reshape(CMP_LEN, HEAD_DIM, CMP_HIDDEN)
    wv = w_cv1.reshape(CMP_LEN, HEAD_DIM, CMP_HIDDEN)
    w1 = jnp.concatenate([jnp.pad(wk, ((0, 0), (0, 0), (0, CMP_HIDDEN))),
                          jnp.pad(wv, ((0, 0), (0, 0), (CMP_HIDDEN, 0)))], axis=1).astype(BF16)
    pe = jnp.concatenate([pe_k, pe_v], axis=1)

    def full(shape):
        return pl.BlockSpec(shape, lambda b: (0,) * len(shape))

    return pl.pallas_call(
        _nsa_compress_kernel,
        grid=(batch,),
        in_specs=[pl.BlockSpec((seq, LANES), lambda b: (b, 0)), full((CMP_LEN, LANES)),
                  full((CMP_LEN, LANES, 2 * CMP_HIDDEN)), full((CMP_HIDDEN, LANES)),
                  full((HEAD_DIM, CMP_HIDDEN))],
        out_specs=[pl.BlockSpec((1, ng, LANES), lambda b: (b, 0, 0)),
                   pl.BlockSpec((1, HEAD_DIM, ng), lambda b: (b, 0, 0))],
        out_shape=[jax.ShapeDtypeStruct((batch, ng, LANES), BF16),
                   jax.ShapeDtypeStruct((batch, HEAD_DIM, ng), BF16)],
        compiler_params=_params(("parallel",)),
        name="nsa_compress",
    )(cmp2d, pe, w1, jnp.pad(w_ck2, ((0, 0), (0, LANES - HEAD_DIM))).astype(BF16), w_cv2.T.astype(BF16))


def _nsa_phases(ovl_ref, q_ref, k_ref, vt_ref, kc_ref, vct_ref, misc_ref, z_ref, o_ref,
                m_ref, acc_ref, wq_ref, s_ref, sc_ref, *, tq, tk, seq):
    qi = pl.program_id(1)
    kpq = tq // tk
    scale = HEAD_DIM ** -0.5
    c1 = scale * LOG2E
    ng = seq // CMP_STRIDE
    ns = seq // SLC_BLOCK
    top = min(SLC_TOPK, ns)
    m_ref[...] = jnp.full(m_ref.shape, NEG_INF, F32)
    acc_ref[...] = jnp.zeros(acc_ref.shape, F32)
    rel = _rel_t(tk, tq)

    qt = q_ref[...].astype(F32).T.astype(BF16)
    zeros_q = jnp.zeros((HEAD_DIM, tq), BF16)
    for h in range(HEADS):
        qh = qt[h * HEAD_DIM:(h + 1) * HEAD_DIM]
        alibi = _alibi_rows(NSA_SLOPES[h] / scale, tq, tk=tk)
        wq_ref[h, 0:LANES, :] = jnp.concatenate([qh, zeros_q], axis=0)
        wq_ref[h, LANES:2 * LANES, :] = alibi
        wq_ref[HEADS + h, 0:LANES, :] = jnp.concatenate([zeros_q, qh], axis=0)
        wq_ref[HEADS + h, LANES:2 * LANES, :] = alibi

    t_lane = qi * tq + lax.broadcasted_iota(jnp.int32, (1, tq), 1)
    n_sub = lax.broadcasted_iota(jnp.int32, (ng, 1), 0)
    c_valid = (n_sub * CMP_STRIDE + (CMP_LEN - 1)) <= t_lane
    kc = kc_ref[0]
    vct_ext = jnp.concatenate([vct_ref[0], jnp.ones((BF16_ROWS, ng), BF16)], axis=0)
    ovl = ovl_ref[...]
    cmp_max = []
    for h in range(HEADS):
        rhs = jnp.concatenate([qt[h * HEAD_DIM:(h + 1) * HEAD_DIM],
                               _alibi_rows(NSA_SLOPES[h] * CMP_STRIDE / scale, tq, HEAD_DIM)], axis=0)
        cmp_max.append(_stage_scores(sc_ref, h, slice(0, tq), _dot(kc, rhs), c_valid))
    imp = jnp.zeros((ns, tq), F32)
    o_cmp = []
    for h in range(HEADS):
        mx = jnp.where(cmp_max[h] > NEG_INF, cmp_max[h], 0.0)
        pb = jnp.exp2(c1 * sc_ref[h] - c1 * mx).astype(BF16)
        o_ext = _dot(vct_ext, pb)
        r = 1.0 / jnp.maximum(o_ext[HEAD_DIM:HEAD_DIM + 1], 1e-30)
        o_cmp.append(o_ext[0:HEAD_DIM] * r)
        imp = imp + _dot(ovl, pb) * r

    def select_blocks():
        j_sub = lax.broadcasted_iota(jnp.int32, (ns, 1), 0)
        j_sub_f = j_sub.astype(F32)
        cur = jnp.right_shift(t_lane, SLC_SHIFT)
        forced = (j_sub == 0) | (j_sub == cur) | (j_sub == cur - 1)
        valid = (j_sub * SLC_BLOCK) <= t_lane
        score = jnp.where(forced, 1e30, jnp.where(valid, imp, -1.0))
        sel = jnp.zeros((ns, tq), F32)
        for _ in range(top):
            mx = jnp.max(score, axis=0, keepdims=True)
            idx = jnp.min(jnp.where(score == mx, j_sub_f, float(ns)), axis=0, keepdims=True)
            pick = j_sub_f == idx
            sel = jnp.where(pick, 1.0, sel)
            score = jnp.where(pick, -2.0, score)
        unsel = jnp.where(sel > 0.5, 0.0, -MASK_BIG).astype(BF16)
        for h in range(HEADS):
            wq_ref[h, LANES + BLK_LANE0:LANES + BLK_LANE0 + ns, :] = unsel

    aux_lane = lax.broadcasted_iota(jnp.int32, (tk, LANES), 1)
    aux_blk = jnp.right_shift(lax.broadcasted_iota(jnp.int32, (tk, LANES), 0), SLC_SHIFT) + BLK_LANE0
    kpos = _key_pos_features(tk)
    ones = jnp.ones((BF16_ROWS, tk), BF16)

    n_slots = s_ref.shape[0]

    def tiles(window, kis, cols, mask_offs=None):
        base = HEADS if window else 0
        loaded = []
        for t, ki in enumerate(kis):
            start = pl.multiple_of(ki * tk, tk)
            k = k_ref[pl.ds(start, tk), :]
            if window:
                vt = vt_ref[HEAD_DIM:2 * HEAD_DIM, pl.ds(start, tk)]
                aux = kpos
                dist = rel[:, cols[t]] + (qi * tq - ki * tk)
                keep = (dist >= 0) & (dist < WINDOW)
            else:
                vt = vt_ref[0:HEAD_DIM, pl.ds(start, tk)]
                onehot = aux_lane == aux_blk + ki * (tk // SLC_BLOCK)
                aux = jnp.where(onehot, jnp.ones_like(kpos), kpos)
                keep = None
            loaded.append((jnp.concatenate([k, aux], axis=1),
                           jnp.concatenate([vt, ones], axis=0), (ki * tk).astype(F32), keep))
        items = [(t, h) for t in range(len(kis)) for h in range(HEADS)]
        s_max = {}

        def scores(i):
            t, h = items[i]
            if window:
                keep = loaded[t][3]
            else:
                keep = None if mask_offs[t] is None else rel[:, cols[t]] >= mask_offs[t]
            s_max[i] = _stage_scores(s_ref, i % n_slots, cols[t],
                                     _dot(loaded[t][0], wq_ref[base + h, :, cols[t]]), keep)

        for i in range(QK_LOOKAHEAD):
            scores(i)
        for i, (t, h) in enumerate(items):
            if i + QK_LOOKAHEAD < len(items):
                scores(i + QK_LOOKAHEAD)
            _lhs, vt_ext, key0, _keep = loaded[t]
            _flash_update_t(lambda i=i, t=t: s_ref[i % n_slots, :, cols[t]], s_max.pop(i), c1,
                            (NSA_SLOPES[h] * LOG2E) * key0, vt_ext, m_ref, acc_ref, base + h, cols[t])

    all_cols = slice(0, tq)
    diag_cols = [slice(d * tk, tq) for d in range(kpq)]

    tiles(True, [qi * kpq + d for d in range(kpq)], diag_cols)
    select_blocks()

    def loop_body(i):
        tiles(False, [i * kpq + d for d in range(kpq)], [all_cols] * kpq, [None] * kpq)

    def back_window():
        n_back = (WINDOW + tk - 1) // tk
        for g in range((n_back + kpq - 1) // kpq):
            backs = list(range(g * kpq + 1, min((g + 1) * kpq, n_back) + 1))
            back_cols = [slice(0, min(tq, -(-(WINDOW - (back - 1) * tk - 1) // LANES) * LANES)) for back in backs]

            @pl.when(qi > g)
            def _():
                tiles(True, [qi * kpq - back for back in backs], back_cols)

    def diag_tiles():
        tiles(False, [qi * kpq + d for d in range(kpq)], diag_cols, [d * tk for d in range(kpq)])

    def finish():
        g_t = jax.nn.sigmoid(misc_ref[...]).T
        outs = []
        for h in range(HEADS):
            r0 = GATE_COL + 3 * h
            outs.append(g_t[r0:r0 + 1] * o_cmp[h] + g_t[r0 + 1:r0 + 2] * _normalized(acc_ref[h])
                        + g_t[r0 + 2:r0 + 3] * _normalized(acc_ref[HEADS + h]))
        y = jnp.concatenate(outs, axis=0).T * _silu(z_ref[...])
        o_ref[...] = y.astype(o_ref.dtype)

    return loop_body, back_window, diag_tiles, finish


def _overlap_t(seq):
    nc = (seq - CMP_LEN) // CMP_STRIDE + 1
    ng = seq // CMP_STRIDE
    ns = seq // SLC_BLOCK
    c_start = np.arange(ng) * CMP_STRIDE
    c_end = c_start + CMP_LEN - 1
    s_start = np.arange(ns) * SLC_BLOCK
    s_end = s_start + SLC_BLOCK - 1
    ov = (c_start[None, :] <= s_end[:, None]) & (c_end[None, :] >= s_start[:, None]) & (np.arange(ng)[None, :] < nc)
    return jnp.asarray(ov.astype(np.float32), dtype=BF16)


def _nsa_operands(q, k2, vt_all, kc, vct, misc, z_all, seq, nq, tq, tk):
    ng = seq // CMP_STRIDE
    ns = seq // SLC_BLOCK
    assert tq - tk < WINDOW and tk % SLC_BLOCK == 0 and ns <= LANES - BLK_LANE0
    inputs = [_overlap_t(seq), q, k2, vt_all, kc, vct, misc, z_all]
    in_specs = [pl.BlockSpec((ns, ng), lambda b, i: (0, 0)),
                pl.BlockSpec((tq, GROUP_W), lambda b, i: (b * nq + i, 0)),
                pl.BlockSpec((seq, LANES), lambda b, i: (b, 0)),
                pl.BlockSpec((2 * HEAD_DIM, seq), lambda b, i: (NSA_VT_BLK, b)),
                pl.BlockSpec((1, ng, LANES), lambda b, i: (b, 0, 0)),
                pl.BlockSpec((1, HEAD_DIM, ng), lambda b, i: (b, 0, 0)),
                pl.BlockSpec((tq, LANES), lambda b, i: (b * nq + i, 0)),
                pl.BlockSpec((tq, GROUP_W), lambda b, i: (b * nq + i, 0))]
    scratch = [pltpu.VMEM((2 * HEADS, 1, tq), F32),
               pltpu.VMEM((2 * HEADS, ACC_ROWS, tq), F32),
               pltpu.VMEM((2 * HEADS, 2 * LANES, tq), BF16),
               pltpu.VMEM((2 * HEADS, tk, tq), F32),
               pltpu.VMEM((HEADS, ng, tq), F32)]
    return inputs, in_specs, scratch


def _mixers_kernel(*refs, n_in, n_scratch, tq, tk_nsa, tk_diff, seq, lam_init):
    bounds = np.cumsum([0] + list(n_in))
    nsa_in, diff_in, ret_in, ssd_in = (refs[a:b] for a, b in zip(bounds[:-1], bounds[1:]))
    nsa_out, diff_out, ret_out, ssd_out = refs[bounds[-1]:bounds[-1] + 4]
    sb = np.cumsum([0] + list(n_scratch)) + bounds[-1] + 4
    nsa_scr, diff_scr, rec_scr = (refs[a:b] for a, b in zip(sb[:-1], sb[1:]))
    ret_st, ssd_ext, ssd_st = rec_scr
    qi = pl.program_id(1)

    @pl.when(qi == 0)
    def _():
        ret_st[...] = jnp.zeros(ret_st.shape, F32)
        ssd_st[...] = jnp.zeros(ssd_st.shape, F32)
        ssd_ext[0:8, :] = jnp.zeros((8, CONV_CH), F32)

    chunks = list(range(ret_out.shape[0]))
    first, second = chunks[:len(chunks) // 2], chunks[len(chunks) // 2:]
    nsa_loop, nsa_back_window, nsa_diag, nsa_finish = _nsa_phases(
        *nsa_in, nsa_out, *nsa_scr, tq=tq, tk=tk_nsa, seq=seq)
    diff_loop, diff_diag, diff_finish = _diff_phases(
        *diff_in, diff_out, *diff_scr, tq=tq, tk=tk_diff, lam_init=lam_init)
    _ret_body(*ret_in, ret_out, ret_st, first)
    _ssd_body(*ssd_in, ssd_out, ssd_ext, ssd_st, first)

    def body(i, carry):
        diff_loop(i)
        nsa_loop(i)
        return carry

    lax.fori_loop(0, qi, body, 0)
    nsa_back_window()
    nsa_diag()
    diff_diag()
    _ret_body(*ret_in, ret_out, ret_st, second)
    _ssd_body(*ssd_in, ssd_out, ssd_ext, ssd_st, second)
    nsa_finish()
    diff_finish()


def _mixers(nsa_args, diff_args, rec_args, batch, seq, layer_idx, tq, tk_nsa, tk_diff):
    nq = seq // tq
    nsa_in, nsa_specs, nsa_scratch = _nsa_operands(*nsa_args, seq, nq, tq, tk_nsa)
    diff_in, diff_specs, diff_scratch = _diff_operands(*diff_args, seq, nq, tq, tk_diff)
    ret_in, ssd_in, rec_specs, rec_out_spec, rec_scratch = _recurrent_operands(*rec_args, nq, tq)
    out_spec = pl.BlockSpec((tq, GROUP_W), lambda b, i: (b * nq + i, 0))
    m = batch * seq
    y_nsa, y_diff, y_ret, y_ssm = pl.pallas_call(
        functools.partial(_mixers_kernel, n_in=(len(nsa_in), len(diff_in), len(ret_in), len(ssd_in)),
                          n_scratch=(len(nsa_scratch), len(diff_scratch), len(rec_scratch)),
                          tq=tq, tk_nsa=tk_nsa, tk_diff=tk_diff, seq=seq,
                          lam_init=0.8 - 0.6 * math.exp(-0.3 * layer_idx)),
        grid=(batch, nq),
        in_specs=nsa_specs + diff_specs + rec_specs,
        out_specs=[out_spec, out_spec, rec_out_spec, rec_out_spec],
        out_shape=[jax.ShapeDtypeStruct((m, GROUP_W), BF16)] * 2
                  + [jax.ShapeDtypeStruct((m // SSM_CHUNK, SSM_CHUNK, GROUP_W), BF16)] * 2,
        scratch_shapes=nsa_scratch + diff_scratch + rec_scratch,
        compiler_params=_params(("parallel", "arbitrary")),
        name="mixers",
    )(*nsa_in, *diff_in, *ret_in, *ssd_in)
    return y_nsa, y_diff, y_ret.reshape(m, GROUP_W), y_ssm.reshape(m, GROUP_W)


def _ret_tables():
    c = RET_CHUNK
    h = np.arange(HEADS, dtype=np.float32)
    log_g = jnp.log(1.0 - 2.0 ** (-5.0 - jnp.asarray(h)))
    pos = jnp.arange(c, dtype=F32)
    rel = pos[:, None] - pos[None, :]
    decay = jnp.where(rel >= 0, jnp.exp(log_g[:, None, None] * jnp.maximum(rel, 0.0)), 0.0)
    xi = jnp.exp(log_g[:, None] * (pos + 1.0))
    zeta = jnp.exp(log_g[:, None] * (c - 1.0 - pos))
    chunk_decay = jnp.exp(log_g * c)
    xi_tab = jnp.repeat(xi.T, HEAD_DIM, axis=1)
    zeta_tab = jnp.repeat(zeta.T, HEAD_DIM, axis=1)
    cd_tab = jnp.repeat(chunk_decay, HEAD_DIM)[None, :]
    return decay, xi_tab, zeta_tab, cd_tab


def _ret_body(decay_ref, xi_ref, zeta_ref, cd_ref, gn_ref, q_ref, k_ref, v_ref, z_ref, o_ref, st_ref, blocks):
    xi = xi_ref[...]
    cd = cd_ref[...]
    for bb in blocks:
        q = (q_ref[bb].astype(F32) * (HEAD_DIM ** -0.5)).astype(BF16)
        k = k_ref[bb]
        v = v_ref[bb]
        kz_t = (k.astype(F32) * zeta_ref[...]).T.astype(BF16)
        outs = []
        for h in range(HEADS):
            sl = slice(h * HEAD_DIM, (h + 1) * HEAD_DIM)
            qh, kh, vh = q[:, sl], k[:, sl], v[:, sl]
            prev = st_ref[h]
            inner = (_dot_nt(qh, kh) * decay_ref[h]).astype(BF16)
            o = _dot(inner, vh) + _dot(qh, prev.astype(BF16)) * xi[:, sl]
            st_ref[h] = prev * cd[:, sl] + _dot(kz_t[sl, :], vh)
            mu = jnp.mean(o, axis=-1, keepdims=True)
            d = o - mu
            var = jnp.mean(d * d, axis=-1, keepdims=True)
            outs.append(d * lax.rsqrt(var + EPS))
        y = jnp.concatenate(outs, axis=-1) * gn_ref[...] * _silu(z_ref[bb])
        o_ref[bb] = y.astype(o_ref.dtype)


def _ssd_body(cw_ref, cb_ref, dtb_ref, a_ref, dsk_ref, nw_ref, xbc_ref, misc_ref, z_ref, o_ref,
              ext_ref, st_ref, blocks):
    L = SSM_CHUNK
    hi = lax.Precision.HIGHEST
    row = lax.broadcasted_iota(jnp.int32, (L, L), 0)
    col = lax.broadcasted_iota(jnp.int32, (L, L), 1)
    causal = row >= col
    tril = jnp.where(causal, 1.0, 0.0).astype(F32)
    dsk = dsk_ref[...]

    for bb in blocks:
        raw = xbc_ref[bb]
        ext_ref[8:8 + L, :] = raw
        conv = cb_ref[...] + raw * cw_ref[CONV_W - 1:CONV_W, :]
        for w in range(CONV_W - 1):
            shift = CONV_W - 1 - w
            conv = conv + ext_ref[8 - shift:8 - shift + L, :] * cw_ref[w:w + 1, :]
        ext_ref[0:8, :] = raw[L - 8:L, :]
        xc = _silu(conv)
        x = xc[:, 0:GROUP_W]
        bm = xc[:, GROUP_W:GROUP_W + 2 * SSM_STATE].astype(BF16)
        cm = xc[:, GROUP_W + 2 * SSM_STATE:].astype(BF16)

        dt_full = jax.nn.softplus(misc_ref[bb] + dtb_ref[...])
        da = dt_full * a_ref[...]
        cs_col = jnp.dot(tril, da, precision=hi, preferred_element_type=F32)
        cs_row = lax.dot_general(da, tril, (((0,), (1,)), ((), ())), precision=hi,
                                 preferred_element_type=F32)

        outs = []
        for h in range(HEADS):
            g = h // 2
            c0 = DT_COL + h
            sl = slice(h * HEAD_DIM, (h + 1) * HEAD_DIM)
            gs = slice(g * SSM_STATE, (g + 1) * SSM_STATE)
            cs_c = cs_col[:, c0:c0 + 1]
            cs_r = cs_row[c0:c0 + 1, :]
            cs_last = cs_col[L - 1:L, c0:c0 + 1]
            xh = x[:, sl]
            xdt = xh * dt_full[:, c0:c0 + 1]
            seg = jnp.exp(jnp.where(causal, cs_c - cs_r, NEG_INF))
            cb = _dot_nt(cm[:, gs], bm[:, gs])
            y = _dot((cb * seg).astype(BF16), xdt.astype(BF16))
            prev = st_ref[h]
            y = y + _dot(cm[:, gs], prev.astype(BF16)) * jnp.exp(cs_c)
            y = y + dsk[:, sl] * xh
            dec = jnp.exp(cs_last - cs_c)
            st_ref[h] = prev * jnp.exp(cs_last) + _dot_tn(bm[:, gs], (xdt * dec).astype(BF16))
            outs.append(y)
        y = jnp.concatenate(outs, axis=-1) * _silu(z_ref[bb])
        ms = jnp.mean(y * y, axis=-1, keepdims=True)
        o_ref[bb] = (y * lax.rsqrt(ms + EPS) * nw_ref[...]).astype(o_ref.dtype)


def _recurrent_operands(ret_qkv, xbc, misc, z_all, gn_w, conv_w, conv_b, dt_bias, a_log, d_skip, norm_w, nq, tq):
    L = SSM_CHUNK
    assert RET_CHUNK == L and tq % L == 0
    cpt = tq // L
    decay, xi_tab, zeta_tab, cd_tab = _ret_tables()
    dtb = jnp.zeros((1, 128), F32).at[0, DT_COL:DT_COL + HEADS].set(dt_bias)
    a_full = jnp.zeros((1, 128), F32).at[0, DT_COL:DT_COL + HEADS].set(-jnp.exp(a_log))
    dsk = jnp.repeat(d_skip, HEAD_DIM)[None, :]

    def full(shape):
        return pl.BlockSpec(shape, lambda b, i: (0,) * len(shape))

    def blk(width, col):
        return pl.BlockSpec((cpt, L, width), lambda b, i: (b * nq + i, 0, col))

    def chunked(a):
        return a.reshape(-1, L, a.shape[-1])

    qkv3, z3 = chunked(ret_qkv), chunked(z_all)
    ret_in = [decay, xi_tab, zeta_tab, cd_tab, gn_w.reshape(1, GROUP_W), qkv3, qkv3, qkv3, z3]
    ret_specs = [full((HEADS, L, L)), full((L, GROUP_W)), full((L, GROUP_W)), full((1, GROUP_W)),
                 full((1, GROUP_W)), blk(GROUP_W, 0), blk(GROUP_W, 1), blk(GROUP_W, 2), blk(GROUP_W, 2)]
    ssd_in = [conv_w, conv_b.reshape(1, CONV_CH), dtb, a_full, dsk, norm_w.reshape(1, GROUP_W),
              chunked(xbc), chunked(misc), z3]
    ssd_specs = [full((CONV_W, CONV_CH)), full((1, CONV_CH)), full((1, 128)), full((1, 128)),
                 full((1, GROUP_W)), full((1, GROUP_W)), blk(CONV_CH, 0), blk(128, 0), blk(GROUP_W, 3)]
    scratch = [pltpu.VMEM((HEADS, HEAD_DIM, HEAD_DIM), F32),
               pltpu.VMEM((8 + L, CONV_CH), F32),
               pltpu.VMEM((HEADS, SSM_STATE, HEAD_DIM), F32)]
    return ret_in, ssd_in, ret_specs + ssd_specs, blk(GROUP_W, 0), scratch


def _pick_tile(n, pref):
    t = pref
    while n % t:
        t //= 2
    return t


def kernel(x, norm_w, w_in, w_out, nsa_pe_k, nsa_pe_v, nsa_w_ck1, nsa_w_ck2, nsa_w_cv1, nsa_w_cv2,
           diff_lam_q1, diff_lam_k1, diff_lam_q2, diff_lam_k2, diff_subln_w, ret_gn_w,
           ssm_conv_w, ssm_conv_b, ssm_dt_bias, ssm_A_log, ssm_D, ssm_norm_w, final_norm_w):
    batch, seq, _ = x.shape
    depth = w_in.shape[0]
    m = batch * seq
    tm = _pick_tile(m, 512)
    tq = _pick_tile(seq, 512)
    tk = _pick_tile(seq, 256)
    tk_diff = _pick_tile(seq, 512)
    w_in_b = w_in.astype(BF16)
    w_r = _relayout_w_in(w_in_b)
    w_t = _relayout_w_in_t(w_in_b)
    w_out_b = w_out.astype(BF16)
    x2d = x.reshape(m, D_MODEL)
    projected = _in_proj(x2d, norm_w[0], w_r[0], w_t[0], tm)
    for i in range(depth):
        nsa_q, nsa_k2, nsa_cmp, misc, z_all, diff_qk, ret_qkv, xbc, vt_all = projected
        kc, vct = _nsa_compress(nsa_cmp, nsa_pe_k[i], nsa_pe_v[i], nsa_w_ck1[i], nsa_w_ck2[i],
                                nsa_w_cv1[i], nsa_w_cv2[i], batch, seq)
        lam_p = jnp.stack([diff_lam_q1[i], diff_lam_k1[i], diff_lam_q2[i], diff_lam_k2[i]])
        ys = _mixers((nsa_q, nsa_k2, vt_all, kc, vct, misc, z_all),
                     (diff_qk, vt_all, z_all, lam_p, diff_subln_w[i]),
                     (ret_qkv, xbc, misc, z_all, ret_gn_w[i], ssm_conv_w[i], ssm_conv_b[i], ssm_dt_bias[i],
                      ssm_A_log[i], ssm_D[i], ssm_norm_w[i]),
                     batch, seq, i, tq, tk, tk_diff)
        if i + 1 < depth:
            x2d, *projected = _out_in_proj(ys, w_out_b[i], x2d, norm_w[i + 1], w_r[i + 1], w_t[i + 1], tm)
        else:
            x2d = _out_proj(ys, w_out_b[i], x2d, final_norm_w, tm)
    return x2d.reshape(batch, seq, D_MODEL)
```

```python
import functools
import math

import numpy as np
import jax
import jax.numpy as jnp
from jax import lax
from jax.experimental import pallas as pl
from jax.experimental.pallas import tpu as pltpu

F32 = jnp.float32
BF16 = jnp.bfloat16
NEG_INF = float("-inf")
LOG2E = 1.4426950408889634

D_MODEL = 1024
DEPTH = 4
GROUP_W = 256
HEADS = 4
HEAD_DIM = 64
EPS = 1e-6
CMP_LEN = 32
CMP_STRIDE = 16
CMP_HIDDEN = 256
SLC_BLOCK = 64
SLC_SHIFT = 6
SLC_TOPK = 16
WINDOW = 512
DIFF_QK_DIM = 32
RET_CHUNK = 128
SSM_STATE = 128
SSM_CHUNK = 128
CONV_W = 4
CONV_CH = 768
N_ALIBI_HEADS = 8
LANES = 128
BF16_ROWS = 16
ACC_ROWS = HEAD_DIM + BF16_ROWS
QK_LOOKAHEAD = 3
ALIBI_ROWS = 3
POS_RADIX = 256
MASK_BIG = 2.0 ** 100
BLK_LANE0 = 64

IN_LAYOUT = (
    ("nsa_q", 256), ("nsa_k_cmp", 64), ("nsa_v_cmp", 64), ("nsa_k_slc", 64), ("nsa_v_slc", 64),
    ("nsa_k_win", 64), ("nsa_v_win", 64), ("nsa_gate", 12), ("nsa_z", 256),
    ("diff_q", 256), ("diff_k", 256), ("diff_v", 256), ("diff_z", 256),
    ("ret_q", 256), ("ret_k", 256), ("ret_v", 256), ("ret_z", 256),
    ("ssm_z", 256), ("ssm_xbc", 768), ("ssm_dt", 4),
)
IN_OFF = {}
_o = 0
for _n, _w in IN_LAYOUT:
    IN_OFF[_n] = (_o, _w)
    _o += _w
IN_W = _o

GATE_COL = 0
DT_COL = 12
IN_OUTPUTS = (
    ("nsa_q", BF16, ("nsa_q",), 256),
    ("nsa_k2", BF16, ("nsa_k_slc", "nsa_k_win"), 128),
    ("nsa_cmp", F32, ("nsa_k_cmp", "nsa_v_cmp"), 128),
    ("misc", F32, ("nsa_gate", "ssm_dt"), 128),
    ("z_all", F32, ("nsa_z", "diff_z", "ret_z", "ssm_z"), 1024),
    ("diff_qk", BF16, ("diff_q", "diff_k"), 512),
    ("ret_qkv", BF16, ("ret_q", "ret_k", "ret_v"), 768),
    ("xbc", F32, ("ssm_xbc",), 768),
)
IN_T_SRC = ("diff_v", "nsa_v_slc", "nsa_v_win")
IN_T_ROWS = 384
DIFF_VT_BLK = 0
NSA_VT_BLK = 2
IN_SEGS = []
_o = 0
for _n, _dt, _src, _w in IN_OUTPUTS:
    IN_SEGS.append((_o, _o + _w))
    _o += _w
IN_WP = _o

VMEM_LIMIT = 56 * 1024 * 1024


def _alibi_slopes():
    return [2.0 ** (-8.0 * (i + 1) / N_ALIBI_HEADS) for i in range(N_ALIBI_HEADS)]


NSA_SLOPES = _alibi_slopes()[0::2]
DIFF_SLOPES = _alibi_slopes()[1::2]


def _silu(x):
    return x * jax.nn.sigmoid(x)


def _dot(a, b):
    return jnp.dot(a, b, preferred_element_type=F32)


def _dot_nt(a, b):
    return lax.dot_general(a, b, (((1,), (1,)), ((), ())), preferred_element_type=F32)


def _dot_tn(a, b):
    return lax.dot_general(a, b, (((0,), (0,)), ((), ())), preferred_element_type=F32)


def _params(sem):
    return pltpu.CompilerParams(dimension_semantics=sem, vmem_limit_bytes=VMEM_LIMIT)


def _relayout_w_in(w_in):
    cols = []
    for _n, _dt, src, width in IN_OUTPUTS:
        used = 0
        for s in src:
            off, w = IN_OFF[s]
            cols.append(w_in[:, :, off:off + w])
            used += w
        if used < width:
            cols.append(jnp.zeros(w_in.shape[:2] + (width - used,), w_in.dtype))
    return jnp.concatenate(cols, axis=-1)


def _relayout_w_in_t(w_in):
    cols = [w_in[:, :, IN_OFF[s][0]:IN_OFF[s][0] + IN_OFF[s][1]] for s in IN_T_SRC]
    return jnp.swapaxes(jnp.concatenate(cols, axis=-1), 1, 2)


def _norm_project(x, nw_ref, w_ref, wt_ref, out_refs):
    ms = jnp.mean(x * x, axis=-1, keepdims=True)
    h = (x * lax.rsqrt(ms + EPS) * nw_ref[...]).astype(BF16)
    for ref, (a, b) in zip(out_refs[:-1], IN_SEGS):
        ref[...] = _dot(h, w_ref[:, a:b]).astype(ref.dtype)
    out_refs[-1][...] = _dot_nt(wt_ref[...], h).astype(BF16)


def _in_proj_kernel(x_ref, nw_ref, w_ref, wt_ref, *out_refs):
    _norm_project(x_ref[...], nw_ref, w_ref, wt_ref, out_refs)


def _in_proj_outputs(m, tm):
    out_shape = [jax.ShapeDtypeStruct((m, b - a), dt) for (_n, dt, _s, _w), (a, b) in zip(IN_OUTPUTS, IN_SEGS)]
    out_specs = [pl.BlockSpec((tm, b - a), lambda i: (i, 0)) for (a, b) in IN_SEGS]
    out_shape.append(jax.ShapeDtypeStruct((IN_T_ROWS, m), BF16))
    out_specs.append(pl.BlockSpec((IN_T_ROWS, tm), lambda i: (0, i)))
    return out_shape, out_specs


def _in_proj(x2d, norm_w, w_r, w_t, tm):
    m = x2d.shape[0]
    out_shape, out_specs = _in_proj_outputs(m, tm)
    return pl.pallas_call(
        _in_proj_kernel,
        grid=(m // tm,),
        in_specs=[pl.BlockSpec((tm, D_MODEL), lambda i: (i, 0)),
                  pl.BlockSpec((1, D_MODEL), lambda i: (0, 0)),
                  pl.BlockSpec((D_MODEL, IN_WP), lambda i: (0, 0)),
                  pl.BlockSpec((IN_T_ROWS, D_MODEL), lambda i: (0, 0))],
        out_specs=out_specs,
        out_shape=out_shape,
        compiler_params=_params(("parallel",)),
        name="in_proj",
    )(x2d, norm_w.reshape(1, D_MODEL), w_r, w_t)


def _out_proj_kernel(y0_ref, y1_ref, y2_ref, y3_ref, w_ref, x_ref, fw_ref, o_ref):
    acc = x_ref[...]
    for g, y_ref in enumerate((y0_ref, y1_ref, y2_ref, y3_ref)):
        acc = acc + _dot(y_ref[...], w_ref[g * GROUP_W:(g + 1) * GROUP_W, :])
    ms = jnp.mean(acc * acc, axis=-1, keepdims=True)
    o_ref[...] = acc * lax.rsqrt(ms + EPS) * fw_ref[...]


def _out_proj(ys, w_out_b, x2d, final_w, tm):
    m = x2d.shape[0]
    yspec = pl.BlockSpec((tm, GROUP_W), lambda i: (i, 0))
    return pl.pallas_call(
        _out_proj_kernel,
        grid=(m // tm,),
        in_specs=[yspec, yspec, yspec, yspec,
                  pl.BlockSpec((D_MODEL, D_MODEL), lambda i: (0, 0)),
                  pl.BlockSpec((tm, D_MODEL), lambda i: (i, 0)),
                  pl.BlockSpec((1, D_MODEL), lambda i: (0, 0))],
        out_specs=pl.BlockSpec((tm, D_MODEL), lambda i: (i, 0)),
        out_shape=jax.ShapeDtypeStruct((m, D_MODEL), F32),
        compiler_params=_params(("parallel",)),
        name="out_proj",
    )(*ys, w_out_b, x2d, final_w.reshape(1, D_MODEL))


def _out_in_proj_kernel(y0_ref, y1_ref, y2_ref, y3_ref, wo_ref, x_ref, nw_ref, w_ref, wt_ref, xo_ref, *out_refs):
    acc = x_ref[...]
    for g, y_ref in enumerate((y0_ref, y1_ref, y2_ref, y3_ref)):
        acc = acc + _dot(y_ref[...], wo_ref[g * GROUP_W:(g + 1) * GROUP_W, :])
    xo_ref[...] = acc
    _norm_project(acc, nw_ref, w_ref, wt_ref, out_refs)


def _out_in_proj(ys, w_out_b, x2d, norm_w, w_r, w_t, tm):
    m = x2d.shape[0]
    yspec = pl.BlockSpec((tm, GROUP_W), lambda i: (i, 0))
    xspec = pl.BlockSpec((tm, D_MODEL), lambda i: (i, 0))
    out_shape, out_specs = _in_proj_outputs(m, tm)
    return pl.pallas_call(
        _out_in_proj_kernel,
        grid=(m // tm,),
        in_specs=[yspec, yspec, yspec, yspec,
                  pl.BlockSpec((D_MODEL, D_MODEL), lambda i: (0, 0)),
                  xspec,
                  pl.BlockSpec((1, D_MODEL), lambda i: (0, 0)),
                  pl.BlockSpec((D_MODEL, IN_WP), lambda i: (0, 0)),
                  pl.BlockSpec((IN_T_ROWS, D_MODEL), lambda i: (0, 0))],
        out_specs=[xspec] + out_specs,
        out_shape=[jax.ShapeDtypeStruct((m, D_MODEL), F32)] + out_shape,
        compiler_params=_params(("parallel",)),
        name="out_in_proj",
    )(*ys, w_out_b, x2d, norm_w.reshape(1, D_MODEL), w_r, w_t)


def _stage_scores(s_ref, slot, cols, s, keep=None):
    if keep is not None:
        s = jnp.where(keep, s, NEG_INF)
    s_ref[slot, :, cols] = s
    return jnp.max(s, axis=0, keepdims=True)


def _flash_update_t(s, s_max, c1, shift, vt_ext, m_ref, acc_ref, idx, cols):
    m_old = m_ref[idx, :, cols]
    m_new = jnp.maximum(m_old, c1 * s_max + shift)
    alpha = jnp.exp2(m_old - m_new)
    p = jnp.exp2(c1 * s() - (m_new - shift))
    acc_ref[idx, :, cols] = alpha * acc_ref[idx, :, cols] + _dot(vt_ext, p.astype(BF16))
    m_ref[idx, :, cols] = m_new


def _bf16_pieces(x, n=3):
    out = []
    for _ in range(n):
        p = float(np.asarray(x, np.float32).astype(BF16).astype(np.float32))
        out.append(p)
        x = x - p
    return out


def _key_pos_features(tk):
    lane = lax.broadcasted_iota(jnp.int32, (tk, LANES), 1)
    row = lax.broadcasted_iota(jnp.int32, (tk, LANES), 0)
    out = jnp.zeros((tk, LANES), jnp.int32)
    for d in range(-(-tk // POS_RADIX)):
        digit = jnp.bitwise_and(jnp.right_shift(row, 8 * d), POS_RADIX - 1)
        out = jnp.where((lane >= ALIBI_ROWS * d) & (lane < ALIBI_ROWS * (d + 1)), digit, out)
    return out.astype(F32).astype(BF16)


def _alibi_rows(beta, tq, rows=LANES, tk=POS_RADIX):
    row = lax.broadcasted_iota(jnp.int32, (rows, tq), 0)
    out = jnp.zeros((rows, tq), F32)
    for d in range(-(-tk // POS_RADIX)):
        for r, piece in enumerate(_bf16_pieces(beta * POS_RADIX ** d, ALIBI_ROWS)):
            out = jnp.where(row == ALIBI_ROWS * d + r, piece, out)
    return out.astype(BF16)


def _normalized(acc):
    return acc[0:HEAD_DIM] / jnp.maximum(acc[HEAD_DIM:HEAD_DIM + 1], 1e-30)


def _rel_t(tk, tq):
    return lax.broadcasted_iota(jnp.int32, (tk, tq), 1) - lax.broadcasted_iota(jnp.int32, (tk, tq), 0)


def _diff_phases(lam_ref, sw_ref, q_ref, k_ref, vt_ref, z_ref, o_ref, m_ref, acc_ref, wq_ref, s_ref,
                 *, tq, tk, lam_init):
    qi = pl.program_id(1)
    kpq = tq // tk
    scale = DIFF_QK_DIM ** -0.5
    c1 = scale * LOG2E
    m_ref[...] = jnp.full(m_ref.shape, NEG_INF, F32)
    acc_ref[...] = jnp.zeros(acc_ref.shape, F32)
    rel = _rel_t(tk, tq)

    qt = q_ref[...].astype(F32).T.astype(BF16)
    row = lax.broadcasted_iota(jnp.int32, (LANES, tq), 0)
    for j in range(2 * HEADS):
        g, r0 = divmod(j * DIFF_QK_DIM, LANES)
        qg = qt[g * LANES:(g + 1) * LANES]
        wq_ref[j, 0:LANES, :] = jnp.where((row >= r0) & (row < r0 + DIFF_QK_DIM), qg, jnp.zeros_like(qg))
        wq_ref[j, LANES:2 * LANES, :] = _alibi_rows(DIFF_SLOPES[j // 2] / scale, tq, tk=tk)

    ones = jnp.ones((BF16_ROWS, tk), BF16)
    kpos = _key_pos_features(tk)

    n_maps = 2 * HEADS
    n_slots = s_ref.shape[0]

    def tiles(kis, mask_offs):
        loaded = []
        for ki in kis:
            start = pl.multiple_of(ki * tk, tk)
            loaded.append((k_ref[pl.ds(start, tk), :],
                           vt_ref[:, pl.ds(start, tk)],
                           (ki * tk).astype(F32)))
        items = [(t, j) for t in range(len(kis)) for j in range(n_maps)]
        cols = slice(0, tq)

        def scores(i):
            t, j = items[i]
            g = (j * DIFF_QK_DIM) // LANES
            lhs = jnp.concatenate([loaded[t][0][:, g * LANES:(g + 1) * LANES], kpos], axis=1)
            s_ref[i % n_slots] = _dot(lhs, wq_ref[j])

        def staged(i):
            t = items[i][0]
            if mask_offs[t] is None:
                return lambda: s_ref[i % n_slots]
            return lambda: jnp.where(rel >= mask_offs[t], s_ref[i % n_slots], NEG_INF)

        for i in range(QK_LOOKAHEAD):
            scores(i)
        for i, (t, j) in enumerate(items):
            h = j // 2
            if i + QK_LOOKAHEAD < len(items):
                scores(i + QK_LOOKAHEAD)
            _k, vt, key0 = loaded[t]
            vt_ext = jnp.concatenate([vt[h * HEAD_DIM:(h + 1) * HEAD_DIM], ones], axis=0)
            s = staged(i)
            _flash_update_t(s, jnp.max(s(), axis=0, keepdims=True), c1, (DIFF_SLOPES[h] * LOG2E) * key0,
                            vt_ext, m_ref, acc_ref, j, cols)

    def loop_body(i):
        tiles([i * kpq + d for d in range(kpq)], [None] * kpq)

    def diag_tiles():
        tiles([qi * kpq + d for d in range(kpq)], [d * tk for d in range(kpq)])

    def finish():
        lp = lam_ref[...]
        lam = (jnp.exp(jnp.sum(lp[0:1] * lp[1:2], axis=-1, keepdims=True))
               - jnp.exp(jnp.sum(lp[2:3] * lp[3:4], axis=-1, keepdims=True)) + lam_init)
        sw = sw_ref[...]
        o_t = jnp.concatenate([_normalized(acc_ref[2 * h]) - lam * _normalized(acc_ref[2 * h + 1])
                               for h in range(HEADS)], axis=0)
        o = o_t.T
        outs = []
        for h in range(HEADS):
            oh = o[:, h * HEAD_DIM:(h + 1) * HEAD_DIM]
            ms = jnp.mean(oh * oh, axis=-1, keepdims=True)
            outs.append(oh * lax.rsqrt(ms + EPS) * sw * (1.0 - lam_init))
        y = jnp.concatenate(outs, axis=-1) * _silu(z_ref[...])
        o_ref[...] = y.astype(o_ref.dtype)

    return loop_body, diag_tiles, finish


def _diff_operands(qk, vt_all, z_all, lam_p, subln_w, seq, nq, tq, tk):
    inputs = [lam_p, subln_w.reshape(1, HEAD_DIM), qk, qk, vt_all, z_all]
    in_specs = [pl.BlockSpec((4, DIFF_QK_DIM), lambda b, i: (0, 0)),
                pl.BlockSpec((1, HEAD_DIM), lambda b, i: (0, 0)),
                pl.BlockSpec((tq, GROUP_W), lambda b, i: (b * nq + i, 0)),
                pl.BlockSpec((seq, GROUP_W), lambda b, i: (b, 1)),
                pl.BlockSpec((GROUP_W, seq), lambda b, i: (DIFF_VT_BLK, b)),
                pl.BlockSpec((tq, GROUP_W), lambda b, i: (b * nq + i, 1))]
    scratch = [pltpu.VMEM((2 * HEADS, 1, tq), F32),
               pltpu.VMEM((2 * HEADS, ACC_ROWS, tq), F32),
               pltpu.VMEM((2 * HEADS, 2 * LANES, tq), BF16),
               pltpu.VMEM((QK_LOOKAHEAD + 1, tk, tq), F32)]
    return inputs, in_specs, scratch


def _nsa_compress_kernel(cmp_ref, pe_ref, w1_ref, wk2_ref, wv2t_ref, kc_ref, vct_ref):
    ng = kc_ref.shape[1]
    top = jnp.zeros((ng, 2 * CMP_HIDDEN), F32)
    bot = jnp.zeros((ng, 2 * CMP_HIDDEN), F32)
    for p in range(CMP_STRIDE):
        x = cmp_ref[pl.ds(p, ng, stride=CMP_STRIDE), :]
        top = top + _dot((x + pe_ref[p:p + 1, :]).astype(BF16), w1_ref[p])
        bot = bot + _dot((x + pe_ref[CMP_STRIDE + p:CMP_STRIDE + p + 1, :]).astype(BF16), w1_ref[CMP_STRIDE + p])
    hid = _silu(top + pltpu.roll(bot, ng - 1, 0)).astype(BF16)
    hk, hv = hid[:, 0:CMP_HIDDEN], hid[:, CMP_HIDDEN:]
    kc = _dot(hk, wk2_ref[...])
    lane = lax.broadcasted_iota(jnp.int32, kc.shape, 1)
    blk = lax.broadcasted_iota(jnp.int32, kc.shape, 0).astype(F32)
    kc = jnp.where((lane >= HEAD_DIM) & (lane < HEAD_DIM + ALIBI_ROWS), blk, kc)
    kc_ref[0] = kc.astype(kc_ref.dtype)
    vct_ref[0] = _dot_nt(wv2t_ref[...], hv).astype(vct_ref.dtype)


def _nsa_compress(cmp2d, pe_k, pe_v, w_ck1, w_ck2, w_cv1, w_cv2, batch, seq):
    ng = seq // CMP_STRIDE
    wk = w_ck1.reshape(CMP_LEN, HEAD_DIM, CMP_HIDDEN)
    wv = w_cv1.reshape(CMP_LEN, HEAD_DIM, CMP_HIDDEN)
    w1 = jnp.concatenate([jnp.pad(wk, ((0, 0), (0, 0), (0, CMP_HIDDEN))),
                          jnp.pad(wv, ((0, 0), (0, 0), (CMP_HIDDEN, 0)))], axis=1).astype(BF16)
    pe = jnp.concatenate([pe_k, pe_v], axis=1)

    def full(shape):
        return pl.BlockSpec(shape, lambda b: (0,) * len(shape))

    return pl.pallas_call(
        _nsa_compress_kernel,
        grid=(batch,),
        in_specs=[pl.BlockSpec((seq, LANES), lambda b: (b, 0)), full((CMP_LEN, LANES)),
                  full((CMP_LEN, LANES, 2 * CMP_HIDDEN)), full((CMP_HIDDEN, LANES)),
                  full((HEAD_DIM, CMP_HIDDEN))],
        out_specs=[pl.BlockSpec((1, ng, LANES), lambda b: (b, 0, 0)),
                   pl.BlockSpec((1, HEAD_DIM, ng), lambda b: (b, 0, 0))],
        out_shape=[jax.ShapeDtypeStruct((batch, ng, LANES), BF16),
                   jax.ShapeDtypeStruct((batch, HEAD_DIM, ng), BF16)],
        compiler_params=_params(("parallel",)),
        name="nsa_compress",
    )(cmp2d, pe, w1, jnp.pad(w_ck2, ((0, 0), (0, LANES - HEAD_DIM))).astype(BF16), w_cv2.T.astype(BF16))


def _nsa_phases(ovl_ref, q_ref, k_ref, vt_ref, kc_ref, vct_ref, misc_ref, z_ref, o_ref,
                m_ref, acc_ref, wq_ref, s_ref, sc_ref, *, tq, tk, seq):
    qi = pl.program_id(1)
    kpq = tq // tk
    scale = HEAD_DIM ** -0.5
    c1 = scale * LOG2E
    ng = seq // CMP_STRIDE
    ns = seq // SLC_BLOCK
    top = min(SLC_TOPK, ns)
    m_ref[...] = jnp.full(m_ref.shape, NEG_INF, F32)
    acc_ref[...] = jnp.zeros(acc_ref.shape, F32)
    rel = _rel_t(tk, tq)

    qt = q_ref[...].astype(F32).T.astype(BF16)
    zeros_q = jnp.zeros((HEAD_DIM, tq), BF16)
    for h in range(HEADS):
        qh = qt[h * HEAD_DIM:(h + 1) * HEAD_DIM]
        alibi = _alibi_rows(NSA_SLOPES[h] / scale, tq, tk=tk)
        wq_ref[h, 0:LANES, :] = jnp.concatenate([qh, zeros_q], axis=0)
        wq_ref[h, LANES:2 * LANES, :] = alibi
        wq_ref[HEADS + h, 0:LANES, :] = jnp.concatenate([zeros_q, qh], axis=0)
        wq_ref[HEADS + h, LANES:2 * LANES, :] = alibi

    t_lane = qi * tq + lax.broadcasted_iota(jnp.int32, (1, tq), 1)
    n_sub = lax.broadcasted_iota(jnp.int32, (ng, 1), 0)
    c_valid = (n_sub * CMP_STRIDE + (CMP_LEN - 1)) <= t_lane
    kc = kc_ref[0]
    vct_ext = jnp.concatenate([vct_ref[0], jnp.ones((BF16_ROWS, ng), BF16)], axis=0)
    ovl = ovl_ref[...]
    cmp_max = []
    for h in range(HEADS):
        rhs = jnp.concatenate([qt[h * HEAD_DIM:(h + 1) * HEAD_DIM],
                               _alibi_rows(NSA_SLOPES[h] * CMP_STRIDE / scale, tq, HEAD_DIM)], axis=0)
        cmp_max.append(_stage_scores(sc_ref, h, slice(0, tq), _dot(kc, rhs), c_valid))
    imp = jnp.zeros((ns, tq), F32)
    o_cmp = []
    for h in range(HEADS):
        mx = jnp.where(cmp_max[h] > NEG_INF, cmp_max[h], 0.0)
        pb = jnp.exp2(c1 * sc_ref[h] - c1 * mx).astype(BF16)
        o_ext = _dot(vct_ext, pb)
        r = 1.0 / jnp.maximum(o_ext[HEAD_DIM:HEAD_DIM + 1], 1e-30)
        o_cmp.append(o_ext[0:HEAD_DIM] * r)
        imp = imp + _dot(ovl, pb) * r

    def select_blocks():
        j_sub = lax.broadcasted_iota(jnp.int32, (ns, 1), 0)
        j_sub_f = j_sub.astype(F32)
        cur = jnp.right_shift(t_lane, SLC_SHIFT)
        forced = (j_sub == 0) | (j_sub == cur) | (j_sub == cur - 1)
        valid = (j_sub * SLC_BLOCK) <= t_lane
        score = jnp.where(forced, 1e30, jnp.where(valid, imp, -1.0))
        sel = jnp.zeros((ns, tq), F32)
        for _ in range(top):
            mx = jnp.max(score, axis=0, keepdims=True)
            idx = jnp.min(jnp.where(score == mx, j_sub_f, float(ns)), axis=0, keepdims=True)
            pick = j_sub_f == idx
            sel = jnp.where(pick, 1.0, sel)
            score = jnp.where(pick, -2.0, score)
        unsel = jnp.where(sel > 0.5, 0.0, -MASK_BIG).astype(BF16)
        for h in range(HEADS):
            wq_ref[h, LANES + BLK_LANE0:LANES + BLK_LANE0 + ns, :] = unsel

    aux_lane = lax.broadcasted_iota(jnp.int32, (tk, LANES), 1)
    aux_blk = jnp.right_shift(lax.broadcasted_iota(jnp.int32, (tk, LANES), 0), SLC_SHIFT) + BLK_LANE0
    kpos = _key_pos_features(tk)
    ones = jnp.ones((BF16_ROWS, tk), BF16)

    n_slots = s_ref.shape[0]

    def tiles(window, kis, cols, mask_offs=None):
        base = HEADS if window else 0
        loaded = []
        for t, ki in enumerate(kis):
            start = pl.multiple_of(ki * tk, tk)
            k = k_ref[pl.ds(start, tk), :]
            if window:
                vt = vt_ref[HEAD_DIM:2 * HEAD_DIM, pl.ds(start, tk)]
                aux = kpos
                dist = rel[:, cols[t]] + (qi * tq - ki * tk)
                keep = (dist >= 0) & (dist < WINDOW)
            else:
                vt = vt_ref[0:HEAD_DIM, pl.ds(start, tk)]
                onehot = aux_lane == aux_blk + ki * (tk // SLC_BLOCK)
                aux = jnp.where(onehot, jnp.ones_like(kpos), kpos)
                keep = None
            loaded.append((jnp.concatenate([k, aux], axis=1),
                           jnp.concatenate([vt, ones], axis=0), (ki * tk).astype(F32), keep))
        items = [(t, h) for t in range(len(kis)) for h in range(HEADS)]
        s_max = {}

        def scores(i):
            t, h = items[i]
            if window:
                keep = loaded[t][3]
            else:
                keep = None if mask_offs[t] is None else rel[:, cols[t]] >= mask_offs[t]
            s_max[i] = _stage_scores(s_ref, i % n_slots, cols[t],
                                     _dot(loaded[t][0], wq_ref[base + h, :, cols[t]]), keep)

        for i in range(QK_LOOKAHEAD):
            scores(i)
        for i, (t, h) in enumerate(items):
            if i + QK_LOOKAHEAD < len(items):
                scores(i + QK_LOOKAHEAD)
            _lhs, vt_ext, key0, _keep = loaded[t]
            _flash_update_t(lambda i=i, t=t: s_ref[i % n_slots, :, cols[t]], s_max.pop(i), c1,
                            (NSA_SLOPES[h] * LOG2E) * key0, vt_ext, m_ref, acc_ref, base + h, cols[t])

    all_cols = slice(0, tq)
    diag_cols = [slice(d * tk, tq) for d in range(kpq)]

    tiles(True, [qi * kpq + d for d in range(kpq)], diag_cols)
    select_blocks()

    def loop_body(i):
        tiles(False, [i * kpq + d for d in range(kpq)], [all_cols] * kpq, [None] * kpq)

    def back_window():
        n_back = (WINDOW + tk - 1) // tk
        for g in range((n_back + kpq - 1) // kpq):
            backs = list(range(g * kpq + 1, min((g + 1) * kpq, n_back) + 1))
            back_cols = [slice(0, min(tq, -(-(WINDOW - (back - 1) * tk - 1) // LANES) * LANES)) for back in backs]

            @pl.when(qi > g)
            def _():
                tiles(True, [qi * kpq - back for back in backs], back_cols)

    def diag_tiles():
        tiles(False, [qi * kpq + d for d in range(kpq)], diag_cols, [d * tk for d in range(kpq)])

    def finish():
        g_t = jax.nn.sigmoid(misc_ref[...]).T
        outs = []
        for h in range(HEADS):
            r0 = GATE_COL + 3 * h
            outs.append(g_t[r0:r0 + 1] * o_cmp[h] + g_t[r0 + 1:r0 + 2] * _normalized(acc_ref[h])
                        + g_t[r0 + 2:r0 + 3] * _normalized(acc_ref[HEADS + h]))
        y = jnp.concatenate(outs, axis=0).T * _silu(z_ref[...])
        o_ref[...] = y.astype(o_ref.dtype)

    return loop_body, back_window, diag_tiles, finish


def _overlap_t(seq):
    nc = (seq - CMP_LEN) // CMP_STRIDE + 1
    ng = seq // CMP_STRIDE
    ns = seq // SLC_BLOCK
    c_start = np.arange(ng) * CMP_STRIDE
    c_end = c_start + CMP_LEN - 1
    s_start = np.arange(ns) * SLC_BLOCK
    s_end = s_start + SLC_BLOCK - 1
    ov = (c_start[None, :] <= s_end[:, None]) & (c_end[None, :] >= s_start[:, None]) & (np.arange(ng)[None, :] < nc)
    return jnp.asarray(ov.astype(np.float32), dtype=BF16)


def _nsa_operands(q, k2, vt_all, kc, vct, misc, z_all, seq, nq, tq, tk):
    ng = seq // CMP_STRIDE
    ns = seq // SLC_BLOCK
    assert tq - tk < WINDOW and tk % SLC_BLOCK == 0 and ns <= LANES - BLK_LANE0
    inputs = [_overlap_t(seq), q, k2, vt_all, kc, vct, misc, z_all]
    in_specs = [pl.BlockSpec((ns, ng), lambda b, i: (0, 0)),
                pl.BlockSpec((tq, GROUP_W), lambda b, i: (b * nq + i, 0)),
                pl.BlockSpec((seq, LANES), lambda b, i: (b, 0)),
                pl.BlockSpec((2 * HEAD_DIM, seq), lambda b, i: (NSA_VT_BLK, b)),
                pl.BlockSpec((1, ng, LANES), lambda b, i: (b, 0, 0)),
                pl.BlockSpec((1, HEAD_DIM, ng), lambda b, i: (b, 0, 0)),
                pl.BlockSpec((tq, LANES), lambda b, i: (b * nq + i, 0)),
                pl.BlockSpec((tq, GROUP_W), lambda b, i: (b * nq + i, 0))]
    scratch = [pltpu.VMEM((2 * HEADS, 1, tq), F32),
               pltpu.VMEM((2 * HEADS, ACC_ROWS, tq), F32),
               pltpu.VMEM((2 * HEADS, 2 * LANES, tq), BF16),
               pltpu.VMEM((2 * HEADS, tk, tq), F32),
               pltpu.VMEM((HEADS, ng, tq), F32)]
    return inputs, in_specs, scratch


def _mixers_kernel(*refs, n_in, n_scratch, tq, tk_nsa, tk_diff, seq, lam_init):
    bounds = np.cumsum([0] + list(n_in))
    nsa_in, diff_in, ret_in, ssd_in = (refs[a:b] for a, b in zip(bounds[:-1], bounds[1:]))
    nsa_out, diff_out, ret_out, ssd_out = refs[bounds[-1]:bounds[-1] + 4]
    sb = np.cumsum([0] + list(n_scratch)) + bounds[-1] + 4
    nsa_scr, diff_scr, rec_scr = (refs[a:b] for a, b in zip(sb[:-1], sb[1:]))
    ret_st, ssd_ext, ssd_st = rec_scr
    qi = pl.program_id(1)

    @pl.when(qi == 0)
    def _():
        ret_st[...] = jnp.zeros(ret_st.shape, F32)
        ssd_st[...] = jnp.zeros(ssd_st.shape, F32)
        ssd_ext[0:8, :] = jnp.zeros((8, CONV_CH), F32)

    chunks = list(range(ret_out.shape[0]))
    first, second = chunks[:len(chunks) // 4], chunks[len(chunks) // 4:]
    nsa_loop, nsa_back_window, nsa_diag, nsa_finish = _nsa_phases(
        *nsa_in, nsa_out, *nsa_scr, tq=tq, tk=tk_nsa, seq=seq)
    diff_loop, diff_diag, diff_finish = _diff_phases(
        *diff_in, diff_out, *diff_scr, tq=tq, tk=tk_diff, lam_init=lam_init)
    _ret_body(*ret_in, ret_out, ret_st, first)
    _ssd_body(*ssd_in, ssd_out, ssd_ext, ssd_st, first)

    def body(i, carry):
        nsa_loop(i)
        diff_loop(i)
        return carry

    lax.fori_loop(0, qi, body, 0)
    nsa_back_window()
    nsa_diag()
    diff_diag()
    _ret_body(*ret_in, ret_out, ret_st, second)
    _ssd_body(*ssd_in, ssd_out, ssd_ext, ssd_st, second)
    nsa_finish()
    diff_finish()


def _mixers(nsa_args, diff_args, rec_args, batch, seq, layer_idx, tq, tk_nsa, tk_diff):
    nq = seq // tq
    nsa_in, nsa_specs, nsa_scratch = _nsa_operands(*nsa_args, seq, nq, tq, tk_nsa)
    diff_in, diff_specs, diff_scratch = _diff_operands(*diff_args, seq, nq, tq, tk_diff)
    ret_in, ssd_in, rec_specs, rec_out_spec, rec_scratch = _recurrent_operands(*rec_args, nq, tq)
    out_spec = pl.BlockSpec((tq, GROUP_W), lambda b, i: (b * nq + i, 0))
    m = batch * seq
    y_nsa, y_diff, y_ret, y_ssm = pl.pallas_call(
        functools.partial(_mixers_kernel, n_in=(len(nsa_in), len(diff_in), len(ret_in), len(ssd_in)),
                          n_scratch=(len(nsa_scratch), len(diff_scratch), len(rec_scratch)),
                          tq=tq, tk_nsa=tk_nsa, tk_diff=tk_diff, seq=seq,
                          lam_init=0.8 - 0.6 * math.exp(-0.3 * layer_idx)),
        grid=(batch, nq),
        in_specs=nsa_specs + diff_specs + rec_specs,
        out_specs=[out_spec, out_spec, rec_out_spec, rec_out_spec],
        out_shape=[jax.ShapeDtypeStruct((m, GROUP_W), BF16)] * 2
                  + [jax.ShapeDtypeStruct((m // SSM_CHUNK, SSM_CHUNK, GROUP_W), BF16)] * 2,
        scratch_shapes=nsa_scratch + diff_scratch + rec_scratch,
        compiler_params=_params(("parallel", "arbitrary")),
        name="mixers",
    )(*nsa_in, *diff_in, *ret_in, *ssd_in)
    return y_nsa, y_diff, y_ret.reshape(m, GROUP_W), y_ssm.reshape(m, GROUP_W)


def _ret_tables():
    c = RET_CHUNK
    h = np.arange(HEADS, dtype=np.float32)
    log_g = jnp.log(1.0 - 2.0 ** (-5.0 - jnp.asarray(h)))
    pos = jnp.arange(c, dtype=F32)
    rel = pos[:, None] - pos[None, :]
    decay = jnp.where(rel >= 0, jnp.exp(log_g[:, None, None] * jnp.maximum(rel, 0.0)), 0.0)
    xi = jnp.exp(log_g[:, None] * (pos + 1.0))
    zeta = jnp.exp(log_g[:, None] * (c - 1.0 - pos))
    chunk_decay = jnp.exp(log_g * c)
    xi_tab = jnp.repeat(xi.T, HEAD_DIM, axis=1)
    zeta_tab = jnp.repeat(zeta.T, HEAD_DIM, axis=1)
    cd_tab = jnp.repeat(chunk_decay, HEAD_DIM)[None, :]
    return decay, xi_tab, zeta_tab, cd_tab


def _ret_body(decay_ref, xi_ref, zeta_ref, cd_ref, gn_ref, q_ref, k_ref, v_ref, z_ref, o_ref, st_ref, blocks):
    xi = xi_ref[...]
    cd = cd_ref[...]
    for bb in blocks:
        q = (q_ref[bb].astype(F32) * (HEAD_DIM ** -0.5)).astype(BF16)
        k = k_ref[bb]
        v = v_ref[bb]
        kz_t = (k.astype(F32) * zeta_ref[...]).T.astype(BF16)
        outs = []
        for h in range(HEADS):
            sl = slice(h * HEAD_DIM, (h + 1) * HEAD_DIM)
            qh, kh, vh = q[:, sl], k[:, sl], v[:, sl]
            prev = st_ref[h]
            inner = (_dot_nt(qh, kh) * decay_ref[h]).astype(BF16)
            o = _dot(inner, vh) + _dot(qh, prev.astype(BF16)) * xi[:, sl]
            st_ref[h] = prev * cd[:, sl] + _dot(kz_t[sl, :], vh)
            mu = jnp.mean(o, axis=-1, keepdims=True)
            d = o - mu
            var = jnp.mean(d * d, axis=-1, keepdims=True)
            outs.append(d * lax.rsqrt(var + EPS))
        y = jnp.concatenate(outs, axis=-1) * gn_ref[...] * _silu(z_ref[bb])
        o_ref[bb] = y.astype(o_ref.dtype)


def _ssd_body(cw_ref, cb_ref, dtb_ref, a_ref, dsk_ref, nw_ref, xbc_ref, misc_ref, z_ref, o_ref,
              ext_ref, st_ref, blocks):
    L = SSM_CHUNK
    hi = lax.Precision.HIGHEST
    row = lax.broadcasted_iota(jnp.int32, (L, L), 0)
    col = lax.broadcasted_iota(jnp.int32, (L, L), 1)
    causal = row >= col
    tril = jnp.where(causal, 1.0, 0.0).astype(F32)
    dsk = dsk_ref[...]

    for bb in blocks:
        raw = xbc_ref[bb]
        ext_ref[8:8 + L, :] = raw
        conv = cb_ref[...] + raw * cw_ref[CONV_W - 1:CONV_W, :]
        for w in range(CONV_W - 1):
            shift = CONV_W - 1 - w
            conv = conv + ext_ref[8 - shift:8 - shift + L, :] * cw_ref[w:w + 1, :]
        ext_ref[0:8, :] = raw[L - 8:L, :]
        xc = _silu(conv)
        x = xc[:, 0:GROUP_W]
        bm = xc[:, GROUP_W:GROUP_W + 2 * SSM_STATE].astype(BF16)
        cm = xc[:, GROUP_W + 2 * SSM_STATE:].astype(BF16)

        dt_full = jax.nn.softplus(misc_ref[bb] + dtb_ref[...])
        da = dt_full * a_ref[...]
        cs_col = jnp.dot(tril, da, precision=hi, preferred_element_type=F32)
        cs_row = lax.dot_general(da, tril, (((0,), (1,)), ((), ())), precision=hi,
                                 preferred_element_type=F32)

        outs = []
        for h in range(HEADS):
            g = h // 2
            c0 = DT_COL + h
            sl = slice(h * HEAD_DIM, (h + 1) * HEAD_DIM)
            gs = slice(g * SSM_STATE, (g + 1) * SSM_STATE)
            cs_c = cs_col[:, c0:c0 + 1]
            cs_r = cs_row[c0:c0 + 1, :]
            cs_last = cs_col[L - 1:L, c0:c0 + 1]
            xh = x[:, sl]
            xdt = xh * dt_full[:, c0:c0 + 1]
            seg = jnp.exp(jnp.where(causal, cs_c - cs_r, NEG_INF))
            cb = _dot_nt(cm[:, gs], bm[:, gs])
            y = _dot((cb * seg).astype(BF16), xdt.astype(BF16))
            prev = st_ref[h]
            y = y + _dot(cm[:, gs], prev.astype(BF16)) * jnp.exp(cs_c)
            y = y + dsk[:, sl] * xh
            dec = jnp.exp(cs_last - cs_c)
            st_ref[h] = prev * jnp.exp(cs_last) + _dot_tn(bm[:, gs], (xdt * dec).astype(BF16))
            outs.append(y)
        y = jnp.concatenate(outs, axis=-1) * _silu(z_ref[bb])
        ms = jnp.mean(y * y, axis=-1, keepdims=True)
        o_ref[bb] = (y * lax.rsqrt(ms + EPS) * nw_ref[...]).astype(o_ref.dtype)


def _recurrent_operands(ret_qkv, xbc, misc, z_all, gn_w, conv_w, conv_b, dt_bias, a_log, d_skip, norm_w, nq, tq):
    L = SSM_CHUNK
    assert RET_CHUNK == L and tq % L == 0
    cpt = tq // L
    decay, xi_tab, zeta_tab, cd_tab = _ret_tables()
    dtb = jnp.zeros((1, 128), F32).at[0, DT_COL:DT_COL + HEADS].set(dt_bias)
    a_full = jnp.zeros((1, 128), F32).at[0, DT_COL:DT_COL + HEADS].set(-jnp.exp(a_log))
    dsk = jnp.repeat(d_skip, HEAD_DIM)[None, :]

    def full(shape):
        return pl.BlockSpec(shape, lambda b, i: (0,) * len(shape))

    def blk(width, col):
        return pl.BlockSpec((cpt, L, width), lambda b, i: (b * nq + i, 0, col))

    def chunked(a):
        return a.reshape(-1, L, a.shape[-1])

    qkv3, z3 = chunked(ret_qkv), chunked(z_all)
    ret_in = [decay, xi_tab, zeta_tab, cd_tab, gn_w.reshape(1, GROUP_W), qkv3, qkv3, qkv3, z3]
    ret_specs = [full((HEADS, L, L)), full((L, GROUP_W)), full((L, GROUP_W)), full((1, GROUP_W)),
                 full((1, GROUP_W)), blk(GROUP_W, 0), blk(GROUP_W, 1), blk(GROUP_W, 2), blk(GROUP_W, 2)]
    ssd_in = [conv_w, conv_b.reshape(1, CONV_CH), dtb, a_full, dsk, norm_w.reshape(1, GROUP_W),
              chunked(xbc), chunked(misc), z3]
    ssd_specs = [full((CONV_W, CONV_CH)), full((1, CONV_CH)), full((1, 128)), full((1, 128)),
                 full((1, GROUP_W)), full((1, GROUP_W)), blk(CONV_CH, 0), blk(128, 0), blk(GROUP_W, 3)]
    scratch = [pltpu.VMEM((HEADS, HEAD_DIM, HEAD_DIM), F32),
               pltpu.VMEM((8 + L, CONV_CH), F32),
               pltpu.VMEM((HEADS, SSM_STATE, HEAD_DIM), F32)]
    return ret_in, ssd_in, ret_specs + ssd_specs, blk(GROUP_W, 0), scratch


def _pick_tile(n, pref):
    t = pref
    while n % t:
        t //= 2
    return t


def kernel(x, norm_w, w_in, w_out, nsa_pe_k, nsa_pe_v, nsa_w_ck1, nsa_w_ck2, nsa_w_cv1, nsa_w_cv2,
           diff_lam_q1, diff_lam_k1, diff_lam_q2, diff_lam_k2, diff_subln_w, ret_gn_w,
           ssm_conv_w, ssm_conv_b, ssm_dt_bias, ssm_A_log, ssm_D, ssm_norm_w, final_norm_w):
    batch, seq, _ = x.shape
    depth = w_in.shape[0]
    m = batch * seq
    tm = _pick_tile(m, 512)
    tq = _pick_tile(seq, 512)
    tk = _pick_tile(seq, 256)
    tk_diff = _pick_tile(seq, 512)
    w_in_b = w_in.astype(BF16)
    w_r = _relayout_w_in(w_in_b)
    w_t = _relayout_w_in_t(w_in_b)
    w_out_b = w_out.astype(BF16)
    x2d = x.reshape(m, D_MODEL)
    projected = _in_proj(x2d, norm_w[0], w_r[0], w_t[0], tm)
    for i in range(depth):
        nsa_q, nsa_k2, nsa_cmp, misc, z_all, diff_qk, ret_qkv, xbc, vt_all = projected
        kc, vct = _nsa_compress(nsa_cmp, nsa_pe_k[i], nsa_pe_v[i], nsa_w_ck1[i], nsa_w_ck2[i],
                                nsa_w_cv1[i], nsa_w_cv2[i], batch, seq)
        lam_p = jnp.stack([diff_lam_q1[i], diff_lam_k1[i], diff_lam_q2[i], diff_lam_k2[i]])
        ys = _mixers((nsa_q, nsa_k2, vt_all, kc, vct, misc, z_all),
                     (diff_qk, vt_all, z_all, lam_p, diff_subln_w[i]),
                     (ret_qkv, xbc, misc, z_all, ret_gn_w[i], ssm_conv_w[i], ssm_conv_b[i], ssm_dt_bias[i],
                      ssm_A_log[i], ssm_D[i], ssm_norm_w[i]),
                     batch, seq, i, tq, tk, tk_diff)
        if i + 1 < depth:
            x2d, *projected = _out_in_proj(ys, w_out_b[i], x2d, norm_w[i + 1], w_r[i + 1], w_t[i + 1], tm)
        else:
            x2d = _out_proj(ys, w_out_b[i], x2d, final_norm_w, tm)
    return x2d.reshape(batch, seq, D_MODEL)
```

```python
import functools
import math

import numpy as np
import jax
import jax.numpy as jnp
from jax import lax
from jax.experimental import pallas as pl
from jax.experimental.pallas import tpu as pltpu

F32 = jnp.float32
BF16 = jnp.bfloat16
NEG_INF = float("-inf")
LOG2E = 1.4426950408889634

D_MODEL = 1024
DEPTH = 4
GROUP_W = 256
HEADS = 4
HEAD_DIM = 64
EPS = 1e-6
CMP_LEN = 32
CMP_STRIDE = 16
CMP_HIDDEN = 256
SLC_BLOCK = 64
SLC_SHIFT = 6
SLC_TOPK = 16
WINDOW = 512
DIFF_QK_DIM = 32
RET_CHUNK = 128
SSM_STATE = 128
SSM_CHUNK = 128
CONV_W = 4
CONV_CH = 768
N_ALIBI_HEADS = 8
LANES = 128
BF16_ROWS = 16
ACC_ROWS = HEAD_DIM + BF16_ROWS
QK_LOOKAHEAD = 3
ALIBI_ROWS = 3
POS_RADIX = 256
MASK_BIG = 2.0 ** 100
BLK_LANE0 = 64

IN_LAYOUT = (
    ("nsa_q", 256), ("nsa_k_cmp", 64), ("nsa_v_cmp", 64), ("nsa_k_slc", 64), ("nsa_v_slc", 64),
    ("nsa_k_win", 64), ("nsa_v_win", 64), ("nsa_gate", 12), ("nsa_z", 256),
    ("diff_q", 256), ("diff_k", 256), ("diff_v", 256), ("diff_z", 256),
    ("ret_q", 256), ("ret_k", 256), ("ret_v", 256), ("ret_z", 256),
    ("ssm_z", 256), ("ssm_xbc", 768), ("ssm_dt", 4),
)
IN_OFF = {}
_o = 0
for _n, _w in IN_LAYOUT:
    IN_OFF[_n] = (_o, _w)
    _o += _w
IN_W = _o

GATE_COL = 0
DT_COL = 12
IN_OUTPUTS = (
    ("nsa_q", BF16, ("nsa_q",), 256),
    ("nsa_k2", BF16, ("nsa_k_slc", "nsa_k_win"), 128),
    ("nsa_cmp", F32, ("nsa_k_cmp", "nsa_v_cmp"), 128),
    ("misc", F32, ("nsa_gate", "ssm_dt"), 128),
    ("z_all", F32, ("nsa_z", "diff_z", "ret_z", "ssm_z"), 1024),
    ("diff_qk", BF16, ("diff_q", "diff_k"), 512),
    ("ret_qkv", BF16, ("ret_q", "ret_k", "ret_v"), 768),
    ("xbc", F32, ("ssm_xbc",), 768),
)
IN_T_SRC = ("diff_v", "nsa_v_slc", "nsa_v_win")
IN_T_ROWS = 384
DIFF_VT_BLK = 0
NSA_VT_BLK = 2
IN_SEGS = []
_o = 0
for _n, _dt, _src, _w in IN_OUTPUTS:
    IN_SEGS.append((_o, _o + _w))
    _o += _w
IN_WP = _o

VMEM_LIMIT = 56 * 1024 * 1024


def _alibi_slopes():
    return [2.0 ** (-8.0 * (i + 1) / N_ALIBI_HEADS) for i in range(N_ALIBI_HEADS)]


NSA_SLOPES = _alibi_slopes()[0::2]
DIFF_SLOPES = _alibi_slopes()[1::2]


def _silu(x):
    return x * jax.nn.sigmoid(x)


def _dot(a, b):
    return jnp.dot(a, b, preferred_element_type=F32)


def _dot_nt(a, b):
    return lax.dot_general(a, b, (((1,), (1,)), ((), ())), preferred_element_type=F32)


def _dot_tn(a, b):
    return lax.dot_general(a, b, (((0,), (0,)), ((), ())), preferred_element_type=F32)


def _params(sem):
    return pltpu.CompilerParams(dimension_semantics=sem, vmem_limit_bytes=VMEM_LIMIT)


def _relayout_w_in(w_in):
    cols = []
    for _n, _dt, src, width in IN_OUTPUTS:
        used = 0
        for s in src:
            off, w = IN_OFF[s]
            cols.append(w_in[:, :, off:off + w])
            used += w
        if used < width:
            cols.append(jnp.zeros(w_in.shape[:2] + (width - used,), w_in.dtype))
    return jnp.concatenate(cols, axis=-1)


def _relayout_w_in_t(w_in):
    cols = [w_in[:, :, IN_OFF[s][0]:IN_OFF[s][0] + IN_OFF[s][1]] for s in IN_T_SRC]
    return jnp.swapaxes(jnp.concatenate(cols, axis=-1), 1, 2)


def _norm_project(x, nw_ref, w_ref, wt_ref, out_refs):
    ms = jnp.mean(x * x, axis=-1, keepdims=True)
    h = (x * lax.rsqrt(ms + EPS) * nw_ref[...]).astype(BF16)
    for ref, (a, b) in zip(out_refs[:-1], IN_SEGS):
        ref[...] = _dot(h, w_ref[:, a:b]).astype(ref.dtype)
    out_refs[-1][...] = _dot_nt(wt_ref[...], h).astype(BF16)


def _in_proj_kernel(x_ref, nw_ref, w_ref, wt_ref, *out_refs):
    _norm_project(x_ref[...], nw_ref, w_ref, wt_ref, out_refs)


def _in_proj_outputs(m, tm):
    out_shape = [jax.ShapeDtypeStruct((m, b - a), dt) for (_n, dt, _s, _w), (a, b) in zip(IN_OUTPUTS, IN_SEGS)]
    out_specs = [pl.BlockSpec((tm, b - a), lambda i: (i, 0)) for (a, b) in IN_SEGS]
    out_shape.append(jax.ShapeDtypeStruct((IN_T_ROWS, m), BF16))
    out_specs.append(pl.BlockSpec((IN_T_ROWS, tm), lambda i: (0, i)))
    return out_shape, out_specs


def _in_proj(x2d, norm_w, w_r, w_t, tm):
    m = x2d.shape[0]
    out_shape, out_specs = _in_proj_outputs(m, tm)
    return pl.pallas_call(
        _in_proj_kernel,
        grid=(m // tm,),
        in_specs=[pl.BlockSpec((tm, D_MODEL), lambda i: (i, 0)),
                  pl.BlockSpec((1, D_MODEL), lambda i: (0, 0)),
                  pl.BlockSpec((D_MODEL, IN_WP), lambda i: (0, 0)),
                  pl.BlockSpec((IN_T_ROWS, D_MODEL), lambda i: (0, 0))],
        out_specs=out_specs,
        out_shape=out_shape,
        compiler_params=_params(("parallel",)),
        name="in_proj",
    )(x2d, norm_w.reshape(1, D_MODEL), w_r, w_t)


def _out_proj_kernel(y0_ref, y1_ref, y2_ref, y3_ref, w_ref, x_ref, fw_ref, o_ref):
    acc = x_ref[...]
    for g, y_ref in enumerate((y0_ref, y1_ref, y2_ref, y3_ref)):
        acc = acc + _dot(y_ref[...], w_ref[g * GROUP_W:(g + 1) * GROUP_W, :])
    ms = jnp.mean(acc * acc, axis=-1, keepdims=True)
    o_ref[...] = acc * lax.rsqrt(ms + EPS) * fw_ref[...]


def _out_proj(ys, w_out_b, x2d, final_w, tm):
    m = x2d.shape[0]
    yspec = pl.BlockSpec((tm, GROUP_W), lambda i: (i, 0))
    return pl.pallas_call(
        _out_proj_kernel,
        grid=(m // tm,),
        in_specs=[yspec, yspec, yspec, yspec,
                  pl.BlockSpec((D_MODEL, D_MODEL), lambda i: (0, 0)),
                  pl.BlockSpec((tm, D_MODEL), lambda i: (i, 0)),
                  pl.BlockSpec((1, D_MODEL), lambda i: (0, 0))],
        out_specs=pl.BlockSpec((tm, D_MODEL), lambda i: (i, 0)),
        out_shape=jax.ShapeDtypeStruct((m, D_MODEL), F32),
        compiler_params=_params(("parallel",)),
        name="out_proj",
    )(*ys, w_out_b, x2d, final_w.reshape(1, D_MODEL))


def _out_in_proj_kernel(y0_ref, y1_ref, y2_ref, y3_ref, wo_ref, x_ref, nw_ref, w_ref, wt_ref, xo_ref, *out_refs):
    acc = x_ref[...]
    for g, y_ref in enumerate((y0_ref, y1_ref, y2_ref, y3_ref)):
        acc = acc + _dot(y_ref[...], wo_ref[g * GROUP_W:(g + 1) * GROUP_W, :])
    xo_ref[...] = acc
    _norm_project(acc, nw_ref, w_ref, wt_ref, out_refs)


def _out_in_proj(ys, w_out_b, x2d, norm_w, w_r, w_t, tm):
    m = x2d.shape[0]
    yspec = pl.BlockSpec((tm, GROUP_W), lambda i: (i, 0))
    xspec = pl.BlockSpec((tm, D_MODEL), lambda i: (i, 0))
    out_shape, out_specs = _in_proj_outputs(m, tm)
    return pl.pallas_call(
        _out_in_proj_kernel,
        grid=(m // tm,),
        in_specs=[yspec, yspec, yspec, yspec,
                  pl.BlockSpec((D_MODEL, D_MODEL), lambda i: (0, 0)),
                  xspec,
                  pl.BlockSpec((1, D_MODEL), lambda i: (0, 0)),
                  pl.BlockSpec((D_MODEL, IN_WP), lambda i: (0, 0)),
                  pl.BlockSpec((IN_T_ROWS, D_MODEL), lambda i: (0, 0))],
        out_specs=[xspec] + out_specs,
        out_shape=[jax.ShapeDtypeStruct((m, D_MODEL), F32)] + out_shape,
        compiler_params=_params(("parallel",)),
        name="out_in_proj",
    )(*ys, w_out_b, x2d, norm_w.reshape(1, D_MODEL), w_r, w_t)


def _stage_scores(s_ref, slot, cols, s, keep=None):
    if keep is not None:
        s = jnp.where(keep, s, NEG_INF)
    s_ref[slot, :, cols] = s
    return jnp.max(s, axis=0, keepdims=True)


def _flash_update_t(s, s_max, c1, shift, vt_ext, m_ref, acc_ref, idx, cols):
    m_old = m_ref[idx, :, cols]
    m_new = jnp.maximum(m_old, c1 * s_max + shift)
    alpha = jnp.exp2(m_old - m_new)
    p = jnp.exp2(c1 * s() - (m_new - shift))
    acc_ref[idx, :, cols] = alpha * acc_ref[idx, :, cols] + _dot(vt_ext, p.astype(BF16))
    m_ref[idx, :, cols] = m_new


def _bf16_pieces(x, n=3):
    out = []
    for _ in range(n):
        p = float(np.asarray(x, np.float32).astype(BF16).astype(np.float32))
        out.append(p)
        x = x - p
    return out


def _key_pos_features(tk):
    lane = lax.broadcasted_iota(jnp.int32, (tk, LANES), 1)
    row = lax.broadcasted_iota(jnp.int32, (tk, LANES), 0)
    out = jnp.zeros((tk, LANES), jnp.int32)
    for d in range(-(-tk // POS_RADIX)):
        digit = jnp.bitwise_and(jnp.right_shift(row, 8 * d), POS_RADIX - 1)
        out = jnp.where((lane >= ALIBI_ROWS * d) & (lane < ALIBI_ROWS * (d + 1)), digit, out)
    return out.astype(F32).astype(BF16)


def _alibi_rows(beta, tq, rows=LANES, tk=POS_RADIX):
    row = lax.broadcasted_iota(jnp.int32, (rows, tq), 0)
    out = jnp.zeros((rows, tq), F32)
    for d in range(-(-tk // POS_RADIX)):
        for r, piece in enumerate(_bf16_pieces(beta * POS_RADIX ** d, ALIBI_ROWS)):
            out = jnp.where(row == ALIBI_ROWS * d + r, piece, out)
    return out.astype(BF16)


def _normalized(acc):
    return acc[0:HEAD_DIM] / jnp.maximum(acc[HEAD_DIM:HEAD_DIM + 1], 1e-30)


def _rel_t(tk, tq):
    return lax.broadcasted_iota(jnp.int32, (tk, tq), 1) - lax.broadcasted_iota(jnp.int32, (tk, tq), 0)


def _diff_phases(lam_ref, sw_ref, q_ref, k_ref, vt_ref, z_ref, o_ref, m_ref, acc_ref, wq_ref, s_ref,
                 *, tq, tk, lam_init):
    qi = pl.program_id(1)
    kpq = tq // tk
    scale = DIFF_QK_DIM ** -0.5
    c1 = scale * LOG2E
    m_ref[...] = jnp.full(m_ref.shape, NEG_INF, F32)
    acc_ref[...] = jnp.zeros(acc_ref.shape, F32)
    rel = _rel_t(tk, tq)

    qt = q_ref[...].astype(F32).T.astype(BF16)
    row = lax.broadcasted_iota(jnp.int32, (LANES, tq), 0)
    for j in range(2 * HEADS):
        g, r0 = divmod(j * DIFF_QK_DIM, LANES)
        qg = qt[g * LANES:(g + 1) * LANES]
        wq_ref[j, 0:LANES, :] = jnp.where((row >= r0) & (row < r0 + DIFF_QK_DIM), qg, jnp.zeros_like(qg))
        wq_ref[j, LANES:2 * LANES, :] = _alibi_rows(DIFF_SLOPES[j // 2] / scale, tq, tk=tk)

    ones = jnp.ones((BF16_ROWS, tk), BF16)
    kpos = _key_pos_features(tk)

    n_maps = 2 * HEADS
    n_slots = s_ref.shape[0]

    def tiles(kis, mask_offs):
        loaded = []
        for ki in kis:
            start = pl.multiple_of(ki * tk, tk)
            loaded.append((k_ref[pl.ds(start, tk), :],
                           vt_ref[:, pl.ds(start, tk)],
                           (ki * tk).astype(F32)))
        items = [(t, j) for t in range(len(kis)) for j in range(n_maps)]
        cols = slice(0, tq)
        s_max = {}

        def scores(i):
            t, j = items[i]
            g = (j * DIFF_QK_DIM) // LANES
            lhs = jnp.concatenate([loaded[t][0][:, g * LANES:(g + 1) * LANES], kpos], axis=1)
            keep = None if mask_offs[t] is None else rel >= mask_offs[t]
            s_max[i] = _stage_scores(s_ref, i % n_slots, cols, _dot(lhs, wq_ref[j]), keep)

        for i in range(QK_LOOKAHEAD):
            scores(i)
        for i, (t, j) in enumerate(items):
            h = j // 2
            if i + QK_LOOKAHEAD < len(items):
                scores(i + QK_LOOKAHEAD)
            _k, vt, key0 = loaded[t]
            vt_ext = jnp.concatenate([vt[h * HEAD_DIM:(h + 1) * HEAD_DIM], ones], axis=0)
            _flash_update_t(lambda i=i: s_ref[i % n_slots], s_max.pop(i), c1, (DIFF_SLOPES[h] * LOG2E) * key0,
                            vt_ext, m_ref, acc_ref, j, cols)

    def loop_body(i):
        tiles([i * kpq + d for d in range(kpq)], [None] * kpq)

    def diag_tiles():
        tiles([qi * kpq + d for d in range(kpq)], [d * tk for d in range(kpq)])

    def finish():
        lp = lam_ref[...]
        lam = (jnp.exp(jnp.sum(lp[0:1] * lp[1:2], axis=-1, keepdims=True))
               - jnp.exp(jnp.sum(lp[2:3] * lp[3:4], axis=-1, keepdims=True)) + lam_init)
        sw = sw_ref[...]
        o_t = jnp.concatenate([_normalized(acc_ref[2 * h]) - lam * _normalized(acc_ref[2 * h + 1])
                               for h in range(HEADS)], axis=0)
        o = o_t.T
        outs = []
        for h in range(HEADS):
            oh = o[:, h * HEAD_DIM:(h + 1) * HEAD_DIM]
            ms = jnp.mean(oh * oh, axis=-1, keepdims=True)
            outs.append(oh * lax.rsqrt(ms + EPS) * sw * (1.0 - lam_init))
        y = jnp.concatenate(outs, axis=-1) * _silu(z_ref[...])
        o_ref[...] = y.astype(o_ref.dtype)

    return loop_body, diag_tiles, finish


def _diff_operands(qk, vt_all, z_all, lam_p, subln_w, seq, nq, tq, tk):
    inputs = [lam_p, subln_w.reshape(1, HEAD_DIM), qk, qk, vt_all, z_all]
    in_specs = [pl.BlockSpec((4, DIFF_QK_DIM), lambda b, i: (0, 0)),
                pl.BlockSpec((1, HEAD_DIM), lambda b, i: (0, 0)),
                pl.BlockSpec((tq, GROUP_W), lambda b, i: (b * nq + i, 0)),
                pl.BlockSpec((seq, GROUP_W), lambda b, i: (b, 1)),
                pl.BlockSpec((GROUP_W, seq), lambda b, i: (DIFF_VT_BLK, b)),
                pl.BlockSpec((tq, GROUP_W), lambda b, i: (b * nq + i, 1))]
    scratch = [pltpu.VMEM((2 * HEADS, 1, tq), F32),
               pltpu.VMEM((2 * HEADS, ACC_ROWS, tq), F32),
               pltpu.VMEM((2 * HEADS, 2 * LANES, tq), BF16),
               pltpu.VMEM((QK_LOOKAHEAD + 1, tk, tq), F32)]
    return inputs, in_specs, scratch


def _nsa_compress_kernel(cmp_ref, pe_ref, w1_ref, wk2_ref, wv2t_ref, kc_ref, vct_ref):
    ng = kc_ref.shape[1]
    top = jnp.zeros((ng, 2 * CMP_HIDDEN), F32)
    bot = jnp.zeros((ng, 2 * CMP_HIDDEN), F32)
    for p in range(CMP_STRIDE):
        x = cmp_ref[pl.ds(p, ng, stride=CMP_STRIDE), :]
        top = top + _dot((x + pe_ref[p:p + 1, :]).astype(BF16), w1_ref[p])
        bot = bot + _dot((x + pe_ref[CMP_STRIDE + p:CMP_STRIDE + p + 1, :]).astype(BF16), w1_ref[CMP_STRIDE + p])
    hid = _silu(top + pltpu.roll(bot, ng - 1, 0)).astype(BF16)
    hk, hv = hid[:, 0:CMP_HIDDEN], hid[:, CMP_HIDDEN:]
    kc = _dot(hk, wk2_ref[...])
    lane = lax.broadcasted_iota(jnp.int32, kc.shape, 1)
    blk = lax.broadcasted_iota(jnp.int32, kc.shape, 0).astype(F32)
    kc = jnp.where((lane >= HEAD_DIM) & (lane < HEAD_DIM + ALIBI_ROWS), blk, kc)
    kc_ref[0] = kc.astype(kc_ref.dtype)
    vct_ref[0] = _dot_nt(wv2t_ref[...], hv).astype(vct_ref.dtype)


def _nsa_compress(cmp2d, pe_k, pe_v, w_ck1, w_ck2, w_cv1, w_cv2, batch, seq):
    ng = seq // CMP_STRIDE
    wk = w_ck1.reshape(CMP_LEN, HEAD_DIM, CMP_HIDDEN)
    wv = w_cv1.reshape(CMP_LEN, HEAD_DIM, CMP_HIDDEN)
    w1 = jnp.concatenate([jnp.pad(wk, ((0, 0), (0, 0), (0, CMP_HIDDEN))),
                          jnp.pad(wv, ((0, 0), (0, 0), (CMP_HIDDEN, 0)))], axis=1).astype(BF16)
    pe = jnp.concatenate([pe_k, pe_v], axis=1)

    def full(shape):
        return pl.BlockSpec(shape, lambda b: (0,) * len(shape))

    return pl.pallas_call(
        _nsa_compress_kernel,
        grid=(batch,),
        in_specs=[pl.BlockSpec((seq, LANES), lambda b: (b, 0)), full((CMP_LEN, LANES)),
                  full((CMP_LEN, LANES, 2 * CMP_HIDDEN)), full((CMP_HIDDEN, LANES)),
                  full((HEAD_DIM, CMP_HIDDEN))],
        out_specs=[pl.BlockSpec((1, ng, LANES), lambda b: (b, 0, 0)),
                   pl.BlockSpec((1, HEAD_DIM, ng), lambda b: (b, 0, 0))],
        out_shape=[jax.ShapeDtypeStruct((batch, ng, LANES), BF16),
                   jax.ShapeDtypeStruct((batch, HEAD_DIM, ng), BF16)],
        compiler_params=_params(("parallel",)),
        name="nsa_compress",
    )(cmp2d, pe, w1, jnp.pad(w_ck2, ((0, 0), (0, LANES - HEAD_DIM))).astype(BF16), w_cv2.T.astype(BF16))


def _nsa_phases(ovl_ref, q_ref, k_ref, vt_ref, kc_ref, vct_ref, misc_ref, z_ref, o_ref,
                m_ref, acc_ref, wq_ref, s_ref, sc_ref, *, tq, tk, seq):
    qi = pl.program_id(1)
    kpq = tq // tk
    scale = HEAD_DIM ** -0.5
    c1 = scale * LOG2E
    ng = seq // CMP_STRIDE
    ns = seq // SLC_BLOCK
    top = min(SLC_TOPK, ns)
    m_ref[...] = jnp.full(m_ref.shape, NEG_INF, F32)
    acc_ref[...] = jnp.zeros(acc_ref.shape, F32)
    rel = _rel_t(tk, tq)

    qt = q_ref[...].astype(F32).T.astype(BF16)
    zeros_q = jnp.zeros((HEAD_DIM, tq), BF16)
    for h in range(HEADS):
        qh = qt[h * HEAD_DIM:(h + 1) * HEAD_DIM]
        alibi = _alibi_rows(NSA_SLOPES[h] / scale, tq, tk=tk)
        wq_ref[h, 0:LANES, :] = jnp.concatenate([qh, zeros_q], axis=0)
        wq_ref[h, LANES:2 * LANES, :] = alibi
        wq_ref[HEADS + h, 0:LANES, :] = jnp.concatenate([zeros_q, qh], axis=0)
        wq_ref[HEADS + h, LANES:2 * LANES, :] = alibi

    t_lane = qi * tq + lax.broadcasted_iota(jnp.int32, (1, tq), 1)
    n_sub = lax.broadcasted_iota(jnp.int32, (ng, 1), 0)
    c_valid = (n_sub * CMP_STRIDE + (CMP_LEN - 1)) <= t_lane
    kc = kc_ref[0]
    vct_ext = jnp.concatenate([vct_ref[0], jnp.ones((BF16_ROWS, ng), BF16)], axis=0)
    ovl = ovl_ref[...]
    cmp_max = []
    for h in range(HEADS):
        rhs = jnp.concatenate([qt[h * HEAD_DIM:(h + 1) * HEAD_DIM],
                               _alibi_rows(NSA_SLOPES[h] * CMP_STRIDE / scale, tq, HEAD_DIM)], axis=0)
        cmp_max.append(_stage_scores(sc_ref, h, slice(0, tq), _dot(kc, rhs), c_valid))
    imp = jnp.zeros((ns, tq), F32)
    o_cmp = []
    for h in range(HEADS):
        mx = jnp.where(cmp_max[h] > NEG_INF, cmp_max[h], 0.0)
        pb = jnp.exp2(c1 * sc_ref[h] - c1 * mx).astype(BF16)
        o_ext = _dot(vct_ext, pb)
        r = 1.0 / jnp.maximum(o_ext[HEAD_DIM:HEAD_DIM + 1], 1e-30)
        o_cmp.append(o_ext[0:HEAD_DIM] * r)
        imp = imp + _dot(ovl, pb) * r

    def select_blocks():
        j_sub = lax.broadcasted_iota(jnp.int32, (ns, 1), 0)
        j_sub_f = j_sub.astype(F32)
        cur = jnp.right_shift(t_lane, SLC_SHIFT)
        forced = (j_sub == 0) | (j_sub == cur) | (j_sub == cur - 1)
        valid = (j_sub * SLC_BLOCK) <= t_lane
        score = jnp.where(forced, 1e30, jnp.where(valid, imp, -1.0))
        sel = jnp.zeros((ns, tq), F32)
        for _ in range(top):
            mx = jnp.max(score, axis=0, keepdims=True)
            idx = jnp.min(jnp.where(score == mx, j_sub_f, float(ns)), axis=0, keepdims=True)
            pick = j_sub_f == idx
            sel = jnp.where(pick, 1.0, sel)
            score = jnp.where(pick, -2.0, score)
        unsel = jnp.where(sel > 0.5, 0.0, -MASK_BIG).astype(BF16)
        for h in range(HEADS):
            wq_ref[h, LANES + BLK_LANE0:LANES + BLK_LANE0 + ns, :] = unsel

    aux_lane = lax.broadcasted_iota(jnp.int32, (tk, LANES), 1)
    aux_blk = jnp.right_shift(lax.broadcasted_iota(jnp.int32, (tk, LANES), 0), SLC_SHIFT) + BLK_LANE0
    kpos = _key_pos_features(tk)
    ones = jnp.ones((BF16_ROWS, tk), BF16)

    n_slots = s_ref.shape[0]

    def tiles(window, kis, cols, mask_offs=None):
        base = HEADS if window else 0
        loaded = []
        for t, ki in enumerate(kis):
            start = pl.multiple_of(ki * tk, tk)
            k = k_ref[pl.ds(start, tk), :]
            if window:
                vt = vt_ref[HEAD_DIM:2 * HEAD_DIM, pl.ds(start, tk)]
                aux = kpos
                dist = rel[:, cols[t]] + (qi * tq - ki * tk)
                keep = (dist >= 0) & (dist < WINDOW)
            else:
                vt = vt_ref[0:HEAD_DIM, pl.ds(start, tk)]
                onehot = aux_lane == aux_blk + ki * (tk // SLC_BLOCK)
                aux = jnp.where(onehot, jnp.ones_like(kpos), kpos)
                keep = None
            loaded.append((jnp.concatenate([k, aux], axis=1),
                           jnp.concatenate([vt, ones], axis=0), (ki * tk).astype(F32), keep))
        items = [(t, h) for t in range(len(kis)) for h in range(HEADS)]
        s_max = {}

        def scores(i):
            t, h = items[i]
            if window:
                keep = loaded[t][3]
            else:
                keep = None if mask_offs[t] is None else rel[:, cols[t]] >= mask_offs[t]
            s_max[i] = _stage_scores(s_ref, i % n_slots, cols[t],
                                     _dot(loaded[t][0], wq_ref[base + h, :, cols[t]]), keep)

        for i in range(QK_LOOKAHEAD):
            scores(i)
        for i, (t, h) in enumerate(items):
            if i + QK_LOOKAHEAD < len(items):
                scores(i + QK_LOOKAHEAD)
            _lhs, vt_ext, key0, _keep = loaded[t]
            _flash_update_t(lambda i=i, t=t: s_ref[i % n_slots, :, cols[t]], s_max.pop(i), c1,
                            (NSA_SLOPES[h] * LOG2E) * key0, vt_ext, m_ref, acc_ref, base + h, cols[t])

    all_cols = slice(0, tq)
    diag_cols = [slice(d * tk, tq) for d in range(kpq)]

    tiles(True, [qi * kpq + d for d in range(kpq)], diag_cols)
    select_blocks()

    def loop_body(i):
        tiles(False, [i * kpq + d for d in range(kpq)], [all_cols] * kpq, [None] * kpq)

    def back_window():
        n_back = (WINDOW + tk - 1) // tk
        for g in range((n_back + kpq - 1) // kpq):
            backs = list(range(g * kpq + 1, min((g + 1) * kpq, n_back) + 1))
            back_cols = [slice(0, min(tq, -(-(WINDOW - (back - 1) * tk - 1) // LANES) * LANES)) for back in backs]

            @pl.when(qi > g)
            def _():
                tiles(True, [qi * kpq - back for back in backs], back_cols)

    def diag_tiles():
        tiles(False, [qi * kpq + d for d in range(kpq)], diag_cols, [d * tk for d in range(kpq)])

    def finish():
        g_t = jax.nn.sigmoid(misc_ref[...]).T
        outs = []
        for h in range(HEADS):
            r0 = GATE_COL + 3 * h
            outs.append(g_t[r0:r0 + 1] * o_cmp[h] + g_t[r0 + 1:r0 + 2] * _normalized(acc_ref[h])
                        + g_t[r0 + 2:r0 + 3] * _normalized(acc_ref[HEADS + h]))
        y = jnp.concatenate(outs, axis=0).T * _silu(z_ref[...])
        o_ref[...] = y.astype(o_ref.dtype)

    return loop_body, back_window, diag_tiles, finish


def _overlap_t(seq):
    nc = (seq - CMP_LEN) // CMP_STRIDE + 1
    ng = seq // CMP_STRIDE
    ns = seq // SLC_BLOCK
    c_start = np.arange(ng) * CMP_STRIDE
    c_end = c_start + CMP_LEN - 1
    s_start = np.arange(ns) * SLC_BLOCK
    s_end = s_start + SLC_BLOCK - 1
    ov = (c_start[None, :] <= s_end[:, None]) & (c_end[None, :] >= s_start[:, None]) & (np.arange(ng)[None, :] < nc)
    return jnp.asarray(ov.astype(np.float32), dtype=BF16)


def _nsa_operands(q, k2, vt_all, kc, vct, misc, z_all, seq, nq, tq, tk):
    ng = seq // CMP_STRIDE
    ns = seq // SLC_BLOCK
    assert tq - tk < WINDOW and tk % SLC_BLOCK == 0 and ns <= LANES - BLK_LANE0
    inputs = [_overlap_t(seq), q, k2, vt_all, kc, vct, misc, z_all]
    in_specs = [pl.BlockSpec((ns, ng), lambda b, i: (0, 0)),
                pl.BlockSpec((tq, GROUP_W), lambda b, i: (b * nq + i, 0)),
                pl.BlockSpec((seq, LANES), lambda b, i: (b, 0)),
                pl.BlockSpec((2 * HEAD_DIM, seq), lambda b, i: (NSA_VT_BLK, b)),
                pl.BlockSpec((1, ng, LANES), lambda b, i: (b, 0, 0)),
                pl.BlockSpec((1, HEAD_DIM, ng), lambda b, i: (b, 0, 0)),
                pl.BlockSpec((tq, LANES), lambda b, i: (b * nq + i, 0)),
                pl.BlockSpec((tq, GROUP_W), lambda b, i: (b * nq + i, 0))]
    scratch = [pltpu.VMEM((2 * HEADS, 1, tq), F32),
               pltpu.VMEM((2 * HEADS, ACC_ROWS, tq), F32),
               pltpu.VMEM((2 * HEADS, 2 * LANES, tq), BF16),
               pltpu.VMEM((2 * HEADS, tk, tq), F32),
               pltpu.VMEM((HEADS, ng, tq), F32)]
    return inputs, in_specs, scratch


def _mixers_kernel(*refs, n_in, n_scratch, tq, tk_nsa, tk_diff, seq, lam_init):
    bounds = np.cumsum([0] + list(n_in))
    nsa_in, diff_in, ret_in, ssd_in = (refs[a:b] for a, b in zip(bounds[:-1], bounds[1:]))
    nsa_out, diff_out, ret_out, ssd_out = refs[bounds[-1]:bounds[-1] + 4]
    sb = np.cumsum([0] + list(n_scratch)) + bounds[-1] + 4
    nsa_scr, diff_scr, rec_scr = (refs[a:b] for a, b in zip(sb[:-1], sb[1:]))
    ret_st, ssd_ext, ssd_st = rec_scr
    qi = pl.program_id(1)

    @pl.when(qi == 0)
    def _():
        ret_st[...] = jnp.zeros(ret_st.shape, F32)
        ssd_st[...] = jnp.zeros(ssd_st.shape, F32)
        ssd_ext[0:8, :] = jnp.zeros((8, CONV_CH), F32)

    chunks = list(range(ret_out.shape[0]))
    first, second = chunks[:len(chunks) // 2], chunks[len(chunks) // 2:]
    nsa_loop, nsa_back_window, nsa_diag, nsa_finish = _nsa_phases(
        *nsa_in, nsa_out, *nsa_scr, tq=tq, tk=tk_nsa, seq=seq)
    diff_loop, diff_diag, diff_finish = _diff_phases(
        *diff_in, diff_out, *diff_scr, tq=tq, tk=tk_diff, lam_init=lam_init)
    _ret_body(*ret_in, ret_out, ret_st, first)
    _ssd_body(*ssd_in, ssd_out, ssd_ext, ssd_st, first)

    def body(i, carry):
        nsa_loop(i)
        diff_loop(i)
        return carry

    lax.fori_loop(0, qi, body, 0)
    nsa_back_window()
    nsa_diag()
    diff_diag()
    _ret_body(*ret_in, ret_out, ret_st, second)
    _ssd_body(*ssd_in, ssd_out, ssd_ext, ssd_st, second)
    nsa_finish()
    diff_finish()


def _mixers(nsa_args, diff_args, rec_args, batch, seq, layer_idx, tq, tk_nsa, tk_diff):
    nq = seq // tq
    nsa_in, nsa_specs, nsa_scratch = _nsa_operands(*nsa_args, seq, nq, tq, tk_nsa)
    diff_in, diff_specs, diff_scratch = _diff_operands(*diff_args, seq, nq, tq, tk_diff)
    ret_in, ssd_in, rec_specs, rec_out_spec, rec_scratch = _recurrent_operands(*rec_args, nq, tq)
    out_spec = pl.BlockSpec((tq, GROUP_W), lambda b, i: (b * nq + i, 0))
    m = batch * seq
    y_nsa, y_diff, y_ret, y_ssm = pl.pallas_call(
        functools.partial(_mixers_kernel, n_in=(len(nsa_in), len(diff_in), len(ret_in), len(ssd_in)),
                          n_scratch=(len(nsa_scratch), len(diff_scratch), len(rec_scratch)),
                          tq=tq, tk_nsa=tk_nsa, tk_diff=tk_diff, seq=seq,
                          lam_init=0.8 - 0.6 * math.exp(-0.3 * layer_idx)),
        grid=(batch, nq),
        in_specs=nsa_specs + diff_specs + rec_specs,
        out_specs=[out_spec, out_spec, rec_out_spec, rec_out_spec],
        out_shape=[jax.ShapeDtypeStruct((m, GROUP_W), BF16)] * 2
                  + [jax.ShapeDtypeStruct((m // SSM_CHUNK, SSM_CHUNK, GROUP_W), BF16)] * 2,
        scratch_shapes=nsa_scratch + diff_scratch + rec_scratch,
        compiler_params=_params(("parallel", "arbitrary")),
        name="mixers",
    )(*nsa_in, *diff_in, *ret_in, *ssd_in)
    return y_nsa, y_diff, y_ret.reshape(m, GROUP_W), y_ssm.reshape(m, GROUP_W)


def _ret_tables():
    c = RET_CHUNK
    h = np.arange(HEADS, dtype=np.float32)
    log_g = jnp.log(1.0 - 2.0 ** (-5.0 - jnp.asarray(h)))
    pos = jnp.arange(c, dtype=F32)
    rel = pos[:, None] - pos[None, :]
    decay = jnp.where(rel >= 0, jnp.exp(log_g[:, None, None] * jnp.maximum(rel, 0.0)), 0.0)
    xi = jnp.exp(log_g[:, None] * (pos + 1.0))
    zeta = jnp.exp(log_g[:, None] * (c - 1.0 - pos))
    chunk_decay = jnp.exp(log_g * c)
    xi_tab = jnp.repeat(xi.T, HEAD_DIM, axis=1)
    zeta_tab = jnp.repeat(zeta.T, HEAD_DIM, axis=1)
    cd_tab = jnp.repeat(chunk_decay, HEAD_DIM)[None, :]
    return decay, xi_tab, zeta_tab, cd_tab


def _ret_body(decay_ref, xi_ref, zeta_ref, cd_ref, gn_ref, q_ref, k_ref, v_ref, z_ref, o_ref, st_ref, blocks):
    xi = xi_ref[...]
    cd = cd_ref[...]
    for bb in blocks:
        q = (q_ref[bb].astype(F32) * (HEAD_DIM ** -0.5)).astype(BF16)
        k = k_ref[bb]
        v = v_ref[bb]
        kz_t = (k.astype(F32) * zeta_ref[...]).T.astype(BF16)
        outs = []
        for h in range(HEADS):
            sl = slice(h * HEAD_DIM, (h + 1) * HEAD_DIM)
            qh, kh, vh = q[:, sl], k[:, sl], v[:, sl]
            prev = st_ref[h]
            inner = (_dot_nt(qh, kh) * decay_ref[h]).astype(BF16)
            o = _dot(inner, vh) + _dot(qh, prev.astype(BF16)) * xi[:, sl]
            st_ref[h] = prev * cd[:, sl] + _dot(kz_t[sl, :], vh)
            mu = jnp.mean(o, axis=-1, keepdims=True)
            d = o - mu
            var = jnp.mean(d * d, axis=-1, keepdims=True)
            outs.append(d * lax.rsqrt(var + EPS))
        y = jnp.concatenate(outs, axis=-1) * gn_ref[...] * _silu(z_ref[bb])
        o_ref[bb] = y.astype(o_ref.dtype)


def _ssd_body(cw_ref, cb_ref, dtb_ref, a_ref, dsk_ref, nw_ref, xbc_ref, misc_ref, z_ref, o_ref,
              ext_ref, st_ref, blocks):
    L = SSM_CHUNK
    hi = lax.Precision.HIGHEST
    row = lax.broadcasted_iota(jnp.int32, (L, L), 0)
    col = lax.broadcasted_iota(jnp.int32, (L, L), 1)
    causal = row >= col
    tril = jnp.where(causal, 1.0, 0.0).astype(F32)
    dsk = dsk_ref[...]

    for bb in blocks:
        raw = xbc_ref[bb]
        ext_ref[8:8 + L, :] = raw
        conv = cb_ref[...] + raw * cw_ref[CONV_W - 1:CONV_W, :]
        for w in range(CONV_W - 1):
            shift = CONV_W - 1 - w
            conv = conv + ext_ref[8 - shift:8 - shift + L, :] * cw_ref[w:w + 1, :]
        ext_ref[0:8, :] = raw[L - 8:L, :]
        xc = _silu(conv)
        x = xc[:, 0:GROUP_W]
        bm = xc[:, GROUP_W:GROUP_W + 2 * SSM_STATE].astype(BF16)
        cm = xc[:, GROUP_W + 2 * SSM_STATE:].astype(BF16)

        dt_full = jax.nn.softplus(misc_ref[bb] + dtb_ref[...])
        da = dt_full * a_ref[...]
        cs_col = jnp.dot(tril, da, precision=hi, preferred_element_type=F32)
        cs_row = lax.dot_general(da, tril, (((0,), (1,)), ((), ())), precision=hi,
                                 preferred_element_type=F32)

        outs = []
        for h in range(HEADS):
            g = h // 2
            c0 = DT_COL + h
            sl = slice(h * HEAD_DIM, (h + 1) * HEAD_DIM)
            gs = slice(g * SSM_STATE, (g + 1) * SSM_STATE)
            cs_c = cs_col[:, c0:c0 + 1]
            cs_r = cs_row[c0:c0 + 1, :]
            cs_last = cs_col[L - 1:L, c0:c0 + 1]
            xh = x[:, sl]
            xdt = xh * dt_full[:, c0:c0 + 1]
            seg = jnp.exp(jnp.where(causal, cs_c - cs_r, NEG_INF))
            cb = _dot_nt(cm[:, gs], bm[:, gs])
            y = _dot((cb * seg).astype(BF16), xdt.astype(BF16))
            prev = st_ref[h]
            y = y + _dot(cm[:, gs], prev.astype(BF16)) * jnp.exp(cs_c)
            y = y + dsk[:, sl] * xh
            dec = jnp.exp(cs_last - cs_c)
            st_ref[h] = prev * jnp.exp(cs_last) + _dot_tn(bm[:, gs], (xdt * dec).astype(BF16))
            outs.append(y)
        y = jnp.concatenate(outs, axis=-1) * _silu(z_ref[bb])
        ms = jnp.mean(y * y, axis=-1, keepdims=True)
        o_ref[bb] = (y * lax.rsqrt(ms + EPS) * nw_ref[...]).astype(o_ref.dtype)


def _recurrent_operands(ret_qkv, xbc, misc, z_all, gn_w, conv_w, conv_b, dt_bias, a_log, d_skip, norm_w, nq, tq):
    L = SSM_CHUNK
    assert RET_CHUNK == L and tq % L == 0
    cpt = tq // L
    decay, xi_tab, zeta_tab, cd_tab = _ret_tables()
    dtb = jnp.zeros((1, 128), F32).at[0, DT_COL:DT_COL + HEADS].set(dt_bias)
    a_full = jnp.zeros((1, 128), F32).at[0, DT_COL:DT_COL + HEADS].set(-jnp.exp(a_log))
    dsk = jnp.repeat(d_skip, HEAD_DIM)[None, :]

    def full(shape):
        return pl.BlockSpec(shape, lambda b, i: (0,) * len(shape))

    def blk(width, col):
        return pl.BlockSpec((cpt, L, width), lambda b, i: (b * nq + i, 0, col))

    def chunked(a):
        return a.reshape(-1, L, a.shape[-1])

    qkv3, z3 = chunked(ret_qkv), chunked(z_all)
    ret_in = [decay, xi_tab, zeta_tab, cd_tab, gn_w.reshape(1, GROUP_W), qkv3, qkv3, qkv3, z3]
    ret_specs = [full((HEADS, L, L)), full((L, GROUP_W)), full((L, GROUP_W)), full((1, GROUP_W)),
                 full((1, GROUP_W)), blk(GROUP_W, 0), blk(GROUP_W, 1), blk(GROUP_W, 2), blk(GROUP_W, 2)]
    ssd_in = [conv_w, conv_b.reshape(1, CONV_CH), dtb, a_full, dsk, norm_w.reshape(1, GROUP_W),
              chunked(xbc), chunked(misc), z3]
    ssd_specs = [full((CONV_W, CONV_CH)), full((1, CONV_CH)), full((1, 128)), full((1, 128)),
                 full((1, GROUP_W)), full((1, GROUP_W)), blk(CONV_CH, 0), blk(128, 0), blk(GROUP_W, 3)]
    scratch = [pltpu.VMEM((HEADS, HEAD_DIM, HEAD_DIM), F32),
               pltpu.VMEM((8 + L, CONV_CH), F32),
               pltpu.VMEM((HEADS, SSM_STATE, HEAD_DIM), F32)]
    return ret_in, ssd_in, ret_specs + ssd_specs, blk(GROUP_W, 0), scratch


def _pick_tile(n, pref):
    t = pref
    while n % t:
        t //= 2
    return t


def kernel(x, norm_w, w_in, w_out, nsa_pe_k, nsa_pe_v, nsa_w_ck1, nsa_w_ck2, nsa_w_cv1, nsa_w_cv2,
           diff_lam_q1, diff_lam_k1, diff_lam_q2, diff_lam_k2, diff_subln_w, ret_gn_w,
           ssm_conv_w, ssm_conv_b, ssm_dt_bias, ssm_A_log, ssm_D, ssm_norm_w, final_norm_w):
    batch, seq, _ = x.shape
    depth = w_in.shape[0]
    m = batch * seq
    tm = _pick_tile(m, 512)
    tq = _pick_tile(seq, 512)
    tk = _pick_tile(seq, 256)
    tk_diff = _pick_tile(seq, 512)
    w_in_b = w_in.astype(BF16)
    w_r = _relayout_w_in(w_in_b)
    w_t = _relayout_w_in_t(w_in_b)
    w_out_b = w_out.astype(BF16)
    x2d = x.reshape(m, D_MODEL)
    projected = _in_proj(x2d, norm_w[0], w_r[0], w_t[0], tm)
    for i in range(depth):
        nsa_q, nsa_k2, nsa_cmp, misc, z_all, diff_qk, ret_qkv, xbc, vt_all = projected
        kc, vct = _nsa_compress(nsa_cmp, nsa_pe_k[i], nsa_pe_v[i], nsa_w_ck1[i], nsa_w_ck2[i],
                                nsa_w_cv1[i], nsa_w_cv2[i], batch, seq)
        lam_p = jnp.stack([diff_lam_q1[i], diff_lam_k1[i], diff_lam_q2[i], diff_lam_k2[i]])
        ys = _mixers((nsa_q, nsa_k2, vt_all, kc, vct, misc, z_all),
                     (diff_qk, vt_all, z_all, lam_p, diff_subln_w[i]),
                     (ret_qkv, xbc, misc, z_all, ret_gn_w[i], ssm_conv_w[i], ssm_conv_b[i], ssm_dt_bias[i],
                      ssm_A_log[i], ssm_D[i], ssm_norm_w[i]),
                     batch, seq, i, tq, tk, tk_diff)
        if i + 1 < depth:
            x2d, *projected = _out_in_proj(ys, w_out_b[i], x2d, norm_w[i + 1], w_r[i + 1], w_t[i + 1], tm)
        else:
            x2d = _out_proj(ys, w_out_b[i], x2d, final_norm_w, tm)
    return x2d.reshape(batch, seq, D_MODEL)
```

```python
import functools
import math

import numpy as np
import jax
import jax.numpy as jnp
from jax import lax
from jax.experimental import pallas as pl
from jax.experimental.pallas import tpu as pltpu

F32 = jnp.float32
BF16 = jnp.bfloat16
NEG_INF = float("-inf")
LOG2E = 1.4426950408889634

D_MODEL = 1024
DEPTH = 4
GROUP_W = 256
HEADS = 4
HEAD_DIM = 64
EPS = 1e-6
CMP_LEN = 32
CMP_STRIDE = 16
CMP_HIDDEN = 256
SLC_BLOCK = 64
SLC_SHIFT = 6
SLC_TOPK = 16
WINDOW = 512
DIFF_QK_DIM = 32
RET_CHUNK = 128
SSM_STATE = 128
SSM_CHUNK = 128
CONV_W = 4
CONV_CH = 768
N_ALIBI_HEADS = 8
LANES = 128
BF16_ROWS = 16
ACC_ROWS = HEAD_DIM + BF16_ROWS
QK_LOOKAHEAD = 3
ALIBI_ROWS = 3
POS_RADIX = 256
MASK_BIG = 2.0 ** 100
BLK_LANE0 = 64

IN_LAYOUT = (
    ("nsa_q", 256), ("nsa_k_cmp", 64), ("nsa_v_cmp", 64), ("nsa_k_slc", 64), ("nsa_v_slc", 64),
    ("nsa_k_win", 64), ("nsa_v_win", 64), ("nsa_gate", 12), ("nsa_z", 256),
    ("diff_q", 256), ("diff_k", 256), ("diff_v", 256), ("diff_z", 256),
    ("ret_q", 256), ("ret_k", 256), ("ret_v", 256), ("ret_z", 256),
    ("ssm_z", 256), ("ssm_xbc", 768), ("ssm_dt", 4),
)
IN_OFF = {}
_o = 0
for _n, _w in IN_LAYOUT:
    IN_OFF[_n] = (_o, _w)
    _o += _w
IN_W = _o

GATE_COL = 0
DT_COL = 12
IN_OUTPUTS = (
    ("nsa_q", BF16, ("nsa_q",), 256),
    ("nsa_k2", BF16, ("nsa_k_slc", "nsa_k_win"), 128),
    ("nsa_cmp", F32, ("nsa_k_cmp", "nsa_v_cmp"), 128),
    ("misc", F32, ("nsa_gate", "ssm_dt"), 128),
    ("z_all", F32, ("nsa_z", "diff_z", "ret_z", "ssm_z"), 1024),
    ("diff_qk", BF16, ("diff_q", "diff_k"), 512),
    ("ret_qkv", BF16, ("ret_q", "ret_k", "ret_v"), 768),
    ("xbc", F32, ("ssm_xbc",), 768),
)
IN_T_SRC = ("diff_v", "nsa_v_slc", "nsa_v_win")
IN_T_ROWS = 384
DIFF_VT_BLK = 0
NSA_VT_BLK = 2
IN_SEGS = []
_o = 0
for _n, _dt, _src, _w in IN_OUTPUTS:
    IN_SEGS.append((_o, _o + _w))
    _o += _w
IN_WP = _o

VMEM_LIMIT = 56 * 1024 * 1024


def _alibi_slopes():
    return [2.0 ** (-8.0 * (i + 1) / N_ALIBI_HEADS) for i in range(N_ALIBI_HEADS)]


NSA_SLOPES = _alibi_slopes()[0::2]
DIFF_SLOPES = _alibi_slopes()[1::2]


def _silu(x):
    return x * jax.nn.sigmoid(x)


def _dot(a, b):
    return jnp.dot(a, b, preferred_element_type=F32)


def _dot_nt(a, b):
    return lax.dot_general(a, b, (((1,), (1,)), ((), ())), preferred_element_type=F32)


def _dot_tn(a, b):
    return lax.dot_general(a, b, (((0,), (0,)), ((), ())), preferred_element_type=F32)


def _params(sem):
    return pltpu.CompilerParams(dimension_semantics=sem, vmem_limit_bytes=VMEM_LIMIT)


def _relayout_w_in(w_in):
    cols = []
    for _n, _dt, src, width in IN_OUTPUTS:
        used = 0
        for s in src:
            off, w = IN_OFF[s]
            cols.append(w_in[:, :, off:off + w])
            used += w
        if used < width:
            cols.append(jnp.zeros(w_in.shape[:2] + (width - used,), w_in.dtype))
    return jnp.concatenate(cols, axis=-1)


def _relayout_w_in_t(w_in):
    cols = [w_in[:, :, IN_OFF[s][0]:IN_OFF[s][0] + IN_OFF[s][1]] for s in IN_T_SRC]
    return jnp.swapaxes(jnp.concatenate(cols, axis=-1), 1, 2)


def _norm_project(x, nw_ref, w_ref, wt_ref, out_refs):
    ms = jnp.mean(x * x, axis=-1, keepdims=True)
    h = (x * lax.rsqrt(ms + EPS) * nw_ref[...]).astype(BF16)
    for ref, (a, b) in zip(out_refs[:-1], IN_SEGS):
        ref[...] = _dot(h, w_ref[:, a:b]).astype(ref.dtype)
    out_refs[-1][...] = _dot_nt(wt_ref[...], h).astype(BF16)


def _in_proj_kernel(x_ref, nw_ref, w_ref, wt_ref, *out_refs):
    _norm_project(x_ref[...], nw_ref, w_ref, wt_ref, out_refs)


def _in_proj_outputs(m, tm):
    out_shape = [jax.ShapeDtypeStruct((m, b - a), dt) for (_n, dt, _s, _w), (a, b) in zip(IN_OUTPUTS, IN_SEGS)]
    out_specs = [pl.BlockSpec((tm, b - a), lambda i: (i, 0)) for (a, b) in IN_SEGS]
    out_shape.append(jax.ShapeDtypeStruct((IN_T_ROWS, m), BF16))
    out_specs.append(pl.BlockSpec((IN_T_ROWS, tm), lambda i: (0, i)))
    return out_shape, out_specs


def _layer_spec(shape, layer):
    return pl.BlockSpec((None,) + shape, lambda i: (layer,) + (0,) * len(shape))


def _in_proj(x2d, norm_w, w_r, w_t, layer, tm):
    m = x2d.shape[0]
    out_shape, out_specs = _in_proj_outputs(m, tm)
    return pl.pallas_call(
        _in_proj_kernel,
        grid=(m // tm,),
        in_specs=[pl.BlockSpec((tm, D_MODEL), lambda i: (i, 0)),
                  pl.BlockSpec((1, D_MODEL), lambda i: (0, 0)),
                  _layer_spec((D_MODEL, IN_WP), layer),
                  _layer_spec((IN_T_ROWS, D_MODEL), layer)],
        out_specs=out_specs,
        out_shape=out_shape,
        compiler_params=_params(("parallel",)),
        name="in_proj",
    )(x2d, norm_w.reshape(1, D_MODEL), w_r, w_t)


def _out_proj_kernel(y0_ref, y1_ref, y2_ref, y3_ref, w_ref, x_ref, fw_ref, o_ref):
    acc = x_ref[...]
    for g, y_ref in enumerate((y0_ref, y1_ref, y2_ref, y3_ref)):
        acc = acc + _dot(y_ref[...], w_ref[g * GROUP_W:(g + 1) * GROUP_W, :])
    ms = jnp.mean(acc * acc, axis=-1, keepdims=True)
    o_ref[...] = acc * lax.rsqrt(ms + EPS) * fw_ref[...]


def _out_proj(ys, w_out_b, layer, x2d, final_w, tm):
    m = x2d.shape[0]
    yspec = pl.BlockSpec((tm, GROUP_W), lambda i: (i, 0))
    return pl.pallas_call(
        _out_proj_kernel,
        grid=(m // tm,),
        in_specs=[yspec, yspec, yspec, yspec,
                  _layer_spec((D_MODEL, D_MODEL), layer),
                  pl.BlockSpec((tm, D_MODEL), lambda i: (i, 0)),
                  pl.BlockSpec((1, D_MODEL), lambda i: (0, 0))],
        out_specs=pl.BlockSpec((tm, D_MODEL), lambda i: (i, 0)),
        out_shape=jax.ShapeDtypeStruct((m, D_MODEL), F32),
        compiler_params=_params(("parallel",)),
        name="out_proj",
    )(*ys, w_out_b, x2d, final_w.reshape(1, D_MODEL))


def _out_in_proj_kernel(y0_ref, y1_ref, y2_ref, y3_ref, wo_ref, x_ref, nw_ref, w_ref, wt_ref, xo_ref, *out_refs):
    acc = x_ref[...]
    for g, y_ref in enumerate((y0_ref, y1_ref, y2_ref, y3_ref)):
        acc = acc + _dot(y_ref[...], wo_ref[g * GROUP_W:(g + 1) * GROUP_W, :])
    xo_ref[...] = acc
    _norm_project(acc, nw_ref, w_ref, wt_ref, out_refs)


def _out_in_proj(ys, w_out_b, layer, x2d, norm_w, w_r, w_t, tm):
    m = x2d.shape[0]
    yspec = pl.BlockSpec((tm, GROUP_W), lambda i: (i, 0))
    xspec = pl.BlockSpec((tm, D_MODEL), lambda i: (i, 0))
    out_shape, out_specs = _in_proj_outputs(m, tm)
    return pl.pallas_call(
        _out_in_proj_kernel,
        grid=(m // tm,),
        in_specs=[yspec, yspec, yspec, yspec,
                  _layer_spec((D_MODEL, D_MODEL), layer),
                  xspec,
                  pl.BlockSpec((1, D_MODEL), lambda i: (0, 0)),
                  _layer_spec((D_MODEL, IN_WP), layer + 1),
                  _layer_spec((IN_T_ROWS, D_MODEL), layer + 1)],
        out_specs=[xspec] + out_specs,
        out_shape=[jax.ShapeDtypeStruct((m, D_MODEL), F32)] + out_shape,
        compiler_params=_params(("parallel",)),
        name="out_in_proj",
    )(*ys, w_out_b, x2d, norm_w.reshape(1, D_MODEL), w_r, w_t)


def _stage_scores(s_ref, slot, cols, s, keep=None):
    if keep is not None:
        s = jnp.where(keep, s, NEG_INF)
    s_ref[slot, :, cols] = s
    return jnp.max(s, axis=0, keepdims=True)


def _flash_update_t(s, s_max, c1, shift, vt_ext, m_ref, acc_ref, idx, cols):
    m_old = m_ref[idx, :, cols]
    m_new = jnp.maximum(m_old, c1 * s_max + shift)
    alpha = jnp.exp2(m_old - m_new)
    p = jnp.exp2(c1 * s() - (m_new - shift))
    acc_ref[idx, :, cols] = alpha * acc_ref[idx, :, cols] + _dot(vt_ext, p.astype(BF16))
    m_ref[idx, :, cols] = m_new


def _bf16_pieces(x, n=3):
    out = []
    for _ in range(n):
        p = float(np.asarray(x, np.float32).astype(BF16).astype(np.float32))
        out.append(p)
        x = x - p
    return out


def _key_pos_features(tk):
    lane = lax.broadcasted_iota(jnp.int32, (tk, LANES), 1)
    row = lax.broadcasted_iota(jnp.int32, (tk, LANES), 0)
    out = jnp.zeros((tk, LANES), jnp.int32)
    for d in range(-(-tk // POS_RADIX)):
        digit = jnp.bitwise_and(jnp.right_shift(row, 8 * d), POS_RADIX - 1)
        out = jnp.where((lane >= ALIBI_ROWS * d) & (lane < ALIBI_ROWS * (d + 1)), digit, out)
    return out.astype(F32).astype(BF16)


def _alibi_rows(beta, tq, rows=LANES, tk=POS_RADIX):
    row = lax.broadcasted_iota(jnp.int32, (rows, tq), 0)
    out = jnp.zeros((rows, tq), F32)
    for d in range(-(-tk // POS_RADIX)):
        for r, piece in enumerate(_bf16_pieces(beta * POS_RADIX ** d, ALIBI_ROWS)):
            out = jnp.where(row == ALIBI_ROWS * d + r, piece, out)
    return out.astype(BF16)


def _normalized(acc):
    return acc[0:HEAD_DIM] / jnp.maximum(acc[HEAD_DIM:HEAD_DIM + 1], 1e-30)


def _rel_t(tk, tq):
    return lax.broadcasted_iota(jnp.int32, (tk, tq), 1) - lax.broadcasted_iota(jnp.int32, (tk, tq), 0)


def _diff_phases(lam_ref, sw_ref, q_ref, k_ref, vt_ref, z_ref, o_ref, m_ref, acc_ref, wq_ref, s_ref,
                 *, tq, tk, lam_init):
    qi = pl.program_id(1)
    kpq = tq // tk
    scale = DIFF_QK_DIM ** -0.5
    c1 = scale * LOG2E
    m_ref[...] = jnp.full(m_ref.shape, NEG_INF, F32)
    acc_ref[...] = jnp.zeros(acc_ref.shape, F32)
    rel = _rel_t(tk, tq)

    qt = q_ref[...].astype(F32).T.astype(BF16)
    row = lax.broadcasted_iota(jnp.int32, (LANES, tq), 0)
    for j in range(2 * HEADS):
        g, r0 = divmod(j * DIFF_QK_DIM, LANES)
        qg = qt[g * LANES:(g + 1) * LANES]
        wq_ref[j, 0:LANES, :] = jnp.where((row >= r0) & (row < r0 + DIFF_QK_DIM), qg, jnp.zeros_like(qg))
        wq_ref[j, LANES:2 * LANES, :] = _alibi_rows(DIFF_SLOPES[j // 2] / scale, tq, tk=tk)

    ones = jnp.ones((BF16_ROWS, tk), BF16)
    kpos = _key_pos_features(tk)

    n_maps = 2 * HEADS
    n_slots = s_ref.shape[0]

    def tiles(kis, mask_offs):
        loaded = []
        for ki in kis:
            start = pl.multiple_of(ki * tk, tk)
            loaded.append((k_ref[pl.ds(start, tk), :],
                           vt_ref[:, pl.ds(start, tk)],
                           (ki * tk).astype(F32)))
        items = [(t, j) for t in range(len(kis)) for j in range(n_maps)]
        cols = slice(0, tq)

        def scores(i):
            t, j = items[i]
            g = (j * DIFF_QK_DIM) // LANES
            lhs = jnp.concatenate([loaded[t][0][:, g * LANES:(g + 1) * LANES], kpos], axis=1)
            s_ref[i % n_slots] = _dot(lhs, wq_ref[j])

        def staged(i):
            t = items[i][0]
            if mask_offs[t] is None:
                return lambda: s_ref[i % n_slots]
            return lambda: jnp.where(rel >= mask_offs[t], s_ref[i % n_slots], NEG_INF)

        for i in range(QK_LOOKAHEAD):
            scores(i)
        for i, (t, j) in enumerate(items):
            h = j // 2
            if i + QK_LOOKAHEAD < len(items):
                scores(i + QK_LOOKAHEAD)
            _k, vt, key0 = loaded[t]
            vt_ext = jnp.concatenate([vt[h * HEAD_DIM:(h + 1) * HEAD_DIM], ones], axis=0)
            s = staged(i)
            _flash_update_t(s, jnp.max(s(), axis=0, keepdims=True), c1, (DIFF_SLOPES[h] * LOG2E) * key0,
                            vt_ext, m_ref, acc_ref, j, cols)

    def loop_body(i):
        tiles([i * kpq + d for d in range(kpq)], [None] * kpq)

    def diag_tiles():
        tiles([qi * kpq + d for d in range(kpq)], [d * tk for d in range(kpq)])

    def finish():
        lp = lam_ref[...]
        lam = (jnp.exp(jnp.sum(lp[0:1] * lp[1:2], axis=-1, keepdims=True))
               - jnp.exp(jnp.sum(lp[2:3] * lp[3:4], axis=-1, keepdims=True)) + lam_init)
        sw = sw_ref[...]
        o_t = jnp.concatenate([_normalized(acc_ref[2 * h]) - lam * _normalized(acc_ref[2 * h + 1])
                               for h in range(HEADS)], axis=0)
        o = o_t.T
        outs = []
        for h in range(HEADS):
            oh = o[:, h * HEAD_DIM:(h + 1) * HEAD_DIM]
            ms = jnp.mean(oh * oh, axis=-1, keepdims=True)
            outs.append(oh * lax.rsqrt(ms + EPS) * sw * (1.0 - lam_init))
        y = jnp.concatenate(outs, axis=-1) * _silu(z_ref[...])
        o_ref[...] = y.astype(o_ref.dtype)

    return loop_body, diag_tiles, finish


def _diff_operands(qk, vt_all, z_all, lam_p, subln_w, seq, nq, tq, tk):
    inputs = [lam_p, subln_w.reshape(1, HEAD_DIM), qk, qk, vt_all, z_all]
    in_specs = [pl.BlockSpec((4, DIFF_QK_DIM), lambda b, i: (0, 0)),
                pl.BlockSpec((1, HEAD_DIM), lambda b, i: (0, 0)),
                pl.BlockSpec((tq, GROUP_W), lambda b, i: (b * nq + i, 0)),
                pl.BlockSpec((seq, GROUP_W), lambda b, i: (b, 1)),
                pl.BlockSpec((GROUP_W, seq), lambda b, i: (DIFF_VT_BLK, b)),
                pl.BlockSpec((tq, GROUP_W), lambda b, i: (b * nq + i, 1))]
    scratch = [pltpu.VMEM((2 * HEADS, 1, tq), F32),
               pltpu.VMEM((2 * HEADS, ACC_ROWS, tq), F32),
               pltpu.VMEM((2 * HEADS, 2 * LANES, tq), BF16),
               pltpu.VMEM((QK_LOOKAHEAD + 1, tk, tq), F32)]
    return inputs, in_specs, scratch


def _nsa_compress_kernel(cmp_ref, pe_ref, w1_ref, wk2_ref, wv2t_ref, kc_ref, vct_ref):
    ng = kc_ref.shape[1]
    top = jnp.zeros((ng, 2 * CMP_HIDDEN), F32)
    bot = jnp.zeros((ng, 2 * CMP_HIDDEN), F32)
    for p in range(CMP_STRIDE):
        x = cmp_ref[pl.ds(p, ng, stride=CMP_STRIDE), :]
        top = top + _dot((x + pe_ref[p:p + 1, :]).astype(BF16), w1_ref[p])
        bot = bot + _dot((x + pe_ref[CMP_STRIDE + p:CMP_STRIDE + p + 1, :]).astype(BF16), w1_ref[CMP_STRIDE + p])
    hid = _silu(top + pltpu.roll(bot, ng - 1, 0)).astype(BF16)
    hk, hv = hid[:, 0:CMP_HIDDEN], hid[:, CMP_HIDDEN:]
    kc = _dot(hk, wk2_ref[...])
    lane = lax.broadcasted_iota(jnp.int32, kc.shape, 1)
    blk = lax.broadcasted_iota(jnp.int32, kc.shape, 0).astype(F32)
    kc = jnp.where((lane >= HEAD_DIM) & (lane < HEAD_DIM + ALIBI_ROWS), blk, kc)
    kc_ref[0] = kc.astype(kc_ref.dtype)
    vct_ref[0] = _dot_nt(wv2t_ref[...], hv).astype(vct_ref.dtype)


def _nsa_compress(cmp2d, pe_k, pe_v, w_ck1, w_ck2, w_cv1, w_cv2, batch, seq):
    ng = seq // CMP_STRIDE
    wk = w_ck1.reshape(CMP_LEN, HEAD_DIM, CMP_HIDDEN)
    wv = w_cv1.reshape(CMP_LEN, HEAD_DIM, CMP_HIDDEN)
    w1 = jnp.concatenate([jnp.pad(wk, ((0, 0), (0, 0), (0, CMP_HIDDEN))),
                          jnp.pad(wv, ((0, 0), (0, 0), (CMP_HIDDEN, 0)))], axis=1).astype(BF16)
    pe = jnp.concatenate([pe_k, pe_v], axis=1)

    def full(shape):
        return pl.BlockSpec(shape, lambda b: (0,) * len(shape))

    return pl.pallas_call(
        _nsa_compress_kernel,
        grid=(batch,),
        in_specs=[pl.BlockSpec((seq, LANES), lambda b: (b, 0)), full((CMP_LEN, LANES)),
                  full((CMP_LEN, LANES, 2 * CMP_HIDDEN)), full((CMP_HIDDEN, LANES)),
                  full((HEAD_DIM, CMP_HIDDEN))],
        out_specs=[pl.BlockSpec((1, ng, LANES), lambda b: (b, 0, 0)),
                   pl.BlockSpec((1, HEAD_DIM, ng), lambda b: (b, 0, 0))],
        out_shape=[jax.ShapeDtypeStruct((batch, ng, LANES), BF16),
                   jax.ShapeDtypeStruct((batch, HEAD_DIM, ng), BF16)],
        compiler_params=_params(("parallel",)),
        name="nsa_compress",
    )(cmp2d, pe, w1, jnp.pad(w_ck2, ((0, 0), (0, LANES - HEAD_DIM))).astype(BF16), w_cv2.T.astype(BF16))


def _nsa_phases(ovl_ref, q_ref, k_ref, vt_ref, kc_ref, vct_ref, misc_ref, z_ref, o_ref,
                m_ref, acc_ref, wq_ref, s_ref, sc_ref, *, tq, tk, seq):
    qi = pl.program_id(1)
    kpq = tq // tk
    scale = HEAD_DIM ** -0.5
    c1 = scale * LOG2E
    ng = seq // CMP_STRIDE
    ns = seq // SLC_BLOCK
    top = min(SLC_TOPK, ns)
    m_ref[...] = jnp.full(m_ref.shape, NEG_INF, F32)
    acc_ref[...] = jnp.zeros(acc_ref.shape, F32)
    rel = _rel_t(tk, tq)

    qt = q_ref[...].astype(F32).T.astype(BF16)
    zeros_q = jnp.zeros((HEAD_DIM, tq), BF16)
    for h in range(HEADS):
        qh = qt[h * HEAD_DIM:(h + 1) * HEAD_DIM]
        alibi = _alibi_rows(NSA_SLOPES[h] / scale, tq, tk=tk)
        wq_ref[h, 0:LANES, :] = jnp.concatenate([qh, zeros_q], axis=0)
        wq_ref[h, LANES:2 * LANES, :] = alibi
        wq_ref[HEADS + h, 0:LANES, :] = jnp.concatenate([zeros_q, qh], axis=0)
        wq_ref[HEADS + h, LANES:2 * LANES, :] = alibi

    t_lane = qi * tq + lax.broadcasted_iota(jnp.int32, (1, tq), 1)
    n_sub = lax.broadcasted_iota(jnp.int32, (ng, 1), 0)
    c_valid = (n_sub * CMP_STRIDE + (CMP_LEN - 1)) <= t_lane
    kc = kc_ref[0]
    vct_ext = jnp.concatenate([vct_ref[0], jnp.ones((BF16_ROWS, ng), BF16)], axis=0)
    ovl = ovl_ref[...]
    cmp_max = []
    for h in range(HEADS):
        rhs = jnp.concatenate([qt[h * HEAD_DIM:(h + 1) * HEAD_DIM],
                               _alibi_rows(NSA_SLOPES[h] * CMP_STRIDE / scale, tq, HEAD_DIM)], axis=0)
        cmp_max.append(_stage_scores(sc_ref, h, slice(0, tq), _dot(kc, rhs), c_valid))
    imp = jnp.zeros((ns, tq), F32)
    o_cmp = []
    for h in range(HEADS):
        mx = jnp.where(cmp_max[h] > NEG_INF, cmp_max[h], 0.0)
        pb = jnp.exp2(c1 * sc_ref[h] - c1 * mx).astype(BF16)
        o_ext = _dot(vct_ext, pb)
        r = 1.0 / jnp.maximum(o_ext[HEAD_DIM:HEAD_DIM + 1], 1e-30)
        o_cmp.append(o_ext[0:HEAD_DIM] * r)
        imp = imp + _dot(ovl, pb) * r

    def select_blocks():
        j_sub = lax.broadcasted_iota(jnp.int32, (ns, 1), 0)
        j_sub_f = j_sub.astype(F32)
        cur = jnp.right_shift(t_lane, SLC_SHIFT)
        forced = (j_sub == 0) | (j_sub == cur) | (j_sub == cur - 1)
        valid = (j_sub * SLC_BLOCK) <= t_lane
        score = jnp.where(forced, 1e30, jnp.where(valid, imp, -1.0))
        sel = jnp.zeros((ns, tq), F32)
        for _ in range(top):
            mx = jnp.max(score, axis=0, keepdims=True)
            idx = jnp.min(jnp.where(score == mx, j_sub_f, float(ns)), axis=0, keepdims=True)
            pick = j_sub_f == idx
            sel = jnp.where(pick, 1.0, sel)
            score = jnp.where(pick, -2.0, score)
        unsel = jnp.where(sel > 0.5, 0.0, -MASK_BIG).astype(BF16)
        for h in range(HEADS):
            wq_ref[h, LANES + BLK_LANE0:LANES + BLK_LANE0 + ns, :] = unsel

    aux_lane = lax.broadcasted_iota(jnp.int32, (tk, LANES), 1)
    aux_blk = jnp.right_shift(lax.broadcasted_iota(jnp.int32, (tk, LANES), 0), SLC_SHIFT) + BLK_LANE0
    kpos = _key_pos_features(tk)
    ones = jnp.ones((BF16_ROWS, tk), BF16)

    n_slots = s_ref.shape[0]

    def tiles(window, kis, cols, mask_offs=None):
        base = HEADS if window else 0
        loaded = []
        for t, ki in enumerate(kis):
            start = pl.multiple_of(ki * tk, tk)
            k = k_ref[pl.ds(start, tk), :]
            if window:
                vt = vt_ref[HEAD_DIM:2 * HEAD_DIM, pl.ds(start, tk)]
                aux = kpos
                dist = rel[:, cols[t]] + (qi * tq - ki * tk)
                keep = (dist >= 0) & (dist < WINDOW)
            else:
                vt = vt_ref[0:HEAD_DIM, pl.ds(start, tk)]
                onehot = aux_lane == aux_blk + ki * (tk // SLC_BLOCK)
                aux = jnp.where(onehot, jnp.ones_like(kpos), kpos)
                keep = None
            loaded.append((jnp.concatenate([k, aux], axis=1),
                           jnp.concatenate([vt, ones], axis=0), (ki * tk).astype(F32), keep))
        items = [(t, h) for t in range(len(kis)) for h in range(HEADS)]
        s_max = {}

        def scores(i):
            t, h = items[i]
            if window:
                keep = loaded[t][3]
            else:
                keep = None if mask_offs[t] is None else rel[:, cols[t]] >= mask_offs[t]
            s_max[i] = _stage_scores(s_ref, i % n_slots, cols[t],
                                     _dot(loaded[t][0], wq_ref[base + h, :, cols[t]]), keep)

        for i in range(QK_LOOKAHEAD):
            scores(i)
        for i, (t, h) in enumerate(items):
            if i + QK_LOOKAHEAD < len(items):
                scores(i + QK_LOOKAHEAD)
            _lhs, vt_ext, key0, _keep = loaded[t]
            _flash_update_t(lambda i=i, t=t: s_ref[i % n_slots, :, cols[t]], s_max.pop(i), c1,
                            (NSA_SLOPES[h] * LOG2E) * key0, vt_ext, m_ref, acc_ref, base + h, cols[t])

    all_cols = slice(0, tq)
    diag_cols = [slice(d * tk, tq) for d in range(kpq)]

    tiles(True, [qi * kpq + d for d in range(kpq)], diag_cols)
    select_blocks()

    def loop_body(i):
        tiles(False, [i * kpq + d for d in range(kpq)], [all_cols] * kpq, [None] * kpq)

    def back_window():
        n_back = (WINDOW + tk - 1) // tk
        for g in range((n_back + kpq - 1) // kpq):
            backs = list(range(g * kpq + 1, min((g + 1) * kpq, n_back) + 1))
            back_cols = [slice(0, min(tq, -(-(WINDOW - (back - 1) * tk - 1) // LANES) * LANES)) for back in backs]

            @pl.when(qi > g)
            def _():
                tiles(True, [qi * kpq - back for back in backs], back_cols)

    def diag_tiles():
        tiles(False, [qi * kpq + d for d in range(kpq)], diag_cols, [d * tk for d in range(kpq)])

    def finish():
        g_t = jax.nn.sigmoid(misc_ref[...]).T
        outs = []
        for h in range(HEADS):
            r0 = GATE_COL + 3 * h
            outs.append(g_t[r0:r0 + 1] * o_cmp[h] + g_t[r0 + 1:r0 + 2] * _normalized(acc_ref[h])
                        + g_t[r0 + 2:r0 + 3] * _normalized(acc_ref[HEADS + h]))
        y = jnp.concatenate(outs, axis=0).T * _silu(z_ref[...])
        o_ref[...] = y.astype(o_ref.dtype)

    return loop_body, back_window, diag_tiles, finish


def _overlap_t(seq):
    nc = (seq - CMP_LEN) // CMP_STRIDE + 1
    ng = seq // CMP_STRIDE
    ns = seq // SLC_BLOCK
    c_start = np.arange(ng) * CMP_STRIDE
    c_end = c_start + CMP_LEN - 1
    s_start = np.arange(ns) * SLC_BLOCK
    s_end = s_start + SLC_BLOCK - 1
    ov = (c_start[None, :] <= s_end[:, None]) & (c_end[None, :] >= s_start[:, None]) & (np.arange(ng)[None, :] < nc)
    return jnp.asarray(ov.astype(np.float32), dtype=BF16)


def _nsa_operands(q, k2, vt_all, kc, vct, misc, z_all, seq, nq, tq, tk):
    ng = seq // CMP_STRIDE
    ns = seq // SLC_BLOCK
    assert tq - tk < WINDOW and tk % SLC_BLOCK == 0 and ns <= LANES - BLK_LANE0
    inputs = [_overlap_t(seq), q, k2, vt_all, kc, vct, misc, z_all]
    in_specs = [pl.BlockSpec((ns, ng), lambda b, i: (0, 0)),
                pl.BlockSpec((tq, GROUP_W), lambda b, i: (b * nq + i, 0)),
                pl.BlockSpec((seq, LANES), lambda b, i: (b, 0)),
                pl.BlockSpec((2 * HEAD_DIM, seq), lambda b, i: (NSA_VT_BLK, b)),
                pl.BlockSpec((1, ng, LANES), lambda b, i: (b, 0, 0)),
                pl.BlockSpec((1, HEAD_DIM, ng), lambda b, i: (b, 0, 0)),
                pl.BlockSpec((tq, LANES), lambda b, i: (b * nq + i, 0)),
                pl.BlockSpec((tq, GROUP_W), lambda b, i: (b * nq + i, 0))]
    scratch = [pltpu.VMEM((2 * HEADS, 1, tq), F32),
               pltpu.VMEM((2 * HEADS, ACC_ROWS, tq), F32),
               pltpu.VMEM((2 * HEADS, 2 * LANES, tq), BF16),
               pltpu.VMEM((2 * HEADS, tk, tq), F32),
               pltpu.VMEM((HEADS, ng, tq), F32)]
    return inputs, in_specs, scratch


def _mixers_kernel(*refs, n_in, n_scratch, tq, tk_nsa, tk_diff, seq, lam_init):
    bounds = np.cumsum([0] + list(n_in))
    nsa_in, diff_in, ret_in, ssd_in = (refs[a:b] for a, b in zip(bounds[:-1], bounds[1:]))
    nsa_out, diff_out, ret_out, ssd_out = refs[bounds[-1]:bounds[-1] + 4]
    sb = np.cumsum([0] + list(n_scratch)) + bounds[-1] + 4
    nsa_scr, diff_scr, rec_scr = (refs[a:b] for a, b in zip(sb[:-1], sb[1:]))
    ret_st, ssd_ext, ssd_st = rec_scr
    qi = pl.program_id(1)

    @pl.when(qi == 0)
    def _():
        ret_st[...] = jnp.zeros(ret_st.shape, F32)
        ssd_st[...] = jnp.zeros(ssd_st.shape, F32)
        ssd_ext[0:8, :] = jnp.zeros((8, CONV_CH), F32)

    chunks = list(range(ret_out.shape[0]))
    first, second = chunks[:len(chunks) // 2], chunks[len(chunks) // 2:]
    nsa_loop, nsa_back_window, nsa_diag, nsa_finish = _nsa_phases(
        *nsa_in, nsa_out, *nsa_scr, tq=tq, tk=tk_nsa, seq=seq)
    diff_loop, diff_diag, diff_finish = _diff_phases(
        *diff_in, diff_out, *diff_scr, tq=tq, tk=tk_diff, lam_init=lam_init)
    _ret_body(*ret_in, ret_out, ret_st, first)
    _ssd_body(*ssd_in, ssd_out, ssd_ext, ssd_st, first)

    def body(i, carry):
        nsa_loop(i)
        diff_loop(i)
        return carry

    lax.fori_loop(0, qi, body, 0)
    nsa_back_window()
    nsa_diag()
    diff_diag()
    _ret_body(*ret_in, ret_out, ret_st, second)
    _ssd_body(*ssd_in, ssd_out, ssd_ext, ssd_st, second)
    nsa_finish()
    diff_finish()


def _mixers(nsa_args, diff_args, rec_args, batch, seq, layer_idx, tq, tk_nsa, tk_diff):
    nq = seq // tq
    nsa_in, nsa_specs, nsa_scratch = _nsa_operands(*nsa_args, seq, nq, tq, tk_nsa)
    diff_in, diff_specs, diff_scratch = _diff_operands(*diff_args, seq, nq, tq, tk_diff)
    ret_in, ssd_in, rec_specs, rec_out_spec, rec_scratch = _recurrent_operands(*rec_args, nq, tq)
    out_spec = pl.BlockSpec((tq, GROUP_W), lambda b, i: (b * nq + i, 0))
    m = batch * seq
    y_nsa, y_diff, y_ret, y_ssm = pl.pallas_call(
        functools.partial(_mixers_kernel, n_in=(len(nsa_in), len(diff_in), len(ret_in), len(ssd_in)),
                          n_scratch=(len(nsa_scratch), len(diff_scratch), len(rec_scratch)),
                          tq=tq, tk_nsa=tk_nsa, tk_diff=tk_diff, seq=seq,
                          lam_init=0.8 - 0.6 * math.exp(-0.3 * layer_idx)),
        grid=(batch, nq),
        in_specs=nsa_specs + diff_specs + rec_specs,
        out_specs=[out_spec, out_spec, rec_out_spec, rec_out_spec],
        out_shape=[jax.ShapeDtypeStruct((m, GROUP_W), BF16)] * 2
                  + [jax.ShapeDtypeStruct((m // SSM_CHUNK, SSM_CHUNK, GROUP_W), BF16)] * 2,
        scratch_shapes=nsa_scratch + diff_scratch + rec_scratch,
        compiler_params=_params(("parallel", "arbitrary")),
        name="mixers",
    )(*nsa_in, *diff_in, *ret_in, *ssd_in)
    return y_nsa, y_diff, y_ret.reshape(m, GROUP_W), y_ssm.reshape(m, GROUP_W)


def _ret_tables():
    c = RET_CHUNK
    h = np.arange(HEADS, dtype=np.float32)
    log_g = jnp.log(1.0 - 2.0 ** (-5.0 - jnp.asarray(h)))
    pos = jnp.arange(c, dtype=F32)
    rel = pos[:, None] - pos[None, :]
    decay = jnp.where(rel >= 0, jnp.exp(log_g[:, None, None] * jnp.maximum(rel, 0.0)), 0.0)
    xi = jnp.exp(log_g[:, None] * (pos + 1.0))
    zeta = jnp.exp(log_g[:, None] * (c - 1.0 - pos))
    chunk_decay = jnp.exp(log_g * c)
    xi_tab = jnp.repeat(xi.T, HEAD_DIM, axis=1)
    zeta_tab = jnp.repeat(zeta.T, HEAD_DIM, axis=1)
    cd_tab = jnp.repeat(chunk_decay, HEAD_DIM)[None, :]
    return decay, xi_tab, zeta_tab, cd_tab


def _ret_body(decay_ref, xi_ref, zeta_ref, cd_ref, gn_ref, q_ref, k_ref, v_ref, z_ref, o_ref, st_ref, blocks):
    xi = xi_ref[...]
    cd = cd_ref[...]
    for bb in blocks:
        q = (q_ref[bb].astype(F32) * (HEAD_DIM ** -0.5)).astype(BF16)
        k = k_ref[bb]
        v = v_ref[bb]
        kz_t = (k.astype(F32) * zeta_ref[...]).T.astype(BF16)
        outs = []
        for h in range(HEADS):
            sl = slice(h * HEAD_DIM, (h + 1) * HEAD_DIM)
            qh, kh, vh = q[:, sl], k[:, sl], v[:, sl]
            prev = st_ref[h]
            inner = (_dot_nt(qh, kh) * decay_ref[h]).astype(BF16)
            o = _dot(inner, vh) + _dot(qh, prev.astype(BF16)) * xi[:, sl]
            st_ref[h] = prev * cd[:, sl] + _dot(kz_t[sl, :], vh)
            mu = jnp.mean(o, axis=-1, keepdims=True)
            d = o - mu
            var = jnp.mean(d * d, axis=-1, keepdims=True)
            outs.append(d * lax.rsqrt(var + EPS))
        y = jnp.concatenate(outs, axis=-1) * gn_ref[...] * _silu(z_ref[bb])
        o_ref[bb] = y.astype(o_ref.dtype)


def _ssd_body(cw_ref, cb_ref, dtb_ref, a_ref, dsk_ref, nw_ref, xbc_ref, misc_ref, z_ref, o_ref,
              ext_ref, st_ref, blocks):
    L = SSM_CHUNK
    hi = lax.Precision.HIGHEST
    row = lax.broadcasted_iota(jnp.int32, (L, L), 0)
    col = lax.broadcasted_iota(jnp.int32, (L, L), 1)
    causal = row >= col
    tril = jnp.where(causal, 1.0, 0.0).astype(F32)
    dsk = dsk_ref[...]

    for bb in blocks:
        raw = xbc_ref[bb]
        ext_ref[8:8 + L, :] = raw
        conv = cb_ref[...] + raw * cw_ref[CONV_W - 1:CONV_W, :]
        for w in range(CONV_W - 1):
            shift = CONV_W - 1 - w
            conv = conv + ext_ref[8 - shift:8 - shift + L, :] * cw_ref[w:w + 1, :]
        ext_ref[0:8, :] = raw[L - 8:L, :]
        xc = _silu(conv)
        x = xc[:, 0:GROUP_W]
        bm = xc[:, GROUP_W:GROUP_W + 2 * SSM_STATE].astype(BF16)
        cm = xc[:, GROUP_W + 2 * SSM_STATE:].astype(BF16)

        dt_full = jax.nn.softplus(misc_ref[bb] + dtb_ref[...])
        da = dt_full * a_ref[...]
        cs_col = jnp.dot(tril, da, precision=hi, preferred_element_type=F32)
        cs_row = lax.dot_general(da, tril, (((0,), (1,)), ((), ())), precision=hi,
                                 preferred_element_type=F32)

        outs = []
        for h in range(HEADS):
            g = h // 2
            c0 = DT_COL + h
            sl = slice(h * HEAD_DIM, (h + 1) * HEAD_DIM)
            gs = slice(g * SSM_STATE, (g + 1) * SSM_STATE)
            cs_c = cs_col[:, c0:c0 + 1]
            cs_r = cs_row[c0:c0 + 1, :]
            cs_last = cs_col[L - 1:L, c0:c0 + 1]
            xh = x[:, sl]
            xdt = xh * dt_full[:, c0:c0 + 1]
            seg = jnp.exp(jnp.where(causal, cs_c - cs_r, NEG_INF))
            cb = _dot_nt(cm[:, gs], bm[:, gs])
            y = _dot((cb * seg).astype(BF16), xdt.astype(BF16))
            prev = st_ref[h]
            y = y + _dot(cm[:, gs], prev.astype(BF16)) * jnp.exp(cs_c)
            y = y + dsk[:, sl] * xh
            dec = jnp.exp(cs_last - cs_c)
            st_ref[h] = prev * jnp.exp(cs_last) + _dot_tn(bm[:, gs], (xdt * dec).astype(BF16))
            outs.append(y)
        y = jnp.concatenate(outs, axis=-1) * _silu(z_ref[bb])
        ms = jnp.mean(y * y, axis=-1, keepdims=True)
        o_ref[bb] = (y * lax.rsqrt(ms + EPS) * nw_ref[...]).astype(o_ref.dtype)


def _recurrent_operands(ret_qkv, xbc, misc, z_all, gn_w, conv_w, conv_b, dt_bias, a_log, d_skip, norm_w, nq, tq):
    L = SSM_CHUNK
    assert RET_CHUNK == L and tq % L == 0
    cpt = tq // L
    decay, xi_tab, zeta_tab, cd_tab = _ret_tables()
    dtb = jnp.zeros((1, 128), F32).at[0, DT_COL:DT_COL + HEADS].set(dt_bias)
    a_full = jnp.zeros((1, 128), F32).at[0, DT_COL:DT_COL + HEADS].set(-jnp.exp(a_log))
    dsk = jnp.repeat(d_skip, HEAD_DIM)[None, :]

    def full(shape):
        return pl.BlockSpec(shape, lambda b, i: (0,) * len(shape))

    def blk(width, col):
        return pl.BlockSpec((cpt, L, width), lambda b, i: (b * nq + i, 0, col))

    def chunked(a):
        return a.reshape(-1, L, a.shape[-1])

    qkv3, z3 = chunked(ret_qkv), chunked(z_all)
    ret_in = [decay, xi_tab, zeta_tab, cd_tab, gn_w.reshape(1, GROUP_W), qkv3, qkv3, qkv3, z3]
    ret_specs = [full((HEADS, L, L)), full((L, GROUP_W)), full((L, GROUP_W)), full((1, GROUP_W)),
                 full((1, GROUP_W)), blk(GROUP_W, 0), blk(GROUP_W, 1), blk(GROUP_W, 2), blk(GROUP_W, 2)]
    ssd_in = [conv_w, conv_b.reshape(1, CONV_CH), dtb, a_full, dsk, norm_w.reshape(1, GROUP_W),
              chunked(xbc), chunked(misc), z3]
    ssd_specs = [full((CONV_W, CONV_CH)), full((1, CONV_CH)), full((1, 128)), full((1, 128)),
                 full((1, GROUP_W)), full((1, GROUP_W)), blk(CONV_CH, 0), blk(128, 0), blk(GROUP_W, 3)]
    scratch = [pltpu.VMEM((HEADS, HEAD_DIM, HEAD_DIM), F32),
               pltpu.VMEM((8 + L, CONV_CH), F32),
               pltpu.VMEM((HEADS, SSM_STATE, HEAD_DIM), F32)]
    return ret_in, ssd_in, ret_specs + ssd_specs, blk(GROUP_W, 0), scratch


def _pick_tile(n, pref):
    t = pref
    while n % t:
        t //= 2
    return t


def kernel(x, norm_w, w_in, w_out, nsa_pe_k, nsa_pe_v, nsa_w_ck1, nsa_w_ck2, nsa_w_cv1, nsa_w_cv2,
           diff_lam_q1, diff_lam_k1, diff_lam_q2, diff_lam_k2, diff_subln_w, ret_gn_w,
           ssm_conv_w, ssm_conv_b, ssm_dt_bias, ssm_A_log, ssm_D, ssm_norm_w, final_norm_w):
    batch, seq, _ = x.shape
    depth = w_in.shape[0]
    m = batch * seq
    tm = _pick_tile(m, 512)
    tq = _pick_tile(seq, 512)
    tk = _pick_tile(seq, 256)
    tk_diff = _pick_tile(seq, 512)
    w_in_b = w_in.astype(BF16)
    w_r = _relayout_w_in(w_in_b)
    w_t = _relayout_w_in_t(w_in_b)
    w_out_b = w_out.astype(BF16)
    x2d = x.reshape(m, D_MODEL)
    projected = _in_proj(x2d, norm_w[0], w_r, w_t, 0, tm)
    for i in range(depth):
        nsa_q, nsa_k2, nsa_cmp, misc, z_all, diff_qk, ret_qkv, xbc, vt_all = projected
        kc, vct = _nsa_compress(nsa_cmp, nsa_pe_k[i], nsa_pe_v[i], nsa_w_ck1[i], nsa_w_ck2[i],
                                nsa_w_cv1[i], nsa_w_cv2[i], batch, seq)
        lam_p = jnp.stack([diff_lam_q1[i], diff_lam_k1[i], diff_lam_q2[i], diff_lam_k2[i]])
        ys = _mixers((nsa_q, nsa_k2, vt_all, kc, vct, misc, z_all),
                     (diff_qk, vt_all, z_all, lam_p, diff_subln_w[i]),
                     (ret_qkv, xbc, misc, z_all, ret_gn_w[i], ssm_conv_w[i], ssm_conv_b[i], ssm_dt_bias[i],
                      ssm_A_log[i], ssm_D[i], ssm_norm_w[i]),
                     batch, seq, i, tq, tk, tk_diff)
        if i + 1 < depth:
            x2d, *projected = _out_in_proj(ys, w_out_b, i, x2d, norm_w[i + 1], w_r, w_t, tm)
        else:
            x2d = _out_proj(ys, w_out_b, i, x2d, final_norm_w, tm)
    return x2d.reshape(batch, seq, D_MODEL)
```

```python
import functools
import math

import numpy as np
import jax
import jax.numpy as jnp
from jax import lax
from jax.experimental import pallas as pl
from jax.experimental.pallas import tpu as pltpu

F32 = jnp.float32
BF16 = jnp.bfloat16
NEG_INF = float("-inf")
LOG2E = 1.4426950408889634

D_MODEL = 1024
DEPTH = 4
GROUP_W = 256
HEADS = 4
HEAD_DIM = 64
EPS = 1e-6
CMP_LEN = 32
CMP_STRIDE = 16
CMP_HIDDEN = 256
SLC_BLOCK = 64
SLC_SHIFT = 6
SLC_TOPK = 16
WINDOW = 512
DIFF_QK_DIM = 32
RET_CHUNK = 128
SSM_STATE = 128
SSM_CHUNK = 128
CONV_W = 4
CONV_CH = 768
N_ALIBI_HEADS = 8
LANES = 128
BF16_ROWS = 16
ACC_ROWS = HEAD_DIM + BF16_ROWS
QK_LOOKAHEAD = 3
NSA_LOOKAHEAD = 5
ALIBI_ROWS = 3
POS_RADIX = 256
MASK_BIG = 2.0 ** 100
BLK_LANE0 = 64

IN_LAYOUT = (
    ("nsa_q", 256), ("nsa_k_cmp", 64), ("nsa_v_cmp", 64), ("nsa_k_slc", 64), ("nsa_v_slc", 64),
    ("nsa_k_win", 64), ("nsa_v_win", 64), ("nsa_gate", 12), ("nsa_z", 256),
    ("diff_q", 256), ("diff_k", 256), ("diff_v", 256), ("diff_z", 256),
    ("ret_q", 256), ("ret_k", 256), ("ret_v", 256), ("ret_z", 256),
    ("ssm_z", 256), ("ssm_xbc", 768), ("ssm_dt", 4),
)
IN_OFF = {}
_o = 0
for _n, _w in IN_LAYOUT:
    IN_OFF[_n] = (_o, _w)
    _o += _w
IN_W = _o

GATE_COL = 0
DT_COL = 12
IN_OUTPUTS = (
    ("nsa_q", BF16, ("nsa_q",), 256),
    ("nsa_k2", BF16, ("nsa_k_slc", "nsa_k_win"), 128),
    ("nsa_cmp", F32, ("nsa_k_cmp", "nsa_v_cmp"), 128),
    ("misc", F32, ("nsa_gate", "ssm_dt"), 128),
    ("z_all", F32, ("nsa_z", "diff_z", "ret_z", "ssm_z"), 1024),
    ("diff_qk", BF16, ("diff_q", "diff_k"), 512),
    ("ret_qkv", BF16, ("ret_q", "ret_k", "ret_v"), 768),
    ("xbc", F32, ("ssm_xbc",), 768),
)
IN_T_SRC = ("diff_v", "nsa_v_slc", "nsa_v_win")
IN_T_ROWS = 384
DIFF_VT_BLK = 0
NSA_VT_BLK = 2
IN_SEGS = []
_o = 0
for _n, _dt, _src, _w in IN_OUTPUTS:
    IN_SEGS.append((_o, _o + _w))
    _o += _w
IN_WP = _o

VMEM_LIMIT = 56 * 1024 * 1024


def _alibi_slopes():
    return [2.0 ** (-8.0 * (i + 1) / N_ALIBI_HEADS) for i in range(N_ALIBI_HEADS)]


NSA_SLOPES = _alibi_slopes()[0::2]
DIFF_SLOPES = _alibi_slopes()[1::2]


def _silu(x):
    return x * jax.nn.sigmoid(x)


def _dot(a, b):
    return jnp.dot(a, b, preferred_element_type=F32)


def _dot_nt(a, b):
    return lax.dot_general(a, b, (((1,), (1,)), ((), ())), preferred_element_type=F32)


def _dot_tn(a, b):
    return lax.dot_general(a, b, (((0,), (0,)), ((), ())), preferred_element_type=F32)


def _params(sem):
    return pltpu.CompilerParams(dimension_semantics=sem, vmem_limit_bytes=VMEM_LIMIT)


def _relayout_w_in(w_in):
    cols = []
    for _n, _dt, src, width in IN_OUTPUTS:
        used = 0
        for s in src:
            off, w = IN_OFF[s]
            cols.append(w_in[:, :, off:off + w])
            used += w
        if used < width:
            cols.append(jnp.zeros(w_in.shape[:2] + (width - used,), w_in.dtype))
    return jnp.concatenate(cols, axis=-1)


def _relayout_w_in_t(w_in):
    cols = [w_in[:, :, IN_OFF[s][0]:IN_OFF[s][0] + IN_OFF[s][1]] for s in IN_T_SRC]
    return jnp.swapaxes(jnp.concatenate(cols, axis=-1), 1, 2)


def _norm_project(x, nw_ref, w_ref, wt_ref, out_refs):
    ms = jnp.mean(x * x, axis=-1, keepdims=True)
    h = (x * lax.rsqrt(ms + EPS) * nw_ref[...]).astype(BF16)
    for ref, (a, b) in zip(out_refs[:-1], IN_SEGS):
        ref[...] = _dot(h, w_ref[:, a:b]).astype(ref.dtype)
    out_refs[-1][...] = _dot_nt(wt_ref[...], h).astype(BF16)


def _in_proj_kernel(x_ref, nw_ref, w_ref, wt_ref, *out_refs):
    _norm_project(x_ref[...], nw_ref, w_ref, wt_ref, out_refs)


def _in_proj_outputs(m, tm):
    out_shape = [jax.ShapeDtypeStruct((m, b - a), dt) for (_n, dt, _s, _w), (a, b) in zip(IN_OUTPUTS, IN_SEGS)]
    out_specs = [pl.BlockSpec((tm, b - a), lambda i: (i, 0)) for (a, b) in IN_SEGS]
    out_shape.append(jax.ShapeDtypeStruct((IN_T_ROWS, m), BF16))
    out_specs.append(pl.BlockSpec((IN_T_ROWS, tm), lambda i: (0, i)))
    return out_shape, out_specs


def _in_proj(x2d, norm_w, w_r, w_t, tm):
    m = x2d.shape[0]
    out_shape, out_specs = _in_proj_outputs(m, tm)
    return pl.pallas_call(
        _in_proj_kernel,
        grid=(m // tm,),
        in_specs=[pl.BlockSpec((tm, D_MODEL), lambda i: (i, 0)),
                  pl.BlockSpec((1, D_MODEL), lambda i: (0, 0)),
                  pl.BlockSpec((D_MODEL, IN_WP), lambda i: (0, 0)),
                  pl.BlockSpec((IN_T_ROWS, D_MODEL), lambda i: (0, 0))],
        out_specs=out_specs,
        out_shape=out_shape,
        compiler_params=_params(("parallel",)),
        name="in_proj",
    )(x2d, norm_w.reshape(1, D_MODEL), w_r, w_t)


def _out_proj_kernel(y0_ref, y1_ref, y2_ref, y3_ref, w_ref, x_ref, fw_ref, o_ref):
    acc = x_ref[...]
    for g, y_ref in enumerate((y0_ref, y1_ref, y2_ref, y3_ref)):
        acc = acc + _dot(y_ref[...], w_ref[g * GROUP_W:(g + 1) * GROUP_W, :])
    ms = jnp.mean(acc * acc, axis=-1, keepdims=True)
    o_ref[...] = acc * lax.rsqrt(ms + EPS) * fw_ref[...]


def _out_proj(ys, w_out_b, x2d, final_w, tm):
    m = x2d.shape[0]
    yspec = pl.BlockSpec((tm, GROUP_W), lambda i: (i, 0))
    return pl.pallas_call(
        _out_proj_kernel,
        grid=(m // tm,),
        in_specs=[yspec, yspec, yspec, yspec,
                  pl.BlockSpec((D_MODEL, D_MODEL), lambda i: (0, 0)),
                  pl.BlockSpec((tm, D_MODEL), lambda i: (i, 0)),
                  pl.BlockSpec((1, D_MODEL), lambda i: (0, 0))],
        out_specs=pl.BlockSpec((tm, D_MODEL), lambda i: (i, 0)),
        out_shape=jax.ShapeDtypeStruct((m, D_MODEL), F32),
        compiler_params=_params(("parallel",)),
        name="out_proj",
    )(*ys, w_out_b, x2d, final_w.reshape(1, D_MODEL))


def _out_in_proj_kernel(y0_ref, y1_ref, y2_ref, y3_ref, wo_ref, x_ref, nw_ref, w_ref, wt_ref, xo_ref, *out_refs):
    acc = x_ref[...]
    for g, y_ref in enumerate((y0_ref, y1_ref, y2_ref, y3_ref)):
        acc = acc + _dot(y_ref[...], wo_ref[g * GROUP_W:(g + 1) * GROUP_W, :])
    xo_ref[...] = acc
    _norm_project(acc, nw_ref, w_ref, wt_ref, out_refs)


def _out_in_proj(ys, w_out_b, x2d, norm_w, w_r, w_t, tm):
    m = x2d.shape[0]
    yspec = pl.BlockSpec((tm, GROUP_W), lambda i: (i, 0))
    xspec = pl.BlockSpec((tm, D_MODEL), lambda i: (i, 0))
    out_shape, out_specs = _in_proj_outputs(m, tm)
    return pl.pallas_call(
        _out_in_proj_kernel,
        grid=(m // tm,),
        in_specs=[yspec, yspec, yspec, yspec,
                  pl.BlockSpec((D_MODEL, D_MODEL), lambda i: (0, 0)),
                  xspec,
                  pl.BlockSpec((1, D_MODEL), lambda i: (0, 0)),
                  pl.BlockSpec((D_MODEL, IN_WP), lambda i: (0, 0)),
                  pl.BlockSpec((IN_T_ROWS, D_MODEL), lambda i: (0, 0))],
        out_specs=[xspec] + out_specs,
        out_shape=[jax.ShapeDtypeStruct((m, D_MODEL), F32)] + out_shape,
        compiler_params=_params(("parallel",)),
        name="out_in_proj",
    )(*ys, w_out_b, x2d, norm_w.reshape(1, D_MODEL), w_r, w_t)


def _stage_scores(s_ref, slot, cols, s, keep=None):
    if keep is not None:
        s = jnp.where(keep, s, NEG_INF)
    s_ref[slot, :, cols] = s
    return jnp.max(s, axis=0, keepdims=True)


def _flash_update_t(s, s_max, c1, shift, vt_ext, m_ref, acc_ref, idx, cols):
    m_old = m_ref[idx, :, cols]
    m_new = jnp.maximum(m_old, c1 * s_max + shift)
    alpha = jnp.exp2(m_old - m_new)
    p = jnp.exp2(c1 * s() - (m_new - shift))
    acc_ref[idx, :, cols] = alpha * acc_ref[idx, :, cols] + _dot(vt_ext, p.astype(BF16))
    m_ref[idx, :, cols] = m_new


def _bf16_pieces(x, n=3):
    out = []
    for _ in range(n):
        p = float(np.asarray(x, np.float32).astype(BF16).astype(np.float32))
        out.append(p)
        x = x - p
    return out


def _key_pos_features(tk):
    lane = lax.broadcasted_iota(jnp.int32, (tk, LANES), 1)
    row = lax.broadcasted_iota(jnp.int32, (tk, LANES), 0)
    out = jnp.zeros((tk, LANES), jnp.int32)
    for d in range(-(-tk // POS_RADIX)):
        digit = jnp.bitwise_and(jnp.right_shift(row, 8 * d), POS_RADIX - 1)
        out = jnp.where((lane >= ALIBI_ROWS * d) & (lane < ALIBI_ROWS * (d + 1)), digit, out)
    return out.astype(F32).astype(BF16)


def _alibi_rows(beta, tq, rows=LANES, tk=POS_RADIX):
    row = lax.broadcasted_iota(jnp.int32, (rows, tq), 0)
    out = jnp.zeros((rows, tq), F32)
    for d in range(-(-tk // POS_RADIX)):
        for r, piece in enumerate(_bf16_pieces(beta * POS_RADIX ** d, ALIBI_ROWS)):
            out = jnp.where(row == ALIBI_ROWS * d + r, piece, out)
    return out.astype(BF16)


def _normalized(acc):
    return acc[0:HEAD_DIM] / jnp.maximum(acc[HEAD_DIM:HEAD_DIM + 1], 1e-30)


def _rel_t(tk, tq):
    return lax.broadcasted_iota(jnp.int32, (tk, tq), 1) - lax.broadcasted_iota(jnp.int32, (tk, tq), 0)


def _diff_phases(lam_ref, sw_ref, q_ref, k_ref, vt_ref, z_ref, o_ref, m_ref, acc_ref, wq_ref, s_ref,
                 *, tq, tk, lam_init):
    qi = pl.program_id(1)
    kpq = tq // tk
    scale = DIFF_QK_DIM ** -0.5
    c1 = scale * LOG2E
    m_ref[...] = jnp.full(m_ref.shape, NEG_INF, F32)
    acc_ref[...] = jnp.zeros(acc_ref.shape, F32)
    rel = _rel_t(tk, tq)

    qt = q_ref[...].astype(F32).T.astype(BF16)
    row = lax.broadcasted_iota(jnp.int32, (LANES, tq), 0)
    for j in range(2 * HEADS):
        g, r0 = divmod(j * DIFF_QK_DIM, LANES)
        qg = qt[g * LANES:(g + 1) * LANES]
        wq_ref[j, 0:LANES, :] = jnp.where((row >= r0) & (row < r0 + DIFF_QK_DIM), qg, jnp.zeros_like(qg))
        wq_ref[j, LANES:2 * LANES, :] = _alibi_rows(DIFF_SLOPES[j // 2] / scale, tq, tk=tk)

    ones = jnp.ones((BF16_ROWS, tk), BF16)
    kpos = _key_pos_features(tk)

    n_maps = 2 * HEADS
    n_slots = s_ref.shape[0]

    def tiles(kis, mask_offs):
        loaded = []
        for ki in kis:
            start = pl.multiple_of(ki * tk, tk)
            loaded.append((k_ref[pl.ds(start, tk), :],
                           vt_ref[:, pl.ds(start, tk)],
                           (ki * tk).astype(F32)))
        items = [(t, j) for t in range(len(kis)) for j in range(n_maps)]
        cols = slice(0, tq)

        def scores(i):
            t, j = items[i]
            g = (j * DIFF_QK_DIM) // LANES
            lhs = jnp.concatenate([loaded[t][0][:, g * LANES:(g + 1) * LANES], kpos], axis=1)
            s_ref[i % n_slots] = _dot(lhs, wq_ref[j])

        def staged(i):
            t = items[i][0]
            if mask_offs[t] is None:
                return lambda: s_ref[i % n_slots]
            return lambda: jnp.where(rel >= mask_offs[t], s_ref[i % n_slots], NEG_INF)

        for i in range(QK_LOOKAHEAD):
            scores(i)
        for i, (t, j) in enumerate(items):
            h = j // 2
            if i + QK_LOOKAHEAD < len(items):
                scores(i + QK_LOOKAHEAD)
            _k, vt, key0 = loaded[t]
            vt_ext = jnp.concatenate([vt[h * HEAD_DIM:(h + 1) * HEAD_DIM], ones], axis=0)
            s = staged(i)
            _flash_update_t(s, jnp.max(s(), axis=0, keepdims=True), c1, (DIFF_SLOPES[h] * LOG2E) * key0,
                            vt_ext, m_ref, acc_ref, j, cols)

    def loop_body(i):
        tiles([i * kpq + d for d in range(kpq)], [None] * kpq)

    def diag_tiles():
        tiles([qi * kpq + d for d in range(kpq)], [d * tk for d in range(kpq)])

    def finish():
        lp = lam_ref[...]
        lam = (jnp.exp(jnp.sum(lp[0:1] * lp[1:2], axis=-1, keepdims=True))
               - jnp.exp(jnp.sum(lp[2:3] * lp[3:4], axis=-1, keepdims=True)) + lam_init)
        sw = sw_ref[...]
        o_t = jnp.concatenate([_normalized(acc_ref[2 * h]) - lam * _normalized(acc_ref[2 * h + 1])
                               for h in range(HEADS)], axis=0)
        o = o_t.T
        outs = []
        for h in range(HEADS):
            oh = o[:, h * HEAD_DIM:(h + 1) * HEAD_DIM]
            ms = jnp.mean(oh * oh, axis=-1, keepdims=True)
            outs.append(oh * lax.rsqrt(ms + EPS) * sw * (1.0 - lam_init))
        y = jnp.concatenate(outs, axis=-1) * _silu(z_ref[...])
        o_ref[...] = y.astype(o_ref.dtype)

    return loop_body, diag_tiles, finish


def _diff_operands(qk, vt_all, z_all, lam_p, subln_w, seq, nq, tq, tk):
    inputs = [lam_p, subln_w.reshape(1, HEAD_DIM), qk, qk, vt_all, z_all]
    in_specs = [pl.BlockSpec((4, DIFF_QK_DIM), lambda b, i: (0, 0)),
                pl.BlockSpec((1, HEAD_DIM), lambda b, i: (0, 0)),
                pl.BlockSpec((tq, GROUP_W), lambda b, i: (b * nq + i, 0)),
                pl.BlockSpec((seq, GROUP_W), lambda b, i: (b, 1)),
                pl.BlockSpec((GROUP_W, seq), lambda b, i: (DIFF_VT_BLK, b)),
                pl.BlockSpec((tq, GROUP_W), lambda b, i: (b * nq + i, 1))]
    scratch = [pltpu.VMEM((2 * HEADS, 1, tq), F32),
               pltpu.VMEM((2 * HEADS, ACC_ROWS, tq), F32),
               pltpu.VMEM((2 * HEADS, 2 * LANES, tq), BF16),
               pltpu.VMEM((QK_LOOKAHEAD + 1, tk, tq), F32)]
    return inputs, in_specs, scratch


def _nsa_compress_kernel(cmp_ref, pe_ref, w1_ref, wk2_ref, wv2t_ref, kc_ref, vct_ref):
    ng = kc_ref.shape[1]
    top = jnp.zeros((ng, 2 * CMP_HIDDEN), F32)
    bot = jnp.zeros((ng, 2 * CMP_HIDDEN), F32)
    for p in range(CMP_STRIDE):
        x = cmp_ref[pl.ds(p, ng, stride=CMP_STRIDE), :]
        top = top + _dot((x + pe_ref[p:p + 1, :]).astype(BF16), w1_ref[p])
        bot = bot + _dot((x + pe_ref[CMP_STRIDE + p:CMP_STRIDE + p + 1, :]).astype(BF16), w1_ref[CMP_STRIDE + p])
    hid = _silu(top + pltpu.roll(bot, ng - 1, 0)).astype(BF16)
    hk, hv = hid[:, 0:CMP_HIDDEN], hid[:, CMP_HIDDEN:]
    kc = _dot(hk, wk2_ref[...])
    lane = lax.broadcasted_iota(jnp.int32, kc.shape, 1)
    blk = lax.broadcasted_iota(jnp.int32, kc.shape, 0).astype(F32)
    kc = jnp.where((lane >= HEAD_DIM) & (lane < HEAD_DIM + ALIBI_ROWS), blk, kc)
    kc_ref[0] = kc.astype(kc_ref.dtype)
    vct_ref[0] = _dot_nt(wv2t_ref[...], hv).astype(vct_ref.dtype)


def _nsa_compress(cmp2d, pe_k, pe_v, w_ck1, w_ck2, w_cv1, w_cv2, batch, seq):
    ng = seq // CMP_STRIDE
    wk = w_ck1.reshape(CMP_LEN, HEAD_DIM, CMP_HIDDEN)
    wv = w_cv1.reshape(CMP_LEN, HEAD_DIM, CMP_HIDDEN)
    w1 = jnp.concatenate([jnp.pad(wk, ((0, 0), (0, 0), (0, CMP_HIDDEN))),
                          jnp.pad(wv, ((0, 0), (0, 0), (CMP_HIDDEN, 0)))], axis=1).astype(BF16)
    pe = jnp.concatenate([pe_k, pe_v], axis=1)

    def full(shape):
        return pl.BlockSpec(shape, lambda b: (0,) * len(shape))

    return pl.pallas_call(
        _nsa_compress_kernel,
        grid=(batch,),
        in_specs=[pl.BlockSpec((seq, LANES), lambda b: (b, 0)), full((CMP_LEN, LANES)),
                  full((CMP_LEN, LANES, 2 * CMP_HIDDEN)), full((CMP_HIDDEN, LANES)),
                  full((HEAD_DIM, CMP_HIDDEN))],
        out_specs=[pl.BlockSpec((1, ng, LANES), lambda b: (b, 0, 0)),
                   pl.BlockSpec((1, HEAD_DIM, ng), lambda b: (b, 0, 0))],
        out_shape=[jax.ShapeDtypeStruct((batch, ng, LANES), BF16),
                   jax.ShapeDtypeStruct((batch, HEAD_DIM, ng), BF16)],
        compiler_params=_params(("parallel",)),
        name="nsa_compress",
    )(cmp2d, pe, w1, jnp.pad(w_ck2, ((0, 0), (0, LANES - HEAD_DIM))).astype(BF16), w_cv2.T.astype(BF16))


def _nsa_phases(ovl_ref, q_ref, k_ref, vt_ref, kc_ref, vct_ref, misc_ref, z_ref, o_ref,
                m_ref, acc_ref, wq_ref, s_ref, sc_ref, *, tq, tk, seq):
    qi = pl.program_id(1)
    kpq = tq // tk
    scale = HEAD_DIM ** -0.5
    c1 = scale * LOG2E
    ng = seq // CMP_STRIDE
    ns = seq // SLC_BLOCK
    top = min(SLC_TOPK, ns)
    m_ref[...] = jnp.full(m_ref.shape, NEG_INF, F32)
    acc_ref[...] = jnp.zeros(acc_ref.shape, F32)
    rel = _rel_t(tk, tq)

    qt = q_ref[...].astype(F32).T.astype(BF16)
    zeros_q = jnp.zeros((HEAD_DIM, tq), BF16)
    for h in range(HEADS):
        qh = qt[h * HEAD_DIM:(h + 1) * HEAD_DIM]
        alibi = _alibi_rows(NSA_SLOPES[h] / scale, tq, tk=tk)
        wq_ref[h, 0:LANES, :] = jnp.concatenate([qh, zeros_q], axis=0)
        wq_ref[h, LANES:2 * LANES, :] = alibi
        wq_ref[HEADS + h, 0:LANES, :] = jnp.concatenate([zeros_q, qh], axis=0)
        wq_ref[HEADS + h, LANES:2 * LANES, :] = alibi

    t_lane = qi * tq + lax.broadcasted_iota(jnp.int32, (1, tq), 1)
    n_sub = lax.broadcasted_iota(jnp.int32, (ng, 1), 0)
    c_valid = (n_sub * CMP_STRIDE + (CMP_LEN - 1)) <= t_lane
    kc = kc_ref[0]
    vct_ext = jnp.concatenate([vct_ref[0], jnp.ones((BF16_ROWS, ng), BF16)], axis=0)
    ovl = ovl_ref[...]
    cmp_max = []
    for h in range(HEADS):
        rhs = jnp.concatenate([qt[h * HEAD_DIM:(h + 1) * HEAD_DIM],
                               _alibi_rows(NSA_SLOPES[h] * CMP_STRIDE / scale, tq, HEAD_DIM)], axis=0)
        cmp_max.append(_stage_scores(sc_ref, h, slice(0, tq), _dot(kc, rhs), c_valid))
    imp = jnp.zeros((ns, tq), F32)
    o_cmp = []
    for h in range(HEADS):
        mx = jnp.where(cmp_max[h] > NEG_INF, cmp_max[h], 0.0)
        pb = jnp.exp2(c1 * sc_ref[h] - c1 * mx).astype(BF16)
        o_ext = _dot(vct_ext, pb)
        r = 1.0 / jnp.maximum(o_ext[HEAD_DIM:HEAD_DIM + 1], 1e-30)
        o_cmp.append(o_ext[0:HEAD_DIM] * r)
        imp = imp + _dot(ovl, pb) * r

    def select_blocks():
        j_sub = lax.broadcasted_iota(jnp.int32, (ns, 1), 0)
        j_sub_f = j_sub.astype(F32)
        cur = jnp.right_shift(t_lane, SLC_SHIFT)
        forced = (j_sub == 0) | (j_sub == cur) | (j_sub == cur - 1)
        valid = (j_sub * SLC_BLOCK) <= t_lane
        score = jnp.where(forced, 1e30, jnp.where(valid, imp, -1.0))
        sel = jnp.zeros((ns, tq), F32)
        for _ in range(top):
            mx = jnp.max(score, axis=0, keepdims=True)
            idx = jnp.min(jnp.where(score == mx, j_sub_f, float(ns)), axis=0, keepdims=True)
            pick = j_sub_f == idx
            sel = jnp.where(pick, 1.0, sel)
            score = jnp.where(pick, -2.0, score)
        unsel = jnp.where(sel > 0.5, 0.0, -MASK_BIG).astype(BF16)
        for h in range(HEADS):
            wq_ref[h, LANES + BLK_LANE0:LANES + BLK_LANE0 + ns, :] = unsel

    aux_lane = lax.broadcasted_iota(jnp.int32, (tk, LANES), 1)
    aux_blk = jnp.right_shift(lax.broadcasted_iota(jnp.int32, (tk, LANES), 0), SLC_SHIFT) + BLK_LANE0
    kpos = _key_pos_features(tk)
    ones = jnp.ones((BF16_ROWS, tk), BF16)

    n_slots = s_ref.shape[0]

    def tiles(window, kis, cols, mask_offs=None):
        base = HEADS if window else 0
        loaded = []
        for t, ki in enumerate(kis):
            start = pl.multiple_of(ki * tk, tk)
            k = k_ref[pl.ds(start, tk), :]
            if window:
                vt = vt_ref[HEAD_DIM:2 * HEAD_DIM, pl.ds(start, tk)]
                aux = kpos
                dist = rel[:, cols[t]] + (qi * tq - ki * tk)
                keep = (dist >= 0) & (dist < WINDOW)
            else:
                vt = vt_ref[0:HEAD_DIM, pl.ds(start, tk)]
                onehot = aux_lane == aux_blk + ki * (tk // SLC_BLOCK)
                aux = jnp.where(onehot, jnp.ones_like(kpos), kpos)
                keep = None
            loaded.append((jnp.concatenate([k, aux], axis=1),
                           jnp.concatenate([vt, ones], axis=0), (ki * tk).astype(F32), keep))
        items = [(t, h) for t in range(len(kis)) for h in range(HEADS)]
        s_max = {}

        def scores(i):
            t, h = items[i]
            if window:
                keep = loaded[t][3]
            else:
                keep = None if mask_offs[t] is None else rel[:, cols[t]] >= mask_offs[t]
            s_max[i] = _stage_scores(s_ref, i % n_slots, cols[t],
                                     _dot(loaded[t][0], wq_ref[base + h, :, cols[t]]), keep)

        for i in range(NSA_LOOKAHEAD):
            scores(i)
        for i, (t, h) in enumerate(items):
            if i + NSA_LOOKAHEAD < len(items):
                scores(i + NSA_LOOKAHEAD)
            _lhs, vt_ext, key0, _keep = loaded[t]
            _flash_update_t(lambda i=i, t=t: s_ref[i % n_slots, :, cols[t]], s_max.pop(i), c1,
                            (NSA_SLOPES[h] * LOG2E) * key0, vt_ext, m_ref, acc_ref, base + h, cols[t])

    all_cols = slice(0, tq)
    diag_cols = [slice(d * tk, tq) for d in range(kpq)]

    tiles(True, [qi * kpq + d for d in range(kpq)], diag_cols)
    select_blocks()

    def loop_body(i):
        tiles(False, [i * kpq + d for d in range(kpq)], [all_cols] * kpq, [None] * kpq)

    def back_window():
        n_back = (WINDOW + tk - 1) // tk
        for g in range((n_back + kpq - 1) // kpq):
            backs = list(range(g * kpq + 1, min((g + 1) * kpq, n_back) + 1))
            back_cols = [slice(0, min(tq, -(-(WINDOW - (back - 1) * tk - 1) // LANES) * LANES)) for back in backs]

            @pl.when(qi > g)
            def _():
                tiles(True, [qi * kpq - back for back in backs], back_cols)

    def diag_tiles():
        tiles(False, [qi * kpq + d for d in range(kpq)], diag_cols, [d * tk for d in range(kpq)])

    def finish():
        g_t = jax.nn.sigmoid(misc_ref[...]).T
        outs = []
        for h in range(HEADS):
            r0 = GATE_COL + 3 * h
            outs.append(g_t[r0:r0 + 1] * o_cmp[h] + g_t[r0 + 1:r0 + 2] * _normalized(acc_ref[h])
                        + g_t[r0 + 2:r0 + 3] * _normalized(acc_ref[HEADS + h]))
        y = jnp.concatenate(outs, axis=0).T * _silu(z_ref[...])
        o_ref[...] = y.astype(o_ref.dtype)

    return loop_body, back_window, diag_tiles, finish


def _overlap_t(seq):
    nc = (seq - CMP_LEN) // CMP_STRIDE + 1
    ng = seq // CMP_STRIDE
    ns = seq // SLC_BLOCK
    c_start = np.arange(ng) * CMP_STRIDE
    c_end = c_start + CMP_LEN - 1
    s_start = np.arange(ns) * SLC_BLOCK
    s_end = s_start + SLC_BLOCK - 1
    ov = (c_start[None, :] <= s_end[:, None]) & (c_end[None, :] >= s_start[:, None]) & (np.arange(ng)[None, :] < nc)
    return jnp.asarray(ov.astype(np.float32), dtype=BF16)


def _nsa_operands(q, k2, vt_all, kc, vct, misc, z_all, seq, nq, tq, tk):
    ng = seq // CMP_STRIDE
    ns = seq // SLC_BLOCK
    assert tq - tk < WINDOW and tk % SLC_BLOCK == 0 and ns <= LANES - BLK_LANE0
    inputs = [_overlap_t(seq), q, k2, vt_all, kc, vct, misc, z_all]
    in_specs = [pl.BlockSpec((ns, ng), lambda b, i: (0, 0)),
                pl.BlockSpec((tq, GROUP_W), lambda b, i: (b * nq + i, 0)),
                pl.BlockSpec((seq, LANES), lambda b, i: (b, 0)),
                pl.BlockSpec((2 * HEAD_DIM, seq), lambda b, i: (NSA_VT_BLK, b)),
                pl.BlockSpec((1, ng, LANES), lambda b, i: (b, 0, 0)),
                pl.BlockSpec((1, HEAD_DIM, ng), lambda b, i: (b, 0, 0)),
                pl.BlockSpec((tq, LANES), lambda b, i: (b * nq + i, 0)),
                pl.BlockSpec((tq, GROUP_W), lambda b, i: (b * nq + i, 0))]
    scratch = [pltpu.VMEM((2 * HEADS, 1, tq), F32),
               pltpu.VMEM((2 * HEADS, ACC_ROWS, tq), F32),
               pltpu.VMEM((2 * HEADS, 2 * LANES, tq), BF16),
               pltpu.VMEM((2 * HEADS, tk, tq), F32),
               pltpu.VMEM((HEADS, ng, tq), F32)]
    return inputs, in_specs, scratch


def _mixers_kernel(*refs, n_in, n_scratch, tq, tk_nsa, tk_diff, seq, lam_init):
    bounds = np.cumsum([0] + list(n_in))
    nsa_in, diff_in, ret_in, ssd_in = (refs[a:b] for a, b in zip(bounds[:-1], bounds[1:]))
    nsa_out, diff_out, ret_out, ssd_out = refs[bounds[-1]:bounds[-1] + 4]
    sb = np.cumsum([0] + list(n_scratch)) + bounds[-1] + 4
    nsa_scr, diff_scr, rec_scr = (refs[a:b] for a, b in zip(sb[:-1], sb[1:]))
    ret_st, ssd_ext, ssd_st = rec_scr
    qi = pl.program_id(1)

    @pl.when(qi == 0)
    def _():
        ret_st[...] = jnp.zeros(ret_st.shape, F32)
        ssd_st[...] = jnp.zeros(ssd_st.shape, F32)
        ssd_ext[0:8, :] = jnp.zeros((8, CONV_CH), F32)

    chunks = list(range(ret_out.shape[0]))
    first, second = chunks[:len(chunks) // 2], chunks[len(chunks) // 2:]
    nsa_loop, nsa_back_window, nsa_diag, nsa_finish = _nsa_phases(
        *nsa_in, nsa_out, *nsa_scr, tq=tq, tk=tk_nsa, seq=seq)
    diff_loop, diff_diag, diff_finish = _diff_phases(
        *diff_in, diff_out, *diff_scr, tq=tq, tk=tk_diff, lam_init=lam_init)
    _ret_body(*ret_in, ret_out, ret_st, first)
    _ssd_body(*ssd_in, ssd_out, ssd_ext, ssd_st, first)

    def body(i, carry):
        nsa_loop(i)
        diff_loop(i)
        return carry

    lax.fori_loop(0, qi, body, 0)
    nsa_back_window()
    nsa_diag()
    diff_diag()
    _ret_body(*ret_in, ret_out, ret_st, second)
    _ssd_body(*ssd_in, ssd_out, ssd_ext, ssd_st, second)
    nsa_finish()
    diff_finish()


def _mixers(nsa_args, diff_args, rec_args, batch, seq, layer_idx, tq, tk_nsa, tk_diff):
    nq = seq // tq
    nsa_in, nsa_specs, nsa_scratch = _nsa_operands(*nsa_args, seq, nq, tq, tk_nsa)
    diff_in, diff_specs, diff_scratch = _diff_operands(*diff_args, seq, nq, tq, tk_diff)
    ret_in, ssd_in, rec_specs, rec_out_spec, rec_scratch = _recurrent_operands(*rec_args, nq, tq)
    out_spec = pl.BlockSpec((tq, GROUP_W), lambda b, i: (b * nq + i, 0))
    m = batch * seq
    y_nsa, y_diff, y_ret, y_ssm = pl.pallas_call(
        functools.partial(_mixers_kernel, n_in=(len(nsa_in), len(diff_in), len(ret_in), len(ssd_in)),
                          n_scratch=(len(nsa_scratch), len(diff_scratch), len(rec_scratch)),
                          tq=tq, tk_nsa=tk_nsa, tk_diff=tk_diff, seq=seq,
                          lam_init=0.8 - 0.6 * math.exp(-0.3 * layer_idx)),
        grid=(batch, nq),
        in_specs=nsa_specs + diff_specs + rec_specs,
        out_specs=[out_spec, out_spec, rec_out_spec, rec_out_spec],
        out_shape=[jax.ShapeDtypeStruct((m, GROUP_W), BF16)] * 2
                  + [jax.ShapeDtypeStruct((m // SSM_CHUNK, SSM_CHUNK, GROUP_W), BF16)] * 2,
        scratch_shapes=nsa_scratch + diff_scratch + rec_scratch,
        compiler_params=_params(("parallel", "arbitrary")),
        name="mixers",
    )(*nsa_in, *diff_in, *ret_in, *ssd_in)
    return y_nsa, y_diff, y_ret.reshape(m, GROUP_W), y_ssm.reshape(m, GROUP_W)


def _ret_tables():
    c = RET_CHUNK
    h = np.arange(HEADS, dtype=np.float32)
    log_g = jnp.log(1.0 - 2.0 ** (-5.0 - jnp.asarray(h)))
    pos = jnp.arange(c, dtype=F32)
    rel = pos[:, None] - pos[None, :]
    decay = jnp.where(rel >= 0, jnp.exp(log_g[:, None, None] * jnp.maximum(rel, 0.0)), 0.0)
    xi = jnp.exp(log_g[:, None] * (pos + 1.0))
    zeta = jnp.exp(log_g[:, None] * (c - 1.0 - pos))
    chunk_decay = jnp.exp(log_g * c)
    xi_tab = jnp.repeat(xi.T, HEAD_DIM, axis=1)
    zeta_tab = jnp.repeat(zeta.T, HEAD_DIM, axis=1)
    cd_tab = jnp.repeat(chunk_decay, HEAD_DIM)[None, :]
    return decay, xi_tab, zeta_tab, cd_tab


def _ret_body(decay_ref, xi_ref, zeta_ref, cd_ref, gn_ref, q_ref, k_ref, v_ref, z_ref, o_ref, st_ref, blocks):
    xi = xi_ref[...]
    cd = cd_ref[...]
    for bb in blocks:
        q = (q_ref[bb].astype(F32) * (HEAD_DIM ** -0.5)).astype(BF16)
        k = k_ref[bb]
        v = v_ref[bb]
        kz_t = (k.astype(F32) * zeta_ref[...]).T.astype(BF16)
        outs = []
        for h in range(HEADS):
            sl = slice(h * HEAD_DIM, (h + 1) * HEAD_DIM)
            qh, kh, vh = q[:, sl], k[:, sl], v[:, sl]
            prev = st_ref[h]
            inner = (_dot_nt(qh, kh) * decay_ref[h]).astype(BF16)
            o = _dot(inner, vh) + _dot(qh, prev.astype(BF16)) * xi[:, sl]
            st_ref[h] = prev * cd[:, sl] + _dot(kz_t[sl, :], vh)
            mu = jnp.mean(o, axis=-1, keepdims=True)
            d = o - mu
            var = jnp.mean(d * d, axis=-1, keepdims=True)
            outs.append(d * lax.rsqrt(var + EPS))
        y = jnp.concatenate(outs, axis=-1) * gn_ref[...] * _silu(z_ref[bb])
        o_ref[bb] = y.astype(o_ref.dtype)


def _ssd_body(cw_ref, cb_ref, dtb_ref, a_ref, dsk_ref, nw_ref, xbc_ref, misc_ref, z_ref, o_ref,
              ext_ref, st_ref, blocks):
    L = SSM_CHUNK
    hi = lax.Precision.HIGHEST
    row = lax.broadcasted_iota(jnp.int32, (L, L), 0)
    col = lax.broadcasted_iota(jnp.int32, (L, L), 1)
    causal = row >= col
    tril = jnp.where(causal, 1.0, 0.0).astype(F32)
    dsk = dsk_ref[...]

    for bb in blocks:
        raw = xbc_ref[bb]
        ext_ref[8:8 + L, :] = raw
        conv = cb_ref[...] + raw * cw_ref[CONV_W - 1:CONV_W, :]
        for w in range(CONV_W - 1):
            shift = CONV_W - 1 - w
            conv = conv + ext_ref[8 - shift:8 - shift + L, :] * cw_ref[w:w + 1, :]
        ext_ref[0:8, :] = raw[L - 8:L, :]
        xc = _silu(conv)
        x = xc[:, 0:GROUP_W]
        bm = xc[:, GROUP_W:GROUP_W + 2 * SSM_STATE].astype(BF16)
        cm = xc[:, GROUP_W + 2 * SSM_STATE:].astype(BF16)

        dt_full = jax.nn.softplus(misc_ref[bb] + dtb_ref[...])
        da = dt_full * a_ref[...]
        cs_col = jnp.dot(tril, da, precision=hi, preferred_element_type=F32)
        cs_row = lax.dot_general(da, tril, (((0,), (1,)), ((), ())), precision=hi,
                                 preferred_element_type=F32)

        outs = []
        for h in range(HEADS):
            g = h // 2
            c0 = DT_COL + h
            sl = slice(h * HEAD_DIM, (h + 1) * HEAD_DIM)
            gs = slice(g * SSM_STATE, (g + 1) * SSM_STATE)
            cs_c = cs_col[:, c0:c0 + 1]
            cs_r = cs_row[c0:c0 + 1, :]
            cs_last = cs_col[L - 1:L, c0:c0 + 1]
            xh = x[:, sl]
            xdt = xh * dt_full[:, c0:c0 + 1]
            seg = jnp.exp(jnp.where(causal, cs_c - cs_r, NEG_INF))
            cb = _dot_nt(cm[:, gs], bm[:, gs])
            y = _dot((cb * seg).astype(BF16), xdt.astype(BF16))
            prev = st_ref[h]
            y = y + _dot(cm[:, gs], prev.astype(BF16)) * jnp.exp(cs_c)
            y = y + dsk[:, sl] * xh
            dec = jnp.exp(cs_last - cs_c)
            st_ref[h] = prev * jnp.exp(cs_last) + _dot_tn(bm[:, gs], (xdt * dec).astype(BF16))
            outs.append(y)
        y = jnp.concatenate(outs, axis=-1) * _silu(z_ref[bb])
        ms = jnp.mean(y * y, axis=-1, keepdims=True)
        o_ref[bb] = (y * lax.rsqrt(ms + EPS) * nw_ref[...]).astype(o_ref.dtype)


def _recurrent_operands(ret_qkv, xbc, misc, z_all, gn_w, conv_w, conv_b, dt_bias, a_log, d_skip, norm_w, nq, tq):
    L = SSM_CHUNK
    assert RET_CHUNK == L and tq % L == 0
    cpt = tq // L
    decay, xi_tab, zeta_tab, cd_tab = _ret_tables()
    dtb = jnp.zeros((1, 128), F32).at[0, DT_COL:DT_COL + HEADS].set(dt_bias)
    a_full = jnp.zeros((1, 128), F32).at[0, DT_COL:DT_COL + HEADS].set(-jnp.exp(a_log))
    dsk = jnp.repeat(d_skip, HEAD_DIM)[None, :]

    def full(shape):
        return pl.BlockSpec(shape, lambda b, i: (0,) * len(shape))

    def blk(width, col):
        return pl.BlockSpec((cpt, L, width), lambda b, i: (b * nq + i, 0, col))

    def chunked(a):
        return a.reshape(-1, L, a.shape[-1])

    qkv3, z3 = chunked(ret_qkv), chunked(z_all)
    ret_in = [decay, xi_tab, zeta_tab, cd_tab, gn_w.reshape(1, GROUP_W), qkv3, qkv3, qkv3, z3]
    ret_specs = [full((HEADS, L, L)), full((L, GROUP_W)), full((L, GROUP_W)), full((1, GROUP_W)),
                 full((1, GROUP_W)), blk(GROUP_W, 0), blk(GROUP_W, 1), blk(GROUP_W, 2), blk(GROUP_W, 2)]
    ssd_in = [conv_w, conv_b.reshape(1, CONV_CH), dtb, a_full, dsk, norm_w.reshape(1, GROUP_W),
              chunked(xbc), chunked(misc), z3]
    ssd_specs = [full((CONV_W, CONV_CH)), full((1, CONV_CH)), full((1, 128)), full((1, 128)),
                 full((1, GROUP_W)), full((1, GROUP_W)), blk(CONV_CH, 0), blk(128, 0), blk(GROUP_W, 3)]
    scratch = [pltpu.VMEM((HEADS, HEAD_DIM, HEAD_DIM), F32),
               pltpu.VMEM((8 + L, CONV_CH), F32),
               pltpu.VMEM((HEADS, SSM_STATE, HEAD_DIM), F32)]
    return ret_in, ssd_in, ret_specs + ssd_specs, blk(GROUP_W, 0), scratch


def _pick_tile(n, pref):
    t = pref
    while n % t:
        t //= 2
    return t


def kernel(x, norm_w, w_in, w_out, nsa_pe_k, nsa_pe_v, nsa_w_ck1, nsa_w_ck2, nsa_w_cv1, nsa_w_cv2,
           diff_lam_q1, diff_lam_k1, diff_lam_q2, diff_lam_k2, diff_subln_w, ret_gn_w,
           ssm_conv_w, ssm_conv_b, ssm_dt_bias, ssm_A_log, ssm_D, ssm_norm_w, final_norm_w):
    batch, seq, _ = x.shape
    depth = w_in.shape[0]
    m = batch * seq
    tm = _pick_tile(m, 512)
    tq = _pick_tile(seq, 512)
    tk = _pick_tile(seq, 256)
    tk_diff = _pick_tile(seq, 512)
    w_in_b = w_in.astype(BF16)
    w_r = _relayout_w_in(w_in_b)
    w_t = _relayout_w_in_t(w_in_b)
    w_out_b = w_out.astype(BF16)
    x2d = x.reshape(m, D_MODEL)
    projected = _in_proj(x2d, norm_w[0], w_r[0], w_t[0], tm)
    for i in range(depth):
        nsa_q, nsa_k2, nsa_cmp, misc, z_all, diff_qk, ret_qkv, xbc, vt_all = projected
        kc, vct = _nsa_compress(nsa_cmp, nsa_pe_k[i], nsa_pe_v[i], nsa_w_ck1[i], nsa_w_ck2[i],
                                nsa_w_cv1[i], nsa_w_cv2[i], batch, seq)
        lam_p = jnp.stack([diff_lam_q1[i], diff_lam_k1[i], diff_lam_q2[i], diff_lam_k2[i]])
        ys = _mixers((nsa_q, nsa_k2, vt_all, kc, vct, misc, z_all),
                     (diff_qk, vt_all, z_all, lam_p, diff_subln_w[i]),
                     (ret_qkv, xbc, misc, z_all, ret_gn_w[i], ssm_conv_w[i], ssm_conv_b[i], ssm_dt_bias[i],
                      ssm_A_log[i], ssm_D[i], ssm_norm_w[i]),
                     batch, seq, i, tq, tk, tk_diff)
        if i + 1 < depth:
            x2d, *projected = _out_in_proj(ys, w_out_b[i], x2d, norm_w[i + 1], w_r[i + 1], w_t[i + 1], tm)
        else:
            x2d = _out_proj(ys, w_out_b[i], x2d, final_norm_w, tm)
    return x2d.reshape(batch, seq, D_MODEL)
```

```python
import functools
import math

import numpy as np
import jax
import jax.numpy as jnp
from jax import lax
from jax.experimental import pallas as pl
from jax.experimental.pallas import tpu as pltpu

F32 = jnp.float32
BF16 = jnp.bfloat16
NEG_INF = float("-inf")
LOG2E = 1.4426950408889634

D_MODEL = 1024
DEPTH = 4
GROUP_W = 256
HEADS = 4
HEAD_DIM = 64
EPS = 1e-6
CMP_LEN = 32
CMP_STRIDE = 16
CMP_HIDDEN = 256
SLC_BLOCK = 64
SLC_SHIFT = 6
SLC_TOPK = 16
WINDOW = 512
DIFF_QK_DIM = 32
RET_CHUNK = 128
SSM_STATE = 128
SSM_CHUNK = 128
CONV_W = 4
CONV_CH = 768
N_ALIBI_HEADS = 8
LANES = 128
BF16_ROWS = 16
ACC_ROWS = HEAD_DIM + BF16_ROWS
QK_LOOKAHEAD = 3
ALIBI_ROWS = 3
POS_RADIX = 256
MASK_BIG = 2.0 ** 100
BLK_LANE0 = 64

IN_LAYOUT = (
    ("nsa_q", 256), ("nsa_k_cmp", 64), ("nsa_v_cmp", 64), ("nsa_k_slc", 64), ("nsa_v_slc", 64),
    ("nsa_k_win", 64), ("nsa_v_win", 64), ("nsa_gate", 12), ("nsa_z", 256),
    ("diff_q", 256), ("diff_k", 256), ("diff_v", 256), ("diff_z", 256),
    ("ret_q", 256), ("ret_k", 256), ("ret_v", 256), ("ret_z", 256),
    ("ssm_z", 256), ("ssm_xbc", 768), ("ssm_dt", 4),
)
IN_OFF = {}
_o = 0
for _n, _w in IN_LAYOUT:
    IN_OFF[_n] = (_o, _w)
    _o += _w
IN_W = _o

GATE_COL = 0
DT_COL = 12
IN_OUTPUTS = (
    ("nsa_q", BF16, ("nsa_q",), 256),
    ("nsa_k2", BF16, ("nsa_k_slc", "nsa_k_win"), 128),
    ("nsa_cmp", F32, ("nsa_k_cmp", "nsa_v_cmp"), 128),
    ("misc", F32, ("nsa_gate", "ssm_dt"), 128),
    ("z_all", F32, ("nsa_z", "diff_z", "ret_z", "ssm_z"), 1024),
    ("diff_qk", BF16, ("diff_q", "diff_k"), 512),
    ("ret_qkv", BF16, ("ret_q", "ret_k", "ret_v"), 768),
    ("xbc", F32, ("ssm_xbc",), 768),
)
IN_T_SRC = ("diff_v", "nsa_v_slc", "nsa_v_win")
IN_T_ROWS = 384
DIFF_VT_BLK = 0
NSA_VT_BLK = 2
IN_SEGS = []
_o = 0
for _n, _dt, _src, _w in IN_OUTPUTS:
    IN_SEGS.append((_o, _o + _w))
    _o += _w
IN_WP = _o

VMEM_LIMIT = 56 * 1024 * 1024


def _alibi_slopes():
    return [2.0 ** (-8.0 * (i + 1) / N_ALIBI_HEADS) for i in range(N_ALIBI_HEADS)]


NSA_SLOPES = _alibi_slopes()[0::2]
DIFF_SLOPES = _alibi_slopes()[1::2]


def _silu(x):
    return x * jax.nn.sigmoid(x)


def _dot(a, b):
    return jnp.dot(a, b, preferred_element_type=F32)


def _dot_nt(a, b):
    return lax.dot_general(a, b, (((1,), (1,)), ((), ())), preferred_element_type=F32)


def _dot_tn(a, b):
    return lax.dot_general(a, b, (((0,), (0,)), ((), ())), preferred_element_type=F32)


def _params(sem):
    return pltpu.CompilerParams(dimension_semantics=sem, vmem_limit_bytes=VMEM_LIMIT)


def _relayout_w_in(w_in):
    cols = []
    for _n, _dt, src, width in IN_OUTPUTS:
        used = 0
        for s in src:
            off, w = IN_OFF[s]
            cols.append(w_in[:, :, off:off + w])
            used += w
        if used < width:
            cols.append(jnp.zeros(w_in.shape[:2] + (width - used,), w_in.dtype))
    return jnp.concatenate(cols, axis=-1)


def _relayout_w_in_t(w_in):
    cols = [w_in[:, :, IN_OFF[s][0]:IN_OFF[s][0] + IN_OFF[s][1]] for s in IN_T_SRC]
    return jnp.swapaxes(jnp.concatenate(cols, axis=-1), 1, 2)


def _norm_project(x, nw_ref, w_ref, wt_ref, out_refs):
    ms = jnp.mean(x * x, axis=-1, keepdims=True)
    h = (x * lax.rsqrt(ms + EPS) * nw_ref[...]).astype(BF16)
    for ref, (a, b) in zip(out_refs[:-1], IN_SEGS):
        ref[...] = _dot(h, w_ref[:, a:b]).astype(ref.dtype)
    out_refs[-1][...] = _dot_nt(wt_ref[...], h).astype(BF16)


def _in_proj_kernel(x_ref, nw_ref, w_ref, wt_ref, *out_refs):
    _norm_project(x_ref[...], nw_ref, w_ref, wt_ref, out_refs)


def _in_proj_outputs(m, tm):
    out_shape = [jax.ShapeDtypeStruct((m, b - a), dt) for (_n, dt, _s, _w), (a, b) in zip(IN_OUTPUTS, IN_SEGS)]
    out_specs = [pl.BlockSpec((tm, b - a), lambda i: (i, 0)) for (a, b) in IN_SEGS]
    out_shape.append(jax.ShapeDtypeStruct((IN_T_ROWS, m), BF16))
    out_specs.append(pl.BlockSpec((IN_T_ROWS, tm), lambda i: (0, i)))
    return out_shape, out_specs


def _in_proj(x2d, norm_w, w_r, w_t, tm):
    m = x2d.shape[0]
    out_shape, out_specs = _in_proj_outputs(m, tm)
    return pl.pallas_call(
        _in_proj_kernel,
        grid=(m // tm,),
        in_specs=[pl.BlockSpec((tm, D_MODEL), lambda i: (i, 0)),
                  pl.BlockSpec((1, D_MODEL), lambda i: (0, 0)),
                  pl.BlockSpec((D_MODEL, IN_WP), lambda i: (0, 0)),
                  pl.BlockSpec((IN_T_ROWS, D_MODEL), lambda i: (0, 0))],
        out_specs=out_specs,
        out_shape=out_shape,
        compiler_params=_params(("parallel",)),
        name="in_proj",
    )(x2d, norm_w.reshape(1, D_MODEL), w_r, w_t)


def _out_proj_kernel(y0_ref, y1_ref, y2_ref, y3_ref, w_ref, x_ref, fw_ref, o_ref):
    acc = x_ref[...]
    for g, y_ref in enumerate((y0_ref, y1_ref, y2_ref, y3_ref)):
        acc = acc + _dot(y_ref[...], w_ref[g * GROUP_W:(g + 1) * GROUP_W, :])
    ms = jnp.mean(acc * acc, axis=-1, keepdims=True)
    o_ref[...] = acc * lax.rsqrt(ms + EPS) * fw_ref[...]


def _out_proj(ys, w_out_b, x2d, final_w, tm):
    m = x2d.shape[0]
    yspec = pl.BlockSpec((tm, GROUP_W), lambda i: (i, 0))
    return pl.pallas_call(
        _out_proj_kernel,
        grid=(m // tm,),
        in_specs=[yspec, yspec, yspec, yspec,
                  pl.BlockSpec((D_MODEL, D_MODEL), lambda i: (0, 0)),
                  pl.BlockSpec((tm, D_MODEL), lambda i: (i, 0)),
                  pl.BlockSpec((1, D_MODEL), lambda i: (0, 0))],
        out_specs=pl.BlockSpec((tm, D_MODEL), lambda i: (i, 0)),
        out_shape=jax.ShapeDtypeStruct((m, D_MODEL), F32),
        compiler_params=_params(("parallel",)),
        name="out_proj",
    )(*ys, w_out_b, x2d, final_w.reshape(1, D_MODEL))


def _out_in_proj_kernel(y0_ref, y1_ref, y2_ref, y3_ref, wo_ref, x_ref, nw_ref, w_ref, wt_ref, xo_ref, *out_refs):
    acc = x_ref[...]
    for g, y_ref in enumerate((y0_ref, y1_ref, y2_ref, y3_ref)):
        acc = acc + _dot(y_ref[...], wo_ref[g * GROUP_W:(g + 1) * GROUP_W, :])
    xo_ref[...] = acc
    _norm_project(acc, nw_ref, w_ref, wt_ref, out_refs)


def _out_in_proj(ys, w_out_b, x2d, norm_w, w_r, w_t, tm):
    m = x2d.shape[0]
    yspec = pl.BlockSpec((tm, GROUP_W), lambda i: (i, 0))
    xspec = pl.BlockSpec((tm, D_MODEL), lambda i: (i, 0))
    out_shape, out_specs = _in_proj_outputs(m, tm)
    return pl.pallas_call(
        _out_in_proj_kernel,
        grid=(m // tm,),
        in_specs=[yspec, yspec, yspec, yspec,
                  pl.BlockSpec((D_MODEL, D_MODEL), lambda i: (0, 0)),
                  xspec,
                  pl.BlockSpec((1, D_MODEL), lambda i: (0, 0)),
                  pl.BlockSpec((D_MODEL, IN_WP), lambda i: (0, 0)),
                  pl.BlockSpec((IN_T_ROWS, D_MODEL), lambda i: (0, 0))],
        out_specs=[xspec] + out_specs,
        out_shape=[jax.ShapeDtypeStruct((m, D_MODEL), F32)] + out_shape,
        compiler_params=_params(("parallel",)),
        name="out_in_proj",
    )(*ys, w_out_b, x2d, norm_w.reshape(1, D_MODEL), w_r, w_t)


def _stage_scores(s_ref, slot, cols, s, keep=None):
    if keep is not None:
        s = jnp.where(keep, s, NEG_INF)
    s_ref[slot, :, cols] = s
    return jnp.max(s, axis=0, keepdims=True)


def _flash_update_t(s, s_max, c1, shift, vt_ext, m_ref, acc_ref, idx, cols):
    m_old = m_ref[idx, :, cols]
    m_new = jnp.maximum(m_old, c1 * s_max + shift)
    alpha = jnp.exp2(m_old - m_new)
    p = jnp.exp2(c1 * s() - (m_new - shift))
    acc_ref[idx, :, cols] = alpha * acc_ref[idx, :, cols] + _dot(vt_ext, p.astype(BF16))
    m_ref[idx, :, cols] = m_new


def _bf16_pieces(x, n=3):
    out = []
    for _ in range(n):
        p = float(np.asarray(x, np.float32).astype(BF16).astype(np.float32))
        out.append(p)
        x = x - p
    return out


def _key_pos_features(tk):
    lane = lax.broadcasted_iota(jnp.int32, (tk, LANES), 1)
    row = lax.broadcasted_iota(jnp.int32, (tk, LANES), 0)
    out = jnp.zeros((tk, LANES), jnp.int32)
    for d in range(-(-tk // POS_RADIX)):
        digit = jnp.bitwise_and(jnp.right_shift(row, 8 * d), POS_RADIX - 1)
        out = jnp.where((lane >= ALIBI_ROWS * d) & (lane < ALIBI_ROWS * (d + 1)), digit, out)
    return out.astype(F32).astype(BF16)


def _alibi_rows(beta, tq, rows=LANES, tk=POS_RADIX):
    row = lax.broadcasted_iota(jnp.int32, (rows, tq), 0)
    out = jnp.zeros((rows, tq), F32)
    for d in range(-(-tk // POS_RADIX)):
        for r, piece in enumerate(_bf16_pieces(beta * POS_RADIX ** d, ALIBI_ROWS)):
            out = jnp.where(row == ALIBI_ROWS * d + r, piece, out)
    return out.astype(BF16)


def _normalized(acc):
    return acc[0:HEAD_DIM] / jnp.maximum(acc[HEAD_DIM:HEAD_DIM + 1], 1e-30)


def _rel_t(tk, tq):
    return lax.broadcasted_iota(jnp.int32, (tk, tq), 1) - lax.broadcasted_iota(jnp.int32, (tk, tq), 0)


def _diff_phases(lam_ref, sw_ref, q_ref, k_ref, vt_ref, z_ref, o_ref, m_ref, acc_ref, wq_ref, s_ref,
                 *, tq, tk, lam_init):
    qi = pl.program_id(1)
    kpq = tq // tk
    scale = DIFF_QK_DIM ** -0.5
    c1 = scale * LOG2E
    m_ref[...] = jnp.full(m_ref.shape, NEG_INF, F32)
    acc_ref[...] = jnp.zeros(acc_ref.shape, F32)
    rel = _rel_t(tk, tq)

    qt = q_ref[...].astype(F32).T.astype(BF16)
    row = lax.broadcasted_iota(jnp.int32, (LANES, tq), 0)
    for j in range(2 * HEADS):
        g, r0 = divmod(j * DIFF_QK_DIM, LANES)
        qg = qt[g * LANES:(g + 1) * LANES]
        wq_ref[j, 0:LANES, :] = jnp.where((row >= r0) & (row < r0 + DIFF_QK_DIM), qg, jnp.zeros_like(qg))
        wq_ref[j, LANES:2 * LANES, :] = _alibi_rows(DIFF_SLOPES[j // 2] / scale, tq, tk=tk)

    ones = jnp.ones((BF16_ROWS, tk), BF16)
    kpos = _key_pos_features(tk)

    n_maps = 2 * HEADS
    n_slots = s_ref.shape[0]

    def tiles(kis, mask_offs):
        loaded = []
        for ki in kis:
            start = pl.multiple_of(ki * tk, tk)
            loaded.append((k_ref[pl.ds(start, tk), :],
                           vt_ref[:, pl.ds(start, tk)],
                           (ki * tk).astype(F32)))
        items = [(t, j) for t in range(len(kis)) for j in range(n_maps)]
        cols = slice(0, tq)

        def scores(i):
            t, j = items[i]
            g = (j * DIFF_QK_DIM) // LANES
            lhs = jnp.concatenate([loaded[t][0][:, g * LANES:(g + 1) * LANES], kpos], axis=1)
            s_ref[i % n_slots] = _dot(lhs, wq_ref[j])

        def staged(i):
            t = items[i][0]
            if mask_offs[t] is None:
                return lambda: s_ref[i % n_slots]
            return lambda: jnp.where(rel >= mask_offs[t], s_ref[i % n_slots], NEG_INF)

        for i in range(QK_LOOKAHEAD):
            scores(i)
        for i, (t, j) in enumerate(items):
            h = j // 2
            if i + QK_LOOKAHEAD < len(items):
                scores(i + QK_LOOKAHEAD)
            _k, vt, key0 = loaded[t]
            vt_ext = jnp.concatenate([vt[h * HEAD_DIM:(h + 1) * HEAD_DIM], ones], axis=0)
            s = staged(i)
            _flash_update_t(s, jnp.max(s(), axis=0, keepdims=True), c1, (DIFF_SLOPES[h] * LOG2E) * key0,
                            vt_ext, m_ref, acc_ref, j, cols)

    def loop_body(i):
        tiles([i * kpq + d for d in range(kpq)], [None] * kpq)

    def diag_tiles():
        tiles([qi * kpq + d for d in range(kpq)], [d * tk for d in range(kpq)])

    def finish():
        lp = lam_ref[...]
        lam = (jnp.exp(jnp.sum(lp[0:1] * lp[1:2], axis=-1, keepdims=True))
               - jnp.exp(jnp.sum(lp[2:3] * lp[3:4], axis=-1, keepdims=True)) + lam_init)
        sw = sw_ref[...]
        o_t = jnp.concatenate([_normalized(acc_ref[2 * h]) - lam * _normalized(acc_ref[2 * h + 1])
                               for h in range(HEADS)], axis=0)
        o = o_t.T
        outs = []
        for h in range(HEADS):
            oh = o[:, h * HEAD_DIM:(h + 1) * HEAD_DIM]
            ms = jnp.mean(oh * oh, axis=-1, keepdims=True)
            outs.append(oh * lax.rsqrt(ms + EPS) * sw * (1.0 - lam_init))
        y = jnp.concatenate(outs, axis=-1) * _silu(z_ref[...])
        o_ref[...] = y.astype(o_ref.dtype)

    return loop_body, diag_tiles, finish


def _diff_operands(qk, vt_all, z_all, lam_p, subln_w, seq, nq, tq, tk):
    inputs = [lam_p, subln_w.reshape(1, HEAD_DIM), qk, qk, vt_all, z_all]
    in_specs = [pl.BlockSpec((4, DIFF_QK_DIM), lambda b, i: (0, 0)),
                pl.BlockSpec((1, HEAD_DIM), lambda b, i: (0, 0)),
                pl.BlockSpec((tq, GROUP_W), lambda b, i: (b * nq + i, 0)),
                pl.BlockSpec((seq, GROUP_W), lambda b, i: (b, 1)),
                pl.BlockSpec((GROUP_W, seq), lambda b, i: (DIFF_VT_BLK, b)),
                pl.BlockSpec((tq, GROUP_W), lambda b, i: (b * nq + i, 1))]
    scratch = [pltpu.VMEM((2 * HEADS, 1, tq), F32),
               pltpu.VMEM((2 * HEADS, ACC_ROWS, tq), F32),
               pltpu.VMEM((2 * HEADS, 2 * LANES, tq), BF16),
               pltpu.VMEM((QK_LOOKAHEAD + 1, tk, tq), F32)]
    return inputs, in_specs, scratch


def _nsa_compress_kernel(cmp_ref, pe_ref, w1_ref, wk2_ref, wv2t_ref, kc_ref, vct_ref):
    ng = kc_ref.shape[1]
    top = jnp.zeros((ng, 2 * CMP_HIDDEN), F32)
    bot = jnp.zeros((ng, 2 * CMP_HIDDEN), F32)
    for p in range(CMP_STRIDE):
        x = cmp_ref[pl.ds(p, ng, stride=CMP_STRIDE), :]
        top = top + _dot((x + pe_ref[p:p + 1, :]).astype(BF16), w1_ref[p])
        bot = bot + _dot((x + pe_ref[CMP_STRIDE + p:CMP_STRIDE + p + 1, :]).astype(BF16), w1_ref[CMP_STRIDE + p])
    hid = _silu(top + pltpu.roll(bot, ng - 1, 0)).astype(BF16)
    hk, hv = hid[:, 0:CMP_HIDDEN], hid[:, CMP_HIDDEN:]
    kc = _dot(hk, wk2_ref[...])
    lane = lax.broadcasted_iota(jnp.int32, kc.shape, 1)
    blk = lax.broadcasted_iota(jnp.int32, kc.shape, 0).astype(F32)
    kc = jnp.where((lane >= HEAD_DIM) & (lane < HEAD_DIM + ALIBI_ROWS), blk, kc)
    kc_ref[0] = kc.astype(kc_ref.dtype)
    vct_ref[0] = _dot_nt(wv2t_ref[...], hv).astype(vct_ref.dtype)


def _nsa_compress(cmp2d, pe_k, pe_v, w_ck1, w_ck2, w_cv1, w_cv2, batch, seq):
    ng = seq // CMP_STRIDE
    wk = w_ck1.reshape(CMP_LEN, HEAD_DIM, CMP_HIDDEN)
    wv = w_cv1.reshape(CMP_LEN, HEAD_DIM, CMP_HIDDEN)
    w1 = jnp.concatenate([jnp.pad(wk, ((0, 0), (0, 0), (0, CMP_HIDDEN))),
                          jnp.pad(wv, ((0, 0), (0, 0), (CMP_HIDDEN, 0)))], axis=1).astype(BF16)
    pe = jnp.concatenate([pe_k, pe_v], axis=1)

    def full(shape):
        return pl.BlockSpec(shape, lambda b: (0,) * len(shape))

    return pl.pallas_call(
        _nsa_compress_kernel,
        grid=(batch,),
        in_specs=[pl.BlockSpec((seq, LANES), lambda b: (b, 0)), full((CMP_LEN, LANES)),
                  full((CMP_LEN, LANES, 2 * CMP_HIDDEN)), full((CMP_HIDDEN, LANES)),
                  full((HEAD_DIM, CMP_HIDDEN))],
        out_specs=[pl.BlockSpec((1, ng, LANES), lambda b: (b, 0, 0)),
                   pl.BlockSpec((1, HEAD_DIM, ng), lambda b: (b, 0, 0))],
        out_shape=[jax.ShapeDtypeStruct((batch, ng, LANES), BF16),
                   jax.ShapeDtypeStruct((batch, HEAD_DIM, ng), BF16)],
        compiler_params=_params(("parallel",)),
        name="nsa_compress",
    )(cmp2d, pe, w1, jnp.pad(w_ck2, ((0, 0), (0, LANES - HEAD_DIM))).astype(BF16), w_cv2.T.astype(BF16))


def _nsa_phases(ovl_ref, q_ref, k_ref, vt_ref, kc_ref, vct_ref, misc_ref, z_ref, o_ref,
                m_ref, acc_ref, wq_ref, s_ref, sc_ref, *, tq, tk, seq):
    qi = pl.program_id(1)
    kpq = tq // tk
    scale = HEAD_DIM ** -0.5
    c1 = scale * LOG2E
    ng = seq // CMP_STRIDE
    ns = seq // SLC_BLOCK
    top = min(SLC_TOPK, ns)
    m_ref[...] = jnp.full(m_ref.shape, NEG_INF, F32)
    acc_ref[...] = jnp.zeros(acc_ref.shape, F32)
    rel = _rel_t(tk, tq)

    qt = q_ref[...].astype(F32).T.astype(BF16)
    zeros_q = jnp.zeros((HEAD_DIM, tq), BF16)
    for h in range(HEADS):
        qh = qt[h * HEAD_DIM:(h + 1) * HEAD_DIM]
        alibi = _alibi_rows(NSA_SLOPES[h] / scale, tq, tk=tk)
        wq_ref[h, 0:LANES, :] = jnp.concatenate([qh, zeros_q], axis=0)
        wq_ref[h, LANES:2 * LANES, :] = alibi
        wq_ref[HEADS + h, 0:LANES, :] = jnp.concatenate([zeros_q, qh], axis=0)
        wq_ref[HEADS + h, LANES:2 * LANES, :] = alibi

    t_lane = qi * tq + lax.broadcasted_iota(jnp.int32, (1, tq), 1)
    n_sub = lax.broadcasted_iota(jnp.int32, (ng, 1), 0)
    c_valid = (n_sub * CMP_STRIDE + (CMP_LEN - 1)) <= t_lane
    kc = kc_ref[0]
    vct_ext = jnp.concatenate([vct_ref[0], jnp.ones((BF16_ROWS, ng), BF16)], axis=0)
    ovl = ovl_ref[...]
    cmp_max = []
    for h in range(HEADS):
        rhs = jnp.concatenate([qt[h * HEAD_DIM:(h + 1) * HEAD_DIM],
                               _alibi_rows(NSA_SLOPES[h] * CMP_STRIDE / scale, tq, HEAD_DIM)], axis=0)
        cmp_max.append(_stage_scores(sc_ref, h, slice(0, tq), _dot(kc, rhs), c_valid))
    imp = jnp.zeros((ns, tq), F32)
    o_cmp = []
    for h in range(HEADS):
        mx = jnp.where(cmp_max[h] > NEG_INF, cmp_max[h], 0.0)
        pb = jnp.exp2(c1 * sc_ref[h] - c1 * mx).astype(BF16)
        o_ext = _dot(vct_ext, pb)
        r = 1.0 / jnp.maximum(o_ext[HEAD_DIM:HEAD_DIM + 1], 1e-30)
        o_cmp.append(o_ext[0:HEAD_DIM] * r)
        imp = imp + _dot(ovl, pb) * r

    def select_blocks():
        j_sub = lax.broadcasted_iota(jnp.int32, (ns, 1), 0)
        j_sub_f = j_sub.astype(F32)
        cur = jnp.right_shift(t_lane, SLC_SHIFT)
        forced = (j_sub == 0) | (j_sub == cur) | (j_sub == cur - 1)
        valid = (j_sub * SLC_BLOCK) <= t_lane
        score = jnp.where(forced, 1e30, jnp.where(valid, imp, -1.0))
        sel = jnp.zeros((ns, tq), F32)
        for _ in range(top):
            mx = jnp.max(score, axis=0, keepdims=True)
            idx = jnp.min(jnp.where(score == mx, j_sub_f, float(ns)), axis=0, keepdims=True)
            pick = j_sub_f == idx
            sel = jnp.where(pick, 1.0, sel)
            score = jnp.where(pick, -2.0, score)
        unsel = jnp.where(sel > 0.5, 0.0, -MASK_BIG).astype(BF16)
        for h in range(HEADS):
            wq_ref[h, LANES + BLK_LANE0:LANES + BLK_LANE0 + ns, :] = unsel

    aux_lane = lax.broadcasted_iota(jnp.int32, (tk, LANES), 1)
    aux_blk = jnp.right_shift(lax.broadcasted_iota(jnp.int32, (tk, LANES), 0), SLC_SHIFT) + BLK_LANE0
    kpos = _key_pos_features(tk)
    ones = jnp.ones((BF16_ROWS, tk), BF16)

    n_slots = s_ref.shape[0]

    def tiles(window, kis, cols, mask_offs=None):
        base = HEADS if window else 0
        loaded = []
        for t, ki in enumerate(kis):
            start = pl.multiple_of(ki * tk, tk)
            k = k_ref[pl.ds(start, tk), :]
            if window:
                vt = vt_ref[HEAD_DIM:2 * HEAD_DIM, pl.ds(start, tk)]
                aux = kpos
                dist = rel[:, cols[t]] + (qi * tq - ki * tk)
                keep = (dist >= 0) & (dist < WINDOW)
            else:
                vt = vt_ref[0:HEAD_DIM, pl.ds(start, tk)]
                onehot = aux_lane == aux_blk + ki * (tk // SLC_BLOCK)
                aux = jnp.where(onehot, jnp.ones_like(kpos), kpos)
                keep = None
            loaded.append((jnp.concatenate([k, aux], axis=1),
                           jnp.concatenate([vt, ones], axis=0), (ki * tk).astype(F32), keep))
        items = [(t, h) for t in range(len(kis)) for h in range(HEADS)]
        s_max = {}

        def scores(i):
            t, h = items[i]
            if window:
                keep = loaded[t][3]
            else:
                keep = None if mask_offs[t] is None else rel[:, cols[t]] >= mask_offs[t]
            s_max[i] = _stage_scores(s_ref, i % n_slots, cols[t],
                                     _dot(loaded[t][0], wq_ref[base + h, :, cols[t]]), keep)

        for i in range(QK_LOOKAHEAD):
            scores(i)
        for i, (t, h) in enumerate(items):
            if i + QK_LOOKAHEAD < len(items):
                scores(i + QK_LOOKAHEAD)
            _lhs, vt_ext, key0, _keep = loaded[t]
            _flash_update_t(lambda i=i, t=t: s_ref[i % n_slots, :, cols[t]], s_max.pop(i), c1,
                            (NSA_SLOPES[h] * LOG2E) * key0, vt_ext, m_ref, acc_ref, base + h, cols[t])

    all_cols = slice(0, tq)
    diag_cols = [slice(d * tk, tq) for d in range(kpq)]

    tiles(True, [qi * kpq + d for d in range(kpq)], diag_cols)
    select_blocks()

    def loop_body(i):
        tiles(False, [i * kpq + d for d in range(kpq)], [all_cols] * kpq, [None] * kpq)

    def back_window():
        n_back = (WINDOW + tk - 1) // tk
        for g in range((n_back + kpq - 1) // kpq):
            backs = list(range(g * kpq + 1, min((g + 1) * kpq, n_back) + 1))
            back_cols = [slice(0, min(tq, -(-(WINDOW - (back - 1) * tk - 1) // LANES) * LANES)) for back in backs]

            @pl.when(qi > g)
            def _():
                tiles(True, [qi * kpq - back for back in backs], back_cols)

    def diag_tiles():
        tiles(False, [qi * kpq + d for d in range(kpq)], diag_cols, [d * tk for d in range(kpq)])

    def finish():
        g_t = jax.nn.sigmoid(misc_ref[...]).T
        outs = []
        for h in range(HEADS):
            r0 = GATE_COL + 3 * h
            outs.append(g_t[r0:r0 + 1] * o_cmp[h] + g_t[r0 + 1:r0 + 2] * _normalized(acc_ref[h])
                        + g_t[r0 + 2:r0 + 3] * _normalized(acc_ref[HEADS + h]))
        y = jnp.concatenate(outs, axis=0).T * _silu(z_ref[...])
        o_ref[...] = y.astype(o_ref.dtype)

    return loop_body, back_window, diag_tiles, finish


def _overlap_t(seq):
    nc = (seq - CMP_LEN) // CMP_STRIDE + 1
    ng = seq // CMP_STRIDE
    ns = seq // SLC_BLOCK
    c_start = np.arange(ng) * CMP_STRIDE
    c_end = c_start + CMP_LEN - 1
    s_start = np.arange(ns) * SLC_BLOCK
    s_end = s_start + SLC_BLOCK - 1
    ov = (c_start[None, :] <= s_end[:, None]) & (c_end[None, :] >= s_start[:, None]) & (np.arange(ng)[None, :] < nc)
    return jnp.asarray(ov.astype(np.float32), dtype=BF16)


def _nsa_operands(q, k2, vt_all, kc, vct, misc, z_all, seq, nq, tq, tk):
    ng = seq // CMP_STRIDE
    ns = seq // SLC_BLOCK
    assert tq - tk < WINDOW and tk % SLC_BLOCK == 0 and ns <= LANES - BLK_LANE0
    inputs = [_overlap_t(seq), q, k2, vt_all, kc, vct, misc, z_all]
    in_specs = [pl.BlockSpec((ns, ng), lambda b, i: (0, 0)),
                pl.BlockSpec((tq, GROUP_W), lambda b, i: (b * nq + i, 0)),
                pl.BlockSpec((seq, LANES), lambda b, i: (b, 0)),
                pl.BlockSpec((2 * HEAD_DIM, seq), lambda b, i: (NSA_VT_BLK, b)),
                pl.BlockSpec((1, ng, LANES), lambda b, i: (b, 0, 0)),
                pl.BlockSpec((1, HEAD_DIM, ng), lambda b, i: (b, 0, 0)),
                pl.BlockSpec((tq, LANES), lambda b, i: (b * nq + i, 0)),
                pl.BlockSpec((tq, GROUP_W), lambda b, i: (b * nq + i, 0))]
    scratch = [pltpu.VMEM((2 * HEADS, 1, tq), F32),
               pltpu.VMEM((2 * HEADS, ACC_ROWS, tq), F32),
               pltpu.VMEM((2 * HEADS, 2 * LANES, tq), BF16),
               pltpu.VMEM((2 * HEADS, tk, tq), F32),
               pltpu.VMEM((HEADS, ng, tq), F32)]
    return inputs, in_specs, scratch


def _mixers_kernel(*refs, n_in, n_scratch, tq, tk_nsa, tk_diff, seq, lam_init):
    bounds = np.cumsum([0] + list(n_in))
    nsa_in, diff_in, ret_in, ssd_in = (refs[a:b] for a, b in zip(bounds[:-1], bounds[1:]))
    nsa_out, diff_out, ret_out, ssd_out = refs[bounds[-1]:bounds[-1] + 4]
    sb = np.cumsum([0] + list(n_scratch)) + bounds[-1] + 4
    nsa_scr, diff_scr, rec_scr = (refs[a:b] for a, b in zip(sb[:-1], sb[1:]))
    ret_st, ssd_ext, ssd_st = rec_scr
    qi = pl.program_id(1)

    @pl.when(qi == 0)
    def _():
        ret_st[...] = jnp.zeros(ret_st.shape, F32)
        ssd_st[...] = jnp.zeros(ssd_st.shape, F32)
        ssd_ext[0:8, :] = jnp.zeros((8, CONV_CH), F32)

    chunks = list(range(ret_out.shape[0]))
    first, second = chunks[:len(chunks) // 2], chunks[len(chunks) // 2:]
    nsa_loop, nsa_back_window, nsa_diag, nsa_finish = _nsa_phases(
        *nsa_in, nsa_out, *nsa_scr, tq=tq, tk=tk_nsa, seq=seq)
    diff_loop, diff_diag, diff_finish = _diff_phases(
        *diff_in, diff_out, *diff_scr, tq=tq, tk=tk_diff, lam_init=lam_init)
    _ret_body(*ret_in, ret_out, ret_st, first)
    _ssd_body(*ssd_in, ssd_out, ssd_ext, ssd_st, first)

    def body(i, carry):
        nsa_loop(i)
        diff_loop(i)
        return carry

    lax.fori_loop(0, qi, body, 0)
    nsa_back_window()
    nsa_diag()
    diff_diag()
    _ret_body(*ret_in, ret_out, ret_st, second)
    _ssd_body(*ssd_in, ssd_out, ssd_ext, ssd_st, second)
    nsa_finish()
    diff_finish()


def _mixers(nsa_args, diff_args, rec_args, batch, seq, layer_idx, tq, tk_nsa, tk_diff):
    nq = seq // tq
    nsa_in, nsa_specs, nsa_scratch = _nsa_operands(*nsa_args, seq, nq, tq, tk_nsa)
    diff_in, diff_specs, diff_scratch = _diff_operands(*diff_args, seq, nq, tq, tk_diff)
    ret_in, ssd_in, rec_specs, rec_out_spec, rec_scratch = _recurrent_operands(*rec_args, nq, tq)
    out_spec = pl.BlockSpec((tq, GROUP_W), lambda b, i: (b * nq + i, 0))
    m = batch * seq
    y_nsa, y_diff, y_ret, y_ssm = pl.pallas_call(
        functools.partial(_mixers_kernel, n_in=(len(nsa_in), len(diff_in), len(ret_in), len(ssd_in)),
                          n_scratch=(len(nsa_scratch), len(diff_scratch), len(rec_scratch)),
                          tq=tq, tk_nsa=tk_nsa, tk_diff=tk_diff, seq=seq,
                          lam_init=0.8 - 0.6 * math.exp(-0.3 * layer_idx)),
        grid=(batch, nq),
        in_specs=nsa_specs + diff_specs + rec_specs,
        out_specs=[out_spec, out_spec, rec_out_spec, rec_out_spec],
        out_shape=[jax.ShapeDtypeStruct((m, GROUP_W), BF16)] * 2
                  + [jax.ShapeDtypeStruct((m // SSM_CHUNK, SSM_CHUNK, GROUP_W), BF16)] * 2,
        scratch_shapes=nsa_scratch + diff_scratch + rec_scratch,
        compiler_params=_params(("parallel", "arbitrary")),
        name="mixers",
    )(*nsa_in, *diff_in, *ret_in, *ssd_in)
    return y_nsa, y_diff, y_ret.reshape(m, GROUP_W), y_ssm.reshape(m, GROUP_W)


def _ret_tables():
    c = RET_CHUNK
    h = np.arange(HEADS, dtype=np.float32)
    log_g = jnp.log(1.0 - 2.0 ** (-5.0 - jnp.asarray(h)))
    pos = jnp.arange(c, dtype=F32)
    rel = pos[:, None] - pos[None, :]
    decay = jnp.where(rel >= 0, jnp.exp(log_g[:, None, None] * jnp.maximum(rel, 0.0)), 0.0)
    xi = jnp.exp(log_g[:, None] * (pos + 1.0))
    zeta = jnp.exp(log_g[:, None] * (c - 1.0 - pos))
    chunk_decay = jnp.exp(log_g * c)
    xi_tab = jnp.repeat(xi.T, HEAD_DIM, axis=1)
    zeta_tab = jnp.repeat(zeta.T, HEAD_DIM, axis=1)
    cd_tab = jnp.repeat(chunk_decay, HEAD_DIM)[None, :]
    return decay, xi_tab, zeta_tab, cd_tab


def _ret_body(decay_ref, xi_ref, zeta_ref, cd_ref, gn_ref, q_ref, k_ref, v_ref, z_ref, o_ref, st_ref, blocks):
    xi = xi_ref[...]
    cd = cd_ref[...]
    for bb in blocks:
        q = (q_ref[bb].astype(F32) * (HEAD_DIM ** -0.5)).astype(BF16)
        k = k_ref[bb]
        v = v_ref[bb]
        kz_t = (k.astype(F32) * zeta_ref[...]).T.astype(BF16)
        outs = []
        for h in range(HEADS):
            sl = slice(h * HEAD_DIM, (h + 1) * HEAD_DIM)
            qh, kh, vh = q[:, sl], k[:, sl], v[:, sl]
            prev = st_ref[h]
            inner = (_dot_nt(qh, kh) * decay_ref[h]).astype(BF16)
            o = _dot(inner, vh) + _dot(qh, prev.astype(BF16)) * xi[:, sl]
            st_ref[h] = prev * cd[:, sl] + _dot(kz_t[sl, :], vh)
            mu = jnp.mean(o, axis=-1, keepdims=True)
            d = o - mu
            var = jnp.mean(d * d, axis=-1, keepdims=True)
            outs.append(d * lax.rsqrt(var + EPS))
        y = jnp.concatenate(outs, axis=-1) * gn_ref[...] * _silu(z_ref[bb])
        o_ref[bb] = y.astype(o_ref.dtype)


def _ssd_body(cw_ref, cb_ref, dtb_ref, a_ref, dsk_ref, nw_ref, xbc_ref, misc_ref, z_ref, o_ref,
              ext_ref, st_ref, blocks):
    L = SSM_CHUNK
    hi = lax.Precision.HIGHEST
    row = lax.broadcasted_iota(jnp.int32, (L, L), 0)
    col = lax.broadcasted_iota(jnp.int32, (L, L), 1)
    causal = row >= col
    tril = jnp.where(causal, 1.0, 0.0).astype(F32)
    dsk = dsk_ref[...]

    for bb in blocks:
        raw = xbc_ref[bb]
        ext_ref[8:8 + L, :] = raw
        conv = cb_ref[...] + raw * cw_ref[CONV_W - 1:CONV_W, :]
        for w in range(CONV_W - 1):
            shift = CONV_W - 1 - w
            conv = conv + ext_ref[8 - shift:8 - shift + L, :] * cw_ref[w:w + 1, :]
        ext_ref[0:8, :] = raw[L - 8:L, :]
        xc = _silu(conv)
        x = xc[:, 0:GROUP_W]
        bm = xc[:, GROUP_W:GROUP_W + 2 * SSM_STATE].astype(BF16)
        cm = xc[:, GROUP_W + 2 * SSM_STATE:].astype(BF16)

        dt_full = jax.nn.softplus(misc_ref[bb] + dtb_ref[...])
        da = dt_full * a_ref[...]
        cs_col = jnp.dot(tril, da, precision=hi, preferred_element_type=F32)
        cs_row = lax.dot_general(da, tril, (((0,), (1,)), ((), ())), precision=hi,
                                 preferred_element_type=F32)

        outs = []
        for h in range(HEADS):
            g = h // 2
            c0 = DT_COL + h
            sl = slice(h * HEAD_DIM, (h + 1) * HEAD_DIM)
            gs = slice(g * SSM_STATE, (g + 1) * SSM_STATE)
            cs_c = cs_col[:, c0:c0 + 1]
            cs_r = cs_row[c0:c0 + 1, :]
            cs_last = cs_col[L - 1:L, c0:c0 + 1]
            xh = x[:, sl]
            xdt = xh * dt_full[:, c0:c0 + 1]
            seg = jnp.exp(jnp.where(causal, cs_c - cs_r, NEG_INF))
            cb = _dot_nt(cm[:, gs], bm[:, gs])
            y = _dot((cb * seg).astype(BF16), xdt.astype(BF16))
            prev = st_ref[h]
            y = y + _dot(cm[:, gs], prev.astype(BF16)) * jnp.exp(cs_c)
            y = y + dsk[:, sl] * xh
            dec = jnp.exp(cs_last - cs_c)
            st_ref[h] = prev * jnp.exp(cs_last) + _dot_tn(bm[:, gs], (xdt * dec).astype(BF16))
            outs.append(y)
        y = jnp.concatenate(outs, axis=-1) * _silu(z_ref[bb])
        ms = jnp.mean(y * y, axis=-1, keepdims=True)
        o_ref[bb] = (y * lax.rsqrt(ms + EPS) * nw_ref[...]).astype(o_ref.dtype)


def _recurrent_operands(ret_qkv, xbc, misc, z_all, gn_w, conv_w, conv_b, dt_bias, a_log, d_skip, norm_w, nq, tq):
    L = SSM_CHUNK
    assert RET_CHUNK == L and tq % L == 0
    cpt = tq // L
    decay, xi_tab, zeta_tab, cd_tab = _ret_tables()
    dtb = jnp.zeros((1, 128), F32).at[0, DT_COL:DT_COL + HEADS].set(dt_bias)
    a_full = jnp.zeros((1, 128), F32).at[0, DT_COL:DT_COL + HEADS].set(-jnp.exp(a_log))
    dsk = jnp.repeat(d_skip, HEAD_DIM)[None, :]

    def full(shape):
        return pl.BlockSpec(shape, lambda b, i: (0,) * len(shape))

    def blk(width, col):
        return pl.BlockSpec((cpt, L, width), lambda b, i: (b * nq + i, 0, col))

    def chunked(a):
        return a.reshape(-1, L, a.shape[-1])

    qkv3, z3 = chunked(ret_qkv), chunked(z_all)
    ret_in = [decay, xi_tab, zeta_tab, cd_tab, gn_w.reshape(1, GROUP_W), qkv3, qkv3, qkv3, z3]
    ret_specs = [full((HEADS, L, L)), full((L, GROUP_W)), full((L, GROUP_W)), full((1, GROUP_W)),
                 full((1, GROUP_W)), blk(GROUP_W, 0), blk(GROUP_W, 1), blk(GROUP_W, 2), blk(GROUP_W, 2)]
    ssd_in = [conv_w, conv_b.reshape(1, CONV_CH), dtb, a_full, dsk, norm_w.reshape(1, GROUP_W),
              chunked(xbc), chunked(misc), z3]
    ssd_specs = [full((CONV_W, CONV_CH)), full((1, CONV_CH)), full((1, 128)), full((1, 128)),
                 full((1, GROUP_W)), full((1, GROUP_W)), blk(CONV_CH, 0), blk(128, 0), blk(GROUP_W, 3)]
    scratch = [pltpu.VMEM((HEADS, HEAD_DIM, HEAD_DIM), F32),
               pltpu.VMEM((8 + L, CONV_CH), F32),
               pltpu.VMEM((HEADS, SSM_STATE, HEAD_DIM), F32)]
    return ret_in, ssd_in, ret_specs + ssd_specs, blk(GROUP_W, 0), scratch


def _pick_tile(n, pref):
    t = pref
    while n % t:
        t //= 2
    return t


def kernel(x, norm_w, w_in, w_out, nsa_pe_k, nsa_pe_v, nsa_w_ck1, nsa_w_ck2, nsa_w_cv1, nsa_w_cv2,
           diff_lam_q1, diff_lam_k1, diff_lam_q2, diff_lam_k2, diff_subln_w, ret_gn_w,
           ssm_conv_w, ssm_conv_b, ssm_dt_bias, ssm_A_log, ssm_D, ssm_norm_w, final_norm_w):
    batch, seq, _ = x.shape
    depth = w_in.shape[0]
    m = batch * seq
    tm = _pick_tile(m, 256)
    tq = _pick_tile(seq, 512)
    tk = _pick_tile(seq, 256)
    tk_diff = _pick_tile(seq, 512)
    w_in_b = w_in.astype(BF16)
    w_r = _relayout_w_in(w_in_b)
    w_t = _relayout_w_in_t(w_in_b)
    w_out_b = w_out.astype(BF16)
    x2d = x.reshape(m, D_MODEL)
    projected = _in_proj(x2d, norm_w[0], w_r[0], w_t[0], tm)
    for i in range(depth):
        nsa_q, nsa_k2, nsa_cmp, misc, z_all, diff_qk, ret_qkv, xbc, vt_all = projected
        kc, vct = _nsa_compress(nsa_cmp, nsa_pe_k[i], nsa_pe_v[i], nsa_w_ck1[i], nsa_w_ck2[i],
                                nsa_w_cv1[i], nsa_w_cv2[i], batch, seq)
        lam_p = jnp.stack([diff_lam_q1[i], diff_lam_k1[i], diff_lam_q2[i], diff_lam_k2[i]])
        ys = _mixers((nsa_q, nsa_k2, vt_all, kc, vct, misc, z_all),
                     (diff_qk, vt_all, z_all, lam_p, diff_subln_w[i]),
                     (ret_qkv, xbc, misc, z_all, ret_gn_w[i], ssm_conv_w[i], ssm_conv_b[i], ssm_dt_bias[i],
                      ssm_A_log[i], ssm_D[i], ssm_norm_w[i]),
                     batch, seq, i, tq, tk, tk_diff)
        if i + 1 < depth:
            x2d, *projected = _out_in_proj(ys, w_out_b[i], x2d, norm_w[i + 1], w_r[i + 1], w_t[i + 1], tm)
        else:
            x2d = _out_proj(ys, w_out_b[i], x2d, final_norm_w, tm)
    return x2d.reshape(batch, seq, D_MODEL)
```
